```python
import jax, jax.numpy as jnp
from jax import lax
import numpy as np

D_MODEL = 2048
BATCH = 8
SEQ = 8192
DEPTH = 2

MIX_WIDTH = 2 * D_MODEL
A_WIDTH = D_MODEL // 1 if False else MIX_WIDTH // 2
A_GROUPS = 8
A_CHUNK = 128
B_WIDTH = MIX_WIDTH // 2
B_HEAD_DIM = 64
B_HEADS = B_WIDTH // B_HEAD_DIM
B_GROUPS = 8
B_STATE = 128
B_CONV = 4
B_CHUNK = 128
B_XBC = B_WIDTH + 2 * B_GROUPS * B_STATE
C_WIDTH = MIX_WIDTH // 2
C_CONV = 3
D_WIDTH = MIX_WIDTH // 2
D_HEAD_DIM = 128
D_HEADS = D_WIDTH // D_HEAD_DIM
D_PATTERNS = ((128, 1), (512, 4), (2048, 16))

EPS = 1e-5
N_EVEN = (DEPTH + 1) // 2
N_ODD = DEPTH // 2
IN_EVEN = 3 * A_WIDTH + B_WIDTH + B_XBC + B_HEADS
IN_ODD = 4 * C_WIDTH + 4 * D_WIDTH

kernel_name = "hybrid_gmlp_ssd_shortconv_dilated_attn"


def rms_norm(x, g):
    xf = x.astype(jnp.float32)
    y = xf * lax.rsqrt(jnp.mean(xf * xf, axis=-1, keepdims=True) + EPS)
    return (y * g.astype(jnp.float32)).astype(x.dtype)


def layer_norm(x, g, b):
    xf = x.astype(jnp.float32)
    mu = jnp.mean(xf, axis=-1, keepdims=True)
    xc = xf - mu
    y = xc * lax.rsqrt(jnp.mean(xc * xc, axis=-1, keepdims=True) + EPS)
    return (y * g.astype(jnp.float32) + b.astype(jnp.float32)).astype(x.dtype)


def causal_dwconv(x, w):
    K, C = w.shape
    return lax.conv_general_dilated(
        x, w[:, None, :].astype(x.dtype), window_strides=(1,),
        padding=[(K - 1, 0)], dimension_numbers=("NWC", "WIO", "NWC"),
        feature_group_count=C)


def gmlp_branch(h, ln_g, ln_b, ws, bs):
    Bb, S, _ = h.shape
    u, v, z = jnp.split(h, 3, axis=-1)
    v = layer_norm(v, ln_g, ln_b)
    G, Q, _ = ws.shape
    causal = jnp.tril(jnp.ones((Q, Q), dtype=bool))
    ws_c = jnp.where(causal, ws, jnp.zeros_like(ws))
    vc = v.reshape(Bb, S // Q, Q, G, A_WIDTH // G)
    mixed = jnp.einsum("gts,bcsgd->bctgd", ws_c, vc) + bs.T[None, None, :, :, None]
    return jax.nn.silu(z) * (u * mixed.reshape(Bb, S, A_WIDTH))


def segsum(x):
    T = x.shape[-1]
    cs = jnp.cumsum(x, axis=-1)
    seg = cs[..., :, None] - cs[..., None, :]
    return jnp.where(jnp.tril(jnp.ones((T, T), dtype=bool)), seg, -jnp.inf)


def ssd_scan(x, dt, a, bm, cm):
    Bb, S, H, P = x.shape
    G, N = bm.shape[2], bm.shape[3]
    R = H // G
    Q = B_CHUNK
    nc = S // Q
    xdt = (x * dt[..., None]).reshape(Bb, nc, Q, G, R, P)
    adt = (dt * a).reshape(Bb, nc, Q, G, R).transpose(0, 3, 4, 1, 2)
    bc = bm.reshape(Bb, nc, Q, G, N)
    cc = cm.reshape(Bb, nc, Q, G, N)
    a_cs = jnp.cumsum(adt, axis=-1)
    L = jnp.exp(segsum(adt))
    cb = jnp.einsum("bclgn,bcsgn->bcgls", cc, bc)
    y_diag = jnp.einsum("bcgls,bgrcls,bcsgrp->bclgrp", cb, L, xdt)
    decay_states = jnp.exp(a_cs[..., -1:] - a_cs)
    states = jnp.einsum("bclgn,bgrcl,bclgrp->bcgrpn", bc, decay_states, xdt)
    chunk_decay = jnp.exp(a_cs[..., -1])

    def step(hstate, inp):
        s_c, d_c = inp
        return hstate * d_c[..., None, None] + s_c, hstate

    _, prev = lax.scan(step, jnp.zeros_like(states[:, 0]),
                       (jnp.moveaxis(states, 1, 0), jnp.moveaxis(chunk_decay, -1, 0)))
    prev = jnp.moveaxis(prev, 0, 1)
    y_off = jnp.einsum("bclgn,bcgrpn,bgrcl->bclgrp", cc, prev, jnp.exp(a_cs))
    return (y_diag + y_off).reshape(Bb, S, H, P)


def ssd_branch(h, conv_w, conv_b, dt_bias, a_log, d_skip, norm_g):
    Bb, S, _ = h.shape
    z = h[..., :B_WIDTH]
    xbc = h[..., B_WIDTH:B_WIDTH + B_XBC]
    dt_raw = h[..., B_WIDTH + B_XBC:]
    xbc = jax.nn.silu(causal_dwconv(xbc, conv_w) + conv_b.astype(xbc.dtype))
    gn = B_GROUPS * B_STATE
    xs = xbc[..., :B_WIDTH].astype(jnp.float32).reshape(Bb, S, B_HEADS, B_HEAD_DIM)
    bm = xbc[..., B_WIDTH:B_WIDTH + gn].astype(jnp.float32).reshape(Bb, S, B_GROUPS, B_STATE)
    cm = xbc[..., B_WIDTH + gn:].astype(jnp.float32).reshape(Bb, S, B_GROUPS, B_STATE)
    dt = jax.nn.softplus(dt_raw.astype(jnp.float32) + dt_bias.astype(jnp.float32))
    a = -jnp.exp(a_log.astype(jnp.float32))
    y = ssd_scan(xs, dt, a, bm, cm) + d_skip.astype(jnp.float32)[:, None] * xs
    y = y.reshape(Bb, S, B_WIDTH) * jax.nn.silu(z.astype(jnp.float32))
    yg = y.reshape(Bb, S, B_GROUPS, B_WIDTH // B_GROUPS)
    yg = yg * lax.rsqrt(jnp.mean(yg * yg, axis=-1, keepdims=True) + EPS)
    return (yg.reshape(Bb, S, B_WIDTH) * norm_g.astype(jnp.float32)).astype(h.dtype)


def shortconv_branch(h, conv_w):
    bg, cg, hx, z = jnp.split(h, 4, axis=-1)
    return jax.nn.silu(z) * (bg * causal_dwconv(cg * hx, conv_w))


def dilated_window_attention(q, k, v, dil, n_back):
    Bb, S, H, E = q.shape
    M = S // dil
    nb = -(-M // n_back)
    Mp = nb * n_back

    def to_blocks(t):
        t = t.reshape(Bb, M, dil, H, E)
        t = jnp.pad(t, ((0, 0), (0, Mp - M), (0, 0), (0, 0), (0, 0)))
        return t.reshape(Bb, nb, n_back, dil, H, E)

    def with_prev(t):
        prev = jnp.pad(t, ((0, 0), (1, 0), (0, 0), (0, 0), (0, 0), (0, 0)))[:, :-1]
        return jnp.concatenate([prev, t], axis=2)

    qb = to_blocks(q)
    kw = with_prev(to_blocks(k))
    vw = with_prev(to_blocks(v))
    s = jnp.einsum("bnarhe,bnjrhe->bnrhaj", qb, kw,
                   preferred_element_type=jnp.float32) * (E ** -0.5)
    a_idx = jnp.arange(n_back)[:, None]
    j_idx = jnp.arange(2 * n_back)[None, :]
    band = (j_idx >= a_idx) & (j_idx <= a_idx + n_back)
    key_ok = (jnp.arange(nb)[:, None] > 0) | (j_idx >= n_back)
    mask = (band[None] & key_ok[:, None, :])[None, :, None, None]
    s = jnp.where(mask, s, -jnp.inf)
    mx = jnp.max(s, axis=-1, keepdims=True)
    p = jnp.exp(s - mx)
    l = jnp.sum(p, axis=-1, keepdims=True)
    o = jnp.einsum("bnrhaj,bnjrhe->bnarhe", p / l, vw.astype(jnp.float32))
    lse = (mx + jnp.log(l))[..., 0].transpose(0, 1, 4, 2, 3)
    o = o.reshape(Bb, Mp, dil, H, E)[:, :M].reshape(Bb, S, H, E)
    lse = lse.reshape(Bb, Mp, dil, H)[:, :M].reshape(Bb, S, H)
    return o, lse


def dilated_attention_branch(h):
    Bb, S, _ = h.shape
    q, k, v, z = jnp.split(h, 4, axis=-1)
    q = q.reshape(Bb, S, D_HEADS, D_HEAD_DIM)
    k = k.reshape(Bb, S, D_HEADS, D_HEAD_DIM)
    v = v.reshape(Bb, S, D_HEADS, D_HEAD_DIM)
    outs, lses = [], []
    for window, dil in D_PATTERNS:
        o, lse = dilated_window_attention(q, k, v, dil, window // dil)
        outs.append(o)
        lses.append(lse)
    wts = jax.nn.softmax(jnp.stack(lses, axis=0), axis=0)
    o = jnp.sum(wts[..., None] * jnp.stack(outs, axis=0), axis=0)
    return jax.nn.silu(z) * o.reshape(Bb, S, D_WIDTH).astype(h.dtype)


def even_layer(x, norm_g, w_in, ln_g, ln_b, ws, bs, conv_w, conv_b, dt_bias,
               a_log, d_skip, ssd_norm_g, w_out):
    h = jnp.einsum("bsd,df->bsf", rms_norm(x, norm_g), w_in)
    ya = gmlp_branch(h[..., :3 * A_WIDTH], ln_g, ln_b, ws, bs)
    yb = ssd_branch(h[..., 3 * A_WIDTH:], conv_w, conv_b, dt_bias, a_log, d_skip, ssd_norm_g)
    y = jnp.concatenate([ya, yb.astype(ya.dtype)], axis=-1)
    return x + jnp.einsum("bsf,fd->bsd", y, w_out).astype(x.dtype)


def odd_layer(x, norm_g, w_in, sconv_w, w_out):
    h = jnp.einsum("bsd,df->bsf", rms_norm(x, norm_g), w_in)
    yc = shortconv_branch(h[..., :4 * C_WIDTH], sconv_w)
    yd = dilated_attention_branch(h[..., 4 * C_WIDTH:])
    y = jnp.concatenate([yc, yd.astype(yc.dtype)], axis=-1)
    return x + jnp.einsum("bsf,fd->bsd", y, w_out).astype(x.dtype)


def _fwd_setup_inputs(seed: int = 0) -> dict:
    key = jax.random.key(seed)
    ks = jax.random.split(key, 20)
    f32 = jnp.float32
    nrm = lambda k, shape, scale: jax.random.normal(k, shape, f32) * scale
    x = jax.random.normal(ks[0], (BATCH, SEQ, D_MODEL), f32)
    even_norm_g = 1.0 + nrm(ks[1], (N_EVEN, D_MODEL), 0.02)
    even_w_in = nrm(ks[2], (N_EVEN, D_MODEL, IN_EVEN), D_MODEL ** -0.5)
    gmlp_ln_g = 1.0 + nrm(ks[3], (N_EVEN, A_WIDTH), 0.02)
    gmlp_ln_b = nrm(ks[4], (N_EVEN, A_WIDTH), 0.02)
    gmlp_ws = nrm(ks[5], (N_EVEN, A_GROUPS, A_CHUNK, A_CHUNK), A_CHUNK ** -0.5)
    gmlp_bs = 1.0 + nrm(ks[6], (N_EVEN, A_GROUPS, A_CHUNK), 0.1)
    ssd_conv_w = nrm(ks[7], (N_EVEN, B_CONV, B_XBC), B_CONV ** -0.5)
    ssd_conv_b = nrm(ks[8], (N_EVEN, B_XBC), 0.02)
    dt0 = jnp.exp(jax.random.uniform(ks[9], (N_EVEN, B_HEADS), f32,
                                     np.log(1e-3).astype(np.float32), np.log(1e-1).astype(np.float32)))
    ssd_dt_bias = dt0 + jnp.log(-jnp.expm1(-dt0))
    ssd_a_log = jnp.log(jax.random.uniform(ks[10], (N_EVEN, B_HEADS), f32, 1.0, 16.0))
    ssd_d = 1.0 + nrm(ks[11], (N_EVEN, B_HEADS), 0.1)
    ssd_norm_g = 1.0 + nrm(ks[12], (N_EVEN, B_WIDTH), 0.02)
    even_w_out = nrm(ks[13], (N_EVEN, MIX_WIDTH, D_MODEL), MIX_WIDTH ** -0.5)
    odd_norm_g = 1.0 + nrm(ks[14], (N_ODD, D_MODEL), 0.02)
    odd_w_in = nrm(ks[15], (N_ODD, D_MODEL, IN_ODD), D_MODEL ** -0.5)
    sconv_w = nrm(ks[16], (N_ODD, C_CONV, C_WIDTH), C_CONV ** -0.5)
    odd_w_out = nrm(ks[17], (N_ODD, MIX_WIDTH, D_MODEL), MIX_WIDTH ** -0.5)
    final_norm_g = 1.0 + nrm(ks[18], (D_MODEL,), 0.02)
    return {"x": x, "even_norm_g": even_norm_g, "even_w_in": even_w_in,
            "gmlp_ln_g": gmlp_ln_g, "gmlp_ln_b": gmlp_ln_b, "gmlp_ws": gmlp_ws,
            "gmlp_bs": gmlp_bs, "ssd_conv_w": ssd_conv_w, "ssd_conv_b": ssd_conv_b,
            "ssd_dt_bias": ssd_dt_bias, "ssd_a_log": ssd_a_log, "ssd_d": ssd_d,
            "ssd_norm_g": ssd_norm_g, "even_w_out": even_w_out,
            "odd_norm_g": odd_norm_g, "odd_w_in": odd_w_in, "sconv_w": sconv_w,
            "odd_w_out": odd_w_out, "final_norm_g": final_norm_g}


def _fwd_reference(x, even_norm_g, even_w_in, gmlp_ln_g, gmlp_ln_b, gmlp_ws, gmlp_bs,
              ssd_conv_w, ssd_conv_b, ssd_dt_bias, ssd_a_log, ssd_d, ssd_norm_g,
              even_w_out, odd_norm_g, odd_w_in, sconv_w, odd_w_out, final_norm_g):
    for layer in range(DEPTH):
        i = layer // 2
        if layer % 2 == 0:
            x = even_layer(x, even_norm_g[i], even_w_in[i], gmlp_ln_g[i], gmlp_ln_b[i],
                           gmlp_ws[i], gmlp_bs[i], ssd_conv_w[i], ssd_conv_b[i],
                           ssd_dt_bias[i], ssd_a_log[i], ssd_d[i], ssd_norm_g[i],
                           even_w_out[i])
        else:
            x = odd_layer(x, odd_norm_g[i], odd_w_in[i], sconv_w[i], odd_w_out[i])
    return rms_norm(x, final_norm_g)


import jax as _jax
import jax.numpy as _jnp

TWIN_FORMAT = 'train_step'
FWD_PARAMS = ['x', 'even_norm_g', 'even_w_in', 'gmlp_ln_g', 'gmlp_ln_b', 'gmlp_ws', 'gmlp_bs', 'ssd_conv_w', 'ssd_conv_b', 'ssd_dt_bias', 'ssd_a_log', 'ssd_d', 'ssd_norm_g', 'even_w_out', 'odd_norm_g', 'odd_w_in', 'sconv_w', 'odd_w_out', 'final_norm_g']
TWIN_WEIGHTS = ['even_norm_g', 'even_w_in', 'gmlp_ln_g', 'gmlp_ln_b', 'gmlp_ws', 'gmlp_bs', 'ssd_conv_w', 'ssd_conv_b', 'ssd_dt_bias', 'ssd_a_log', 'ssd_d', 'ssd_norm_g', 'even_w_out', 'odd_norm_g', 'odd_w_in', 'sconv_w', 'odd_w_out', 'final_norm_g']
TWIN_DIFF_INPUT = 'x'
TWIN_INPUTS = ['x', 'even_norm_g', 'even_w_in', 'gmlp_ln_g', 'gmlp_ln_b', 'gmlp_ws', 'gmlp_bs', 'ssd_conv_w', 'ssd_conv_b', 'ssd_dt_bias', 'ssd_a_log', 'ssd_d', 'ssd_norm_g', 'even_w_out', 'odd_norm_g', 'odd_w_in', 'sconv_w', 'odd_w_out', 'final_norm_g', 'loss_target', 'm_even_norm_g', 'm_even_w_in', 'm_gmlp_ln_g', 'm_gmlp_ln_b', 'm_gmlp_ws', 'm_gmlp_bs', 'm_ssd_conv_w', 'm_ssd_conv_b', 'm_ssd_dt_bias', 'm_ssd_a_log', 'm_ssd_d', 'm_ssd_norm_g', 'm_even_w_out', 'm_odd_norm_g', 'm_odd_w_in', 'm_sconv_w', 'm_odd_w_out', 'm_final_norm_g', 'v_even_norm_g', 'v_even_w_in', 'v_gmlp_ln_g', 'v_gmlp_ln_b', 'v_gmlp_ws', 'v_gmlp_bs', 'v_ssd_conv_w', 'v_ssd_conv_b', 'v_ssd_dt_bias', 'v_ssd_a_log', 'v_ssd_d', 'v_ssd_norm_g', 'v_even_w_out', 'v_odd_norm_g', 'v_odd_w_in', 'v_sconv_w', 'v_odd_w_out', 'v_final_norm_g']
TWIN_OUTPUTS = ['loss', 'grad_x', 'grad_even_norm_g', 'grad_even_w_in', 'grad_gmlp_ln_g', 'grad_gmlp_ln_b', 'grad_gmlp_ws', 'grad_gmlp_bs', 'grad_ssd_conv_w', 'grad_ssd_conv_b', 'grad_ssd_dt_bias', 'grad_ssd_a_log', 'grad_ssd_d', 'grad_ssd_norm_g', 'grad_even_w_out', 'grad_odd_norm_g', 'grad_odd_w_in', 'grad_sconv_w', 'grad_odd_w_out', 'grad_final_norm_g', 'delta_even_norm_g', 'delta_even_w_in', 'delta_gmlp_ln_g', 'delta_gmlp_ln_b', 'delta_gmlp_ws', 'delta_gmlp_bs', 'delta_ssd_conv_w', 'delta_ssd_conv_b', 'delta_ssd_dt_bias', 'delta_ssd_a_log', 'delta_ssd_d', 'delta_ssd_norm_g', 'delta_even_w_out', 'delta_odd_norm_g', 'delta_odd_w_in', 'delta_sconv_w', 'delta_odd_w_out', 'delta_final_norm_g', 'new_m_even_norm_g', 'new_m_even_w_in', 'new_m_gmlp_ln_g', 'new_m_gmlp_ln_b', 'new_m_gmlp_ws', 'new_m_gmlp_bs', 'new_m_ssd_conv_w', 'new_m_ssd_conv_b', 'new_m_ssd_dt_bias', 'new_m_ssd_a_log', 'new_m_ssd_d', 'new_m_ssd_norm_g', 'new_m_even_w_out', 'new_m_odd_norm_g', 'new_m_odd_w_in', 'new_m_sconv_w', 'new_m_odd_w_out', 'new_m_final_norm_g', 'new_v_even_norm_g', 'new_v_even_w_in', 'new_v_gmlp_ln_g', 'new_v_gmlp_ln_b', 'new_v_gmlp_ws', 'new_v_gmlp_bs', 'new_v_ssd_conv_w', 'new_v_ssd_conv_b', 'new_v_ssd_dt_bias', 'new_v_ssd_a_log', 'new_v_ssd_d', 'new_v_ssd_norm_g', 'new_v_even_w_out', 'new_v_odd_norm_g', 'new_v_odd_w_in', 'new_v_sconv_w', 'new_v_odd_w_out', 'new_v_final_norm_g']
TWIN_LEAF_KINDS = {'loss': 'loss', 'grad_x': 'grad_x', 'grad_even_norm_g': 'grad_w', 'grad_even_w_in': 'grad_w', 'grad_gmlp_ln_g': 'grad_w', 'grad_gmlp_ln_b': 'grad_w', 'grad_gmlp_ws': 'grad_w', 'grad_gmlp_bs': 'grad_w', 'grad_ssd_conv_w': 'grad_w', 'grad_ssd_conv_b': 'grad_w', 'grad_ssd_dt_bias': 'grad_w', 'grad_ssd_a_log': 'grad_w', 'grad_ssd_d': 'grad_w', 'grad_ssd_norm_g': 'grad_w', 'grad_even_w_out': 'grad_w', 'grad_odd_norm_g': 'grad_w', 'grad_odd_w_in': 'grad_w', 'grad_sconv_w': 'grad_w', 'grad_odd_w_out': 'grad_w', 'grad_final_norm_g': 'grad_w', 'delta_even_norm_g': 'delta_w', 'delta_even_w_in': 'delta_w', 'delta_gmlp_ln_g': 'delta_w', 'delta_gmlp_ln_b': 'delta_w', 'delta_gmlp_ws': 'delta_w', 'delta_gmlp_bs': 'delta_w', 'delta_ssd_conv_w': 'delta_w', 'delta_ssd_conv_b': 'delta_w', 'delta_ssd_dt_bias': 'delta_w', 'delta_ssd_a_log': 'delta_w', 'delta_ssd_d': 'delta_w', 'delta_ssd_norm_g': 'delta_w', 'delta_even_w_out': 'delta_w', 'delta_odd_norm_g': 'delta_w', 'delta_odd_w_in': 'delta_w', 'delta_sconv_w': 'delta_w', 'delta_odd_w_out': 'delta_w', 'delta_final_norm_g': 'delta_w', 'new_m_even_norm_g': 'new_m', 'new_m_even_w_in': 'new_m', 'new_m_gmlp_ln_g': 'new_m', 'new_m_gmlp_ln_b': 'new_m', 'new_m_gmlp_ws': 'new_m', 'new_m_gmlp_bs': 'new_m', 'new_m_ssd_conv_w': 'new_m', 'new_m_ssd_conv_b': 'new_m', 'new_m_ssd_dt_bias': 'new_m', 'new_m_ssd_a_log': 'new_m', 'new_m_ssd_d': 'new_m', 'new_m_ssd_norm_g': 'new_m', 'new_m_even_w_out': 'new_m', 'new_m_odd_norm_g': 'new_m', 'new_m_odd_w_in': 'new_m', 'new_m_sconv_w': 'new_m', 'new_m_odd_w_out': 'new_m', 'new_m_final_norm_g': 'new_m', 'new_v_even_norm_g': 'new_v', 'new_v_even_w_in': 'new_v', 'new_v_gmlp_ln_g': 'new_v', 'new_v_gmlp_ln_b': 'new_v', 'new_v_gmlp_ws': 'new_v', 'new_v_gmlp_bs': 'new_v', 'new_v_ssd_conv_w': 'new_v', 'new_v_ssd_conv_b': 'new_v', 'new_v_ssd_dt_bias': 'new_v', 'new_v_ssd_a_log': 'new_v', 'new_v_ssd_d': 'new_v', 'new_v_ssd_norm_g': 'new_v', 'new_v_even_w_out': 'new_v', 'new_v_odd_norm_g': 'new_v', 'new_v_odd_w_in': 'new_v', 'new_v_sconv_w': 'new_v', 'new_v_odd_w_out': 'new_v', 'new_v_final_norm_g': 'new_v'}


def _forward(args):
    return _fwd_reference(*[args[k] for k in FWD_PARAMS])


def _output_shape():
    def fwd():
        inp = _fwd_setup_inputs(0)
        return _fwd_reference(*[inp[k] for k in FWD_PARAMS])
    out = _jax.eval_shape(fwd)
    return out.shape, out.dtype

N_MICROBATCH = 1
ADAM_LR = 0.001
ADAM_B1 = 0.9
ADAM_B2 = 0.999
ADAM_EPS = 1e-08
ADAM_WD = 0.01
ADAM_STEP = 10
PER_EXAMPLE_BATCH_AXIS = {'x': 0, 'loss_target': 0}
SHARED_INPUTS = []
_WEIGHT_DTYPES = {'even_norm_g': _jnp.float32, 'even_w_in': _jnp.float32, 'gmlp_ln_g': _jnp.float32, 'gmlp_ln_b': _jnp.float32, 'gmlp_ws': _jnp.float32, 'gmlp_bs': _jnp.float32, 'ssd_conv_w': _jnp.float32, 'ssd_conv_b': _jnp.float32, 'ssd_dt_bias': _jnp.float32, 'ssd_a_log': _jnp.float32, 'ssd_d': _jnp.float32, 'ssd_norm_g': _jnp.float32, 'even_w_out': _jnp.float32, 'odd_norm_g': _jnp.float32, 'odd_w_in': _jnp.float32, 'sconv_w': _jnp.float32, 'odd_w_out': _jnp.float32, 'final_norm_g': _jnp.float32}
MOMENT_SCALE = {'even_norm_g': 1.430736e-01, 'even_w_in': 5.786213e-02, 'gmlp_ln_g': 3.271292e-02, 'gmlp_ln_b': 3.048251e-02, 'gmlp_ws': 4.506280e-02, 'gmlp_bs': 7.401142e-02, 'ssd_conv_w': 5.713238e-02, 'ssd_conv_b': 7.738843e-02, 'ssd_dt_bias': 3.989004e-01, 'ssd_a_log': 1.916311e-01, 'ssd_d': 5.620882e-01, 'ssd_norm_g': 7.627340e-02, 'even_w_out': 9.405674e-02, 'odd_norm_g': 7.708111e-02, 'odd_w_in': 2.719009e-02, 'sconv_w': 3.846777e-02, 'odd_w_out': 3.847343e-02, 'final_norm_g': 3.199052e+01}


def _to_microbatches(a, axis):
    t = _jnp.moveaxis(a, axis, 0)
    t = t.reshape((N_MICROBATCH, t.shape[0] // N_MICROBATCH) + t.shape[1:])
    return _jnp.moveaxis(t, 1, axis + 1)


def setup_inputs(seed: int = 0) -> dict:
    inp = _fwd_setup_inputs(seed)
    key = _jax.random.fold_in(_jax.random.key(seed), 7919)
    shape, _ = _output_shape()
    out = dict(inp)
    out["loss_target"] = _jax.random.normal(_jax.random.fold_in(key, 0), shape, _jnp.float32)
    for i, name in enumerate(TWIN_WEIGHTS):
        w = inp[name].astype(_jnp.float32)
        if MOMENT_SCALE is None:
            s = _jnp.sqrt(_jnp.mean(_jnp.square(w)) + 1e-30)
        else:
            s = MOMENT_SCALE[name]
        km, kv = _jax.random.split(_jax.random.fold_in(key, i + 1))
        out[name] = w
        out["m_" + name] = s * _jax.random.normal(km, w.shape, _jnp.float32)
        out["v_" + name] = (s * s) * _jax.random.uniform(kv, w.shape, _jnp.float32, 0.5, 1.5)
    if N_MICROBATCH > 1:
        for name, axis in PER_EXAMPLE_BATCH_AXIS.items():
            out[name] = _to_microbatches(out[name], axis)
    return {'x': out['x'], 'even_norm_g': out['even_norm_g'], 'even_w_in': out['even_w_in'], 'gmlp_ln_g': out['gmlp_ln_g'], 'gmlp_ln_b': out['gmlp_ln_b'], 'gmlp_ws': out['gmlp_ws'], 'gmlp_bs': out['gmlp_bs'], 'ssd_conv_w': out['ssd_conv_w'], 'ssd_conv_b': out['ssd_conv_b'], 'ssd_dt_bias': out['ssd_dt_bias'], 'ssd_a_log': out['ssd_a_log'], 'ssd_d': out['ssd_d'], 'ssd_norm_g': out['ssd_norm_g'], 'even_w_out': out['even_w_out'], 'odd_norm_g': out['odd_norm_g'], 'odd_w_in': out['odd_w_in'], 'sconv_w': out['sconv_w'], 'odd_w_out': out['odd_w_out'], 'final_norm_g': out['final_norm_g'], 'loss_target': out['loss_target'], 'm_even_norm_g': out['m_even_norm_g'], 'm_even_w_in': out['m_even_w_in'], 'm_gmlp_ln_g': out['m_gmlp_ln_g'], 'm_gmlp_ln_b': out['m_gmlp_ln_b'], 'm_gmlp_ws': out['m_gmlp_ws'], 'm_gmlp_bs': out['m_gmlp_bs'], 'm_ssd_conv_w': out['m_ssd_conv_w'], 'm_ssd_conv_b': out['m_ssd_conv_b'], 'm_ssd_dt_bias': out['m_ssd_dt_bias'], 'm_ssd_a_log': out['m_ssd_a_log'], 'm_ssd_d': out['m_ssd_d'], 'm_ssd_norm_g': out['m_ssd_norm_g'], 'm_even_w_out': out['m_even_w_out'], 'm_odd_norm_g': out['m_odd_norm_g'], 'm_odd_w_in': out['m_odd_w_in'], 'm_sconv_w': out['m_sconv_w'], 'm_odd_w_out': out['m_odd_w_out'], 'm_final_norm_g': out['m_final_norm_g'], 'v_even_norm_g': out['v_even_norm_g'], 'v_even_w_in': out['v_even_w_in'], 'v_gmlp_ln_g': out['v_gmlp_ln_g'], 'v_gmlp_ln_b': out['v_gmlp_ln_b'], 'v_gmlp_ws': out['v_gmlp_ws'], 'v_gmlp_bs': out['v_gmlp_bs'], 'v_ssd_conv_w': out['v_ssd_conv_w'], 'v_ssd_conv_b': out['v_ssd_conv_b'], 'v_ssd_dt_bias': out['v_ssd_dt_bias'], 'v_ssd_a_log': out['v_ssd_a_log'], 'v_ssd_d': out['v_ssd_d'], 'v_ssd_norm_g': out['v_ssd_norm_g'], 'v_even_w_out': out['v_even_w_out'], 'v_odd_norm_g': out['v_odd_norm_g'], 'v_odd_w_in': out['v_odd_w_in'], 'v_sconv_w': out['v_sconv_w'], 'v_odd_w_out': out['v_odd_w_out'], 'v_final_norm_g': out['v_final_norm_g']}


def _loss(weights, diff, rest, loss_target):
    with _jax.named_scope("forward"):
        args = {**rest, TWIN_DIFF_INPUT: diff, **{k: w.astype(_WEIGHT_DTYPES[k]) for k, w in weights.items()}}
        y = _forward(args)
    with _jax.named_scope("loss_head"):
        err = _jnp.square(y.astype(_jnp.float32) - loss_target)
        return 0.5 * _jnp.sum(_jnp.mean(err, axis=-1)) if err.ndim else 0.5 * err


def _adamw(w, g, m, v):
    m = ADAM_B1 * m + (1.0 - ADAM_B1) * g
    v = ADAM_B2 * v + (1.0 - ADAM_B2) * _jnp.square(g)
    m_hat = m / (1.0 - ADAM_B1 ** ADAM_STEP)
    v_hat = v / (1.0 - ADAM_B2 ** ADAM_STEP)
    delta = -ADAM_LR * (m_hat / (_jnp.sqrt(v_hat) + ADAM_EPS) + ADAM_WD * w)
    return delta, m, v


def reference(x, even_norm_g, even_w_in, gmlp_ln_g, gmlp_ln_b, gmlp_ws, gmlp_bs, ssd_conv_w, ssd_conv_b, ssd_dt_bias, ssd_a_log, ssd_d, ssd_norm_g, even_w_out, odd_norm_g, odd_w_in, sconv_w, odd_w_out, final_norm_g, loss_target, m_even_norm_g, m_even_w_in, m_gmlp_ln_g, m_gmlp_ln_b, m_gmlp_ws, m_gmlp_bs, m_ssd_conv_w, m_ssd_conv_b, m_ssd_dt_bias, m_ssd_a_log, m_ssd_d, m_ssd_norm_g, m_even_w_out, m_odd_norm_g, m_odd_w_in, m_sconv_w, m_odd_w_out, m_final_norm_g, v_even_norm_g, v_even_w_in, v_gmlp_ln_g, v_gmlp_ln_b, v_gmlp_ws, v_gmlp_bs, v_ssd_conv_w, v_ssd_conv_b, v_ssd_dt_bias, v_ssd_a_log, v_ssd_d, v_ssd_norm_g, v_even_w_out, v_odd_norm_g, v_odd_w_in, v_sconv_w, v_odd_w_out, v_final_norm_g):
    given = dict(x=x, even_norm_g=even_norm_g, even_w_in=even_w_in, gmlp_ln_g=gmlp_ln_g, gmlp_ln_b=gmlp_ln_b, gmlp_ws=gmlp_ws, gmlp_bs=gmlp_bs, ssd_conv_w=ssd_conv_w, ssd_conv_b=ssd_conv_b, ssd_dt_bias=ssd_dt_bias, ssd_a_log=ssd_a_log, ssd_d=ssd_d, ssd_norm_g=ssd_norm_g, even_w_out=even_w_out, odd_norm_g=odd_norm_g, odd_w_in=odd_w_in, sconv_w=sconv_w, odd_w_out=odd_w_out, final_norm_g=final_norm_g, loss_target=loss_target, m_even_norm_g=m_even_norm_g, m_even_w_in=m_even_w_in, m_gmlp_ln_g=m_gmlp_ln_g, m_gmlp_ln_b=m_gmlp_ln_b, m_gmlp_ws=m_gmlp_ws, m_gmlp_bs=m_gmlp_bs, m_ssd_conv_w=m_ssd_conv_w, m_ssd_conv_b=m_ssd_conv_b, m_ssd_dt_bias=m_ssd_dt_bias, m_ssd_a_log=m_ssd_a_log, m_ssd_d=m_ssd_d, m_ssd_norm_g=m_ssd_norm_g, m_even_w_out=m_even_w_out, m_odd_norm_g=m_odd_norm_g, m_odd_w_in=m_odd_w_in, m_sconv_w=m_sconv_w, m_odd_w_out=m_odd_w_out, m_final_norm_g=m_final_norm_g, v_even_norm_g=v_even_norm_g, v_even_w_in=v_even_w_in, v_gmlp_ln_g=v_gmlp_ln_g, v_gmlp_ln_b=v_gmlp_ln_b, v_gmlp_ws=v_gmlp_ws, v_gmlp_bs=v_gmlp_bs, v_ssd_conv_w=v_ssd_conv_w, v_ssd_conv_b=v_ssd_conv_b, v_ssd_dt_bias=v_ssd_dt_bias, v_ssd_a_log=v_ssd_a_log, v_ssd_d=v_ssd_d, v_ssd_norm_g=v_ssd_norm_g, v_even_w_out=v_even_w_out, v_odd_norm_g=v_odd_norm_g, v_odd_w_in=v_odd_w_in, v_sconv_w=v_sconv_w, v_odd_w_out=v_odd_w_out, v_final_norm_g=v_final_norm_g)
    weights = {n: given[n] for n in TWIN_WEIGHTS}
    shared = {n: given[n] for n in SHARED_INPUTS}
    per_example = {n: given[n] for n in ['x']}
    grad_fn = _jax.value_and_grad(_loss, argnums=(0, 1))

    def one_microbatch(ex, loss_target):
        ex = dict(ex)
        diff = ex.pop(TWIN_DIFF_INPUT)
        return grad_fn(weights, diff, {**shared, **ex}, loss_target)

    if N_MICROBATCH == 1:
        loss, (grad_w, grad_x) = one_microbatch(per_example, given["loss_target"])
    else:
        def body(carry, xs):
            loss_sum, grad_sum = carry
            l_k, (gw_k, gx_k) = one_microbatch(xs[0], xs[1])
            with _jax.named_scope("update"):
                return (loss_sum + l_k, _jax.tree.map(_jnp.add, grad_sum, gw_k)), gx_k

        init = (_jnp.zeros((), _jnp.float32), _jax.tree.map(_jnp.zeros_like, weights))
        (loss, grad_w), grad_x = _jax.lax.scan(body, init, (per_example, given["loss_target"]))
    with _jax.named_scope("update"):
        delta_w, new_m, new_v = {}, {}, {}
        for n in TWIN_WEIGHTS:
            delta_w[n], new_m[n], new_v[n] = _adamw(weights[n], grad_w[n], given["m_" + n], given["v_" + n])
    return (loss, grad_x, *[grad_w[n] for n in TWIN_WEIGHTS], *[delta_w[n] for n in TWIN_WEIGHTS],
            *[new_m[n] for n in TWIN_WEIGHTS], *[new_v[n] for n in TWIN_WEIGHTS])
```

```python
import functools

import jax
import jax.numpy as jnp
from jax import lax
from jax.experimental import pallas as pl
from jax.experimental.pallas import tpu as pltpu

F32 = jnp.float32
BF16 = jnp.bfloat16
SDS = jax.ShapeDtypeStruct
MESH = pl.DeviceIdType.MESH

D_MODEL = 2048
A_WIDTH = 2048
A_GROUPS = 8
CHUNK = 128
B_WIDTH = 2048
B_HEADS = 32
B_HEAD_DIM = 64
B_GROUPS = 8
B_STATE = 128
B_CONV = 4
B_XBC = B_WIDTH + 2 * B_GROUPS * B_STATE
C_WIDTH = 2048
C_CONV = 3
D_HEADS = 16
D_HEAD_DIM = 128
D_PATTERNS = ((128, 1), (512, 4), (2048, 16))
ATT_BLOCK = 128
ATT_SUPER = 2048
EVEN_MAIN = 3 * A_WIDTH + B_WIDTH + B_XBC
IN_EVEN = EVEN_MAIN + B_HEADS
IN_ODD = 4 * C_WIDTH + 4 * D_HEADS * D_HEAD_DIM
LANES = 128
SUBLANES = 8
EPS = 1e-5
ADAM_LR = 0.001
ADAM_B1 = 0.9
ADAM_B2 = 0.999
ADAM_EPS = 1e-08
ADAM_WD = 0.01
ADAM_STEP = 10
N_CHIPS = 4
N_DEV = 8
VMEM_LIMIT_BYTES = 56 * 1024 * 1024


def _cp(*sem):
    return pltpu.CompilerParams(dimension_semantics=sem, vmem_limit_bytes=VMEM_LIMIT_BYTES)


def _full(shape):
    return pl.BlockSpec(shape, lambda *_: (0,) * len(shape))


def _silu(x):
    return x * jax.nn.sigmoid(x)


def _dot_nn(a, b):
    return lax.dot_general(a, b, (((1,), (0,)), ((), ())), preferred_element_type=F32)


def _dot_nt(a, b):
    return lax.dot_general(a, b, (((1,), (1,)), ((), ())), preferred_element_type=F32)


def _dot_tn(a, b):
    return lax.dot_general(a, b, (((0,), (0,)), ((), ())), preferred_element_type=F32)


def _tril(n):
    return lax.broadcasted_iota(jnp.int32, (n, n), 0) >= lax.broadcasted_iota(jnp.int32, (n, n), 1)


_DOTS = {"nn": _dot_nn, "nt": _dot_nt, "tn": _dot_tn}


def matmul(a, b, mode, *, name, out_dtype=F32, res=None, tm=1024, tn=1024, tk=512):
    if mode == "tn":
        (K, M), (K2, N) = a.shape, b.shape
    elif mode == "nt":
        (M, K), (N, K2) = a.shape, b.shape
    else:
        (M, K), (K2, N) = a.shape, b.shape
    assert K == K2, (a.shape, b.shape, mode)
    tm, tn, tk = min(tm, M), min(tn, N), min(tk, K)
    assert M % tm == 0 and N % tn == 0 and K % tk == 0, (M, N, K, tm, tn, tk)
    nk = K // tk
    dot = _DOTS[mode]

    def body(*refs):
        if res is None:
            a_ref, b_ref, o_ref, acc_ref = refs
        else:
            a_ref, b_ref, r_ref, o_ref, acc_ref = refs
        k = pl.program_id(2)

        @pl.when(k == 0)
        def _():
            acc_ref[...] = jnp.zeros_like(acc_ref)

        acc_ref[...] += dot(a_ref[...], b_ref[...])

        @pl.when(k == nk - 1)
        def _():
            acc = acc_ref[...]
            if res is not None:
                acc = acc + r_ref[...]
            o_ref[...] = acc.astype(o_ref.dtype)

    a_spec = pl.BlockSpec((tk, tm), lambda i, j, k: (k, i)) if mode == "tn" else pl.BlockSpec((tm, tk), lambda i, j, k: (i, k))
    b_spec = pl.BlockSpec((tn, tk), lambda i, j, k: (j, k)) if mode == "nt" else pl.BlockSpec((tk, tn), lambda i, j, k: (k, j))
    o_spec = pl.BlockSpec((tm, tn), lambda i, j, k: (i, j))
    in_specs, args = [a_spec, b_spec], [a, b]
    if res is not None:
        in_specs.append(o_spec)
        args.append(res)
    return pl.pallas_call(
        body, name=name, grid=(M // tm, N // tn, nk), in_specs=in_specs, out_specs=o_spec,
        out_shape=SDS((M, N), out_dtype), scratch_shapes=[pltpu.VMEM((tm, tn), F32)],
        compiler_params=_cp("parallel", "parallel", "arbitrary"),
    )(*args)


ROW_TILE = 512


def _rms(x, g):
    return x * lax.rsqrt(jnp.mean(x * x, axis=-1, keepdims=True) + EPS) * g


def rmsnorm_fwd(x, g, *, name):
    T, D = x.shape

    def body(x_ref, g_ref, o_ref):
        o_ref[...] = _rms(x_ref[...], g_ref[...]).astype(BF16)

    row = pl.BlockSpec((ROW_TILE, D), lambda i: (i, 0))
    return pl.pallas_call(body, name=name, grid=(T // ROW_TILE,), in_specs=[row, _full((1, D))], out_specs=row,
                          out_shape=SDS((T, D), BF16), compiler_params=_cp("parallel"))(x, g)


def rmsnorm_bwd(x, g, dxn, dres, *, name):
    T, D = x.shape

    def body(x_ref, g_ref, dxn_ref, dres_ref, dx_ref, dxb_ref, dg_ref):
        _, vjp = jax.vjp(_rms, x_ref[...], g_ref[...])
        dx, dg = vjp(dxn_ref[...])
        dx = dx + dres_ref[...]
        dx_ref[...] = dx
        dxb_ref[...] = dx.astype(BF16)

        @pl.when(pl.program_id(0) == 0)
        def _():
            dg_ref[...] = jnp.zeros_like(dg_ref)

        dg_ref[...] += dg

    row = pl.BlockSpec((ROW_TILE, D), lambda i: (i, 0))
    return pl.pallas_call(
        body, name=name, grid=(T // ROW_TILE,), in_specs=[row, _full((1, D)), row, row],
        out_specs=[row, row, _full((1, D))], out_shape=[SDS((T, D), F32), SDS((T, D), BF16), SDS((1, D), F32)],
        compiler_params=_cp("arbitrary"))(x, g, dxn, dres)


def _loss_tile(x, g, tgt):
    err = jnp.square(_rms(x, g) - tgt)
    return 0.5 * jnp.sum(jnp.mean(err, axis=-1))


def loss_head(x, g, tgt):
    T, D = x.shape

    def body(x_ref, g_ref, t_ref, loss_ref, dx_ref, dxb_ref, dg_ref):
        loss, vjp = jax.vjp(_loss_tile, x_ref[...], g_ref[...], t_ref[...])
        dx, dg, _ = vjp(jnp.ones((), F32))
        dx_ref[...] = dx
        dxb_ref[...] = dx.astype(BF16)

        @pl.when(pl.program_id(0) == 0)
        def _():
            dg_ref[...] = jnp.zeros_like(dg_ref)
            loss_ref[...] = jnp.zeros_like(loss_ref)

        dg_ref[...] += dg
        loss_ref[...] += jnp.reshape(loss, (1, 1))

    row = pl.BlockSpec((ROW_TILE, D), lambda i: (i, 0))
    return pl.pallas_call(
        body, name="loss_head", grid=(T // ROW_TILE,), in_specs=[row, _full((1, D)), row],
        out_specs=[_full((1, 1)), row, row, _full((1, D))],
        out_shape=[SDS((1, 1), F32), SDS((T, D), F32), SDS((T, D), BF16), SDS((1, D), F32)],
        compiler_params=_cp("arbitrary"))(x, g, tgt)


TILE_BYTES = 1 << 20


def _row_tile(rows, row_bytes):
    for cand in (512, 256, 128, 64, 32, 16, 8):
        if rows % cand == 0 and cand * row_bytes <= TILE_BYTES:
            return cand
    return rows


def adamw(w, g, m, v, *, name):
    R, C = w.shape
    tr = _row_tile(R, C * 4)

    def body(w_ref, g_ref, m_ref, v_ref, d_ref, nm_ref, nv_ref):
        gg = g_ref[...]
        mm = ADAM_B1 * m_ref[...] + (1.0 - ADAM_B1) * gg
        vv = ADAM_B2 * v_ref[...] + (1.0 - ADAM_B2) * jnp.square(gg)
        m_hat = mm / (1.0 - ADAM_B1 ** ADAM_STEP)
        v_hat = vv / (1.0 - ADAM_B2 ** ADAM_STEP)
        d_ref[...] = -ADAM_LR * (m_hat / (jnp.sqrt(v_hat) + ADAM_EPS) + ADAM_WD * w_ref[...])
        nm_ref[...] = mm
        nv_ref[...] = vv

    blk = pl.BlockSpec((tr, C), lambda i: (i, 0))
    return pl.pallas_call(body, name=name, grid=(R // tr,), in_specs=[blk] * 4, out_specs=[blk] * 3,
                          out_shape=[SDS((R, C), F32)] * 3, compiler_params=_cp("parallel"))(w, g, m, v)


def sum_leading(a, *, name):
    n, R, C = a.shape
    tr = _row_tile(R, n * C * 4)

    def body(a_ref, o_ref):
        acc = a_ref[0]
        for j in range(1, n):
            acc = acc + a_ref[j]
        o_ref[...] = acc

    return pl.pallas_call(body, name=name, grid=(R // tr,), in_specs=[pl.BlockSpec((n, tr, C), lambda i: (0, i, 0))],
                          out_specs=pl.BlockSpec((tr, C), lambda i: (i, 0)), out_shape=SDS((R, C), F32),
                          compiler_params=_cp("parallel"))(a)


def add2(a, b, *, name):
    n, R, C = a.shape

    def body(a_ref, b_ref, o_ref):
        o_ref[...] = a_ref[...] + b_ref[...]

    tr = _row_tile(R, C * 4)
    blk = pl.BlockSpec((1, tr, C), lambda k, i: (k, i, 0))
    return pl.pallas_call(body, name=name, grid=(n, R // tr), in_specs=[blk, blk], out_specs=blk,
                          out_shape=SDS((n, R, C), F32), compiler_params=_cp("parallel", "parallel"))(a, b)


def _gmlp_chunk(u, v, z, ln_g, ln_b, wsc, bs_t):
    mu = jnp.mean(v, axis=-1, keepdims=True)
    xc = v - mu
    vn = xc * lax.rsqrt(jnp.mean(xc * xc, axis=-1, keepdims=True) + EPS) * ln_g + ln_b
    gw = A_WIDTH // A_GROUPS
    outs = []
    for g in range(A_GROUPS):
        m = _dot_nn(wsc[g].astype(BF16), vn[:, g * gw:(g + 1) * gw].astype(BF16))
        outs.append(m + bs_t[:, g:g + 1])
    return _silu(z) * (u * jnp.concatenate(outs, axis=1))


def _h_cols(width, idx, rows=CHUNK):
    return pl.BlockSpec((rows, width), lambda i: (i, idx))


def gmlp_fwd(h, ln_g, ln_b, ws, bs_t):
    T = h.shape[0]

    def body(u_ref, v_ref, z_ref, g_ref, b_ref, ws_ref, bs_ref, o_ref):
        wsc = jnp.where(_tril(CHUNK)[None], ws_ref[...], 0.0)
        o_ref[...] = _gmlp_chunk(u_ref[...], v_ref[...], z_ref[...], g_ref[...], b_ref[...], wsc, bs_ref[...]).astype(BF16)

    return pl.pallas_call(
        body, name="gmlp_fwd", grid=(T // CHUNK,),
        in_specs=[_h_cols(A_WIDTH, 0), _h_cols(A_WIDTH, 1), _h_cols(A_WIDTH, 2), _full((1, A_WIDTH)), _full((1, A_WIDTH)),
                  _full((A_GROUPS, CHUNK, CHUNK)), _full((CHUNK, A_GROUPS))],
        out_specs=_h_cols(A_WIDTH, 0), out_shape=SDS((T, A_WIDTH), BF16), compiler_params=_cp("parallel"),
    )(h, h, h, ln_g, ln_b, ws, bs_t)


def gmlp_bwd(h, dy, ln_g, ln_b, ws, bs_t):
    T = h.shape[0]

    def body(u_ref, v_ref, z_ref, dy_ref, g_ref, b_ref, ws_ref, bs_ref, du_ref, dv_ref, dz_ref, dg_ref, db_ref, dws_ref, dbs_ref):
        tri = _tril(CHUNK)[None]
        wsc = jnp.where(tri, ws_ref[...], 0.0)
        _, vjp = jax.vjp(_gmlp_chunk, u_ref[...], v_ref[...], z_ref[...], g_ref[...], b_ref[...], wsc, bs_ref[...])
        du, dv, dz, dg, db, dws, dbs = vjp(dy_ref[...])
        du_ref[...] = du.astype(BF16)
        dv_ref[...] = dv.astype(BF16)
        dz_ref[...] = dz.astype(BF16)

        @pl.when(pl.program_id(0) == 0)
        def _():
            dg_ref[...] = jnp.zeros_like(dg_ref)
            db_ref[...] = jnp.zeros_like(db_ref)
            dws_ref[...] = jnp.zeros_like(dws_ref)
            dbs_ref[...] = jnp.zeros_like(dbs_ref)

        dg_ref[...] += dg
        db_ref[...] += db
        dws_ref[...] += jnp.where(tri, dws, 0.0)
        dbs_ref[...] += dbs

    row = _h_cols(A_WIDTH, 0)
    pshapes = [(1, A_WIDTH), (1, A_WIDTH), (A_GROUPS, CHUNK, CHUNK), (CHUNK, A_GROUPS)]
    return pl.pallas_call(
        body, name="gmlp_bwd", grid=(T // CHUNK,),
        in_specs=[_h_cols(A_WIDTH, 0), _h_cols(A_WIDTH, 1), _h_cols(A_WIDTH, 2), _h_cols(A_WIDTH, 0)] + [_full(s) for s in pshapes],
        out_specs=[row, row, row] + [_full(s) for s in pshapes],
        out_shape=[SDS((T, A_WIDTH), BF16)] * 3 + [SDS(s, F32) for s in pshapes],
        compiler_params=_cp("arbitrary"),
    )(h, h, h, dy, ln_g, ln_b, ws, bs_t)


CONV_ROWS = 256
CONV_COLS = 512


def _taps(xe, w, rows):
    K = w.shape[0]
    acc = None
    for k in range(K):
        off = SUBLANES - (K - 1) + k
        term = w[k:k + 1, :] * xe[off:off + rows, :]
        acc = term if acc is None else acc + term
    return acc


def _ssd_conv_tile(x, halo, w, b):
    return _silu(_taps(jnp.concatenate([halo, x], axis=0), w, x.shape[0]) + b)


def _halo_spec(cols, col_idx, nt=None):
    rpb = CONV_ROWS // SUBLANES
    if nt is None:
        return pl.BlockSpec((SUBLANES, cols), lambda c, i: (jnp.maximum(i * rpb - 1, 0), col_idx(c)))
    return pl.BlockSpec((SUBLANES, cols), lambda c, j: (jnp.maximum((nt - 1 - j) * rpb - 1, 0), col_idx(c)))


def ssd_conv_fwd(h, w, b):
    T = h.shape[0]
    nc = B_XBC // CONV_COLS
    base = (3 * A_WIDTH + B_WIDTH) // CONV_COLS

    def body(x_ref, halo_ref, w_ref, b_ref, o_ref):
        halo = jnp.where(pl.program_id(1) > 0, halo_ref[...], 0.0)
        o_ref[...] = _ssd_conv_tile(x_ref[...], halo, w_ref[...], b_ref[...])

    return pl.pallas_call(
        body, name="ssd_conv_fwd", grid=(nc, T // CONV_ROWS),
        in_specs=[pl.BlockSpec((CONV_ROWS, CONV_COLS), lambda c, i: (i, base + c)), _halo_spec(CONV_COLS, lambda c: base + c),
                  pl.BlockSpec((B_CONV, CONV_COLS), lambda c, i: (0, c)), pl.BlockSpec((1, CONV_COLS), lambda c, i: (0, c))],
        out_specs=pl.BlockSpec((CONV_ROWS, CONV_COLS), lambda c, i: (i, c)),
        out_shape=SDS((T, B_XBC), F32), compiler_params=_cp("parallel", "parallel"),
    )(h, h, w, b)


def ssd_conv_bwd(h, dy, w, b):
    T = h.shape[0]
    nc = B_XBC // CONV_COLS
    nt = T // CONV_ROWS
    base = (3 * A_WIDTH + B_WIDTH) // CONV_COLS

    def body(x_ref, halo_ref, dy_ref, w_ref, b_ref, dx_ref, dw_ref, db_ref, carry_ref):
        j = pl.program_id(1)
        halo = jnp.where(j < nt - 1, halo_ref[...], 0.0)
        _, vjp = jax.vjp(_ssd_conv_tile, x_ref[...], halo, w_ref[...], b_ref[...])
        dx, dhalo, dw, db = vjp(dy_ref[...])

        @pl.when(j == 0)
        def _():
            carry_ref[...] = jnp.zeros_like(carry_ref)
            dw_ref[...] = jnp.zeros_like(dw_ref)
            db_ref[...] = jnp.zeros_like(db_ref)

        tail = dx[CONV_ROWS - SUBLANES:, :] + carry_ref[...]
        dx_ref[...] = jnp.concatenate([dx[:CONV_ROWS - SUBLANES, :], tail], axis=0).astype(BF16)
        carry_ref[...] = dhalo
        dw_ref[...] += dw
        db_ref[...] += db

    return pl.pallas_call(
        body, name="ssd_conv_bwd", grid=(nc, nt),
        in_specs=[pl.BlockSpec((CONV_ROWS, CONV_COLS), lambda c, j: (nt - 1 - j, base + c)),
                  _halo_spec(CONV_COLS, lambda c: base + c, nt),
                  pl.BlockSpec((CONV_ROWS, CONV_COLS), lambda c, j: (nt - 1 - j, c)),
                  pl.BlockSpec((B_CONV, CONV_COLS), lambda c, j: (0, c)), pl.BlockSpec((1, CONV_COLS), lambda c, j: (0, c))],
        out_specs=[pl.BlockSpec((CONV_ROWS, CONV_COLS), lambda c, j: (nt - 1 - j, c)),
                   pl.BlockSpec((B_CONV, CONV_COLS), lambda c, j: (0, c)), pl.BlockSpec((1, CONV_COLS), lambda c, j: (0, c))],
        out_shape=[SDS((T, B_XBC), BF16), SDS((B_CONV, B_XBC), F32), SDS((1, B_XBC), F32)],
        scratch_shapes=[pltpu.VMEM((SUBLANES, CONV_COLS), F32)],
        compiler_params=_cp("parallel", "arbitrary"),
    )(h, h, dy, w, b)


def _sconv_tile(bg, cg, hx, z, cg_halo, hx_halo, w):
    ch = jnp.concatenate([cg_halo * hx_halo, cg * hx], axis=0)
    return _silu(z) * (bg * _taps(ch, w, bg.shape[0]))


def sconv_fwd(h, w):
    T = h.shape[0]
    nc = C_WIDTH // CONV_COLS

    def col(seg):
        return pl.BlockSpec((CONV_ROWS, CONV_COLS), lambda c, i: (i, seg * nc + c))

    def body(bg_ref, cg_ref, hx_ref, z_ref, cgh_ref, hxh_ref, w_ref, o_ref):
        first = pl.program_id(1) == 0
        cgh = jnp.where(first, 0.0, cgh_ref[...])
        hxh = jnp.where(first, 0.0, hxh_ref[...])
        o_ref[...] = _sconv_tile(bg_ref[...], cg_ref[...], hx_ref[...], z_ref[...], cgh, hxh, w_ref[...]).astype(BF16)

    return pl.pallas_call(
        body, name="sconv_fwd", grid=(nc, T // CONV_ROWS),
        in_specs=[col(0), col(1), col(2), col(3), _halo_spec(CONV_COLS, lambda c: nc + c), _halo_spec(CONV_COLS, lambda c: 2 * nc + c),
                  pl.BlockSpec((C_CONV, CONV_COLS), lambda c, i: (0, c))],
        out_specs=pl.BlockSpec((CONV_ROWS, CONV_COLS), lambda c, i: (i, c)),
        out_shape=SDS((T, C_WIDTH), BF16), compiler_params=_cp("parallel", "parallel"),
    )(h, h, h, h, h, h, w)


def sconv_bwd(h, dy, w):
    T = h.shape[0]
    nc = C_WIDTH // CONV_COLS
    nt = T // CONV_ROWS

    def col(seg):
        return pl.BlockSpec((CONV_ROWS, CONV_COLS), lambda c, j: (nt - 1 - j, seg * nc + c))

    def body(bg_ref, cg_ref, hx_ref, z_ref, cgh_ref, hxh_ref, dy_ref, w_ref, dbg_ref, dcg_ref, dhx_ref, dz_ref, dw_ref, ccg_ref, chx_ref):
        j = pl.program_id(1)
        first = j == nt - 1
        cgh = jnp.where(first, 0.0, cgh_ref[...])
        hxh = jnp.where(first, 0.0, hxh_ref[...])
        _, vjp = jax.vjp(_sconv_tile, bg_ref[...], cg_ref[...], hx_ref[...], z_ref[...], cgh, hxh, w_ref[...])
        dbg, dcg, dhx, dz, dcgh, dhxh, dw = vjp(dy_ref[...])

        @pl.when(j == 0)
        def _():
            ccg_ref[...] = jnp.zeros_like(ccg_ref)
            chx_ref[...] = jnp.zeros_like(chx_ref)
            dw_ref[...] = jnp.zeros_like(dw_ref)

        def with_carry(d, carry_ref):
            tail = d[CONV_ROWS - SUBLANES:, :] + carry_ref[...]
            return jnp.concatenate([d[:CONV_ROWS - SUBLANES, :], tail], axis=0).astype(BF16)

        dbg_ref[...] = dbg.astype(BF16)
        dz_ref[...] = dz.astype(BF16)
        dcg_ref[...] = with_carry(dcg, ccg_ref)
        dhx_ref[...] = with_carry(dhx, chx_ref)
        ccg_ref[...] = dcgh
        chx_ref[...] = dhxh
        dw_ref[...] += dw

    out_row = pl.BlockSpec((CONV_ROWS, CONV_COLS), lambda c, j: (nt - 1 - j, c))
    wspec = pl.BlockSpec((C_CONV, CONV_COLS), lambda c, j: (0, c))
    return pl.pallas_call(
        body, name="sconv_bwd", grid=(nc, nt),
        in_specs=[col(0), col(1), col(2), col(3), _halo_spec(CONV_COLS, lambda c: nc + c, nt), _halo_spec(CONV_COLS, lambda c: 2 * nc + c, nt),
                  out_row, wspec],
        out_specs=[out_row] * 4 + [wspec],
        out_shape=[SDS((T, C_WIDTH), BF16)] * 4 + [SDS((C_CONV, C_WIDTH), F32)],
        scratch_shapes=[pltpu.VMEM((SUBLANES, CONV_COLS), F32)] * 2,
        compiler_params=_cp("parallel", "arbitrary"),
    )(h, h, h, h, h, h, dy, w)


def _softplus(x):
    return jnp.maximum(x, 0.0) + jnp.log(1.0 + jnp.exp(-jnp.abs(x)))


def _ssd_chunk(xs, bm, cm, dtr, z, prev, dt_bias, a_log, d_skip, norm_g):
    tril = _tril(CHUNK)
    dt = _softplus(dtr + dt_bias)
    adt = dt * (-jnp.exp(a_log))
    a_cs = jnp.dot(tril.astype(F32), adt, precision=lax.Precision.HIGHEST, preferred_element_type=F32)
    a_cs_t = a_cs.T
    a_last = a_cs[CHUNK - 1:CHUNK, :]
    decay_st = jnp.exp(a_last - a_cs)
    e_cs = jnp.exp(a_cs)
    chunk_decay = jnp.exp(a_last)
    hpg = B_HEADS // B_GROUPS
    ys, nxt = [], []
    for g in range(B_GROUPS):
        bg = bm[:, g * B_STATE:(g + 1) * B_STATE].astype(BF16)
        cg = cm[:, g * B_STATE:(g + 1) * B_STATE].astype(BF16)
        cb = _dot_nt(cg, bg)
        for r in range(hpg):
            hh = g * hpg + r
            xh = xs[:, hh * B_HEAD_DIM:(hh + 1) * B_HEAD_DIM]
            xdt = xh * dt[:, hh:hh + 1]
            seg = jnp.where(tril, a_cs[:, hh:hh + 1] - a_cs_t[hh:hh + 1, :], -jnp.inf)
            y_diag = _dot_nn((cb * jnp.exp(seg)).astype(BF16), xdt.astype(BF16))
            st = _dot_tn((xdt * decay_st[:, hh:hh + 1]).astype(BF16), bg)
            ph = prev[hh]
            y_off = _dot_nt(cg, ph.astype(BF16)) * e_cs[:, hh:hh + 1]
            nxt.append(ph * chunk_decay[:, hh:hh + 1] + st)
            ys.append(y_diag + y_off + d_skip[:, hh:hh + 1] * xh)
    y = jnp.concatenate(ys, axis=1) * _silu(z)
    gw = B_WIDTH // B_GROUPS
    outs = []
    for g in range(B_GROUPS):
        yg = y[:, g * gw:(g + 1) * gw]
        outs.append(yg * lax.rsqrt(jnp.mean(yg * yg, axis=-1, keepdims=True) + EPS))
    return jnp.concatenate(outs, axis=1) * norm_g, jnp.stack(nxt)


_SSD_PARAM_SHAPES = [(1, LANES), (1, LANES), (1, LANES), (1, B_WIDTH)]
_STATE_SHAPE = (B_HEADS, B_HEAD_DIM, B_STATE)


def ssd_fwd(xbc, dtr, h, dt_bias, a_log, d_skip, norm_g):
    T = xbc.shape[0]
    nc = T // CHUNK

    def body(xs_ref, b_ref, c_ref, dt_ref, z_ref, p0, p1, p2, p3, y_ref, st_ref, state):
        @pl.when(pl.program_id(0) == 0)
        def _():
            state[...] = jnp.zeros_like(state)

        prev = state[...]
        st_ref[0] = prev
        yb, nxt = _ssd_chunk(xs_ref[...], b_ref[...], c_ref[...], dt_ref[...], z_ref[...], prev, p0[...], p1[...], p2[...], p3[...])
        y_ref[...] = yb.astype(BF16)
        state[...] = nxt

    return pl.pallas_call(
        body, name="ssd_fwd", grid=(nc,),
        in_specs=[_h_cols(B_WIDTH, 0), _h_cols(B_GROUPS * B_STATE, 2), _h_cols(B_GROUPS * B_STATE, 3), _h_cols(LANES, 0), _h_cols(B_WIDTH, 3)]
        + [_full(s) for s in _SSD_PARAM_SHAPES],
        out_specs=[_h_cols(B_WIDTH, 0), pl.BlockSpec((1,) + _STATE_SHAPE, lambda i: (i, 0, 0, 0))],
        out_shape=[SDS((T, B_WIDTH), BF16), SDS((nc,) + _STATE_SHAPE, F32)],
        scratch_shapes=[pltpu.VMEM(_STATE_SHAPE, F32)], compiler_params=_cp("arbitrary"),
    )(xbc, xbc, xbc, dtr, h, dt_bias, a_log, d_skip, norm_g)


def ssd_bwd(xbc, dtr, h, states, dy, dt_bias, a_log, d_skip, norm_g):
    T = xbc.shape[0]
    nc = T // CHUNK

    def rev(width, idx):
        return pl.BlockSpec((CHUNK, width), lambda j: (nc - 1 - j, idx))

    def body(xs_ref, b_ref, c_ref, dt_ref, z_ref, st_ref, dy_ref, p0, p1, p2, p3,
             dxs_ref, db_ref, dc_ref, ddt_ref, dz_ref, g0, g1, g2, g3, dstate):
        @pl.when(pl.program_id(0) == 0)
        def _():
            dstate[...] = jnp.zeros_like(dstate)
            for gref in (g0, g1, g2, g3):
                gref[...] = jnp.zeros_like(gref)

        _, vjp = jax.vjp(_ssd_chunk, xs_ref[...], b_ref[...], c_ref[...], dt_ref[...], z_ref[...], st_ref[0],
                         p0[...], p1[...], p2[...], p3[...])
        dxs, dbm, dcm, ddt, dz, dprev, d0, d1, d2, d3 = vjp((dy_ref[...], dstate[...]))
        dxs_ref[...] = dxs
        db_ref[...] = dbm
        dc_ref[...] = dcm
        ddt_ref[...] = ddt
        dz_ref[...] = dz.astype(BF16)
        dstate[...] = dprev
        g0[...] += d0
        g1[...] += d1
        g2[...] += d2
        g3[...] += d3

    gn = B_GROUPS * B_STATE
    return pl.pallas_call(
        body, name="ssd_bwd", grid=(nc,),
        in_specs=[rev(B_WIDTH, 0), rev(gn, 2), rev(gn, 3), rev(LANES, 0), rev(B_WIDTH, 3),
                  pl.BlockSpec((1,) + _STATE_SHAPE, lambda j: (nc - 1 - j, 0, 0, 0)), rev(B_WIDTH, 1)]
        + [_full(s) for s in _SSD_PARAM_SHAPES],
        out_specs=[rev(B_WIDTH, 0), rev(gn, 0), rev(gn, 0), rev(LANES, 0), rev(B_WIDTH, 0)] + [_full(s) for s in _SSD_PARAM_SHAPES],
        out_shape=[SDS((T, B_WIDTH), F32), SDS((T, gn), F32), SDS((T, gn), F32), SDS((T, LANES), F32), SDS((T, B_WIDTH), BF16)]
        + [SDS(s, F32) for s in _SSD_PARAM_SHAPES],
        scratch_shapes=[pltpu.VMEM(_STATE_SHAPE, F32)], compiler_params=_cp("arbitrary"),
    )(xbc, xbc, xbc, dtr, h, states, dy, dt_bias, a_log, d_skip, norm_g)


ATT_SCALE = D_HEAD_DIM ** -0.5
Q_COL, K_COL, V_COL, Z_COL = (4 * C_WIDTH // LANES + i * D_HEADS for i in range(4))


def _att_blocks():
    out = []
    for pi, (_, dil) in enumerate(D_PATTERNS):
        nb = ATT_SUPER // (ATT_BLOCK * dil)
        for r in range(dil):
            for b in range(nb):
                start = r + dil * ATT_BLOCK * b
                if b > 0:
                    out.append((pi, dil, start, False, start - dil * ATT_BLOCK))
                else:
                    out.append((pi, dil, start, True, r + dil * ATT_BLOCK * (nb - 1)))
    return out


def _rows(start, dil):
    return pl.ds(start, ATT_BLOCK) if dil == 1 else pl.ds(start, ATT_BLOCK, stride=dil)


def _att_masks():
    row = lax.broadcasted_iota(jnp.int32, (ATT_BLOCK, ATT_BLOCK), 0)
    col = lax.broadcasted_iota(jnp.int32, (ATT_BLOCK, ATT_BLOCK), 1)
    return col <= row, col >= row


def _att_spec(col0, shift=0, last=None):
    def imap(hh, n):
        m = n + shift
        if shift < 0:
            m = jnp.maximum(m, 0)
        if shift > 0:
            m = jnp.minimum(m, last)
        return (m, col0 + hh)
    return pl.BlockSpec((ATT_SUPER, D_HEAD_DIM), imap)


def _att_out_spec():
    return pl.BlockSpec((ATT_SUPER, D_HEAD_DIM), lambda hh, n: (n, hh))


def attn_fwd(h):
    T = h.shape[0]
    blocks = _att_blocks()
    npat = len(D_PATTERNS)

    def body(q_ref, kc_ref, kp_ref, vc_ref, vp_ref, z_ref, yd_ref, o_ref, lse_ref, *scratch):
        o_s, l_s = scratch[:npat], scratch[npat:]
        has_prev = pl.program_id(1) > 0
        mask_c, mask_p = _att_masks()
        for pi, dil, start, from_prev, pstart in blocks:
            rows, prows = _rows(start, dil), _rows(pstart, dil)
            q = q_ref[rows, :].astype(BF16)
            kc = kc_ref[rows, :].astype(BF16)
            vc = vc_ref[rows, :].astype(BF16)
            kp = (kp_ref if from_prev else kc_ref)[prows, :].astype(BF16)
            vp = (vp_ref if from_prev else vc_ref)[prows, :].astype(BF16)
            mp = jnp.logical_and(mask_p, has_prev) if from_prev else mask_p
            s_c = jnp.where(mask_c, _dot_nt(q, kc) * ATT_SCALE, -jnp.inf)
            s_p = jnp.where(mp, _dot_nt(q, kp) * ATT_SCALE, -jnp.inf)
            m = jnp.maximum(jnp.max(s_c, axis=-1, keepdims=True), jnp.max(s_p, axis=-1, keepdims=True))
            p_c = jnp.exp(s_c - m)
            p_p = jnp.exp(s_p - m)
            l = jnp.sum(p_c, axis=-1, keepdims=True) + jnp.sum(p_p, axis=-1, keepdims=True)
            o = _dot_nn((p_c / l).astype(BF16), vc) + _dot_nn((p_p / l).astype(BF16), vp)
            o_s[pi][rows, :] = o
            l_s[pi][rows, :] = jnp.broadcast_to(m + jnp.log(l), (ATT_BLOCK, D_HEAD_DIM))
        lses = [l_s[pi][...] for pi in range(npat)]
        mx = functools.reduce(jnp.maximum, lses)
        ws = [jnp.exp(l - mx) for l in lses]
        den = functools.reduce(lambda a, b: a + b, ws)
        o = functools.reduce(lambda a, b: a + b, [(w / den) * o_s[pi][...] for pi, w in enumerate(ws)])
        o_ref[...] = o
        lse_ref[...] = mx + jnp.log(den)
        yd_ref[...] = (_silu(z_ref[...]) * o).astype(BF16)

    n_super = T // ATT_SUPER
    return pl.pallas_call(
        body, name="attn_fwd", grid=(D_HEADS, n_super),
        in_specs=[_att_spec(Q_COL), _att_spec(K_COL), _att_spec(K_COL, -1), _att_spec(V_COL), _att_spec(V_COL, -1), _att_spec(Z_COL)],
        out_specs=[_att_out_spec()] * 3,
        out_shape=[SDS((T, D_HEADS * D_HEAD_DIM), BF16), SDS((T, D_HEADS * D_HEAD_DIM), F32), SDS((T, D_HEADS * D_HEAD_DIM), F32)],
        scratch_shapes=[pltpu.VMEM((ATT_SUPER, D_HEAD_DIM), F32)] * (2 * npat),
        compiler_params=_cp("parallel", "arbitrary"),
    )(h, h, h, h, h, h)


def _dsilu(z):
    s = jax.nn.sigmoid(z)
    return s * (1.0 + z * (1.0 - s))


def attn_bwd_dq(h, o, lse, dy):
    T = h.shape[0]
    blocks = _att_blocks()
    dy_col = C_WIDTH // LANES

    def body(q_ref, kc_ref, kp_ref, vc_ref, vp_ref, z_ref, o_ref, lse_ref, dy_ref, dq_ref, dz_ref, do_s, dd_s, dq_s):
        has_prev = pl.program_id(1) > 0
        mask_c, mask_p = _att_masks()
        z, oo, dyd = z_ref[...], o_ref[...], dy_ref[...]
        do = dyd * _silu(z)
        dz_ref[...] = (dyd * oo * _dsilu(z)).astype(BF16)
        do_s[...] = do
        dd_s[...] = jnp.broadcast_to(jnp.sum(do * oo, axis=-1, keepdims=True), (ATT_SUPER, D_HEAD_DIM))
        for pi, dil, start, from_prev, pstart in blocks:
            rows, prows = _rows(start, dil), _rows(pstart, dil)
            q = q_ref[rows, :].astype(BF16)
            kc = kc_ref[rows, :].astype(BF16)
            vc = vc_ref[rows, :].astype(BF16)
            kp = (kp_ref if from_prev else kc_ref)[prows, :].astype(BF16)
            vp = (vp_ref if from_prev else vc_ref)[prows, :].astype(BF16)
            mp = jnp.logical_and(mask_p, has_prev) if from_prev else mask_p
            lse_b, dd_b, do_b = lse_ref[rows, :], dd_s[rows, :], do_s[rows, :].astype(BF16)
            p_c = jnp.where(mask_c, jnp.exp(_dot_nt(q, kc) * ATT_SCALE - lse_b), 0.0)
            p_p = jnp.where(mp, jnp.exp(_dot_nt(q, kp) * ATT_SCALE - lse_b), 0.0)
            ds_c = p_c * (_dot_nt(do_b, vc) - dd_b) * ATT_SCALE
            ds_p = p_p * (_dot_nt(do_b, vp) - dd_b) * ATT_SCALE
            dq = _dot_nn(ds_c.astype(BF16), kc) + _dot_nn(ds_p.astype(BF16), kp)
            if pi == 0:
                dq_s[rows, :] = dq
            else:
                dq_s[rows, :] += dq
        dq_ref[...] = dq_s[...].astype(BF16)

    n_super = T // ATT_SUPER
    blk = (ATT_SUPER, D_HEAD_DIM)
    return pl.pallas_call(
        body, name="attn_bwd_dq", grid=(D_HEADS, n_super),
        in_specs=[_att_spec(Q_COL), _att_spec(K_COL), _att_spec(K_COL, -1), _att_spec(V_COL), _att_spec(V_COL, -1), _att_spec(Z_COL),
                  _att_spec(0), _att_spec(0), _att_spec(dy_col)],
        out_specs=[_att_out_spec()] * 2,
        out_shape=[SDS((T, D_HEADS * D_HEAD_DIM), BF16)] * 2,
        scratch_shapes=[pltpu.VMEM(blk, F32)] * 3,
        compiler_params=_cp("parallel", "arbitrary"),
    )(h, h, h, h, h, h, o, lse, dy)


def attn_bwd_dkv(h, o, lse, dy):
    T = h.shape[0]
    n_super = T // ATT_SUPER
    last = n_super - 1
    dy_col = C_WIDTH // LANES
    npat = len(D_PATTERNS)

    def body(k_ref, v_ref, qc_ref, qn_ref, zc_ref, zn_ref, oc_ref, on_ref, lc_ref, ln_ref, dyc_ref, dyn_ref, dk_ref, dv_ref,
             do_c, do_n, dd_c, dd_n, dk_s, dv_s):
        do_s, dd_s = (do_c, do_n), (dd_c, dd_n)
        has_next = pl.program_id(1) < last
        mask_c, mask_p = _att_masks()
        for i, (z_ref, oo_ref, dyd_ref) in enumerate(((zc_ref, oc_ref, dyc_ref), (zn_ref, on_ref, dyn_ref))):
            do = dyd_ref[...] * _silu(z_ref[...])
            do_s[i][...] = do
            dd_s[i][...] = jnp.broadcast_to(jnp.sum(do * oo_ref[...], axis=-1, keepdims=True), (ATT_SUPER, D_HEAD_DIM))
        for pi, (_, dil) in enumerate(D_PATTERNS):
            nb = ATT_SUPER // (ATT_BLOCK * dil)
            for r in range(dil):
                for b in range(nb):
                    start = r + dil * ATT_BLOCK * b
                    rows = _rows(start, dil)
                    kb = k_ref[rows, :].astype(BF16)
                    vb = v_ref[rows, :].astype(BF16)
                    dk = dv = None
                    for src, qstart in ((0, start), (0, start + dil * ATT_BLOCK) if b + 1 < nb else (1, r)):
                        own = qstart == start and src == 0
                        qrows = _rows(qstart, dil)
                        q = (qn_ref if src else qc_ref)[qrows, :].astype(BF16)
                        lse_b = (ln_ref if src else lc_ref)[qrows, :]
                        do_b = do_s[src][qrows, :].astype(BF16)
                        dd_b = dd_s[src][qrows, :]
                        mask = mask_c if own else (jnp.logical_and(mask_p, has_next) if src else mask_p)
                        p = jnp.where(mask, jnp.exp(_dot_nt(q, kb) * ATT_SCALE - lse_b), 0.0)
                        ds = p * (_dot_nt(do_b, vb) - dd_b) * ATT_SCALE
                        dv_t = _dot_tn(p.astype(BF16), do_b)
                        dk_t = _dot_tn(ds.astype(BF16), q)
                        dk = dk_t if dk is None else dk + dk_t
                        dv = dv_t if dv is None else dv + dv_t
                    if pi == 0:
                        dk_s[rows, :] = dk
                        dv_s[rows, :] = dv
                    else:
                        dk_s[rows, :] += dk
                        dv_s[rows, :] += dv
        dk_ref[...] = dk_s[...].astype(BF16)
        dv_ref[...] = dv_s[...].astype(BF16)

    blk = (ATT_SUPER, D_HEAD_DIM)

    def pair(col0):
        return [_att_spec(col0), _att_spec(col0, 1, last)]

    return pl.pallas_call(
        body, name="attn_bwd_dkv", grid=(D_HEADS, n_super),
        in_specs=[_att_spec(K_COL), _att_spec(V_COL)] + pair(Q_COL) + pair(Z_COL) + pair(0) + pair(0) + pair(dy_col),
        out_specs=[_att_out_spec()] * 2,
        out_shape=[SDS((T, D_HEADS * D_HEAD_DIM), BF16)] * 2,
        scratch_shapes=[pltpu.VMEM(blk, F32)] * 6,
        compiler_params=_cp("parallel", "arbitrary"),
    )(h, h, h, h, h, h, o, o, lse, lse, dy, dy)


ANY = pl.BlockSpec(memory_space=pl.ANY)
COMM_PARAMS = pltpu.CompilerParams()


def _place():
    x, y, c = lax.axis_index("x"), lax.axis_index("y"), lax.axis_index("c")
    return x, y, c, [(1 - x, y), (x, 1 - y), (1 - x, 1 - y)]


def _rcopy(src, dst, ssem, rsem, dev):
    return pltpu.make_async_remote_copy(src_ref=src, dst_ref=dst, send_sem=ssem, recv_sem=rsem, device_id=dev, device_id_type=MESH)


def gather_shards(arrs):
    n = len(arrs)

    def body(*refs):
        ins, outs = refs[:n], refs[n:2 * n]
        ssem, rsem, lsem = refs[2 * n:]
        x, y, c, chips = _place()
        me, sib = 2 * x + y, (x, y, 1 - c)
        local = [pltpu.make_async_copy(ins[a], outs[a].at[me], lsem.at[a]) for a in range(n)]
        for cp in local:
            cp.start()
        sent = []
        for a in range(n):
            for j, (px, py) in enumerate(chips):
                cp = _rcopy(ins[a].at[c], outs[a].at[me, c], ssem.at[6 * a + j], rsem.at[6 * a + j], (px, py, c))
                cp.start()
                sent.append(cp)
        for a in range(n):
            for j, (px, py) in enumerate(chips):
                slot = outs[a].at[2 * px + py, c]
                _rcopy(slot, slot, ssem.at[6 * a + j], rsem.at[6 * a + j], (px, py, c)).wait_recv()
                cp = _rcopy(slot, slot, ssem.at[6 * a + 3 + j], rsem.at[6 * a + 3 + j], sib)
                cp.start()
                sent.append(cp)
        for a in range(n):
            for j, (px, py) in enumerate(chips):
                slot = outs[a].at[2 * px + py, 1 - c]
                _rcopy(slot, slot, ssem.at[6 * a + 3 + j], rsem.at[6 * a + 3 + j], sib).wait_recv()
        for cp in sent:
            cp.wait_send()
        for cp in local:
            cp.wait()

    return pl.pallas_call(
        body, name="comm_gather_shards", in_specs=[ANY] * n, out_specs=[ANY] * n,
        out_shape=[SDS((N_CHIPS,) + a.shape, a.dtype) for a in arrs],
        scratch_shapes=[pltpu.SemaphoreType.DMA((6 * n,)), pltpu.SemaphoreType.DMA((6 * n,)), pltpu.SemaphoreType.DMA((n,))],
        compiler_params=COMM_PARAMS,
    )(*arrs)


def swap_sibling_halves(arrs):
    n = len(arrs)

    def body(*refs):
        ins, outs = refs[:n], refs[n:2 * n]
        ssem, rsem = refs[2 * n:]
        x, y, c, _ = _place()
        cps = [_rcopy(ins[a].at[1 - c], outs[a], ssem.at[a], rsem.at[a], (x, y, 1 - c)) for a in range(n)]
        for cp in cps:
            cp.start()
        for cp in cps:
            cp.wait()

    return pl.pallas_call(
        body, name="comm_swap_sibling", in_specs=[ANY] * n, out_specs=[ANY] * n,
        out_shape=[SDS(a.shape[1:], a.dtype) for a in arrs],
        scratch_shapes=[pltpu.SemaphoreType.DMA((n,)), pltpu.SemaphoreType.DMA((n,))], compiler_params=COMM_PARAMS,
    )(*arrs)


def scatter_to_chips(arrs):
    n = len(arrs)

    def body(*refs):
        ins, outs = refs[:n], refs[n:2 * n]
        ssem, rsem, lsem = refs[2 * n:]
        x, y, c, chips = _place()
        me = 2 * x + y
        local = [pltpu.make_async_copy(ins[a].at[me], outs[a].at[3], lsem.at[a]) for a in range(n)]
        cps = [_rcopy(ins[a].at[2 * px + py], outs[a].at[j], ssem.at[3 * a + j], rsem.at[3 * a + j], (px, py, c))
               for a in range(n) for j, (px, py) in enumerate(chips)]
        for cp in local + cps:
            cp.start()
        for cp in cps:
            cp.wait()
        for cp in local:
            cp.wait()

    return pl.pallas_call(
        body, name="comm_scatter_chips", in_specs=[ANY] * n, out_specs=[ANY] * n,
        out_shape=[SDS(a.shape, a.dtype) for a in arrs],
        scratch_shapes=[pltpu.SemaphoreType.DMA((3 * n,)), pltpu.SemaphoreType.DMA((3 * n,)), pltpu.SemaphoreType.DMA((n,))],
        compiler_params=COMM_PARAMS,
    )(*arrs)


def join_sibling_halves(arrs):
    n = len(arrs)

    def body(*refs):
        ins, outs = refs[:n], refs[n:2 * n]
        ssem, rsem, lsem = refs[2 * n:]
        x, y, c, _ = _place()
        local = [pltpu.make_async_copy(ins[a], outs[a].at[c], lsem.at[a]) for a in range(n)]
        cps = [_rcopy(ins[a], outs[a].at[c], ssem.at[a], rsem.at[a], (x, y, 1 - c)) for a in range(n)]
        for cp in local + cps:
            cp.start()
        for a, cp in enumerate(cps):
            cp.wait_send()
            _rcopy(ins[a], outs[a].at[1 - c], ssem.at[a], rsem.at[a], (x, y, 1 - c)).wait_recv()
        for cp in local:
            cp.wait()

    return pl.pallas_call(
        body, name="comm_join_sibling", in_specs=[ANY] * n, out_specs=[ANY] * n,
        out_shape=[SDS((2,) + a.shape, a.dtype) for a in arrs],
        scratch_shapes=[pltpu.SemaphoreType.DMA((n,)), pltpu.SemaphoreType.DMA((n,)), pltpu.SemaphoreType.DMA((n,))],
        compiler_params=COMM_PARAMS,
    )(*arrs)


def gather_all(buf):
    def body(in_ref, out_ref, ssem, rsem, lsem):
        x, y, c, _ = _place()
        me = 4 * x + 2 * y + c
        local = pltpu.make_async_copy(in_ref, out_ref.at[me], lsem)
        local.start()
        flips = [(a, b, e) for a in (0, 1) for b in (0, 1) for e in (0, 1)][1:]
        cps = []
        for i, (a, b, e) in enumerate(flips):
            peer = (x ^ a, y ^ b, c ^ e)
            cps.append(_rcopy(in_ref, out_ref.at[me], ssem.at[i], rsem.at[i], peer))
        for cp in cps:
            cp.start()
        for i, (a, b, e) in enumerate(flips):
            cps[i].wait_send()
            slot = out_ref.at[4 * (x ^ a) + 2 * (y ^ b) + (c ^ e)]
            _rcopy(slot, slot, ssem.at[i], rsem.at[i], (x ^ a, y ^ b, c ^ e)).wait_recv()
        local.wait()

    return pl.pallas_call(
        body, name="comm_gather_all", in_specs=[ANY], out_specs=ANY, out_shape=SDS((N_DEV,) + buf.shape, buf.dtype),
        scratch_shapes=[pltpu.SemaphoreType.DMA((N_DEV - 1,)), pltpu.SemaphoreType.DMA((N_DEV - 1,)), pltpu.SemaphoreType.DMA],
        compiler_params=COMM_PARAMS,
    )(buf)


def _pack(parts):
    flat = jnp.concatenate([p.reshape(-1) for p in parts])
    pad = (-flat.size) % (SUBLANES * LANES)
    return jnp.pad(flat, (0, pad)).reshape(-1, LANES)


def _unpack(buf, shapes):
    flat, out, off = buf.reshape(-1), [], 0
    for s in shapes:
        n = 1
        for d in s:
            n *= d
        out.append(flat[off:off + n].reshape(s))
        off += n
    return out


def _pad_lanes(v):
    v = v.reshape(1, -1)
    return jnp.pad(v, ((0, 0), (0, LANES - v.shape[1])))


def _as2d(a):
    return a.reshape(1, -1) if a.ndim == 1 else a.reshape(-1, a.shape[-1])


def kernel(x, even_norm_g, even_w_in, gmlp_ln_g, gmlp_ln_b, gmlp_ws, gmlp_bs, ssd_conv_w, ssd_conv_b, ssd_dt_bias, ssd_a_log, ssd_d, ssd_norm_g, even_w_out, odd_norm_g, odd_w_in, sconv_w, odd_w_out, final_norm_g, loss_target, m_even_norm_g, m_even_w_in, m_gmlp_ln_g, m_gmlp_ln_b, m_gmlp_ws, m_gmlp_bs, m_ssd_conv_w, m_ssd_conv_b, m_ssd_dt_bias, m_ssd_a_log, m_ssd_d, m_ssd_norm_g, m_even_w_out, m_odd_norm_g, m_odd_w_in, m_sconv_w, m_odd_w_out, m_final_norm_g, v_even_norm_g, v_even_w_in, v_gmlp_ln_g, v_gmlp_ln_b, v_gmlp_ws, v_gmlp_bs, v_ssd_conv_w, v_ssd_conv_b, v_ssd_dt_bias, v_ssd_a_log, v_ssd_d, v_ssd_norm_g, v_even_w_out, v_odd_norm_g, v_odd_w_in, v_sconv_w, v_odd_w_out, v_final_norm_g):
    weights = dict(even_norm_g=even_norm_g, even_w_in=even_w_in, gmlp_ln_g=gmlp_ln_g, gmlp_ln_b=gmlp_ln_b, gmlp_ws=gmlp_ws, gmlp_bs=gmlp_bs, ssd_conv_w=ssd_conv_w, ssd_conv_b=ssd_conv_b, ssd_dt_bias=ssd_dt_bias, ssd_a_log=ssd_a_log, ssd_d=ssd_d, ssd_norm_g=ssd_norm_g, even_w_out=even_w_out, odd_norm_g=odd_norm_g, odd_w_in=odd_w_in, sconv_w=sconv_w, odd_w_out=odd_w_out, final_norm_g=final_norm_g)
    moms_m = dict(even_norm_g=m_even_norm_g, even_w_in=m_even_w_in, gmlp_ln_g=m_gmlp_ln_g, gmlp_ln_b=m_gmlp_ln_b, gmlp_ws=m_gmlp_ws, gmlp_bs=m_gmlp_bs, ssd_conv_w=m_ssd_conv_w, ssd_conv_b=m_ssd_conv_b, ssd_dt_bias=m_ssd_dt_bias, ssd_a_log=m_ssd_a_log, ssd_d=m_ssd_d, ssd_norm_g=m_ssd_norm_g, even_w_out=m_even_w_out, odd_norm_g=m_odd_norm_g, odd_w_in=m_odd_w_in, sconv_w=m_sconv_w, odd_w_out=m_odd_w_out, final_norm_g=m_final_norm_g)
    moms_v = dict(even_norm_g=v_even_norm_g, even_w_in=v_even_w_in, gmlp_ln_g=v_gmlp_ln_g, gmlp_ln_b=v_gmlp_ln_b, gmlp_ws=v_gmlp_ws, gmlp_bs=v_gmlp_bs, ssd_conv_w=v_ssd_conv_w, ssd_conv_b=v_ssd_conv_b, ssd_dt_bias=v_ssd_dt_bias, ssd_a_log=v_ssd_a_log, ssd_d=v_ssd_d, ssd_norm_g=v_ssd_norm_g, even_w_out=v_even_w_out, odd_norm_g=v_odd_norm_g, odd_w_in=v_odd_w_in, sconv_w=v_sconv_w, odd_w_out=v_odd_w_out, final_norm_g=v_final_norm_g)
    names = list(weights)

    xs = x[0]
    tgt = loss_target[0]
    T = xs.shape[0]
    chip = 2 * lax.axis_index("x") + lax.axis_index("y")
    core = lax.axis_index("c")
    cshard = B_XBC // N_CHIPS
    dshard = D_MODEL // N_CHIPS

    def halves(w):
        return w.astype(BF16).reshape(2, w.shape[0] // 2, w.shape[1])

    small_shard = jnp.concatenate([ssd_conv_w[0].reshape(-1), odd_norm_g[0], sconv_w[0].reshape(-1)])
    g_wie, g_woe, g_wio, g_woo, g_small = gather_shards(
        [halves(even_w_in[0]), halves(even_w_out[0]), halves(odd_w_in[0]), halves(odd_w_out[0]), small_shard.reshape(2, -1, LANES)])
    wie = g_wie.reshape(N_CHIPS, D_MODEL, IN_EVEN // N_CHIPS).transpose(1, 0, 2).reshape(D_MODEL, IN_EVEN)
    wie_main = wie[:, :EVEN_MAIN]
    wie_dt = jnp.pad(wie[:, EVEN_MAIN:], ((0, 0), (0, LANES - B_HEADS)))
    woe = g_woe.reshape(2 * A_WIDTH, D_MODEL)
    wio = g_wio.reshape(N_CHIPS, D_MODEL, IN_ODD // N_CHIPS).transpose(1, 0, 2).reshape(D_MODEL, IN_ODD)
    woo = g_woo.reshape(2 * C_WIDTH, D_MODEL)
    g_small = g_small.reshape(N_CHIPS, -1)
    n_cw = B_CONV * cshard
    conv_w = g_small[:, :n_cw].reshape(N_CHIPS, B_CONV, cshard).transpose(1, 0, 2).reshape(B_CONV, B_XBC)
    odd_g = g_small[:, n_cw:n_cw + dshard].reshape(1, D_MODEL)
    sconv = g_small[:, n_cw + dshard:].reshape(N_CHIPS, C_CONV, dshard).transpose(1, 0, 2).reshape(C_CONV, C_WIDTH)

    even_g = even_norm_g
    ln_g, ln_b = gmlp_ln_g, gmlp_ln_b
    ws, bs_t = gmlp_ws[0], gmlp_bs[0].T
    conv_b = ssd_conv_b
    dt_bias, a_log, d_skip = _pad_lanes(ssd_dt_bias), _pad_lanes(ssd_a_log), _pad_lanes(ssd_d)
    norm_g = ssd_norm_g
    fin_g = final_norm_g.reshape(1, D_MODEL)

    xn0 = rmsnorm_fwd(xs, even_g, name="even_norm")
    h0 = matmul(xn0, wie_main, "nn", name="even_in", tk=D_MODEL, tn=512)
    dtr = matmul(xn0, wie_dt, "nn", name="even_in_dt", tk=D_MODEL)
    ya = gmlp_fwd(h0, ln_g, ln_b, ws, bs_t)
    xbc = ssd_conv_fwd(h0, conv_w, conv_b)
    yb, states = ssd_fwd(xbc, dtr, h0, dt_bias, a_log, d_skip, norm_g)
    y0 = jnp.concatenate([ya, yb], axis=1)
    x1 = matmul(y0, woe, "nn", name="even_out", res=xs)

    xn1 = rmsnorm_fwd(x1, odd_g, name="odd_norm")
    h1 = matmul(xn1, wio, "nn", name="odd_in", tk=D_MODEL, tn=512)
    yc = sconv_fwd(h1, sconv)
    yd, att_o, att_lse = attn_fwd(h1)
    y1 = jnp.concatenate([yc, yd], axis=1)
    x2 = matmul(y1, woo, "nn", name="odd_out", res=x1)

    loss_part, dx2, dx2b, d_fin_g = loss_head(x2, fin_g, tgt)

    dy1 = matmul(dx2b, woo, "nt", name="odd_out_dy")
    d_woo = matmul(y1, dx2b, "tn", name="odd_out_dw")
    dbg, dcg, dhx, dzc, d_sconv = sconv_bwd(h1, dy1, sconv)
    dq, dzd = attn_bwd_dq(h1, att_o, att_lse, dy1)
    dk, dv = attn_bwd_dkv(h1, att_o, att_lse, dy1)
    dh1 = jnp.concatenate([dbg, dcg, dhx, dzc, dq, dk, dv, dzd], axis=1)
    dxn1 = matmul(dh1, wio, "nt", name="odd_in_dx")
    d_wio = matmul(xn1, dh1, "tn", name="odd_in_dw")
    dx1, dx1b, d_odd_g = rmsnorm_bwd(x1, odd_g, dxn1, dx2, name="odd_norm_bwd")

    dy0 = matmul(dx1b, woe, "nt", name="even_out_dy")
    d_woe = matmul(y0, dx1b, "tn", name="even_out_dw")
    du, dvv, dza, d_ln_g, d_ln_b, d_ws, d_bs_t = gmlp_bwd(h0, dy0, ln_g, ln_b, ws, bs_t)
    dxs, dbm, dcm, ddtr, dzb, d_dt_bias, d_a_log, d_d, d_norm_g = ssd_bwd(xbc, dtr, h0, states, dy0, dt_bias, a_log, d_skip, norm_g)
    dxbc, d_conv_w, d_conv_b = ssd_conv_bwd(h0, jnp.concatenate([dxs, dbm, dcm], axis=1), conv_w, conv_b)
    dh0 = jnp.concatenate([du, dvv, dza, dzb, dxbc], axis=1)
    ddtr_b = ddtr.astype(BF16)
    dxn0 = matmul(ddtr_b, wie_dt, "nt", name="even_in_dx_dt")
    dxn0 = matmul(dh0, wie_main, "nt", name="even_in_dx", res=dxn0)
    d_wie_main = matmul(xn0, dh0, "tn", name="even_in_dw")
    d_wie_dt = matmul(xn0, ddtr_b, "tn", name="even_in_dw_dt")
    grad_x, _, d_even_g = rmsnorm_bwd(xs, even_g, dxn0, dx1, name="even_norm_bwd")
    d_wie = jnp.concatenate([d_wie_main, d_wie_dt[:, :B_HEADS]], axis=1)

    def by_cols(g):
        r, n = g.shape
        return g.reshape(2, r // 2, N_CHIPS, n // N_CHIPS).transpose(0, 2, 1, 3)

    def by_rows(g):
        r, n = g.shape
        return g.reshape(N_CHIPS, 2, r // (2 * N_CHIPS), n).transpose(1, 0, 2, 3)

    pieces = [by_cols(d_wie), by_rows(d_woe), by_cols(d_wio), by_rows(d_woo)]
    from_sib = swap_sibling_halves(pieces)
    mine = [lax.dynamic_index_in_dim(p, core, axis=0, keepdims=False) for p in pieces]
    chip_sums = [add2(a, b, name=f"chip_sum_{i}") for i, (a, b) in enumerate(zip(mine, from_sib))]
    landed = scatter_to_chips(chip_sums)
    totals = [sum_leading(a, name=f"total_{i}") for i, a in enumerate(landed)]
    joined = join_sibling_halves(totals)
    big = dict(even_w_in=joined[0].reshape(even_w_in.shape), even_w_out=joined[1].reshape(even_w_out.shape),
               odd_w_in=joined[2].reshape(odd_w_in.shape), odd_w_out=joined[3].reshape(odd_w_out.shape))

    small_names = ["even_norm_g", "gmlp_ln_g", "gmlp_ln_b", "gmlp_ws", "gmlp_bs", "ssd_conv_w", "ssd_conv_b", "ssd_dt_bias",
                   "ssd_a_log", "ssd_d", "ssd_norm_g", "odd_norm_g", "sconv_w", "final_norm_g"]
    small_parts = [d_even_g, d_ln_g, d_ln_b, d_ws, d_bs_t.T, d_conv_w, d_conv_b, d_dt_bias[:, :B_HEADS], d_a_log[:, :B_HEADS],
                   d_d[:, :B_HEADS], d_norm_g, d_odd_g, d_sconv, d_fin_g]
    small_sum = sum_leading(gather_all(_pack(small_parts)), name="small_sum")
    full = dict(zip(small_names, _unpack(small_sum, [p.shape for p in small_parts])))
    grads = dict(big)
    for nm in small_names:
        g = full[nm]
        if nm == "ssd_conv_w":
            g = lax.dynamic_slice_in_dim(g, chip * cshard, cshard, axis=1)
        elif nm in ("odd_norm_g", "sconv_w"):
            g = lax.dynamic_slice_in_dim(g, chip * dshard, dshard, axis=1)
        grads[nm] = g.reshape(weights[nm].shape)

    deltas, new_m, new_v = {}, {}, {}
    for nm in names:
        w = weights[nm]
        d, nm_, nv_ = adamw(_as2d(w), _as2d(grads[nm]), _as2d(moms_m[nm]), _as2d(moms_v[nm]), name=f"adamw_{nm}")
        deltas[nm], new_m[nm], new_v[nm] = d.reshape(w.shape), nm_.reshape(w.shape), nv_.reshape(w.shape)

    loss = lax.psum(loss_part[0, 0], ("x", "y", "c"))
    return (loss, grad_x[None], *[grads[n] for n in names], *[deltas[n] for n in names],
            *[new_m[n] for n in names], *[new_v[n] for n in names])
```

```python
import functools

import jax
import jax.numpy as jnp
from jax import lax
from jax.experimental import pallas as pl
from jax.experimental.pallas import tpu as pltpu

F32 = jnp.float32
BF16 = jnp.bfloat16
SDS = jax.ShapeDtypeStruct
MESH = pl.DeviceIdType.MESH

D_MODEL = 2048
A_WIDTH = 2048
A_GROUPS = 8
CHUNK = 128
B_WIDTH = 2048
B_HEADS = 32
B_HEAD_DIM = 64
B_GROUPS = 8
B_STATE = 128
B_CONV = 4
B_XBC = B_WIDTH + 2 * B_GROUPS * B_STATE
C_WIDTH = 2048
C_CONV = 3
D_HEADS = 16
D_HEAD_DIM = 128
D_PATTERNS = ((128, 1), (512, 4), (2048, 16))
ATT_BLOCK = 128
ATT_SUPER = 2048
EVEN_MAIN = 3 * A_WIDTH + B_WIDTH + B_XBC
IN_EVEN = EVEN_MAIN + B_HEADS
IN_ODD = 4 * C_WIDTH + 4 * D_HEADS * D_HEAD_DIM
LANES = 128
SUBLANES = 8
EPS = 1e-5
ADAM_LR = 0.001
ADAM_B1 = 0.9
ADAM_B2 = 0.999
ADAM_EPS = 1e-08
ADAM_WD = 0.01
ADAM_STEP = 10
N_CHIPS = 4
N_DEV = 8
VMEM_LIMIT_BYTES = 56 * 1024 * 1024


def _cp(*sem):
    return pltpu.CompilerParams(dimension_semantics=sem, vmem_limit_bytes=VMEM_LIMIT_BYTES)


def _full(shape):
    return pl.BlockSpec(shape, lambda *_: (0,) * len(shape))


def _silu(x):
    return x * jax.nn.sigmoid(x)


def _dot_nn(a, b):
    return lax.dot_general(a, b, (((1,), (0,)), ((), ())), preferred_element_type=F32)


def _dot_nt(a, b):
    return lax.dot_general(a, b, (((1,), (1,)), ((), ())), preferred_element_type=F32)


def _dot_tn(a, b):
    return lax.dot_general(a, b, (((0,), (0,)), ((), ())), preferred_element_type=F32)


def _tril(n):
    return lax.broadcasted_iota(jnp.int32, (n, n), 0) >= lax.broadcasted_iota(jnp.int32, (n, n), 1)


_DOTS = {"nn": _dot_nn, "nt": _dot_nt, "tn": _dot_tn}


def matmul(a, b, mode, *, name, out_dtype=F32, res=None, tm=1024, tn=1024, tk=512):
    if mode == "tn":
        (K, M), (K2, N) = a.shape, b.shape
    elif mode == "nt":
        (M, K), (N, K2) = a.shape, b.shape
    else:
        (M, K), (K2, N) = a.shape, b.shape
    assert K == K2, (a.shape, b.shape, mode)
    tm, tn, tk = min(tm, M), min(tn, N), min(tk, K)
    assert M % tm == 0 and N % tn == 0 and K % tk == 0, (M, N, K, tm, tn, tk)
    nk = K // tk
    dot = _DOTS[mode]

    def body(*refs):
        if res is None:
            a_ref, b_ref, o_ref, acc_ref = refs
        else:
            a_ref, b_ref, r_ref, o_ref, acc_ref = refs
        k = pl.program_id(2)

        @pl.when(k == 0)
        def _():
            acc_ref[...] = jnp.zeros_like(acc_ref)

        acc_ref[...] += dot(a_ref[...], b_ref[...])

        @pl.when(k == nk - 1)
        def _():
            acc = acc_ref[...]
            if res is not None:
                acc = acc + r_ref[...]
            o_ref[...] = acc.astype(o_ref.dtype)

    a_spec = pl.BlockSpec((tk, tm), lambda i, j, k: (k, i)) if mode == "tn" else pl.BlockSpec((tm, tk), lambda i, j, k: (i, k))
    b_spec = pl.BlockSpec((tn, tk), lambda i, j, k: (j, k)) if mode == "nt" else pl.BlockSpec((tk, tn), lambda i, j, k: (k, j))
    o_spec = pl.BlockSpec((tm, tn), lambda i, j, k: (i, j))
    in_specs, args = [a_spec, b_spec], [a, b]
    if res is not None:
        in_specs.append(o_spec)
        args.append(res)
    return pl.pallas_call(
        body, name=name, grid=(M // tm, N // tn, nk), in_specs=in_specs, out_specs=o_spec,
        out_shape=SDS((M, N), out_dtype), scratch_shapes=[pltpu.VMEM((tm, tn), F32)],
        compiler_params=_cp("parallel", "parallel", "arbitrary"),
    )(*args)


ROW_TILE = 512


def _rms(x, g):
    return x * lax.rsqrt(jnp.mean(x * x, axis=-1, keepdims=True) + EPS) * g


def rmsnorm_fwd(x, g, *, name):
    T, D = x.shape

    def body(x_ref, g_ref, o_ref):
        o_ref[...] = _rms(x_ref[...], g_ref[...]).astype(BF16)

    row = pl.BlockSpec((ROW_TILE, D), lambda i: (i, 0))
    return pl.pallas_call(body, name=name, grid=(T // ROW_TILE,), in_specs=[row, _full((1, D))], out_specs=row,
                          out_shape=SDS((T, D), BF16), compiler_params=_cp("parallel"))(x, g)


def rmsnorm_bwd(x, g, dxn, dres, *, name):
    T, D = x.shape

    def body(x_ref, g_ref, dxn_ref, dres_ref, dx_ref, dxb_ref, dg_ref):
        _, vjp = jax.vjp(_rms, x_ref[...], g_ref[...])
        dx, dg = vjp(dxn_ref[...])
        dx = dx + dres_ref[...]
        dx_ref[...] = dx
        dxb_ref[...] = dx.astype(BF16)

        @pl.when(pl.program_id(0) == 0)
        def _():
            dg_ref[...] = jnp.zeros_like(dg_ref)

        dg_ref[...] += dg

    row = pl.BlockSpec((ROW_TILE, D), lambda i: (i, 0))
    return pl.pallas_call(
        body, name=name, grid=(T // ROW_TILE,), in_specs=[row, _full((1, D)), row, row],
        out_specs=[row, row, _full((1, D))], out_shape=[SDS((T, D), F32), SDS((T, D), BF16), SDS((1, D), F32)],
        compiler_params=_cp("arbitrary"))(x, g, dxn, dres)


def _loss_tile(x, g, tgt):
    err = jnp.square(_rms(x, g) - tgt)
    return 0.5 * jnp.sum(jnp.mean(err, axis=-1))


def loss_head(x, g, tgt):
    T, D = x.shape

    def body(x_ref, g_ref, t_ref, loss_ref, dx_ref, dxb_ref, dg_ref):
        loss, vjp = jax.vjp(_loss_tile, x_ref[...], g_ref[...], t_ref[...])
        dx, dg, _ = vjp(jnp.ones((), F32))
        dx_ref[...] = dx
        dxb_ref[...] = dx.astype(BF16)

        @pl.when(pl.program_id(0) == 0)
        def _():
            dg_ref[...] = jnp.zeros_like(dg_ref)
            loss_ref[...] = jnp.zeros_like(loss_ref)

        dg_ref[...] += dg
        loss_ref[...] += jnp.reshape(loss, (1, 1))

    row = pl.BlockSpec((ROW_TILE, D), lambda i: (i, 0))
    return pl.pallas_call(
        body, name="loss_head", grid=(T // ROW_TILE,), in_specs=[row, _full((1, D)), row],
        out_specs=[_full((1, 1)), row, row, _full((1, D))],
        out_shape=[SDS((1, 1), F32), SDS((T, D), F32), SDS((T, D), BF16), SDS((1, D), F32)],
        compiler_params=_cp("arbitrary"))(x, g, tgt)


TILE_BYTES = 1 << 20


def _row_tile(rows, row_bytes):
    for cand in (512, 256, 128, 64, 32, 16, 8):
        if rows % cand == 0 and cand * row_bytes <= TILE_BYTES:
            return cand
    return rows


def adamw(w, g, m, v, *, name):
    R, C = w.shape
    tr = _row_tile(R, C * 4)

    def body(w_ref, g_ref, m_ref, v_ref, d_ref, nm_ref, nv_ref):
        gg = g_ref[...]
        mm = ADAM_B1 * m_ref[...] + (1.0 - ADAM_B1) * gg
        vv = ADAM_B2 * v_ref[...] + (1.0 - ADAM_B2) * jnp.square(gg)
        m_hat = mm / (1.0 - ADAM_B1 ** ADAM_STEP)
        v_hat = vv / (1.0 - ADAM_B2 ** ADAM_STEP)
        d_ref[...] = -ADAM_LR * (m_hat / (jnp.sqrt(v_hat) + ADAM_EPS) + ADAM_WD * w_ref[...])
        nm_ref[...] = mm
        nv_ref[...] = vv

    blk = pl.BlockSpec((tr, C), lambda i: (i, 0))
    return pl.pallas_call(body, name=name, grid=(R // tr,), in_specs=[blk] * 4, out_specs=[blk] * 3,
                          out_shape=[SDS((R, C), F32)] * 3, compiler_params=_cp("parallel"))(w, g, m, v)


def sum_leading(a, *, name):
    n, R, C = a.shape
    tr = _row_tile(R, n * C * 4)

    def body(a_ref, o_ref):
        acc = a_ref[0]
        for j in range(1, n):
            acc = acc + a_ref[j]
        o_ref[...] = acc

    return pl.pallas_call(body, name=name, grid=(R // tr,), in_specs=[pl.BlockSpec((n, tr, C), lambda i: (0, i, 0))],
                          out_specs=pl.BlockSpec((tr, C), lambda i: (i, 0)), out_shape=SDS((R, C), F32),
                          compiler_params=_cp("parallel"))(a)


def chip_sum(pieces, from_sibling, core, *, name):
    _, n, R, C = pieces.shape

    def body(c_ref, a_ref, b_ref, o_ref):
        o_ref[...] = (a_ref[0] + b_ref[...]).astype(BF16)

    tr = _row_tile(R, C * 4)
    blk = pl.BlockSpec((1, tr, C), lambda k, i, c_ref: (k, i, 0))
    mine = pl.BlockSpec((1, 1, tr, C), lambda k, i, c_ref: (c_ref[0], k, i, 0))
    return pl.pallas_call(
        body, name=name, out_shape=SDS((n, R, C), BF16),
        grid_spec=pltpu.PrefetchScalarGridSpec(num_scalar_prefetch=1, grid=(n, R // tr), in_specs=[mine, blk], out_specs=blk),
        compiler_params=_cp("parallel", "parallel"))(core.reshape(1), pieces, from_sibling)


def total_sum(sums, landed, chip, *, name):
    n, R, C = landed.shape

    def body(k_ref, s_ref, l_ref, o_ref):
        acc = s_ref[0].astype(F32)
        for j in range(n):
            acc = acc + l_ref[j].astype(F32)
        o_ref[...] = acc

    tr = _row_tile(R, n * C * 2)
    return pl.pallas_call(
        body, name=name, out_shape=SDS((R, C), F32),
        grid_spec=pltpu.PrefetchScalarGridSpec(
            num_scalar_prefetch=1, grid=(R // tr,),
            in_specs=[pl.BlockSpec((1, tr, C), lambda i, k_ref: (k_ref[0], i, 0)), pl.BlockSpec((n, tr, C), lambda i, k_ref: (0, i, 0))],
            out_specs=pl.BlockSpec((tr, C), lambda i, k_ref: (i, 0))),
        compiler_params=_cp("parallel"))(chip.reshape(1), sums, landed)


def _gmlp_chunk(u, v, z, ln_g, ln_b, wsc, bs_t):
    mu = jnp.mean(v, axis=-1, keepdims=True)
    xc = v - mu
    vn = xc * lax.rsqrt(jnp.mean(xc * xc, axis=-1, keepdims=True) + EPS) * ln_g + ln_b
    gw = A_WIDTH // A_GROUPS
    outs = []
    for g in range(A_GROUPS):
        m = _dot_nn(wsc[g].astype(BF16), vn[:, g * gw:(g + 1) * gw].astype(BF16))
        outs.append(m + bs_t[:, g:g + 1])
    return _silu(z) * (u * jnp.concatenate(outs, axis=1))


def _h_cols(width, idx, rows=CHUNK):
    return pl.BlockSpec((rows, width), lambda i: (i, idx))


def gmlp_fwd(h, ln_g, ln_b, ws, bs_t):
    T = h.shape[0]

    def body(u_ref, v_ref, z_ref, g_ref, b_ref, ws_ref, bs_ref, o_ref):
        wsc = jnp.where(_tril(CHUNK)[None], ws_ref[...], 0.0)
        o_ref[...] = _gmlp_chunk(u_ref[...], v_ref[...], z_ref[...], g_ref[...], b_ref[...], wsc, bs_ref[...]).astype(BF16)

    return pl.pallas_call(
        body, name="gmlp_fwd", grid=(T // CHUNK,),
        in_specs=[_h_cols(A_WIDTH, 0), _h_cols(A_WIDTH, 1), _h_cols(A_WIDTH, 2), _full((1, A_WIDTH)), _full((1, A_WIDTH)),
                  _full((A_GROUPS, CHUNK, CHUNK)), _full((CHUNK, A_GROUPS))],
        out_specs=_h_cols(A_WIDTH, 0), out_shape=SDS((T, A_WIDTH), BF16), compiler_params=_cp("parallel"),
    )(h, h, h, ln_g, ln_b, ws, bs_t)


def gmlp_bwd(h, dy, ln_g, ln_b, ws, bs_t):
    T = h.shape[0]

    def body(u_ref, v_ref, z_ref, dy_ref, g_ref, b_ref, ws_ref, bs_ref, du_ref, dv_ref, dz_ref, dg_ref, db_ref, dws_ref, dbs_ref):
        tri = _tril(CHUNK)[None]
        wsc = jnp.where(tri, ws_ref[...], 0.0)
        _, vjp = jax.vjp(_gmlp_chunk, u_ref[...], v_ref[...], z_ref[...], g_ref[...], b_ref[...], wsc, bs_ref[...])
        du, dv, dz, dg, db, dws, dbs = vjp(dy_ref[...])
        du_ref[...] = du.astype(BF16)
        dv_ref[...] = dv.astype(BF16)
        dz_ref[...] = dz.astype(BF16)

        @pl.when(pl.program_id(0) == 0)
        def _():
            dg_ref[...] = jnp.zeros_like(dg_ref)
            db_ref[...] = jnp.zeros_like(db_ref)
            dws_ref[...] = jnp.zeros_like(dws_ref)
            dbs_ref[...] = jnp.zeros_like(dbs_ref)

        dg_ref[...] += dg
        db_ref[...] += db
        dws_ref[...] += jnp.where(tri, dws, 0.0)
        dbs_ref[...] += dbs

    row = _h_cols(A_WIDTH, 0)
    pshapes = [(1, A_WIDTH), (1, A_WIDTH), (A_GROUPS, CHUNK, CHUNK), (CHUNK, A_GROUPS)]
    return pl.pallas_call(
        body, name="gmlp_bwd", grid=(T // CHUNK,),
        in_specs=[_h_cols(A_WIDTH, 0), _h_cols(A_WIDTH, 1), _h_cols(A_WIDTH, 2), _h_cols(A_WIDTH, 0)] + [_full(s) for s in pshapes],
        out_specs=[row, row, row] + [_full(s) for s in pshapes],
        out_shape=[SDS((T, A_WIDTH), BF16)] * 3 + [SDS(s, F32) for s in pshapes],
        compiler_params=_cp("arbitrary"),
    )(h, h, h, dy, ln_g, ln_b, ws, bs_t)


CONV_ROWS = 256
CONV_COLS = 512


def _taps(xe, w, rows):
    K = w.shape[0]
    acc = None
    for k in range(K):
        off = SUBLANES - (K - 1) + k
        term = w[k:k + 1, :] * xe[off:off + rows, :]
        acc = term if acc is None else acc + term
    return acc


def _ssd_conv_tile(x, halo, w, b):
    return _silu(_taps(jnp.concatenate([halo, x], axis=0), w, x.shape[0]) + b)


def _halo_spec(cols, col_idx, nt=None):
    rpb = CONV_ROWS // SUBLANES
    if nt is None:
        return pl.BlockSpec((SUBLANES, cols), lambda c, i: (jnp.maximum(i * rpb - 1, 0), col_idx(c)))
    return pl.BlockSpec((SUBLANES, cols), lambda c, j: (jnp.maximum((nt - 1 - j) * rpb - 1, 0), col_idx(c)))


def ssd_conv_fwd(h, w, b):
    T = h.shape[0]
    nc = B_XBC // CONV_COLS
    base = (3 * A_WIDTH + B_WIDTH) // CONV_COLS

    def body(x_ref, halo_ref, w_ref, b_ref, o_ref):
        halo = jnp.where(pl.program_id(1) > 0, halo_ref[...], 0.0)
        o_ref[...] = _ssd_conv_tile(x_ref[...], halo, w_ref[...], b_ref[...])

    return pl.pallas_call(
        body, name="ssd_conv_fwd", grid=(nc, T // CONV_ROWS),
        in_specs=[pl.BlockSpec((CONV_ROWS, CONV_COLS), lambda c, i: (i, base + c)), _halo_spec(CONV_COLS, lambda c: base + c),
                  pl.BlockSpec((B_CONV, CONV_COLS), lambda c, i: (0, c)), pl.BlockSpec((1, CONV_COLS), lambda c, i: (0, c))],
        out_specs=pl.BlockSpec((CONV_ROWS, CONV_COLS), lambda c, i: (i, c)),
        out_shape=SDS((T, B_XBC), F32), compiler_params=_cp("parallel", "parallel"),
    )(h, h, w, b)


def ssd_conv_bwd(h, dy, w, b):
    T = h.shape[0]
    nc = B_XBC // CONV_COLS
    nt = T // CONV_ROWS
    base = (3 * A_WIDTH + B_WIDTH) // CONV_COLS

    def body(x_ref, halo_ref, dy_ref, w_ref, b_ref, dx_ref, dw_ref, db_ref, carry_ref):
        j = pl.program_id(1)
        halo = jnp.where(j < nt - 1, halo_ref[...], 0.0)
        _, vjp = jax.vjp(_ssd_conv_tile, x_ref[...], halo, w_ref[...], b_ref[...])
        dx, dhalo, dw, db = vjp(dy_ref[...])

        @pl.when(j == 0)
        def _():
            carry_ref[...] = jnp.zeros_like(carry_ref)
            dw_ref[...] = jnp.zeros_like(dw_ref)
            db_ref[...] = jnp.zeros_like(db_ref)

        tail = dx[CONV_ROWS - SUBLANES:, :] + carry_ref[...]
        dx_ref[...] = jnp.concatenate([dx[:CONV_ROWS - SUBLANES, :], tail], axis=0).astype(BF16)
        carry_ref[...] = dhalo
        dw_ref[...] += dw
        db_ref[...] += db

    return pl.pallas_call(
        body, name="ssd_conv_bwd", grid=(nc, nt),
        in_specs=[pl.BlockSpec((CONV_ROWS, CONV_COLS), lambda c, j: (nt - 1 - j, base + c)),
                  _halo_spec(CONV_COLS, lambda c: base + c, nt),
                  pl.BlockSpec((CONV_ROWS, CONV_COLS), lambda c, j: (nt - 1 - j, c)),
                  pl.BlockSpec((B_CONV, CONV_COLS), lambda c, j: (0, c)), pl.BlockSpec((1, CONV_COLS), lambda c, j: (0, c))],
        out_specs=[pl.BlockSpec((CONV_ROWS, CONV_COLS), lambda c, j: (nt - 1 - j, c)),
                   pl.BlockSpec((B_CONV, CONV_COLS), lambda c, j: (0, c)), pl.BlockSpec((1, CONV_COLS), lambda c, j: (0, c))],
        out_shape=[SDS((T, B_XBC), BF16), SDS((B_CONV, B_XBC), F32), SDS((1, B_XBC), F32)],
        scratch_shapes=[pltpu.VMEM((SUBLANES, CONV_COLS), F32)],
        compiler_params=_cp("parallel", "arbitrary"),
    )(h, h, dy, w, b)


def _sconv_tile(bg, cg, hx, z, cg_halo, hx_halo, w):
    ch = jnp.concatenate([cg_halo * hx_halo, cg * hx], axis=0)
    return _silu(z) * (bg * _taps(ch, w, bg.shape[0]))


def sconv_fwd(h, w):
    T = h.shape[0]
    nc = C_WIDTH // CONV_COLS

    def col(seg):
        return pl.BlockSpec((CONV_ROWS, CONV_COLS), lambda c, i: (i, seg * nc + c))

    def body(bg_ref, cg_ref, hx_ref, z_ref, cgh_ref, hxh_ref, w_ref, o_ref):
        first = pl.program_id(1) == 0
        cgh = jnp.where(first, 0.0, cgh_ref[...])
        hxh = jnp.where(first, 0.0, hxh_ref[...])
        o_ref[...] = _sconv_tile(bg_ref[...], cg_ref[...], hx_ref[...], z_ref[...], cgh, hxh, w_ref[...]).astype(BF16)

    return pl.pallas_call(
        body, name="sconv_fwd", grid=(nc, T // CONV_ROWS),
        in_specs=[col(0), col(1), col(2), col(3), _halo_spec(CONV_COLS, lambda c: nc + c), _halo_spec(CONV_COLS, lambda c: 2 * nc + c),
                  pl.BlockSpec((C_CONV, CONV_COLS), lambda c, i: (0, c))],
        out_specs=pl.BlockSpec((CONV_ROWS, CONV_COLS), lambda c, i: (i, c)),
        out_shape=SDS((T, C_WIDTH), BF16), compiler_params=_cp("parallel", "parallel"),
    )(h, h, h, h, h, h, w)


def sconv_bwd(h, dy, w):
    T = h.shape[0]
    nc = C_WIDTH // CONV_COLS
    nt = T // CONV_ROWS

    def col(seg):
        return pl.BlockSpec((CONV_ROWS, CONV_COLS), lambda c, j: (nt - 1 - j, seg * nc + c))

    def body(bg_ref, cg_ref, hx_ref, z_ref, cgh_ref, hxh_ref, dy_ref, w_ref, dbg_ref, dcg_ref, dhx_ref, dz_ref, dw_ref, ccg_ref, chx_ref):
        j = pl.program_id(1)
        first = j == nt - 1
        cgh = jnp.where(first, 0.0, cgh_ref[...])
        hxh = jnp.where(first, 0.0, hxh_ref[...])
        _, vjp = jax.vjp(_sconv_tile, bg_ref[...], cg_ref[...], hx_ref[...], z_ref[...], cgh, hxh, w_ref[...])
        dbg, dcg, dhx, dz, dcgh, dhxh, dw = vjp(dy_ref[...])

        @pl.when(j == 0)
        def _():
            ccg_ref[...] = jnp.zeros_like(ccg_ref)
            chx_ref[...] = jnp.zeros_like(chx_ref)
            dw_ref[...] = jnp.zeros_like(dw_ref)

        def with_carry(d, carry_ref):
            tail = d[CONV_ROWS - SUBLANES:, :] + carry_ref[...]
            return jnp.concatenate([d[:CONV_ROWS - SUBLANES, :], tail], axis=0).astype(BF16)

        dbg_ref[...] = dbg.astype(BF16)
        dz_ref[...] = dz.astype(BF16)
        dcg_ref[...] = with_carry(dcg, ccg_ref)
        dhx_ref[...] = with_carry(dhx, chx_ref)
        ccg_ref[...] = dcgh
        chx_ref[...] = dhxh
        dw_ref[...] += dw

    out_row = pl.BlockSpec((CONV_ROWS, CONV_COLS), lambda c, j: (nt - 1 - j, c))
    wspec = pl.BlockSpec((C_CONV, CONV_COLS), lambda c, j: (0, c))
    return pl.pallas_call(
        body, name="sconv_bwd", grid=(nc, nt),
        in_specs=[col(0), col(1), col(2), col(3), _halo_spec(CONV_COLS, lambda c: nc + c, nt), _halo_spec(CONV_COLS, lambda c: 2 * nc + c, nt),
                  out_row, wspec],
        out_specs=[out_row] * 4 + [wspec],
        out_shape=[SDS((T, C_WIDTH), BF16)] * 4 + [SDS((C_CONV, C_WIDTH), F32)],
        scratch_shapes=[pltpu.VMEM((SUBLANES, CONV_COLS), F32)] * 2,
        compiler_params=_cp("parallel", "arbitrary"),
    )(h, h, h, h, h, h, dy, w)


def _softplus(x):
    return jnp.maximum(x, 0.0) + jnp.log(1.0 + jnp.exp(-jnp.abs(x)))


def _ssd_chunk(xs, bm, cm, dtr, z, prev, dt_bias, a_log, d_skip, norm_g):
    tril = _tril(CHUNK)
    dt = _softplus(dtr + dt_bias)
    adt = dt * (-jnp.exp(a_log))
    a_cs = jnp.dot(tril.astype(F32), adt, precision=lax.Precision.HIGHEST, preferred_element_type=F32)
    a_cs_t = a_cs.T
    a_last = a_cs[CHUNK - 1:CHUNK, :]
    decay_st = jnp.exp(a_last - a_cs)
    e_cs = jnp.exp(a_cs)
    chunk_decay = jnp.exp(a_last)
    hpg = B_HEADS // B_GROUPS
    ys, nxt = [], []
    for g in range(B_GROUPS):
        bg = bm[:, g * B_STATE:(g + 1) * B_STATE].astype(BF16)
        cg = cm[:, g * B_STATE:(g + 1) * B_STATE].astype(BF16)
        cb = _dot_nt(cg, bg)
        for r in range(hpg):
            hh = g * hpg + r
            xh = xs[:, hh * B_HEAD_DIM:(hh + 1) * B_HEAD_DIM]
            xdt = xh * dt[:, hh:hh + 1]
            seg = jnp.where(tril, a_cs[:, hh:hh + 1] - a_cs_t[hh:hh + 1, :], -jnp.inf)
            y_diag = _dot_nn((cb * jnp.exp(seg)).astype(BF16), xdt.astype(BF16))
            st = _dot_tn((xdt * decay_st[:, hh:hh + 1]).astype(BF16), bg)
            ph = prev[hh]
            y_off = _dot_nt(cg, ph.astype(BF16)) * e_cs[:, hh:hh + 1]
            nxt.append(ph * chunk_decay[:, hh:hh + 1] + st)
            ys.append(y_diag + y_off + d_skip[:, hh:hh + 1] * xh)
    y = jnp.concatenate(ys, axis=1) * _silu(z)
    gw = B_WIDTH // B_GROUPS
    outs = []
    for g in range(B_GROUPS):
        yg = y[:, g * gw:(g + 1) * gw]
        outs.append(yg * lax.rsqrt(jnp.mean(yg * yg, axis=-1, keepdims=True) + EPS))
    return jnp.concatenate(outs, axis=1) * norm_g, jnp.stack(nxt)


_SSD_PARAM_SHAPES = [(1, LANES), (1, LANES), (1, LANES), (1, B_WIDTH)]
_STATE_SHAPE = (B_HEADS, B_HEAD_DIM, B_STATE)


def ssd_fwd(xbc, dtr, h, dt_bias, a_log, d_skip, norm_g):
    T = xbc.shape[0]
    nc = T // CHUNK

    def body(xs_ref, b_ref, c_ref, dt_ref, z_ref, p0, p1, p2, p3, y_ref, st_ref, state):
        @pl.when(pl.program_id(0) == 0)
        def _():
            state[...] = jnp.zeros_like(state)

        prev = state[...]
        st_ref[0] = prev
        yb, nxt = _ssd_chunk(xs_ref[...], b_ref[...], c_ref[...], dt_ref[...], z_ref[...], prev, p0[...], p1[...], p2[...], p3[...])
        y_ref[...] = yb.astype(BF16)
        state[...] = nxt

    return pl.pallas_call(
        body, name="ssd_fwd", grid=(nc,),
        in_specs=[_h_cols(B_WIDTH, 0), _h_cols(B_GROUPS * B_STATE, 2), _h_cols(B_GROUPS * B_STATE, 3), _h_cols(LANES, 0), _h_cols(B_WIDTH, 3)]
        + [_full(s) for s in _SSD_PARAM_SHAPES],
        out_specs=[_h_cols(B_WIDTH, 0), pl.BlockSpec((1,) + _STATE_SHAPE, lambda i: (i, 0, 0, 0))],
        out_shape=[SDS((T, B_WIDTH), BF16), SDS((nc,) + _STATE_SHAPE, F32)],
        scratch_shapes=[pltpu.VMEM(_STATE_SHAPE, F32)], compiler_params=_cp("arbitrary"),
    )(xbc, xbc, xbc, dtr, h, dt_bias, a_log, d_skip, norm_g)


def ssd_bwd(xbc, dtr, h, states, dy, dt_bias, a_log, d_skip, norm_g):
    T = xbc.shape[0]
    nc = T // CHUNK

    def rev(width, idx):
        return pl.BlockSpec((CHUNK, width), lambda j: (nc - 1 - j, idx))

    def body(xs_ref, b_ref, c_ref, dt_ref, z_ref, st_ref, dy_ref, p0, p1, p2, p3,
             dxs_ref, db_ref, dc_ref, ddt_ref, dz_ref, g0, g1, g2, g3, dstate):
        @pl.when(pl.program_id(0) == 0)
        def _():
            dstate[...] = jnp.zeros_like(dstate)
            for gref in (g0, g1, g2, g3):
                gref[...] = jnp.zeros_like(gref)

        _, vjp = jax.vjp(_ssd_chunk, xs_ref[...], b_ref[...], c_ref[...], dt_ref[...], z_ref[...], st_ref[0],
                         p0[...], p1[...], p2[...], p3[...])
        dxs, dbm, dcm, ddt, dz, dprev, d0, d1, d2, d3 = vjp((dy_ref[...], dstate[...]))
        dxs_ref[...] = dxs
        db_ref[...] = dbm
        dc_ref[...] = dcm
        ddt_ref[...] = ddt
        dz_ref[...] = dz.astype(BF16)
        dstate[...] = dprev
        g0[...] += d0
        g1[...] += d1
        g2[...] += d2
        g3[...] += d3

    gn = B_GROUPS * B_STATE
    return pl.pallas_call(
        body, name="ssd_bwd", grid=(nc,),
        in_specs=[rev(B_WIDTH, 0), rev(gn, 2), rev(gn, 3), rev(LANES, 0), rev(B_WIDTH, 3),
                  pl.BlockSpec((1,) + _STATE_SHAPE, lambda j: (nc - 1 - j, 0, 0, 0)), rev(B_WIDTH, 1)]
        + [_full(s) for s in _SSD_PARAM_SHAPES],
        out_specs=[rev(B_WIDTH, 0), rev(gn, 0), rev(gn, 0), rev(LANES, 0), rev(B_WIDTH, 0)] + [_full(s) for s in _SSD_PARAM_SHAPES],
        out_shape=[SDS((T, B_WIDTH), F32), SDS((T, gn), F32), SDS((T, gn), F32), SDS((T, LANES), F32), SDS((T, B_WIDTH), BF16)]
        + [SDS(s, F32) for s in _SSD_PARAM_SHAPES],
        scratch_shapes=[pltpu.VMEM(_STATE_SHAPE, F32)], compiler_params=_cp("arbitrary"),
    )(xbc, xbc, xbc, dtr, h, states, dy, dt_bias, a_log, d_skip, norm_g)


ATT_SCALE = D_HEAD_DIM ** -0.5
Q_COL, K_COL, V_COL, Z_COL = (4 * C_WIDTH // LANES + i * D_HEADS for i in range(4))


ATT_NBLK = ATT_SUPER // ATT_BLOCK


def _res_rows(r, first, count, dil):
    return pl.ds(r + dil * first, count) if dil == 1 else pl.ds(r + dil * first, count, stride=dil)


def _blocks(ref, dil, dtype=None):
    n = ATT_SUPER // dil
    parts = []
    for r in range(dil):
        v = ref[_res_rows(r, 0, n, dil), :]
        parts.append((v if dtype is None else v.astype(dtype)).reshape(n // ATT_BLOCK, ATT_BLOCK, D_HEAD_DIM))
    return parts[0] if dil == 1 else jnp.concatenate(parts, axis=0)


def _blocks_before(cur_ref, prev_ref, dil, dtype):
    n = ATT_SUPER // dil
    parts = []
    for r in range(dil):
        v = prev_ref[_res_rows(r, n - ATT_BLOCK, ATT_BLOCK, dil), :]
        if n > ATT_BLOCK:
            v = jnp.concatenate([v, cur_ref[_res_rows(r, 0, n - ATT_BLOCK, dil), :]], axis=0)
        parts.append(v.astype(dtype).reshape(n // ATT_BLOCK, ATT_BLOCK, D_HEAD_DIM))
    return parts[0] if dil == 1 else jnp.concatenate(parts, axis=0)


def _blocks_after(cur_ref, next_ref, dil, dtype=None):
    n = ATT_SUPER // dil
    parts = []
    for r in range(dil):
        v = next_ref[_res_rows(r, 0, ATT_BLOCK, dil), :]
        if n > ATT_BLOCK:
            v = jnp.concatenate([cur_ref[_res_rows(r, ATT_BLOCK, n - ATT_BLOCK, dil), :], v], axis=0)
        parts.append((v if dtype is None else v.astype(dtype)).reshape(n // ATT_BLOCK, ATT_BLOCK, D_HEAD_DIM))
    return parts[0] if dil == 1 else jnp.concatenate(parts, axis=0)


def _unblock(ref, val, dil, add=False):
    n = ATT_SUPER // dil
    nb = n // ATT_BLOCK
    for r in range(dil):
        v = val[r * nb:(r + 1) * nb].reshape(n, D_HEAD_DIM)
        if add:
            ref[_res_rows(r, 0, n, dil), :] += v
        else:
            ref[_res_rows(r, 0, n, dil), :] = v


def _att_masks(dil, edge_ok, edge_last=False):
    shape = (ATT_NBLK, ATT_BLOCK, ATT_BLOCK)
    blk = lax.broadcasted_iota(jnp.int32, shape, 0)
    row = lax.broadcasted_iota(jnp.int32, shape, 1)
    col = lax.broadcasted_iota(jnp.int32, shape, 2)
    nb = ATT_NBLK // dil
    at_edge = (blk % nb) == (nb - 1 if edge_last else 0)
    return col <= row, jnp.logical_and(col >= row, jnp.logical_or(jnp.logical_not(at_edge), edge_ok))


def _bdot_nt(a, b):
    return lax.dot_general(a, b, (((2,), (2,)), ((0,), (0,))), preferred_element_type=F32)


def _bdot_nn(a, b):
    return lax.dot_general(a, b, (((2,), (1,)), ((0,), (0,))), preferred_element_type=F32)


def _bdot_tn(a, b):
    return lax.dot_general(a, b, (((1,), (1,)), ((0,), (0,))), preferred_element_type=F32)


def _att_spec(col0, shift=0, last=None):
    def imap(hh, n):
        m = n + shift
        if shift < 0:
            m = jnp.maximum(m, 0)
        if shift > 0:
            m = jnp.minimum(m, last)
        return (m, col0 + hh)
    return pl.BlockSpec((ATT_SUPER, D_HEAD_DIM), imap)


def _att_out_spec():
    return pl.BlockSpec((ATT_SUPER, D_HEAD_DIM), lambda hh, n: (n, hh))


def attn_fwd(h):
    T = h.shape[0]
    npat = len(D_PATTERNS)

    def body(q_ref, kc_ref, kp_ref, vc_ref, vp_ref, z_ref, yd_ref, o_ref, lse_ref, *scratch):
        o_s, l_s = scratch[:npat], scratch[npat:]
        has_prev = pl.program_id(1) > 0
        for pi, (_, dil) in enumerate(D_PATTERNS):
            mask_c, mask_p = _att_masks(dil, has_prev)
            q = _blocks(q_ref, dil, BF16)
            kc, vc = _blocks(kc_ref, dil, BF16), _blocks(vc_ref, dil, BF16)
            kp, vp = _blocks_before(kc_ref, kp_ref, dil, BF16), _blocks_before(vc_ref, vp_ref, dil, BF16)
            s_c = jnp.where(mask_c, _bdot_nt(q, kc) * ATT_SCALE, -jnp.inf)
            s_p = jnp.where(mask_p, _bdot_nt(q, kp) * ATT_SCALE, -jnp.inf)
            m = jnp.maximum(jnp.max(s_c, axis=-1, keepdims=True), jnp.max(s_p, axis=-1, keepdims=True))
            p_c = jnp.exp(s_c - m)
            p_p = jnp.exp(s_p - m)
            l = jnp.sum(p_c, axis=-1, keepdims=True) + jnp.sum(p_p, axis=-1, keepdims=True)
            o = _bdot_nn((p_c / l).astype(BF16), vc) + _bdot_nn((p_p / l).astype(BF16), vp)
            _unblock(o_s[pi], o, dil)
            _unblock(l_s[pi], jnp.broadcast_to(m + jnp.log(l), o.shape), dil)
        lses = [l_s[pi][...] for pi in range(npat)]
        mx = functools.reduce(jnp.maximum, lses)
        ws = [jnp.exp(l - mx) for l in lses]
        den = functools.reduce(lambda a, b: a + b, ws)
        o = functools.reduce(lambda a, b: a + b, [(w / den) * o_s[pi][...] for pi, w in enumerate(ws)])
        o_ref[...] = o
        lse_ref[...] = mx + jnp.log(den)
        yd_ref[...] = (_silu(z_ref[...]) * o).astype(BF16)

    n_super = T // ATT_SUPER
    return pl.pallas_call(
        body, name="attn_fwd", grid=(D_HEADS, n_super),
        in_specs=[_att_spec(Q_COL), _att_spec(K_COL), _att_spec(K_COL, -1), _att_spec(V_COL), _att_spec(V_COL, -1), _att_spec(Z_COL)],
        out_specs=[_att_out_spec()] * 3,
        out_shape=[SDS((T, D_HEADS * D_HEAD_DIM), BF16), SDS((T, D_HEADS * D_HEAD_DIM), F32), SDS((T, D_HEADS * D_HEAD_DIM), F32)],
        scratch_shapes=[pltpu.VMEM((ATT_SUPER, D_HEAD_DIM), F32)] * (2 * npat),
        compiler_params=_cp("parallel", "arbitrary"),
    )(h, h, h, h, h, h)


def _dsilu(z):
    s = jax.nn.sigmoid(z)
    return s * (1.0 + z * (1.0 - s))


def attn_bwd_dq(h, o, lse, dy):
    T = h.shape[0]
    dy_col = C_WIDTH // LANES

    def body(q_ref, kc_ref, kp_ref, vc_ref, vp_ref, z_ref, o_ref, lse_ref, dy_ref, dq_ref, dz_ref, do_s, dd_s, dq_s):
        has_prev = pl.program_id(1) > 0
        z, oo, dyd = z_ref[...], o_ref[...], dy_ref[...]
        do = dyd * _silu(z)
        dz_ref[...] = (dyd * oo * _dsilu(z)).astype(BF16)
        do_s[...] = do
        dd_s[...] = jnp.broadcast_to(jnp.sum(do * oo, axis=-1, keepdims=True), (ATT_SUPER, D_HEAD_DIM))
        for pi, (_, dil) in enumerate(D_PATTERNS):
            mask_c, mask_p = _att_masks(dil, has_prev)
            q = _blocks(q_ref, dil, BF16)
            kc, vc = _blocks(kc_ref, dil, BF16), _blocks(vc_ref, dil, BF16)
            kp, vp = _blocks_before(kc_ref, kp_ref, dil, BF16), _blocks_before(vc_ref, vp_ref, dil, BF16)
            lse_b, dd_b, do_b = _blocks(lse_ref, dil), _blocks(dd_s, dil), _blocks(do_s, dil, BF16)
            p_c = jnp.where(mask_c, jnp.exp(_bdot_nt(q, kc) * ATT_SCALE - lse_b), 0.0)
            p_p = jnp.where(mask_p, jnp.exp(_bdot_nt(q, kp) * ATT_SCALE - lse_b), 0.0)
            ds_c = p_c * (_bdot_nt(do_b, vc) - dd_b) * ATT_SCALE
            ds_p = p_p * (_bdot_nt(do_b, vp) - dd_b) * ATT_SCALE
            dq = _bdot_nn(ds_c.astype(BF16), kc) + _bdot_nn(ds_p.astype(BF16), kp)
            _unblock(dq_s, dq, dil, add=pi > 0)
        dq_ref[...] = dq_s[...].astype(BF16)

    n_super = T // ATT_SUPER
    blk = (ATT_SUPER, D_HEAD_DIM)
    return pl.pallas_call(
        body, name="attn_bwd_dq", grid=(D_HEADS, n_super),
        in_specs=[_att_spec(Q_COL), _att_spec(K_COL), _att_spec(K_COL, -1), _att_spec(V_COL), _att_spec(V_COL, -1), _att_spec(Z_COL),
                  _att_spec(0), _att_spec(0), _att_spec(dy_col)],
        out_specs=[_att_out_spec()] * 2,
        out_shape=[SDS((T, D_HEADS * D_HEAD_DIM), BF16)] * 2,
        scratch_shapes=[pltpu.VMEM(blk, F32)] * 3,
        compiler_params=_cp("parallel", "arbitrary"),
    )(h, h, h, h, h, h, o, lse, dy)


def attn_bwd_dkv(h, o, lse, dy):
    T = h.shape[0]
    n_super = T // ATT_SUPER
    last = n_super - 1
    dy_col = C_WIDTH // LANES
    npat = len(D_PATTERNS)

    def body(k_ref, v_ref, qc_ref, qn_ref, zc_ref, zn_ref, oc_ref, on_ref, lc_ref, ln_ref, dyc_ref, dyn_ref, dk_ref, dv_ref,
             do_c, do_n, dd_c, dd_n, dk_s, dv_s):
        do_s, dd_s = (do_c, do_n), (dd_c, dd_n)
        has_next = pl.program_id(1) < last
        for i, (z_ref, oo_ref, dyd_ref) in enumerate(((zc_ref, oc_ref, dyc_ref), (zn_ref, on_ref, dyn_ref))):
            do = dyd_ref[...] * _silu(z_ref[...])
            do_s[i][...] = do
            dd_s[i][...] = jnp.broadcast_to(jnp.sum(do * oo_ref[...], axis=-1, keepdims=True), (ATT_SUPER, D_HEAD_DIM))
        for pi, (_, dil) in enumerate(D_PATTERNS):
            mask_c, mask_p = _att_masks(dil, has_next, edge_last=True)
            kb, vb = _blocks(k_ref, dil, BF16), _blocks(v_ref, dil, BF16)
            dk = dv = None
            for own in (True, False):
                if own:
                    q, lse_b = _blocks(qc_ref, dil, BF16), _blocks(lc_ref, dil)
                    do_b, dd_b = _blocks(do_c, dil, BF16), _blocks(dd_c, dil)
                else:
                    q, lse_b = _blocks_after(qc_ref, qn_ref, dil, BF16), _blocks_after(lc_ref, ln_ref, dil)
                    do_b, dd_b = _blocks_after(do_c, do_n, dil, BF16), _blocks_after(dd_c, dd_n, dil)
                p = jnp.where(mask_c if own else mask_p, jnp.exp(_bdot_nt(q, kb) * ATT_SCALE - lse_b), 0.0)
                ds = p * (_bdot_nt(do_b, vb) - dd_b) * ATT_SCALE
                dv_t = _bdot_tn(p.astype(BF16), do_b)
                dk_t = _bdot_tn(ds.astype(BF16), q)
                dk = dk_t if dk is None else dk + dk_t
                dv = dv_t if dv is None else dv + dv_t
            _unblock(dk_s, dk, dil, add=pi > 0)
            _unblock(dv_s, dv, dil, add=pi > 0)
        dk_ref[...] = dk_s[...].astype(BF16)
        dv_ref[...] = dv_s[...].astype(BF16)

    blk = (ATT_SUPER, D_HEAD_DIM)

    def pair(col0):
        return [_att_spec(col0), _att_spec(col0, 1, last)]

    return pl.pallas_call(
        body, name="attn_bwd_dkv", grid=(D_HEADS, n_super),
        in_specs=[_att_spec(K_COL), _att_spec(V_COL)] + pair(Q_COL) + pair(Z_COL) + pair(0) + pair(0) + pair(dy_col),
        out_specs=[_att_out_spec()] * 2,
        out_shape=[SDS((T, D_HEADS * D_HEAD_DIM), BF16)] * 2,
        scratch_shapes=[pltpu.VMEM(blk, F32)] * 6,
        compiler_params=_cp("parallel", "arbitrary"),
    )(h, h, h, h, h, h, o, o, lse, lse, dy, dy)


ANY = pl.BlockSpec(memory_space=pl.ANY)
COMM_PARAMS = pltpu.CompilerParams()


def _place():
    x, y, c = lax.axis_index("x"), lax.axis_index("y"), lax.axis_index("c")
    return x, y, c, [(1 - x, y), (x, 1 - y), (1 - x, 1 - y)]


def _rcopy(src, dst, ssem, rsem, dev):
    return pltpu.make_async_remote_copy(src_ref=src, dst_ref=dst, send_sem=ssem, recv_sem=rsem, device_id=dev, device_id_type=MESH)


def gather_shards(arrs):
    n = len(arrs)
    per = 7

    def body(*refs):
        ins, outs = refs[:n], refs[n:2 * n]
        ssem, rsem = refs[2 * n:]
        x, y, c, chips = _place()
        me, sib = 2 * x + y, (x, y, 1 - c)
        sent = []
        for a in range(n):
            for j, (px, py) in enumerate(chips):
                cp = _rcopy(ins[a].at[c], outs[a].at[me, c], ssem.at[per * a + j], rsem.at[per * a + j], (px, py, c))
                cp.start()
                sent.append(cp)
        own = [_rcopy(ins[a], outs[a].at[me], ssem.at[per * a + 6], rsem.at[per * a + 6], sib) for a in range(n)]
        for cp in own:
            cp.start()
        for a in range(n):
            for j, (px, py) in enumerate(chips):
                slot = outs[a].at[2 * px + py, c]
                _rcopy(slot, slot, ssem.at[per * a + j], rsem.at[per * a + j], (px, py, c)).wait_recv()
                cp = _rcopy(slot, slot, ssem.at[per * a + 3 + j], rsem.at[per * a + 3 + j], sib)
                cp.start()
                sent.append(cp)
        for a in range(n):
            for j, (px, py) in enumerate(chips):
                slot = outs[a].at[2 * px + py, 1 - c]
                _rcopy(slot, slot, ssem.at[per * a + 3 + j], rsem.at[per * a + 3 + j], sib).wait_recv()
        for cp in sent:
            cp.wait_send()
        for cp in own:
            cp.wait()

    return pl.pallas_call(
        body, name="comm_gather_shards", in_specs=[ANY] * n, out_specs=[ANY] * n,
        out_shape=[SDS((N_CHIPS,) + a.shape, a.dtype) for a in arrs],
        scratch_shapes=[pltpu.SemaphoreType.DMA((per * n,)), pltpu.SemaphoreType.DMA((per * n,))],
        compiler_params=COMM_PARAMS,
    )(*arrs)


def swap_sibling_halves(arrs):
    n = len(arrs)

    def body(*refs):
        ins, outs = refs[:n], refs[n:2 * n]
        ssem, rsem = refs[2 * n:]
        x, y, c, _ = _place()
        cps = [_rcopy(ins[a].at[1 - c], outs[a], ssem.at[a], rsem.at[a], (x, y, 1 - c)) for a in range(n)]
        for cp in cps:
            cp.start()
        for cp in cps:
            cp.wait()

    return pl.pallas_call(
        body, name="comm_swap_sibling", in_specs=[ANY] * n, out_specs=[ANY] * n,
        out_shape=[SDS(a.shape[1:], a.dtype) for a in arrs],
        scratch_shapes=[pltpu.SemaphoreType.DMA((n,)), pltpu.SemaphoreType.DMA((n,))], compiler_params=COMM_PARAMS,
    )(*arrs)


def scatter_to_chips(arrs):
    n = len(arrs)

    def body(*refs):
        ins, outs = refs[:n], refs[n:2 * n]
        ssem, rsem = refs[2 * n:]
        x, y, c, chips = _place()
        cps = [_rcopy(ins[a].at[2 * px + py], outs[a].at[j], ssem.at[3 * a + j], rsem.at[3 * a + j], (px, py, c))
               for a in range(n) for j, (px, py) in enumerate(chips)]
        for cp in cps:
            cp.start()
        for cp in cps:
            cp.wait()

    return pl.pallas_call(
        body, name="comm_scatter_chips", in_specs=[ANY] * n, out_specs=[ANY] * n,
        out_shape=[SDS((N_CHIPS - 1,) + a.shape[1:], a.dtype) for a in arrs],
        scratch_shapes=[pltpu.SemaphoreType.DMA((3 * n,)), pltpu.SemaphoreType.DMA((3 * n,))],
        compiler_params=COMM_PARAMS,
    )(*arrs)


def swap_with_sibling(arrs):
    n = len(arrs)

    def body(*refs):
        ins, outs = refs[:n], refs[n:2 * n]
        ssem, rsem = refs[2 * n:]
        x, y, c, _ = _place()
        cps = [_rcopy(ins[a], outs[a], ssem.at[a], rsem.at[a], (x, y, 1 - c)) for a in range(n)]
        for cp in cps:
            cp.start()
        for cp in cps:
            cp.wait()

    return pl.pallas_call(
        body, name="comm_swap_totals", in_specs=[ANY] * n, out_specs=[ANY] * n,
        out_shape=[SDS(a.shape, a.dtype) for a in arrs],
        scratch_shapes=[pltpu.SemaphoreType.DMA((n,)), pltpu.SemaphoreType.DMA((n,))], compiler_params=COMM_PARAMS,
    )(*arrs)


def gather_all(buf):
    def body(in_ref, out_ref, ssem, rsem, lsem):
        x, y, c, _ = _place()
        me = 4 * x + 2 * y + c
        local = pltpu.make_async_copy(in_ref, out_ref.at[me], lsem)
        local.start()
        flips = [(a, b, e) for a in (0, 1) for b in (0, 1) for e in (0, 1)][1:]
        cps = []
        for i, (a, b, e) in enumerate(flips):
            peer = (x ^ a, y ^ b, c ^ e)
            cps.append(_rcopy(in_ref, out_ref.at[me], ssem.at[i], rsem.at[i], peer))
        for cp in cps:
            cp.start()
        for i, (a, b, e) in enumerate(flips):
            cps[i].wait_send()
            slot = out_ref.at[4 * (x ^ a) + 2 * (y ^ b) + (c ^ e)]
            _rcopy(slot, slot, ssem.at[i], rsem.at[i], (x ^ a, y ^ b, c ^ e)).wait_recv()
        local.wait()

    return pl.pallas_call(
        body, name="comm_gather_all", in_specs=[ANY], out_specs=ANY, out_shape=SDS((N_DEV,) + buf.shape, buf.dtype),
        scratch_shapes=[pltpu.SemaphoreType.DMA((N_DEV - 1,)), pltpu.SemaphoreType.DMA((N_DEV - 1,)), pltpu.SemaphoreType.DMA],
        compiler_params=COMM_PARAMS,
    )(buf)


def _pack(parts):
    flat = jnp.concatenate([p.reshape(-1) for p in parts])
    pad = (-flat.size) % (SUBLANES * LANES)
    return jnp.pad(flat, (0, pad)).reshape(-1, LANES)


def _unpack(buf, shapes):
    flat, out, off = buf.reshape(-1), [], 0
    for s in shapes:
        n = 1
        for d in s:
            n *= d
        out.append(flat[off:off + n].reshape(s))
        off += n
    return out


def _pad_lanes(v):
    v = v.reshape(1, -1)
    return jnp.pad(v, ((0, 0), (0, LANES - v.shape[1])))


def _as2d(a):
    return a.reshape(1, -1) if a.ndim == 1 else a.reshape(-1, a.shape[-1])


def kernel(x, even_norm_g, even_w_in, gmlp_ln_g, gmlp_ln_b, gmlp_ws, gmlp_bs, ssd_conv_w, ssd_conv_b, ssd_dt_bias, ssd_a_log, ssd_d, ssd_norm_g, even_w_out, odd_norm_g, odd_w_in, sconv_w, odd_w_out, final_norm_g, loss_target, m_even_norm_g, m_even_w_in, m_gmlp_ln_g, m_gmlp_ln_b, m_gmlp_ws, m_gmlp_bs, m_ssd_conv_w, m_ssd_conv_b, m_ssd_dt_bias, m_ssd_a_log, m_ssd_d, m_ssd_norm_g, m_even_w_out, m_odd_norm_g, m_odd_w_in, m_sconv_w, m_odd_w_out, m_final_norm_g, v_even_norm_g, v_even_w_in, v_gmlp_ln_g, v_gmlp_ln_b, v_gmlp_ws, v_gmlp_bs, v_ssd_conv_w, v_ssd_conv_b, v_ssd_dt_bias, v_ssd_a_log, v_ssd_d, v_ssd_norm_g, v_even_w_out, v_odd_norm_g, v_odd_w_in, v_sconv_w, v_odd_w_out, v_final_norm_g):
    weights = dict(even_norm_g=even_norm_g, even_w_in=even_w_in, gmlp_ln_g=gmlp_ln_g, gmlp_ln_b=gmlp_ln_b, gmlp_ws=gmlp_ws, gmlp_bs=gmlp_bs, ssd_conv_w=ssd_conv_w, ssd_conv_b=ssd_conv_b, ssd_dt_bias=ssd_dt_bias, ssd_a_log=ssd_a_log, ssd_d=ssd_d, ssd_norm_g=ssd_norm_g, even_w_out=even_w_out, odd_norm_g=odd_norm_g, odd_w_in=odd_w_in, sconv_w=sconv_w, odd_w_out=odd_w_out, final_norm_g=final_norm_g)
    moms_m = dict(even_norm_g=m_even_norm_g, even_w_in=m_even_w_in, gmlp_ln_g=m_gmlp_ln_g, gmlp_ln_b=m_gmlp_ln_b, gmlp_ws=m_gmlp_ws, gmlp_bs=m_gmlp_bs, ssd_conv_w=m_ssd_conv_w, ssd_conv_b=m_ssd_conv_b, ssd_dt_bias=m_ssd_dt_bias, ssd_a_log=m_ssd_a_log, ssd_d=m_ssd_d, ssd_norm_g=m_ssd_norm_g, even_w_out=m_even_w_out, odd_norm_g=m_odd_norm_g, odd_w_in=m_odd_w_in, sconv_w=m_sconv_w, odd_w_out=m_odd_w_out, final_norm_g=m_final_norm_g)
    moms_v = dict(even_norm_g=v_even_norm_g, even_w_in=v_even_w_in, gmlp_ln_g=v_gmlp_ln_g, gmlp_ln_b=v_gmlp_ln_b, gmlp_ws=v_gmlp_ws, gmlp_bs=v_gmlp_bs, ssd_conv_w=v_ssd_conv_w, ssd_conv_b=v_ssd_conv_b, ssd_dt_bias=v_ssd_dt_bias, ssd_a_log=v_ssd_a_log, ssd_d=v_ssd_d, ssd_norm_g=v_ssd_norm_g, even_w_out=v_even_w_out, odd_norm_g=v_odd_norm_g, odd_w_in=v_odd_w_in, sconv_w=v_sconv_w, odd_w_out=v_odd_w_out, final_norm_g=v_final_norm_g)
    names = list(weights)

    xs = x[0]
    tgt = loss_target[0]
    T = xs.shape[0]
    chip = 2 * lax.axis_index("x") + lax.axis_index("y")
    core = lax.axis_index("c")
    cshard = B_XBC // N_CHIPS
    dshard = D_MODEL // N_CHIPS

    def halves(w):
        return w.astype(BF16).reshape(2, w.shape[0] // 2, w.shape[1])

    small_shard = jnp.concatenate([ssd_conv_w[0].reshape(-1), odd_norm_g[0], sconv_w[0].reshape(-1)])
    g_wie, g_woe, g_wio, g_woo, g_small = gather_shards(
        [halves(even_w_in[0]), halves(even_w_out[0]), halves(odd_w_in[0]), halves(odd_w_out[0]), small_shard.reshape(2, -1, LANES)])
    wie = g_wie.reshape(N_CHIPS, D_MODEL, IN_EVEN // N_CHIPS).transpose(1, 0, 2).reshape(D_MODEL, IN_EVEN)
    wie_main = wie[:, :EVEN_MAIN]
    wie_dt = jnp.pad(wie[:, EVEN_MAIN:], ((0, 0), (0, LANES - B_HEADS)))
    woe = g_woe.reshape(2 * A_WIDTH, D_MODEL)
    wio = g_wio.reshape(N_CHIPS, D_MODEL, IN_ODD // N_CHIPS).transpose(1, 0, 2).reshape(D_MODEL, IN_ODD)
    woo = g_woo.reshape(2 * C_WIDTH, D_MODEL)
    g_small = g_small.reshape(N_CHIPS, -1)
    n_cw = B_CONV * cshard
    conv_w = g_small[:, :n_cw].reshape(N_CHIPS, B_CONV, cshard).transpose(1, 0, 2).reshape(B_CONV, B_XBC)
    odd_g = g_small[:, n_cw:n_cw + dshard].reshape(1, D_MODEL)
    sconv = g_small[:, n_cw + dshard:].reshape(N_CHIPS, C_CONV, dshard).transpose(1, 0, 2).reshape(C_CONV, C_WIDTH)

    even_g = even_norm_g
    ln_g, ln_b = gmlp_ln_g, gmlp_ln_b
    ws, bs_t = gmlp_ws[0], gmlp_bs[0].T
    conv_b = ssd_conv_b
    dt_bias, a_log, d_skip = _pad_lanes(ssd_dt_bias), _pad_lanes(ssd_a_log), _pad_lanes(ssd_d)
    norm_g = ssd_norm_g
    fin_g = final_norm_g.reshape(1, D_MODEL)

    xn0 = rmsnorm_fwd(xs, even_g, name="even_norm")
    h0 = matmul(xn0, wie_main, "nn", name="even_in", tk=D_MODEL, tn=512)
    dtr = matmul(xn0, wie_dt, "nn", name="even_in_dt", tk=D_MODEL)
    ya = gmlp_fwd(h0, ln_g, ln_b, ws, bs_t)
    xbc = ssd_conv_fwd(h0, conv_w, conv_b)
    yb, states = ssd_fwd(xbc, dtr, h0, dt_bias, a_log, d_skip, norm_g)
    y0 = jnp.concatenate([ya, yb], axis=1)
    x1 = matmul(y0, woe, "nn", name="even_out", res=xs)

    xn1 = rmsnorm_fwd(x1, odd_g, name="odd_norm")
    h1 = matmul(xn1, wio, "nn", name="odd_in", tk=D_MODEL, tn=512)
    yc = sconv_fwd(h1, sconv)
    yd, att_o, att_lse = attn_fwd(h1)
    y1 = jnp.concatenate([yc, yd], axis=1)
    x2 = matmul(y1, woo, "nn", name="odd_out", res=x1)

    loss_part, dx2, dx2b, d_fin_g = loss_head(x2, fin_g, tgt)

    dy1 = matmul(dx2b, woo, "nt", name="odd_out_dy")
    d_woo = matmul(y1, dx2b, "tn", name="odd_out_dw")
    dbg, dcg, dhx, dzc, d_sconv = sconv_bwd(h1, dy1, sconv)
    dq, dzd = attn_bwd_dq(h1, att_o, att_lse, dy1)
    dk, dv = attn_bwd_dkv(h1, att_o, att_lse, dy1)
    dh1 = jnp.concatenate([dbg, dcg, dhx, dzc, dq, dk, dv, dzd], axis=1)
    dxn1 = matmul(dh1, wio, "nt", name="odd_in_dx")
    d_wio = matmul(xn1, dh1, "tn", name="odd_in_dw")
    dx1, dx1b, d_odd_g = rmsnorm_bwd(x1, odd_g, dxn1, dx2, name="odd_norm_bwd")

    dy0 = matmul(dx1b, woe, "nt", name="even_out_dy")
    d_woe = matmul(y0, dx1b, "tn", name="even_out_dw")
    du, dvv, dza, d_ln_g, d_ln_b, d_ws, d_bs_t = gmlp_bwd(h0, dy0, ln_g, ln_b, ws, bs_t)
    dxs, dbm, dcm, ddtr, dzb, d_dt_bias, d_a_log, d_d, d_norm_g = ssd_bwd(xbc, dtr, h0, states, dy0, dt_bias, a_log, d_skip, norm_g)
    dxbc, d_conv_w, d_conv_b = ssd_conv_bwd(h0, jnp.concatenate([dxs, dbm, dcm], axis=1), conv_w, conv_b)
    dh0 = jnp.concatenate([du, dvv, dza, dzb, dxbc], axis=1)
    ddtr_b = ddtr.astype(BF16)
    dxn0 = matmul(ddtr_b, wie_dt, "nt", name="even_in_dx_dt")
    dxn0 = matmul(dh0, wie_main, "nt", name="even_in_dx", res=dxn0)
    d_wie_main = matmul(xn0, dh0, "tn", name="even_in_dw")
    d_wie_dt = matmul(xn0, ddtr_b, "tn", name="even_in_dw_dt")
    grad_x, _, d_even_g = rmsnorm_bwd(xs, even_g, dxn0, dx1, name="even_norm_bwd")
    d_wie = jnp.concatenate([d_wie_main, d_wie_dt[:, :B_HEADS]], axis=1)

    def by_cols(g):
        r, n = g.shape
        return g.reshape(2, r // 2, N_CHIPS, n // N_CHIPS).transpose(0, 2, 1, 3)

    def by_rows(g):
        r, n = g.shape
        return g.reshape(N_CHIPS, 2, r // (2 * N_CHIPS), n).transpose(1, 0, 2, 3)

    pieces = [by_cols(d_wie), by_rows(d_woe), by_cols(d_wio), by_rows(d_woo)]
    from_sib = swap_sibling_halves(pieces)
    chip_sums = [chip_sum(p, s, core, name=f"chip_sum_{i}") for i, (p, s) in enumerate(zip(pieces, from_sib))]
    landed = scatter_to_chips(chip_sums)
    totals = [total_sum(s, l, chip, name=f"total_{i}") for i, (s, l) in enumerate(zip(chip_sums, landed))]
    from_sib = swap_with_sibling(totals)
    joined = [jnp.where(core == 0, jnp.stack([t, s]), jnp.stack([s, t])) for t, s in zip(totals, from_sib)]
    big = dict(even_w_in=joined[0].reshape(even_w_in.shape), even_w_out=joined[1].reshape(even_w_out.shape),
               odd_w_in=joined[2].reshape(odd_w_in.shape), odd_w_out=joined[3].reshape(odd_w_out.shape))

    small_names = ["even_norm_g", "gmlp_ln_g", "gmlp_ln_b", "gmlp_ws", "gmlp_bs", "ssd_conv_w", "ssd_conv_b", "ssd_dt_bias",
                   "ssd_a_log", "ssd_d", "ssd_norm_g", "odd_norm_g", "sconv_w", "final_norm_g"]
    small_parts = [d_even_g, d_ln_g, d_ln_b, d_ws, d_bs_t.T, d_conv_w, d_conv_b, d_dt_bias[:, :B_HEADS], d_a_log[:, :B_HEADS],
                   d_d[:, :B_HEADS], d_norm_g, d_odd_g, d_sconv, d_fin_g]
    small_sum = sum_leading(gather_all(_pack(small_parts)), name="small_sum")
    full = dict(zip(small_names, _unpack(small_sum, [p.shape for p in small_parts])))
    grads = dict(big)
    for nm in small_names:
        g = full[nm]
        if nm == "ssd_conv_w":
            g = lax.dynamic_slice_in_dim(g, chip * cshard, cshard, axis=1)
        elif nm in ("odd_norm_g", "sconv_w"):
            g = lax.dynamic_slice_in_dim(g, chip * dshard, dshard, axis=1)
        grads[nm] = g.reshape(weights[nm].shape)

    deltas, new_m, new_v = {}, {}, {}
    for nm in names:
        w = weights[nm]
        d, nm_, nv_ = adamw(_as2d(w), _as2d(grads[nm]), _as2d(moms_m[nm]), _as2d(moms_v[nm]), name=f"adamw_{nm}")
        deltas[nm], new_m[nm], new_v[nm] = d.reshape(w.shape), nm_.reshape(w.shape), nv_.reshape(w.shape)

    loss = lax.psum(loss_part[0, 0], ("x", "y", "c"))
    return (loss, grad_x[None], *[grads[n] for n in names], *[deltas[n] for n in names],
            *[new_m[n] for n in names], *[new_v[n] for n in names])
```

```python
import functools

import jax
import jax.numpy as jnp
from jax import lax
from jax.experimental import pallas as pl
from jax.experimental.pallas import tpu as pltpu

F32 = jnp.float32
BF16 = jnp.bfloat16
SDS = jax.ShapeDtypeStruct
MESH = pl.DeviceIdType.MESH

D_MODEL = 2048
A_WIDTH = 2048
A_GROUPS = 8
CHUNK = 128
B_WIDTH = 2048
B_HEADS = 32
B_HEAD_DIM = 64
B_GROUPS = 8
B_STATE = 128
B_CONV = 4
B_XBC = B_WIDTH + 2 * B_GROUPS * B_STATE
C_WIDTH = 2048
C_CONV = 3
D_HEADS = 16
D_HEAD_DIM = 128
D_PATTERNS = ((128, 1), (512, 4), (2048, 16))
ATT_BLOCK = 128
ATT_SUPER = 2048
EVEN_MAIN = 3 * A_WIDTH + B_WIDTH + B_XBC
IN_EVEN = EVEN_MAIN + B_HEADS
IN_ODD = 4 * C_WIDTH + 4 * D_HEADS * D_HEAD_DIM
LANES = 128
SUBLANES = 8
EPS = 1e-5
ADAM_LR = 0.001
ADAM_B1 = 0.9
ADAM_B2 = 0.999
ADAM_EPS = 1e-08
ADAM_WD = 0.01
ADAM_STEP = 10
N_CHIPS = 4
N_DEV = 8
VMEM_LIMIT_BYTES = 56 * 1024 * 1024


def _cp(*sem):
    return pltpu.CompilerParams(dimension_semantics=sem, vmem_limit_bytes=VMEM_LIMIT_BYTES)


def _full(shape):
    return pl.BlockSpec(shape, lambda *_: (0,) * len(shape))


def _silu(x):
    return x * jax.nn.sigmoid(x)


def _dot_nn(a, b):
    return lax.dot_general(a, b, (((1,), (0,)), ((), ())), preferred_element_type=F32)


def _dot_nt(a, b):
    return lax.dot_general(a, b, (((1,), (1,)), ((), ())), preferred_element_type=F32)


def _dot_tn(a, b):
    return lax.dot_general(a, b, (((0,), (0,)), ((), ())), preferred_element_type=F32)


def _tril(n):
    return lax.broadcasted_iota(jnp.int32, (n, n), 0) >= lax.broadcasted_iota(jnp.int32, (n, n), 1)


_DOTS = {"nn": _dot_nn, "nt": _dot_nt, "tn": _dot_tn}


def matmul(a, b, mode, *, name, out_dtype=F32, res=None, tm=1024, tn=1024, tk=2048):
    shards = b.shape[0] if b.ndim == 3 else 1
    b_rows, b_cols = b.shape[-2], b.shape[-1] * shards
    if mode == "tn":
        (K, M), (K2, N) = a.shape, (b_rows, b_cols)
    elif mode == "nt":
        (M, K), (N, K2) = a.shape, (b_rows, b_cols)
    else:
        (M, K), (K2, N) = a.shape, (b_rows, b_cols)
    assert K == K2, (a.shape, b.shape, mode)
    tm, tn, tk = min(tm, M), min(tn, N), min(tk, K)
    if shards > 1:
        assert mode != "tn" and b.shape[-1] % (tk if mode == "nt" else tn) == 0
    assert M % tm == 0 and N % tn == 0 and K % tk == 0, (M, N, K, tm, tn, tk)
    nk = K // tk
    dot = _DOTS[mode]

    def body(*refs):
        a_ref, b_ref = refs[:2]
        r_ref = refs[2] if res is not None else None
        o_ref = refs[3] if res is not None else refs[2]

        def product():
            bv = b_ref[0] if shards > 1 else b_ref[...]
            return dot(a_ref[...], bv)

        def finish(acc):
            if res is not None:
                acc = acc + r_ref[...]
            o_ref[...] = acc.astype(o_ref.dtype)

        if nk == 1:
            finish(product())
            return
        acc_ref = refs[-1]
        k = pl.program_id(2)

        @pl.when(k == 0)
        def _():
            acc_ref[...] = product()

        @pl.when(jnp.logical_and(k > 0, k < nk - 1))
        def _():
            acc_ref[...] += product()

        @pl.when(k == nk - 1)
        def _():
            finish(acc_ref[...] + product())

    a_spec = pl.BlockSpec((tk, tm), lambda i, j, k: (k, i)) if mode == "tn" else pl.BlockSpec((tm, tk), lambda i, j, k: (i, k))
    if shards > 1 and mode == "nn":
        per = b.shape[-1] // tn
        b_spec = pl.BlockSpec((1, tk, tn), lambda i, j, k: (j // per, k, j % per))
    elif shards > 1:
        per = b.shape[-1] // tk
        b_spec = pl.BlockSpec((1, tn, tk), lambda i, j, k: (k // per, j, k % per))
    elif mode == "nt":
        b_spec = pl.BlockSpec((tn, tk), lambda i, j, k: (j, k))
    else:
        b_spec = pl.BlockSpec((tk, tn), lambda i, j, k: (k, j))
    o_spec = pl.BlockSpec((tm, tn), lambda i, j, k: (i, j))
    in_specs, args = [a_spec, b_spec], [a, b]
    if res is not None:
        in_specs.append(o_spec)
        args.append(res)
    return pl.pallas_call(
        body, name=name, grid=(M // tm, N // tn, nk), in_specs=in_specs, out_specs=o_spec,
        out_shape=SDS((M, N), out_dtype), scratch_shapes=[pltpu.VMEM((tm, tn), F32)] if nk > 1 else [],
        compiler_params=_cp("parallel", "parallel", "arbitrary"),
    )(*args)


ROW_TILE = 512


def _rms(x, g):
    return x * lax.rsqrt(jnp.mean(x * x, axis=-1, keepdims=True) + EPS) * g


def rmsnorm_fwd(x, g, *, name):
    T, D = x.shape

    def body(x_ref, g_ref, o_ref):
        o_ref[...] = _rms(x_ref[...], g_ref[...]).astype(BF16)

    row = pl.BlockSpec((ROW_TILE, D), lambda i: (i, 0))
    return pl.pallas_call(body, name=name, grid=(T // ROW_TILE,), in_specs=[row, _full((1, D))], out_specs=row,
                          out_shape=SDS((T, D), BF16), compiler_params=_cp("parallel"))(x, g)


def rmsnorm_bwd(x, g, dxn, dres, *, name):
    T, D = x.shape

    def body(x_ref, g_ref, dxn_ref, dres_ref, dx_ref, dxb_ref, dg_ref):
        _, vjp = jax.vjp(_rms, x_ref[...], g_ref[...])
        dx, dg = vjp(dxn_ref[...])
        dx = dx + dres_ref[...]
        dx_ref[...] = dx
        dxb_ref[...] = dx.astype(BF16)

        @pl.when(pl.program_id(0) == 0)
        def _():
            dg_ref[...] = jnp.zeros_like(dg_ref)

        dg_ref[...] += dg

    row = pl.BlockSpec((ROW_TILE, D), lambda i: (i, 0))
    return pl.pallas_call(
        body, name=name, grid=(T // ROW_TILE,), in_specs=[row, _full((1, D)), row, row],
        out_specs=[row, row, _full((1, D))], out_shape=[SDS((T, D), F32), SDS((T, D), BF16), SDS((1, D), F32)],
        compiler_params=_cp("arbitrary"))(x, g, dxn, dres)


def _loss_tile(x, g, tgt):
    err = jnp.square(_rms(x, g) - tgt)
    return 0.5 * jnp.sum(jnp.mean(err, axis=-1))


def loss_head(x, g, tgt):
    T, D = x.shape

    def body(x_ref, g_ref, t_ref, loss_ref, dx_ref, dxb_ref, dg_ref):
        loss, vjp = jax.vjp(_loss_tile, x_ref[...], g_ref[...], t_ref[...])
        dx, dg, _ = vjp(jnp.ones((), F32))
        dx_ref[...] = dx
        dxb_ref[...] = dx.astype(BF16)

        @pl.when(pl.program_id(0) == 0)
        def _():
            dg_ref[...] = jnp.zeros_like(dg_ref)
            loss_ref[...] = jnp.zeros_like(loss_ref)

        dg_ref[...] += dg
        loss_ref[...] += jnp.reshape(loss, (1, 1))

    row = pl.BlockSpec((ROW_TILE, D), lambda i: (i, 0))
    return pl.pallas_call(
        body, name="loss_head", grid=(T // ROW_TILE,), in_specs=[row, _full((1, D)), row],
        out_specs=[_full((1, 1)), row, row, _full((1, D))],
        out_shape=[SDS((1, 1), F32), SDS((T, D), F32), SDS((T, D), BF16), SDS((1, D), F32)],
        compiler_params=_cp("arbitrary"))(x, g, tgt)


TILE_BYTES = 1 << 20


def _row_tile(rows, row_bytes):
    for cand in (512, 256, 128, 64, 32, 16, 8):
        if rows % cand == 0 and cand * row_bytes <= TILE_BYTES:
            return cand
    return rows


def adamw(w, g, m, v, *, name):
    R, C = w.shape
    tr = _row_tile(R, C * 4)

    def body(w_ref, g_ref, m_ref, v_ref, d_ref, nm_ref, nv_ref):
        gg = g_ref[...]
        mm = ADAM_B1 * m_ref[...] + (1.0 - ADAM_B1) * gg
        vv = ADAM_B2 * v_ref[...] + (1.0 - ADAM_B2) * jnp.square(gg)
        m_hat = mm / (1.0 - ADAM_B1 ** ADAM_STEP)
        v_hat = vv / (1.0 - ADAM_B2 ** ADAM_STEP)
        d_ref[...] = -ADAM_LR * (m_hat / (jnp.sqrt(v_hat) + ADAM_EPS) + ADAM_WD * w_ref[...])
        nm_ref[...] = mm
        nv_ref[...] = vv

    blk = pl.BlockSpec((tr, C), lambda i: (i, 0))
    return pl.pallas_call(body, name=name, grid=(R // tr,), in_specs=[blk] * 4, out_specs=[blk] * 3,
                          out_shape=[SDS((R, C), F32)] * 3, compiler_params=_cp("parallel"))(w, g, m, v)


def sum_leading(a, *, name):
    n, R, C = a.shape
    tr = _row_tile(R, n * C * 4)

    def body(a_ref, o_ref):
        acc = a_ref[0]
        for j in range(1, n):
            acc = acc + a_ref[j]
        o_ref[...] = acc

    return pl.pallas_call(body, name=name, grid=(R // tr,), in_specs=[pl.BlockSpec((n, tr, C), lambda i: (0, i, 0))],
                          out_specs=pl.BlockSpec((tr, C), lambda i: (i, 0)), out_shape=SDS((R, C), F32),
                          compiler_params=_cp("parallel"))(a)


def chip_sum(pieces, from_sibling, core, *, name):
    _, n, R, C = pieces.shape

    def body(c_ref, a_ref, b_ref, o_ref):
        o_ref[...] = (a_ref[0] + b_ref[...]).astype(BF16)

    tr = _row_tile(R, C * 4)
    blk = pl.BlockSpec((1, tr, C), lambda k, i, c_ref: (k, i, 0))
    mine = pl.BlockSpec((1, 1, tr, C), lambda k, i, c_ref: (c_ref[0], k, i, 0))
    return pl.pallas_call(
        body, name=name, out_shape=SDS((n, R, C), BF16),
        grid_spec=pltpu.PrefetchScalarGridSpec(num_scalar_prefetch=1, grid=(n, R // tr), in_specs=[mine, blk], out_specs=blk),
        compiler_params=_cp("parallel", "parallel"))(core.reshape(1), pieces, from_sibling)


def total_sum(sums, landed, chip, *, name):
    n, R, C = landed.shape

    def body(k_ref, s_ref, l_ref, o_ref):
        acc = s_ref[0].astype(F32)
        for j in range(n):
            acc = acc + l_ref[j].astype(F32)
        o_ref[...] = acc

    tr = _row_tile(R, n * C * 2)
    return pl.pallas_call(
        body, name=name, out_shape=SDS((R, C), F32),
        grid_spec=pltpu.PrefetchScalarGridSpec(
            num_scalar_prefetch=1, grid=(R // tr,),
            in_specs=[pl.BlockSpec((1, tr, C), lambda i, k_ref: (k_ref[0], i, 0)), pl.BlockSpec((n, tr, C), lambda i, k_ref: (0, i, 0))],
            out_specs=pl.BlockSpec((tr, C), lambda i, k_ref: (i, 0))),
        compiler_params=_cp("parallel"))(chip.reshape(1), sums, landed)


def _gmlp_chunk(u, v, z, ln_g, ln_b, wsc, bs_t):
    mu = jnp.mean(v, axis=-1, keepdims=True)
    xc = v - mu
    vn = xc * lax.rsqrt(jnp.mean(xc * xc, axis=-1, keepdims=True) + EPS) * ln_g + ln_b
    gw = A_WIDTH // A_GROUPS
    outs = []
    for g in range(A_GROUPS):
        m = _dot_nn(wsc[g].astype(BF16), vn[:, g * gw:(g + 1) * gw].astype(BF16))
        outs.append(m + bs_t[:, g:g + 1])
    return _silu(z) * (u * jnp.concatenate(outs, axis=1))


def _h_cols(width, idx, rows=CHUNK):
    return pl.BlockSpec((rows, width), lambda i: (i, idx))


def gmlp_fwd(h, ln_g, ln_b, ws, bs_t):
    T = h.shape[0]

    def body(u_ref, v_ref, z_ref, g_ref, b_ref, ws_ref, bs_ref, o_ref):
        wsc = jnp.where(_tril(CHUNK)[None], ws_ref[...], 0.0)
        o_ref[...] = _gmlp_chunk(u_ref[...], v_ref[...], z_ref[...], g_ref[...], b_ref[...], wsc, bs_ref[...]).astype(BF16)

    return pl.pallas_call(
        body, name="gmlp_fwd", grid=(T // CHUNK,),
        in_specs=[_h_cols(A_WIDTH, 0), _h_cols(A_WIDTH, 1), _h_cols(A_WIDTH, 2), _full((1, A_WIDTH)), _full((1, A_WIDTH)),
                  _full((A_GROUPS, CHUNK, CHUNK)), _full((CHUNK, A_GROUPS))],
        out_specs=_h_cols(A_WIDTH, 0), out_shape=SDS((T, A_WIDTH), BF16), compiler_params=_cp("parallel"),
    )(h, h, h, ln_g, ln_b, ws, bs_t)


def gmlp_bwd(h, dy, ln_g, ln_b, ws, bs_t):
    T = h.shape[0]

    def body(u_ref, v_ref, z_ref, dy_ref, g_ref, b_ref, ws_ref, bs_ref, duvz_ref, dg_ref, db_ref, dws_ref, dbs_ref):
        tri = _tril(CHUNK)[None]
        wsc = jnp.where(tri, ws_ref[...], 0.0)
        _, vjp = jax.vjp(_gmlp_chunk, u_ref[...], v_ref[...], z_ref[...], g_ref[...], b_ref[...], wsc, bs_ref[...])
        du, dv, dz, dg, db, dws, dbs = vjp(dy_ref[...])
        duvz_ref[:, :A_WIDTH] = du.astype(BF16)
        duvz_ref[:, A_WIDTH:2 * A_WIDTH] = dv.astype(BF16)
        duvz_ref[:, 2 * A_WIDTH:] = dz.astype(BF16)

        @pl.when(pl.program_id(0) == 0)
        def _():
            dg_ref[...] = jnp.zeros_like(dg_ref)
            db_ref[...] = jnp.zeros_like(db_ref)
            dws_ref[...] = jnp.zeros_like(dws_ref)
            dbs_ref[...] = jnp.zeros_like(dbs_ref)

        dg_ref[...] += dg
        db_ref[...] += db
        dws_ref[...] += jnp.where(tri, dws, 0.0)
        dbs_ref[...] += dbs

    pshapes = [(1, A_WIDTH), (1, A_WIDTH), (A_GROUPS, CHUNK, CHUNK), (CHUNK, A_GROUPS)]
    return pl.pallas_call(
        body, name="gmlp_bwd", grid=(T // CHUNK,),
        in_specs=[_h_cols(A_WIDTH, 0), _h_cols(A_WIDTH, 1), _h_cols(A_WIDTH, 2), _h_cols(A_WIDTH, 0)] + [_full(s) for s in pshapes],
        out_specs=[_h_cols(3 * A_WIDTH, 0)] + [_full(s) for s in pshapes],
        out_shape=[SDS((T, 3 * A_WIDTH), BF16)] + [SDS(s, F32) for s in pshapes],
        compiler_params=_cp("arbitrary"),
    )(h, h, h, dy, ln_g, ln_b, ws, bs_t)


CONV_ROWS = 256
CONV_COLS = 512


def _taps(xe, w, rows):
    K = w.shape[0]
    acc = None
    for k in range(K):
        off = SUBLANES - (K - 1) + k
        term = w[k:k + 1, :] * xe[off:off + rows, :]
        acc = term if acc is None else acc + term
    return acc


def _ssd_conv_tile(x, halo, w, b):
    return _silu(_taps(jnp.concatenate([halo, x], axis=0), w, x.shape[0]) + b)


def _halo_spec(cols, col_idx, nt=None):
    rpb = CONV_ROWS // SUBLANES
    if nt is None:
        return pl.BlockSpec((SUBLANES, cols), lambda c, i: (jnp.maximum(i * rpb - 1, 0), col_idx(c)))
    return pl.BlockSpec((SUBLANES, cols), lambda c, j: (jnp.maximum((nt - 1 - j) * rpb - 1, 0), col_idx(c)))


def ssd_conv_fwd(h, w, b):
    T = h.shape[0]
    nc = B_XBC // CONV_COLS
    base = (3 * A_WIDTH + B_WIDTH) // CONV_COLS

    def body(x_ref, halo_ref, w_ref, b_ref, o_ref):
        halo = jnp.where(pl.program_id(1) > 0, halo_ref[...], 0.0)
        o_ref[...] = _ssd_conv_tile(x_ref[...], halo, w_ref[...], b_ref[...])

    return pl.pallas_call(
        body, name="ssd_conv_fwd", grid=(nc, T // CONV_ROWS),
        in_specs=[pl.BlockSpec((CONV_ROWS, CONV_COLS), lambda c, i: (i, base + c)), _halo_spec(CONV_COLS, lambda c: base + c),
                  pl.BlockSpec((B_CONV, CONV_COLS), lambda c, i: (0, c)), pl.BlockSpec((1, CONV_COLS), lambda c, i: (0, c))],
        out_specs=pl.BlockSpec((CONV_ROWS, CONV_COLS), lambda c, i: (i, c)),
        out_shape=SDS((T, B_XBC), F32), compiler_params=_cp("parallel", "parallel"),
    )(h, h, w, b)


def ssd_conv_bwd(h, dy, w, b):
    T = h.shape[0]
    nc = B_XBC // CONV_COLS
    nt = T // CONV_ROWS
    base = (3 * A_WIDTH + B_WIDTH) // CONV_COLS

    def body(x_ref, halo_ref, dy_ref, w_ref, b_ref, dx_ref, dw_ref, db_ref, carry_ref):
        j = pl.program_id(1)
        halo = jnp.where(j < nt - 1, halo_ref[...], 0.0)
        _, vjp = jax.vjp(_ssd_conv_tile, x_ref[...], halo, w_ref[...], b_ref[...])
        dx, dhalo, dw, db = vjp(dy_ref[...])

        @pl.when(j == 0)
        def _():
            carry_ref[...] = jnp.zeros_like(carry_ref)
            dw_ref[...] = jnp.zeros_like(dw_ref)
            db_ref[...] = jnp.zeros_like(db_ref)

        tail = dx[CONV_ROWS - SUBLANES:, :] + carry_ref[...]
        dx_ref[...] = jnp.concatenate([dx[:CONV_ROWS - SUBLANES, :], tail], axis=0).astype(BF16)
        carry_ref[...] = dhalo
        dw_ref[...] += dw
        db_ref[...] += db

    return pl.pallas_call(
        body, name="ssd_conv_bwd", grid=(nc, nt),
        in_specs=[pl.BlockSpec((CONV_ROWS, CONV_COLS), lambda c, j: (nt - 1 - j, base + c)),
                  _halo_spec(CONV_COLS, lambda c: base + c, nt),
                  pl.BlockSpec((CONV_ROWS, CONV_COLS), lambda c, j: (nt - 1 - j, c)),
                  pl.BlockSpec((B_CONV, CONV_COLS), lambda c, j: (0, c)), pl.BlockSpec((1, CONV_COLS), lambda c, j: (0, c))],
        out_specs=[pl.BlockSpec((CONV_ROWS, CONV_COLS), lambda c, j: (nt - 1 - j, c)),
                   pl.BlockSpec((B_CONV, CONV_COLS), lambda c, j: (0, c)), pl.BlockSpec((1, CONV_COLS), lambda c, j: (0, c))],
        out_shape=[SDS((T, B_XBC), BF16), SDS((B_CONV, B_XBC), F32), SDS((1, B_XBC), F32)],
        scratch_shapes=[pltpu.VMEM((SUBLANES, CONV_COLS), F32)],
        compiler_params=_cp("parallel", "arbitrary"),
    )(h, h, dy, w, b)


def _sconv_tile(bg, cg, hx, z, cg_halo, hx_halo, w):
    ch = jnp.concatenate([cg_halo * hx_halo, cg * hx], axis=0)
    return _silu(z) * (bg * _taps(ch, w, bg.shape[0]))


def sconv_fwd(h, w):
    T = h.shape[0]
    nc = C_WIDTH // CONV_COLS

    def col(seg):
        return pl.BlockSpec((CONV_ROWS, CONV_COLS), lambda c, i: (i, seg * nc + c))

    def body(bg_ref, cg_ref, hx_ref, z_ref, cgh_ref, hxh_ref, w_ref, o_ref):
        first = pl.program_id(1) == 0
        cgh = jnp.where(first, 0.0, cgh_ref[...])
        hxh = jnp.where(first, 0.0, hxh_ref[...])
        o_ref[...] = _sconv_tile(bg_ref[...], cg_ref[...], hx_ref[...], z_ref[...], cgh, hxh, w_ref[...]).astype(BF16)

    return pl.pallas_call(
        body, name="sconv_fwd", grid=(nc, T // CONV_ROWS),
        in_specs=[col(0), col(1), col(2), col(3), _halo_spec(CONV_COLS, lambda c: nc + c), _halo_spec(CONV_COLS, lambda c: 2 * nc + c),
                  pl.BlockSpec((C_CONV, CONV_COLS), lambda c, i: (0, c))],
        out_specs=pl.BlockSpec((CONV_ROWS, CONV_COLS), lambda c, i: (i, c)),
        out_shape=SDS((T, C_WIDTH), BF16), compiler_params=_cp("parallel", "parallel"),
    )(h, h, h, h, h, h, w)


def sconv_bwd(h, dy, w):
    T = h.shape[0]
    nc = C_WIDTH // CONV_COLS
    nt = T // CONV_ROWS

    def col(seg):
        return pl.BlockSpec((CONV_ROWS, CONV_COLS), lambda c, j: (nt - 1 - j, seg * nc + c))

    def body(bg_ref, cg_ref, hx_ref, z_ref, cgh_ref, hxh_ref, dy_ref, w_ref, dbg_ref, dcg_ref, dhx_ref, dz_ref, dw_ref, ccg_ref, chx_ref):
        j = pl.program_id(1)
        first = j == nt - 1
        cgh = jnp.where(first, 0.0, cgh_ref[...])
        hxh = jnp.where(first, 0.0, hxh_ref[...])
        _, vjp = jax.vjp(_sconv_tile, bg_ref[...], cg_ref[...], hx_ref[...], z_ref[...], cgh, hxh, w_ref[...])
        dbg, dcg, dhx, dz, dcgh, dhxh, dw = vjp(dy_ref[...])

        @pl.when(j == 0)
        def _():
            ccg_ref[...] = jnp.zeros_like(ccg_ref)
            chx_ref[...] = jnp.zeros_like(chx_ref)
            dw_ref[...] = jnp.zeros_like(dw_ref)

        def with_carry(d, carry_ref):
            tail = d[CONV_ROWS - SUBLANES:, :] + carry_ref[...]
            return jnp.concatenate([d[:CONV_ROWS - SUBLANES, :], tail], axis=0).astype(BF16)

        dbg_ref[...] = dbg.astype(BF16)
        dz_ref[...] = dz.astype(BF16)
        dcg_ref[...] = with_carry(dcg, ccg_ref)
        dhx_ref[...] = with_carry(dhx, chx_ref)
        ccg_ref[...] = dcgh
        chx_ref[...] = dhxh
        dw_ref[...] += dw

    out_row = pl.BlockSpec((CONV_ROWS, CONV_COLS), lambda c, j: (nt - 1 - j, c))
    wspec = pl.BlockSpec((C_CONV, CONV_COLS), lambda c, j: (0, c))
    return pl.pallas_call(
        body, name="sconv_bwd", grid=(nc, nt),
        in_specs=[col(0), col(1), col(2), col(3), _halo_spec(CONV_COLS, lambda c: nc + c, nt), _halo_spec(CONV_COLS, lambda c: 2 * nc + c, nt),
                  out_row, wspec],
        out_specs=[out_row] * 4 + [wspec],
        out_shape=[SDS((T, C_WIDTH), BF16)] * 4 + [SDS((C_CONV, C_WIDTH), F32)],
        scratch_shapes=[pltpu.VMEM((SUBLANES, CONV_COLS), F32)] * 2,
        compiler_params=_cp("parallel", "arbitrary"),
    )(h, h, h, h, h, h, dy, w)


def _softplus(x):
    return jnp.maximum(x, 0.0) + jnp.log(1.0 + jnp.exp(-jnp.abs(x)))


def _ssd_chunk(xs, bm, cm, dtr, z, prev, dt_bias, a_log, d_skip, norm_g):
    tril = _tril(CHUNK)
    dt = _softplus(dtr + dt_bias)
    adt = dt * (-jnp.exp(a_log))
    a_cs = jnp.dot(tril.astype(F32), adt, precision=lax.Precision.HIGHEST, preferred_element_type=F32)
    a_cs_t = a_cs.T
    a_last = a_cs[CHUNK - 1:CHUNK, :]
    dt_f = _spread_heads(dt, B_HEAD_DIM)
    dec_f = _spread_heads(jnp.exp(a_last - a_cs), B_HEAD_DIM)
    ecs_f = _spread_heads(jnp.exp(a_cs), B_HEAD_DIM)
    dsk_f = _spread_heads(d_skip, B_HEAD_DIM)
    cd_t = jnp.exp(a_cs_t[:, CHUNK - 1:CHUNK])
    xdt = xs * dt_f
    xdd = xdt * dec_f
    colb = _spread_heads(a_cs, CHUNK)
    rowb = jnp.concatenate([jnp.broadcast_to(a_cs_t[hh:hh + 1, :], (CHUNK, CHUNK)) for hh in range(B_HEADS)], axis=1)
    wide = (CHUNK, B_HEADS * CHUNK)
    keep = lax.broadcasted_iota(jnp.int32, wide, 0) >= lax.broadcasted_iota(jnp.int32, wide, 1) % CHUNK
    decay = jnp.exp(jnp.where(keep, colb - rowb, -jnp.inf))
    hpg = B_HEADS // B_GROUPS
    gw = B_WIDTH // B_GROUPS
    low_half = lax.broadcasted_iota(jnp.int32, (CHUNK, 2 * B_HEAD_DIM), 1) < B_HEAD_DIM
    ys, nxt = [], []
    for g in range(B_GROUPS):
        bg = bm[:, g * B_STATE:(g + 1) * B_STATE].astype(BF16)
        cg = cm[:, g * B_STATE:(g + 1) * B_STATE].astype(BF16)
        cb = _dot_nt(cg, bg)
        cbl = (decay[:, g * hpg * CHUNK:(g + 1) * hpg * CHUNK] * jnp.concatenate([cb] * hpg, axis=1)).astype(BF16)
        pg = prev[g * gw:(g + 1) * gw, :]
        y_off = _dot_nt(cg, pg.astype(BF16)) * ecs_f[:, g * gw:(g + 1) * gw]
        st = _dot_tn(xdd[:, g * gw:(g + 1) * gw].astype(BF16), bg)
        cd = jnp.concatenate([jnp.broadcast_to(cd_t[g * hpg + r:g * hpg + r + 1, :], (B_HEAD_DIM, 1)) for r in range(hpg)], axis=0)
        nxt.append(pg * cd + st)
        pairs = []
        for j in range(hpg // 2):
            xp = xdt[:, g * gw + 2 * j * B_HEAD_DIM:g * gw + 2 * (j + 1) * B_HEAD_DIM]
            rhs = jnp.concatenate([jnp.where(low_half, xp, 0.0), jnp.where(low_half, 0.0, xp)], axis=0).astype(BF16)
            pairs.append(_dot_nn(cbl[:, 2 * j * CHUNK:2 * (j + 1) * CHUNK], rhs))
        ys.append(jnp.concatenate(pairs, axis=1) + y_off)
    y = (jnp.concatenate(ys, axis=1) + dsk_f * xs) * _silu(z)
    outs = []
    for g in range(B_GROUPS):
        yg = y[:, g * gw:(g + 1) * gw]
        outs.append(yg * lax.rsqrt(jnp.mean(yg * yg, axis=-1, keepdims=True) + EPS))
    return jnp.concatenate(outs, axis=1) * norm_g, jnp.concatenate(nxt, axis=0)


def _spread_heads(v, width):
    n = B_HEADS * width
    one_hot = (lax.broadcasted_iota(jnp.int32, (LANES, n), 0) == lax.broadcasted_iota(jnp.int32, (LANES, n), 1) // width)
    return jnp.dot(v, one_hot.astype(F32), precision=lax.Precision.HIGHEST, preferred_element_type=F32)


_SSD_PARAM_SHAPES = [(1, LANES), (1, LANES), (1, LANES), (1, B_WIDTH)]
_STATE_SHAPE = (B_WIDTH, B_STATE)


def ssd_fwd(xbc, dtr, h, dt_bias, a_log, d_skip, norm_g):
    T = xbc.shape[0]
    nc = T // CHUNK

    def body(xs_ref, b_ref, c_ref, dt_ref, z_ref, p0, p1, p2, p3, y_ref, st_ref, state):
        @pl.when(pl.program_id(0) == 0)
        def _():
            state[...] = jnp.zeros_like(state)

        prev = state[...]
        st_ref[0] = prev
        yb, nxt = _ssd_chunk(xs_ref[...], b_ref[...], c_ref[...], dt_ref[...], z_ref[...], prev, p0[...], p1[...], p2[...], p3[...])
        y_ref[...] = yb.astype(BF16)
        state[...] = nxt

    return pl.pallas_call(
        body, name="ssd_fwd", grid=(nc,),
        in_specs=[_h_cols(B_WIDTH, 0), _h_cols(B_GROUPS * B_STATE, 2), _h_cols(B_GROUPS * B_STATE, 3), _h_cols(LANES, 0), _h_cols(B_WIDTH, 3)]
        + [_full(s) for s in _SSD_PARAM_SHAPES],
        out_specs=[_h_cols(B_WIDTH, 0), pl.BlockSpec((1,) + _STATE_SHAPE, lambda i: (i, 0, 0))],
        out_shape=[SDS((T, B_WIDTH), BF16), SDS((nc,) + _STATE_SHAPE, F32)],
        scratch_shapes=[pltpu.VMEM(_STATE_SHAPE, F32)], compiler_params=_cp("arbitrary"),
    )(xbc, xbc, xbc, dtr, h, dt_bias, a_log, d_skip, norm_g)


def ssd_bwd(xbc, dtr, h, states, dy, dt_bias, a_log, d_skip, norm_g):
    T = xbc.shape[0]
    nc = T // CHUNK

    def rev(width, idx):
        return pl.BlockSpec((CHUNK, width), lambda j: (nc - 1 - j, idx))

    def body(xs_ref, b_ref, c_ref, dt_ref, z_ref, st_ref, dy_ref, p0, p1, p2, p3,
             dxbc_ref, ddt_ref, dz_ref, g0, g1, g2, g3, dstate):
        @pl.when(pl.program_id(0) == 0)
        def _():
            dstate[...] = jnp.zeros_like(dstate)
            for gref in (g0, g1, g2, g3):
                gref[...] = jnp.zeros_like(gref)

        _, vjp = jax.vjp(_ssd_chunk, xs_ref[...], b_ref[...], c_ref[...], dt_ref[...], z_ref[...], st_ref[0],
                         p0[...], p1[...], p2[...], p3[...])
        dxs, dbm, dcm, ddt, dz, dprev, d0, d1, d2, d3 = vjp((dy_ref[...], dstate[...]))
        dxbc_ref[:, :B_WIDTH] = dxs
        dxbc_ref[:, B_WIDTH:B_WIDTH + gn] = dbm
        dxbc_ref[:, B_WIDTH + gn:] = dcm
        ddt_ref[...] = ddt
        dz_ref[...] = dz.astype(BF16)
        dstate[...] = dprev
        g0[...] += d0
        g1[...] += d1
        g2[...] += d2
        g3[...] += d3

    gn = B_GROUPS * B_STATE
    return pl.pallas_call(
        body, name="ssd_bwd", grid=(nc,),
        in_specs=[rev(B_WIDTH, 0), rev(gn, 2), rev(gn, 3), rev(LANES, 0), rev(B_WIDTH, 3),
                  pl.BlockSpec((1,) + _STATE_SHAPE, lambda j: (nc - 1 - j, 0, 0)), rev(B_WIDTH, 1)]
        + [_full(s) for s in _SSD_PARAM_SHAPES],
        out_specs=[rev(B_XBC, 0), rev(LANES, 0), rev(B_WIDTH, 0)] + [_full(s) for s in _SSD_PARAM_SHAPES],
        out_shape=[SDS((T, B_XBC), F32), SDS((T, LANES), F32), SDS((T, B_WIDTH), BF16)]
        + [SDS(s, F32) for s in _SSD_PARAM_SHAPES],
        scratch_shapes=[pltpu.VMEM(_STATE_SHAPE, F32)], compiler_params=_cp("arbitrary"),
    )(xbc, xbc, xbc, dtr, h, states, dy, dt_bias, a_log, d_skip, norm_g)


ATT_SCALE = D_HEAD_DIM ** -0.5
Q_COL, K_COL, V_COL, Z_COL = (4 * C_WIDTH // LANES + i * D_HEADS for i in range(4))


ATT_NBLK = ATT_SUPER // ATT_BLOCK


def _res_rows(r, first, count, dil):
    return pl.ds(r + dil * first, count) if dil == 1 else pl.ds(r + dil * first, count, stride=dil)


def _blocks(ref, dil, dtype=None):
    n = ATT_SUPER // dil
    parts = []
    for r in range(dil):
        v = ref[_res_rows(r, 0, n, dil), :]
        parts.append((v if dtype is None else v.astype(dtype)).reshape(n // ATT_BLOCK, ATT_BLOCK, D_HEAD_DIM))
    return parts[0] if dil == 1 else jnp.concatenate(parts, axis=0)


def _blocks_before(cur_ref, prev_ref, dil, dtype):
    n = ATT_SUPER // dil
    parts = []
    for r in range(dil):
        v = prev_ref[_res_rows(r, n - ATT_BLOCK, ATT_BLOCK, dil), :]
        if n > ATT_BLOCK:
            v = jnp.concatenate([v, cur_ref[_res_rows(r, 0, n - ATT_BLOCK, dil), :]], axis=0)
        parts.append(v.astype(dtype).reshape(n // ATT_BLOCK, ATT_BLOCK, D_HEAD_DIM))
    return parts[0] if dil == 1 else jnp.concatenate(parts, axis=0)


def _blocks_after(cur_ref, next_ref, dil, dtype=None):
    n = ATT_SUPER // dil
    parts = []
    for r in range(dil):
        v = next_ref[_res_rows(r, 0, ATT_BLOCK, dil), :]
        if n > ATT_BLOCK:
            v = jnp.concatenate([cur_ref[_res_rows(r, ATT_BLOCK, n - ATT_BLOCK, dil), :], v], axis=0)
        parts.append((v if dtype is None else v.astype(dtype)).reshape(n // ATT_BLOCK, ATT_BLOCK, D_HEAD_DIM))
    return parts[0] if dil == 1 else jnp.concatenate(parts, axis=0)


def _unblock(ref, val, dil, add=False):
    n = ATT_SUPER // dil
    nb = n // ATT_BLOCK
    for r in range(dil):
        v = val[r * nb:(r + 1) * nb].reshape(n, D_HEAD_DIM)
        if add:
            ref[_res_rows(r, 0, n, dil), :] += v
        else:
            ref[_res_rows(r, 0, n, dil), :] = v


def _att_masks(dil, edge_ok, edge_last=False):
    shape = (ATT_NBLK, ATT_BLOCK, ATT_BLOCK)
    blk = lax.broadcasted_iota(jnp.int32, shape, 0)
    row = lax.broadcasted_iota(jnp.int32, shape, 1)
    col = lax.broadcasted_iota(jnp.int32, shape, 2)
    nb = ATT_NBLK // dil
    at_edge = (blk % nb) == (nb - 1 if edge_last else 0)
    return col <= row, jnp.logical_and(col >= row, jnp.logical_or(jnp.logical_not(at_edge), edge_ok))


def _bdot_nt(a, b):
    return lax.dot_general(a, b, (((2,), (2,)), ((0,), (0,))), preferred_element_type=F32)


def _bdot_nn(a, b):
    return lax.dot_general(a, b, (((2,), (1,)), ((0,), (0,))), preferred_element_type=F32)


def _bdot_tn(a, b):
    return lax.dot_general(a, b, (((1,), (1,)), ((0,), (0,))), preferred_element_type=F32)


def _att_spec(col0, shift=0, last=None):
    def imap(hh, n):
        m = n + shift
        if shift < 0:
            m = jnp.maximum(m, 0)
        if shift > 0:
            m = jnp.minimum(m, last)
        return (m, col0 + hh)
    return pl.BlockSpec((ATT_SUPER, D_HEAD_DIM), imap)


def _att_out_spec():
    return pl.BlockSpec((ATT_SUPER, D_HEAD_DIM), lambda hh, n: (n, hh))


def attn_fwd(h):
    T = h.shape[0]
    npat = len(D_PATTERNS)

    def body(q_ref, kc_ref, kp_ref, vc_ref, vp_ref, z_ref, yd_ref, o_ref, lse_ref, *scratch):
        o_s, l_s = scratch[:npat], scratch[npat:]
        has_prev = pl.program_id(1) > 0
        for pi, (_, dil) in enumerate(D_PATTERNS):
            mask_c, mask_p = _att_masks(dil, has_prev)
            q = _blocks(q_ref, dil, BF16)
            kc, vc = _blocks(kc_ref, dil, BF16), _blocks(vc_ref, dil, BF16)
            kp, vp = _blocks_before(kc_ref, kp_ref, dil, BF16), _blocks_before(vc_ref, vp_ref, dil, BF16)
            s_c = jnp.where(mask_c, _bdot_nt(q, kc) * ATT_SCALE, -jnp.inf)
            s_p = jnp.where(mask_p, _bdot_nt(q, kp) * ATT_SCALE, -jnp.inf)
            m = jnp.maximum(jnp.max(s_c, axis=-1, keepdims=True), jnp.max(s_p, axis=-1, keepdims=True))
            p_c = jnp.exp(s_c - m)
            p_p = jnp.exp(s_p - m)
            l = jnp.sum(p_c, axis=-1, keepdims=True) + jnp.sum(p_p, axis=-1, keepdims=True)
            o = _bdot_nn((p_c / l).astype(BF16), vc) + _bdot_nn((p_p / l).astype(BF16), vp)
            _unblock(o_s[pi], o, dil)
            _unblock(l_s[pi], jnp.broadcast_to(m + jnp.log(l), o.shape), dil)
        lses = [l_s[pi][...] for pi in range(npat)]
        mx = functools.reduce(jnp.maximum, lses)
        ws = [jnp.exp(l - mx) for l in lses]
        den = functools.reduce(lambda a, b: a + b, ws)
        o = functools.reduce(lambda a, b: a + b, [(w / den) * o_s[pi][...] for pi, w in enumerate(ws)])
        o_ref[...] = o
        lse_ref[...] = mx + jnp.log(den)
        yd_ref[...] = (_silu(z_ref[...]) * o).astype(BF16)

    n_super = T // ATT_SUPER
    return pl.pallas_call(
        body, name="attn_fwd", grid=(D_HEADS, n_super),
        in_specs=[_att_spec(Q_COL), _att_spec(K_COL), _att_spec(K_COL, -1), _att_spec(V_COL), _att_spec(V_COL, -1), _att_spec(Z_COL)],
        out_specs=[_att_out_spec()] * 3,
        out_shape=[SDS((T, D_HEADS * D_HEAD_DIM), BF16), SDS((T, D_HEADS * D_HEAD_DIM), F32), SDS((T, D_HEADS * D_HEAD_DIM), F32)],
        scratch_shapes=[pltpu.VMEM((ATT_SUPER, D_HEAD_DIM), F32)] * (2 * npat),
        compiler_params=_cp("parallel", "arbitrary"),
    )(h, h, h, h, h, h)


def _dsilu(z):
    s = jax.nn.sigmoid(z)
    return s * (1.0 + z * (1.0 - s))


def attn_bwd_dq(h, o, lse, dy):
    T = h.shape[0]
    dy_col = C_WIDTH // LANES

    def body(q_ref, kc_ref, kp_ref, vc_ref, vp_ref, z_ref, o_ref, lse_ref, dy_ref, dq_ref, dz_ref, do_s, dd_s, dq_s):
        has_prev = pl.program_id(1) > 0
        z, oo, dyd = z_ref[...], o_ref[...], dy_ref[...]
        do = dyd * _silu(z)
        dz_ref[...] = (dyd * oo * _dsilu(z)).astype(BF16)
        do_s[...] = do
        dd_s[...] = jnp.broadcast_to(jnp.sum(do * oo, axis=-1, keepdims=True), (ATT_SUPER, D_HEAD_DIM))
        for pi, (_, dil) in enumerate(D_PATTERNS):
            mask_c, mask_p = _att_masks(dil, has_prev)
            q = _blocks(q_ref, dil, BF16)
            kc, vc = _blocks(kc_ref, dil, BF16), _blocks(vc_ref, dil, BF16)
            kp, vp = _blocks_before(kc_ref, kp_ref, dil, BF16), _blocks_before(vc_ref, vp_ref, dil, BF16)
            lse_b, dd_b, do_b = _blocks(lse_ref, dil), _blocks(dd_s, dil), _blocks(do_s, dil, BF16)
            p_c = jnp.where(mask_c, jnp.exp(_bdot_nt(q, kc) * ATT_SCALE - lse_b), 0.0)
            p_p = jnp.where(mask_p, jnp.exp(_bdot_nt(q, kp) * ATT_SCALE - lse_b), 0.0)
            ds_c = p_c * (_bdot_nt(do_b, vc) - dd_b) * ATT_SCALE
            ds_p = p_p * (_bdot_nt(do_b, vp) - dd_b) * ATT_SCALE
            dq = _bdot_nn(ds_c.astype(BF16), kc) + _bdot_nn(ds_p.astype(BF16), kp)
            _unblock(dq_s, dq, dil, add=pi > 0)
        dq_ref[...] = dq_s[...].astype(BF16)

    n_super = T // ATT_SUPER
    blk = (ATT_SUPER, D_HEAD_DIM)
    return pl.pallas_call(
        body, name="attn_bwd_dq", grid=(D_HEADS, n_super),
        in_specs=[_att_spec(Q_COL), _att_spec(K_COL), _att_spec(K_COL, -1), _att_spec(V_COL), _att_spec(V_COL, -1), _att_spec(Z_COL),
                  _att_spec(0), _att_spec(0), _att_spec(dy_col)],
        out_specs=[_att_out_spec()] * 2,
        out_shape=[SDS((T, D_HEADS * D_HEAD_DIM), BF16)] * 2,
        scratch_shapes=[pltpu.VMEM(blk, F32)] * 3,
        compiler_params=_cp("parallel", "arbitrary"),
    )(h, h, h, h, h, h, o, lse, dy)


def attn_bwd_dkv(h, o, lse, dy):
    T = h.shape[0]
    n_super = T // ATT_SUPER
    last = n_super - 1
    dy_col = C_WIDTH // LANES
    npat = len(D_PATTERNS)

    def body(k_ref, v_ref, qc_ref, qn_ref, zc_ref, zn_ref, oc_ref, on_ref, lc_ref, ln_ref, dyc_ref, dyn_ref, dk_ref, dv_ref,
             do_c, do_n, dd_c, dd_n, dk_s, dv_s):
        do_s, dd_s = (do_c, do_n), (dd_c, dd_n)
        has_next = pl.program_id(1) < last
        for i, (z_ref, oo_ref, dyd_ref) in enumerate(((zc_ref, oc_ref, dyc_ref), (zn_ref, on_ref, dyn_ref))):
            do = dyd_ref[...] * _silu(z_ref[...])
            do_s[i][...] = do
            dd_s[i][...] = jnp.broadcast_to(jnp.sum(do * oo_ref[...], axis=-1, keepdims=True), (ATT_SUPER, D_HEAD_DIM))
        for pi, (_, dil) in enumerate(D_PATTERNS):
            mask_c, mask_p = _att_masks(dil, has_next, edge_last=True)
            kb, vb = _blocks(k_ref, dil, BF16), _blocks(v_ref, dil, BF16)
            dk = dv = None
            for own in (True, False):
                if own:
                    q, lse_b = _blocks(qc_ref, dil, BF16), _blocks(lc_ref, dil)
                    do_b, dd_b = _blocks(do_c, dil, BF16), _blocks(dd_c, dil)
                else:
                    q, lse_b = _blocks_after(qc_ref, qn_ref, dil, BF16), _blocks_after(lc_ref, ln_ref, dil)
                    do_b, dd_b = _blocks_after(do_c, do_n, dil, BF16), _blocks_after(dd_c, dd_n, dil)
                p = jnp.where(mask_c if own else mask_p, jnp.exp(_bdot_nt(q, kb) * ATT_SCALE - lse_b), 0.0)
                ds = p * (_bdot_nt(do_b, vb) - dd_b) * ATT_SCALE
                dv_t = _bdot_tn(p.astype(BF16), do_b)
                dk_t = _bdot_tn(ds.astype(BF16), q)
                dk = dk_t if dk is None else dk + dk_t
                dv = dv_t if dv is None else dv + dv_t
            _unblock(dk_s, dk, dil, add=pi > 0)
            _unblock(dv_s, dv, dil, add=pi > 0)
        dk_ref[...] = dk_s[...].astype(BF16)
        dv_ref[...] = dv_s[...].astype(BF16)

    blk = (ATT_SUPER, D_HEAD_DIM)

    def pair(col0):
        return [_att_spec(col0), _att_spec(col0, 1, last)]

    return pl.pallas_call(
        body, name="attn_bwd_dkv", grid=(D_HEADS, n_super),
        in_specs=[_att_spec(K_COL), _att_spec(V_COL)] + pair(Q_COL) + pair(Z_COL) + pair(0) + pair(0) + pair(dy_col),
        out_specs=[_att_out_spec()] * 2,
        out_shape=[SDS((T, D_HEADS * D_HEAD_DIM), BF16)] * 2,
        scratch_shapes=[pltpu.VMEM(blk, F32)] * 6,
        compiler_params=_cp("parallel", "arbitrary"),
    )(h, h, h, h, h, h, o, o, lse, lse, dy, dy)


ANY = pl.BlockSpec(memory_space=pl.ANY)
COMM_PARAMS = pltpu.CompilerParams()


def _place():
    x, y, c = lax.axis_index("x"), lax.axis_index("y"), lax.axis_index("c")
    return x, y, c, [(1 - x, y), (x, 1 - y), (1 - x, 1 - y)]


def _rcopy(src, dst, ssem, rsem, dev):
    return pltpu.make_async_remote_copy(src_ref=src, dst_ref=dst, send_sem=ssem, recv_sem=rsem, device_id=dev, device_id_type=MESH)


def gather_shards(arrs):
    n = len(arrs)
    per = 7

    def body(*refs):
        ins, outs = refs[:n], refs[n:2 * n]
        ssem, rsem = refs[2 * n:]
        x, y, c, chips = _place()
        me, sib = 2 * x + y, (x, y, 1 - c)
        sent = []
        for a in range(n):
            for j, (px, py) in enumerate(chips):
                cp = _rcopy(ins[a].at[c], outs[a].at[me, c], ssem.at[per * a + j], rsem.at[per * a + j], (px, py, c))
                cp.start()
                sent.append(cp)
        own = [_rcopy(ins[a], outs[a].at[me], ssem.at[per * a + 6], rsem.at[per * a + 6], sib) for a in range(n)]
        for cp in own:
            cp.start()
        for a in range(n):
            for j, (px, py) in enumerate(chips):
                slot = outs[a].at[2 * px + py, c]
                _rcopy(slot, slot, ssem.at[per * a + j], rsem.at[per * a + j], (px, py, c)).wait_recv()
                cp = _rcopy(slot, slot, ssem.at[per * a + 3 + j], rsem.at[per * a + 3 + j], sib)
                cp.start()
                sent.append(cp)
        for a in range(n):
            for j, (px, py) in enumerate(chips):
                slot = outs[a].at[2 * px + py, 1 - c]
                _rcopy(slot, slot, ssem.at[per * a + 3 + j], rsem.at[per * a + 3 + j], sib).wait_recv()
        for cp in sent:
            cp.wait_send()
        for cp in own:
            cp.wait()

    return pl.pallas_call(
        body, name="comm_gather_shards", in_specs=[ANY] * n, out_specs=[ANY] * n,
        out_shape=[SDS((N_CHIPS,) + a.shape, a.dtype) for a in arrs],
        scratch_shapes=[pltpu.SemaphoreType.DMA((per * n,)), pltpu.SemaphoreType.DMA((per * n,))],
        compiler_params=COMM_PARAMS,
    )(*arrs)


def swap_sibling_halves(arrs):
    n = len(arrs)

    def body(*refs):
        ins, outs = refs[:n], refs[n:2 * n]
        ssem, rsem = refs[2 * n:]
        x, y, c, _ = _place()
        cps = [_rcopy(ins[a].at[1 - c], outs[a], ssem.at[a], rsem.at[a], (x, y, 1 - c)) for a in range(n)]
        for cp in cps:
            cp.start()
        for cp in cps:
            cp.wait()

    return pl.pallas_call(
        body, name="comm_swap_sibling", in_specs=[ANY] * n, out_specs=[ANY] * n,
        out_shape=[SDS(a.shape[1:], a.dtype) for a in arrs],
        scratch_shapes=[pltpu.SemaphoreType.DMA((n,)), pltpu.SemaphoreType.DMA((n,))], compiler_params=COMM_PARAMS,
    )(*arrs)


def scatter_to_chips(arrs):
    n = len(arrs)

    def body(*refs):
        ins, outs = refs[:n], refs[n:2 * n]
        ssem, rsem = refs[2 * n:]
        x, y, c, chips = _place()
        cps = [_rcopy(ins[a].at[2 * px + py], outs[a].at[j], ssem.at[3 * a + j], rsem.at[3 * a + j], (px, py, c))
               for a in range(n) for j, (px, py) in enumerate(chips)]
        for cp in cps:
            cp.start()
        for cp in cps:
            cp.wait()

    return pl.pallas_call(
        body, name="comm_scatter_chips", in_specs=[ANY] * n, out_specs=[ANY] * n,
        out_shape=[SDS((N_CHIPS - 1,) + a.shape[1:], a.dtype) for a in arrs],
        scratch_shapes=[pltpu.SemaphoreType.DMA((3 * n,)), pltpu.SemaphoreType.DMA((3 * n,))],
        compiler_params=COMM_PARAMS,
    )(*arrs)


def swap_with_sibling(arrs):
    n = len(arrs)

    def body(*refs):
        ins, outs = refs[:n], refs[n:2 * n]
        ssem, rsem = refs[2 * n:]
        x, y, c, _ = _place()
        cps = [_rcopy(ins[a], outs[a], ssem.at[a], rsem.at[a], (x, y, 1 - c)) for a in range(n)]
        for cp in cps:
            cp.start()
        for cp in cps:
            cp.wait()

    return pl.pallas_call(
        body, name="comm_swap_totals", in_specs=[ANY] * n, out_specs=[ANY] * n,
        out_shape=[SDS(a.shape, a.dtype) for a in arrs],
        scratch_shapes=[pltpu.SemaphoreType.DMA((n,)), pltpu.SemaphoreType.DMA((n,))], compiler_params=COMM_PARAMS,
    )(*arrs)


def gather_all(buf):
    def body(in_ref, out_ref, ssem, rsem, lsem):
        x, y, c, _ = _place()
        me = 4 * x + 2 * y + c
        local = pltpu.make_async_copy(in_ref, out_ref.at[me], lsem)
        local.start()
        flips = [(a, b, e) for a in (0, 1) for b in (0, 1) for e in (0, 1)][1:]
        cps = []
        for i, (a, b, e) in enumerate(flips):
            peer = (x ^ a, y ^ b, c ^ e)
            cps.append(_rcopy(in_ref, out_ref.at[me], ssem.at[i], rsem.at[i], peer))
        for cp in cps:
            cp.start()
        for i, (a, b, e) in enumerate(flips):
            cps[i].wait_send()
            slot = out_ref.at[4 * (x ^ a) + 2 * (y ^ b) + (c ^ e)]
            _rcopy(slot, slot, ssem.at[i], rsem.at[i], (x ^ a, y ^ b, c ^ e)).wait_recv()
        local.wait()

    return pl.pallas_call(
        body, name="comm_gather_all", in_specs=[ANY], out_specs=ANY, out_shape=SDS((N_DEV,) + buf.shape, buf.dtype),
        scratch_shapes=[pltpu.SemaphoreType.DMA((N_DEV - 1,)), pltpu.SemaphoreType.DMA((N_DEV - 1,)), pltpu.SemaphoreType.DMA],
        compiler_params=COMM_PARAMS,
    )(buf)


def _pack_offsets(parts):
    offs, r = [], 0
    for p in parts:
        offs.append(r)
        r += -(-p.shape[0] // SUBLANES) * SUBLANES
    return offs, r


def pack_rows(parts):
    offs, total = _pack_offsets(parts)

    def body(*refs):
        out = refs[-1]
        out[...] = jnp.zeros_like(out)
        for ref, off in zip(refs[:-1], offs):
            out[off:off + ref.shape[0], :] = ref[...]

    vmem = pl.BlockSpec(memory_space=pltpu.VMEM)
    return pl.pallas_call(body, name="pack_small", in_specs=[vmem] * len(parts), out_specs=vmem,
                          out_shape=SDS((total, LANES), F32))(*parts)


def _unpack(buf, parts):
    offs, _ = _pack_offsets(parts)
    return [buf[off:off + p.shape[0]] for p, off in zip(parts, offs)]


def _pad_lanes(v):
    v = v.reshape(1, -1)
    return jnp.pad(v, ((0, 0), (0, LANES - v.shape[1])))


def _as2d(a):
    return a.reshape(1, -1) if a.ndim == 1 else a.reshape(-1, a.shape[-1])


def kernel(x, even_norm_g, even_w_in, gmlp_ln_g, gmlp_ln_b, gmlp_ws, gmlp_bs, ssd_conv_w, ssd_conv_b, ssd_dt_bias, ssd_a_log, ssd_d, ssd_norm_g, even_w_out, odd_norm_g, odd_w_in, sconv_w, odd_w_out, final_norm_g, loss_target, m_even_norm_g, m_even_w_in, m_gmlp_ln_g, m_gmlp_ln_b, m_gmlp_ws, m_gmlp_bs, m_ssd_conv_w, m_ssd_conv_b, m_ssd_dt_bias, m_ssd_a_log, m_ssd_d, m_ssd_norm_g, m_even_w_out, m_odd_norm_g, m_odd_w_in, m_sconv_w, m_odd_w_out, m_final_norm_g, v_even_norm_g, v_even_w_in, v_gmlp_ln_g, v_gmlp_ln_b, v_gmlp_ws, v_gmlp_bs, v_ssd_conv_w, v_ssd_conv_b, v_ssd_dt_bias, v_ssd_a_log, v_ssd_d, v_ssd_norm_g, v_even_w_out, v_odd_norm_g, v_odd_w_in, v_sconv_w, v_odd_w_out, v_final_norm_g):
    weights = dict(even_norm_g=even_norm_g, even_w_in=even_w_in, gmlp_ln_g=gmlp_ln_g, gmlp_ln_b=gmlp_ln_b, gmlp_ws=gmlp_ws, gmlp_bs=gmlp_bs, ssd_conv_w=ssd_conv_w, ssd_conv_b=ssd_conv_b, ssd_dt_bias=ssd_dt_bias, ssd_a_log=ssd_a_log, ssd_d=ssd_d, ssd_norm_g=ssd_norm_g, even_w_out=even_w_out, odd_norm_g=odd_norm_g, odd_w_in=odd_w_in, sconv_w=sconv_w, odd_w_out=odd_w_out, final_norm_g=final_norm_g)
    moms_m = dict(even_norm_g=m_even_norm_g, even_w_in=m_even_w_in, gmlp_ln_g=m_gmlp_ln_g, gmlp_ln_b=m_gmlp_ln_b, gmlp_ws=m_gmlp_ws, gmlp_bs=m_gmlp_bs, ssd_conv_w=m_ssd_conv_w, ssd_conv_b=m_ssd_conv_b, ssd_dt_bias=m_ssd_dt_bias, ssd_a_log=m_ssd_a_log, ssd_d=m_ssd_d, ssd_norm_g=m_ssd_norm_g, even_w_out=m_even_w_out, odd_norm_g=m_odd_norm_g, odd_w_in=m_odd_w_in, sconv_w=m_sconv_w, odd_w_out=m_odd_w_out, final_norm_g=m_final_norm_g)
    moms_v = dict(even_norm_g=v_even_norm_g, even_w_in=v_even_w_in, gmlp_ln_g=v_gmlp_ln_g, gmlp_ln_b=v_gmlp_ln_b, gmlp_ws=v_gmlp_ws, gmlp_bs=v_gmlp_bs, ssd_conv_w=v_ssd_conv_w, ssd_conv_b=v_ssd_conv_b, ssd_dt_bias=v_ssd_dt_bias, ssd_a_log=v_ssd_a_log, ssd_d=v_ssd_d, ssd_norm_g=v_ssd_norm_g, even_w_out=v_even_w_out, odd_norm_g=v_odd_norm_g, odd_w_in=v_odd_w_in, sconv_w=v_sconv_w, odd_w_out=v_odd_w_out, final_norm_g=v_final_norm_g)
    names = list(weights)

    xs = x[0]
    tgt = loss_target[0]
    T = xs.shape[0]
    chip = 2 * lax.axis_index("x") + lax.axis_index("y")
    core = lax.axis_index("c")
    cshard = B_XBC // N_CHIPS
    dshard = D_MODEL // N_CHIPS

    def halves(w):
        return w.astype(BF16).reshape(2, w.shape[0] // 2, w.shape[1])

    small_shard = jnp.concatenate([ssd_conv_w[0].reshape(-1), odd_norm_g[0], sconv_w[0].reshape(-1)])
    g_wie, g_woe, g_wio, g_woo, g_small = gather_shards(
        [halves(even_w_in[0]), halves(even_w_out[0]), halves(odd_w_in[0]), halves(odd_w_out[0]), small_shard.reshape(2, -1, LANES)])
    wie = g_wie.reshape(N_CHIPS, D_MODEL, IN_EVEN // N_CHIPS).transpose(1, 0, 2).reshape(D_MODEL, IN_EVEN)
    wie_main = wie[:, :EVEN_MAIN]
    wie_dt = jnp.pad(wie[:, EVEN_MAIN:], ((0, 0), (0, LANES - B_HEADS)))
    woe = g_woe.reshape(2 * A_WIDTH, D_MODEL)
    wio = g_wio.reshape(N_CHIPS, D_MODEL, IN_ODD // N_CHIPS)
    woo = g_woo.reshape(2 * C_WIDTH, D_MODEL)
    g_small = g_small.reshape(N_CHIPS, -1)
    n_cw = B_CONV * cshard
    conv_w = g_small[:, :n_cw].reshape(N_CHIPS, B_CONV, cshard).transpose(1, 0, 2).reshape(B_CONV, B_XBC)
    odd_g = g_small[:, n_cw:n_cw + dshard].reshape(1, D_MODEL)
    sconv = g_small[:, n_cw + dshard:].reshape(N_CHIPS, C_CONV, dshard).transpose(1, 0, 2).reshape(C_CONV, C_WIDTH)

    even_g = even_norm_g
    ln_g, ln_b = gmlp_ln_g, gmlp_ln_b
    ws, bs_t = gmlp_ws[0], gmlp_bs[0].T
    conv_b = ssd_conv_b
    dt_bias, a_log, d_skip = _pad_lanes(ssd_dt_bias), _pad_lanes(ssd_a_log), _pad_lanes(ssd_d)
    norm_g = ssd_norm_g
    fin_g = final_norm_g.reshape(1, D_MODEL)

    xn0 = rmsnorm_fwd(xs, even_g, name="even_norm")
    h0 = matmul(xn0, wie_main, "nn", name="even_in")
    dtr = matmul(xn0, wie_dt, "nn", name="even_in_dt", tk=D_MODEL)
    ya = gmlp_fwd(h0, ln_g, ln_b, ws, bs_t)
    xbc = ssd_conv_fwd(h0, conv_w, conv_b)
    yb, states = ssd_fwd(xbc, dtr, h0, dt_bias, a_log, d_skip, norm_g)
    y0 = jnp.concatenate([ya, yb], axis=1)
    x1 = matmul(y0, woe, "nn", name="even_out", res=xs)

    xn1 = rmsnorm_fwd(x1, odd_g, name="odd_norm")
    h1 = matmul(xn1, wio, "nn", name="odd_in")
    yc = sconv_fwd(h1, sconv)
    yd, att_o, att_lse = attn_fwd(h1)
    y1 = jnp.concatenate([yc, yd], axis=1)
    x2 = matmul(y1, woo, "nn", name="odd_out", res=x1)

    loss_part, dx2, dx2b, d_fin_g = loss_head(x2, fin_g, tgt)

    dy1 = matmul(dx2b, woo, "nt", name="odd_out_dy")
    d_woo = matmul(y1, dx2b, "tn", name="odd_out_dw")
    dbg, dcg, dhx, dzc, d_sconv = sconv_bwd(h1, dy1, sconv)
    dq, dzd = attn_bwd_dq(h1, att_o, att_lse, dy1)
    dk, dv = attn_bwd_dkv(h1, att_o, att_lse, dy1)
    dh1 = jnp.concatenate([dbg, dcg, dhx, dzc, dq, dk, dv, dzd], axis=1)
    dxn1 = matmul(dh1, wio, "nt", name="odd_in_dx")
    d_wio = matmul(xn1, dh1, "tn", name="odd_in_dw")
    dx1, dx1b, d_odd_g = rmsnorm_bwd(x1, odd_g, dxn1, dx2, name="odd_norm_bwd")

    dy0 = matmul(dx1b, woe, "nt", name="even_out_dy")
    d_woe = matmul(y0, dx1b, "tn", name="even_out_dw")
    duvz, d_ln_g, d_ln_b, d_ws, d_bs_t = gmlp_bwd(h0, dy0, ln_g, ln_b, ws, bs_t)
    dxbc_act, ddtr, dzb, d_dt_bias, d_a_log, d_d, d_norm_g = ssd_bwd(xbc, dtr, h0, states, dy0, dt_bias, a_log, d_skip, norm_g)
    dxbc, d_conv_w, d_conv_b = ssd_conv_bwd(h0, dxbc_act, conv_w, conv_b)
    dh0 = jnp.concatenate([duvz, dzb, dxbc], axis=1)
    ddtr_b = ddtr.astype(BF16)
    dxn0 = matmul(ddtr_b, wie_dt, "nt", name="even_in_dx_dt")
    dxn0 = matmul(dh0, wie_main, "nt", name="even_in_dx", res=dxn0)
    d_wie_main = matmul(xn0, dh0, "tn", name="even_in_dw")
    d_wie_dt = matmul(xn0, ddtr_b, "tn", name="even_in_dw_dt")
    grad_x, _, d_even_g = rmsnorm_bwd(xs, even_g, dxn0, dx1, name="even_norm_bwd")
    d_wie = jnp.concatenate([d_wie_main, d_wie_dt[:, :B_HEADS]], axis=1)

    def by_cols(g):
        r, n = g.shape
        return g.reshape(2, r // 2, N_CHIPS, n // N_CHIPS).transpose(0, 2, 1, 3)

    def by_rows(g):
        r, n = g.shape
        return g.reshape(N_CHIPS, 2, r // (2 * N_CHIPS), n).transpose(1, 0, 2, 3)

    pieces = [by_cols(d_wie), by_rows(d_woe), by_cols(d_wio), by_rows(d_woo)]
    from_sib = swap_sibling_halves(pieces)
    chip_sums = [chip_sum(p, s, core, name=f"chip_sum_{i}") for i, (p, s) in enumerate(zip(pieces, from_sib))]
    landed = scatter_to_chips(chip_sums)
    totals = [total_sum(s, l, chip, name=f"total_{i}") for i, (s, l) in enumerate(zip(chip_sums, landed))]
    from_sib = swap_with_sibling(totals)
    joined = [jnp.where(core == 0, jnp.stack([t, s]), jnp.stack([s, t])) for t, s in zip(totals, from_sib)]
    big = dict(even_w_in=joined[0].reshape(even_w_in.shape), even_w_out=joined[1].reshape(even_w_out.shape),
               odd_w_in=joined[2].reshape(odd_w_in.shape), odd_w_out=joined[3].reshape(odd_w_out.shape))

    small_names = ["even_norm_g", "gmlp_ln_g", "gmlp_ln_b", "gmlp_ws", "gmlp_bs", "ssd_conv_w", "ssd_conv_b", "ssd_dt_bias",
                   "ssd_a_log", "ssd_d", "ssd_norm_g", "odd_norm_g", "sconv_w", "final_norm_g"]
    small_parts = [d_even_g, d_ln_g, d_ln_b, d_ws, d_bs_t.T, d_conv_w, d_conv_b, d_dt_bias, d_a_log, d_d, d_norm_g, d_odd_g, d_sconv, d_fin_g]
    small_shapes = [p.shape for p in small_parts]
    small_rows = [p.reshape(-1, LANES) for p in small_parts]
    small_sum = sum_leading(gather_all(pack_rows(small_rows)), name="small_sum")
    full = {nm: rows.reshape(shape) for nm, rows, shape in zip(small_names, _unpack(small_sum, small_rows), small_shapes)}
    grads = dict(big)
    for nm in small_names:
        g = full[nm]
        if nm in ("ssd_dt_bias", "ssd_a_log", "ssd_d"):
            g = g[:, :B_HEADS]
        elif nm == "ssd_conv_w":
            g = lax.dynamic_slice_in_dim(g, chip * cshard, cshard, axis=1)
        elif nm in ("odd_norm_g", "sconv_w"):
            g = lax.dynamic_slice_in_dim(g, chip * dshard, dshard, axis=1)
        grads[nm] = g.reshape(weights[nm].shape)

    deltas, new_m, new_v = {}, {}, {}
    for nm in names:
        w = weights[nm]
        d, nm_, nv_ = adamw(_as2d(w), _as2d(grads[nm]), _as2d(moms_m[nm]), _as2d(moms_v[nm]), name=f"adamw_{nm}")
        deltas[nm], new_m[nm], new_v[nm] = d.reshape(w.shape), nm_.reshape(w.shape), nv_.reshape(w.shape)

    loss = lax.psum(loss_part[0, 0], ("x", "y", "c"))
    return (loss, grad_x[None], *[grads[n] for n in names], *[deltas[n] for n in names],
            *[new_m[n] for n in names], *[new_v[n] for n in names])
```

```python
import functools

import jax
import jax.numpy as jnp
from jax import lax
from jax.experimental import pallas as pl
from jax.experimental.pallas import tpu as pltpu

F32 = jnp.float32
BF16 = jnp.bfloat16
SDS = jax.ShapeDtypeStruct
MESH = pl.DeviceIdType.MESH

D_MODEL = 2048
A_WIDTH = 2048
A_GROUPS = 8
CHUNK = 128
B_WIDTH = 2048
B_HEADS = 32
B_HEAD_DIM = 64
B_GROUPS = 8
B_STATE = 128
B_CONV = 4
B_XBC = B_WIDTH + 2 * B_GROUPS * B_STATE
C_WIDTH = 2048
C_CONV = 3
D_HEADS = 16
D_HEAD_DIM = 128
D_PATTERNS = ((128, 1), (512, 4), (2048, 16))
ATT_BLOCK = 128
ATT_SUPER = 2048
EVEN_MAIN = 3 * A_WIDTH + B_WIDTH + B_XBC
IN_EVEN = EVEN_MAIN + B_HEADS
IN_ODD = 4 * C_WIDTH + 4 * D_HEADS * D_HEAD_DIM
LANES = 128
SUBLANES = 8
EPS = 1e-5
ADAM_LR = 0.001
ADAM_B1 = 0.9
ADAM_B2 = 0.999
ADAM_EPS = 1e-08
ADAM_WD = 0.01
ADAM_STEP = 10
N_CHIPS = 4
N_DEV = 8
VMEM_LIMIT_BYTES = 56 * 1024 * 1024


def _cp(*sem):
    return pltpu.CompilerParams(dimension_semantics=sem, vmem_limit_bytes=VMEM_LIMIT_BYTES)


def _full(shape):
    return pl.BlockSpec(shape, lambda *_: (0,) * len(shape))


def _silu(x):
    return x * jax.nn.sigmoid(x)


def _dot_nn(a, b):
    return lax.dot_general(a, b, (((1,), (0,)), ((), ())), preferred_element_type=F32)


def _dot_nt(a, b):
    return lax.dot_general(a, b, (((1,), (1,)), ((), ())), preferred_element_type=F32)


def _dot_tn(a, b):
    return lax.dot_general(a, b, (((0,), (0,)), ((), ())), preferred_element_type=F32)


def _tril(n):
    return lax.broadcasted_iota(jnp.int32, (n, n), 0) >= lax.broadcasted_iota(jnp.int32, (n, n), 1)


_DOTS = {"nn": _dot_nn, "nt": _dot_nt, "tn": _dot_tn}


class Rider:
    def __init__(self, ins, out_shapes, n_sems, phases):
        self.ins, self.out_shapes, self.n_sems, self.phases = list(ins), list(out_shapes), n_sems, list(phases)


def _call(body, *, name, grid, in_specs, out_specs, out_shape, scratch_shapes, sem, args, rider=None):
    in_specs, out_specs, out_shape, scratch_shapes = list(in_specs), list(out_specs), list(out_shape), list(scratch_shapes)
    if rider is None:
        res = pl.pallas_call(body, name=name, grid=grid, in_specs=in_specs, out_specs=out_specs, out_shape=out_shape,
                             scratch_shapes=scratch_shapes, compiler_params=_cp(*sem))(*args)
        return list(res), []
    counts = [len(in_specs), len(rider.ins), len(out_specs), len(rider.out_shapes), len(scratch_shapes), 2]
    total = 1
    for g in grid:
        total *= g

    def wrapped(*refs):
        groups, pos = [], 0
        for n in counts:
            groups.append(refs[pos:pos + n])
            pos += n
        ins, rins, outs, routs, scr, (ssem, rsem) = groups
        step = 0
        for d, g in enumerate(grid):
            step = step * g + pl.program_id(d)
        for frac, fn in rider.phases:
            @pl.when(step == min(int(frac * total), total - 1))
            def _(fn=fn):
                fn(rins, routs, ssem, rsem)
        body(*ins, *outs, *scr)

    dma = pltpu.SemaphoreType.DMA((rider.n_sems,))
    res = pl.pallas_call(
        wrapped, name=name, grid=grid, in_specs=in_specs + [ANY] * len(rider.ins), out_specs=out_specs + [ANY] * len(rider.out_shapes),
        out_shape=out_shape + rider.out_shapes, scratch_shapes=scratch_shapes + [dma, dma],
        compiler_params=_cp(*(("arbitrary",) * len(grid))))(*args, *rider.ins)
    return list(res[:len(out_specs)]), list(res[len(out_specs):])


def run_rider(rider, *, name):
    def body(*refs):
        n_in, n_out = len(rider.ins), len(rider.out_shapes)
        ins, outs, (ssem, rsem) = refs[:n_in], refs[n_in:n_in + n_out], refs[n_in + n_out:]
        for _, fn in rider.phases:
            fn(ins, outs, ssem, rsem)

    dma = pltpu.SemaphoreType.DMA((rider.n_sems,))
    return list(pl.pallas_call(body, name=name, in_specs=[ANY] * len(rider.ins), out_specs=[ANY] * len(rider.out_shapes),
                               out_shape=rider.out_shapes, scratch_shapes=[dma, dma])(*rider.ins))


def matmul(a, b, mode, *, name, out_dtype=F32, res=None, tm=1024, tn=1024, tk=2048, out_layout=None, rider=None):
    shards = b.shape[0] if b.ndim == 3 else 1
    b_rows, b_cols = b.shape[-2], b.shape[-1] * shards
    if mode == "tn":
        (K, M), (K2, N) = a.shape, (b_rows, b_cols)
    elif mode == "nt":
        (M, K), (N, K2) = a.shape, (b_rows, b_cols)
    else:
        (M, K), (K2, N) = a.shape, (b_rows, b_cols)
    assert K == K2, (a.shape, b.shape, mode)
    tm, tn, tk = min(tm, M), min(tn, N), min(tk, K)
    if shards > 1:
        assert mode != "tn" and b.shape[-1] % (tk if mode == "nt" else tn) == 0
    assert M % tm == 0 and N % tn == 0 and K % tk == 0, (M, N, K, tm, tn, tk)
    nk = K // tk
    dot = _DOTS[mode]

    def body(*refs):
        a_ref, b_ref = refs[:2]
        r_ref = refs[2] if res is not None else None
        o_ref = refs[3] if res is not None else refs[2]

        def product():
            bv = b_ref[0] if shards > 1 else b_ref[...]
            return dot(a_ref[...], bv)

        def finish(acc):
            if res is not None:
                acc = acc + r_ref[...]
            o_ref[...] = acc.astype(o_ref.dtype).reshape(o_ref.shape)

        if nk == 1:
            finish(product())
            return
        acc_ref = refs[-1]
        k = pl.program_id(2)

        @pl.when(k == 0)
        def _():
            acc_ref[...] = product()

        @pl.when(jnp.logical_and(k > 0, k < nk - 1))
        def _():
            acc_ref[...] += product()

        @pl.when(k == nk - 1)
        def _():
            finish(acc_ref[...] + product())

    a_spec = pl.BlockSpec((tk, tm), lambda i, j, k: (k, i)) if mode == "tn" else pl.BlockSpec((tm, tk), lambda i, j, k: (i, k))
    if shards > 1 and mode == "nn":
        per = b.shape[-1] // tn
        b_spec = pl.BlockSpec((1, tk, tn), lambda i, j, k: (j // per, k, j % per))
    elif shards > 1:
        per = b.shape[-1] // tk
        b_spec = pl.BlockSpec((1, tn, tk), lambda i, j, k: (k // per, j, k % per))
    elif mode == "nt":
        b_spec = pl.BlockSpec((tn, tk), lambda i, j, k: (j, k))
    else:
        b_spec = pl.BlockSpec((tk, tn), lambda i, j, k: (k, j))
    o_spec = pl.BlockSpec((tm, tn), lambda i, j, k: (i, j))
    in_specs, args = [a_spec, b_spec], [a, b]
    if res is not None:
        in_specs.append(o_spec)
        args.append(res)
    out_shape = SDS((M, N), out_dtype)
    if out_layout is not None:
        out_shape, o_spec = SDS(out_layout[0], out_dtype), pl.BlockSpec(out_layout[1], out_layout[2])
    outs, rider_outs = _call(
        body, name=name, grid=(M // tm, N // tn, nk), in_specs=in_specs, out_specs=[o_spec], out_shape=[out_shape],
        scratch_shapes=[pltpu.VMEM((tm, tn), F32)] if nk > 1 else [], sem=("parallel", "parallel", "arbitrary"),
        args=args, rider=rider)
    return outs[0] if rider is None else (outs[0], rider_outs)


ROW_TILE = 512
ROW_PIECE = 512


def _rms(x, g):
    return x * lax.rsqrt(jnp.mean(x * x, axis=-1, keepdims=True) + EPS) * g


def rmsnorm_fwd(x, g, *, name):
    T, D = x.shape

    def body(x_ref, g_ref, o_ref):
        o_ref[...] = _rms(x_ref[...], g_ref[...]).astype(BF16)

    row = pl.BlockSpec((ROW_TILE, D), lambda i: (i, 0))
    return pl.pallas_call(body, name=name, grid=(T // ROW_TILE,), in_specs=[row, _full((1, D))], out_specs=row,
                          out_shape=SDS((T, D), BF16), compiler_params=_cp("parallel"))(x, g)


def rmsnorm_bwd(x, g, dxn, dres, *, name):
    T, D = x.shape

    def body(x_ref, g_ref, dxn_ref, dres_ref, dx_ref, dxb_ref, dg_ref):
        _, vjp = jax.vjp(_rms, x_ref[...], g_ref[...])
        dx, dg = vjp(dxn_ref[...])
        dx = dx + dres_ref[...]
        dx_ref[...] = dx
        dxb_ref[...] = dx.astype(BF16)

        @pl.when(pl.program_id(0) == 0)
        def _():
            dg_ref[...] = jnp.zeros_like(dg_ref)

        dg_ref[...] += dg

    row = pl.BlockSpec((ROW_TILE, D), lambda i: (i, 0))
    return pl.pallas_call(
        body, name=name, grid=(T // ROW_TILE,), in_specs=[row, _full((1, D)), row, row],
        out_specs=[row, row, _full((1, D))], out_shape=[SDS((T, D), F32), SDS((T, D), BF16), SDS((1, D), F32)],
        compiler_params=_cp("arbitrary"))(x, g, dxn, dres)


def _loss_tile(x, g, tgt):
    err = jnp.square(_rms(x, g) - tgt)
    return 0.5 * jnp.sum(jnp.mean(err, axis=-1))


def loss_head(x, g, tgt):
    T, D = x.shape

    def body(x_ref, g_ref, t_ref, loss_ref, dx_ref, dxb_ref, dg_ref):
        loss, vjp = jax.vjp(_loss_tile, x_ref[...], g_ref[...], t_ref[...])
        dx, dg, _ = vjp(jnp.ones((), F32))
        dx_ref[...] = dx
        dxb_ref[...] = dx.astype(BF16)

        @pl.when(pl.program_id(0) == 0)
        def _():
            dg_ref[...] = jnp.zeros_like(dg_ref)
            loss_ref[...] = jnp.zeros_like(loss_ref)

        dg_ref[...] += dg
        loss_ref[...] += jnp.reshape(loss, (1, 1))

    row = pl.BlockSpec((ROW_TILE, D), lambda i: (i, 0))
    return pl.pallas_call(
        body, name="loss_head", grid=(T // ROW_TILE,), in_specs=[row, _full((1, D)), row],
        out_specs=[_full((1, 1)), row, row, _full((1, D))],
        out_shape=[SDS((1, 1), F32), SDS((T, D), F32), SDS((T, D), BF16), SDS((1, D), F32)],
        compiler_params=_cp("arbitrary"))(x, g, tgt)


TILE_BYTES = 1 << 20


def _row_tile(rows, row_bytes):
    for cand in (512, 256, 128, 64, 32, 16, 8):
        if rows % cand == 0 and cand * row_bytes <= TILE_BYTES:
            return cand
    return rows


def adamw(w, g, m, v, *, name):
    R, C = w.shape
    tr = _row_tile(R, C * 4)

    def body(w_ref, g_ref, m_ref, v_ref, d_ref, nm_ref, nv_ref):
        gg = g_ref[...]
        mm = ADAM_B1 * m_ref[...] + (1.0 - ADAM_B1) * gg
        vv = ADAM_B2 * v_ref[...] + (1.0 - ADAM_B2) * jnp.square(gg)
        m_hat = mm / (1.0 - ADAM_B1 ** ADAM_STEP)
        v_hat = vv / (1.0 - ADAM_B2 ** ADAM_STEP)
        d_ref[...] = -ADAM_LR * (m_hat / (jnp.sqrt(v_hat) + ADAM_EPS) + ADAM_WD * w_ref[...])
        nm_ref[...] = mm
        nv_ref[...] = vv

    blk = pl.BlockSpec((tr, C), lambda i: (i, 0))
    return pl.pallas_call(body, name=name, grid=(R // tr,), in_specs=[blk] * 4, out_specs=[blk] * 3,
                          out_shape=[SDS((R, C), F32)] * 3, compiler_params=_cp("parallel"))(w, g, m, v)


def sum_leading(a, *, name):
    n, R, C = a.shape
    tr = _row_tile(R, n * C * 4)

    def body(a_ref, o_ref):
        acc = a_ref[0]
        for j in range(1, n):
            acc = acc + a_ref[j]
        o_ref[...] = acc

    return pl.pallas_call(body, name=name, grid=(R // tr,), in_specs=[pl.BlockSpec((n, tr, C), lambda i: (0, i, 0))],
                          out_specs=pl.BlockSpec((tr, C), lambda i: (i, 0)), out_shape=SDS((R, C), F32),
                          compiler_params=_cp("parallel"))(a)


def chip_sum(pieces, from_sibling, core, *, name):
    _, n, R, C = pieces.shape

    def body(c_ref, a_ref, b_ref, o_ref):
        o_ref[...] = (a_ref[0] + b_ref[...]).astype(BF16)

    tr = _row_tile(R, C * 4)
    blk = pl.BlockSpec((1, tr, C), lambda k, i, c_ref: (k, i, 0))
    mine = pl.BlockSpec((1, 1, tr, C), lambda k, i, c_ref: (c_ref[0], k, i, 0))
    return pl.pallas_call(
        body, name=name, out_shape=SDS((n, R, C), BF16),
        grid_spec=pltpu.PrefetchScalarGridSpec(num_scalar_prefetch=1, grid=(n, R // tr), in_specs=[mine, blk], out_specs=blk),
        compiler_params=_cp("parallel", "parallel"))(core.reshape(1), pieces, from_sibling)


def total_sum(sums, landed, chip, *, name):
    n, R, C = landed.shape

    def body(k_ref, s_ref, l_ref, o_ref):
        acc = s_ref[0].astype(F32)
        for j in range(n):
            acc = acc + l_ref[j].astype(F32)
        o_ref[...] = acc

    tr = _row_tile(R, n * C * 2)
    return pl.pallas_call(
        body, name=name, out_shape=SDS((R, C), F32),
        grid_spec=pltpu.PrefetchScalarGridSpec(
            num_scalar_prefetch=1, grid=(R // tr,),
            in_specs=[pl.BlockSpec((1, tr, C), lambda i, k_ref: (k_ref[0], i, 0)), pl.BlockSpec((n, tr, C), lambda i, k_ref: (0, i, 0))],
            out_specs=pl.BlockSpec((tr, C), lambda i, k_ref: (i, 0))),
        compiler_params=_cp("parallel"))(chip.reshape(1), sums, landed)


def _gmlp_chunk(u, v, z, ln_g, ln_b, wsc, bs_t):
    mu = jnp.mean(v, axis=-1, keepdims=True)
    xc = v - mu
    vn = xc * lax.rsqrt(jnp.mean(xc * xc, axis=-1, keepdims=True) + EPS) * ln_g + ln_b
    gw = A_WIDTH // A_GROUPS
    outs = []
    for g in range(A_GROUPS):
        m = _dot_nn(wsc[g].astype(BF16), vn[:, g * gw:(g + 1) * gw].astype(BF16))
        outs.append(m + bs_t[:, g:g + 1])
    return _silu(z) * (u * jnp.concatenate(outs, axis=1))


def _h_cols(width, idx, rows=CHUNK):
    return pl.BlockSpec((rows, width), lambda i: (i, idx))


def gmlp_fwd(h, ln_g, ln_b, ws, bs_t):
    T = h.shape[0]

    def body(u_ref, v_ref, z_ref, g_ref, b_ref, ws_ref, bs_ref, o_ref):
        wsc = jnp.where(_tril(CHUNK)[None], ws_ref[...], 0.0)
        o_ref[...] = _gmlp_chunk(u_ref[...], v_ref[...], z_ref[...], g_ref[...], b_ref[...], wsc, bs_ref[...]).astype(BF16)

    return pl.pallas_call(
        body, name="gmlp_fwd", grid=(T // CHUNK,),
        in_specs=[_h_cols(A_WIDTH, 0), _h_cols(A_WIDTH, 1), _h_cols(A_WIDTH, 2), _full((1, A_WIDTH)), _full((1, A_WIDTH)),
                  _full((A_GROUPS, CHUNK, CHUNK)), _full((CHUNK, A_GROUPS))],
        out_specs=_h_cols(A_WIDTH, 0), out_shape=SDS((T, A_WIDTH), BF16), compiler_params=_cp("parallel"),
    )(h, h, h, ln_g, ln_b, ws, bs_t)


def gmlp_bwd(h, dy, ln_g, ln_b, ws, bs_t):
    T = h.shape[0]

    def body(u_ref, v_ref, z_ref, dy_ref, g_ref, b_ref, ws_ref, bs_ref, duvz_ref, dg_ref, db_ref, dws_ref, dbs_ref):
        tri = _tril(CHUNK)[None]
        wsc = jnp.where(tri, ws_ref[...], 0.0)
        _, vjp = jax.vjp(_gmlp_chunk, u_ref[...], v_ref[...], z_ref[...], g_ref[...], b_ref[...], wsc, bs_ref[...])
        du, dv, dz, dg, db, dws, dbs = vjp(dy_ref[...])
        duvz_ref[:, :A_WIDTH] = du.astype(BF16)
        duvz_ref[:, A_WIDTH:2 * A_WIDTH] = dv.astype(BF16)
        duvz_ref[:, 2 * A_WIDTH:] = dz.astype(BF16)

        @pl.when(pl.program_id(0) == 0)
        def _():
            dg_ref[...] = jnp.zeros_like(dg_ref)
            db_ref[...] = jnp.zeros_like(db_ref)
            dws_ref[...] = jnp.zeros_like(dws_ref)
            dbs_ref[...] = jnp.zeros_like(dbs_ref)

        dg_ref[...] += dg
        db_ref[...] += db
        dws_ref[...] += jnp.where(tri, dws, 0.0)
        dbs_ref[...] += dbs

    pshapes = [(1, A_WIDTH), (1, A_WIDTH), (A_GROUPS, CHUNK, CHUNK), (CHUNK, A_GROUPS)]
    return pl.pallas_call(
        body, name="gmlp_bwd", grid=(T // CHUNK,),
        in_specs=[_h_cols(A_WIDTH, 0), _h_cols(A_WIDTH, 1), _h_cols(A_WIDTH, 2), _h_cols(A_WIDTH, 0)] + [_full(s) for s in pshapes],
        out_specs=[_h_cols(3 * A_WIDTH, 0)] + [_full(s) for s in pshapes],
        out_shape=[SDS((T, 3 * A_WIDTH), BF16)] + [SDS(s, F32) for s in pshapes],
        compiler_params=_cp("arbitrary"),
    )(h, h, h, dy, ln_g, ln_b, ws, bs_t)


CONV_ROWS = 256
CONV_COLS = 512


def _taps(xe, w, rows):
    K = w.shape[0]
    acc = None
    for k in range(K):
        off = SUBLANES - (K - 1) + k
        term = w[k:k + 1, :] * xe[off:off + rows, :]
        acc = term if acc is None else acc + term
    return acc


def _ssd_conv_tile(x, halo, w, b):
    return _silu(_taps(jnp.concatenate([halo, x], axis=0), w, x.shape[0]) + b)


def _halo_spec(cols, col_idx, nt=None):
    rpb = CONV_ROWS // SUBLANES
    if nt is None:
        return pl.BlockSpec((SUBLANES, cols), lambda c, i: (jnp.maximum(i * rpb - 1, 0), col_idx(c)))
    return pl.BlockSpec((SUBLANES, cols), lambda c, j: (jnp.maximum((nt - 1 - j) * rpb - 1, 0), col_idx(c)))


def ssd_conv_fwd(h, w, b):
    T = h.shape[0]
    nc = B_XBC // CONV_COLS
    base = (3 * A_WIDTH + B_WIDTH) // CONV_COLS

    def body(x_ref, halo_ref, w_ref, b_ref, o_ref):
        halo = jnp.where(pl.program_id(1) > 0, halo_ref[...], 0.0)
        o_ref[...] = _ssd_conv_tile(x_ref[...], halo, w_ref[...], b_ref[...])

    return pl.pallas_call(
        body, name="ssd_conv_fwd", grid=(nc, T // CONV_ROWS),
        in_specs=[pl.BlockSpec((CONV_ROWS, CONV_COLS), lambda c, i: (i, base + c)), _halo_spec(CONV_COLS, lambda c: base + c),
                  pl.BlockSpec((B_CONV, CONV_COLS), lambda c, i: (0, c)), pl.BlockSpec((1, CONV_COLS), lambda c, i: (0, c))],
        out_specs=pl.BlockSpec((CONV_ROWS, CONV_COLS), lambda c, i: (i, c)),
        out_shape=SDS((T, B_XBC), F32), compiler_params=_cp("parallel", "parallel"),
    )(h, h, w, b)


def ssd_conv_bwd(h, dy, w, b, rider=None):
    T = h.shape[0]
    nc = B_XBC // CONV_COLS
    nt = T // CONV_ROWS
    base = (3 * A_WIDTH + B_WIDTH) // CONV_COLS

    def body(x_ref, halo_ref, dy_ref, w_ref, b_ref, dx_ref, dw_ref, db_ref, carry_ref):
        j = pl.program_id(1)
        halo = jnp.where(j < nt - 1, halo_ref[...], 0.0)
        _, vjp = jax.vjp(_ssd_conv_tile, x_ref[...], halo, w_ref[...], b_ref[...])
        dx, dhalo, dw, db = vjp(dy_ref[...])

        @pl.when(j == 0)
        def _():
            carry_ref[...] = jnp.zeros_like(carry_ref)
            dw_ref[...] = jnp.zeros_like(dw_ref)
            db_ref[...] = jnp.zeros_like(db_ref)

        tail = dx[CONV_ROWS - SUBLANES:, :] + carry_ref[...]
        dx_ref[...] = jnp.concatenate([dx[:CONV_ROWS - SUBLANES, :], tail], axis=0).astype(BF16)
        carry_ref[...] = dhalo
        dw_ref[...] += dw
        db_ref[...] += db

    outs, rider_outs = _call(
        body, name="ssd_conv_bwd", grid=(nc, nt),
        in_specs=[pl.BlockSpec((CONV_ROWS, CONV_COLS), lambda c, j: (nt - 1 - j, base + c)),
                  _halo_spec(CONV_COLS, lambda c: base + c, nt),
                  pl.BlockSpec((CONV_ROWS, CONV_COLS), lambda c, j: (nt - 1 - j, c)),
                  pl.BlockSpec((B_CONV, CONV_COLS), lambda c, j: (0, c)), pl.BlockSpec((1, CONV_COLS), lambda c, j: (0, c))],
        out_specs=[pl.BlockSpec((CONV_ROWS, CONV_COLS), lambda c, j: (nt - 1 - j, c)),
                   pl.BlockSpec((B_CONV, CONV_COLS), lambda c, j: (0, c)), pl.BlockSpec((1, CONV_COLS), lambda c, j: (0, c))],
        out_shape=[SDS((T, B_XBC), BF16), SDS((B_CONV, B_XBC), F32), SDS((1, B_XBC), F32)],
        scratch_shapes=[pltpu.VMEM((SUBLANES, CONV_COLS), F32)], sem=("parallel", "arbitrary"),
        args=(h, h, dy, w, b), rider=rider)
    return outs if rider is None else (outs, rider_outs)


def _sconv_tile(bg, cg, hx, z, cg_halo, hx_halo, w):
    ch = jnp.concatenate([cg_halo * hx_halo, cg * hx], axis=0)
    return _silu(z) * (bg * _taps(ch, w, bg.shape[0]))


def sconv_fwd(h, w):
    T = h.shape[0]
    nc = C_WIDTH // CONV_COLS

    def col(seg):
        return pl.BlockSpec((CONV_ROWS, CONV_COLS), lambda c, i: (i, seg * nc + c))

    def body(bg_ref, cg_ref, hx_ref, z_ref, cgh_ref, hxh_ref, w_ref, o_ref):
        first = pl.program_id(1) == 0
        cgh = jnp.where(first, 0.0, cgh_ref[...])
        hxh = jnp.where(first, 0.0, hxh_ref[...])
        o_ref[...] = _sconv_tile(bg_ref[...], cg_ref[...], hx_ref[...], z_ref[...], cgh, hxh, w_ref[...]).astype(BF16)

    return pl.pallas_call(
        body, name="sconv_fwd", grid=(nc, T // CONV_ROWS),
        in_specs=[col(0), col(1), col(2), col(3), _halo_spec(CONV_COLS, lambda c: nc + c), _halo_spec(CONV_COLS, lambda c: 2 * nc + c),
                  pl.BlockSpec((C_CONV, CONV_COLS), lambda c, i: (0, c))],
        out_specs=pl.BlockSpec((CONV_ROWS, CONV_COLS), lambda c, i: (i, c)),
        out_shape=SDS((T, C_WIDTH), BF16), compiler_params=_cp("parallel", "parallel"),
    )(h, h, h, h, h, h, w)


def sconv_bwd(h, dy, w):
    T = h.shape[0]
    nc = C_WIDTH // CONV_COLS
    nt = T // CONV_ROWS

    def col(seg):
        return pl.BlockSpec((CONV_ROWS, CONV_COLS), lambda c, j: (nt - 1 - j, seg * nc + c))

    def body(bg_ref, cg_ref, hx_ref, z_ref, cgh_ref, hxh_ref, dy_ref, w_ref, dbg_ref, dcg_ref, dhx_ref, dz_ref, dw_ref, ccg_ref, chx_ref):
        j = pl.program_id(1)
        first = j == nt - 1
        cgh = jnp.where(first, 0.0, cgh_ref[...])
        hxh = jnp.where(first, 0.0, hxh_ref[...])
        _, vjp = jax.vjp(_sconv_tile, bg_ref[...], cg_ref[...], hx_ref[...], z_ref[...], cgh, hxh, w_ref[...])
        dbg, dcg, dhx, dz, dcgh, dhxh, dw = vjp(dy_ref[...])

        @pl.when(j == 0)
        def _():
            ccg_ref[...] = jnp.zeros_like(ccg_ref)
            chx_ref[...] = jnp.zeros_like(chx_ref)
            dw_ref[...] = jnp.zeros_like(dw_ref)

        def with_carry(d, carry_ref):
            tail = d[CONV_ROWS - SUBLANES:, :] + carry_ref[...]
            return jnp.concatenate([d[:CONV_ROWS - SUBLANES, :], tail], axis=0).astype(BF16)

        dbg_ref[...] = dbg.astype(BF16)
        dz_ref[...] = dz.astype(BF16)
        dcg_ref[...] = with_carry(dcg, ccg_ref)
        dhx_ref[...] = with_carry(dhx, chx_ref)
        ccg_ref[...] = dcgh
        chx_ref[...] = dhxh
        dw_ref[...] += dw

    out_row = pl.BlockSpec((CONV_ROWS, CONV_COLS), lambda c, j: (nt - 1 - j, c))
    wspec = pl.BlockSpec((C_CONV, CONV_COLS), lambda c, j: (0, c))
    return pl.pallas_call(
        body, name="sconv_bwd", grid=(nc, nt),
        in_specs=[col(0), col(1), col(2), col(3), _halo_spec(CONV_COLS, lambda c: nc + c, nt), _halo_spec(CONV_COLS, lambda c: 2 * nc + c, nt),
                  out_row, wspec],
        out_specs=[out_row] * 4 + [wspec],
        out_shape=[SDS((T, C_WIDTH), BF16)] * 4 + [SDS((C_CONV, C_WIDTH), F32)],
        scratch_shapes=[pltpu.VMEM((SUBLANES, CONV_COLS), F32)] * 2,
        compiler_params=_cp("parallel", "arbitrary"),
    )(h, h, h, h, h, h, dy, w)


def _softplus(x):
    return jnp.maximum(x, 0.0) + jnp.log(1.0 + jnp.exp(-jnp.abs(x)))


def _ssd_chunk(xs, bm, cm, dtr, z, prev, dt_bias, a_log, d_skip, norm_g):
    tril = _tril(CHUNK)
    dt = _softplus(dtr + dt_bias)
    adt = dt * (-jnp.exp(a_log))
    a_cs = jnp.dot(tril.astype(F32), adt, precision=lax.Precision.HIGHEST, preferred_element_type=F32)
    a_cs_t = a_cs.T
    a_last = a_cs[CHUNK - 1:CHUNK, :]
    dt_f = _spread_heads(dt, B_HEAD_DIM)
    dec_f = _spread_heads(jnp.exp(a_last - a_cs), B_HEAD_DIM)
    ecs_f = _spread_heads(jnp.exp(a_cs), B_HEAD_DIM)
    dsk_f = _spread_heads(d_skip, B_HEAD_DIM)
    cd_t = jnp.exp(a_cs_t[:, CHUNK - 1:CHUNK])
    xdt = xs * dt_f
    xdd = xdt * dec_f
    colb = _spread_heads(a_cs, CHUNK)
    rowb = jnp.concatenate([jnp.broadcast_to(a_cs_t[hh:hh + 1, :], (CHUNK, CHUNK)) for hh in range(B_HEADS)], axis=1)
    wide = (CHUNK, B_HEADS * CHUNK)
    keep = lax.broadcasted_iota(jnp.int32, wide, 0) >= lax.broadcasted_iota(jnp.int32, wide, 1) % CHUNK
    decay = jnp.exp(jnp.where(keep, colb - rowb, -jnp.inf))
    hpg = B_HEADS // B_GROUPS
    gw = B_WIDTH // B_GROUPS
    low_half = lax.broadcasted_iota(jnp.int32, (CHUNK, 2 * B_HEAD_DIM), 1) < B_HEAD_DIM
    ys, nxt = [], []
    for g in range(B_GROUPS):
        bg = bm[:, g * B_STATE:(g + 1) * B_STATE].astype(BF16)
        cg = cm[:, g * B_STATE:(g + 1) * B_STATE].astype(BF16)
        cb = _dot_nt(cg, bg)
        cbl = (decay[:, g * hpg * CHUNK:(g + 1) * hpg * CHUNK] * jnp.concatenate([cb] * hpg, axis=1)).astype(BF16)
        pg = prev[g * gw:(g + 1) * gw, :]
        y_off = _dot_nt(cg, pg.astype(BF16)) * ecs_f[:, g * gw:(g + 1) * gw]
        st = _dot_tn(xdd[:, g * gw:(g + 1) * gw].astype(BF16), bg)
        cd = jnp.concatenate([jnp.broadcast_to(cd_t[g * hpg + r:g * hpg + r + 1, :], (B_HEAD_DIM, 1)) for r in range(hpg)], axis=0)
        nxt.append(pg * cd + st)
        pairs = []
        for j in range(hpg // 2):
            xp = xdt[:, g * gw + 2 * j * B_HEAD_DIM:g * gw + 2 * (j + 1) * B_HEAD_DIM]
            rhs = jnp.concatenate([jnp.where(low_half, xp, 0.0), jnp.where(low_half, 0.0, xp)], axis=0).astype(BF16)
            pairs.append(_dot_nn(cbl[:, 2 * j * CHUNK:2 * (j + 1) * CHUNK], rhs))
        ys.append(jnp.concatenate(pairs, axis=1) + y_off)
    y = (jnp.concatenate(ys, axis=1) + dsk_f * xs) * _silu(z)
    outs = []
    for g in range(B_GROUPS):
        yg = y[:, g * gw:(g + 1) * gw]
        outs.append(yg * lax.rsqrt(jnp.mean(yg * yg, axis=-1, keepdims=True) + EPS))
    return jnp.concatenate(outs, axis=1) * norm_g, jnp.concatenate(nxt, axis=0)


def _spread_heads(v, width):
    n = B_HEADS * width
    one_hot = (lax.broadcasted_iota(jnp.int32, (LANES, n), 0) == lax.broadcasted_iota(jnp.int32, (LANES, n), 1) // width)
    return jnp.dot(v, one_hot.astype(F32), precision=lax.Precision.HIGHEST, preferred_element_type=F32)


_SSD_PARAM_SHAPES = [(1, LANES), (1, LANES), (1, LANES), (1, B_WIDTH)]
_STATE_SHAPE = (B_WIDTH, B_STATE)


def ssd_fwd(xbc, dtr, h, dt_bias, a_log, d_skip, norm_g, rider=None):
    T = xbc.shape[0]
    nc = T // CHUNK

    def body(xs_ref, b_ref, c_ref, dt_ref, z_ref, p0, p1, p2, p3, y_ref, st_ref, state):
        @pl.when(pl.program_id(0) == 0)
        def _():
            state[...] = jnp.zeros_like(state)

        prev = state[...]
        st_ref[0] = prev
        yb, nxt = _ssd_chunk(xs_ref[...], b_ref[...], c_ref[...], dt_ref[...], z_ref[...], prev, p0[...], p1[...], p2[...], p3[...])
        y_ref[...] = yb.astype(BF16)
        state[...] = nxt

    outs, rider_outs = _call(
        body, name="ssd_fwd", grid=(nc,),
        in_specs=[_h_cols(B_WIDTH, 0), _h_cols(B_GROUPS * B_STATE, 2), _h_cols(B_GROUPS * B_STATE, 3), _h_cols(LANES, 0), _h_cols(B_WIDTH, 3)]
        + [_full(s) for s in _SSD_PARAM_SHAPES],
        out_specs=[_h_cols(B_WIDTH, 0), pl.BlockSpec((1,) + _STATE_SHAPE, lambda i: (i, 0, 0))],
        out_shape=[SDS((T, B_WIDTH), BF16), SDS((nc,) + _STATE_SHAPE, F32)],
        scratch_shapes=[pltpu.VMEM(_STATE_SHAPE, F32)], sem=("arbitrary",),
        args=(xbc, xbc, xbc, dtr, h, dt_bias, a_log, d_skip, norm_g), rider=rider)
    return outs if rider is None else (outs, rider_outs)


def ssd_bwd(xbc, dtr, h, states, dy, dt_bias, a_log, d_skip, norm_g, rider=None):
    T = xbc.shape[0]
    nc = T // CHUNK

    def rev(width, idx):
        return pl.BlockSpec((CHUNK, width), lambda j: (nc - 1 - j, idx))

    def body(xs_ref, b_ref, c_ref, dt_ref, z_ref, st_ref, dy_ref, p0, p1, p2, p3,
             dxbc_ref, ddt_ref, dz_ref, g0, g1, g2, g3, dstate):
        @pl.when(pl.program_id(0) == 0)
        def _():
            dstate[...] = jnp.zeros_like(dstate)
            for gref in (g0, g1, g2, g3):
                gref[...] = jnp.zeros_like(gref)

        _, vjp = jax.vjp(_ssd_chunk, xs_ref[...], b_ref[...], c_ref[...], dt_ref[...], z_ref[...], st_ref[0],
                         p0[...], p1[...], p2[...], p3[...])
        dxs, dbm, dcm, ddt, dz, dprev, d0, d1, d2, d3 = vjp((dy_ref[...], dstate[...]))
        dxbc_ref[:, :B_WIDTH] = dxs
        dxbc_ref[:, B_WIDTH:B_WIDTH + gn] = dbm
        dxbc_ref[:, B_WIDTH + gn:] = dcm
        ddt_ref[...] = ddt
        dz_ref[...] = dz.astype(BF16)
        dstate[...] = dprev
        g0[...] += d0
        g1[...] += d1
        g2[...] += d2
        g3[...] += d3

    gn = B_GROUPS * B_STATE
    outs, rider_outs = _call(
        body, name="ssd_bwd", grid=(nc,),
        in_specs=[rev(B_WIDTH, 0), rev(gn, 2), rev(gn, 3), rev(LANES, 0), rev(B_WIDTH, 3),
                  pl.BlockSpec((1,) + _STATE_SHAPE, lambda j: (nc - 1 - j, 0, 0)), rev(B_WIDTH, 1)]
        + [_full(s) for s in _SSD_PARAM_SHAPES],
        out_specs=[rev(B_XBC, 0), rev(LANES, 0), rev(B_WIDTH, 0)] + [_full(s) for s in _SSD_PARAM_SHAPES],
        out_shape=[SDS((T, B_XBC), F32), SDS((T, LANES), F32), SDS((T, B_WIDTH), BF16)]
        + [SDS(s, F32) for s in _SSD_PARAM_SHAPES],
        scratch_shapes=[pltpu.VMEM(_STATE_SHAPE, F32)], sem=("arbitrary",),
        args=(xbc, xbc, xbc, dtr, h, states, dy, dt_bias, a_log, d_skip, norm_g), rider=rider)
    return outs if rider is None else (outs, rider_outs)


ATT_SCALE = D_HEAD_DIM ** -0.5
Q_COL, K_COL, V_COL, Z_COL = (4 * C_WIDTH // LANES + i * D_HEADS for i in range(4))


ATT_NBLK = ATT_SUPER // ATT_BLOCK


def _res_rows(r, first, count, dil):
    return pl.ds(r + dil * first, count) if dil == 1 else pl.ds(r + dil * first, count, stride=dil)


def _blocks(ref, dil, dtype=None):
    n = ATT_SUPER // dil
    parts = []
    for r in range(dil):
        v = ref[_res_rows(r, 0, n, dil), :]
        parts.append((v if dtype is None else v.astype(dtype)).reshape(n // ATT_BLOCK, ATT_BLOCK, D_HEAD_DIM))
    return parts[0] if dil == 1 else jnp.concatenate(parts, axis=0)


def _blocks_before(cur_ref, prev_ref, dil, dtype):
    n = ATT_SUPER // dil
    parts = []
    for r in range(dil):
        v = prev_ref[_res_rows(r, n - ATT_BLOCK, ATT_BLOCK, dil), :]
        if n > ATT_BLOCK:
            v = jnp.concatenate([v, cur_ref[_res_rows(r, 0, n - ATT_BLOCK, dil), :]], axis=0)
        parts.append(v.astype(dtype).reshape(n // ATT_BLOCK, ATT_BLOCK, D_HEAD_DIM))
    return parts[0] if dil == 1 else jnp.concatenate(parts, axis=0)


def _blocks_after(cur_ref, next_ref, dil, dtype=None):
    n = ATT_SUPER // dil
    parts = []
    for r in range(dil):
        v = next_ref[_res_rows(r, 0, ATT_BLOCK, dil), :]
        if n > ATT_BLOCK:
            v = jnp.concatenate([cur_ref[_res_rows(r, ATT_BLOCK, n - ATT_BLOCK, dil), :], v], axis=0)
        parts.append((v if dtype is None else v.astype(dtype)).reshape(n // ATT_BLOCK, ATT_BLOCK, D_HEAD_DIM))
    return parts[0] if dil == 1 else jnp.concatenate(parts, axis=0)


def _unblock(ref, val, dil, add=False):
    n = ATT_SUPER // dil
    nb = n // ATT_BLOCK
    for r in range(dil):
        v = val[r * nb:(r + 1) * nb].reshape(n, D_HEAD_DIM)
        if add:
            ref[_res_rows(r, 0, n, dil), :] += v
        else:
            ref[_res_rows(r, 0, n, dil), :] = v


def _att_masks(dil, edge_ok, edge_last=False):
    shape = (ATT_NBLK, ATT_BLOCK, ATT_BLOCK)
    blk = lax.broadcasted_iota(jnp.int32, shape, 0)
    row = lax.broadcasted_iota(jnp.int32, shape, 1)
    col = lax.broadcasted_iota(jnp.int32, shape, 2)
    nb = ATT_NBLK // dil
    at_edge = (blk % nb) == (nb - 1 if edge_last else 0)
    return col <= row, jnp.logical_and(col >= row, jnp.logical_or(jnp.logical_not(at_edge), edge_ok))


def _bdot_nt(a, b):
    return lax.dot_general(a, b, (((2,), (2,)), ((0,), (0,))), preferred_element_type=F32)


def _bdot_nn(a, b):
    return lax.dot_general(a, b, (((2,), (1,)), ((0,), (0,))), preferred_element_type=F32)


def _bdot_tn(a, b):
    return lax.dot_general(a, b, (((1,), (1,)), ((0,), (0,))), preferred_element_type=F32)


def _att_spec(col0, shift=0, last=None):
    def imap(hh, n):
        m = n + shift
        if shift < 0:
            m = jnp.maximum(m, 0)
        if shift > 0:
            m = jnp.minimum(m, last)
        return (m, col0 + hh)
    return pl.BlockSpec((ATT_SUPER, D_HEAD_DIM), imap)


def _att_out_spec():
    return pl.BlockSpec((ATT_SUPER, D_HEAD_DIM), lambda hh, n: (n, hh))


def attn_fwd(h):
    T = h.shape[0]
    npat = len(D_PATTERNS)

    def body(q_ref, kc_ref, kp_ref, vc_ref, vp_ref, z_ref, yd_ref, o_ref, lse_ref, *scratch):
        o_s, l_s = scratch[:npat], scratch[npat:]
        has_prev = pl.program_id(1) > 0
        for pi, (_, dil) in enumerate(D_PATTERNS):
            mask_c, mask_p = _att_masks(dil, has_prev)
            q = _blocks(q_ref, dil, BF16)
            kc, vc = _blocks(kc_ref, dil, BF16), _blocks(vc_ref, dil, BF16)
            kp, vp = _blocks_before(kc_ref, kp_ref, dil, BF16), _blocks_before(vc_ref, vp_ref, dil, BF16)
            s_c = jnp.where(mask_c, _bdot_nt(q, kc) * ATT_SCALE, -jnp.inf)
            s_p = jnp.where(mask_p, _bdot_nt(q, kp) * ATT_SCALE, -jnp.inf)
            m = jnp.maximum(jnp.max(s_c, axis=-1, keepdims=True), jnp.max(s_p, axis=-1, keepdims=True))
            p_c = jnp.exp(s_c - m)
            p_p = jnp.exp(s_p - m)
            l = jnp.sum(p_c, axis=-1, keepdims=True) + jnp.sum(p_p, axis=-1, keepdims=True)
            o = _bdot_nn((p_c / l).astype(BF16), vc) + _bdot_nn((p_p / l).astype(BF16), vp)
            _unblock(o_s[pi], o, dil)
            _unblock(l_s[pi], jnp.broadcast_to(m + jnp.log(l), o.shape), dil)
        lses = [l_s[pi][...] for pi in range(npat)]
        mx = functools.reduce(jnp.maximum, lses)
        ws = [jnp.exp(l - mx) for l in lses]
        den = functools.reduce(lambda a, b: a + b, ws)
        o = functools.reduce(lambda a, b: a + b, [(w / den) * o_s[pi][...] for pi, w in enumerate(ws)])
        o_ref[...] = o
        lse_ref[...] = mx + jnp.log(den)
        yd_ref[...] = (_silu(z_ref[...]) * o).astype(BF16)

    n_super = T // ATT_SUPER
    return pl.pallas_call(
        body, name="attn_fwd", grid=(D_HEADS, n_super),
        in_specs=[_att_spec(Q_COL), _att_spec(K_COL), _att_spec(K_COL, -1), _att_spec(V_COL), _att_spec(V_COL, -1), _att_spec(Z_COL)],
        out_specs=[_att_out_spec()] * 3,
        out_shape=[SDS((T, D_HEADS * D_HEAD_DIM), BF16), SDS((T, D_HEADS * D_HEAD_DIM), F32), SDS((T, D_HEADS * D_HEAD_DIM), F32)],
        scratch_shapes=[pltpu.VMEM((ATT_SUPER, D_HEAD_DIM), F32)] * (2 * npat),
        compiler_params=_cp("parallel", "arbitrary"),
    )(h, h, h, h, h, h)


def _dsilu(z):
    s = jax.nn.sigmoid(z)
    return s * (1.0 + z * (1.0 - s))


def attn_bwd_dq(h, o, lse, dy):
    T = h.shape[0]
    dy_col = C_WIDTH // LANES

    def body(q_ref, kc_ref, kp_ref, vc_ref, vp_ref, z_ref, o_ref, lse_ref, dy_ref, dq_ref, dz_ref, do_s, dd_s, dq_s):
        has_prev = pl.program_id(1) > 0
        z, oo, dyd = z_ref[...], o_ref[...], dy_ref[...]
        do = dyd * _silu(z)
        dz_ref[...] = (dyd * oo * _dsilu(z)).astype(BF16)
        do_s[...] = do
        dd_s[...] = jnp.broadcast_to(jnp.sum(do * oo, axis=-1, keepdims=True), (ATT_SUPER, D_HEAD_DIM))
        for pi, (_, dil) in enumerate(D_PATTERNS):
            mask_c, mask_p = _att_masks(dil, has_prev)
            q = _blocks(q_ref, dil, BF16)
            kc, vc = _blocks(kc_ref, dil, BF16), _blocks(vc_ref, dil, BF16)
            kp, vp = _blocks_before(kc_ref, kp_ref, dil, BF16), _blocks_before(vc_ref, vp_ref, dil, BF16)
            lse_b, dd_b, do_b = _blocks(lse_ref, dil), _blocks(dd_s, dil), _blocks(do_s, dil, BF16)
            p_c = jnp.where(mask_c, jnp.exp(_bdot_nt(q, kc) * ATT_SCALE - lse_b), 0.0)
            p_p = jnp.where(mask_p, jnp.exp(_bdot_nt(q, kp) * ATT_SCALE - lse_b), 0.0)
            ds_c = p_c * (_bdot_nt(do_b, vc) - dd_b) * ATT_SCALE
            ds_p = p_p * (_bdot_nt(do_b, vp) - dd_b) * ATT_SCALE
            dq = _bdot_nn(ds_c.astype(BF16), kc) + _bdot_nn(ds_p.astype(BF16), kp)
            _unblock(dq_s, dq, dil, add=pi > 0)
        dq_ref[...] = dq_s[...].astype(BF16)

    n_super = T // ATT_SUPER
    blk = (ATT_SUPER, D_HEAD_DIM)
    return pl.pallas_call(
        body, name="attn_bwd_dq", grid=(D_HEADS, n_super),
        in_specs=[_att_spec(Q_COL), _att_spec(K_COL), _att_spec(K_COL, -1), _att_spec(V_COL), _att_spec(V_COL, -1), _att_spec(Z_COL),
                  _att_spec(0), _att_spec(0), _att_spec(dy_col)],
        out_specs=[_att_out_spec()] * 2,
        out_shape=[SDS((T, D_HEADS * D_HEAD_DIM), BF16)] * 2,
        scratch_shapes=[pltpu.VMEM(blk, F32)] * 3,
        compiler_params=_cp("parallel", "arbitrary"),
    )(h, h, h, h, h, h, o, lse, dy)


def attn_bwd_dkv(h, o, lse, dy):
    T = h.shape[0]
    n_super = T // ATT_SUPER
    last = n_super - 1
    dy_col = C_WIDTH // LANES
    npat = len(D_PATTERNS)

    def body(k_ref, v_ref, qc_ref, qn_ref, zc_ref, zn_ref, oc_ref, on_ref, lc_ref, ln_ref, dyc_ref, dyn_ref, dk_ref, dv_ref,
             do_c, do_n, dd_c, dd_n, dk_s, dv_s):
        do_s, dd_s = (do_c, do_n), (dd_c, dd_n)
        has_next = pl.program_id(1) < last
        for i, (z_ref, oo_ref, dyd_ref) in enumerate(((zc_ref, oc_ref, dyc_ref), (zn_ref, on_ref, dyn_ref))):
            do = dyd_ref[...] * _silu(z_ref[...])
            do_s[i][...] = do
            dd_s[i][...] = jnp.broadcast_to(jnp.sum(do * oo_ref[...], axis=-1, keepdims=True), (ATT_SUPER, D_HEAD_DIM))
        for pi, (_, dil) in enumerate(D_PATTERNS):
            mask_c, mask_p = _att_masks(dil, has_next, edge_last=True)
            kb, vb = _blocks(k_ref, dil, BF16), _blocks(v_ref, dil, BF16)
            dk = dv = None
            for own in (True, False):
                if own:
                    q, lse_b = _blocks(qc_ref, dil, BF16), _blocks(lc_ref, dil)
                    do_b, dd_b = _blocks(do_c, dil, BF16), _blocks(dd_c, dil)
                else:
                    q, lse_b = _blocks_after(qc_ref, qn_ref, dil, BF16), _blocks_after(lc_ref, ln_ref, dil)
                    do_b, dd_b = _blocks_after(do_c, do_n, dil, BF16), _blocks_after(dd_c, dd_n, dil)
                p = jnp.where(mask_c if own else mask_p, jnp.exp(_bdot_nt(q, kb) * ATT_SCALE - lse_b), 0.0)
                ds = p * (_bdot_nt(do_b, vb) - dd_b) * ATT_SCALE
                dv_t = _bdot_tn(p.astype(BF16), do_b)
                dk_t = _bdot_tn(ds.astype(BF16), q)
                dk = dk_t if dk is None else dk + dk_t
                dv = dv_t if dv is None else dv + dv_t
            _unblock(dk_s, dk, dil, add=pi > 0)
            _unblock(dv_s, dv, dil, add=pi > 0)
        dk_ref[...] = dk_s[...].astype(BF16)
        dv_ref[...] = dv_s[...].astype(BF16)

    blk = (ATT_SUPER, D_HEAD_DIM)

    def pair(col0):
        return [_att_spec(col0), _att_spec(col0, 1, last)]

    return pl.pallas_call(
        body, name="attn_bwd_dkv", grid=(D_HEADS, n_super),
        in_specs=[_att_spec(K_COL), _att_spec(V_COL)] + pair(Q_COL) + pair(Z_COL) + pair(0) + pair(0) + pair(dy_col),
        out_specs=[_att_out_spec()] * 2,
        out_shape=[SDS((T, D_HEADS * D_HEAD_DIM), BF16)] * 2,
        scratch_shapes=[pltpu.VMEM(blk, F32)] * 6,
        compiler_params=_cp("parallel", "arbitrary"),
    )(h, h, h, h, h, h, o, o, lse, lse, dy, dy)


ANY = pl.BlockSpec(memory_space=pl.ANY)
COMM_PARAMS = pltpu.CompilerParams()


def _place():
    x, y, c = lax.axis_index("x"), lax.axis_index("y"), lax.axis_index("c")
    return x, y, c, [(1 - x, y), (x, 1 - y), (1 - x, 1 - y)]


def _rcopy(src, dst, ssem, rsem, dev):
    return pltpu.make_async_remote_copy(src_ref=src, dst_ref=dst, send_sem=ssem, recv_sem=rsem, device_id=dev, device_id_type=MESH)


def gather_rider(arrs, fractions=(0.0, 0.6, 1.0)):
    n = len(arrs)
    per = 7

    def to_chips(ins, outs, ssem, rsem):
        x, y, c, chips = _place()
        return [_rcopy(ins[a].at[c], outs[a].at[2 * x + y, c], ssem.at[per * a + j], rsem.at[per * a + j], (px, py, c))
                for a in range(n) for j, (px, py) in enumerate(chips)]

    def passed_on(outs, ssem, rsem, half):
        x, y, c, chips = _place()
        cps = []
        for a in range(n):
            for j, (px, py) in enumerate(chips):
                slot = outs[a].at[2 * px + py, half(c)]
                cps.append(_rcopy(slot, slot, ssem.at[per * a + 3 + j], rsem.at[per * a + 3 + j], (x, y, 1 - c)))
        return cps

    def own(ins, outs, ssem, rsem):
        x, y, c, _ = _place()
        return [_rcopy(ins[a], outs[a].at[2 * x + y], ssem.at[per * a + 6], rsem.at[per * a + 6], (x, y, 1 - c)) for a in range(n)]

    def start(ins, outs, ssem, rsem):
        for cp in to_chips(ins, outs, ssem, rsem) + own(ins, outs, ssem, rsem):
            cp.start()

    def pass_on(ins, outs, ssem, rsem):
        x, y, c, chips = _place()
        landed = [_rcopy(outs[a].at[2 * px + py, c], outs[a].at[2 * px + py, c], ssem.at[per * a + j], rsem.at[per * a + j], (px, py, c))
                  for a in range(n) for j, (px, py) in enumerate(chips)]
        for arrival, cp in zip(landed, passed_on(outs, ssem, rsem, lambda c: c)):
            arrival.wait_recv()
            cp.start()

    def finish(ins, outs, ssem, rsem):
        for cp in passed_on(outs, ssem, rsem, lambda c: 1 - c):
            cp.wait_recv()
        for cp in to_chips(ins, outs, ssem, rsem) + passed_on(outs, ssem, rsem, lambda c: c):
            cp.wait_send()
        for cp in own(ins, outs, ssem, rsem):
            cp.wait()

    return Rider(arrs, [SDS((N_CHIPS,) + a.shape, a.dtype) for a in arrs], per * n,
                 [(fractions[0], start), (fractions[1], pass_on), (fractions[2], finish)])


def _copies_rider(ins, out_shapes, n_sems, make):
    def start(*refs):
        for cp in make(*refs):
            cp.start()

    def finish(*refs):
        for cp in make(*refs):
            cp.wait()

    return Rider(ins, out_shapes, n_sems, [(0.0, start), (1.0, finish)])


def swap_halves_rider(arrs):
    def make(ins, outs, ssem, rsem):
        x, y, c, _ = _place()
        return [_rcopy(ins[a].at[1 - c], outs[a], ssem.at[a], rsem.at[a], (x, y, 1 - c)) for a in range(len(arrs))]
    return _copies_rider(arrs, [SDS(a.shape[1:], a.dtype) for a in arrs], len(arrs), make)


def scatter_rider(arrs):
    def make(ins, outs, ssem, rsem):
        x, y, c, chips = _place()
        return [_rcopy(ins[a].at[2 * px + py], outs[a].at[j], ssem.at[3 * a + j], rsem.at[3 * a + j], (px, py, c))
                for a in range(len(arrs)) for j, (px, py) in enumerate(chips)]
    return _copies_rider(arrs, [SDS((N_CHIPS - 1,) + a.shape[1:], a.dtype) for a in arrs], 3 * len(arrs), make)


def swap_rider(arrs):
    def make(ins, outs, ssem, rsem):
        x, y, c, _ = _place()
        return [_rcopy(ins[a], outs[a], ssem.at[a], rsem.at[a], (x, y, 1 - c)) for a in range(len(arrs))]
    return _copies_rider(arrs, [SDS(a.shape, a.dtype) for a in arrs], len(arrs), make)


def gather_all(buf):
    def body(in_ref, out_ref, ssem, rsem, lsem):
        x, y, c, _ = _place()
        me = 4 * x + 2 * y + c
        local = pltpu.make_async_copy(in_ref, out_ref.at[me], lsem)
        local.start()
        flips = [(a, b, e) for a in (0, 1) for b in (0, 1) for e in (0, 1)][1:]
        cps = []
        for i, (a, b, e) in enumerate(flips):
            peer = (x ^ a, y ^ b, c ^ e)
            cps.append(_rcopy(in_ref, out_ref.at[me], ssem.at[i], rsem.at[i], peer))
        for cp in cps:
            cp.start()
        for i, (a, b, e) in enumerate(flips):
            cps[i].wait_send()
            slot = out_ref.at[4 * (x ^ a) + 2 * (y ^ b) + (c ^ e)]
            _rcopy(slot, slot, ssem.at[i], rsem.at[i], (x ^ a, y ^ b, c ^ e)).wait_recv()
        local.wait()

    return pl.pallas_call(
        body, name="comm_gather_all", in_specs=[ANY], out_specs=ANY, out_shape=SDS((N_DEV,) + buf.shape, buf.dtype),
        scratch_shapes=[pltpu.SemaphoreType.DMA((N_DEV - 1,)), pltpu.SemaphoreType.DMA((N_DEV - 1,)), pltpu.SemaphoreType.DMA],
        compiler_params=COMM_PARAMS,
    )(buf)


def _pack_offsets(parts):
    offs, r = [], 0
    for p in parts:
        offs.append(r)
        r += -(-p.shape[0] // SUBLANES) * SUBLANES
    return offs, r


def pack_rows(parts):
    offs, total = _pack_offsets(parts)

    def body(*refs):
        out = refs[-1]
        out[...] = jnp.zeros_like(out)
        for ref, off in zip(refs[:-1], offs):
            out[off:off + ref.shape[0], :] = ref[...]

    vmem = pl.BlockSpec(memory_space=pltpu.VMEM)
    return pl.pallas_call(body, name="pack_small", in_specs=[vmem] * len(parts), out_specs=vmem,
                          out_shape=SDS((total, LANES), F32))(*parts)


def _unpack(buf, parts):
    offs, _ = _pack_offsets(parts)
    return [buf[off:off + p.shape[0]] for p, off in zip(parts, offs)]


def _pad_lanes(v):
    v = v.reshape(1, -1)
    return jnp.pad(v, ((0, 0), (0, LANES - v.shape[1])))


def _as2d(a):
    return a.reshape(1, -1) if a.ndim == 1 else a.reshape(-1, a.shape[-1])


def kernel(x, even_norm_g, even_w_in, gmlp_ln_g, gmlp_ln_b, gmlp_ws, gmlp_bs, ssd_conv_w, ssd_conv_b, ssd_dt_bias, ssd_a_log, ssd_d, ssd_norm_g, even_w_out, odd_norm_g, odd_w_in, sconv_w, odd_w_out, final_norm_g, loss_target, m_even_norm_g, m_even_w_in, m_gmlp_ln_g, m_gmlp_ln_b, m_gmlp_ws, m_gmlp_bs, m_ssd_conv_w, m_ssd_conv_b, m_ssd_dt_bias, m_ssd_a_log, m_ssd_d, m_ssd_norm_g, m_even_w_out, m_odd_norm_g, m_odd_w_in, m_sconv_w, m_odd_w_out, m_final_norm_g, v_even_norm_g, v_even_w_in, v_gmlp_ln_g, v_gmlp_ln_b, v_gmlp_ws, v_gmlp_bs, v_ssd_conv_w, v_ssd_conv_b, v_ssd_dt_bias, v_ssd_a_log, v_ssd_d, v_ssd_norm_g, v_even_w_out, v_odd_norm_g, v_odd_w_in, v_sconv_w, v_odd_w_out, v_final_norm_g):
    weights = dict(even_norm_g=even_norm_g, even_w_in=even_w_in, gmlp_ln_g=gmlp_ln_g, gmlp_ln_b=gmlp_ln_b, gmlp_ws=gmlp_ws, gmlp_bs=gmlp_bs, ssd_conv_w=ssd_conv_w, ssd_conv_b=ssd_conv_b, ssd_dt_bias=ssd_dt_bias, ssd_a_log=ssd_a_log, ssd_d=ssd_d, ssd_norm_g=ssd_norm_g, even_w_out=even_w_out, odd_norm_g=odd_norm_g, odd_w_in=odd_w_in, sconv_w=sconv_w, odd_w_out=odd_w_out, final_norm_g=final_norm_g)
    moms_m = dict(even_norm_g=m_even_norm_g, even_w_in=m_even_w_in, gmlp_ln_g=m_gmlp_ln_g, gmlp_ln_b=m_gmlp_ln_b, gmlp_ws=m_gmlp_ws, gmlp_bs=m_gmlp_bs, ssd_conv_w=m_ssd_conv_w, ssd_conv_b=m_ssd_conv_b, ssd_dt_bias=m_ssd_dt_bias, ssd_a_log=m_ssd_a_log, ssd_d=m_ssd_d, ssd_norm_g=m_ssd_norm_g, even_w_out=m_even_w_out, odd_norm_g=m_odd_norm_g, odd_w_in=m_odd_w_in, sconv_w=m_sconv_w, odd_w_out=m_odd_w_out, final_norm_g=m_final_norm_g)
    moms_v = dict(even_norm_g=v_even_norm_g, even_w_in=v_even_w_in, gmlp_ln_g=v_gmlp_ln_g, gmlp_ln_b=v_gmlp_ln_b, gmlp_ws=v_gmlp_ws, gmlp_bs=v_gmlp_bs, ssd_conv_w=v_ssd_conv_w, ssd_conv_b=v_ssd_conv_b, ssd_dt_bias=v_ssd_dt_bias, ssd_a_log=v_ssd_a_log, ssd_d=v_ssd_d, ssd_norm_g=v_ssd_norm_g, even_w_out=v_even_w_out, odd_norm_g=v_odd_norm_g, odd_w_in=v_odd_w_in, sconv_w=v_sconv_w, odd_w_out=v_odd_w_out, final_norm_g=v_final_norm_g)
    names = list(weights)

    xs = x[0]
    tgt = loss_target[0]
    T = xs.shape[0]
    chip = 2 * lax.axis_index("x") + lax.axis_index("y")
    core = lax.axis_index("c")
    cshard = B_XBC // N_CHIPS
    dshard = D_MODEL // N_CHIPS

    def halves(w):
        return w.astype(BF16).reshape(2, w.shape[0] // 2, w.shape[1])

    small_shard = jnp.concatenate([ssd_conv_w[0].reshape(-1), odd_norm_g[0], sconv_w[0].reshape(-1)])
    g_wie, g_small = run_rider(gather_rider([halves(even_w_in[0]), small_shard.reshape(2, -1, LANES)]), name="comm_gather_first")
    wie = g_wie.reshape(N_CHIPS, D_MODEL, IN_EVEN // N_CHIPS).transpose(1, 0, 2).reshape(D_MODEL, IN_EVEN)
    wie_main = wie[:, :EVEN_MAIN]
    wie_dt = jnp.pad(wie[:, EVEN_MAIN:], ((0, 0), (0, LANES - B_HEADS)))
    g_small = g_small.reshape(N_CHIPS, -1)
    n_cw = B_CONV * cshard
    conv_w = g_small[:, :n_cw].reshape(N_CHIPS, B_CONV, cshard).transpose(1, 0, 2).reshape(B_CONV, B_XBC)
    odd_g = g_small[:, n_cw:n_cw + dshard].reshape(1, D_MODEL)
    sconv = g_small[:, n_cw + dshard:].reshape(N_CHIPS, C_CONV, dshard).transpose(1, 0, 2).reshape(C_CONV, C_WIDTH)

    even_g = even_norm_g
    ln_g, ln_b = gmlp_ln_g, gmlp_ln_b
    ws, bs_t = gmlp_ws[0], gmlp_bs[0].T
    conv_b = ssd_conv_b
    dt_bias, a_log, d_skip = _pad_lanes(ssd_dt_bias), _pad_lanes(ssd_a_log), _pad_lanes(ssd_d)
    norm_g = ssd_norm_g
    fin_g = final_norm_g.reshape(1, D_MODEL)

    xn0 = rmsnorm_fwd(xs, even_g, name="even_norm")
    h0, (g_wio,) = matmul(xn0, wie_main, "nn", name="even_in", rider=gather_rider([halves(odd_w_in[0])]))
    wio = g_wio.reshape(N_CHIPS, D_MODEL, IN_ODD // N_CHIPS)
    dtr = matmul(xn0, wie_dt, "nn", name="even_in_dt", tk=D_MODEL)
    ya = gmlp_fwd(h0, ln_g, ln_b, ws, bs_t)
    xbc = ssd_conv_fwd(h0, conv_w, conv_b)
    (yb, states), (g_woe, g_woo) = ssd_fwd(xbc, dtr, h0, dt_bias, a_log, d_skip, norm_g,
                                           rider=gather_rider([halves(even_w_out[0]), halves(odd_w_out[0])]))
    woe = g_woe.reshape(2 * A_WIDTH, D_MODEL)
    woo = g_woo.reshape(2 * C_WIDTH, D_MODEL)
    y0 = jnp.concatenate([ya, yb], axis=1)
    x1 = matmul(y0, woe, "nn", name="even_out", res=xs)

    xn1 = rmsnorm_fwd(x1, odd_g, name="odd_norm")
    h1 = matmul(xn1, wio, "nn", name="odd_in")
    yc = sconv_fwd(h1, sconv)
    yd, att_o, att_lse = attn_fwd(h1)
    y1 = jnp.concatenate([yc, yd], axis=1)
    x2 = matmul(y1, woo, "nn", name="odd_out", res=x1)

    loss_part, dx2, dx2b, d_fin_g = loss_head(x2, fin_g, tgt)

    tile = 1024
    rows_layout = ((2, N_CHIPS, ROW_PIECE, D_MODEL), (1, 1, ROW_PIECE, tile), lambda i, j, k: (i % 2, i // 2, 0, j))
    per_chip = IN_ODD // N_CHIPS // tile
    cols_layout = ((2, N_CHIPS, D_MODEL // 2, IN_ODD // N_CHIPS), (1, 1, tile, tile), lambda i, j, k: (i, j // per_chip, 0, j % per_chip))
    dy1 = matmul(dx2b, woo, "nt", name="odd_out_dy")
    d_woo = matmul(y1, dx2b, "tn", name="odd_out_dw", tm=ROW_PIECE, out_layout=rows_layout)
    dbg, dcg, dhx, dzc, d_sconv = sconv_bwd(h1, dy1, sconv)
    dq, dzd = attn_bwd_dq(h1, att_o, att_lse, dy1)
    dk, dv = attn_bwd_dkv(h1, att_o, att_lse, dy1)
    dh1 = jnp.concatenate([dbg, dcg, dhx, dzc, dq, dk, dv, dzd], axis=1)
    dxn1 = matmul(dh1, wio, "nt", name="odd_in_dx")
    d_wio = matmul(xn1, dh1, "tn", name="odd_in_dw", out_layout=cols_layout)
    dx1, dx1b, d_odd_g = rmsnorm_bwd(x1, odd_g, dxn1, dx2, name="odd_norm_bwd")

    dy0, odd_sib = matmul(dx1b, woe, "nt", name="even_out_dy", rider=swap_halves_rider([d_wio, d_woo]))
    odd_sums = [chip_sum(p, s, core, name=f"chip_sum_odd_{i}") for i, (p, s) in enumerate(zip([d_wio, d_woo], odd_sib))]
    d_woe = matmul(y0, dx1b, "tn", name="even_out_dw", tm=ROW_PIECE, out_layout=rows_layout)
    duvz, d_ln_g, d_ln_b, d_ws, d_bs_t = gmlp_bwd(h0, dy0, ln_g, ln_b, ws, bs_t)
    (dxbc_act, ddtr, dzb, d_dt_bias, d_a_log, d_d, d_norm_g), odd_landed = ssd_bwd(
        xbc, dtr, h0, states, dy0, dt_bias, a_log, d_skip, norm_g, rider=scatter_rider(odd_sums))
    odd_totals = [total_sum(s, l, chip, name=f"total_odd_{i}") for i, (s, l) in enumerate(zip(odd_sums, odd_landed))]
    (dxbc, d_conv_w, d_conv_b), odd_totals_sib = ssd_conv_bwd(h0, dxbc_act, conv_w, conv_b, rider=swap_rider(odd_totals))
    dh0 = jnp.concatenate([duvz, dzb, dxbc], axis=1)
    ddtr_b = ddtr.astype(BF16)
    dxn0 = matmul(ddtr_b, wie_dt, "nt", name="even_in_dx_dt")
    dxn0 = matmul(dh0, wie_main, "nt", name="even_in_dx", res=dxn0)
    d_wie_main = matmul(xn0, dh0, "tn", name="even_in_dw")
    d_wie_dt = matmul(xn0, ddtr_b, "tn", name="even_in_dw_dt")
    grad_x, _, d_even_g = rmsnorm_bwd(xs, even_g, dxn0, dx1, name="even_norm_bwd")
    d_wie = jnp.concatenate([d_wie_main, d_wie_dt[:, :B_HEADS]], axis=1)

    def by_cols(g):
        r, n = g.shape
        return g.reshape(2, r // 2, N_CHIPS, n // N_CHIPS).transpose(0, 2, 1, 3)

    pieces = [by_cols(d_wie), d_woe]
    from_sib = run_rider(swap_halves_rider(pieces), name="comm_swap_even")
    chip_sums = [chip_sum(p, s, core, name=f"chip_sum_even_{i}") for i, (p, s) in enumerate(zip(pieces, from_sib))]
    landed = run_rider(scatter_rider(chip_sums), name="comm_scatter_even")
    totals = [total_sum(s, l, chip, name=f"total_even_{i}") for i, (s, l) in enumerate(zip(chip_sums, landed))]
    totals_sib = run_rider(swap_rider(totals), name="comm_swap_totals_even")
    joined = [jnp.where(core == 0, jnp.stack([t, s]), jnp.stack([s, t]))
              for t, s in zip(totals + odd_totals, totals_sib + odd_totals_sib)]
    big = dict(even_w_in=joined[0].reshape(even_w_in.shape), even_w_out=joined[1].reshape(even_w_out.shape),
               odd_w_in=joined[2].reshape(odd_w_in.shape), odd_w_out=joined[3].reshape(odd_w_out.shape))

    small_names = ["even_norm_g", "gmlp_ln_g", "gmlp_ln_b", "gmlp_ws", "gmlp_bs", "ssd_conv_w", "ssd_conv_b", "ssd_dt_bias",
                   "ssd_a_log", "ssd_d", "ssd_norm_g", "odd_norm_g", "sconv_w", "final_norm_g"]
    small_parts = [d_even_g, d_ln_g, d_ln_b, d_ws, d_bs_t.T, d_conv_w, d_conv_b, d_dt_bias, d_a_log, d_d, d_norm_g, d_odd_g, d_sconv, d_fin_g]
    small_shapes = [p.shape for p in small_parts]
    small_rows = [p.reshape(-1, LANES) for p in small_parts]
    small_sum = sum_leading(gather_all(pack_rows(small_rows)), name="small_sum")
    full = {nm: rows.reshape(shape) for nm, rows, shape in zip(small_names, _unpack(small_sum, small_rows), small_shapes)}
    grads = dict(big)
    for nm in small_names:
        g = full[nm]
        if nm in ("ssd_dt_bias", "ssd_a_log", "ssd_d"):
            g = g[:, :B_HEADS]
        elif nm == "ssd_conv_w":
            g = lax.dynamic_slice_in_dim(g, chip * cshard, cshard, axis=1)
        elif nm in ("odd_norm_g", "sconv_w"):
            g = lax.dynamic_slice_in_dim(g, chip * dshard, dshard, axis=1)
        grads[nm] = g.reshape(weights[nm].shape)

    deltas, new_m, new_v = {}, {}, {}
    for nm in names:
        w = weights[nm]
        d, nm_, nv_ = adamw(_as2d(w), _as2d(grads[nm]), _as2d(moms_m[nm]), _as2d(moms_v[nm]), name=f"adamw_{nm}")
        deltas[nm], new_m[nm], new_v[nm] = d.reshape(w.shape), nm_.reshape(w.shape), nv_.reshape(w.shape)

    loss = lax.psum(loss_part[0, 0], ("x", "y", "c"))
    return (loss, grad_x[None], *[grads[n] for n in names], *[deltas[n] for n in names],
            *[new_m[n] for n in names], *[new_v[n] for n in names])
```

```python
import functools

import jax
import jax.numpy as jnp
from jax import lax
from jax.experimental import pallas as pl
from jax.experimental.pallas import tpu as pltpu

F32 = jnp.float32
BF16 = jnp.bfloat16
SDS = jax.ShapeDtypeStruct
MESH = pl.DeviceIdType.MESH

D_MODEL = 2048
A_WIDTH = 2048
A_GROUPS = 8
CHUNK = 128
B_WIDTH = 2048
B_HEADS = 32
B_HEAD_DIM = 64
B_GROUPS = 8
B_STATE = 128
B_CONV = 4
B_XBC = B_WIDTH + 2 * B_GROUPS * B_STATE
C_WIDTH = 2048
C_CONV = 3
D_HEADS = 16
D_HEAD_DIM = 128
D_PATTERNS = ((128, 1), (512, 4), (2048, 16))
ATT_BLOCK = 128
ATT_SUPER = 2048
EVEN_MAIN = 3 * A_WIDTH + B_WIDTH + B_XBC
IN_EVEN = EVEN_MAIN + B_HEADS
IN_ODD = 4 * C_WIDTH + 4 * D_HEADS * D_HEAD_DIM
LANES = 128
SUBLANES = 8
EPS = 1e-5
ADAM_LR = 0.001
ADAM_B1 = 0.9
ADAM_B2 = 0.999
ADAM_EPS = 1e-08
ADAM_WD = 0.01
ADAM_STEP = 10
N_CHIPS = 4
N_DEV = 8
VMEM_LIMIT_BYTES = 56 * 1024 * 1024


def _cp(*sem):
    return pltpu.CompilerParams(dimension_semantics=sem, vmem_limit_bytes=VMEM_LIMIT_BYTES)


def _full(shape):
    return pl.BlockSpec(shape, lambda *_: (0,) * len(shape))


def _silu(x):
    return x * jax.nn.sigmoid(x)


def _dot_nn(a, b):
    return lax.dot_general(a, b, (((1,), (0,)), ((), ())), preferred_element_type=F32)


def _dot_nt(a, b):
    return lax.dot_general(a, b, (((1,), (1,)), ((), ())), preferred_element_type=F32)


def _dot_tn(a, b):
    return lax.dot_general(a, b, (((0,), (0,)), ((), ())), preferred_element_type=F32)


def _tril(n):
    return lax.broadcasted_iota(jnp.int32, (n, n), 0) >= lax.broadcasted_iota(jnp.int32, (n, n), 1)


_DOTS = {"nn": _dot_nn, "nt": _dot_nt, "tn": _dot_tn}


class Rider:
    def __init__(self, ins, out_shapes, n_sems, phases):
        self.ins, self.out_shapes, self.n_sems, self.phases = list(ins), list(out_shapes), n_sems, list(phases)


def _call(body, *, name, grid, in_specs, out_specs, out_shape, scratch_shapes, sem, args, rider=None):
    in_specs, out_specs, out_shape, scratch_shapes = list(in_specs), list(out_specs), list(out_shape), list(scratch_shapes)
    if rider is None:
        res = pl.pallas_call(body, name=name, grid=grid, in_specs=in_specs, out_specs=out_specs, out_shape=out_shape,
                             scratch_shapes=scratch_shapes, compiler_params=_cp(*sem))(*args)
        return list(res), []
    counts = [len(in_specs), len(rider.ins), len(out_specs), len(rider.out_shapes), len(scratch_shapes), 2]
    total = 1
    for g in grid:
        total *= g

    def wrapped(*refs):
        groups, pos = [], 0
        for n in counts:
            groups.append(refs[pos:pos + n])
            pos += n
        ins, rins, outs, routs, scr, (ssem, rsem) = groups
        step = 0
        for d, g in enumerate(grid):
            step = step * g + pl.program_id(d)
        for frac, fn in rider.phases:
            @pl.when(step == min(int(frac * total), total - 1))
            def _(fn=fn):
                fn(rins, routs, ssem, rsem)
        body(*ins, *outs, *scr)

    dma = pltpu.SemaphoreType.DMA((rider.n_sems,))
    res = pl.pallas_call(
        wrapped, name=name, grid=grid, in_specs=in_specs + [ANY] * len(rider.ins), out_specs=out_specs + [ANY] * len(rider.out_shapes),
        out_shape=out_shape + rider.out_shapes, scratch_shapes=scratch_shapes + [dma, dma],
        compiler_params=_cp(*(("arbitrary",) * len(grid))))(*args, *rider.ins)
    return list(res[:len(out_specs)]), list(res[len(out_specs):])


def run_rider(rider, *, name):
    def body(*refs):
        n_in, n_out = len(rider.ins), len(rider.out_shapes)
        ins, outs, (ssem, rsem) = refs[:n_in], refs[n_in:n_in + n_out], refs[n_in + n_out:]
        for _, fn in rider.phases:
            fn(ins, outs, ssem, rsem)

    dma = pltpu.SemaphoreType.DMA((rider.n_sems,))
    return list(pl.pallas_call(body, name=name, in_specs=[ANY] * len(rider.ins), out_specs=[ANY] * len(rider.out_shapes),
                               out_shape=rider.out_shapes, scratch_shapes=[dma, dma])(*rider.ins))


def matmul(a, b, mode, *, name, out_dtype=F32, res=None, tm=1024, tn=1024, tk=2048, out_layout=None, rider=None):
    shards = b.shape[0] if b.ndim == 3 else 1
    b_rows, b_cols = b.shape[-2], b.shape[-1] * shards
    if mode == "tn":
        (K, M), (K2, N) = a.shape, (b_rows, b_cols)
    elif mode == "nt":
        (M, K), (N, K2) = a.shape, (b_rows, b_cols)
    else:
        (M, K), (K2, N) = a.shape, (b_rows, b_cols)
    assert K == K2, (a.shape, b.shape, mode)
    tm, tn, tk = min(tm, M), min(tn, N), min(tk, K)
    if shards > 1:
        assert mode != "tn" and b.shape[-1] % (tk if mode == "nt" else tn) == 0
    assert M % tm == 0 and N % tn == 0 and K % tk == 0, (M, N, K, tm, tn, tk)
    nk = K // tk
    dot = _DOTS[mode]

    def body(*refs):
        a_ref, b_ref = refs[:2]
        r_ref = refs[2] if res is not None else None
        o_ref = refs[3] if res is not None else refs[2]

        def product():
            bv = b_ref[0] if shards > 1 else b_ref[...]
            return dot(a_ref[...], bv)

        def finish(acc):
            if res is not None:
                acc = acc + r_ref[...]
            o_ref[...] = acc.astype(o_ref.dtype).reshape(o_ref.shape)

        if nk == 1:
            finish(product())
            return
        acc_ref = refs[-1]
        k = pl.program_id(2)

        @pl.when(k == 0)
        def _():
            acc_ref[...] = product()

        @pl.when(jnp.logical_and(k > 0, k < nk - 1))
        def _():
            acc_ref[...] += product()

        @pl.when(k == nk - 1)
        def _():
            finish(acc_ref[...] + product())

    a_spec = pl.BlockSpec((tk, tm), lambda i, j, k: (k, i)) if mode == "tn" else pl.BlockSpec((tm, tk), lambda i, j, k: (i, k))
    if shards > 1 and mode == "nn":
        per = b.shape[-1] // tn
        b_spec = pl.BlockSpec((1, tk, tn), lambda i, j, k: (j // per, k, j % per))
    elif shards > 1:
        per = b.shape[-1] // tk
        b_spec = pl.BlockSpec((1, tn, tk), lambda i, j, k: (k // per, j, k % per))
    elif mode == "nt":
        b_spec = pl.BlockSpec((tn, tk), lambda i, j, k: (j, k))
    else:
        b_spec = pl.BlockSpec((tk, tn), lambda i, j, k: (k, j))
    o_spec = pl.BlockSpec((tm, tn), lambda i, j, k: (i, j))
    in_specs, args = [a_spec, b_spec], [a, b]
    if res is not None:
        in_specs.append(o_spec)
        args.append(res)
    out_shape = SDS((M, N), out_dtype)
    if out_layout is not None:
        out_shape, o_spec = SDS(out_layout[0], out_dtype), pl.BlockSpec(out_layout[1], out_layout[2])
    outs, rider_outs = _call(
        body, name=name, grid=(M // tm, N // tn, nk), in_specs=in_specs, out_specs=[o_spec], out_shape=[out_shape],
        scratch_shapes=[pltpu.VMEM((tm, tn), F32)] if nk > 1 else [], sem=("parallel", "parallel", "arbitrary"),
        args=args, rider=rider)
    return outs[0] if rider is None else (outs[0], rider_outs)


ROW_TILE = 512
ROW_PIECE = 512


def _rms(x, g):
    return x * lax.rsqrt(jnp.mean(x * x, axis=-1, keepdims=True) + EPS) * g


def rmsnorm_fwd(x, g, *, name):
    T, D = x.shape

    def body(x_ref, g_ref, o_ref):
        o_ref[...] = _rms(x_ref[...], g_ref[...]).astype(BF16)

    row = pl.BlockSpec((ROW_TILE, D), lambda i: (i, 0))
    return pl.pallas_call(body, name=name, grid=(T // ROW_TILE,), in_specs=[row, _full((1, D))], out_specs=row,
                          out_shape=SDS((T, D), BF16), compiler_params=_cp("parallel"))(x, g)


def rmsnorm_bwd(x, g, dxn, dres, *, name, rider=None):
    T, D = x.shape

    def body(x_ref, g_ref, dxn_ref, dres_ref, dx_ref, dxb_ref, dg_ref):
        _, vjp = jax.vjp(_rms, x_ref[...], g_ref[...])
        dx, dg = vjp(dxn_ref[...])
        dx = dx + dres_ref[...]
        dx_ref[...] = dx
        dxb_ref[...] = dx.astype(BF16)

        @pl.when(pl.program_id(0) == 0)
        def _():
            dg_ref[...] = jnp.zeros_like(dg_ref)

        dg_ref[...] += dg

    row = pl.BlockSpec((ROW_TILE, D), lambda i: (i, 0))
    outs, rider_outs = _call(
        body, name=name, grid=(T // ROW_TILE,), in_specs=[row, _full((1, D)), row, row],
        out_specs=[row, row, _full((1, D))], out_shape=[SDS((T, D), F32), SDS((T, D), BF16), SDS((1, D), F32)],
        scratch_shapes=[], sem=("arbitrary",), args=(x, g, dxn, dres), rider=rider)
    return outs if rider is None else (outs, rider_outs)


def _loss_tile(x, g, tgt):
    err = jnp.square(_rms(x, g) - tgt)
    return 0.5 * jnp.sum(jnp.mean(err, axis=-1))


def loss_head(x, g, tgt):
    T, D = x.shape

    def body(x_ref, g_ref, t_ref, loss_ref, dx_ref, dxb_ref, dg_ref):
        loss, vjp = jax.vjp(_loss_tile, x_ref[...], g_ref[...], t_ref[...])
        dx, dg, _ = vjp(jnp.ones((), F32))
        dx_ref[...] = dx
        dxb_ref[...] = dx.astype(BF16)

        @pl.when(pl.program_id(0) == 0)
        def _():
            dg_ref[...] = jnp.zeros_like(dg_ref)
            loss_ref[...] = jnp.zeros_like(loss_ref)

        dg_ref[...] += dg
        loss_ref[...] += jnp.reshape(loss, (1, 1))

    row = pl.BlockSpec((ROW_TILE, D), lambda i: (i, 0))
    return pl.pallas_call(
        body, name="loss_head", grid=(T // ROW_TILE,), in_specs=[row, _full((1, D)), row],
        out_specs=[_full((1, 1)), row, row, _full((1, D))],
        out_shape=[SDS((1, 1), F32), SDS((T, D), F32), SDS((T, D), BF16), SDS((1, D), F32)],
        compiler_params=_cp("arbitrary"))(x, g, tgt)


TILE_BYTES = 1 << 20


def _row_tile(rows, row_bytes):
    for cand in (512, 256, 128, 64, 32, 16, 8):
        if rows % cand == 0 and cand * row_bytes <= TILE_BYTES:
            return cand
    return rows


def adamw(w, g, m, v, *, name):
    R, C = w.shape
    tr = _row_tile(R, C * 4)

    def body(w_ref, g_ref, m_ref, v_ref, d_ref, nm_ref, nv_ref):
        gg = g_ref[...]
        mm = ADAM_B1 * m_ref[...] + (1.0 - ADAM_B1) * gg
        vv = ADAM_B2 * v_ref[...] + (1.0 - ADAM_B2) * jnp.square(gg)
        m_hat = mm / (1.0 - ADAM_B1 ** ADAM_STEP)
        v_hat = vv / (1.0 - ADAM_B2 ** ADAM_STEP)
        d_ref[...] = -ADAM_LR * (m_hat / (jnp.sqrt(v_hat) + ADAM_EPS) + ADAM_WD * w_ref[...])
        nm_ref[...] = mm
        nv_ref[...] = vv

    blk = pl.BlockSpec((tr, C), lambda i: (i, 0))
    return pl.pallas_call(body, name=name, grid=(R // tr,), in_specs=[blk] * 4, out_specs=[blk] * 3,
                          out_shape=[SDS((R, C), F32)] * 3, compiler_params=_cp("parallel"))(w, g, m, v)


def sum_leading(a, *, name):
    n, R, C = a.shape
    tr = _row_tile(R, n * C * 4)

    def body(a_ref, o_ref):
        acc = a_ref[0]
        for j in range(1, n):
            acc = acc + a_ref[j]
        o_ref[...] = acc

    return pl.pallas_call(body, name=name, grid=(R // tr,), in_specs=[pl.BlockSpec((n, tr, C), lambda i: (0, i, 0))],
                          out_specs=pl.BlockSpec((tr, C), lambda i: (i, 0)), out_shape=SDS((R, C), F32),
                          compiler_params=_cp("parallel"))(a)


def chip_sum(pieces, from_sibling, core, *, name):
    _, n, R, C = pieces.shape

    def body(c_ref, a_ref, b_ref, o_ref):
        o_ref[...] = (a_ref[0] + b_ref[...]).astype(BF16)

    tr = _row_tile(R, C * 4)
    blk = pl.BlockSpec((1, tr, C), lambda k, i, c_ref: (k, i, 0))
    mine = pl.BlockSpec((1, 1, tr, C), lambda k, i, c_ref: (c_ref[0], k, i, 0))
    return pl.pallas_call(
        body, name=name, out_shape=SDS((n, R, C), BF16),
        grid_spec=pltpu.PrefetchScalarGridSpec(num_scalar_prefetch=1, grid=(n, R // tr), in_specs=[mine, blk], out_specs=blk),
        compiler_params=_cp("parallel", "parallel"))(core.reshape(1), pieces, from_sibling)


def total_sum(sums, landed, chip, *, name):
    n, R, C = landed.shape

    def body(k_ref, s_ref, l_ref, o_ref):
        acc = s_ref[0].astype(F32)
        for j in range(n):
            acc = acc + l_ref[j].astype(F32)
        o_ref[...] = acc

    tr = _row_tile(R, n * C * 2)
    return pl.pallas_call(
        body, name=name, out_shape=SDS((R, C), F32),
        grid_spec=pltpu.PrefetchScalarGridSpec(
            num_scalar_prefetch=1, grid=(R // tr,),
            in_specs=[pl.BlockSpec((1, tr, C), lambda i, k_ref: (k_ref[0], i, 0)), pl.BlockSpec((n, tr, C), lambda i, k_ref: (0, i, 0))],
            out_specs=pl.BlockSpec((tr, C), lambda i, k_ref: (i, 0))),
        compiler_params=_cp("parallel"))(chip.reshape(1), sums, landed)


def _gmlp_chunk(u, v, z, ln_g, ln_b, wsc, bs_t):
    mu = jnp.mean(v, axis=-1, keepdims=True)
    xc = v - mu
    vn = xc * lax.rsqrt(jnp.mean(xc * xc, axis=-1, keepdims=True) + EPS) * ln_g + ln_b
    gw = A_WIDTH // A_GROUPS
    outs = []
    for g in range(A_GROUPS):
        m = _dot_nn(wsc[g].astype(BF16), vn[:, g * gw:(g + 1) * gw].astype(BF16))
        outs.append(m + bs_t[:, g:g + 1])
    return _silu(z) * (u * jnp.concatenate(outs, axis=1))


def _h_cols(width, idx, rows=CHUNK):
    return pl.BlockSpec((rows, width), lambda i: (i, idx))


def gmlp_fwd(h, ln_g, ln_b, ws, bs_t):
    T = h.shape[0]

    def body(u_ref, v_ref, z_ref, g_ref, b_ref, ws_ref, bs_ref, o_ref):
        wsc = jnp.where(_tril(CHUNK)[None], ws_ref[...], 0.0)
        o_ref[...] = _gmlp_chunk(u_ref[...], v_ref[...], z_ref[...], g_ref[...], b_ref[...], wsc, bs_ref[...]).astype(BF16)

    return pl.pallas_call(
        body, name="gmlp_fwd", grid=(T // CHUNK,),
        in_specs=[_h_cols(A_WIDTH, 0), _h_cols(A_WIDTH, 1), _h_cols(A_WIDTH, 2), _full((1, A_WIDTH)), _full((1, A_WIDTH)),
                  _full((A_GROUPS, CHUNK, CHUNK)), _full((CHUNK, A_GROUPS))],
        out_specs=_h_cols(A_WIDTH, 0), out_shape=SDS((T, A_WIDTH), BF16), compiler_params=_cp("parallel"),
    )(h, h, h, ln_g, ln_b, ws, bs_t)


def gmlp_bwd(h, dy, ln_g, ln_b, ws, bs_t):
    T = h.shape[0]

    def body(u_ref, v_ref, z_ref, dy_ref, g_ref, b_ref, ws_ref, bs_ref, duvz_ref, dg_ref, db_ref, dws_ref, dbs_ref):
        tri = _tril(CHUNK)[None]
        wsc = jnp.where(tri, ws_ref[...], 0.0)
        _, vjp = jax.vjp(_gmlp_chunk, u_ref[...], v_ref[...], z_ref[...], g_ref[...], b_ref[...], wsc, bs_ref[...])
        du, dv, dz, dg, db, dws, dbs = vjp(dy_ref[...])
        duvz_ref[:, :A_WIDTH] = du.astype(BF16)
        duvz_ref[:, A_WIDTH:2 * A_WIDTH] = dv.astype(BF16)
        duvz_ref[:, 2 * A_WIDTH:] = dz.astype(BF16)

        @pl.when(pl.program_id(0) == 0)
        def _():
            dg_ref[...] = jnp.zeros_like(dg_ref)
            db_ref[...] = jnp.zeros_like(db_ref)
            dws_ref[...] = jnp.zeros_like(dws_ref)
            dbs_ref[...] = jnp.zeros_like(dbs_ref)

        dg_ref[...] += dg
        db_ref[...] += db
        dws_ref[...] += jnp.where(tri, dws, 0.0)
        dbs_ref[...] += dbs

    pshapes = [(1, A_WIDTH), (1, A_WIDTH), (A_GROUPS, CHUNK, CHUNK), (CHUNK, A_GROUPS)]
    return pl.pallas_call(
        body, name="gmlp_bwd", grid=(T // CHUNK,),
        in_specs=[_h_cols(A_WIDTH, 0), _h_cols(A_WIDTH, 1), _h_cols(A_WIDTH, 2), _h_cols(A_WIDTH, 0)] + [_full(s) for s in pshapes],
        out_specs=[_h_cols(3 * A_WIDTH, 0)] + [_full(s) for s in pshapes],
        out_shape=[SDS((T, 3 * A_WIDTH), BF16)] + [SDS(s, F32) for s in pshapes],
        compiler_params=_cp("arbitrary"),
    )(h, h, h, dy, ln_g, ln_b, ws, bs_t)


CONV_ROWS = 256
CONV_COLS = 512


def _taps(xe, w, rows):
    K = w.shape[0]
    acc = None
    for k in range(K):
        off = SUBLANES - (K - 1) + k
        term = w[k:k + 1, :] * xe[off:off + rows, :]
        acc = term if acc is None else acc + term
    return acc


def _ssd_conv_tile(x, halo, w, b):
    return _silu(_taps(jnp.concatenate([halo, x], axis=0), w, x.shape[0]) + b)


def _halo_spec(cols, col_idx, nt=None):
    rpb = CONV_ROWS // SUBLANES
    if nt is None:
        return pl.BlockSpec((SUBLANES, cols), lambda c, i: (jnp.maximum(i * rpb - 1, 0), col_idx(c)))
    return pl.BlockSpec((SUBLANES, cols), lambda c, j: (jnp.maximum((nt - 1 - j) * rpb - 1, 0), col_idx(c)))


def ssd_conv_fwd(h, w, b):
    T = h.shape[0]
    nc = B_XBC // CONV_COLS
    base = (3 * A_WIDTH + B_WIDTH) // CONV_COLS

    def body(x_ref, halo_ref, w_ref, b_ref, o_ref):
        halo = jnp.where(pl.program_id(1) > 0, halo_ref[...], 0.0)
        o_ref[...] = _ssd_conv_tile(x_ref[...], halo, w_ref[...], b_ref[...])

    return pl.pallas_call(
        body, name="ssd_conv_fwd", grid=(nc, T // CONV_ROWS),
        in_specs=[pl.BlockSpec((CONV_ROWS, CONV_COLS), lambda c, i: (i, base + c)), _halo_spec(CONV_COLS, lambda c: base + c),
                  pl.BlockSpec((B_CONV, CONV_COLS), lambda c, i: (0, c)), pl.BlockSpec((1, CONV_COLS), lambda c, i: (0, c))],
        out_specs=pl.BlockSpec((CONV_ROWS, CONV_COLS), lambda c, i: (i, c)),
        out_shape=SDS((T, B_XBC), F32), compiler_params=_cp("parallel", "parallel"),
    )(h, h, w, b)


def ssd_conv_bwd(h, dy, w, b, rider=None):
    T = h.shape[0]
    nc = B_XBC // CONV_COLS
    nt = T // CONV_ROWS
    base = (3 * A_WIDTH + B_WIDTH) // CONV_COLS

    def body(x_ref, halo_ref, dy_ref, w_ref, b_ref, dx_ref, dw_ref, db_ref, carry_ref):
        j = pl.program_id(1)
        halo = jnp.where(j < nt - 1, halo_ref[...], 0.0)
        _, vjp = jax.vjp(_ssd_conv_tile, x_ref[...], halo, w_ref[...], b_ref[...])
        dx, dhalo, dw, db = vjp(dy_ref[...])

        @pl.when(j == 0)
        def _():
            carry_ref[...] = jnp.zeros_like(carry_ref)
            dw_ref[...] = jnp.zeros_like(dw_ref)
            db_ref[...] = jnp.zeros_like(db_ref)

        tail = dx[CONV_ROWS - SUBLANES:, :] + carry_ref[...]
        dx_ref[...] = jnp.concatenate([dx[:CONV_ROWS - SUBLANES, :], tail], axis=0).astype(BF16)
        carry_ref[...] = dhalo
        dw_ref[...] += dw
        db_ref[...] += db

    outs, rider_outs = _call(
        body, name="ssd_conv_bwd", grid=(nc, nt),
        in_specs=[pl.BlockSpec((CONV_ROWS, CONV_COLS), lambda c, j: (nt - 1 - j, base + c)),
                  _halo_spec(CONV_COLS, lambda c: base + c, nt),
                  pl.BlockSpec((CONV_ROWS, CONV_COLS), lambda c, j: (nt - 1 - j, c)),
                  pl.BlockSpec((B_CONV, CONV_COLS), lambda c, j: (0, c)), pl.BlockSpec((1, CONV_COLS), lambda c, j: (0, c))],
        out_specs=[pl.BlockSpec((CONV_ROWS, CONV_COLS), lambda c, j: (nt - 1 - j, c)),
                   pl.BlockSpec((B_CONV, CONV_COLS), lambda c, j: (0, c)), pl.BlockSpec((1, CONV_COLS), lambda c, j: (0, c))],
        out_shape=[SDS((T, B_XBC), BF16), SDS((B_CONV, B_XBC), F32), SDS((1, B_XBC), F32)],
        scratch_shapes=[pltpu.VMEM((SUBLANES, CONV_COLS), F32)], sem=("parallel", "arbitrary"),
        args=(h, h, dy, w, b), rider=rider)
    return outs if rider is None else (outs, rider_outs)


def _sconv_tile(bg, cg, hx, z, cg_halo, hx_halo, w):
    ch = jnp.concatenate([cg_halo * hx_halo, cg * hx], axis=0)
    return _silu(z) * (bg * _taps(ch, w, bg.shape[0]))


def sconv_fwd(h, w):
    T = h.shape[0]
    nc = C_WIDTH // CONV_COLS

    def col(seg):
        return pl.BlockSpec((CONV_ROWS, CONV_COLS), lambda c, i: (i, seg * nc + c))

    def body(bg_ref, cg_ref, hx_ref, z_ref, cgh_ref, hxh_ref, w_ref, o_ref):
        first = pl.program_id(1) == 0
        cgh = jnp.where(first, 0.0, cgh_ref[...])
        hxh = jnp.where(first, 0.0, hxh_ref[...])
        o_ref[...] = _sconv_tile(bg_ref[...], cg_ref[...], hx_ref[...], z_ref[...], cgh, hxh, w_ref[...]).astype(BF16)

    return pl.pallas_call(
        body, name="sconv_fwd", grid=(nc, T // CONV_ROWS),
        in_specs=[col(0), col(1), col(2), col(3), _halo_spec(CONV_COLS, lambda c: nc + c), _halo_spec(CONV_COLS, lambda c: 2 * nc + c),
                  pl.BlockSpec((C_CONV, CONV_COLS), lambda c, i: (0, c))],
        out_specs=pl.BlockSpec((CONV_ROWS, CONV_COLS), lambda c, i: (i, c)),
        out_shape=SDS((T, C_WIDTH), BF16), compiler_params=_cp("parallel", "parallel"),
    )(h, h, h, h, h, h, w)


def sconv_bwd(h, dy, w):
    T = h.shape[0]
    nc = C_WIDTH // CONV_COLS
    nt = T // CONV_ROWS

    def col(seg):
        return pl.BlockSpec((CONV_ROWS, CONV_COLS), lambda c, j: (nt - 1 - j, seg * nc + c))

    def body(bg_ref, cg_ref, hx_ref, z_ref, cgh_ref, hxh_ref, dy_ref, w_ref, dbg_ref, dcg_ref, dhx_ref, dz_ref, dw_ref, ccg_ref, chx_ref):
        j = pl.program_id(1)
        first = j == nt - 1
        cgh = jnp.where(first, 0.0, cgh_ref[...])
        hxh = jnp.where(first, 0.0, hxh_ref[...])
        _, vjp = jax.vjp(_sconv_tile, bg_ref[...], cg_ref[...], hx_ref[...], z_ref[...], cgh, hxh, w_ref[...])
        dbg, dcg, dhx, dz, dcgh, dhxh, dw = vjp(dy_ref[...])

        @pl.when(j == 0)
        def _():
            ccg_ref[...] = jnp.zeros_like(ccg_ref)
            chx_ref[...] = jnp.zeros_like(chx_ref)
            dw_ref[...] = jnp.zeros_like(dw_ref)

        def with_carry(d, carry_ref):
            tail = d[CONV_ROWS - SUBLANES:, :] + carry_ref[...]
            return jnp.concatenate([d[:CONV_ROWS - SUBLANES, :], tail], axis=0).astype(BF16)

        dbg_ref[...] = dbg.astype(BF16)
        dz_ref[...] = dz.astype(BF16)
        dcg_ref[...] = with_carry(dcg, ccg_ref)
        dhx_ref[...] = with_carry(dhx, chx_ref)
        ccg_ref[...] = dcgh
        chx_ref[...] = dhxh
        dw_ref[...] += dw

    out_row = pl.BlockSpec((CONV_ROWS, CONV_COLS), lambda c, j: (nt - 1 - j, c))
    wspec = pl.BlockSpec((C_CONV, CONV_COLS), lambda c, j: (0, c))
    return pl.pallas_call(
        body, name="sconv_bwd", grid=(nc, nt),
        in_specs=[col(0), col(1), col(2), col(3), _halo_spec(CONV_COLS, lambda c: nc + c, nt), _halo_spec(CONV_COLS, lambda c: 2 * nc + c, nt),
                  out_row, wspec],
        out_specs=[out_row] * 4 + [wspec],
        out_shape=[SDS((T, C_WIDTH), BF16)] * 4 + [SDS((C_CONV, C_WIDTH), F32)],
        scratch_shapes=[pltpu.VMEM((SUBLANES, CONV_COLS), F32)] * 2,
        compiler_params=_cp("parallel", "arbitrary"),
    )(h, h, h, h, h, h, dy, w)


def _softplus(x):
    return jnp.maximum(x, 0.0) + jnp.log(1.0 + jnp.exp(-jnp.abs(x)))


def _ssd_chunk(xs, bm, cm, dtr, z, prev, dt_bias, a_log, d_skip, norm_g):
    tril = _tril(CHUNK)
    dt = _softplus(dtr + dt_bias)
    adt = dt * (-jnp.exp(a_log))
    a_cs = jnp.dot(tril.astype(F32), adt, precision=lax.Precision.HIGHEST, preferred_element_type=F32)
    a_cs_t = a_cs.T
    a_last = a_cs[CHUNK - 1:CHUNK, :]
    dt_f = _spread_heads(dt, B_HEAD_DIM)
    dec_f = _spread_heads(jnp.exp(a_last - a_cs), B_HEAD_DIM)
    ecs_f = _spread_heads(jnp.exp(a_cs), B_HEAD_DIM)
    dsk_f = _spread_heads(d_skip, B_HEAD_DIM)
    cd_t = jnp.exp(a_cs_t[:, CHUNK - 1:CHUNK])
    xdt = xs * dt_f
    xdd = xdt * dec_f
    colb = _spread_heads(a_cs, CHUNK)
    rowb = jnp.concatenate([jnp.broadcast_to(a_cs_t[hh:hh + 1, :], (CHUNK, CHUNK)) for hh in range(B_HEADS)], axis=1)
    wide = (CHUNK, B_HEADS * CHUNK)
    keep = lax.broadcasted_iota(jnp.int32, wide, 0) >= lax.broadcasted_iota(jnp.int32, wide, 1) % CHUNK
    decay = jnp.exp(jnp.where(keep, colb - rowb, -jnp.inf))
    hpg = B_HEADS // B_GROUPS
    gw = B_WIDTH // B_GROUPS
    low_half = lax.broadcasted_iota(jnp.int32, (CHUNK, 2 * B_HEAD_DIM), 1) < B_HEAD_DIM
    ys, nxt = [], []
    for g in range(B_GROUPS):
        bg = bm[:, g * B_STATE:(g + 1) * B_STATE].astype(BF16)
        cg = cm[:, g * B_STATE:(g + 1) * B_STATE].astype(BF16)
        cb = _dot_nt(cg, bg)
        cbl = (decay[:, g * hpg * CHUNK:(g + 1) * hpg * CHUNK] * jnp.concatenate([cb] * hpg, axis=1)).astype(BF16)
        pg = prev[g * gw:(g + 1) * gw, :]
        y_off = _dot_nt(cg, pg.astype(BF16)) * ecs_f[:, g * gw:(g + 1) * gw]
        st = _dot_tn(xdd[:, g * gw:(g + 1) * gw].astype(BF16), bg)
        cd = jnp.concatenate([jnp.broadcast_to(cd_t[g * hpg + r:g * hpg + r + 1, :], (B_HEAD_DIM, 1)) for r in range(hpg)], axis=0)
        nxt.append(pg * cd + st)
        pairs = []
        for j in range(hpg // 2):
            xp = xdt[:, g * gw + 2 * j * B_HEAD_DIM:g * gw + 2 * (j + 1) * B_HEAD_DIM]
            rhs = jnp.concatenate([jnp.where(low_half, xp, 0.0), jnp.where(low_half, 0.0, xp)], axis=0).astype(BF16)
            pairs.append(_dot_nn(cbl[:, 2 * j * CHUNK:2 * (j + 1) * CHUNK], rhs))
        ys.append(jnp.concatenate(pairs, axis=1) + y_off)
    y = (jnp.concatenate(ys, axis=1) + dsk_f * xs) * _silu(z)
    outs = []
    for g in range(B_GROUPS):
        yg = y[:, g * gw:(g + 1) * gw]
        outs.append(yg * lax.rsqrt(jnp.mean(yg * yg, axis=-1, keepdims=True) + EPS))
    return jnp.concatenate(outs, axis=1) * norm_g, jnp.concatenate(nxt, axis=0)


def _split3(v):
    hi = v.astype(BF16)
    r1 = v - hi.astype(F32)
    mid = r1.astype(BF16)
    return hi, mid, (r1 - mid.astype(F32)).astype(BF16)


def _head_one_hot(width, parts):
    n = B_HEADS * width
    shape = (parts * LANES, n)
    return (lax.broadcasted_iota(jnp.int32, shape, 0) % LANES == lax.broadcasted_iota(jnp.int32, shape, 1) // width).astype(BF16)


@functools.partial(jax.custom_vjp, nondiff_argnums=(1,))
def _spread_heads(v, width):
    return _dot_nn(jnp.concatenate(_split3(v), axis=1), _head_one_hot(width, 3))


def _spread_heads_fwd(v, width):
    return _spread_heads(v, width), None


def _spread_heads_bwd(width, _, g):
    return (_dot_nt(jnp.concatenate(_split3(g), axis=1), jnp.concatenate([_head_one_hot(width, 1)] * 3, axis=1)),)


_spread_heads.defvjp(_spread_heads_fwd, _spread_heads_bwd)


_SSD_PARAM_SHAPES = [(1, LANES), (1, LANES), (1, LANES), (1, B_WIDTH)]
_STATE_SHAPE = (B_WIDTH, B_STATE)


def ssd_fwd(xbc, dtr, h, dt_bias, a_log, d_skip, norm_g, rider=None):
    T = xbc.shape[0]
    nc = T // CHUNK

    def body(xs_ref, b_ref, c_ref, dt_ref, z_ref, p0, p1, p2, p3, y_ref, st_ref, state):
        @pl.when(pl.program_id(0) == 0)
        def _():
            state[...] = jnp.zeros_like(state)

        prev = state[...]
        st_ref[0] = prev
        yb, nxt = _ssd_chunk(xs_ref[...], b_ref[...], c_ref[...], dt_ref[...], z_ref[...], prev, p0[...], p1[...], p2[...], p3[...])
        y_ref[...] = yb.astype(BF16)
        state[...] = nxt

    outs, rider_outs = _call(
        body, name="ssd_fwd", grid=(nc,),
        in_specs=[_h_cols(B_WIDTH, 0), _h_cols(B_GROUPS * B_STATE, 2), _h_cols(B_GROUPS * B_STATE, 3), _h_cols(LANES, 0), _h_cols(B_WIDTH, 3)]
        + [_full(s) for s in _SSD_PARAM_SHAPES],
        out_specs=[_h_cols(B_WIDTH, 0), pl.BlockSpec((1,) + _STATE_SHAPE, lambda i: (i, 0, 0))],
        out_shape=[SDS((T, B_WIDTH), BF16), SDS((nc,) + _STATE_SHAPE, F32)],
        scratch_shapes=[pltpu.VMEM(_STATE_SHAPE, F32)], sem=("arbitrary",),
        args=(xbc, xbc, xbc, dtr, h, dt_bias, a_log, d_skip, norm_g), rider=rider)
    return outs if rider is None else (outs, rider_outs)


def ssd_bwd(xbc, dtr, h, states, dy, dt_bias, a_log, d_skip, norm_g, rider=None):
    T = xbc.shape[0]
    nc = T // CHUNK

    def rev(width, idx):
        return pl.BlockSpec((CHUNK, width), lambda j: (nc - 1 - j, idx))

    def body(xs_ref, b_ref, c_ref, dt_ref, z_ref, st_ref, dy_ref, p0, p1, p2, p3,
             dxbc_ref, ddt_ref, dz_ref, g0, g1, g2, g3, dstate):
        @pl.when(pl.program_id(0) == 0)
        def _():
            dstate[...] = jnp.zeros_like(dstate)
            for gref in (g0, g1, g2, g3):
                gref[...] = jnp.zeros_like(gref)

        _, vjp = jax.vjp(_ssd_chunk, xs_ref[...], b_ref[...], c_ref[...], dt_ref[...], z_ref[...], st_ref[0],
                         p0[...], p1[...], p2[...], p3[...])
        dxs, dbm, dcm, ddt, dz, dprev, d0, d1, d2, d3 = vjp((dy_ref[...], dstate[...]))
        dxbc_ref[:, :B_WIDTH] = dxs
        dxbc_ref[:, B_WIDTH:B_WIDTH + gn] = dbm
        dxbc_ref[:, B_WIDTH + gn:] = dcm
        ddt_ref[...] = ddt
        dz_ref[...] = dz.astype(BF16)
        dstate[...] = dprev
        g0[...] += d0
        g1[...] += d1
        g2[...] += d2
        g3[...] += d3

    gn = B_GROUPS * B_STATE
    outs, rider_outs = _call(
        body, name="ssd_bwd", grid=(nc,),
        in_specs=[rev(B_WIDTH, 0), rev(gn, 2), rev(gn, 3), rev(LANES, 0), rev(B_WIDTH, 3),
                  pl.BlockSpec((1,) + _STATE_SHAPE, lambda j: (nc - 1 - j, 0, 0)), rev(B_WIDTH, 1)]
        + [_full(s) for s in _SSD_PARAM_SHAPES],
        out_specs=[rev(B_XBC, 0), rev(LANES, 0), rev(B_WIDTH, 0)] + [_full(s) for s in _SSD_PARAM_SHAPES],
        out_shape=[SDS((T, B_XBC), F32), SDS((T, LANES), F32), SDS((T, B_WIDTH), BF16)]
        + [SDS(s, F32) for s in _SSD_PARAM_SHAPES],
        scratch_shapes=[pltpu.VMEM(_STATE_SHAPE, F32)], sem=("arbitrary",),
        args=(xbc, xbc, xbc, dtr, h, states, dy, dt_bias, a_log, d_skip, norm_g), rider=rider)
    return outs if rider is None else (outs, rider_outs)


ATT_SCALE = D_HEAD_DIM ** -0.5
Q_COL, K_COL, V_COL, Z_COL = (4 * C_WIDTH // LANES + i * D_HEADS for i in range(4))


ATT_NBLK = ATT_SUPER // ATT_BLOCK


def _res_rows(r, first, count, dil):
    return pl.ds(r + dil * first, count) if dil == 1 else pl.ds(r + dil * first, count, stride=dil)


def _blocks(ref, dil, dtype=None):
    n = ATT_SUPER // dil
    parts = []
    for r in range(dil):
        v = ref[_res_rows(r, 0, n, dil), :]
        parts.append((v if dtype is None else v.astype(dtype)).reshape(n // ATT_BLOCK, ATT_BLOCK, D_HEAD_DIM))
    return parts[0] if dil == 1 else jnp.concatenate(parts, axis=0)


def _blocks_before(cur_ref, prev_ref, dil, dtype):
    n = ATT_SUPER // dil
    parts = []
    for r in range(dil):
        v = prev_ref[_res_rows(r, n - ATT_BLOCK, ATT_BLOCK, dil), :]
        if n > ATT_BLOCK:
            v = jnp.concatenate([v, cur_ref[_res_rows(r, 0, n - ATT_BLOCK, dil), :]], axis=0)
        parts.append(v.astype(dtype).reshape(n // ATT_BLOCK, ATT_BLOCK, D_HEAD_DIM))
    return parts[0] if dil == 1 else jnp.concatenate(parts, axis=0)


def _blocks_after(cur_ref, next_ref, dil, dtype=None):
    n = ATT_SUPER // dil
    parts = []
    for r in range(dil):
        v = next_ref[_res_rows(r, 0, ATT_BLOCK, dil), :]
        if n > ATT_BLOCK:
            v = jnp.concatenate([cur_ref[_res_rows(r, ATT_BLOCK, n - ATT_BLOCK, dil), :], v], axis=0)
        parts.append((v if dtype is None else v.astype(dtype)).reshape(n // ATT_BLOCK, ATT_BLOCK, D_HEAD_DIM))
    return parts[0] if dil == 1 else jnp.concatenate(parts, axis=0)


def _unblock(ref, val, dil, add=False):
    n = ATT_SUPER // dil
    nb = n // ATT_BLOCK
    for r in range(dil):
        v = val[r * nb:(r + 1) * nb].reshape(n, D_HEAD_DIM)
        if add:
            ref[_res_rows(r, 0, n, dil), :] += v
        else:
            ref[_res_rows(r, 0, n, dil), :] = v


def _att_masks(dil, edge_ok, edge_last=False):
    shape = (ATT_NBLK, ATT_BLOCK, ATT_BLOCK)
    blk = lax.broadcasted_iota(jnp.int32, shape, 0)
    row = lax.broadcasted_iota(jnp.int32, shape, 1)
    col = lax.broadcasted_iota(jnp.int32, shape, 2)
    nb = ATT_NBLK // dil
    at_edge = (blk % nb) == (nb - 1 if edge_last else 0)
    return col <= row, jnp.logical_and(col >= row, jnp.logical_or(jnp.logical_not(at_edge), edge_ok))


def _bdot_nt(a, b):
    return lax.dot_general(a, b, (((2,), (2,)), ((0,), (0,))), preferred_element_type=F32)


def _bdot_nn(a, b):
    return lax.dot_general(a, b, (((2,), (1,)), ((0,), (0,))), preferred_element_type=F32)


def _bdot_tn(a, b):
    return lax.dot_general(a, b, (((1,), (1,)), ((0,), (0,))), preferred_element_type=F32)


def _att_spec(col0, shift=0, last=None):
    def imap(hh, n):
        m = n + shift
        if shift < 0:
            m = jnp.maximum(m, 0)
        if shift > 0:
            m = jnp.minimum(m, last)
        return (m, col0 + hh)
    return pl.BlockSpec((ATT_SUPER, D_HEAD_DIM), imap)


def _att_out_spec():
    return pl.BlockSpec((ATT_SUPER, D_HEAD_DIM), lambda hh, n: (n, hh))


def attn_fwd(h):
    T = h.shape[0]
    npat = len(D_PATTERNS)

    def body(q_ref, kc_ref, kp_ref, vc_ref, vp_ref, z_ref, yd_ref, o_ref, lse_ref, *scratch):
        o_s, l_s = scratch[:npat], scratch[npat:]
        has_prev = pl.program_id(1) > 0
        for pi, (_, dil) in enumerate(D_PATTERNS):
            mask_c, mask_p = _att_masks(dil, has_prev)
            q = _blocks(q_ref, dil, BF16)
            kc, vc = _blocks(kc_ref, dil, BF16), _blocks(vc_ref, dil, BF16)
            kp, vp = _blocks_before(kc_ref, kp_ref, dil, BF16), _blocks_before(vc_ref, vp_ref, dil, BF16)
            s_c = jnp.where(mask_c, _bdot_nt(q, kc) * ATT_SCALE, -jnp.inf)
            s_p = jnp.where(mask_p, _bdot_nt(q, kp) * ATT_SCALE, -jnp.inf)
            m = jnp.maximum(jnp.max(s_c, axis=-1, keepdims=True), jnp.max(s_p, axis=-1, keepdims=True))
            p_c = jnp.exp(s_c - m)
            p_p = jnp.exp(s_p - m)
            l = jnp.sum(p_c, axis=-1, keepdims=True) + jnp.sum(p_p, axis=-1, keepdims=True)
            o = _bdot_nn((p_c / l).astype(BF16), vc) + _bdot_nn((p_p / l).astype(BF16), vp)
            _unblock(o_s[pi], o, dil)
            _unblock(l_s[pi], jnp.broadcast_to(m + jnp.log(l), o.shape), dil)
        lses = [l_s[pi][...] for pi in range(npat)]
        mx = functools.reduce(jnp.maximum, lses)
        ws = [jnp.exp(l - mx) for l in lses]
        den = functools.reduce(lambda a, b: a + b, ws)
        o = functools.reduce(lambda a, b: a + b, [(w / den) * o_s[pi][...] for pi, w in enumerate(ws)])
        o_ref[...] = o
        lse_ref[...] = mx + jnp.log(den)
        yd_ref[...] = (_silu(z_ref[...]) * o).astype(BF16)

    n_super = T // ATT_SUPER
    return pl.pallas_call(
        body, name="attn_fwd", grid=(D_HEADS, n_super),
        in_specs=[_att_spec(Q_COL), _att_spec(K_COL), _att_spec(K_COL, -1), _att_spec(V_COL), _att_spec(V_COL, -1), _att_spec(Z_COL)],
        out_specs=[_att_out_spec()] * 3,
        out_shape=[SDS((T, D_HEADS * D_HEAD_DIM), BF16), SDS((T, D_HEADS * D_HEAD_DIM), F32), SDS((T, D_HEADS * D_HEAD_DIM), F32)],
        scratch_shapes=[pltpu.VMEM((ATT_SUPER, D_HEAD_DIM), F32)] * (2 * npat),
        compiler_params=_cp("parallel", "arbitrary"),
    )(h, h, h, h, h, h)


def _dsilu(z):
    s = jax.nn.sigmoid(z)
    return s * (1.0 + z * (1.0 - s))


def attn_bwd_dq(h, o, lse, dy):
    T = h.shape[0]
    dy_col = C_WIDTH // LANES

    def body(q_ref, kc_ref, kp_ref, vc_ref, vp_ref, z_ref, o_ref, lse_ref, dy_ref, dq_ref, dz_ref, do_s, dd_s, dq_s):
        has_prev = pl.program_id(1) > 0
        z, oo, dyd = z_ref[...], o_ref[...], dy_ref[...]
        do = dyd * _silu(z)
        dz_ref[...] = (dyd * oo * _dsilu(z)).astype(BF16)
        do_s[...] = do
        dd_s[...] = jnp.broadcast_to(jnp.sum(do * oo, axis=-1, keepdims=True), (ATT_SUPER, D_HEAD_DIM))
        for pi, (_, dil) in enumerate(D_PATTERNS):
            mask_c, mask_p = _att_masks(dil, has_prev)
            q = _blocks(q_ref, dil, BF16)
            kc, vc = _blocks(kc_ref, dil, BF16), _blocks(vc_ref, dil, BF16)
            kp, vp = _blocks_before(kc_ref, kp_ref, dil, BF16), _blocks_before(vc_ref, vp_ref, dil, BF16)
            lse_b, dd_b, do_b = _blocks(lse_ref, dil), _blocks(dd_s, dil), _blocks(do_s, dil, BF16)
            p_c = jnp.where(mask_c, jnp.exp(_bdot_nt(q, kc) * ATT_SCALE - lse_b), 0.0)
            p_p = jnp.where(mask_p, jnp.exp(_bdot_nt(q, kp) * ATT_SCALE - lse_b), 0.0)
            ds_c = p_c * (_bdot_nt(do_b, vc) - dd_b) * ATT_SCALE
            ds_p = p_p * (_bdot_nt(do_b, vp) - dd_b) * ATT_SCALE
            dq = _bdot_nn(ds_c.astype(BF16), kc) + _bdot_nn(ds_p.astype(BF16), kp)
            _unblock(dq_s, dq, dil, add=pi > 0)
        dq_ref[...] = dq_s[...].astype(BF16)

    n_super = T // ATT_SUPER
    blk = (ATT_SUPER, D_HEAD_DIM)
    return pl.pallas_call(
        body, name="attn_bwd_dq", grid=(D_HEADS, n_super),
        in_specs=[_att_spec(Q_COL), _att_spec(K_COL), _att_spec(K_COL, -1), _att_spec(V_COL), _att_spec(V_COL, -1), _att_spec(Z_COL),
                  _att_spec(0), _att_spec(0), _att_spec(dy_col)],
        out_specs=[_att_out_spec()] * 2,
        out_shape=[SDS((T, D_HEADS * D_HEAD_DIM), BF16)] * 2,
        scratch_shapes=[pltpu.VMEM(blk, F32)] * 3,
        compiler_params=_cp("parallel", "arbitrary"),
    )(h, h, h, h, h, h, o, lse, dy)


def attn_bwd_dkv(h, o, lse, dy):
    T = h.shape[0]
    n_super = T // ATT_SUPER
    last = n_super - 1
    dy_col = C_WIDTH // LANES
    npat = len(D_PATTERNS)

    def body(k_ref, v_ref, qc_ref, qn_ref, zc_ref, zn_ref, oc_ref, on_ref, lc_ref, ln_ref, dyc_ref, dyn_ref, dk_ref, dv_ref,
             do_c, do_n, dd_c, dd_n, dk_s, dv_s):
        do_s, dd_s = (do_c, do_n), (dd_c, dd_n)
        has_next = pl.program_id(1) < last
        for i, (z_ref, oo_ref, dyd_ref) in enumerate(((zc_ref, oc_ref, dyc_ref), (zn_ref, on_ref, dyn_ref))):
            do = dyd_ref[...] * _silu(z_ref[...])
            do_s[i][...] = do
            dd_s[i][...] = jnp.broadcast_to(jnp.sum(do * oo_ref[...], axis=-1, keepdims=True), (ATT_SUPER, D_HEAD_DIM))
        for pi, (_, dil) in enumerate(D_PATTERNS):
            mask_c, mask_p = _att_masks(dil, has_next, edge_last=True)
            kb, vb = _blocks(k_ref, dil, BF16), _blocks(v_ref, dil, BF16)
            dk = dv = None
            for own in (True, False):
                if own:
                    q, lse_b = _blocks(qc_ref, dil, BF16), _blocks(lc_ref, dil)
                    do_b, dd_b = _blocks(do_c, dil, BF16), _blocks(dd_c, dil)
                else:
                    q, lse_b = _blocks_after(qc_ref, qn_ref, dil, BF16), _blocks_after(lc_ref, ln_ref, dil)
                    do_b, dd_b = _blocks_after(do_c, do_n, dil, BF16), _blocks_after(dd_c, dd_n, dil)
                p = jnp.where(mask_c if own else mask_p, jnp.exp(_bdot_nt(q, kb) * ATT_SCALE - lse_b), 0.0)
                ds = p * (_bdot_nt(do_b, vb) - dd_b) * ATT_SCALE
                dv_t = _bdot_tn(p.astype(BF16), do_b)
                dk_t = _bdot_tn(ds.astype(BF16), q)
                dk = dk_t if dk is None else dk + dk_t
                dv = dv_t if dv is None else dv + dv_t
            _unblock(dk_s, dk, dil, add=pi > 0)
            _unblock(dv_s, dv, dil, add=pi > 0)
        dk_ref[...] = dk_s[...].astype(BF16)
        dv_ref[...] = dv_s[...].astype(BF16)

    blk = (ATT_SUPER, D_HEAD_DIM)

    def pair(col0):
        return [_att_spec(col0), _att_spec(col0, 1, last)]

    return pl.pallas_call(
        body, name="attn_bwd_dkv", grid=(D_HEADS, n_super),
        in_specs=[_att_spec(K_COL), _att_spec(V_COL)] + pair(Q_COL) + pair(Z_COL) + pair(0) + pair(0) + pair(dy_col),
        out_specs=[_att_out_spec()] * 2,
        out_shape=[SDS((T, D_HEADS * D_HEAD_DIM), BF16)] * 2,
        scratch_shapes=[pltpu.VMEM(blk, F32)] * 6,
        compiler_params=_cp("parallel", "arbitrary"),
    )(h, h, h, h, h, h, o, o, lse, lse, dy, dy)


ANY = pl.BlockSpec(memory_space=pl.ANY)
COMM_PARAMS = pltpu.CompilerParams()


def _place():
    x, y, c = lax.axis_index("x"), lax.axis_index("y"), lax.axis_index("c")
    return x, y, c, [(1 - x, y), (x, 1 - y), (1 - x, 1 - y)]


def _rcopy(src, dst, ssem, rsem, dev):
    return pltpu.make_async_remote_copy(src_ref=src, dst_ref=dst, send_sem=ssem, recv_sem=rsem, device_id=dev, device_id_type=MESH)


def gather_rider(arrs, fractions=(0.0, 0.6, 1.0)):
    n = len(arrs)
    per = 7

    def to_chips(ins, outs, ssem, rsem):
        x, y, c, chips = _place()
        return [_rcopy(ins[a].at[c], outs[a].at[2 * x + y, c], ssem.at[per * a + j], rsem.at[per * a + j], (px, py, c))
                for a in range(n) for j, (px, py) in enumerate(chips)]

    def passed_on(outs, ssem, rsem, half):
        x, y, c, chips = _place()
        cps = []
        for a in range(n):
            for j, (px, py) in enumerate(chips):
                slot = outs[a].at[2 * px + py, half(c)]
                cps.append(_rcopy(slot, slot, ssem.at[per * a + 3 + j], rsem.at[per * a + 3 + j], (x, y, 1 - c)))
        return cps

    def own(ins, outs, ssem, rsem):
        x, y, c, _ = _place()
        return [_rcopy(ins[a], outs[a].at[2 * x + y], ssem.at[per * a + 6], rsem.at[per * a + 6], (x, y, 1 - c)) for a in range(n)]

    def start(ins, outs, ssem, rsem):
        for cp in to_chips(ins, outs, ssem, rsem) + own(ins, outs, ssem, rsem):
            cp.start()

    def pass_on(ins, outs, ssem, rsem):
        x, y, c, chips = _place()
        landed = [_rcopy(outs[a].at[2 * px + py, c], outs[a].at[2 * px + py, c], ssem.at[per * a + j], rsem.at[per * a + j], (px, py, c))
                  for a in range(n) for j, (px, py) in enumerate(chips)]
        for arrival, cp in zip(landed, passed_on(outs, ssem, rsem, lambda c: c)):
            arrival.wait_recv()
            cp.start()

    def finish(ins, outs, ssem, rsem):
        for cp in passed_on(outs, ssem, rsem, lambda c: 1 - c):
            cp.wait_recv()
        for cp in to_chips(ins, outs, ssem, rsem) + passed_on(outs, ssem, rsem, lambda c: c):
            cp.wait_send()
        for cp in own(ins, outs, ssem, rsem):
            cp.wait()

    return Rider(arrs, [SDS((N_CHIPS,) + a.shape, a.dtype) for a in arrs], per * n,
                 [(fractions[0], start), (fractions[1], pass_on), (fractions[2], finish)])


def _copies_rider(ins, out_shapes, n_sems, make):
    def start(*refs):
        for cp in make(*refs):
            cp.start()

    def finish(*refs):
        for cp in make(*refs):
            cp.wait()

    return Rider(ins, out_shapes, n_sems, [(0.0, start), (1.0, finish)])


def swap_halves_rider(arrs):
    def make(ins, outs, ssem, rsem):
        x, y, c, _ = _place()
        return [_rcopy(ins[a].at[1 - c], outs[a], ssem.at[a], rsem.at[a], (x, y, 1 - c)) for a in range(len(arrs))]
    return _copies_rider(arrs, [SDS(a.shape[1:], a.dtype) for a in arrs], len(arrs), make)


def scatter_rider(arrs):
    def make(ins, outs, ssem, rsem):
        x, y, c, chips = _place()
        return [_rcopy(ins[a].at[2 * px + py], outs[a].at[j], ssem.at[3 * a + j], rsem.at[3 * a + j], (px, py, c))
                for a in range(len(arrs)) for j, (px, py) in enumerate(chips)]
    return _copies_rider(arrs, [SDS((N_CHIPS - 1,) + a.shape[1:], a.dtype) for a in arrs], 3 * len(arrs), make)


def swap_rider(arrs):
    def make(ins, outs, ssem, rsem):
        x, y, c, _ = _place()
        return [_rcopy(ins[a], outs[a], ssem.at[a], rsem.at[a], (x, y, 1 - c)) for a in range(len(arrs))]
    return _copies_rider(arrs, [SDS(a.shape, a.dtype) for a in arrs], len(arrs), make)


def gather_all(buf):
    def body(in_ref, out_ref, ssem, rsem, lsem):
        x, y, c, _ = _place()
        me = 4 * x + 2 * y + c
        local = pltpu.make_async_copy(in_ref, out_ref.at[me], lsem)
        local.start()
        flips = [(a, b, e) for a in (0, 1) for b in (0, 1) for e in (0, 1)][1:]
        cps = []
        for i, (a, b, e) in enumerate(flips):
            peer = (x ^ a, y ^ b, c ^ e)
            cps.append(_rcopy(in_ref, out_ref.at[me], ssem.at[i], rsem.at[i], peer))
        for cp in cps:
            cp.start()
        for i, (a, b, e) in enumerate(flips):
            cps[i].wait_send()
            slot = out_ref.at[4 * (x ^ a) + 2 * (y ^ b) + (c ^ e)]
            _rcopy(slot, slot, ssem.at[i], rsem.at[i], (x ^ a, y ^ b, c ^ e)).wait_recv()
        local.wait()

    return pl.pallas_call(
        body, name="comm_gather_all", in_specs=[ANY], out_specs=ANY, out_shape=SDS((N_DEV,) + buf.shape, buf.dtype),
        scratch_shapes=[pltpu.SemaphoreType.DMA((N_DEV - 1,)), pltpu.SemaphoreType.DMA((N_DEV - 1,)), pltpu.SemaphoreType.DMA],
        compiler_params=COMM_PARAMS,
    )(buf)


def _pack_offsets(parts):
    offs, r = [], 0
    for p in parts:
        offs.append(r)
        r += -(-p.shape[0] // SUBLANES) * SUBLANES
    return offs, r


def pack_rows(parts):
    offs, total = _pack_offsets(parts)

    def body(*refs):
        out = refs[-1]
        out[...] = jnp.zeros_like(out)
        for ref, off in zip(refs[:-1], offs):
            out[off:off + ref.shape[0], :] = ref[...]

    vmem = pl.BlockSpec(memory_space=pltpu.VMEM)
    return pl.pallas_call(body, name="pack_small", in_specs=[vmem] * len(parts), out_specs=vmem,
                          out_shape=SDS((total, LANES), F32))(*parts)


def _unpack(buf, parts):
    offs, _ = _pack_offsets(parts)
    return [buf[off:off + p.shape[0]] for p, off in zip(parts, offs)]


def _pad_lanes(v):
    v = v.reshape(1, -1)
    return jnp.pad(v, ((0, 0), (0, LANES - v.shape[1])))


def _as2d(a):
    return a.reshape(1, -1) if a.ndim == 1 else a.reshape(-1, a.shape[-1])


def kernel(x, even_norm_g, even_w_in, gmlp_ln_g, gmlp_ln_b, gmlp_ws, gmlp_bs, ssd_conv_w, ssd_conv_b, ssd_dt_bias, ssd_a_log, ssd_d, ssd_norm_g, even_w_out, odd_norm_g, odd_w_in, sconv_w, odd_w_out, final_norm_g, loss_target, m_even_norm_g, m_even_w_in, m_gmlp_ln_g, m_gmlp_ln_b, m_gmlp_ws, m_gmlp_bs, m_ssd_conv_w, m_ssd_conv_b, m_ssd_dt_bias, m_ssd_a_log, m_ssd_d, m_ssd_norm_g, m_even_w_out, m_odd_norm_g, m_odd_w_in, m_sconv_w, m_odd_w_out, m_final_norm_g, v_even_norm_g, v_even_w_in, v_gmlp_ln_g, v_gmlp_ln_b, v_gmlp_ws, v_gmlp_bs, v_ssd_conv_w, v_ssd_conv_b, v_ssd_dt_bias, v_ssd_a_log, v_ssd_d, v_ssd_norm_g, v_even_w_out, v_odd_norm_g, v_odd_w_in, v_sconv_w, v_odd_w_out, v_final_norm_g):
    weights = dict(even_norm_g=even_norm_g, even_w_in=even_w_in, gmlp_ln_g=gmlp_ln_g, gmlp_ln_b=gmlp_ln_b, gmlp_ws=gmlp_ws, gmlp_bs=gmlp_bs, ssd_conv_w=ssd_conv_w, ssd_conv_b=ssd_conv_b, ssd_dt_bias=ssd_dt_bias, ssd_a_log=ssd_a_log, ssd_d=ssd_d, ssd_norm_g=ssd_norm_g, even_w_out=even_w_out, odd_norm_g=odd_norm_g, odd_w_in=odd_w_in, sconv_w=sconv_w, odd_w_out=odd_w_out, final_norm_g=final_norm_g)
    moms_m = dict(even_norm_g=m_even_norm_g, even_w_in=m_even_w_in, gmlp_ln_g=m_gmlp_ln_g, gmlp_ln_b=m_gmlp_ln_b, gmlp_ws=m_gmlp_ws, gmlp_bs=m_gmlp_bs, ssd_conv_w=m_ssd_conv_w, ssd_conv_b=m_ssd_conv_b, ssd_dt_bias=m_ssd_dt_bias, ssd_a_log=m_ssd_a_log, ssd_d=m_ssd_d, ssd_norm_g=m_ssd_norm_g, even_w_out=m_even_w_out, odd_norm_g=m_odd_norm_g, odd_w_in=m_odd_w_in, sconv_w=m_sconv_w, odd_w_out=m_odd_w_out, final_norm_g=m_final_norm_g)
    moms_v = dict(even_norm_g=v_even_norm_g, even_w_in=v_even_w_in, gmlp_ln_g=v_gmlp_ln_g, gmlp_ln_b=v_gmlp_ln_b, gmlp_ws=v_gmlp_ws, gmlp_bs=v_gmlp_bs, ssd_conv_w=v_ssd_conv_w, ssd_conv_b=v_ssd_conv_b, ssd_dt_bias=v_ssd_dt_bias, ssd_a_log=v_ssd_a_log, ssd_d=v_ssd_d, ssd_norm_g=v_ssd_norm_g, even_w_out=v_even_w_out, odd_norm_g=v_odd_norm_g, odd_w_in=v_odd_w_in, sconv_w=v_sconv_w, odd_w_out=v_odd_w_out, final_norm_g=v_final_norm_g)
    names = list(weights)

    xs = x[0]
    tgt = loss_target[0]
    T = xs.shape[0]
    chip = 2 * lax.axis_index("x") + lax.axis_index("y")
    core = lax.axis_index("c")
    cshard = B_XBC // N_CHIPS
    dshard = D_MODEL // N_CHIPS

    def halves(w):
        return w.astype(BF16).reshape(2, w.shape[0] // 2, w.shape[1])

    small_shard = jnp.concatenate([ssd_conv_w[0].reshape(-1), odd_norm_g[0], sconv_w[0].reshape(-1)])
    g_wie, g_small = run_rider(gather_rider([halves(even_w_in[0]), small_shard.reshape(2, -1, LANES)]), name="comm_gather_first")
    wie = g_wie.reshape(N_CHIPS, D_MODEL, IN_EVEN // N_CHIPS).transpose(1, 0, 2).reshape(D_MODEL, IN_EVEN)
    wie_main = wie[:, :EVEN_MAIN]
    wie_dt = jnp.pad(wie[:, EVEN_MAIN:], ((0, 0), (0, LANES - B_HEADS)))
    g_small = g_small.reshape(N_CHIPS, -1)
    n_cw = B_CONV * cshard
    conv_w = g_small[:, :n_cw].reshape(N_CHIPS, B_CONV, cshard).transpose(1, 0, 2).reshape(B_CONV, B_XBC)
    odd_g = g_small[:, n_cw:n_cw + dshard].reshape(1, D_MODEL)
    sconv = g_small[:, n_cw + dshard:].reshape(N_CHIPS, C_CONV, dshard).transpose(1, 0, 2).reshape(C_CONV, C_WIDTH)

    even_g = even_norm_g
    ln_g, ln_b = gmlp_ln_g, gmlp_ln_b
    ws, bs_t = gmlp_ws[0], gmlp_bs[0].T
    conv_b = ssd_conv_b
    dt_bias, a_log, d_skip = _pad_lanes(ssd_dt_bias), _pad_lanes(ssd_a_log), _pad_lanes(ssd_d)
    norm_g = ssd_norm_g
    fin_g = final_norm_g.reshape(1, D_MODEL)

    xn0 = rmsnorm_fwd(xs, even_g, name="even_norm")
    h0, (g_wio,) = matmul(xn0, wie_main, "nn", name="even_in", rider=gather_rider([halves(odd_w_in[0])]))
    wio = g_wio.reshape(N_CHIPS, D_MODEL, IN_ODD // N_CHIPS)
    dtr = matmul(xn0, wie_dt, "nn", name="even_in_dt", tk=D_MODEL)
    ya = gmlp_fwd(h0, ln_g, ln_b, ws, bs_t)
    xbc = ssd_conv_fwd(h0, conv_w, conv_b)
    (yb, states), (g_woe, g_woo) = ssd_fwd(xbc, dtr, h0, dt_bias, a_log, d_skip, norm_g,
                                           rider=gather_rider([halves(even_w_out[0]), halves(odd_w_out[0])]))
    woe = g_woe.reshape(2 * A_WIDTH, D_MODEL)
    woo = g_woo.reshape(2 * C_WIDTH, D_MODEL)
    y0 = jnp.concatenate([ya, yb], axis=1)
    x1 = matmul(y0, woe, "nn", name="even_out", res=xs)

    xn1 = rmsnorm_fwd(x1, odd_g, name="odd_norm")
    h1 = matmul(xn1, wio, "nn", name="odd_in")
    yc = sconv_fwd(h1, sconv)
    yd, att_o, att_lse = attn_fwd(h1)
    y1 = jnp.concatenate([yc, yd], axis=1)
    x2 = matmul(y1, woo, "nn", name="odd_out", res=x1)

    loss_part, dx2, dx2b, d_fin_g = loss_head(x2, fin_g, tgt)

    tile = 1024
    rows_layout = ((2, N_CHIPS, ROW_PIECE, D_MODEL), (1, 1, ROW_PIECE, tile), lambda i, j, k: (i % 2, i // 2, 0, j))
    per_chip = IN_ODD // N_CHIPS // tile
    cols_layout = ((2, N_CHIPS, D_MODEL // 2, IN_ODD // N_CHIPS), (1, 1, tile, tile), lambda i, j, k: (i, j // per_chip, 0, j % per_chip))
    dy1 = matmul(dx2b, woo, "nt", name="odd_out_dy")
    d_woo = matmul(y1, dx2b, "tn", name="odd_out_dw", tm=ROW_PIECE, out_layout=rows_layout)
    dbg, dcg, dhx, dzc, d_sconv = sconv_bwd(h1, dy1, sconv)
    dq, dzd = attn_bwd_dq(h1, att_o, att_lse, dy1)
    dk, dv = attn_bwd_dkv(h1, att_o, att_lse, dy1)
    dh1 = jnp.concatenate([dbg, dcg, dhx, dzc, dq, dk, dv, dzd], axis=1)
    dxn1 = matmul(dh1, wio, "nt", name="odd_in_dx")
    d_wio = matmul(xn1, dh1, "tn", name="odd_in_dw", out_layout=cols_layout)
    dx1, dx1b, d_odd_g = rmsnorm_bwd(x1, odd_g, dxn1, dx2, name="odd_norm_bwd")

    d_woe = matmul(y0, dx1b, "tn", name="even_out_dw", tm=ROW_PIECE, out_layout=rows_layout)
    first = [d_wio, d_woo, d_woe]
    dy0, first_sib = matmul(dx1b, woe, "nt", name="even_out_dy", rider=swap_halves_rider(first))
    first_sums = [chip_sum(p, s, core, name=f"chip_sum_first_{i}") for i, (p, s) in enumerate(zip(first, first_sib))]
    duvz, d_ln_g, d_ln_b, d_ws, d_bs_t = gmlp_bwd(h0, dy0, ln_g, ln_b, ws, bs_t)
    (dxbc_act, ddtr, dzb, d_dt_bias, d_a_log, d_d, d_norm_g), first_landed = ssd_bwd(
        xbc, dtr, h0, states, dy0, dt_bias, a_log, d_skip, norm_g, rider=scatter_rider(first_sums))
    first_totals = [total_sum(s, l, chip, name=f"total_first_{i}") for i, (s, l) in enumerate(zip(first_sums, first_landed))]
    (dxbc, d_conv_w, d_conv_b), first_totals_sib = ssd_conv_bwd(h0, dxbc_act, conv_w, conv_b, rider=swap_rider(first_totals))
    dh0 = jnp.concatenate([duvz, dzb, dxbc], axis=1)
    ddtr_b = ddtr.astype(BF16)
    d_wie_main = matmul(xn0, dh0, "tn", name="even_in_dw")
    d_wie_dt = matmul(xn0, ddtr_b, "tn", name="even_in_dw_dt")
    d_wie = jnp.concatenate([d_wie_main, d_wie_dt[:, :B_HEADS]], axis=1)
    last = [d_wie.reshape(2, D_MODEL // 2, N_CHIPS, IN_EVEN // N_CHIPS).transpose(0, 2, 1, 3)]
    dxn0, last_sib = matmul(ddtr_b, wie_dt, "nt", name="even_in_dx_dt", rider=swap_halves_rider(last))
    last_sums = [chip_sum(last[0], last_sib[0], core, name="chip_sum_last")]
    dxn0, last_landed = matmul(dh0, wie_main, "nt", name="even_in_dx", res=dxn0, rider=scatter_rider(last_sums))
    last_totals = [total_sum(last_sums[0], last_landed[0], chip, name="total_last")]
    (grad_x, _, d_even_g), last_totals_sib = rmsnorm_bwd(xs, even_g, dxn0, dx1, name="even_norm_bwd", rider=swap_rider(last_totals))
    joined = [jnp.where(core == 0, jnp.stack([t, s]), jnp.stack([s, t]))
              for t, s in zip(last_totals + first_totals, last_totals_sib + first_totals_sib)]
    big = dict(even_w_in=joined[0].reshape(even_w_in.shape), odd_w_in=joined[1].reshape(odd_w_in.shape),
               odd_w_out=joined[2].reshape(odd_w_out.shape), even_w_out=joined[3].reshape(even_w_out.shape))

    small_names = ["even_norm_g", "gmlp_ln_g", "gmlp_ln_b", "gmlp_ws", "gmlp_bs", "ssd_conv_w", "ssd_conv_b", "ssd_dt_bias",
                   "ssd_a_log", "ssd_d", "ssd_norm_g", "odd_norm_g", "sconv_w", "final_norm_g"]
    small_parts = [d_even_g, d_ln_g, d_ln_b, d_ws, d_bs_t.T, d_conv_w, d_conv_b, d_dt_bias, d_a_log, d_d, d_norm_g, d_odd_g, d_sconv, d_fin_g]
    small_shapes = [p.shape for p in small_parts]
    small_rows = [p.reshape(-1, LANES) for p in small_parts]
    small_sum = sum_leading(gather_all(pack_rows(small_rows)), name="small_sum")
    full = {nm: rows.reshape(shape) for nm, rows, shape in zip(small_names, _unpack(small_sum, small_rows), small_shapes)}
    grads = dict(big)
    for nm in small_names:
        g = full[nm]
        if nm in ("ssd_dt_bias", "ssd_a_log", "ssd_d"):
            g = g[:, :B_HEADS]
        elif nm == "ssd_conv_w":
            g = lax.dynamic_slice_in_dim(g, chip * cshard, cshard, axis=1)
        elif nm in ("odd_norm_g", "sconv_w"):
            g = lax.dynamic_slice_in_dim(g, chip * dshard, dshard, axis=1)
        grads[nm] = g.reshape(weights[nm].shape)

    deltas, new_m, new_v = {}, {}, {}
    for nm in names:
        w = weights[nm]
        d, nm_, nv_ = adamw(_as2d(w), _as2d(grads[nm]), _as2d(moms_m[nm]), _as2d(moms_v[nm]), name=f"adamw_{nm}")
        deltas[nm], new_m[nm], new_v[nm] = d.reshape(w.shape), nm_.reshape(w.shape), nv_.reshape(w.shape)

    loss = lax.psum(loss_part[0, 0], ("x", "y", "c"))
    return (loss, grad_x[None], *[grads[n] for n in names], *[deltas[n] for n in names],
            *[new_m[n] for n in names], *[new_v[n] for n in names])
```

```python
import functools

import jax
import jax.numpy as jnp
from jax import lax
from jax.experimental import pallas as pl
from jax.experimental.pallas import tpu as pltpu

F32 = jnp.float32
BF16 = jnp.bfloat16
SDS = jax.ShapeDtypeStruct
MESH = pl.DeviceIdType.MESH

D_MODEL = 2048
A_WIDTH = 2048
A_GROUPS = 8
CHUNK = 128
B_WIDTH = 2048
B_HEADS = 32
B_HEAD_DIM = 64
B_GROUPS = 8
B_STATE = 128
B_CONV = 4
B_XBC = B_WIDTH + 2 * B_GROUPS * B_STATE
C_WIDTH = 2048
C_CONV = 3
D_HEADS = 16
D_HEAD_DIM = 128
D_PATTERNS = ((128, 1), (512, 4), (2048, 16))
ATT_BLOCK = 128
ATT_SUPER = 2048
EVEN_MAIN = 3 * A_WIDTH + B_WIDTH + B_XBC
IN_EVEN = EVEN_MAIN + B_HEADS
IN_ODD = 4 * C_WIDTH + 4 * D_HEADS * D_HEAD_DIM
LANES = 128
SUBLANES = 8
EPS = 1e-5
ADAM_LR = 0.001
ADAM_B1 = 0.9
ADAM_B2 = 0.999
ADAM_EPS = 1e-08
ADAM_WD = 0.01
ADAM_STEP = 10
N_CHIPS = 4
N_DEV = 8
VMEM_LIMIT_BYTES = 56 * 1024 * 1024


def _cp(*sem):
    return pltpu.CompilerParams(dimension_semantics=sem, vmem_limit_bytes=VMEM_LIMIT_BYTES)


def _full(shape):
    return pl.BlockSpec(shape, lambda *_: (0,) * len(shape))


def _silu(x):
    return x * jax.nn.sigmoid(x)


def _dot_nn(a, b):
    return lax.dot_general(a, b, (((1,), (0,)), ((), ())), preferred_element_type=F32)


def _dot_nt(a, b):
    return lax.dot_general(a, b, (((1,), (1,)), ((), ())), preferred_element_type=F32)


def _dot_tn(a, b):
    return lax.dot_general(a, b, (((0,), (0,)), ((), ())), preferred_element_type=F32)


def _tril(n):
    return lax.broadcasted_iota(jnp.int32, (n, n), 0) >= lax.broadcasted_iota(jnp.int32, (n, n), 1)


_DOTS = {"nn": _dot_nn, "nt": _dot_nt, "tn": _dot_tn}


class Rider:
    def __init__(self, ins, out_shapes, n_sems, phases):
        self.ins, self.out_shapes, self.n_sems, self.phases = list(ins), list(out_shapes), n_sems, list(phases)


def _call(body, *, name, grid, in_specs, out_specs, out_shape, scratch_shapes, sem, args, rider=None):
    in_specs, out_specs, out_shape, scratch_shapes = list(in_specs), list(out_specs), list(out_shape), list(scratch_shapes)
    if rider is None:
        res = pl.pallas_call(body, name=name, grid=grid, in_specs=in_specs, out_specs=out_specs, out_shape=out_shape,
                             scratch_shapes=scratch_shapes, compiler_params=_cp(*sem))(*args)
        return list(res), []
    counts = [len(in_specs), len(rider.ins), len(out_specs), len(rider.out_shapes), len(scratch_shapes), 2]
    total = 1
    for g in grid:
        total *= g

    def wrapped(*refs):
        groups, pos = [], 0
        for n in counts:
            groups.append(refs[pos:pos + n])
            pos += n
        ins, rins, outs, routs, scr, (ssem, rsem) = groups
        step = 0
        for d, g in enumerate(grid):
            step = step * g + pl.program_id(d)
        for frac, fn in rider.phases:
            @pl.when(step == min(int(frac * total), total - 1))
            def _(fn=fn):
                fn(rins, routs, ssem, rsem)
        body(*ins, *outs, *scr)

    dma = pltpu.SemaphoreType.DMA((rider.n_sems,))
    res = pl.pallas_call(
        wrapped, name=name, grid=grid, in_specs=in_specs + [ANY] * len(rider.ins), out_specs=out_specs + [ANY] * len(rider.out_shapes),
        out_shape=out_shape + rider.out_shapes, scratch_shapes=scratch_shapes + [dma, dma],
        compiler_params=_cp(*(("arbitrary",) * len(grid))))(*args, *rider.ins)
    return list(res[:len(out_specs)]), list(res[len(out_specs):])


def run_rider(rider, *, name):
    def body(*refs):
        n_in, n_out = len(rider.ins), len(rider.out_shapes)
        ins, outs, (ssem, rsem) = refs[:n_in], refs[n_in:n_in + n_out], refs[n_in + n_out:]
        for _, fn in rider.phases:
            fn(ins, outs, ssem, rsem)

    dma = pltpu.SemaphoreType.DMA((rider.n_sems,))
    return list(pl.pallas_call(body, name=name, in_specs=[ANY] * len(rider.ins), out_specs=[ANY] * len(rider.out_shapes),
                               out_shape=rider.out_shapes, scratch_shapes=[dma, dma])(*rider.ins))


def matmul(a, b, mode, *, name, out_dtype=F32, res=None, tm=1024, tn=1024, tk=2048, out_layout=None, rider=None):
    a_parts = list(a) if isinstance(a, (list, tuple)) else [a]
    b_parts = list(b) if isinstance(b, (list, tuple)) else [b]
    assert len(b_parts) == 1 or mode == "tn"
    b0 = b_parts[0]
    shards = b0.shape[0] if b0.ndim == 3 else 1
    a_rows, a_cols = a_parts[0].shape[0], sum(p.shape[1] for p in a_parts)
    b_rows, b_cols = b0.shape[-2], sum(p.shape[-1] for p in b_parts) * shards
    if mode == "tn":
        (K, M), (K2, N) = (a_rows, a_cols), (b_rows, b_cols)
    elif mode == "nt":
        (M, K), (N, K2) = (a_rows, a_cols), (b_rows, b_cols)
    else:
        (M, K), (K2, N) = (a_rows, a_cols), (b_rows, b_cols)
    assert K == K2, (mode, K, K2)
    tm, tn, tk = min(tm, M), min(tn, N), min(tk, K)
    if shards > 1:
        assert mode != "tn" and b0.shape[-1] % (tk if mode == "nt" else tn) == 0
    assert M % tm == 0 and N % tn == 0 and K % tk == 0, (M, N, K, tm, tn, tk)
    nk = K // tk
    dot = _DOTS[mode]

    def spans(parts, tile):
        out, off = [], 0
        for p in parts:
            assert p.shape[-1] % tile == 0, (p.shape, tile)
            out.append((off, p.shape[-1] // tile))
            off += p.shape[-1] // tile
        return out

    a_axis = 0 if mode == "tn" else 2
    a_spans = spans(a_parts, tm if mode == "tn" else tk)
    b_spans = spans(b_parts, tn) if len(b_parts) > 1 else [(0, N // tn)]

    def inside(t, span):
        return jnp.logical_and(t >= span[0], t < span[0] + span[1])

    def body(*refs):
        a_refs, b_refs, rest = refs[:len(a_parts)], refs[len(a_parts):len(a_parts) + len(b_parts)], refs[len(a_parts) + len(b_parts):]
        r_ref = rest[0] if res is not None else None
        o_ref = rest[1] if res is not None else rest[0]
        k = pl.program_id(2)

        def finish(acc):
            if res is not None:
                acc = acc + r_ref[...]
            o_ref[...] = acc.astype(o_ref.dtype).reshape(o_ref.shape)

        def emit(a_ref, b_ref):
            def product():
                return dot(a_ref[...], b_ref[0] if shards > 1 else b_ref[...])

            if nk == 1:
                finish(product())
                return
            acc_ref = rest[-1]

            @pl.when(k == 0)
            def _():
                acc_ref[...] = product()

            @pl.when(jnp.logical_and(k > 0, k < nk - 1))
            def _():
                acc_ref[...] += product()

            @pl.when(k == nk - 1)
            def _():
                finish(acc_ref[...] + product())

        for a_ref, a_span in zip(a_refs, a_spans):
            for b_ref, b_span in zip(b_refs, b_spans):
                conds = ([inside(pl.program_id(a_axis), a_span)] if len(a_parts) > 1 else []) + \
                        ([inside(pl.program_id(1), b_span)] if len(b_parts) > 1 else [])
                if conds:
                    pl.when(functools.reduce(jnp.logical_and, conds))(functools.partial(emit, a_ref, b_ref))
                else:
                    emit(a_ref, b_ref)

    def piece_index(t, span):
        return jnp.clip(t - span[0], 0, span[1] - 1)

    def a_spec_of(span):
        if len(a_parts) == 1:
            return pl.BlockSpec((tk, tm), lambda i, j, k: (k, i)) if mode == "tn" else pl.BlockSpec((tm, tk), lambda i, j, k: (i, k))
        if mode == "tn":
            return pl.BlockSpec((tk, tm), lambda i, j, k: (jnp.where(inside(i, span), k, 0), piece_index(i, span)))
        return pl.BlockSpec((tm, tk), lambda i, j, k: (i, piece_index(k, span)))

    def b_spec_of(span):
        if shards > 1 and mode == "nn":
            per = b0.shape[-1] // tn
            return pl.BlockSpec((1, tk, tn), lambda i, j, k: (j // per, k, j % per))
        if shards > 1:
            per = b0.shape[-1] // tk
            return pl.BlockSpec((1, tn, tk), lambda i, j, k: (k // per, j, k % per))
        if mode == "nt":
            return pl.BlockSpec((tn, tk), lambda i, j, k: (j, k))
        if len(b_parts) == 1:
            return pl.BlockSpec((tk, tn), lambda i, j, k: (k, j))
        return pl.BlockSpec((tk, tn), lambda i, j, k: (jnp.where(inside(j, span), k, 0), piece_index(j, span)))

    o_spec = pl.BlockSpec((tm, tn), lambda i, j, k: (i, j))
    in_specs = [a_spec_of(s) for s in a_spans] + [b_spec_of(s) for s in b_spans]
    args = a_parts + b_parts
    if res is not None:
        in_specs.append(o_spec)
        args.append(res)
    out_shape = SDS((M, N), out_dtype)
    if out_layout is not None:
        out_shape, o_spec = SDS(out_layout[0], out_dtype), pl.BlockSpec(out_layout[1], out_layout[2])
    outs, rider_outs = _call(
        body, name=name, grid=(M // tm, N // tn, nk), in_specs=in_specs, out_specs=[o_spec], out_shape=[out_shape],
        scratch_shapes=[pltpu.VMEM((tm, tn), F32)] if nk > 1 else [], sem=("parallel", "parallel", "arbitrary"),
        args=args, rider=rider)
    return outs[0] if rider is None else (outs[0], rider_outs)


ROW_TILE = 512
ROW_PIECE = 512


def _rms(x, g):
    return x * lax.rsqrt(jnp.mean(x * x, axis=-1, keepdims=True) + EPS) * g


def rmsnorm_fwd(x, g, *, name):
    T, D = x.shape

    def body(x_ref, g_ref, o_ref):
        o_ref[...] = _rms(x_ref[...], g_ref[...]).astype(BF16)

    row = pl.BlockSpec((ROW_TILE, D), lambda i: (i, 0))
    return pl.pallas_call(body, name=name, grid=(T // ROW_TILE,), in_specs=[row, _full((1, D))], out_specs=row,
                          out_shape=SDS((T, D), BF16), compiler_params=_cp("parallel"))(x, g)


def rmsnorm_bwd(x, g, dxn, dres, *, name, rider=None):
    T, D = x.shape

    def body(x_ref, g_ref, dxn_ref, dres_ref, dx_ref, dxb_ref, dg_ref):
        _, vjp = jax.vjp(_rms, x_ref[...], g_ref[...])
        dx, dg = vjp(dxn_ref[...])
        dx = dx + dres_ref[...]
        dx_ref[...] = dx
        dxb_ref[...] = dx.astype(BF16)

        @pl.when(pl.program_id(0) == 0)
        def _():
            dg_ref[...] = jnp.zeros_like(dg_ref)

        dg_ref[...] += dg

    row = pl.BlockSpec((ROW_TILE, D), lambda i: (i, 0))
    outs, rider_outs = _call(
        body, name=name, grid=(T // ROW_TILE,), in_specs=[row, _full((1, D)), row, row],
        out_specs=[row, row, _full((1, D))], out_shape=[SDS((T, D), F32), SDS((T, D), BF16), SDS((1, D), F32)],
        scratch_shapes=[], sem=("arbitrary",), args=(x, g, dxn, dres), rider=rider)
    return outs if rider is None else (outs, rider_outs)


def _loss_tile(x, g, tgt):
    err = jnp.square(_rms(x, g) - tgt)
    return 0.5 * jnp.sum(jnp.mean(err, axis=-1))


def loss_head(x, g, tgt):
    T, D = x.shape

    def body(x_ref, g_ref, t_ref, loss_ref, dx_ref, dxb_ref, dg_ref):
        loss, vjp = jax.vjp(_loss_tile, x_ref[...], g_ref[...], t_ref[...])
        dx, dg, _ = vjp(jnp.ones((), F32))
        dx_ref[...] = dx
        dxb_ref[...] = dx.astype(BF16)

        @pl.when(pl.program_id(0) == 0)
        def _():
            dg_ref[...] = jnp.zeros_like(dg_ref)
            loss_ref[...] = jnp.zeros_like(loss_ref)

        dg_ref[...] += dg
        loss_ref[...] += jnp.reshape(loss, (1, 1))

    row = pl.BlockSpec((ROW_TILE, D), lambda i: (i, 0))
    return pl.pallas_call(
        body, name="loss_head", grid=(T // ROW_TILE,), in_specs=[row, _full((1, D)), row],
        out_specs=[_full((1, 1)), row, row, _full((1, D))],
        out_shape=[SDS((1, 1), F32), SDS((T, D), F32), SDS((T, D), BF16), SDS((1, D), F32)],
        compiler_params=_cp("arbitrary"))(x, g, tgt)


TILE_BYTES = 1 << 20


def _row_tile(rows, row_bytes):
    for cand in (512, 256, 128, 64, 32, 16, 8):
        if rows % cand == 0 and cand * row_bytes <= TILE_BYTES:
            return cand
    return rows


def adamw(w, g, m, v, *, name):
    R, C = w.shape
    tr = _row_tile(R, C * 4)

    def body(w_ref, g_ref, m_ref, v_ref, d_ref, nm_ref, nv_ref):
        gg = g_ref[...]
        mm = ADAM_B1 * m_ref[...] + (1.0 - ADAM_B1) * gg
        vv = ADAM_B2 * v_ref[...] + (1.0 - ADAM_B2) * jnp.square(gg)
        m_hat = mm / (1.0 - ADAM_B1 ** ADAM_STEP)
        v_hat = vv / (1.0 - ADAM_B2 ** ADAM_STEP)
        d_ref[...] = -ADAM_LR * (m_hat / (jnp.sqrt(v_hat) + ADAM_EPS) + ADAM_WD * w_ref[...])
        nm_ref[...] = mm
        nv_ref[...] = vv

    blk = pl.BlockSpec((tr, C), lambda i: (i, 0))
    return pl.pallas_call(body, name=name, grid=(R // tr,), in_specs=[blk] * 4, out_specs=[blk] * 3,
                          out_shape=[SDS((R, C), F32)] * 3, compiler_params=_cp("parallel"))(w, g, m, v)


def sum_leading(a, *, name):
    n, R, C = a.shape
    tr = _row_tile(R, n * C * 4)

    def body(a_ref, o_ref):
        acc = a_ref[0]
        for j in range(1, n):
            acc = acc + a_ref[j]
        o_ref[...] = acc

    return pl.pallas_call(body, name=name, grid=(R // tr,), in_specs=[pl.BlockSpec((n, tr, C), lambda i: (0, i, 0))],
                          out_specs=pl.BlockSpec((tr, C), lambda i: (i, 0)), out_shape=SDS((R, C), F32),
                          compiler_params=_cp("parallel"))(a)


def chip_sum(pieces, from_sibling, core, *, name):
    _, n, R, C = pieces.shape

    def body(c_ref, a_ref, b_ref, o_ref):
        o_ref[...] = (a_ref[0] + b_ref[...]).astype(BF16)

    tr = _row_tile(R, C * 4)
    blk = pl.BlockSpec((1, tr, C), lambda k, i, c_ref: (k, i, 0))
    mine = pl.BlockSpec((1, 1, tr, C), lambda k, i, c_ref: (c_ref[0], k, i, 0))
    return pl.pallas_call(
        body, name=name, out_shape=SDS((n, R, C), BF16),
        grid_spec=pltpu.PrefetchScalarGridSpec(num_scalar_prefetch=1, grid=(n, R // tr), in_specs=[mine, blk], out_specs=blk),
        compiler_params=_cp("parallel", "parallel"))(core.reshape(1), pieces, from_sibling)


def total_sum(sums, landed, chip, *, name):
    n, R, C = landed.shape

    def body(k_ref, s_ref, l_ref, o_ref):
        acc = s_ref[0].astype(F32)
        for j in range(n):
            acc = acc + l_ref[j].astype(F32)
        o_ref[...] = acc

    tr = _row_tile(R, n * C * 2)
    return pl.pallas_call(
        body, name=name, out_shape=SDS((R, C), F32),
        grid_spec=pltpu.PrefetchScalarGridSpec(
            num_scalar_prefetch=1, grid=(R // tr,),
            in_specs=[pl.BlockSpec((1, tr, C), lambda i, k_ref: (k_ref[0], i, 0)), pl.BlockSpec((n, tr, C), lambda i, k_ref: (0, i, 0))],
            out_specs=pl.BlockSpec((tr, C), lambda i, k_ref: (i, 0))),
        compiler_params=_cp("parallel"))(chip.reshape(1), sums, landed)


def _gmlp_chunk(u, v, z, ln_g, ln_b, wsc, bs_t):
    mu = jnp.mean(v, axis=-1, keepdims=True)
    xc = v - mu
    vn = xc * lax.rsqrt(jnp.mean(xc * xc, axis=-1, keepdims=True) + EPS) * ln_g + ln_b
    gw = A_WIDTH // A_GROUPS
    outs = []
    for g in range(A_GROUPS):
        m = _dot_nn(wsc[g].astype(BF16), vn[:, g * gw:(g + 1) * gw].astype(BF16))
        outs.append(m + bs_t[:, g:g + 1])
    return _silu(z) * (u * jnp.concatenate(outs, axis=1))


def _h_cols(width, idx, rows=CHUNK):
    return pl.BlockSpec((rows, width), lambda i: (i, idx))


def gmlp_fwd(h, ln_g, ln_b, ws, bs_t):
    T = h.shape[0]

    def body(u_ref, v_ref, z_ref, g_ref, b_ref, ws_ref, bs_ref, o_ref):
        wsc = jnp.where(_tril(CHUNK)[None], ws_ref[...], 0.0)
        o_ref[...] = _gmlp_chunk(u_ref[...], v_ref[...], z_ref[...], g_ref[...], b_ref[...], wsc, bs_ref[...]).astype(BF16)

    return pl.pallas_call(
        body, name="gmlp_fwd", grid=(T // CHUNK,),
        in_specs=[_h_cols(A_WIDTH, 0), _h_cols(A_WIDTH, 1), _h_cols(A_WIDTH, 2), _full((1, A_WIDTH)), _full((1, A_WIDTH)),
                  _full((A_GROUPS, CHUNK, CHUNK)), _full((CHUNK, A_GROUPS))],
        out_specs=_h_cols(A_WIDTH, 0), out_shape=SDS((T, A_WIDTH), BF16), compiler_params=_cp("parallel"),
    )(h, h, h, ln_g, ln_b, ws, bs_t)


def gmlp_bwd(h, dy, ln_g, ln_b, ws, bs_t):
    T = h.shape[0]

    def body(u_ref, v_ref, z_ref, dy_ref, g_ref, b_ref, ws_ref, bs_ref, duvz_ref, dg_ref, db_ref, dws_ref, dbs_ref):
        tri = _tril(CHUNK)[None]
        wsc = jnp.where(tri, ws_ref[...], 0.0)
        _, vjp = jax.vjp(_gmlp_chunk, u_ref[...], v_ref[...], z_ref[...], g_ref[...], b_ref[...], wsc, bs_ref[...])
        du, dv, dz, dg, db, dws, dbs = vjp(dy_ref[...])
        duvz_ref[:, :A_WIDTH] = du.astype(BF16)
        duvz_ref[:, A_WIDTH:2 * A_WIDTH] = dv.astype(BF16)
        duvz_ref[:, 2 * A_WIDTH:] = dz.astype(BF16)

        @pl.when(pl.program_id(0) == 0)
        def _():
            dg_ref[...] = jnp.zeros_like(dg_ref)
            db_ref[...] = jnp.zeros_like(db_ref)
            dws_ref[...] = jnp.zeros_like(dws_ref)
            dbs_ref[...] = jnp.zeros_like(dbs_ref)

        dg_ref[...] += dg
        db_ref[...] += db
        dws_ref[...] += jnp.where(tri, dws, 0.0)
        dbs_ref[...] += dbs

    pshapes = [(1, A_WIDTH), (1, A_WIDTH), (A_GROUPS, CHUNK, CHUNK), (CHUNK, A_GROUPS)]
    return pl.pallas_call(
        body, name="gmlp_bwd", grid=(T // CHUNK,),
        in_specs=[_h_cols(A_WIDTH, 0), _h_cols(A_WIDTH, 1), _h_cols(A_WIDTH, 2), _h_cols(A_WIDTH, 0)] + [_full(s) for s in pshapes],
        out_specs=[_h_cols(3 * A_WIDTH, 0)] + [_full(s) for s in pshapes],
        out_shape=[SDS((T, 3 * A_WIDTH), BF16)] + [SDS(s, F32) for s in pshapes],
        compiler_params=_cp("arbitrary"),
    )(h, h, h, dy, ln_g, ln_b, ws, bs_t)


CONV_ROWS = 256
CONV_COLS = 512


def _taps(xe, w, rows):
    K = w.shape[0]
    acc = None
    for k in range(K):
        off = SUBLANES - (K - 1) + k
        term = w[k:k + 1, :] * xe[off:off + rows, :]
        acc = term if acc is None else acc + term
    return acc


def _ssd_conv_tile(x, halo, w, b):
    return _silu(_taps(jnp.concatenate([halo, x], axis=0), w, x.shape[0]) + b)


def _halo_spec(cols, col_idx, nt=None):
    rpb = CONV_ROWS // SUBLANES
    if nt is None:
        return pl.BlockSpec((SUBLANES, cols), lambda c, i: (jnp.maximum(i * rpb - 1, 0), col_idx(c)))
    return pl.BlockSpec((SUBLANES, cols), lambda c, j: (jnp.maximum((nt - 1 - j) * rpb - 1, 0), col_idx(c)))


def ssd_conv_fwd(h, w, b):
    T = h.shape[0]
    nc = B_XBC // CONV_COLS
    base = (3 * A_WIDTH + B_WIDTH) // CONV_COLS

    def body(x_ref, halo_ref, w_ref, b_ref, o_ref):
        halo = jnp.where(pl.program_id(1) > 0, halo_ref[...], 0.0)
        o_ref[...] = _ssd_conv_tile(x_ref[...], halo, w_ref[...], b_ref[...])

    return pl.pallas_call(
        body, name="ssd_conv_fwd", grid=(nc, T // CONV_ROWS),
        in_specs=[pl.BlockSpec((CONV_ROWS, CONV_COLS), lambda c, i: (i, base + c)), _halo_spec(CONV_COLS, lambda c: base + c),
                  pl.BlockSpec((B_CONV, CONV_COLS), lambda c, i: (0, c)), pl.BlockSpec((1, CONV_COLS), lambda c, i: (0, c))],
        out_specs=pl.BlockSpec((CONV_ROWS, CONV_COLS), lambda c, i: (i, c)),
        out_shape=SDS((T, B_XBC), F32), compiler_params=_cp("parallel", "parallel"),
    )(h, h, w, b)


def ssd_conv_bwd(h, dy, w, b, rider=None):
    T = h.shape[0]
    nc = B_XBC // CONV_COLS
    nt = T // CONV_ROWS
    base = (3 * A_WIDTH + B_WIDTH) // CONV_COLS

    def body(x_ref, halo_ref, dy_ref, w_ref, b_ref, dx_ref, dw_ref, db_ref, carry_ref):
        j = pl.program_id(1)
        halo = jnp.where(j < nt - 1, halo_ref[...], 0.0)
        _, vjp = jax.vjp(_ssd_conv_tile, x_ref[...], halo, w_ref[...], b_ref[...])
        dx, dhalo, dw, db = vjp(dy_ref[...])

        @pl.when(j == 0)
        def _():
            carry_ref[...] = jnp.zeros_like(carry_ref)
            dw_ref[...] = jnp.zeros_like(dw_ref)
            db_ref[...] = jnp.zeros_like(db_ref)

        tail = dx[CONV_ROWS - SUBLANES:, :] + carry_ref[...]
        dx_ref[...] = jnp.concatenate([dx[:CONV_ROWS - SUBLANES, :], tail], axis=0).astype(BF16)
        carry_ref[...] = dhalo
        dw_ref[...] += dw
        db_ref[...] += db

    outs, rider_outs = _call(
        body, name="ssd_conv_bwd", grid=(nc, nt),
        in_specs=[pl.BlockSpec((CONV_ROWS, CONV_COLS), lambda c, j: (nt - 1 - j, base + c)),
                  _halo_spec(CONV_COLS, lambda c: base + c, nt),
                  pl.BlockSpec((CONV_ROWS, CONV_COLS), lambda c, j: (nt - 1 - j, c)),
                  pl.BlockSpec((B_CONV, CONV_COLS), lambda c, j: (0, c)), pl.BlockSpec((1, CONV_COLS), lambda c, j: (0, c))],
        out_specs=[pl.BlockSpec((CONV_ROWS, CONV_COLS), lambda c, j: (nt - 1 - j, c)),
                   pl.BlockSpec((B_CONV, CONV_COLS), lambda c, j: (0, c)), pl.BlockSpec((1, CONV_COLS), lambda c, j: (0, c))],
        out_shape=[SDS((T, B_XBC), BF16), SDS((B_CONV, B_XBC), F32), SDS((1, B_XBC), F32)],
        scratch_shapes=[pltpu.VMEM((SUBLANES, CONV_COLS), F32)], sem=("parallel", "arbitrary"),
        args=(h, h, dy, w, b), rider=rider)
    return outs if rider is None else (outs, rider_outs)


def _sconv_tile(bg, cg, hx, z, cg_halo, hx_halo, w):
    ch = jnp.concatenate([cg_halo * hx_halo, cg * hx], axis=0)
    return _silu(z) * (bg * _taps(ch, w, bg.shape[0]))


def sconv_fwd(h, w):
    T = h.shape[0]
    nc = C_WIDTH // CONV_COLS

    def col(seg):
        return pl.BlockSpec((CONV_ROWS, CONV_COLS), lambda c, i: (i, seg * nc + c))

    def body(bg_ref, cg_ref, hx_ref, z_ref, cgh_ref, hxh_ref, w_ref, o_ref):
        first = pl.program_id(1) == 0
        cgh = jnp.where(first, 0.0, cgh_ref[...])
        hxh = jnp.where(first, 0.0, hxh_ref[...])
        o_ref[...] = _sconv_tile(bg_ref[...], cg_ref[...], hx_ref[...], z_ref[...], cgh, hxh, w_ref[...]).astype(BF16)

    return pl.pallas_call(
        body, name="sconv_fwd", grid=(nc, T // CONV_ROWS),
        in_specs=[col(0), col(1), col(2), col(3), _halo_spec(CONV_COLS, lambda c: nc + c), _halo_spec(CONV_COLS, lambda c: 2 * nc + c),
                  pl.BlockSpec((C_CONV, CONV_COLS), lambda c, i: (0, c))],
        out_specs=pl.BlockSpec((CONV_ROWS, CONV_COLS), lambda c, i: (i, c)),
        out_shape=SDS((T, C_WIDTH), BF16), compiler_params=_cp("parallel", "parallel"),
    )(h, h, h, h, h, h, w)


def sconv_bwd(h, dy, w):
    T = h.shape[0]
    nc = C_WIDTH // CONV_COLS
    nt = T // CONV_ROWS

    def col(seg):
        return pl.BlockSpec((CONV_ROWS, CONV_COLS), lambda c, j: (nt - 1 - j, seg * nc + c))

    def body(bg_ref, cg_ref, hx_ref, z_ref, cgh_ref, hxh_ref, dy_ref, w_ref, dbg_ref, dcg_ref, dhx_ref, dz_ref, dw_ref, ccg_ref, chx_ref):
        j = pl.program_id(1)
        first = j == nt - 1
        cgh = jnp.where(first, 0.0, cgh_ref[...])
        hxh = jnp.where(first, 0.0, hxh_ref[...])
        _, vjp = jax.vjp(_sconv_tile, bg_ref[...], cg_ref[...], hx_ref[...], z_ref[...], cgh, hxh, w_ref[...])
        dbg, dcg, dhx, dz, dcgh, dhxh, dw = vjp(dy_ref[...])

        @pl.when(j == 0)
        def _():
            ccg_ref[...] = jnp.zeros_like(ccg_ref)
            chx_ref[...] = jnp.zeros_like(chx_ref)
            dw_ref[...] = jnp.zeros_like(dw_ref)

        def with_carry(d, carry_ref):
            tail = d[CONV_ROWS - SUBLANES:, :] + carry_ref[...]
            return jnp.concatenate([d[:CONV_ROWS - SUBLANES, :], tail], axis=0).astype(BF16)

        dbg_ref[...] = dbg.astype(BF16)
        dz_ref[...] = dz.astype(BF16)
        dcg_ref[...] = with_carry(dcg, ccg_ref)
        dhx_ref[...] = with_carry(dhx, chx_ref)
        ccg_ref[...] = dcgh
        chx_ref[...] = dhxh
        dw_ref[...] += dw

    out_row = pl.BlockSpec((CONV_ROWS, CONV_COLS), lambda c, j: (nt - 1 - j, c))
    wspec = pl.BlockSpec((C_CONV, CONV_COLS), lambda c, j: (0, c))
    return pl.pallas_call(
        body, name="sconv_bwd", grid=(nc, nt),
        in_specs=[col(0), col(1), col(2), col(3), _halo_spec(CONV_COLS, lambda c: nc + c, nt), _halo_spec(CONV_COLS, lambda c: 2 * nc + c, nt),
                  out_row, wspec],
        out_specs=[out_row] * 4 + [wspec],
        out_shape=[SDS((T, C_WIDTH), BF16)] * 4 + [SDS((C_CONV, C_WIDTH), F32)],
        scratch_shapes=[pltpu.VMEM((SUBLANES, CONV_COLS), F32)] * 2,
        compiler_params=_cp("parallel", "arbitrary"),
    )(h, h, h, h, h, h, dy, w)


def _softplus(x):
    return jnp.maximum(x, 0.0) + jnp.log(1.0 + jnp.exp(-jnp.abs(x)))


def _ssd_chunk(xs, bm, cm, dtr, z, prev, dt_bias, a_log, d_skip, norm_g):
    tril = _tril(CHUNK)
    dt = _softplus(dtr + dt_bias)
    adt = dt * (-jnp.exp(a_log))
    a_cs = jnp.dot(tril.astype(F32), adt, precision=lax.Precision.HIGHEST, preferred_element_type=F32)
    a_cs_t = a_cs.T
    a_last = a_cs[CHUNK - 1:CHUNK, :]
    dt_f = _spread_heads(dt, B_HEAD_DIM)
    dec_f = _spread_heads(jnp.exp(a_last - a_cs), B_HEAD_DIM)
    ecs_f = _spread_heads(jnp.exp(a_cs), B_HEAD_DIM)
    dsk_f = _spread_heads(d_skip, B_HEAD_DIM)
    cd_t = jnp.exp(a_cs_t[:, CHUNK - 1:CHUNK])
    xdt = xs * dt_f
    xdd = xdt * dec_f
    colb = _spread_heads(a_cs, CHUNK)
    rowb = jnp.concatenate([jnp.broadcast_to(a_cs_t[hh:hh + 1, :], (CHUNK, CHUNK)) for hh in range(B_HEADS)], axis=1)
    wide = (CHUNK, B_HEADS * CHUNK)
    keep = lax.broadcasted_iota(jnp.int32, wide, 0) >= lax.broadcasted_iota(jnp.int32, wide, 1) % CHUNK
    decay = jnp.exp(jnp.where(keep, colb - rowb, -jnp.inf))
    hpg = B_HEADS // B_GROUPS
    gw = B_WIDTH // B_GROUPS
    low_half = lax.broadcasted_iota(jnp.int32, (CHUNK, 2 * B_HEAD_DIM), 1) < B_HEAD_DIM
    ys, nxt = [], []
    for g in range(B_GROUPS):
        bg = bm[:, g * B_STATE:(g + 1) * B_STATE].astype(BF16)
        cg = cm[:, g * B_STATE:(g + 1) * B_STATE].astype(BF16)
        cb = _dot_nt(cg, bg)
        cbl = (decay[:, g * hpg * CHUNK:(g + 1) * hpg * CHUNK] * jnp.concatenate([cb] * hpg, axis=1)).astype(BF16)
        pg = prev[g * gw:(g + 1) * gw, :]
        y_off = _dot_nt(cg, pg.astype(BF16)) * ecs_f[:, g * gw:(g + 1) * gw]
        st = _dot_tn(xdd[:, g * gw:(g + 1) * gw].astype(BF16), bg)
        cd = jnp.concatenate([jnp.broadcast_to(cd_t[g * hpg + r:g * hpg + r + 1, :], (B_HEAD_DIM, 1)) for r in range(hpg)], axis=0)
        nxt.append(pg * cd + st)
        pairs = []
        for j in range(hpg // 2):
            xp = xdt[:, g * gw + 2 * j * B_HEAD_DIM:g * gw + 2 * (j + 1) * B_HEAD_DIM]
            rhs = jnp.concatenate([jnp.where(low_half, xp, 0.0), jnp.where(low_half, 0.0, xp)], axis=0).astype(BF16)
            pairs.append(_dot_nn(cbl[:, 2 * j * CHUNK:2 * (j + 1) * CHUNK], rhs))
        ys.append(jnp.concatenate(pairs, axis=1) + y_off)
    y = (jnp.concatenate(ys, axis=1) + dsk_f * xs) * _silu(z)
    outs = []
    for g in range(B_GROUPS):
        yg = y[:, g * gw:(g + 1) * gw]
        outs.append(yg * lax.rsqrt(jnp.mean(yg * yg, axis=-1, keepdims=True) + EPS))
    return jnp.concatenate(outs, axis=1) * norm_g, jnp.concatenate(nxt, axis=0)


def _split3(v):
    hi = v.astype(BF16)
    r1 = v - hi.astype(F32)
    mid = r1.astype(BF16)
    return hi, mid, (r1 - mid.astype(F32)).astype(BF16)


def _head_one_hot(width, parts):
    n = B_HEADS * width
    shape = (parts * LANES, n)
    return (lax.broadcasted_iota(jnp.int32, shape, 0) % LANES == lax.broadcasted_iota(jnp.int32, shape, 1) // width).astype(BF16)


@functools.partial(jax.custom_vjp, nondiff_argnums=(1,))
def _spread_heads(v, width):
    return _dot_nn(jnp.concatenate(_split3(v), axis=1), _head_one_hot(width, 3))


def _spread_heads_fwd(v, width):
    return _spread_heads(v, width), None


def _spread_heads_bwd(width, _, g):
    return (_dot_nt(jnp.concatenate(_split3(g), axis=1), jnp.concatenate([_head_one_hot(width, 1)] * 3, axis=1)),)


_spread_heads.defvjp(_spread_heads_fwd, _spread_heads_bwd)


_SSD_PARAM_SHAPES = [(1, LANES), (1, LANES), (1, LANES), (1, B_WIDTH)]
_STATE_SHAPE = (B_WIDTH, B_STATE)


def ssd_fwd(xbc, dtr, h, dt_bias, a_log, d_skip, norm_g, rider=None):
    T = xbc.shape[0]
    nc = T // CHUNK

    def body(xs_ref, b_ref, c_ref, dt_ref, z_ref, p0, p1, p2, p3, y_ref, st_ref, state):
        @pl.when(pl.program_id(0) == 0)
        def _():
            state[...] = jnp.zeros_like(state)

        prev = state[...]
        st_ref[0] = prev
        yb, nxt = _ssd_chunk(xs_ref[...], b_ref[...], c_ref[...], dt_ref[...], z_ref[...], prev, p0[...], p1[...], p2[...], p3[...])
        y_ref[...] = yb.astype(BF16)
        state[...] = nxt

    outs, rider_outs = _call(
        body, name="ssd_fwd", grid=(nc,),
        in_specs=[_h_cols(B_WIDTH, 0), _h_cols(B_GROUPS * B_STATE, 2), _h_cols(B_GROUPS * B_STATE, 3), _h_cols(LANES, 0), _h_cols(B_WIDTH, 3)]
        + [_full(s) for s in _SSD_PARAM_SHAPES],
        out_specs=[_h_cols(B_WIDTH, 0), pl.BlockSpec((1,) + _STATE_SHAPE, lambda i: (i, 0, 0))],
        out_shape=[SDS((T, B_WIDTH), BF16), SDS((nc,) + _STATE_SHAPE, F32)],
        scratch_shapes=[pltpu.VMEM(_STATE_SHAPE, F32)], sem=("arbitrary",),
        args=(xbc, xbc, xbc, dtr, h, dt_bias, a_log, d_skip, norm_g), rider=rider)
    return outs if rider is None else (outs, rider_outs)


def ssd_bwd(xbc, dtr, h, states, dy, dt_bias, a_log, d_skip, norm_g, rider=None):
    T = xbc.shape[0]
    nc = T // CHUNK

    def rev(width, idx):
        return pl.BlockSpec((CHUNK, width), lambda j: (nc - 1 - j, idx))

    def body(xs_ref, b_ref, c_ref, dt_ref, z_ref, st_ref, dy_ref, p0, p1, p2, p3,
             dxbc_ref, ddt_ref, dz_ref, g0, g1, g2, g3, dstate):
        @pl.when(pl.program_id(0) == 0)
        def _():
            dstate[...] = jnp.zeros_like(dstate)
            for gref in (g0, g1, g2, g3):
                gref[...] = jnp.zeros_like(gref)

        _, vjp = jax.vjp(_ssd_chunk, xs_ref[...], b_ref[...], c_ref[...], dt_ref[...], z_ref[...], st_ref[0],
                         p0[...], p1[...], p2[...], p3[...])
        dxs, dbm, dcm, ddt, dz, dprev, d0, d1, d2, d3 = vjp((dy_ref[...], dstate[...]))
        dxbc_ref[:, :B_WIDTH] = dxs
        dxbc_ref[:, B_WIDTH:B_WIDTH + gn] = dbm
        dxbc_ref[:, B_WIDTH + gn:] = dcm
        ddt_ref[...] = ddt.astype(BF16)
        dz_ref[...] = dz.astype(BF16)
        dstate[...] = dprev
        g0[...] += d0
        g1[...] += d1
        g2[...] += d2
        g3[...] += d3

    gn = B_GROUPS * B_STATE
    outs, rider_outs = _call(
        body, name="ssd_bwd", grid=(nc,),
        in_specs=[rev(B_WIDTH, 0), rev(gn, 2), rev(gn, 3), rev(LANES, 0), rev(B_WIDTH, 3),
                  pl.BlockSpec((1,) + _STATE_SHAPE, lambda j: (nc - 1 - j, 0, 0)), rev(B_WIDTH, 1)]
        + [_full(s) for s in _SSD_PARAM_SHAPES],
        out_specs=[rev(B_XBC, 0), rev(LANES, 0), rev(B_WIDTH, 0)] + [_full(s) for s in _SSD_PARAM_SHAPES],
        out_shape=[SDS((T, B_XBC), F32), SDS((T, LANES), BF16), SDS((T, B_WIDTH), BF16)]
        + [SDS(s, F32) for s in _SSD_PARAM_SHAPES],
        scratch_shapes=[pltpu.VMEM(_STATE_SHAPE, F32)], sem=("arbitrary",),
        args=(xbc, xbc, xbc, dtr, h, states, dy, dt_bias, a_log, d_skip, norm_g), rider=rider)
    return outs if rider is None else (outs, rider_outs)


ATT_SCALE = D_HEAD_DIM ** -0.5
Q_COL, K_COL, V_COL, Z_COL = (4 * C_WIDTH // LANES + i * D_HEADS for i in range(4))


ATT_NBLK = ATT_SUPER // ATT_BLOCK


def _res_rows(r, first, count, dil):
    return pl.ds(r + dil * first, count) if dil == 1 else pl.ds(r + dil * first, count, stride=dil)


def _blocks(ref, dil, dtype=None):
    n = ATT_SUPER // dil
    parts = []
    for r in range(dil):
        v = ref[_res_rows(r, 0, n, dil), :]
        parts.append((v if dtype is None else v.astype(dtype)).reshape(n // ATT_BLOCK, ATT_BLOCK, D_HEAD_DIM))
    return parts[0] if dil == 1 else jnp.concatenate(parts, axis=0)


def _blocks_before(cur_ref, prev_ref, dil, dtype):
    n = ATT_SUPER // dil
    parts = []
    for r in range(dil):
        v = prev_ref[_res_rows(r, n - ATT_BLOCK, ATT_BLOCK, dil), :]
        if n > ATT_BLOCK:
            v = jnp.concatenate([v, cur_ref[_res_rows(r, 0, n - ATT_BLOCK, dil), :]], axis=0)
        parts.append(v.astype(dtype).reshape(n // ATT_BLOCK, ATT_BLOCK, D_HEAD_DIM))
    return parts[0] if dil == 1 else jnp.concatenate(parts, axis=0)


def _blocks_after(cur_ref, next_ref, dil, dtype=None):
    n = ATT_SUPER // dil
    parts = []
    for r in range(dil):
        v = next_ref[_res_rows(r, 0, ATT_BLOCK, dil), :]
        if n > ATT_BLOCK:
            v = jnp.concatenate([cur_ref[_res_rows(r, ATT_BLOCK, n - ATT_BLOCK, dil), :], v], axis=0)
        parts.append((v if dtype is None else v.astype(dtype)).reshape(n // ATT_BLOCK, ATT_BLOCK, D_HEAD_DIM))
    return parts[0] if dil == 1 else jnp.concatenate(parts, axis=0)


def _unblock(ref, val, dil, add=False):
    n = ATT_SUPER // dil
    nb = n // ATT_BLOCK
    for r in range(dil):
        v = val[r * nb:(r + 1) * nb].reshape(n, D_HEAD_DIM)
        if add:
            ref[_res_rows(r, 0, n, dil), :] += v
        else:
            ref[_res_rows(r, 0, n, dil), :] = v


def _att_masks(dil, edge_ok, edge_last=False):
    shape = (ATT_NBLK, ATT_BLOCK, ATT_BLOCK)
    blk = lax.broadcasted_iota(jnp.int32, shape, 0)
    row = lax.broadcasted_iota(jnp.int32, shape, 1)
    col = lax.broadcasted_iota(jnp.int32, shape, 2)
    nb = ATT_NBLK // dil
    at_edge = (blk % nb) == (nb - 1 if edge_last else 0)
    return col <= row, jnp.logical_and(col >= row, jnp.logical_or(jnp.logical_not(at_edge), edge_ok))


def _bdot_nt(a, b):
    return lax.dot_general(a, b, (((2,), (2,)), ((0,), (0,))), preferred_element_type=F32)


def _bdot_nn(a, b):
    return lax.dot_general(a, b, (((2,), (1,)), ((0,), (0,))), preferred_element_type=F32)


def _bdot_tn(a, b):
    return lax.dot_general(a, b, (((1,), (1,)), ((0,), (0,))), preferred_element_type=F32)


def _att_spec(col0, shift=0, last=None):
    def imap(hh, n):
        m = n + shift
        if shift < 0:
            m = jnp.maximum(m, 0)
        if shift > 0:
            m = jnp.minimum(m, last)
        return (m, col0 + hh)
    return pl.BlockSpec((ATT_SUPER, D_HEAD_DIM), imap)


def _att_out_spec():
    return pl.BlockSpec((ATT_SUPER, D_HEAD_DIM), lambda hh, n: (n, hh))


def attn_fwd(h):
    T = h.shape[0]
    npat = len(D_PATTERNS)

    def body(q_ref, kc_ref, kp_ref, vc_ref, vp_ref, z_ref, yd_ref, o_ref, lse_ref, *scratch):
        o_s, l_s = scratch[:npat], scratch[npat:]
        has_prev = pl.program_id(1) > 0
        for pi, (_, dil) in enumerate(D_PATTERNS):
            mask_c, mask_p = _att_masks(dil, has_prev)
            q = _blocks(q_ref, dil, BF16)
            kc, vc = _blocks(kc_ref, dil, BF16), _blocks(vc_ref, dil, BF16)
            kp, vp = _blocks_before(kc_ref, kp_ref, dil, BF16), _blocks_before(vc_ref, vp_ref, dil, BF16)
            s_c = jnp.where(mask_c, _bdot_nt(q, kc) * ATT_SCALE, -jnp.inf)
            s_p = jnp.where(mask_p, _bdot_nt(q, kp) * ATT_SCALE, -jnp.inf)
            m = jnp.maximum(jnp.max(s_c, axis=-1, keepdims=True), jnp.max(s_p, axis=-1, keepdims=True))
            p_c = jnp.exp(s_c - m)
            p_p = jnp.exp(s_p - m)
            l = jnp.sum(p_c, axis=-1, keepdims=True) + jnp.sum(p_p, axis=-1, keepdims=True)
            o = _bdot_nn((p_c / l).astype(BF16), vc) + _bdot_nn((p_p / l).astype(BF16), vp)
            _unblock(o_s[pi], o, dil)
            _unblock(l_s[pi], jnp.broadcast_to(m + jnp.log(l), o.shape), dil)
        lses = [l_s[pi][...] for pi in range(npat)]
        mx = functools.reduce(jnp.maximum, lses)
        ws = [jnp.exp(l - mx) for l in lses]
        den = functools.reduce(lambda a, b: a + b, ws)
        o = functools.reduce(lambda a, b: a + b, [(w / den) * o_s[pi][...] for pi, w in enumerate(ws)])
        o_ref[...] = o
        lse_ref[...] = mx + jnp.log(den)
        yd_ref[...] = (_silu(z_ref[...]) * o).astype(BF16)

    n_super = T // ATT_SUPER
    return pl.pallas_call(
        body, name="attn_fwd", grid=(D_HEADS, n_super),
        in_specs=[_att_spec(Q_COL), _att_spec(K_COL), _att_spec(K_COL, -1), _att_spec(V_COL), _att_spec(V_COL, -1), _att_spec(Z_COL)],
        out_specs=[_att_out_spec()] * 3,
        out_shape=[SDS((T, D_HEADS * D_HEAD_DIM), BF16), SDS((T, D_HEADS * D_HEAD_DIM), F32), SDS((T, D_HEADS * D_HEAD_DIM), F32)],
        scratch_shapes=[pltpu.VMEM((ATT_SUPER, D_HEAD_DIM), F32)] * (2 * npat),
        compiler_params=_cp("parallel", "arbitrary"),
    )(h, h, h, h, h, h)


def _dsilu(z):
    s = jax.nn.sigmoid(z)
    return s * (1.0 + z * (1.0 - s))


def attn_bwd_dq(h, o, lse, dy):
    T = h.shape[0]
    dy_col = C_WIDTH // LANES

    def body(q_ref, kc_ref, kp_ref, vc_ref, vp_ref, z_ref, o_ref, lse_ref, dy_ref, dq_ref, dz_ref, do_s, dd_s, dq_s):
        has_prev = pl.program_id(1) > 0
        z, oo, dyd = z_ref[...], o_ref[...], dy_ref[...]
        do = dyd * _silu(z)
        dz_ref[...] = (dyd * oo * _dsilu(z)).astype(BF16)
        do_s[...] = do
        dd_s[...] = jnp.broadcast_to(jnp.sum(do * oo, axis=-1, keepdims=True), (ATT_SUPER, D_HEAD_DIM))
        for pi, (_, dil) in enumerate(D_PATTERNS):
            mask_c, mask_p = _att_masks(dil, has_prev)
            q = _blocks(q_ref, dil, BF16)
            kc, vc = _blocks(kc_ref, dil, BF16), _blocks(vc_ref, dil, BF16)
            kp, vp = _blocks_before(kc_ref, kp_ref, dil, BF16), _blocks_before(vc_ref, vp_ref, dil, BF16)
            lse_b, dd_b, do_b = _blocks(lse_ref, dil), _blocks(dd_s, dil), _blocks(do_s, dil, BF16)
            p_c = jnp.where(mask_c, jnp.exp(_bdot_nt(q, kc) * ATT_SCALE - lse_b), 0.0)
            p_p = jnp.where(mask_p, jnp.exp(_bdot_nt(q, kp) * ATT_SCALE - lse_b), 0.0)
            ds_c = p_c * (_bdot_nt(do_b, vc) - dd_b) * ATT_SCALE
            ds_p = p_p * (_bdot_nt(do_b, vp) - dd_b) * ATT_SCALE
            dq = _bdot_nn(ds_c.astype(BF16), kc) + _bdot_nn(ds_p.astype(BF16), kp)
            _unblock(dq_s, dq, dil, add=pi > 0)
        dq_ref[...] = dq_s[...].astype(BF16)

    n_super = T // ATT_SUPER
    blk = (ATT_SUPER, D_HEAD_DIM)
    return pl.pallas_call(
        body, name="attn_bwd_dq", grid=(D_HEADS, n_super),
        in_specs=[_att_spec(Q_COL), _att_spec(K_COL), _att_spec(K_COL, -1), _att_spec(V_COL), _att_spec(V_COL, -1), _att_spec(Z_COL),
                  _att_spec(0), _att_spec(0), _att_spec(dy_col)],
        out_specs=[_att_out_spec()] * 2,
        out_shape=[SDS((T, D_HEADS * D_HEAD_DIM), BF16)] * 2,
        scratch_shapes=[pltpu.VMEM(blk, F32)] * 3,
        compiler_params=_cp("parallel", "arbitrary"),
    )(h, h, h, h, h, h, o, lse, dy)


def attn_bwd_dkv(h, o, lse, dy):
    T = h.shape[0]
    n_super = T // ATT_SUPER
    last = n_super - 1
    dy_col = C_WIDTH // LANES
    npat = len(D_PATTERNS)

    def body(k_ref, v_ref, qc_ref, qn_ref, zc_ref, zn_ref, oc_ref, on_ref, lc_ref, ln_ref, dyc_ref, dyn_ref, dk_ref, dv_ref,
             do_c, do_n, dd_c, dd_n, dk_s, dv_s):
        do_s, dd_s = (do_c, do_n), (dd_c, dd_n)
        has_next = pl.program_id(1) < last
        for i, (z_ref, oo_ref, dyd_ref) in enumerate(((zc_ref, oc_ref, dyc_ref), (zn_ref, on_ref, dyn_ref))):
            do = dyd_ref[...] * _silu(z_ref[...])
            do_s[i][...] = do
            dd_s[i][...] = jnp.broadcast_to(jnp.sum(do * oo_ref[...], axis=-1, keepdims=True), (ATT_SUPER, D_HEAD_DIM))
        for pi, (_, dil) in enumerate(D_PATTERNS):
            mask_c, mask_p = _att_masks(dil, has_next, edge_last=True)
            kb, vb = _blocks(k_ref, dil, BF16), _blocks(v_ref, dil, BF16)
            dk = dv = None
            for own in (True, False):
                if own:
                    q, lse_b = _blocks(qc_ref, dil, BF16), _blocks(lc_ref, dil)
                    do_b, dd_b = _blocks(do_c, dil, BF16), _blocks(dd_c, dil)
                else:
                    q, lse_b = _blocks_after(qc_ref, qn_ref, dil, BF16), _blocks_after(lc_ref, ln_ref, dil)
                    do_b, dd_b = _blocks_after(do_c, do_n, dil, BF16), _blocks_after(dd_c, dd_n, dil)
                p = jnp.where(mask_c if own else mask_p, jnp.exp(_bdot_nt(q, kb) * ATT_SCALE - lse_b), 0.0)
                ds = p * (_bdot_nt(do_b, vb) - dd_b) * ATT_SCALE
                dv_t = _bdot_tn(p.astype(BF16), do_b)
                dk_t = _bdot_tn(ds.astype(BF16), q)
                dk = dk_t if dk is None else dk + dk_t
                dv = dv_t if dv is None else dv + dv_t
            _unblock(dk_s, dk, dil, add=pi > 0)
            _unblock(dv_s, dv, dil, add=pi > 0)
        dk_ref[...] = dk_s[...].astype(BF16)
        dv_ref[...] = dv_s[...].astype(BF16)

    blk = (ATT_SUPER, D_HEAD_DIM)

    def pair(col0):
        return [_att_spec(col0), _att_spec(col0, 1, last)]

    return pl.pallas_call(
        body, name="attn_bwd_dkv", grid=(D_HEADS, n_super),
        in_specs=[_att_spec(K_COL), _att_spec(V_COL)] + pair(Q_COL) + pair(Z_COL) + pair(0) + pair(0) + pair(dy_col),
        out_specs=[_att_out_spec()] * 2,
        out_shape=[SDS((T, D_HEADS * D_HEAD_DIM), BF16)] * 2,
        scratch_shapes=[pltpu.VMEM(blk, F32)] * 6,
        compiler_params=_cp("parallel", "arbitrary"),
    )(h, h, h, h, h, h, o, o, lse, lse, dy, dy)


ANY = pl.BlockSpec(memory_space=pl.ANY)
COMM_PARAMS = pltpu.CompilerParams()


def _place():
    x, y, c = lax.axis_index("x"), lax.axis_index("y"), lax.axis_index("c")
    return x, y, c, [(1 - x, y), (x, 1 - y), (1 - x, 1 - y)]


def _rcopy(src, dst, ssem, rsem, dev):
    return pltpu.make_async_remote_copy(src_ref=src, dst_ref=dst, send_sem=ssem, recv_sem=rsem, device_id=dev, device_id_type=MESH)


def gather_rider(arrs, fractions=(0.0, 0.6, 1.0)):
    n = len(arrs)
    per = 7

    def to_chips(ins, outs, ssem, rsem):
        x, y, c, chips = _place()
        return [_rcopy(ins[a].at[c], outs[a].at[2 * x + y, c], ssem.at[per * a + j], rsem.at[per * a + j], (px, py, c))
                for a in range(n) for j, (px, py) in enumerate(chips)]

    def passed_on(outs, ssem, rsem, half):
        x, y, c, chips = _place()
        cps = []
        for a in range(n):
            for j, (px, py) in enumerate(chips):
                slot = outs[a].at[2 * px + py, half(c)]
                cps.append(_rcopy(slot, slot, ssem.at[per * a + 3 + j], rsem.at[per * a + 3 + j], (x, y, 1 - c)))
        return cps

    def own(ins, outs, ssem, rsem):
        x, y, c, _ = _place()
        return [_rcopy(ins[a], outs[a].at[2 * x + y], ssem.at[per * a + 6], rsem.at[per * a + 6], (x, y, 1 - c)) for a in range(n)]

    def start(ins, outs, ssem, rsem):
        for cp in to_chips(ins, outs, ssem, rsem) + own(ins, outs, ssem, rsem):
            cp.start()

    def pass_on(ins, outs, ssem, rsem):
        x, y, c, chips = _place()
        landed = [_rcopy(outs[a].at[2 * px + py, c], outs[a].at[2 * px + py, c], ssem.at[per * a + j], rsem.at[per * a + j], (px, py, c))
                  for a in range(n) for j, (px, py) in enumerate(chips)]
        for arrival, cp in zip(landed, passed_on(outs, ssem, rsem, lambda c: c)):
            arrival.wait_recv()
            cp.start()

    def finish(ins, outs, ssem, rsem):
        for cp in passed_on(outs, ssem, rsem, lambda c: 1 - c):
            cp.wait_recv()
        for cp in to_chips(ins, outs, ssem, rsem) + passed_on(outs, ssem, rsem, lambda c: c):
            cp.wait_send()
        for cp in own(ins, outs, ssem, rsem):
            cp.wait()

    return Rider(arrs, [SDS((N_CHIPS,) + a.shape, a.dtype) for a in arrs], per * n,
                 [(fractions[0], start), (fractions[1], pass_on), (fractions[2], finish)])


def _copies_rider(ins, out_shapes, n_sems, make):
    def start(*refs):
        for cp in make(*refs):
            cp.start()

    def finish(*refs):
        for cp in make(*refs):
            cp.wait()

    return Rider(ins, out_shapes, n_sems, [(0.0, start), (1.0, finish)])


def swap_halves_rider(arrs):
    def make(ins, outs, ssem, rsem):
        x, y, c, _ = _place()
        return [_rcopy(ins[a].at[1 - c], outs[a], ssem.at[a], rsem.at[a], (x, y, 1 - c)) for a in range(len(arrs))]
    return _copies_rider(arrs, [SDS(a.shape[1:], a.dtype) for a in arrs], len(arrs), make)


def scatter_rider(arrs):
    def make(ins, outs, ssem, rsem):
        x, y, c, chips = _place()
        return [_rcopy(ins[a].at[2 * px + py], outs[a].at[j], ssem.at[3 * a + j], rsem.at[3 * a + j], (px, py, c))
                for a in range(len(arrs)) for j, (px, py) in enumerate(chips)]
    return _copies_rider(arrs, [SDS((N_CHIPS - 1,) + a.shape[1:], a.dtype) for a in arrs], 3 * len(arrs), make)


def swap_rider(arrs):
    def make(ins, outs, ssem, rsem):
        x, y, c, _ = _place()
        return [_rcopy(ins[a], outs[a], ssem.at[a], rsem.at[a], (x, y, 1 - c)) for a in range(len(arrs))]
    return _copies_rider(arrs, [SDS(a.shape, a.dtype) for a in arrs], len(arrs), make)


def gather_all(buf):
    def body(in_ref, out_ref, ssem, rsem, lsem):
        x, y, c, _ = _place()
        me = 4 * x + 2 * y + c
        local = pltpu.make_async_copy(in_ref, out_ref.at[me], lsem)
        local.start()
        flips = [(a, b, e) for a in (0, 1) for b in (0, 1) for e in (0, 1)][1:]
        cps = []
        for i, (a, b, e) in enumerate(flips):
            peer = (x ^ a, y ^ b, c ^ e)
            cps.append(_rcopy(in_ref, out_ref.at[me], ssem.at[i], rsem.at[i], peer))
        for cp in cps:
            cp.start()
        for i, (a, b, e) in enumerate(flips):
            cps[i].wait_send()
            slot = out_ref.at[4 * (x ^ a) + 2 * (y ^ b) + (c ^ e)]
            _rcopy(slot, slot, ssem.at[i], rsem.at[i], (x ^ a, y ^ b, c ^ e)).wait_recv()
        local.wait()

    return pl.pallas_call(
        body, name="comm_gather_all", in_specs=[ANY], out_specs=ANY, out_shape=SDS((N_DEV,) + buf.shape, buf.dtype),
        scratch_shapes=[pltpu.SemaphoreType.DMA((N_DEV - 1,)), pltpu.SemaphoreType.DMA((N_DEV - 1,)), pltpu.SemaphoreType.DMA],
        compiler_params=COMM_PARAMS,
    )(buf)


def _pack_offsets(parts):
    offs, r = [], 0
    for p in parts:
        offs.append(r)
        r += -(-p.shape[0] // SUBLANES) * SUBLANES
    return offs, r


def pack_rows(parts):
    offs, total = _pack_offsets(parts)

    def body(*refs):
        out = refs[-1]
        out[...] = jnp.zeros_like(out)
        for ref, off in zip(refs[:-1], offs):
            out[off:off + ref.shape[0], :] = ref[...]

    vmem = pl.BlockSpec(memory_space=pltpu.VMEM)
    return pl.pallas_call(body, name="pack_small", in_specs=[vmem] * len(parts), out_specs=vmem,
                          out_shape=SDS((total, LANES), F32))(*parts)


def _unpack(buf, parts):
    offs, _ = _pack_offsets(parts)
    return [buf[off:off + p.shape[0]] for p, off in zip(parts, offs)]


def _pad_lanes(v):
    v = v.reshape(1, -1)
    return jnp.pad(v, ((0, 0), (0, LANES - v.shape[1])))


def _as2d(a):
    return a.reshape(1, -1) if a.ndim == 1 else a.reshape(-1, a.shape[-1])


def kernel(x, even_norm_g, even_w_in, gmlp_ln_g, gmlp_ln_b, gmlp_ws, gmlp_bs, ssd_conv_w, ssd_conv_b, ssd_dt_bias, ssd_a_log, ssd_d, ssd_norm_g, even_w_out, odd_norm_g, odd_w_in, sconv_w, odd_w_out, final_norm_g, loss_target, m_even_norm_g, m_even_w_in, m_gmlp_ln_g, m_gmlp_ln_b, m_gmlp_ws, m_gmlp_bs, m_ssd_conv_w, m_ssd_conv_b, m_ssd_dt_bias, m_ssd_a_log, m_ssd_d, m_ssd_norm_g, m_even_w_out, m_odd_norm_g, m_odd_w_in, m_sconv_w, m_odd_w_out, m_final_norm_g, v_even_norm_g, v_even_w_in, v_gmlp_ln_g, v_gmlp_ln_b, v_gmlp_ws, v_gmlp_bs, v_ssd_conv_w, v_ssd_conv_b, v_ssd_dt_bias, v_ssd_a_log, v_ssd_d, v_ssd_norm_g, v_even_w_out, v_odd_norm_g, v_odd_w_in, v_sconv_w, v_odd_w_out, v_final_norm_g):
    weights = dict(even_norm_g=even_norm_g, even_w_in=even_w_in, gmlp_ln_g=gmlp_ln_g, gmlp_ln_b=gmlp_ln_b, gmlp_ws=gmlp_ws, gmlp_bs=gmlp_bs, ssd_conv_w=ssd_conv_w, ssd_conv_b=ssd_conv_b, ssd_dt_bias=ssd_dt_bias, ssd_a_log=ssd_a_log, ssd_d=ssd_d, ssd_norm_g=ssd_norm_g, even_w_out=even_w_out, odd_norm_g=odd_norm_g, odd_w_in=odd_w_in, sconv_w=sconv_w, odd_w_out=odd_w_out, final_norm_g=final_norm_g)
    moms_m = dict(even_norm_g=m_even_norm_g, even_w_in=m_even_w_in, gmlp_ln_g=m_gmlp_ln_g, gmlp_ln_b=m_gmlp_ln_b, gmlp_ws=m_gmlp_ws, gmlp_bs=m_gmlp_bs, ssd_conv_w=m_ssd_conv_w, ssd_conv_b=m_ssd_conv_b, ssd_dt_bias=m_ssd_dt_bias, ssd_a_log=m_ssd_a_log, ssd_d=m_ssd_d, ssd_norm_g=m_ssd_norm_g, even_w_out=m_even_w_out, odd_norm_g=m_odd_norm_g, odd_w_in=m_odd_w_in, sconv_w=m_sconv_w, odd_w_out=m_odd_w_out, final_norm_g=m_final_norm_g)
    moms_v = dict(even_norm_g=v_even_norm_g, even_w_in=v_even_w_in, gmlp_ln_g=v_gmlp_ln_g, gmlp_ln_b=v_gmlp_ln_b, gmlp_ws=v_gmlp_ws, gmlp_bs=v_gmlp_bs, ssd_conv_w=v_ssd_conv_w, ssd_conv_b=v_ssd_conv_b, ssd_dt_bias=v_ssd_dt_bias, ssd_a_log=v_ssd_a_log, ssd_d=v_ssd_d, ssd_norm_g=v_ssd_norm_g, even_w_out=v_even_w_out, odd_norm_g=v_odd_norm_g, odd_w_in=v_odd_w_in, sconv_w=v_sconv_w, odd_w_out=v_odd_w_out, final_norm_g=v_final_norm_g)
    names = list(weights)

    xs = x[0]
    tgt = loss_target[0]
    T = xs.shape[0]
    chip = 2 * lax.axis_index("x") + lax.axis_index("y")
    core = lax.axis_index("c")
    cshard = B_XBC // N_CHIPS
    dshard = D_MODEL // N_CHIPS

    def halves(w):
        return w.astype(BF16).reshape(2, w.shape[0] // 2, w.shape[1])

    small_shard = jnp.concatenate([ssd_conv_w[0].reshape(-1), odd_norm_g[0], sconv_w[0].reshape(-1)])
    g_wie, g_small = run_rider(gather_rider([halves(even_w_in[0]), small_shard.reshape(2, -1, LANES)]), name="comm_gather_first")
    wie = g_wie.reshape(N_CHIPS, D_MODEL, IN_EVEN // N_CHIPS).transpose(1, 0, 2).reshape(D_MODEL, IN_EVEN)
    wie_main = wie[:, :EVEN_MAIN]
    wie_dt = jnp.pad(wie[:, EVEN_MAIN:], ((0, 0), (0, LANES - B_HEADS)))
    g_small = g_small.reshape(N_CHIPS, -1)
    n_cw = B_CONV * cshard
    conv_w = g_small[:, :n_cw].reshape(N_CHIPS, B_CONV, cshard).transpose(1, 0, 2).reshape(B_CONV, B_XBC)
    odd_g = g_small[:, n_cw:n_cw + dshard].reshape(1, D_MODEL)
    sconv = g_small[:, n_cw + dshard:].reshape(N_CHIPS, C_CONV, dshard).transpose(1, 0, 2).reshape(C_CONV, C_WIDTH)

    even_g = even_norm_g
    ln_g, ln_b = gmlp_ln_g, gmlp_ln_b
    ws, bs_t = gmlp_ws[0], gmlp_bs[0].T
    conv_b = ssd_conv_b
    dt_bias, a_log, d_skip = _pad_lanes(ssd_dt_bias), _pad_lanes(ssd_a_log), _pad_lanes(ssd_d)
    norm_g = ssd_norm_g
    fin_g = final_norm_g.reshape(1, D_MODEL)

    xn0 = rmsnorm_fwd(xs, even_g, name="even_norm")
    h0, (g_wio,) = matmul(xn0, wie_main, "nn", name="even_in", rider=gather_rider([halves(odd_w_in[0])]))
    wio = g_wio.reshape(N_CHIPS, D_MODEL, IN_ODD // N_CHIPS)
    dtr = matmul(xn0, wie_dt, "nn", name="even_in_dt", tk=D_MODEL)
    ya = gmlp_fwd(h0, ln_g, ln_b, ws, bs_t)
    xbc = ssd_conv_fwd(h0, conv_w, conv_b)
    (yb, states), (g_woe, g_woo) = ssd_fwd(xbc, dtr, h0, dt_bias, a_log, d_skip, norm_g,
                                           rider=gather_rider([halves(even_w_out[0]), halves(odd_w_out[0])]))
    woe = g_woe.reshape(2 * A_WIDTH, D_MODEL)
    woo = g_woo.reshape(2 * C_WIDTH, D_MODEL)
    y0 = [ya, yb]
    x1 = matmul(y0, woe, "nn", name="even_out", res=xs)

    xn1 = rmsnorm_fwd(x1, odd_g, name="odd_norm")
    h1 = matmul(xn1, wio, "nn", name="odd_in")
    yc = sconv_fwd(h1, sconv)
    yd, att_o, att_lse = attn_fwd(h1)
    y1 = [yc, yd]
    x2 = matmul(y1, woo, "nn", name="odd_out", res=x1)

    loss_part, dx2, dx2b, d_fin_g = loss_head(x2, fin_g, tgt)

    tile = 1024
    rows_layout = ((2, N_CHIPS, ROW_PIECE, D_MODEL), (1, 1, ROW_PIECE, tile), lambda i, j, k: (i % 2, i // 2, 0, j))
    per_chip = IN_ODD // N_CHIPS // tile
    cols_layout = ((2, N_CHIPS, D_MODEL // 2, IN_ODD // N_CHIPS), (1, 1, tile, tile), lambda i, j, k: (i, j // per_chip, 0, j % per_chip))
    dy1 = matmul(dx2b, woo, "nt", name="odd_out_dy")
    d_woo = matmul(y1, dx2b, "tn", name="odd_out_dw", tm=ROW_PIECE, out_layout=rows_layout)
    dbg, dcg, dhx, dzc, d_sconv = sconv_bwd(h1, dy1, sconv)
    dq, dzd = attn_bwd_dq(h1, att_o, att_lse, dy1)
    dk, dv = attn_bwd_dkv(h1, att_o, att_lse, dy1)
    dh1 = [dbg, dcg, dhx, dzc, dq, dk, dv, dzd]
    dxn1 = matmul(dh1, wio, "nt", name="odd_in_dx", tm=512)
    d_wio = matmul(xn1, dh1, "tn", name="odd_in_dw", tk=1024, out_layout=cols_layout)
    dx1, dx1b, d_odd_g = rmsnorm_bwd(x1, odd_g, dxn1, dx2, name="odd_norm_bwd")

    d_woe = matmul(y0, dx1b, "tn", name="even_out_dw", tm=ROW_PIECE, out_layout=rows_layout)
    first = [d_wio, d_woo, d_woe]
    dy0, first_sib = matmul(dx1b, woe, "nt", name="even_out_dy", rider=swap_halves_rider(first))
    first_sums = [chip_sum(p, s, core, name=f"chip_sum_first_{i}") for i, (p, s) in enumerate(zip(first, first_sib))]
    duvz, d_ln_g, d_ln_b, d_ws, d_bs_t = gmlp_bwd(h0, dy0, ln_g, ln_b, ws, bs_t)
    (dxbc_act, ddtr, dzb, d_dt_bias, d_a_log, d_d, d_norm_g), first_landed = ssd_bwd(
        xbc, dtr, h0, states, dy0, dt_bias, a_log, d_skip, norm_g, rider=scatter_rider(first_sums))
    first_totals = [total_sum(s, l, chip, name=f"total_first_{i}") for i, (s, l) in enumerate(zip(first_sums, first_landed))]
    (dxbc, d_conv_w, d_conv_b), first_totals_sib = ssd_conv_bwd(h0, dxbc_act, conv_w, conv_b, rider=swap_rider(first_totals))
    dh0 = [duvz, dzb, dxbc]
    ddtr_b = ddtr
    d_wie_main = matmul(xn0, dh0, "tn", name="even_in_dw")
    d_wie_dt = matmul(xn0, ddtr_b, "tn", name="even_in_dw_dt")
    d_wie = jnp.concatenate([d_wie_main, d_wie_dt[:, :B_HEADS]], axis=1)
    last = [d_wie.reshape(2, D_MODEL // 2, N_CHIPS, IN_EVEN // N_CHIPS).transpose(0, 2, 1, 3)]
    dxn0, last_sib = matmul(ddtr_b, wie_dt, "nt", name="even_in_dx_dt", rider=swap_halves_rider(last))
    last_sums = [chip_sum(last[0], last_sib[0], core, name="chip_sum_last")]
    dxn0, last_landed = matmul(dh0, wie_main, "nt", name="even_in_dx", tm=512, res=dxn0, rider=scatter_rider(last_sums))
    last_totals = [total_sum(last_sums[0], last_landed[0], chip, name="total_last")]
    (grad_x, _, d_even_g), last_totals_sib = rmsnorm_bwd(xs, even_g, dxn0, dx1, name="even_norm_bwd", rider=swap_rider(last_totals))
    joined = [jnp.where(core == 0, jnp.stack([t, s]), jnp.stack([s, t]))
              for t, s in zip(last_totals + first_totals, last_totals_sib + first_totals_sib)]
    big = dict(even_w_in=joined[0].reshape(even_w_in.shape), odd_w_in=joined[1].reshape(odd_w_in.shape),
               odd_w_out=joined[2].reshape(odd_w_out.shape), even_w_out=joined[3].reshape(even_w_out.shape))

    small_names = ["even_norm_g", "gmlp_ln_g", "gmlp_ln_b", "gmlp_ws", "gmlp_bs", "ssd_conv_w", "ssd_conv_b", "ssd_dt_bias",
                   "ssd_a_log", "ssd_d", "ssd_norm_g", "odd_norm_g", "sconv_w", "final_norm_g"]
    small_parts = [d_even_g, d_ln_g, d_ln_b, d_ws, d_bs_t.T, d_conv_w, d_conv_b, d_dt_bias, d_a_log, d_d, d_norm_g, d_odd_g, d_sconv, d_fin_g]
    small_shapes = [p.shape for p in small_parts]
    small_rows = [p.reshape(-1, LANES) for p in small_parts]
    small_sum = sum_leading(gather_all(pack_rows(small_rows)), name="small_sum")
    full = {nm: rows.reshape(shape) for nm, rows, shape in zip(small_names, _unpack(small_sum, small_rows), small_shapes)}
    grads = dict(big)
    for nm in small_names:
        g = full[nm]
        if nm in ("ssd_dt_bias", "ssd_a_log", "ssd_d"):
            g = g[:, :B_HEADS]
        elif nm == "ssd_conv_w":
            g = lax.dynamic_slice_in_dim(g, chip * cshard, cshard, axis=1)
        elif nm in ("odd_norm_g", "sconv_w"):
            g = lax.dynamic_slice_in_dim(g, chip * dshard, dshard, axis=1)
        grads[nm] = g.reshape(weights[nm].shape)

    deltas, new_m, new_v = {}, {}, {}
    for nm in names:
        w = weights[nm]
        d, nm_, nv_ = adamw(_as2d(w), _as2d(grads[nm]), _as2d(moms_m[nm]), _as2d(moms_v[nm]), name=f"adamw_{nm}")
        deltas[nm], new_m[nm], new_v[nm] = d.reshape(w.shape), nm_.reshape(w.shape), nv_.reshape(w.shape)

    loss = lax.psum(loss_part[0, 0], ("x", "y", "c"))
    return (loss, grad_x[None], *[grads[n] for n in names], *[deltas[n] for n in names],
            *[new_m[n] for n in names], *[new_v[n] for n in names])
```

```python
import functools

import jax
import jax.numpy as jnp
from jax import lax
from jax.experimental import pallas as pl
from jax.experimental.pallas import tpu as pltpu

F32 = jnp.float32
BF16 = jnp.bfloat16
SDS = jax.ShapeDtypeStruct
MESH = pl.DeviceIdType.MESH

D_MODEL = 2048
A_WIDTH = 2048
A_GROUPS = 8
CHUNK = 128
B_WIDTH = 2048
B_HEADS = 32
B_HEAD_DIM = 64
B_GROUPS = 8
B_STATE = 128
B_CONV = 4
B_XBC = B_WIDTH + 2 * B_GROUPS * B_STATE
C_WIDTH = 2048
C_CONV = 3
D_HEADS = 16
D_HEAD_DIM = 128
D_PATTERNS = ((128, 1), (512, 4), (2048, 16))
ATT_BLOCK = 128
ATT_SUPER = 2048
EVEN_MAIN = 3 * A_WIDTH + B_WIDTH + B_XBC
IN_EVEN = EVEN_MAIN + B_HEADS
IN_ODD = 4 * C_WIDTH + 4 * D_HEADS * D_HEAD_DIM
LANES = 128
SUBLANES = 8
EPS = 1e-5
ADAM_LR = 0.001
ADAM_B1 = 0.9
ADAM_B2 = 0.999
ADAM_EPS = 1e-08
ADAM_WD = 0.01
ADAM_STEP = 10
N_CHIPS = 4
N_DEV = 8
VMEM_LIMIT_BYTES = 56 * 1024 * 1024


def _cp(*sem):
    return pltpu.CompilerParams(dimension_semantics=sem, vmem_limit_bytes=VMEM_LIMIT_BYTES)


def _full(shape):
    return pl.BlockSpec(shape, lambda *_: (0,) * len(shape))


def _silu(x):
    return x * jax.nn.sigmoid(x)


def _dot_nn(a, b):
    return lax.dot_general(a, b, (((1,), (0,)), ((), ())), preferred_element_type=F32)


def _dot_nt(a, b):
    return lax.dot_general(a, b, (((1,), (1,)), ((), ())), preferred_element_type=F32)


def _dot_tn(a, b):
    return lax.dot_general(a, b, (((0,), (0,)), ((), ())), preferred_element_type=F32)


def _tril(n):
    return lax.broadcasted_iota(jnp.int32, (n, n), 0) >= lax.broadcasted_iota(jnp.int32, (n, n), 1)


_DOTS = {"nn": _dot_nn, "nt": _dot_nt, "tn": _dot_tn}


class Rider:
    def __init__(self, ins, out_shapes, n_sems, phases):
        self.ins, self.out_shapes, self.n_sems, self.phases = list(ins), list(out_shapes), n_sems, list(phases)


def _call(body, *, name, grid, in_specs, out_specs, out_shape, scratch_shapes, sem, args, rider=None):
    in_specs, out_specs, out_shape, scratch_shapes = list(in_specs), list(out_specs), list(out_shape), list(scratch_shapes)
    if rider is None:
        res = pl.pallas_call(body, name=name, grid=grid, in_specs=in_specs, out_specs=out_specs, out_shape=out_shape,
                             scratch_shapes=scratch_shapes, compiler_params=_cp(*sem))(*args)
        return list(res), []
    counts = [len(in_specs), len(rider.ins), len(out_specs), len(rider.out_shapes), len(scratch_shapes), 2]
    total = 1
    for g in grid:
        total *= g

    def wrapped(*refs):
        groups, pos = [], 0
        for n in counts:
            groups.append(refs[pos:pos + n])
            pos += n
        ins, rins, outs, routs, scr, (ssem, rsem) = groups
        step = 0
        for d, g in enumerate(grid):
            step = step * g + pl.program_id(d)
        for frac, fn in rider.phases:
            @pl.when(step == min(int(frac * total), total - 1))
            def _(fn=fn):
                fn(rins, routs, ssem, rsem)
        body(*ins, *outs, *scr)

    dma = pltpu.SemaphoreType.DMA((rider.n_sems,))
    res = pl.pallas_call(
        wrapped, name=name, grid=grid, in_specs=in_specs + [ANY] * len(rider.ins), out_specs=out_specs + [ANY] * len(rider.out_shapes),
        out_shape=out_shape + rider.out_shapes, scratch_shapes=scratch_shapes + [dma, dma],
        compiler_params=_cp(*(("arbitrary",) * len(grid))))(*args, *rider.ins)
    return list(res[:len(out_specs)]), list(res[len(out_specs):])


def run_rider(rider, *, name):
    def body(*refs):
        n_in, n_out = len(rider.ins), len(rider.out_shapes)
        ins, outs, (ssem, rsem) = refs[:n_in], refs[n_in:n_in + n_out], refs[n_in + n_out:]
        for _, fn in rider.phases:
            fn(ins, outs, ssem, rsem)

    dma = pltpu.SemaphoreType.DMA((rider.n_sems,))
    return list(pl.pallas_call(body, name=name, in_specs=[ANY] * len(rider.ins), out_specs=[ANY] * len(rider.out_shapes),
                               out_shape=rider.out_shapes, scratch_shapes=[dma, dma])(*rider.ins))


def matmul(a, b, mode, *, name, out_dtype=F32, res=None, tm=1024, tn=1024, tk=2048, out_layout=None, rider=None):
    a_parts = list(a) if isinstance(a, (list, tuple)) else [a]
    b_parts = list(b) if isinstance(b, (list, tuple)) else [b]
    assert len(b_parts) == 1 or mode == "tn"
    b0 = b_parts[0]
    shards = b0.shape[0] if b0.ndim == 3 else 1
    a_rows, a_cols = a_parts[0].shape[0], sum(p.shape[1] for p in a_parts)
    b_rows, b_cols = b0.shape[-2], sum(p.shape[-1] for p in b_parts) * shards
    if mode == "tn":
        (K, M), (K2, N) = (a_rows, a_cols), (b_rows, b_cols)
    elif mode == "nt":
        (M, K), (N, K2) = (a_rows, a_cols), (b_rows, b_cols)
    else:
        (M, K), (K2, N) = (a_rows, a_cols), (b_rows, b_cols)
    assert K == K2, (mode, K, K2)
    tm, tn, tk = min(tm, M), min(tn, N), min(tk, K)
    if shards > 1:
        assert mode != "tn" and b0.shape[-1] % (tk if mode == "nt" else tn) == 0
    assert M % tm == 0 and N % tn == 0 and K % tk == 0, (M, N, K, tm, tn, tk)
    nk = K // tk
    dot = _DOTS[mode]

    def spans(parts, tile):
        out, off = [], 0
        for p in parts:
            assert p.shape[-1] % tile == 0, (p.shape, tile)
            out.append((off, p.shape[-1] // tile))
            off += p.shape[-1] // tile
        return out

    a_axis = 0 if mode == "tn" else 2
    a_spans = spans(a_parts, tm if mode == "tn" else tk)
    b_spans = spans(b_parts, tn) if len(b_parts) > 1 else [(0, N // tn)]

    def inside(t, span):
        return jnp.logical_and(t >= span[0], t < span[0] + span[1])

    def body(*refs):
        a_refs, b_refs, rest = refs[:len(a_parts)], refs[len(a_parts):len(a_parts) + len(b_parts)], refs[len(a_parts) + len(b_parts):]
        r_ref = rest[0] if res is not None else None
        o_ref = rest[1] if res is not None else rest[0]
        k = pl.program_id(2)

        def finish(acc):
            if res is not None:
                acc = acc + r_ref[...]
            o_ref[...] = acc.astype(o_ref.dtype).reshape(o_ref.shape)

        def emit(a_ref, b_ref, conds, k_lo, k_hi):
            def region(*more):
                cs = conds + list(more)
                return pl.when(functools.reduce(jnp.logical_and, cs)) if cs else (lambda f: f())

            def product():
                return dot(a_ref[...], b_ref[0] if shards > 1 else b_ref[...])

            if nk == 1:
                region()(lambda: finish(product()))
                return
            acc_ref = rest[-1]
            if k_lo == 0:
                @region(k == 0)
                def _():
                    acc_ref[...] = product()

            if max(k_lo, 1) < min(k_hi, nk - 1):
                @region(k > 0, k < nk - 1)
                def _():
                    acc_ref[...] += product()

            if k_hi == nk:
                @region(k == nk - 1)
                def _():
                    finish(acc_ref[...] + product())

        for a_ref, a_span in zip(a_refs, a_spans):
            for b_ref, b_span in zip(b_refs, b_spans):
                by_k = a_axis == 2 and len(a_parts) > 1
                emit(a_ref, b_ref, ([inside(pl.program_id(a_axis), a_span)] if len(a_parts) > 1 else []) +
                     ([inside(pl.program_id(1), b_span)] if len(b_parts) > 1 else []),
                     a_span[0] if by_k else 0, a_span[0] + a_span[1] if by_k else nk)

    def piece_index(t, span):
        return jnp.clip(t - span[0], 0, span[1] - 1)

    def a_spec_of(span):
        if len(a_parts) == 1:
            return pl.BlockSpec((tk, tm), lambda i, j, k: (k, i)) if mode == "tn" else pl.BlockSpec((tm, tk), lambda i, j, k: (i, k))
        if mode == "tn":
            return pl.BlockSpec((tk, tm), lambda i, j, k: (jnp.where(inside(i, span), k, 0), piece_index(i, span)))
        return pl.BlockSpec((tm, tk), lambda i, j, k: (i, piece_index(k, span)))

    def b_spec_of(span):
        if shards > 1 and mode == "nn":
            per = b0.shape[-1] // tn
            return pl.BlockSpec((1, tk, tn), lambda i, j, k: (j // per, k, j % per))
        if shards > 1:
            per = b0.shape[-1] // tk
            return pl.BlockSpec((1, tn, tk), lambda i, j, k: (k // per, j, k % per))
        if mode == "nt":
            return pl.BlockSpec((tn, tk), lambda i, j, k: (j, k))
        if len(b_parts) == 1:
            return pl.BlockSpec((tk, tn), lambda i, j, k: (k, j))
        return pl.BlockSpec((tk, tn), lambda i, j, k: (jnp.where(inside(j, span), k, 0), piece_index(j, span)))

    o_spec = pl.BlockSpec((tm, tn), lambda i, j, k: (i, j))
    in_specs = [a_spec_of(s) for s in a_spans] + [b_spec_of(s) for s in b_spans]
    args = a_parts + b_parts
    if res is not None:
        in_specs.append(o_spec)
        args.append(res)
    out_shape = SDS((M, N), out_dtype)
    if out_layout is not None:
        out_shape, o_spec = SDS(out_layout[0], out_dtype), pl.BlockSpec(out_layout[1], out_layout[2])
    outs, rider_outs = _call(
        body, name=name, grid=(M // tm, N // tn, nk), in_specs=in_specs, out_specs=[o_spec], out_shape=[out_shape],
        scratch_shapes=[pltpu.VMEM((tm, tn), F32)] if nk > 1 else [], sem=("parallel", "parallel", "arbitrary"),
        args=args, rider=rider)
    return outs[0] if rider is None else (outs[0], rider_outs)


ROW_TILE = 512
ROW_PIECE = 512


def _rms(x, g):
    return x * lax.rsqrt(jnp.mean(x * x, axis=-1, keepdims=True) + EPS) * g


def rmsnorm_fwd(x, g, *, name):
    T, D = x.shape

    def body(x_ref, g_ref, o_ref):
        o_ref[...] = _rms(x_ref[...], g_ref[...]).astype(BF16)

    row = pl.BlockSpec((ROW_TILE, D), lambda i: (i, 0))
    return pl.pallas_call(body, name=name, grid=(T // ROW_TILE,), in_specs=[row, _full((1, D))], out_specs=row,
                          out_shape=SDS((T, D), BF16), compiler_params=_cp("parallel"))(x, g)


def rmsnorm_bwd(x, g, dxn, dres, *, name, rider=None):
    T, D = x.shape

    def body(x_ref, g_ref, dxn_ref, dres_ref, dx_ref, dxb_ref, dg_ref):
        _, vjp = jax.vjp(_rms, x_ref[...], g_ref[...])
        dx, dg = vjp(dxn_ref[...])
        dx = dx + dres_ref[...]
        dx_ref[...] = dx
        dxb_ref[...] = dx.astype(BF16)

        @pl.when(pl.program_id(0) == 0)
        def _():
            dg_ref[...] = jnp.zeros_like(dg_ref)

        dg_ref[...] += dg

    row = pl.BlockSpec((ROW_TILE, D), lambda i: (i, 0))
    outs, rider_outs = _call(
        body, name=name, grid=(T // ROW_TILE,), in_specs=[row, _full((1, D)), row, row],
        out_specs=[row, row, _full((1, D))], out_shape=[SDS((T, D), F32), SDS((T, D), BF16), SDS((1, D), F32)],
        scratch_shapes=[], sem=("arbitrary",), args=(x, g, dxn, dres), rider=rider)
    return outs if rider is None else (outs, rider_outs)


def _loss_tile(x, g, tgt):
    err = jnp.square(_rms(x, g) - tgt)
    return 0.5 * jnp.sum(jnp.mean(err, axis=-1))


def loss_head(x, g, tgt):
    T, D = x.shape

    def body(x_ref, g_ref, t_ref, loss_ref, dx_ref, dxb_ref, dg_ref):
        loss, vjp = jax.vjp(_loss_tile, x_ref[...], g_ref[...], t_ref[...])
        dx, dg, _ = vjp(jnp.ones((), F32))
        dx_ref[...] = dx
        dxb_ref[...] = dx.astype(BF16)

        @pl.when(pl.program_id(0) == 0)
        def _():
            dg_ref[...] = jnp.zeros_like(dg_ref)
            loss_ref[...] = jnp.zeros_like(loss_ref)

        dg_ref[...] += dg
        loss_ref[...] += jnp.reshape(loss, (1, 1))

    row = pl.BlockSpec((ROW_TILE, D), lambda i: (i, 0))
    return pl.pallas_call(
        body, name="loss_head", grid=(T // ROW_TILE,), in_specs=[row, _full((1, D)), row],
        out_specs=[_full((1, 1)), row, row, _full((1, D))],
        out_shape=[SDS((1, 1), F32), SDS((T, D), F32), SDS((T, D), BF16), SDS((1, D), F32)],
        compiler_params=_cp("arbitrary"))(x, g, tgt)


TILE_BYTES = 1 << 20


def _row_tile(rows, row_bytes):
    for cand in (512, 256, 128, 64, 32, 16, 8):
        if rows % cand == 0 and cand * row_bytes <= TILE_BYTES:
            return cand
    return rows


def adamw(w, g, m, v, *, name):
    R, C = w.shape
    tr = _row_tile(R, C * 4)

    def body(w_ref, g_ref, m_ref, v_ref, d_ref, nm_ref, nv_ref):
        gg = g_ref[...]
        mm = ADAM_B1 * m_ref[...] + (1.0 - ADAM_B1) * gg
        vv = ADAM_B2 * v_ref[...] + (1.0 - ADAM_B2) * jnp.square(gg)
        m_hat = mm / (1.0 - ADAM_B1 ** ADAM_STEP)
        v_hat = vv / (1.0 - ADAM_B2 ** ADAM_STEP)
        d_ref[...] = -ADAM_LR * (m_hat / (jnp.sqrt(v_hat) + ADAM_EPS) + ADAM_WD * w_ref[...])
        nm_ref[...] = mm
        nv_ref[...] = vv

    blk = pl.BlockSpec((tr, C), lambda i: (i, 0))
    return pl.pallas_call(body, name=name, grid=(R // tr,), in_specs=[blk] * 4, out_specs=[blk] * 3,
                          out_shape=[SDS((R, C), F32)] * 3, compiler_params=_cp("parallel"))(w, g, m, v)


def sum_leading(a, *, name, rider=None):
    n, R, C = a.shape
    tr = _row_tile(R, n * C * 4)

    def body(a_ref, o_ref):
        acc = a_ref[0]
        for j in range(1, n):
            acc = acc + a_ref[j]
        o_ref[...] = acc

    outs, rider_outs = _call(body, name=name, grid=(R // tr,), in_specs=[pl.BlockSpec((n, tr, C), lambda i: (0, i, 0))],
                             out_specs=[pl.BlockSpec((tr, C), lambda i: (i, 0))], out_shape=[SDS((R, C), F32)],
                             scratch_shapes=[], sem=("parallel",), args=(a,), rider=rider)
    return outs[0] if rider is None else (outs[0], rider_outs)


def chip_sum(pieces, from_sibling, core, *, name):
    _, n, R, C = pieces.shape

    def body(c_ref, a_ref, b_ref, o_ref):
        o_ref[...] = (a_ref[0] + b_ref[...]).astype(BF16)

    tr = _row_tile(R, C * 4)
    blk = pl.BlockSpec((1, tr, C), lambda k, i, c_ref: (k, i, 0))
    mine = pl.BlockSpec((1, 1, tr, C), lambda k, i, c_ref: (c_ref[0], k, i, 0))
    return pl.pallas_call(
        body, name=name, out_shape=SDS((n, R, C), BF16),
        grid_spec=pltpu.PrefetchScalarGridSpec(num_scalar_prefetch=1, grid=(n, R // tr), in_specs=[mine, blk], out_specs=blk),
        compiler_params=_cp("parallel", "parallel"))(core.reshape(1), pieces, from_sibling)


def total_sum(sums, landed, chip, *, name):
    n, R, C = landed.shape

    def body(k_ref, s_ref, l_ref, o_ref):
        acc = s_ref[0].astype(F32)
        for j in range(n):
            acc = acc + l_ref[j].astype(F32)
        o_ref[...] = acc

    tr = _row_tile(R, n * C * 2)
    return pl.pallas_call(
        body, name=name, out_shape=SDS((R, C), F32),
        grid_spec=pltpu.PrefetchScalarGridSpec(
            num_scalar_prefetch=1, grid=(R // tr,),
            in_specs=[pl.BlockSpec((1, tr, C), lambda i, k_ref: (k_ref[0], i, 0)), pl.BlockSpec((n, tr, C), lambda i, k_ref: (0, i, 0))],
            out_specs=pl.BlockSpec((tr, C), lambda i, k_ref: (i, 0))),
        compiler_params=_cp("parallel"))(chip.reshape(1), sums, landed)


def _gmlp_chunk(u, v, z, ln_g, ln_b, wsc, bs_t):
    mu = jnp.mean(v, axis=-1, keepdims=True)
    xc = v - mu
    vn = xc * lax.rsqrt(jnp.mean(xc * xc, axis=-1, keepdims=True) + EPS) * ln_g + ln_b
    gw = A_WIDTH // A_GROUPS
    outs = []
    for g in range(A_GROUPS):
        m = _dot_nn(wsc[g].astype(BF16), vn[:, g * gw:(g + 1) * gw].astype(BF16))
        outs.append(m + bs_t[:, g:g + 1])
    return _silu(z) * (u * jnp.concatenate(outs, axis=1))


def _h_cols(width, idx, rows=CHUNK):
    return pl.BlockSpec((rows, width), lambda i: (i, idx))


def gmlp_fwd(h, ln_g, ln_b, ws, bs_t):
    T = h.shape[0]

    def body(u_ref, v_ref, z_ref, g_ref, b_ref, ws_ref, bs_ref, o_ref):
        wsc = jnp.where(_tril(CHUNK)[None], ws_ref[...], 0.0)
        o_ref[...] = _gmlp_chunk(u_ref[...], v_ref[...], z_ref[...], g_ref[...], b_ref[...], wsc, bs_ref[...]).astype(BF16)

    return pl.pallas_call(
        body, name="gmlp_fwd", grid=(T // CHUNK,),
        in_specs=[_h_cols(A_WIDTH, 0), _h_cols(A_WIDTH, 1), _h_cols(A_WIDTH, 2), _full((1, A_WIDTH)), _full((1, A_WIDTH)),
                  _full((A_GROUPS, CHUNK, CHUNK)), _full((CHUNK, A_GROUPS))],
        out_specs=_h_cols(A_WIDTH, 0), out_shape=SDS((T, A_WIDTH), BF16), compiler_params=_cp("parallel"),
    )(h, h, h, ln_g, ln_b, ws, bs_t)


def gmlp_bwd(h, dy, ln_g, ln_b, ws, bs_t):
    T = h.shape[0]

    def body(u_ref, v_ref, z_ref, dy_ref, g_ref, b_ref, ws_ref, bs_ref, duvz_ref, dg_ref, db_ref, dws_ref, dbs_ref):
        tri = _tril(CHUNK)[None]
        wsc = jnp.where(tri, ws_ref[...], 0.0)
        _, vjp = jax.vjp(_gmlp_chunk, u_ref[...], v_ref[...], z_ref[...], g_ref[...], b_ref[...], wsc, bs_ref[...])
        du, dv, dz, dg, db, dws, dbs = vjp(dy_ref[...])
        duvz_ref[:, :A_WIDTH] = du.astype(BF16)
        duvz_ref[:, A_WIDTH:2 * A_WIDTH] = dv.astype(BF16)
        duvz_ref[:, 2 * A_WIDTH:] = dz.astype(BF16)

        @pl.when(pl.program_id(0) == 0)
        def _():
            dg_ref[...] = jnp.zeros_like(dg_ref)
            db_ref[...] = jnp.zeros_like(db_ref)
            dws_ref[...] = jnp.zeros_like(dws_ref)
            dbs_ref[...] = jnp.zeros_like(dbs_ref)

        dg_ref[...] += dg
        db_ref[...] += db
        dws_ref[...] += jnp.where(tri, dws, 0.0)
        dbs_ref[...] += dbs

    pshapes = [(1, A_WIDTH), (1, A_WIDTH), (A_GROUPS, CHUNK, CHUNK), (CHUNK, A_GROUPS)]
    return pl.pallas_call(
        body, name="gmlp_bwd", grid=(T // CHUNK,),
        in_specs=[_h_cols(A_WIDTH, 0), _h_cols(A_WIDTH, 1), _h_cols(A_WIDTH, 2), _h_cols(A_WIDTH, 0)] + [_full(s) for s in pshapes],
        out_specs=[_h_cols(3 * A_WIDTH, 0)] + [_full(s) for s in pshapes],
        out_shape=[SDS((T, 3 * A_WIDTH), BF16)] + [SDS(s, F32) for s in pshapes],
        compiler_params=_cp("arbitrary"),
    )(h, h, h, dy, ln_g, ln_b, ws, bs_t)


CONV_ROWS = 256
CONV_COLS = 512


def _taps(xe, w, rows):
    K = w.shape[0]
    acc = None
    for k in range(K):
        off = SUBLANES - (K - 1) + k
        term = w[k:k + 1, :] * xe[off:off + rows, :]
        acc = term if acc is None else acc + term
    return acc


def _ssd_conv_tile(x, halo, w, b):
    return _silu(_taps(jnp.concatenate([halo, x], axis=0), w, x.shape[0]) + b)


def _halo_spec(cols, col_idx, nt=None):
    rpb = CONV_ROWS // SUBLANES
    if nt is None:
        return pl.BlockSpec((SUBLANES, cols), lambda c, i: (jnp.maximum(i * rpb - 1, 0), col_idx(c)))
    return pl.BlockSpec((SUBLANES, cols), lambda c, j: (jnp.maximum((nt - 1 - j) * rpb - 1, 0), col_idx(c)))


def ssd_conv_fwd(h, w, b):
    T = h.shape[0]
    nc = B_XBC // CONV_COLS
    base = (3 * A_WIDTH + B_WIDTH) // CONV_COLS

    def body(x_ref, halo_ref, w_ref, b_ref, o_ref):
        halo = jnp.where(pl.program_id(1) > 0, halo_ref[...], 0.0)
        o_ref[...] = _ssd_conv_tile(x_ref[...], halo, w_ref[...], b_ref[...])

    return pl.pallas_call(
        body, name="ssd_conv_fwd", grid=(nc, T // CONV_ROWS),
        in_specs=[pl.BlockSpec((CONV_ROWS, CONV_COLS), lambda c, i: (i, base + c)), _halo_spec(CONV_COLS, lambda c: base + c),
                  pl.BlockSpec((B_CONV, CONV_COLS), lambda c, i: (0, c)), pl.BlockSpec((1, CONV_COLS), lambda c, i: (0, c))],
        out_specs=pl.BlockSpec((CONV_ROWS, CONV_COLS), lambda c, i: (i, c)),
        out_shape=SDS((T, B_XBC), F32), compiler_params=_cp("parallel", "parallel"),
    )(h, h, w, b)


def ssd_conv_bwd(h, dy, w, b, rider=None):
    T = h.shape[0]
    nc = B_XBC // CONV_COLS
    nt = T // CONV_ROWS
    base = (3 * A_WIDTH + B_WIDTH) // CONV_COLS

    def body(x_ref, halo_ref, dy_ref, w_ref, b_ref, dx_ref, dw_ref, db_ref, carry_ref):
        j = pl.program_id(1)
        halo = jnp.where(j < nt - 1, halo_ref[...], 0.0)
        _, vjp = jax.vjp(_ssd_conv_tile, x_ref[...], halo, w_ref[...], b_ref[...])
        dx, dhalo, dw, db = vjp(dy_ref[...])

        @pl.when(j == 0)
        def _():
            carry_ref[...] = jnp.zeros_like(carry_ref)
            dw_ref[...] = jnp.zeros_like(dw_ref)
            db_ref[...] = jnp.zeros_like(db_ref)

        tail = dx[CONV_ROWS - SUBLANES:, :] + carry_ref[...]
        dx_ref[...] = jnp.concatenate([dx[:CONV_ROWS - SUBLANES, :], tail], axis=0).astype(BF16)
        carry_ref[...] = dhalo
        dw_ref[...] += dw
        db_ref[...] += db

    outs, rider_outs = _call(
        body, name="ssd_conv_bwd", grid=(nc, nt),
        in_specs=[pl.BlockSpec((CONV_ROWS, CONV_COLS), lambda c, j: (nt - 1 - j, base + c)),
                  _halo_spec(CONV_COLS, lambda c: base + c, nt),
                  pl.BlockSpec((CONV_ROWS, CONV_COLS), lambda c, j: (nt - 1 - j, c)),
                  pl.BlockSpec((B_CONV, CONV_COLS), lambda c, j: (0, c)), pl.BlockSpec((1, CONV_COLS), lambda c, j: (0, c))],
        out_specs=[pl.BlockSpec((CONV_ROWS, CONV_COLS), lambda c, j: (nt - 1 - j, c)),
                   pl.BlockSpec((B_CONV, CONV_COLS), lambda c, j: (0, c)), pl.BlockSpec((1, CONV_COLS), lambda c, j: (0, c))],
        out_shape=[SDS((T, B_XBC), BF16), SDS((B_CONV, B_XBC), F32), SDS((1, B_XBC), F32)],
        scratch_shapes=[pltpu.VMEM((SUBLANES, CONV_COLS), F32)], sem=("parallel", "arbitrary"),
        args=(h, h, dy, w, b), rider=rider)
    return outs if rider is None else (outs, rider_outs)


def _sconv_tile(bg, cg, hx, z, cg_halo, hx_halo, w):
    ch = jnp.concatenate([cg_halo * hx_halo, cg * hx], axis=0)
    return _silu(z) * (bg * _taps(ch, w, bg.shape[0]))


def sconv_fwd(h, w):
    T = h.shape[0]
    nc = C_WIDTH // CONV_COLS

    def col(seg):
        return pl.BlockSpec((CONV_ROWS, CONV_COLS), lambda c, i: (i, seg * nc + c))

    def body(bg_ref, cg_ref, hx_ref, z_ref, cgh_ref, hxh_ref, w_ref, o_ref):
        first = pl.program_id(1) == 0
        cgh = jnp.where(first, 0.0, cgh_ref[...])
        hxh = jnp.where(first, 0.0, hxh_ref[...])
        o_ref[...] = _sconv_tile(bg_ref[...], cg_ref[...], hx_ref[...], z_ref[...], cgh, hxh, w_ref[...]).astype(BF16)

    return pl.pallas_call(
        body, name="sconv_fwd", grid=(nc, T // CONV_ROWS),
        in_specs=[col(0), col(1), col(2), col(3), _halo_spec(CONV_COLS, lambda c: nc + c), _halo_spec(CONV_COLS, lambda c: 2 * nc + c),
                  pl.BlockSpec((C_CONV, CONV_COLS), lambda c, i: (0, c))],
        out_specs=pl.BlockSpec((CONV_ROWS, CONV_COLS), lambda c, i: (i, c)),
        out_shape=SDS((T, C_WIDTH), BF16), compiler_params=_cp("parallel", "parallel"),
    )(h, h, h, h, h, h, w)


def sconv_bwd(h, dy, w):
    T = h.shape[0]
    nc = C_WIDTH // CONV_COLS
    nt = T // CONV_ROWS

    def col(seg):
        return pl.BlockSpec((CONV_ROWS, CONV_COLS), lambda c, j: (nt - 1 - j, seg * nc + c))

    def body(bg_ref, cg_ref, hx_ref, z_ref, cgh_ref, hxh_ref, dy_ref, w_ref, dbg_ref, dcg_ref, dhx_ref, dz_ref, dw_ref, ccg_ref, chx_ref):
        j = pl.program_id(1)
        first = j == nt - 1
        cgh = jnp.where(first, 0.0, cgh_ref[...])
        hxh = jnp.where(first, 0.0, hxh_ref[...])
        _, vjp = jax.vjp(_sconv_tile, bg_ref[...], cg_ref[...], hx_ref[...], z_ref[...], cgh, hxh, w_ref[...])
        dbg, dcg, dhx, dz, dcgh, dhxh, dw = vjp(dy_ref[...])

        @pl.when(j == 0)
        def _():
            ccg_ref[...] = jnp.zeros_like(ccg_ref)
            chx_ref[...] = jnp.zeros_like(chx_ref)
            dw_ref[...] = jnp.zeros_like(dw_ref)

        def with_carry(d, carry_ref):
            tail = d[CONV_ROWS - SUBLANES:, :] + carry_ref[...]
            return jnp.concatenate([d[:CONV_ROWS - SUBLANES, :], tail], axis=0).astype(BF16)

        dbg_ref[...] = dbg.astype(BF16)
        dz_ref[...] = dz.astype(BF16)
        dcg_ref[...] = with_carry(dcg, ccg_ref)
        dhx_ref[...] = with_carry(dhx, chx_ref)
        ccg_ref[...] = dcgh
        chx_ref[...] = dhxh
        dw_ref[...] += dw

    out_row = pl.BlockSpec((CONV_ROWS, CONV_COLS), lambda c, j: (nt - 1 - j, c))
    wspec = pl.BlockSpec((C_CONV, CONV_COLS), lambda c, j: (0, c))
    return pl.pallas_call(
        body, name="sconv_bwd", grid=(nc, nt),
        in_specs=[col(0), col(1), col(2), col(3), _halo_spec(CONV_COLS, lambda c: nc + c, nt), _halo_spec(CONV_COLS, lambda c: 2 * nc + c, nt),
                  out_row, wspec],
        out_specs=[out_row] * 4 + [wspec],
        out_shape=[SDS((T, C_WIDTH), BF16)] * 4 + [SDS((C_CONV, C_WIDTH), F32)],
        scratch_shapes=[pltpu.VMEM((SUBLANES, CONV_COLS), F32)] * 2,
        compiler_params=_cp("parallel", "arbitrary"),
    )(h, h, h, h, h, h, dy, w)


def _softplus(x):
    return jnp.maximum(x, 0.0) + jnp.log(1.0 + jnp.exp(-jnp.abs(x)))


def _ssd_chunk(xs, bm, cm, dtr, z, prev, dt_bias, a_log, d_skip, norm_g):
    tril = _tril(CHUNK)
    dt = _softplus(dtr + dt_bias)
    adt = dt * (-jnp.exp(a_log))
    a_cs = jnp.dot(tril.astype(F32), adt, precision=lax.Precision.HIGHEST, preferred_element_type=F32)
    a_cs_t = a_cs.T
    a_last = a_cs[CHUNK - 1:CHUNK, :]
    dt_f = _spread_heads(dt, B_HEAD_DIM)
    dec_f = _spread_heads(jnp.exp(a_last - a_cs), B_HEAD_DIM)
    ecs_f = _spread_heads(jnp.exp(a_cs), B_HEAD_DIM)
    dsk_f = _spread_heads(d_skip, B_HEAD_DIM)
    cd_t = jnp.exp(a_cs_t[:, CHUNK - 1:CHUNK])
    xdt = xs * dt_f
    xdd = xdt * dec_f
    colb = _spread_heads(a_cs, CHUNK)
    rowb = jnp.concatenate([jnp.broadcast_to(a_cs_t[hh:hh + 1, :], (CHUNK, CHUNK)) for hh in range(B_HEADS)], axis=1)
    wide = (CHUNK, B_HEADS * CHUNK)
    keep = lax.broadcasted_iota(jnp.int32, wide, 0) >= lax.broadcasted_iota(jnp.int32, wide, 1) % CHUNK
    decay = jnp.exp(jnp.where(keep, colb - rowb, -jnp.inf))
    hpg = B_HEADS // B_GROUPS
    gw = B_WIDTH // B_GROUPS
    low_half = lax.broadcasted_iota(jnp.int32, (CHUNK, 2 * B_HEAD_DIM), 1) < B_HEAD_DIM
    ys, nxt = [], []
    for g in range(B_GROUPS):
        bg = bm[:, g * B_STATE:(g + 1) * B_STATE].astype(BF16)
        cg = cm[:, g * B_STATE:(g + 1) * B_STATE].astype(BF16)
        cb = _dot_nt(cg, bg)
        cbl = (decay[:, g * hpg * CHUNK:(g + 1) * hpg * CHUNK] * jnp.concatenate([cb] * hpg, axis=1)).astype(BF16)
        pg = prev[g * gw:(g + 1) * gw, :]
        y_off = _dot_nt(cg, pg.astype(BF16)) * ecs_f[:, g * gw:(g + 1) * gw]
        st = _dot_tn(xdd[:, g * gw:(g + 1) * gw].astype(BF16), bg)
        cd = jnp.concatenate([jnp.broadcast_to(cd_t[g * hpg + r:g * hpg + r + 1, :], (B_HEAD_DIM, 1)) for r in range(hpg)], axis=0)
        nxt.append(pg * cd + st)
        pairs = []
        for j in range(hpg // 2):
            xp = xdt[:, g * gw + 2 * j * B_HEAD_DIM:g * gw + 2 * (j + 1) * B_HEAD_DIM]
            rhs = jnp.concatenate([jnp.where(low_half, xp, 0.0), jnp.where(low_half, 0.0, xp)], axis=0).astype(BF16)
            pairs.append(_dot_nn(cbl[:, 2 * j * CHUNK:2 * (j + 1) * CHUNK], rhs))
        ys.append(jnp.concatenate(pairs, axis=1) + y_off)
    y = (jnp.concatenate(ys, axis=1) + dsk_f * xs) * _silu(z)
    outs = []
    for g in range(B_GROUPS):
        yg = y[:, g * gw:(g + 1) * gw]
        outs.append(yg * lax.rsqrt(jnp.mean(yg * yg, axis=-1, keepdims=True) + EPS))
    return jnp.concatenate(outs, axis=1) * norm_g, jnp.concatenate(nxt, axis=0)


def _split3(v):
    hi = v.astype(BF16)
    r1 = v - hi.astype(F32)
    mid = r1.astype(BF16)
    return hi, mid, (r1 - mid.astype(F32)).astype(BF16)


def _head_one_hot(width, parts):
    n = B_HEADS * width
    shape = (parts * LANES, n)
    return (lax.broadcasted_iota(jnp.int32, shape, 0) % LANES == lax.broadcasted_iota(jnp.int32, shape, 1) // width).astype(BF16)


@functools.partial(jax.custom_vjp, nondiff_argnums=(1,))
def _spread_heads(v, width):
    return _dot_nn(jnp.concatenate(_split3(v), axis=1), _head_one_hot(width, 3))


def _spread_heads_fwd(v, width):
    return _spread_heads(v, width), None


def _spread_heads_bwd(width, _, g):
    return (_dot_nt(jnp.concatenate(_split3(g), axis=1), jnp.concatenate([_head_one_hot(width, 1)] * 3, axis=1)),)


_spread_heads.defvjp(_spread_heads_fwd, _spread_heads_bwd)


_SSD_PARAM_SHAPES = [(1, LANES), (1, LANES), (1, LANES), (1, B_WIDTH)]
_STATE_SHAPE = (B_WIDTH, B_STATE)


def ssd_fwd(xbc, dtr, h, dt_bias, a_log, d_skip, norm_g, rider=None):
    T = xbc.shape[0]
    nc = T // CHUNK

    def body(xs_ref, b_ref, c_ref, dt_ref, z_ref, p0, p1, p2, p3, y_ref, st_ref, state):
        @pl.when(pl.program_id(0) == 0)
        def _():
            state[...] = jnp.zeros_like(state)

        prev = state[...]
        st_ref[0] = prev
        yb, nxt = _ssd_chunk(xs_ref[...], b_ref[...], c_ref[...], dt_ref[...], z_ref[...], prev, p0[...], p1[...], p2[...], p3[...])
        y_ref[...] = yb.astype(BF16)
        state[...] = nxt

    outs, rider_outs = _call(
        body, name="ssd_fwd", grid=(nc,),
        in_specs=[_h_cols(B_WIDTH, 0), _h_cols(B_GROUPS * B_STATE, 2), _h_cols(B_GROUPS * B_STATE, 3), _h_cols(LANES, 0), _h_cols(B_WIDTH, 3)]
        + [_full(s) for s in _SSD_PARAM_SHAPES],
        out_specs=[_h_cols(B_WIDTH, 0), pl.BlockSpec((1,) + _STATE_SHAPE, lambda i: (i, 0, 0))],
        out_shape=[SDS((T, B_WIDTH), BF16), SDS((nc,) + _STATE_SHAPE, F32)],
        scratch_shapes=[pltpu.VMEM(_STATE_SHAPE, F32)], sem=("arbitrary",),
        args=(xbc, xbc, xbc, dtr, h, dt_bias, a_log, d_skip, norm_g), rider=rider)
    return outs if rider is None else (outs, rider_outs)


def ssd_bwd(xbc, dtr, h, states, dy, dt_bias, a_log, d_skip, norm_g, rider=None):
    T = xbc.shape[0]
    nc = T // CHUNK

    def rev(width, idx):
        return pl.BlockSpec((CHUNK, width), lambda j: (nc - 1 - j, idx))

    def body(xs_ref, b_ref, c_ref, dt_ref, z_ref, st_ref, dy_ref, p0, p1, p2, p3,
             dxbc_ref, ddt_ref, dz_ref, g0, g1, g2, g3, dstate):
        @pl.when(pl.program_id(0) == 0)
        def _():
            dstate[...] = jnp.zeros_like(dstate)
            for gref in (g0, g1, g2, g3):
                gref[...] = jnp.zeros_like(gref)

        _, vjp = jax.vjp(_ssd_chunk, xs_ref[...], b_ref[...], c_ref[...], dt_ref[...], z_ref[...], st_ref[0],
                         p0[...], p1[...], p2[...], p3[...])
        dxs, dbm, dcm, ddt, dz, dprev, d0, d1, d2, d3 = vjp((dy_ref[...], dstate[...]))
        dxbc_ref[:, :B_WIDTH] = dxs
        dxbc_ref[:, B_WIDTH:B_WIDTH + gn] = dbm
        dxbc_ref[:, B_WIDTH + gn:] = dcm
        ddt_ref[...] = ddt.astype(BF16)
        dz_ref[...] = dz.astype(BF16)
        dstate[...] = dprev
        g0[...] += d0
        g1[...] += d1
        g2[...] += d2
        g3[...] += d3

    gn = B_GROUPS * B_STATE
    outs, rider_outs = _call(
        body, name="ssd_bwd", grid=(nc,),
        in_specs=[rev(B_WIDTH, 0), rev(gn, 2), rev(gn, 3), rev(LANES, 0), rev(B_WIDTH, 3),
                  pl.BlockSpec((1,) + _STATE_SHAPE, lambda j: (nc - 1 - j, 0, 0)), rev(B_WIDTH, 1)]
        + [_full(s) for s in _SSD_PARAM_SHAPES],
        out_specs=[rev(B_XBC, 0), rev(LANES, 0), rev(B_WIDTH, 0)] + [_full(s) for s in _SSD_PARAM_SHAPES],
        out_shape=[SDS((T, B_XBC), F32), SDS((T, LANES), BF16), SDS((T, B_WIDTH), BF16)]
        + [SDS(s, F32) for s in _SSD_PARAM_SHAPES],
        scratch_shapes=[pltpu.VMEM(_STATE_SHAPE, F32)], sem=("arbitrary",),
        args=(xbc, xbc, xbc, dtr, h, states, dy, dt_bias, a_log, d_skip, norm_g), rider=rider)
    return outs if rider is None else (outs, rider_outs)


ATT_SCALE = D_HEAD_DIM ** -0.5
Q_COL, K_COL, V_COL, Z_COL = (4 * C_WIDTH // LANES + i * D_HEADS for i in range(4))


ATT_NBLK = ATT_SUPER // ATT_BLOCK


def _res_rows(r, first, count, dil):
    return pl.ds(r + dil * first, count) if dil == 1 else pl.ds(r + dil * first, count, stride=dil)


def _blocks(ref, dil, dtype=None):
    n = ATT_SUPER // dil
    parts = []
    for r in range(dil):
        v = ref[_res_rows(r, 0, n, dil), :]
        parts.append((v if dtype is None else v.astype(dtype)).reshape(n // ATT_BLOCK, ATT_BLOCK, D_HEAD_DIM))
    return parts[0] if dil == 1 else jnp.concatenate(parts, axis=0)


def _blocks_before(cur_ref, prev_ref, dil, dtype):
    n = ATT_SUPER // dil
    parts = []
    for r in range(dil):
        v = prev_ref[_res_rows(r, n - ATT_BLOCK, ATT_BLOCK, dil), :]
        if n > ATT_BLOCK:
            v = jnp.concatenate([v, cur_ref[_res_rows(r, 0, n - ATT_BLOCK, dil), :]], axis=0)
        parts.append(v.astype(dtype).reshape(n // ATT_BLOCK, ATT_BLOCK, D_HEAD_DIM))
    return parts[0] if dil == 1 else jnp.concatenate(parts, axis=0)


def _blocks_after(cur_ref, next_ref, dil, dtype=None):
    n = ATT_SUPER // dil
    parts = []
    for r in range(dil):
        v = next_ref[_res_rows(r, 0, ATT_BLOCK, dil), :]
        if n > ATT_BLOCK:
            v = jnp.concatenate([cur_ref[_res_rows(r, ATT_BLOCK, n - ATT_BLOCK, dil), :], v], axis=0)
        parts.append((v if dtype is None else v.astype(dtype)).reshape(n // ATT_BLOCK, ATT_BLOCK, D_HEAD_DIM))
    return parts[0] if dil == 1 else jnp.concatenate(parts, axis=0)


def _unblock(ref, val, dil, add=False):
    n = ATT_SUPER // dil
    nb = n // ATT_BLOCK
    for r in range(dil):
        v = val[r * nb:(r + 1) * nb].reshape(n, D_HEAD_DIM)
        if add:
            ref[_res_rows(r, 0, n, dil), :] += v
        else:
            ref[_res_rows(r, 0, n, dil), :] = v


def _att_masks(dil, edge_ok, edge_last=False):
    shape = (ATT_NBLK, ATT_BLOCK, ATT_BLOCK)
    blk = lax.broadcasted_iota(jnp.int32, shape, 0)
    row = lax.broadcasted_iota(jnp.int32, shape, 1)
    col = lax.broadcasted_iota(jnp.int32, shape, 2)
    nb = ATT_NBLK // dil
    at_edge = (blk % nb) == (nb - 1 if edge_last else 0)
    return col <= row, jnp.logical_and(col >= row, jnp.logical_or(jnp.logical_not(at_edge), edge_ok))


def _bdot_nt(a, b):
    return lax.dot_general(a, b, (((2,), (2,)), ((0,), (0,))), preferred_element_type=F32)


def _bdot_nn(a, b):
    return lax.dot_general(a, b, (((2,), (1,)), ((0,), (0,))), preferred_element_type=F32)


def _bdot_tn(a, b):
    return lax.dot_general(a, b, (((1,), (1,)), ((0,), (0,))), preferred_element_type=F32)


def _att_spec(col0, shift=0, last=None):
    def imap(hh, n):
        m = n + shift
        if shift < 0:
            m = jnp.maximum(m, 0)
        if shift > 0:
            m = jnp.minimum(m, last)
        return (m, col0 + hh)
    return pl.BlockSpec((ATT_SUPER, D_HEAD_DIM), imap)


def _att_out_spec():
    return pl.BlockSpec((ATT_SUPER, D_HEAD_DIM), lambda hh, n: (n, hh))


def attn_fwd(h):
    T = h.shape[0]
    npat = len(D_PATTERNS)

    def body(q_ref, kc_ref, kp_ref, vc_ref, vp_ref, z_ref, yd_ref, o_ref, lse_ref, *scratch):
        o_s, l_s = scratch[:npat], scratch[npat:]
        has_prev = pl.program_id(1) > 0
        for pi, (_, dil) in enumerate(D_PATTERNS):
            mask_c, mask_p = _att_masks(dil, has_prev)
            q = _blocks(q_ref, dil, BF16)
            kc, vc = _blocks(kc_ref, dil, BF16), _blocks(vc_ref, dil, BF16)
            kp, vp = _blocks_before(kc_ref, kp_ref, dil, BF16), _blocks_before(vc_ref, vp_ref, dil, BF16)
            s_c = jnp.where(mask_c, _bdot_nt(q, kc) * ATT_SCALE, -jnp.inf)
            s_p = jnp.where(mask_p, _bdot_nt(q, kp) * ATT_SCALE, -jnp.inf)
            m = jnp.maximum(jnp.max(s_c, axis=-1, keepdims=True), jnp.max(s_p, axis=-1, keepdims=True))
            p_c = jnp.exp(s_c - m)
            p_p = jnp.exp(s_p - m)
            l = jnp.sum(p_c, axis=-1, keepdims=True) + jnp.sum(p_p, axis=-1, keepdims=True)
            o = _bdot_nn((p_c / l).astype(BF16), vc) + _bdot_nn((p_p / l).astype(BF16), vp)
            _unblock(o_s[pi], o, dil)
            _unblock(l_s[pi], jnp.broadcast_to(m + jnp.log(l), o.shape), dil)
        lses = [l_s[pi][...] for pi in range(npat)]
        mx = functools.reduce(jnp.maximum, lses)
        ws = [jnp.exp(l - mx) for l in lses]
        den = functools.reduce(lambda a, b: a + b, ws)
        o = functools.reduce(lambda a, b: a + b, [(w / den) * o_s[pi][...] for pi, w in enumerate(ws)])
        o_ref[...] = o
        lse_ref[...] = mx + jnp.log(den)
        yd_ref[...] = (_silu(z_ref[...]) * o).astype(BF16)

    n_super = T // ATT_SUPER
    return pl.pallas_call(
        body, name="attn_fwd", grid=(D_HEADS, n_super),
        in_specs=[_att_spec(Q_COL), _att_spec(K_COL), _att_spec(K_COL, -1), _att_spec(V_COL), _att_spec(V_COL, -1), _att_spec(Z_COL)],
        out_specs=[_att_out_spec()] * 3,
        out_shape=[SDS((T, D_HEADS * D_HEAD_DIM), BF16), SDS((T, D_HEADS * D_HEAD_DIM), F32), SDS((T, D_HEADS * D_HEAD_DIM), F32)],
        scratch_shapes=[pltpu.VMEM((ATT_SUPER, D_HEAD_DIM), F32)] * (2 * npat),
        compiler_params=_cp("parallel", "arbitrary"),
    )(h, h, h, h, h, h)


def _dsilu(z):
    s = jax.nn.sigmoid(z)
    return s * (1.0 + z * (1.0 - s))


def attn_bwd_dq(h, o, lse, dy):
    T = h.shape[0]
    dy_col = C_WIDTH // LANES

    def body(q_ref, kc_ref, kp_ref, vc_ref, vp_ref, z_ref, o_ref, lse_ref, dy_ref, dq_ref, dz_ref, do_s, dd_s, dq_s):
        has_prev = pl.program_id(1) > 0
        z, oo, dyd = z_ref[...], o_ref[...], dy_ref[...]
        do = dyd * _silu(z)
        dz_ref[...] = (dyd * oo * _dsilu(z)).astype(BF16)
        do_s[...] = do
        dd_s[...] = jnp.broadcast_to(jnp.sum(do * oo, axis=-1, keepdims=True), (ATT_SUPER, D_HEAD_DIM))
        for pi, (_, dil) in enumerate(D_PATTERNS):
            mask_c, mask_p = _att_masks(dil, has_prev)
            q = _blocks(q_ref, dil, BF16)
            kc, vc = _blocks(kc_ref, dil, BF16), _blocks(vc_ref, dil, BF16)
            kp, vp = _blocks_before(kc_ref, kp_ref, dil, BF16), _blocks_before(vc_ref, vp_ref, dil, BF16)
            lse_b, dd_b, do_b = _blocks(lse_ref, dil), _blocks(dd_s, dil), _blocks(do_s, dil, BF16)
            p_c = jnp.where(mask_c, jnp.exp(_bdot_nt(q, kc) * ATT_SCALE - lse_b), 0.0)
            p_p = jnp.where(mask_p, jnp.exp(_bdot_nt(q, kp) * ATT_SCALE - lse_b), 0.0)
            ds_c = p_c * (_bdot_nt(do_b, vc) - dd_b) * ATT_SCALE
            ds_p = p_p * (_bdot_nt(do_b, vp) - dd_b) * ATT_SCALE
            dq = _bdot_nn(ds_c.astype(BF16), kc) + _bdot_nn(ds_p.astype(BF16), kp)
            _unblock(dq_s, dq, dil, add=pi > 0)
        dq_ref[...] = dq_s[...].astype(BF16)

    n_super = T // ATT_SUPER
    blk = (ATT_SUPER, D_HEAD_DIM)
    return pl.pallas_call(
        body, name="attn_bwd_dq", grid=(D_HEADS, n_super),
        in_specs=[_att_spec(Q_COL), _att_spec(K_COL), _att_spec(K_COL, -1), _att_spec(V_COL), _att_spec(V_COL, -1), _att_spec(Z_COL),
                  _att_spec(0), _att_spec(0), _att_spec(dy_col)],
        out_specs=[_att_out_spec()] * 2,
        out_shape=[SDS((T, D_HEADS * D_HEAD_DIM), BF16)] * 2,
        scratch_shapes=[pltpu.VMEM(blk, F32)] * 3,
        compiler_params=_cp("parallel", "arbitrary"),
    )(h, h, h, h, h, h, o, lse, dy)


def attn_bwd_dkv(h, o, lse, dy):
    T = h.shape[0]
    n_super = T // ATT_SUPER
    last = n_super - 1
    dy_col = C_WIDTH // LANES
    npat = len(D_PATTERNS)

    def body(k_ref, v_ref, qc_ref, qn_ref, zc_ref, zn_ref, oc_ref, on_ref, lc_ref, ln_ref, dyc_ref, dyn_ref, dk_ref, dv_ref,
             do_c, do_n, dd_c, dd_n, dk_s, dv_s):
        do_s, dd_s = (do_c, do_n), (dd_c, dd_n)
        has_next = pl.program_id(1) < last
        for i, (z_ref, oo_ref, dyd_ref) in enumerate(((zc_ref, oc_ref, dyc_ref), (zn_ref, on_ref, dyn_ref))):
            do = dyd_ref[...] * _silu(z_ref[...])
            do_s[i][...] = do
            dd_s[i][...] = jnp.broadcast_to(jnp.sum(do * oo_ref[...], axis=-1, keepdims=True), (ATT_SUPER, D_HEAD_DIM))
        for pi, (_, dil) in enumerate(D_PATTERNS):
            mask_c, mask_p = _att_masks(dil, has_next, edge_last=True)
            kb, vb = _blocks(k_ref, dil, BF16), _blocks(v_ref, dil, BF16)
            dk = dv = None
            for own in (True, False):
                if own:
                    q, lse_b = _blocks(qc_ref, dil, BF16), _blocks(lc_ref, dil)
                    do_b, dd_b = _blocks(do_c, dil, BF16), _blocks(dd_c, dil)
                else:
                    q, lse_b = _blocks_after(qc_ref, qn_ref, dil, BF16), _blocks_after(lc_ref, ln_ref, dil)
                    do_b, dd_b = _blocks_after(do_c, do_n, dil, BF16), _blocks_after(dd_c, dd_n, dil)
                p = jnp.where(mask_c if own else mask_p, jnp.exp(_bdot_nt(q, kb) * ATT_SCALE - lse_b), 0.0)
                ds = p * (_bdot_nt(do_b, vb) - dd_b) * ATT_SCALE
                dv_t = _bdot_tn(p.astype(BF16), do_b)
                dk_t = _bdot_tn(ds.astype(BF16), q)
                dk = dk_t if dk is None else dk + dk_t
                dv = dv_t if dv is None else dv + dv_t
            _unblock(dk_s, dk, dil, add=pi > 0)
            _unblock(dv_s, dv, dil, add=pi > 0)
        dk_ref[...] = dk_s[...].astype(BF16)
        dv_ref[...] = dv_s[...].astype(BF16)

    blk = (ATT_SUPER, D_HEAD_DIM)

    def pair(col0):
        return [_att_spec(col0), _att_spec(col0, 1, last)]

    return pl.pallas_call(
        body, name="attn_bwd_dkv", grid=(D_HEADS, n_super),
        in_specs=[_att_spec(K_COL), _att_spec(V_COL)] + pair(Q_COL) + pair(Z_COL) + pair(0) + pair(0) + pair(dy_col),
        out_specs=[_att_out_spec()] * 2,
        out_shape=[SDS((T, D_HEADS * D_HEAD_DIM), BF16)] * 2,
        scratch_shapes=[pltpu.VMEM(blk, F32)] * 6,
        compiler_params=_cp("parallel", "arbitrary"),
    )(h, h, h, h, h, h, o, o, lse, lse, dy, dy)


ANY = pl.BlockSpec(memory_space=pl.ANY)
COMM_PARAMS = pltpu.CompilerParams()


def _place():
    x, y, c = lax.axis_index("x"), lax.axis_index("y"), lax.axis_index("c")
    return x, y, c, [(1 - x, y), (x, 1 - y), (1 - x, 1 - y)]


def _rcopy(src, dst, ssem, rsem, dev):
    return pltpu.make_async_remote_copy(src_ref=src, dst_ref=dst, send_sem=ssem, recv_sem=rsem, device_id=dev, device_id_type=MESH)


def gather_rider(arrs, fractions=(0.0, 0.6, 1.0)):
    n = len(arrs)
    per = 7

    def to_chips(ins, outs, ssem, rsem):
        x, y, c, chips = _place()
        return [_rcopy(ins[a].at[c], outs[a].at[2 * x + y, c], ssem.at[per * a + j], rsem.at[per * a + j], (px, py, c))
                for a in range(n) for j, (px, py) in enumerate(chips)]

    def passed_on(outs, ssem, rsem, half):
        x, y, c, chips = _place()
        cps = []
        for a in range(n):
            for j, (px, py) in enumerate(chips):
                slot = outs[a].at[2 * px + py, half(c)]
                cps.append(_rcopy(slot, slot, ssem.at[per * a + 3 + j], rsem.at[per * a + 3 + j], (x, y, 1 - c)))
        return cps

    def own(ins, outs, ssem, rsem):
        x, y, c, _ = _place()
        return [_rcopy(ins[a], outs[a].at[2 * x + y], ssem.at[per * a + 6], rsem.at[per * a + 6], (x, y, 1 - c)) for a in range(n)]

    def start(ins, outs, ssem, rsem):
        for cp in to_chips(ins, outs, ssem, rsem) + own(ins, outs, ssem, rsem):
            cp.start()

    def pass_on(ins, outs, ssem, rsem):
        x, y, c, chips = _place()
        landed = [_rcopy(outs[a].at[2 * px + py, c], outs[a].at[2 * px + py, c], ssem.at[per * a + j], rsem.at[per * a + j], (px, py, c))
                  for a in range(n) for j, (px, py) in enumerate(chips)]
        for arrival, cp in zip(landed, passed_on(outs, ssem, rsem, lambda c: c)):
            arrival.wait_recv()
            cp.start()

    def finish(ins, outs, ssem, rsem):
        for cp in passed_on(outs, ssem, rsem, lambda c: 1 - c):
            cp.wait_recv()
        for cp in to_chips(ins, outs, ssem, rsem) + passed_on(outs, ssem, rsem, lambda c: c):
            cp.wait_send()
        for cp in own(ins, outs, ssem, rsem):
            cp.wait()

    return Rider(arrs, [SDS((N_CHIPS,) + a.shape, a.dtype) for a in arrs], per * n,
                 [(fractions[0], start), (fractions[1], pass_on), (fractions[2], finish)])


def _copies_rider(ins, out_shapes, n_sems, make):
    def start(*refs):
        for cp in make(*refs):
            cp.start()

    def finish(*refs):
        for cp in make(*refs):
            cp.wait()

    return Rider(ins, out_shapes, n_sems, [(0.0, start), (1.0, finish)])


def swap_halves_rider(arrs):
    def make(ins, outs, ssem, rsem):
        x, y, c, _ = _place()
        return [_rcopy(ins[a].at[1 - c], outs[a], ssem.at[a], rsem.at[a], (x, y, 1 - c)) for a in range(len(arrs))]
    return _copies_rider(arrs, [SDS(a.shape[1:], a.dtype) for a in arrs], len(arrs), make)


def scatter_rider(arrs):
    def make(ins, outs, ssem, rsem):
        x, y, c, chips = _place()
        return [_rcopy(ins[a].at[2 * px + py], outs[a].at[j], ssem.at[3 * a + j], rsem.at[3 * a + j], (px, py, c))
                for a in range(len(arrs)) for j, (px, py) in enumerate(chips)]
    return _copies_rider(arrs, [SDS((N_CHIPS - 1,) + a.shape[1:], a.dtype) for a in arrs], 3 * len(arrs), make)


def swap_rider(arrs):
    def make(ins, outs, ssem, rsem):
        x, y, c, _ = _place()
        return [_rcopy(ins[a], outs[a], ssem.at[a], rsem.at[a], (x, y, 1 - c)) for a in range(len(arrs))]
    return _copies_rider(arrs, [SDS(a.shape, a.dtype) for a in arrs], len(arrs), make)


def gather_all(buf):
    def body(in_ref, out_ref, ssem, rsem, lsem):
        x, y, c, _ = _place()
        me = 4 * x + 2 * y + c
        local = pltpu.make_async_copy(in_ref, out_ref.at[me], lsem)
        local.start()
        flips = [(a, b, e) for a in (0, 1) for b in (0, 1) for e in (0, 1)][1:]
        cps = []
        for i, (a, b, e) in enumerate(flips):
            peer = (x ^ a, y ^ b, c ^ e)
            cps.append(_rcopy(in_ref, out_ref.at[me], ssem.at[i], rsem.at[i], peer))
        for cp in cps:
            cp.start()
        for i, (a, b, e) in enumerate(flips):
            cps[i].wait_send()
            slot = out_ref.at[4 * (x ^ a) + 2 * (y ^ b) + (c ^ e)]
            _rcopy(slot, slot, ssem.at[i], rsem.at[i], (x ^ a, y ^ b, c ^ e)).wait_recv()
        local.wait()

    return pl.pallas_call(
        body, name="comm_gather_all", in_specs=[ANY], out_specs=ANY, out_shape=SDS((N_DEV,) + buf.shape, buf.dtype),
        scratch_shapes=[pltpu.SemaphoreType.DMA((N_DEV - 1,)), pltpu.SemaphoreType.DMA((N_DEV - 1,)), pltpu.SemaphoreType.DMA],
        compiler_params=COMM_PARAMS,
    )(buf)


def _pack_offsets(parts):
    offs, r = [], 0
    for p in parts:
        offs.append(r)
        r += -(-p.shape[0] // SUBLANES) * SUBLANES
    return offs, r


def pack_rows(parts):
    offs, total = _pack_offsets(parts)

    def body(*refs):
        out = refs[-1]
        out[...] = jnp.zeros_like(out)
        for ref, off in zip(refs[:-1], offs):
            out[off:off + ref.shape[0], :] = ref[...]

    vmem = pl.BlockSpec(memory_space=pltpu.VMEM)
    return pl.pallas_call(body, name="pack_small", in_specs=[vmem] * len(parts), out_specs=vmem,
                          out_shape=SDS((total, LANES), F32))(*parts)


EVEN_SHARD = IN_EVEN // N_CHIPS


def wie_from_shards(g):
    tr = 256

    def body(g_ref, main_ref, dt_ref):
        full = jnp.concatenate([g_ref[k] for k in range(N_CHIPS)], axis=1)
        main_ref[...] = full[:, :EVEN_MAIN]
        dt_ref[...] = jnp.concatenate([full[:, EVEN_MAIN:], jnp.zeros((tr, LANES - B_HEADS), full.dtype)], axis=1)

    return pl.pallas_call(
        body, name="wie_from_shards", grid=(D_MODEL // tr,),
        in_specs=[pl.BlockSpec((N_CHIPS, tr, EVEN_SHARD), lambda i: (0, i, 0))],
        out_specs=[pl.BlockSpec((tr, EVEN_MAIN), lambda i: (i, 0)), pl.BlockSpec((tr, LANES), lambda i: (i, 0))],
        out_shape=[SDS((D_MODEL, EVEN_MAIN), g.dtype), SDS((D_MODEL, LANES), g.dtype)], compiler_params=_cp("parallel"))(g)


def wie_grad_to_pieces(main, dt):
    tr = 128
    per_half = D_MODEL // 2 // tr

    def body(m_ref, d_ref, o_ref):
        full = jnp.concatenate([m_ref[...], d_ref[:, :B_HEADS]], axis=1)
        for k in range(N_CHIPS):
            o_ref[0, k] = full[:, k * EVEN_SHARD:(k + 1) * EVEN_SHARD]

    return pl.pallas_call(
        body, name="wie_grad_to_pieces", grid=(2, per_half),
        in_specs=[pl.BlockSpec((tr, EVEN_MAIN), lambda c, i: (c * per_half + i, 0)), pl.BlockSpec((tr, LANES), lambda c, i: (c * per_half + i, 0))],
        out_specs=pl.BlockSpec((1, N_CHIPS, tr, EVEN_SHARD), lambda c, i: (c, 0, i, 0)),
        out_shape=SDS((2, N_CHIPS, D_MODEL // 2, EVEN_SHARD), F32), compiler_params=_cp("parallel", "parallel"))(main, dt)


def _unpack(buf, parts):
    offs, _ = _pack_offsets(parts)
    return [buf[off:off + p.shape[0]] for p, off in zip(parts, offs)]


def _pad_lanes(v):
    v = v.reshape(1, -1)
    return jnp.pad(v, ((0, 0), (0, LANES - v.shape[1])))


def _as2d(a):
    return a.reshape(1, -1) if a.ndim == 1 else a.reshape(-1, a.shape[-1])


def kernel(x, even_norm_g, even_w_in, gmlp_ln_g, gmlp_ln_b, gmlp_ws, gmlp_bs, ssd_conv_w, ssd_conv_b, ssd_dt_bias, ssd_a_log, ssd_d, ssd_norm_g, even_w_out, odd_norm_g, odd_w_in, sconv_w, odd_w_out, final_norm_g, loss_target, m_even_norm_g, m_even_w_in, m_gmlp_ln_g, m_gmlp_ln_b, m_gmlp_ws, m_gmlp_bs, m_ssd_conv_w, m_ssd_conv_b, m_ssd_dt_bias, m_ssd_a_log, m_ssd_d, m_ssd_norm_g, m_even_w_out, m_odd_norm_g, m_odd_w_in, m_sconv_w, m_odd_w_out, m_final_norm_g, v_even_norm_g, v_even_w_in, v_gmlp_ln_g, v_gmlp_ln_b, v_gmlp_ws, v_gmlp_bs, v_ssd_conv_w, v_ssd_conv_b, v_ssd_dt_bias, v_ssd_a_log, v_ssd_d, v_ssd_norm_g, v_even_w_out, v_odd_norm_g, v_odd_w_in, v_sconv_w, v_odd_w_out, v_final_norm_g):
    weights = dict(even_norm_g=even_norm_g, even_w_in=even_w_in, gmlp_ln_g=gmlp_ln_g, gmlp_ln_b=gmlp_ln_b, gmlp_ws=gmlp_ws, gmlp_bs=gmlp_bs, ssd_conv_w=ssd_conv_w, ssd_conv_b=ssd_conv_b, ssd_dt_bias=ssd_dt_bias, ssd_a_log=ssd_a_log, ssd_d=ssd_d, ssd_norm_g=ssd_norm_g, even_w_out=even_w_out, odd_norm_g=odd_norm_g, odd_w_in=odd_w_in, sconv_w=sconv_w, odd_w_out=odd_w_out, final_norm_g=final_norm_g)
    moms_m = dict(even_norm_g=m_even_norm_g, even_w_in=m_even_w_in, gmlp_ln_g=m_gmlp_ln_g, gmlp_ln_b=m_gmlp_ln_b, gmlp_ws=m_gmlp_ws, gmlp_bs=m_gmlp_bs, ssd_conv_w=m_ssd_conv_w, ssd_conv_b=m_ssd_conv_b, ssd_dt_bias=m_ssd_dt_bias, ssd_a_log=m_ssd_a_log, ssd_d=m_ssd_d, ssd_norm_g=m_ssd_norm_g, even_w_out=m_even_w_out, odd_norm_g=m_odd_norm_g, odd_w_in=m_odd_w_in, sconv_w=m_sconv_w, odd_w_out=m_odd_w_out, final_norm_g=m_final_norm_g)
    moms_v = dict(even_norm_g=v_even_norm_g, even_w_in=v_even_w_in, gmlp_ln_g=v_gmlp_ln_g, gmlp_ln_b=v_gmlp_ln_b, gmlp_ws=v_gmlp_ws, gmlp_bs=v_gmlp_bs, ssd_conv_w=v_ssd_conv_w, ssd_conv_b=v_ssd_conv_b, ssd_dt_bias=v_ssd_dt_bias, ssd_a_log=v_ssd_a_log, ssd_d=v_ssd_d, ssd_norm_g=v_ssd_norm_g, even_w_out=v_even_w_out, odd_norm_g=v_odd_norm_g, odd_w_in=v_odd_w_in, sconv_w=v_sconv_w, odd_w_out=v_odd_w_out, final_norm_g=v_final_norm_g)
    names = list(weights)

    xs = x[0]
    tgt = loss_target[0]
    T = xs.shape[0]
    chip = 2 * lax.axis_index("x") + lax.axis_index("y")
    core = lax.axis_index("c")
    cshard = B_XBC // N_CHIPS
    dshard = D_MODEL // N_CHIPS

    def halves(w):
        return w.astype(BF16).reshape(2, w.shape[0] // 2, w.shape[1])

    small_shard = jnp.concatenate([ssd_conv_w[0].reshape(-1), odd_norm_g[0], sconv_w[0].reshape(-1)])
    g_wie, g_small = run_rider(gather_rider([halves(even_w_in[0]), small_shard.reshape(2, -1, LANES)]), name="comm_gather_first")
    wie_main, wie_dt = wie_from_shards(g_wie.reshape(N_CHIPS, D_MODEL, EVEN_SHARD))
    g_small = g_small.reshape(N_CHIPS, -1)
    n_cw = B_CONV * cshard
    conv_w = g_small[:, :n_cw].reshape(N_CHIPS, B_CONV, cshard).transpose(1, 0, 2).reshape(B_CONV, B_XBC)
    odd_g = g_small[:, n_cw:n_cw + dshard].reshape(1, D_MODEL)
    sconv = g_small[:, n_cw + dshard:].reshape(N_CHIPS, C_CONV, dshard).transpose(1, 0, 2).reshape(C_CONV, C_WIDTH)

    even_g = even_norm_g
    ln_g, ln_b = gmlp_ln_g, gmlp_ln_b
    ws, bs_t = gmlp_ws[0], gmlp_bs[0].T
    conv_b = ssd_conv_b
    dt_bias, a_log, d_skip = _pad_lanes(ssd_dt_bias), _pad_lanes(ssd_a_log), _pad_lanes(ssd_d)
    norm_g = ssd_norm_g
    fin_g = final_norm_g.reshape(1, D_MODEL)

    xn0 = rmsnorm_fwd(xs, even_g, name="even_norm")
    h0, (g_wio,) = matmul(xn0, wie_main, "nn", name="even_in", rider=gather_rider([halves(odd_w_in[0])], (0.0, 0.88, 1.0)))
    wio = g_wio.reshape(N_CHIPS, D_MODEL, IN_ODD // N_CHIPS)
    dtr = matmul(xn0, wie_dt, "nn", name="even_in_dt", tk=D_MODEL)
    ya = gmlp_fwd(h0, ln_g, ln_b, ws, bs_t)
    xbc = ssd_conv_fwd(h0, conv_w, conv_b)
    (yb, states), (g_woe, g_woo) = ssd_fwd(xbc, dtr, h0, dt_bias, a_log, d_skip, norm_g,
                                           rider=gather_rider([halves(even_w_out[0]), halves(odd_w_out[0])]))
    woe = g_woe.reshape(2 * A_WIDTH, D_MODEL)
    woo = g_woo.reshape(2 * C_WIDTH, D_MODEL)
    y0 = [ya, yb]
    x1 = matmul(y0, woe, "nn", name="even_out", res=xs)

    xn1 = rmsnorm_fwd(x1, odd_g, name="odd_norm")
    h1 = matmul(xn1, wio, "nn", name="odd_in")
    yc = sconv_fwd(h1, sconv)
    yd, att_o, att_lse = attn_fwd(h1)
    y1 = [yc, yd]
    x2 = matmul(y1, woo, "nn", name="odd_out", res=x1)

    loss_part, dx2, dx2b, d_fin_g = loss_head(x2, fin_g, tgt)

    tile = 1024
    rows_layout = ((2, N_CHIPS, ROW_PIECE, D_MODEL), (1, 1, ROW_PIECE, tile), lambda i, j, k: (i % 2, i // 2, 0, j))
    per_chip = IN_ODD // N_CHIPS // tile
    cols_layout = ((2, N_CHIPS, D_MODEL // 2, IN_ODD // N_CHIPS), (1, 1, tile, tile), lambda i, j, k: (i, j // per_chip, 0, j % per_chip))
    dy1 = matmul(dx2b, woo, "nt", name="odd_out_dy")
    d_woo = matmul(y1, dx2b, "tn", name="odd_out_dw", tm=ROW_PIECE, out_layout=rows_layout)
    dbg, dcg, dhx, dzc, d_sconv = sconv_bwd(h1, dy1, sconv)
    dq, dzd = attn_bwd_dq(h1, att_o, att_lse, dy1)
    dk, dv = attn_bwd_dkv(h1, att_o, att_lse, dy1)
    dh1 = jnp.concatenate([dbg, dcg, dhx, dzc, dq, dk, dv, dzd], axis=1)
    dxn1 = matmul(dh1, wio, "nt", name="odd_in_dx")
    d_wio = matmul(xn1, dh1, "tn", name="odd_in_dw", out_layout=cols_layout)
    dx1, dx1b, d_odd_g = rmsnorm_bwd(x1, odd_g, dxn1, dx2, name="odd_norm_bwd")

    d_woe = matmul(y0, dx1b, "tn", name="even_out_dw", tm=ROW_PIECE, out_layout=rows_layout)
    first = [d_wio, d_woo, d_woe]
    dy0, first_sib = matmul(dx1b, woe, "nt", name="even_out_dy", rider=swap_halves_rider(first))
    first_sums = [chip_sum(p, s, core, name=f"chip_sum_first_{i}") for i, (p, s) in enumerate(zip(first, first_sib))]
    duvz, d_ln_g, d_ln_b, d_ws, d_bs_t = gmlp_bwd(h0, dy0, ln_g, ln_b, ws, bs_t)
    (dxbc_act, ddtr, dzb, d_dt_bias, d_a_log, d_d, d_norm_g), first_landed = ssd_bwd(
        xbc, dtr, h0, states, dy0, dt_bias, a_log, d_skip, norm_g, rider=scatter_rider(first_sums))
    first_totals = [total_sum(s, l, chip, name=f"total_first_{i}") for i, (s, l) in enumerate(zip(first_sums, first_landed))]
    (dxbc, d_conv_w, d_conv_b), first_totals_sib = ssd_conv_bwd(h0, dxbc_act, conv_w, conv_b, rider=swap_rider(first_totals))
    dh0 = jnp.concatenate([duvz, dzb, dxbc], axis=1)
    ddtr_b = ddtr
    d_wie_main = matmul(xn0, dh0, "tn", name="even_in_dw")
    d_wie_dt = matmul(xn0, ddtr_b, "tn", name="even_in_dw_dt")
    last = [wie_grad_to_pieces(d_wie_main, d_wie_dt)]
    dxn0, last_sib = matmul(ddtr_b, wie_dt, "nt", name="even_in_dx_dt", rider=swap_halves_rider(last))
    last_sums = [chip_sum(last[0], last_sib[0], core, name="chip_sum_last")]
    dxn0, last_landed = matmul(dh0, wie_main, "nt", name="even_in_dx", res=dxn0, rider=scatter_rider(last_sums))
    last_totals = [total_sum(last_sums[0], last_landed[0], chip, name="total_last")]
    grad_x, _, d_even_g = rmsnorm_bwd(xs, even_g, dxn0, dx1, name="even_norm_bwd")

    small_names = ["even_norm_g", "gmlp_ln_g", "gmlp_ln_b", "gmlp_ws", "gmlp_bs", "ssd_conv_w", "ssd_conv_b", "ssd_dt_bias",
                   "ssd_a_log", "ssd_d", "ssd_norm_g", "odd_norm_g", "sconv_w", "final_norm_g"]
    small_parts = [d_even_g, d_ln_g, d_ln_b, d_ws, d_bs_t.T, d_conv_w, d_conv_b, d_dt_bias, d_a_log, d_d, d_norm_g, d_odd_g, d_sconv, d_fin_g]
    small_shapes = [p.shape for p in small_parts]
    small_rows = [p.reshape(-1, LANES) for p in small_parts]
    small_sum, last_totals_sib = sum_leading(gather_all(pack_rows(small_rows)), name="small_sum", rider=swap_rider(last_totals))
    full = {nm: rows.reshape(shape) for nm, rows, shape in zip(small_names, _unpack(small_sum, small_rows), small_shapes)}
    joined = [jnp.where(core == 0, jnp.stack([t, s]), jnp.stack([s, t]))
              for t, s in zip(last_totals + first_totals, last_totals_sib + first_totals_sib)]
    grads = dict(even_w_in=joined[0].reshape(even_w_in.shape), odd_w_in=joined[1].reshape(odd_w_in.shape),
                 odd_w_out=joined[2].reshape(odd_w_out.shape), even_w_out=joined[3].reshape(even_w_out.shape))
    for nm in small_names:
        g = full[nm]
        if nm in ("ssd_dt_bias", "ssd_a_log", "ssd_d"):
            g = g[:, :B_HEADS]
        elif nm == "ssd_conv_w":
            g = lax.dynamic_slice_in_dim(g, chip * cshard, cshard, axis=1)
        elif nm in ("odd_norm_g", "sconv_w"):
            g = lax.dynamic_slice_in_dim(g, chip * dshard, dshard, axis=1)
        grads[nm] = g.reshape(weights[nm].shape)

    deltas, new_m, new_v = {}, {}, {}
    for nm in names:
        w = weights[nm]
        d, nm_, nv_ = adamw(_as2d(w), _as2d(grads[nm]), _as2d(moms_m[nm]), _as2d(moms_v[nm]), name=f"adamw_{nm}")
        deltas[nm], new_m[nm], new_v[nm] = d.reshape(w.shape), nm_.reshape(w.shape), nv_.reshape(w.shape)

    loss = lax.psum(loss_part[0, 0], ("x", "y", "c"))
    return (loss, grad_x[None], *[grads[n] for n in names], *[deltas[n] for n in names],
            *[new_m[n] for n in names], *[new_v[n] for n in names])
```

```python
import functools

import jax
import jax.numpy as jnp
from jax import lax
from jax.experimental import pallas as pl
from jax.experimental.pallas import tpu as pltpu

F32 = jnp.float32
BF16 = jnp.bfloat16
SDS = jax.ShapeDtypeStruct
MESH = pl.DeviceIdType.MESH

D_MODEL = 2048
A_WIDTH = 2048
A_GROUPS = 8
CHUNK = 128
B_WIDTH = 2048
B_HEADS = 32
B_HEAD_DIM = 64
B_GROUPS = 8
B_STATE = 128
B_CONV = 4
B_XBC = B_WIDTH + 2 * B_GROUPS * B_STATE
C_WIDTH = 2048
C_CONV = 3
D_HEADS = 16
D_HEAD_DIM = 128
D_PATTERNS = ((128, 1), (512, 4), (2048, 16))
ATT_BLOCK = 128
ATT_SUPER = 2048
EVEN_MAIN = 3 * A_WIDTH + B_WIDTH + B_XBC
IN_EVEN = EVEN_MAIN + B_HEADS
IN_ODD = 4 * C_WIDTH + 4 * D_HEADS * D_HEAD_DIM
LANES = 128
SUBLANES = 8
EPS = 1e-5
ADAM_LR = 0.001
ADAM_B1 = 0.9
ADAM_B2 = 0.999
ADAM_EPS = 1e-08
ADAM_WD = 0.01
ADAM_STEP = 10
N_CHIPS = 4
N_DEV = 8
VMEM_LIMIT_BYTES = 56 * 1024 * 1024


def _cp(*sem):
    return pltpu.CompilerParams(dimension_semantics=sem, vmem_limit_bytes=VMEM_LIMIT_BYTES)


def _full(shape):
    return pl.BlockSpec(shape, lambda *_: (0,) * len(shape))


def _silu(x):
    return x * jax.nn.sigmoid(x)


def _dot_nn(a, b):
    return lax.dot_general(a, b, (((1,), (0,)), ((), ())), preferred_element_type=F32)


def _dot_nt(a, b):
    return lax.dot_general(a, b, (((1,), (1,)), ((), ())), preferred_element_type=F32)


def _dot_tn(a, b):
    return lax.dot_general(a, b, (((0,), (0,)), ((), ())), preferred_element_type=F32)


def _tril(n):
    return lax.broadcasted_iota(jnp.int32, (n, n), 0) >= lax.broadcasted_iota(jnp.int32, (n, n), 1)


_DOTS = {"nn": _dot_nn, "nt": _dot_nt, "tn": _dot_tn}


class Rider:
    def __init__(self, ins, out_shapes, n_sems, phases):
        self.ins, self.out_shapes, self.n_sems, self.phases = list(ins), list(out_shapes), n_sems, list(phases)


def _call(body, *, name, grid, in_specs, out_specs, out_shape, scratch_shapes, sem, args, rider=None):
    in_specs, out_specs, out_shape, scratch_shapes = list(in_specs), list(out_specs), list(out_shape), list(scratch_shapes)
    if rider is None:
        res = pl.pallas_call(body, name=name, grid=grid, in_specs=in_specs, out_specs=out_specs, out_shape=out_shape,
                             scratch_shapes=scratch_shapes, compiler_params=_cp(*sem))(*args)
        return list(res), []
    counts = [len(in_specs), len(rider.ins), len(out_specs), len(rider.out_shapes), len(scratch_shapes), 2]
    total = 1
    for g in grid:
        total *= g

    def wrapped(*refs):
        groups, pos = [], 0
        for n in counts:
            groups.append(refs[pos:pos + n])
            pos += n
        ins, rins, outs, routs, scr, (ssem, rsem) = groups
        step = 0
        for d, g in enumerate(grid):
            step = step * g + pl.program_id(d)
        for frac, fn in rider.phases:
            @pl.when(step == min(int(frac * total), total - 1))
            def _(fn=fn):
                fn(rins, routs, ssem, rsem)
        body(*ins, *outs, *scr)

    dma = pltpu.SemaphoreType.DMA((rider.n_sems,))
    res = pl.pallas_call(
        wrapped, name=name, grid=grid, in_specs=in_specs + [ANY] * len(rider.ins), out_specs=out_specs + [ANY] * len(rider.out_shapes),
        out_shape=out_shape + rider.out_shapes, scratch_shapes=scratch_shapes + [dma, dma],
        compiler_params=_cp(*(("arbitrary",) * len(grid))))(*args, *rider.ins)
    return list(res[:len(out_specs)]), list(res[len(out_specs):])


def run_rider(rider, *, name):
    def body(*refs):
        n_in, n_out = len(rider.ins), len(rider.out_shapes)
        ins, outs, (ssem, rsem) = refs[:n_in], refs[n_in:n_in + n_out], refs[n_in + n_out:]
        for _, fn in rider.phases:
            fn(ins, outs, ssem, rsem)

    dma = pltpu.SemaphoreType.DMA((rider.n_sems,))
    return list(pl.pallas_call(body, name=name, in_specs=[ANY] * len(rider.ins), out_specs=[ANY] * len(rider.out_shapes),
                               out_shape=rider.out_shapes, scratch_shapes=[dma, dma])(*rider.ins))


def matmul(a, b, mode, *, name, out_dtype=F32, res=None, tm=1024, tn=1024, tk=2048, out_layout=None, rider=None):
    a_parts = list(a) if isinstance(a, (list, tuple)) else [a]
    b_parts = list(b) if isinstance(b, (list, tuple)) else [b]
    assert len(b_parts) == 1 or mode == "tn"
    b0 = b_parts[0]
    shards = b0.shape[0] if b0.ndim == 3 else 1
    a_rows, a_cols = a_parts[0].shape[0], sum(p.shape[1] for p in a_parts)
    b_rows, b_cols = b0.shape[-2], sum(p.shape[-1] for p in b_parts) * shards
    if mode == "tn":
        (K, M), (K2, N) = (a_rows, a_cols), (b_rows, b_cols)
    elif mode == "nt":
        (M, K), (N, K2) = (a_rows, a_cols), (b_rows, b_cols)
    else:
        (M, K), (K2, N) = (a_rows, a_cols), (b_rows, b_cols)
    assert K == K2, (mode, K, K2)
    tm, tn, tk = min(tm, M), min(tn, N), min(tk, K)
    if shards > 1:
        assert mode != "tn" and b0.shape[-1] % (tk if mode == "nt" else tn) == 0
    assert M % tm == 0 and N % tn == 0 and K % tk == 0, (M, N, K, tm, tn, tk)
    nk = K // tk
    dot = _DOTS[mode]

    def spans(parts, tile):
        out, off = [], 0
        for p in parts:
            assert p.shape[-1] % tile == 0, (p.shape, tile)
            out.append((off, p.shape[-1] // tile))
            off += p.shape[-1] // tile
        return out

    a_axis = 0 if mode == "tn" else 2
    a_spans = spans(a_parts, tm if mode == "tn" else tk)
    b_spans = spans(b_parts, tn) if len(b_parts) > 1 else [(0, N // tn)]

    def inside(t, span):
        return jnp.logical_and(t >= span[0], t < span[0] + span[1])

    def body(*refs):
        a_refs, b_refs, rest = refs[:len(a_parts)], refs[len(a_parts):len(a_parts) + len(b_parts)], refs[len(a_parts) + len(b_parts):]
        r_ref = rest[0] if res is not None else None
        o_ref = rest[1] if res is not None else rest[0]
        k = pl.program_id(2)

        def finish(acc):
            if res is not None:
                acc = acc + r_ref[...]
            o_ref[...] = acc.astype(o_ref.dtype).reshape(o_ref.shape)

        def emit(a_ref, b_ref, conds, k_lo, k_hi):
            def region(*more):
                cs = conds + list(more)
                return pl.when(functools.reduce(jnp.logical_and, cs)) if cs else (lambda f: f())

            def product():
                return dot(a_ref[...], b_ref[0] if shards > 1 else b_ref[...])

            if nk == 1:
                region()(lambda: finish(product()))
                return
            acc_ref = rest[-1]
            if k_lo == 0:
                @region(k == 0)
                def _():
                    acc_ref[...] = product()

            if max(k_lo, 1) < min(k_hi, nk - 1):
                @region(k > 0, k < nk - 1)
                def _():
                    acc_ref[...] += product()

            if k_hi == nk:
                @region(k == nk - 1)
                def _():
                    finish(acc_ref[...] + product())

        for a_ref, a_span in zip(a_refs, a_spans):
            for b_ref, b_span in zip(b_refs, b_spans):
                by_k = a_axis == 2 and len(a_parts) > 1
                emit(a_ref, b_ref, ([inside(pl.program_id(a_axis), a_span)] if len(a_parts) > 1 else []) +
                     ([inside(pl.program_id(1), b_span)] if len(b_parts) > 1 else []),
                     a_span[0] if by_k else 0, a_span[0] + a_span[1] if by_k else nk)

    def piece_index(t, span):
        return jnp.clip(t - span[0], 0, span[1] - 1)

    def a_spec_of(span):
        if len(a_parts) == 1:
            return pl.BlockSpec((tk, tm), lambda i, j, k: (k, i)) if mode == "tn" else pl.BlockSpec((tm, tk), lambda i, j, k: (i, k))
        if mode == "tn":
            return pl.BlockSpec((tk, tm), lambda i, j, k: (jnp.where(inside(i, span), k, 0), piece_index(i, span)))
        return pl.BlockSpec((tm, tk), lambda i, j, k: (i, piece_index(k, span)))

    def b_spec_of(span):
        if shards > 1 and mode == "nn":
            per = b0.shape[-1] // tn
            return pl.BlockSpec((1, tk, tn), lambda i, j, k: (j // per, k, j % per))
        if shards > 1:
            per = b0.shape[-1] // tk
            return pl.BlockSpec((1, tn, tk), lambda i, j, k: (k // per, j, k % per))
        if mode == "nt":
            return pl.BlockSpec((tn, tk), lambda i, j, k: (j, k))
        if len(b_parts) == 1:
            return pl.BlockSpec((tk, tn), lambda i, j, k: (k, j))
        return pl.BlockSpec((tk, tn), lambda i, j, k: (jnp.where(inside(j, span), k, 0), piece_index(j, span)))

    o_spec = pl.BlockSpec((tm, tn), lambda i, j, k: (i, j))
    in_specs = [a_spec_of(s) for s in a_spans] + [b_spec_of(s) for s in b_spans]
    args = a_parts + b_parts
    if res is not None:
        in_specs.append(o_spec)
        args.append(res)
    out_shape = SDS((M, N), out_dtype)
    if out_layout is not None:
        out_shape, o_spec = SDS(out_layout[0], out_dtype), pl.BlockSpec(out_layout[1], out_layout[2])
    outs, rider_outs = _call(
        body, name=name, grid=(M // tm, N // tn, nk), in_specs=in_specs, out_specs=[o_spec], out_shape=[out_shape],
        scratch_shapes=[pltpu.VMEM((tm, tn), F32)] if nk > 1 else [], sem=("parallel", "parallel", "arbitrary"),
        args=args, rider=rider)
    return outs[0] if rider is None else (outs[0], rider_outs)


ROW_TILE = 512
ROW_PIECE = 512


def _rms(x, g):
    return x * lax.rsqrt(jnp.mean(x * x, axis=-1, keepdims=True) + EPS) * g


def rmsnorm_fwd(x, g, *, name):
    T, D = x.shape

    def body(x_ref, g_ref, o_ref):
        o_ref[...] = _rms(x_ref[...], g_ref[...]).astype(BF16)

    row = pl.BlockSpec((ROW_TILE, D), lambda i: (i, 0))
    return pl.pallas_call(body, name=name, grid=(T // ROW_TILE,), in_specs=[row, _full((1, D))], out_specs=row,
                          out_shape=SDS((T, D), BF16), compiler_params=_cp("parallel"))(x, g)


def rmsnorm_bwd(x, g, dxn, dres, *, name, rider=None):
    T, D = x.shape

    def body(x_ref, g_ref, dxn_ref, dres_ref, dx_ref, dxb_ref, dg_ref):
        _, vjp = jax.vjp(_rms, x_ref[...], g_ref[...])
        dx, dg = vjp(dxn_ref[...])
        dx = dx + dres_ref[...]
        dx_ref[...] = dx
        dxb_ref[...] = dx.astype(BF16)

        @pl.when(pl.program_id(0) == 0)
        def _():
            dg_ref[...] = jnp.zeros_like(dg_ref)

        dg_ref[...] += dg

    row = pl.BlockSpec((ROW_TILE, D), lambda i: (i, 0))
    outs, rider_outs = _call(
        body, name=name, grid=(T // ROW_TILE,), in_specs=[row, _full((1, D)), row, row],
        out_specs=[row, row, _full((1, D))], out_shape=[SDS((T, D), F32), SDS((T, D), BF16), SDS((1, D), F32)],
        scratch_shapes=[], sem=("arbitrary",), args=(x, g, dxn, dres), rider=rider)
    return outs if rider is None else (outs, rider_outs)


def _loss_tile(x, g, tgt):
    err = jnp.square(_rms(x, g) - tgt)
    return 0.5 * jnp.sum(jnp.mean(err, axis=-1))


def loss_head(x, g, tgt):
    T, D = x.shape

    def body(x_ref, g_ref, t_ref, loss_ref, dx_ref, dxb_ref, dg_ref):
        loss, vjp = jax.vjp(_loss_tile, x_ref[...], g_ref[...], t_ref[...])
        dx, dg, _ = vjp(jnp.ones((), F32))
        dx_ref[...] = dx
        dxb_ref[...] = dx.astype(BF16)

        @pl.when(pl.program_id(0) == 0)
        def _():
            dg_ref[...] = jnp.zeros_like(dg_ref)
            loss_ref[...] = jnp.zeros_like(loss_ref)

        dg_ref[...] += dg
        loss_ref[...] += jnp.reshape(loss, (1, 1))

    row = pl.BlockSpec((ROW_TILE, D), lambda i: (i, 0))
    return pl.pallas_call(
        body, name="loss_head", grid=(T // ROW_TILE,), in_specs=[row, _full((1, D)), row],
        out_specs=[_full((1, 1)), row, row, _full((1, D))],
        out_shape=[SDS((1, 1), F32), SDS((T, D), F32), SDS((T, D), BF16), SDS((1, D), F32)],
        compiler_params=_cp("arbitrary"))(x, g, tgt)


TILE_BYTES = 1 << 20


def _row_tile(rows, row_bytes):
    for cand in (512, 256, 128, 64, 32, 16, 8):
        if rows % cand == 0 and cand * row_bytes <= TILE_BYTES:
            return cand
    return rows


def adamw(w, g, m, v, *, name):
    R, C = w.shape
    tr = _row_tile(R, C * 4)

    def body(w_ref, g_ref, m_ref, v_ref, d_ref, nm_ref, nv_ref):
        gg = g_ref[...]
        mm = ADAM_B1 * m_ref[...] + (1.0 - ADAM_B1) * gg
        vv = ADAM_B2 * v_ref[...] + (1.0 - ADAM_B2) * jnp.square(gg)
        m_hat = mm / (1.0 - ADAM_B1 ** ADAM_STEP)
        v_hat = vv / (1.0 - ADAM_B2 ** ADAM_STEP)
        d_ref[...] = -ADAM_LR * (m_hat / (jnp.sqrt(v_hat) + ADAM_EPS) + ADAM_WD * w_ref[...])
        nm_ref[...] = mm
        nv_ref[...] = vv

    blk = pl.BlockSpec((tr, C), lambda i: (i, 0))
    return pl.pallas_call(body, name=name, grid=(R // tr,), in_specs=[blk] * 4, out_specs=[blk] * 3,
                          out_shape=[SDS((R, C), F32)] * 3, compiler_params=_cp("parallel"))(w, g, m, v)


def sum_leading(a, *, name, rider=None):
    n, R, C = a.shape
    tr = _row_tile(R, n * C * 4)

    def body(a_ref, o_ref):
        acc = a_ref[0]
        for j in range(1, n):
            acc = acc + a_ref[j]
        o_ref[...] = acc

    outs, rider_outs = _call(body, name=name, grid=(R // tr,), in_specs=[pl.BlockSpec((n, tr, C), lambda i: (0, i, 0))],
                             out_specs=[pl.BlockSpec((tr, C), lambda i: (i, 0))], out_shape=[SDS((R, C), F32)],
                             scratch_shapes=[], sem=("parallel",), args=(a,), rider=rider)
    return outs[0] if rider is None else (outs[0], rider_outs)


def chip_sum(pieces, from_sibling, core, *, name):
    _, n, R, C = pieces.shape

    def body(c_ref, a_ref, b_ref, o_ref):
        o_ref[...] = (a_ref[0] + b_ref[...]).astype(BF16)

    tr = _row_tile(R, C * 4)
    blk = pl.BlockSpec((1, tr, C), lambda k, i, c_ref: (k, i, 0))
    mine = pl.BlockSpec((1, 1, tr, C), lambda k, i, c_ref: (c_ref[0], k, i, 0))
    return pl.pallas_call(
        body, name=name, out_shape=SDS((n, R, C), BF16),
        grid_spec=pltpu.PrefetchScalarGridSpec(num_scalar_prefetch=1, grid=(n, R // tr), in_specs=[mine, blk], out_specs=blk),
        compiler_params=_cp("parallel", "parallel"))(core.reshape(1), pieces, from_sibling)


def total_sum(sums, landed, chip, *, name):
    n, R, C = landed.shape

    def body(k_ref, s_ref, l_ref, o_ref):
        acc = s_ref[0].astype(F32)
        for j in range(n):
            acc = acc + l_ref[j].astype(F32)
        o_ref[...] = acc

    tr = _row_tile(R, n * C * 2)
    return pl.pallas_call(
        body, name=name, out_shape=SDS((R, C), F32),
        grid_spec=pltpu.PrefetchScalarGridSpec(
            num_scalar_prefetch=1, grid=(R // tr,),
            in_specs=[pl.BlockSpec((1, tr, C), lambda i, k_ref: (k_ref[0], i, 0)), pl.BlockSpec((n, tr, C), lambda i, k_ref: (0, i, 0))],
            out_specs=pl.BlockSpec((tr, C), lambda i, k_ref: (i, 0))),
        compiler_params=_cp("parallel"))(chip.reshape(1), sums, landed)


def _gmlp_chunk(u, v, z, ln_g, ln_b, wsc, bs_t):
    mu = jnp.mean(v, axis=-1, keepdims=True)
    xc = v - mu
    vn = xc * lax.rsqrt(jnp.mean(xc * xc, axis=-1, keepdims=True) + EPS) * ln_g + ln_b
    gw = A_WIDTH // A_GROUPS
    outs = []
    for g in range(A_GROUPS):
        m = _dot_nn(wsc[g].astype(BF16), vn[:, g * gw:(g + 1) * gw].astype(BF16))
        outs.append(m + bs_t[:, g:g + 1])
    return _silu(z) * (u * jnp.concatenate(outs, axis=1))


def _h_cols(width, idx, rows=CHUNK):
    return pl.BlockSpec((rows, width), lambda i: (i, idx))


def gmlp_fwd(h, ln_g, ln_b, ws, bs_t):
    T = h.shape[0]

    def body(u_ref, v_ref, z_ref, g_ref, b_ref, ws_ref, bs_ref, o_ref):
        wsc = jnp.where(_tril(CHUNK)[None], ws_ref[...], 0.0)
        o_ref[...] = _gmlp_chunk(u_ref[...], v_ref[...], z_ref[...], g_ref[...], b_ref[...], wsc, bs_ref[...]).astype(BF16)

    return pl.pallas_call(
        body, name="gmlp_fwd", grid=(T // CHUNK,),
        in_specs=[_h_cols(A_WIDTH, 0), _h_cols(A_WIDTH, 1), _h_cols(A_WIDTH, 2), _full((1, A_WIDTH)), _full((1, A_WIDTH)),
                  _full((A_GROUPS, CHUNK, CHUNK)), _full((CHUNK, A_GROUPS))],
        out_specs=_h_cols(A_WIDTH, 0), out_shape=SDS((T, A_WIDTH), BF16), compiler_params=_cp("parallel"),
    )(h, h, h, ln_g, ln_b, ws, bs_t)


def gmlp_bwd(h, dy, ln_g, ln_b, ws, bs_t):
    T = h.shape[0]

    def body(u_ref, v_ref, z_ref, dy_ref, g_ref, b_ref, ws_ref, bs_ref, duvz_ref, dg_ref, db_ref, dws_ref, dbs_ref):
        tri = _tril(CHUNK)[None]
        wsc = jnp.where(tri, ws_ref[...], 0.0)
        _, vjp = jax.vjp(_gmlp_chunk, u_ref[...], v_ref[...], z_ref[...], g_ref[...], b_ref[...], wsc, bs_ref[...])
        du, dv, dz, dg, db, dws, dbs = vjp(dy_ref[...])
        duvz_ref[:, :A_WIDTH] = du.astype(BF16)
        duvz_ref[:, A_WIDTH:2 * A_WIDTH] = dv.astype(BF16)
        duvz_ref[:, 2 * A_WIDTH:] = dz.astype(BF16)

        @pl.when(pl.program_id(0) == 0)
        def _():
            dg_ref[...] = jnp.zeros_like(dg_ref)
            db_ref[...] = jnp.zeros_like(db_ref)
            dws_ref[...] = jnp.zeros_like(dws_ref)
            dbs_ref[...] = jnp.zeros_like(dbs_ref)

        dg_ref[...] += dg
        db_ref[...] += db
        dws_ref[...] += jnp.where(tri, dws, 0.0)
        dbs_ref[...] += dbs

    pshapes = [(1, A_WIDTH), (1, A_WIDTH), (A_GROUPS, CHUNK, CHUNK), (CHUNK, A_GROUPS)]
    return pl.pallas_call(
        body, name="gmlp_bwd", grid=(T // CHUNK,),
        in_specs=[_h_cols(A_WIDTH, 0), _h_cols(A_WIDTH, 1), _h_cols(A_WIDTH, 2), _h_cols(A_WIDTH, 0)] + [_full(s) for s in pshapes],
        out_specs=[_h_cols(3 * A_WIDTH, 0)] + [_full(s) for s in pshapes],
        out_shape=[SDS((T, 3 * A_WIDTH), BF16)] + [SDS(s, F32) for s in pshapes],
        compiler_params=_cp("arbitrary"),
    )(h, h, h, dy, ln_g, ln_b, ws, bs_t)


CONV_ROWS = 256
CONV_COLS = 512


def _row8():
    return lax.broadcasted_iota(jnp.int32, (SUBLANES, 1), 0)


def _delayed(x, halo, s):
    if s == 0:
        return x
    r = pltpu.roll(x, s, axis=0)
    top = jnp.where(_row8() < s, pltpu.roll(halo, s, axis=0), r[:SUBLANES])
    return jnp.concatenate([top, r[SUBLANES:]], axis=0)


def _advanced(d, s):
    if s == 0:
        return d
    rows = d.shape[0]
    r = pltpu.roll(d, rows - s, axis=0)
    bottom = jnp.where(_row8() >= SUBLANES - s, 0.0, r[rows - SUBLANES:])
    return jnp.concatenate([r[:rows - SUBLANES], bottom], axis=0)


def _conv(x, halo, w):
    K = w.shape[0]
    acc = None
    for s in range(K):
        term = w[K - 1 - s:K - s, :] * _delayed(x, halo, s)
        acc = term if acc is None else acc + term
    return acc


def _conv_bwd(x, halo, w, d):
    K = w.shape[0]
    dx, dhalo, dws = None, None, [None] * K
    for s in range(K):
        wk = w[K - 1 - s:K - s, :]
        dws[K - 1 - s] = jnp.sum(d * _delayed(x, halo, s), axis=0, keepdims=True)
        term = wk * _advanced(d, s)
        dx = term if dx is None else dx + term
        if s:
            part = wk * jnp.where(_row8() >= SUBLANES - s, pltpu.roll(d[:SUBLANES], SUBLANES - s, axis=0), 0.0)
            dhalo = part if dhalo is None else dhalo + part
    return dx, dhalo, jnp.concatenate(dws, axis=0)


def _add_to_tail(d, carry):
    return jnp.concatenate([d[:d.shape[0] - SUBLANES], d[d.shape[0] - SUBLANES:] + carry], axis=0)


def _halo_spec(cols, col_idx, nt=None):
    rpb = CONV_ROWS // SUBLANES
    if nt is None:
        return pl.BlockSpec((SUBLANES, cols), lambda c, i: (jnp.maximum(i * rpb - 1, 0), col_idx(c)))
    return pl.BlockSpec((SUBLANES, cols), lambda c, j: (jnp.maximum((nt - 1 - j) * rpb - 1, 0), col_idx(c)))


def ssd_conv_fwd(h, w, b):
    T = h.shape[0]
    nc = B_XBC // CONV_COLS
    base = (3 * A_WIDTH + B_WIDTH) // CONV_COLS

    def body(x_ref, halo_ref, w_ref, b_ref, o_ref):
        halo = jnp.where(pl.program_id(1) > 0, halo_ref[...], 0.0)
        o_ref[...] = _silu(_conv(x_ref[...], halo, w_ref[...]) + b_ref[...])

    return pl.pallas_call(
        body, name="ssd_conv_fwd", grid=(nc, T // CONV_ROWS),
        in_specs=[pl.BlockSpec((CONV_ROWS, CONV_COLS), lambda c, i: (i, base + c)), _halo_spec(CONV_COLS, lambda c: base + c),
                  pl.BlockSpec((B_CONV, CONV_COLS), lambda c, i: (0, c)), pl.BlockSpec((1, CONV_COLS), lambda c, i: (0, c))],
        out_specs=pl.BlockSpec((CONV_ROWS, CONV_COLS), lambda c, i: (i, c)),
        out_shape=SDS((T, B_XBC), F32), compiler_params=_cp("parallel", "parallel"),
    )(h, h, w, b)


def ssd_conv_bwd(h, dy, w, b, rider=None):
    T = h.shape[0]
    nc = B_XBC // CONV_COLS
    nt = T // CONV_ROWS
    base = (3 * A_WIDTH + B_WIDTH) // CONV_COLS

    def body(x_ref, halo_ref, dy_ref, w_ref, b_ref, dx_ref, dw_ref, db_ref, carry_ref):
        j = pl.program_id(1)
        halo = jnp.where(j < nt - 1, halo_ref[...], 0.0)
        x, w = x_ref[...], w_ref[...]
        pre = _conv(x, halo, w) + b_ref[...]
        dpre = dy_ref[...] * _dsilu(pre)
        dx, dhalo, dw = _conv_bwd(x, halo, w, dpre)

        @pl.when(j == 0)
        def _():
            carry_ref[...] = jnp.zeros_like(carry_ref)
            dw_ref[...] = jnp.zeros_like(dw_ref)
            db_ref[...] = jnp.zeros_like(db_ref)

        dx_ref[...] = _add_to_tail(dx, carry_ref[...]).astype(BF16)
        carry_ref[...] = dhalo
        dw_ref[...] += dw
        db_ref[...] += jnp.sum(dpre, axis=0, keepdims=True)

    outs, rider_outs = _call(
        body, name="ssd_conv_bwd", grid=(nc, nt),
        in_specs=[pl.BlockSpec((CONV_ROWS, CONV_COLS), lambda c, j: (nt - 1 - j, base + c)),
                  _halo_spec(CONV_COLS, lambda c: base + c, nt),
                  pl.BlockSpec((CONV_ROWS, CONV_COLS), lambda c, j: (nt - 1 - j, c)),
                  pl.BlockSpec((B_CONV, CONV_COLS), lambda c, j: (0, c)), pl.BlockSpec((1, CONV_COLS), lambda c, j: (0, c))],
        out_specs=[pl.BlockSpec((CONV_ROWS, CONV_COLS), lambda c, j: (nt - 1 - j, c)),
                   pl.BlockSpec((B_CONV, CONV_COLS), lambda c, j: (0, c)), pl.BlockSpec((1, CONV_COLS), lambda c, j: (0, c))],
        out_shape=[SDS((T, B_XBC), BF16), SDS((B_CONV, B_XBC), F32), SDS((1, B_XBC), F32)],
        scratch_shapes=[pltpu.VMEM((SUBLANES, CONV_COLS), F32)], sem=("parallel", "arbitrary"),
        args=(h, h, dy, w, b), rider=rider)
    return outs if rider is None else (outs, rider_outs)


def sconv_fwd(h, w):
    T = h.shape[0]
    nc = C_WIDTH // CONV_COLS

    def col(seg):
        return pl.BlockSpec((CONV_ROWS, CONV_COLS), lambda c, i: (i, seg * nc + c))

    def body(bg_ref, cg_ref, hx_ref, z_ref, cgh_ref, hxh_ref, w_ref, o_ref):
        first = pl.program_id(1) == 0
        cgh = jnp.where(first, 0.0, cgh_ref[...])
        hxh = jnp.where(first, 0.0, hxh_ref[...])
        conv = _conv(cg_ref[...] * hx_ref[...], cgh * hxh, w_ref[...])
        o_ref[...] = (_silu(z_ref[...]) * (bg_ref[...] * conv)).astype(BF16)

    return pl.pallas_call(
        body, name="sconv_fwd", grid=(nc, T // CONV_ROWS),
        in_specs=[col(0), col(1), col(2), col(3), _halo_spec(CONV_COLS, lambda c: nc + c), _halo_spec(CONV_COLS, lambda c: 2 * nc + c),
                  pl.BlockSpec((C_CONV, CONV_COLS), lambda c, i: (0, c))],
        out_specs=pl.BlockSpec((CONV_ROWS, CONV_COLS), lambda c, i: (i, c)),
        out_shape=SDS((T, C_WIDTH), BF16), compiler_params=_cp("parallel", "parallel"),
    )(h, h, h, h, h, h, w)


def sconv_bwd(h, dy, w):
    T = h.shape[0]
    nc = C_WIDTH // CONV_COLS
    nt = T // CONV_ROWS

    def col(seg):
        return pl.BlockSpec((CONV_ROWS, CONV_COLS), lambda c, j: (nt - 1 - j, seg * nc + c))

    def body(bg_ref, cg_ref, hx_ref, z_ref, cgh_ref, hxh_ref, dy_ref, w_ref, dbg_ref, dcg_ref, dhx_ref, dz_ref, dw_ref, carry_ref):
        j = pl.program_id(1)
        first = j == nt - 1
        cgh = jnp.where(first, 0.0, cgh_ref[...])
        hxh = jnp.where(first, 0.0, hxh_ref[...])
        bg, cg, hx, z, w, dy = bg_ref[...], cg_ref[...], hx_ref[...], z_ref[...], w_ref[...], dy_ref[...]
        ch, ch_halo = cg * hx, cgh * hxh
        conv = _conv(ch, ch_halo, w)
        gated = dy * _silu(z)
        dch, dch_halo, dw = _conv_bwd(ch, ch_halo, w, gated * bg)

        @pl.when(j == 0)
        def _():
            carry_ref[...] = jnp.zeros_like(carry_ref)
            dw_ref[...] = jnp.zeros_like(dw_ref)

        dch = _add_to_tail(dch, carry_ref[...])
        dbg_ref[...] = (gated * conv).astype(BF16)
        dz_ref[...] = (dy * bg * conv * _dsilu(z)).astype(BF16)
        dcg_ref[...] = (dch * hx).astype(BF16)
        dhx_ref[...] = (dch * cg).astype(BF16)
        carry_ref[...] = dch_halo
        dw_ref[...] += dw

    out_row = pl.BlockSpec((CONV_ROWS, CONV_COLS), lambda c, j: (nt - 1 - j, c))
    wspec = pl.BlockSpec((C_CONV, CONV_COLS), lambda c, j: (0, c))
    return pl.pallas_call(
        body, name="sconv_bwd", grid=(nc, nt),
        in_specs=[col(0), col(1), col(2), col(3), _halo_spec(CONV_COLS, lambda c: nc + c, nt), _halo_spec(CONV_COLS, lambda c: 2 * nc + c, nt),
                  out_row, wspec],
        out_specs=[out_row] * 4 + [wspec],
        out_shape=[SDS((T, C_WIDTH), BF16)] * 4 + [SDS((C_CONV, C_WIDTH), F32)],
        scratch_shapes=[pltpu.VMEM((SUBLANES, CONV_COLS), F32)],
        compiler_params=_cp("parallel", "arbitrary"),
    )(h, h, h, h, h, h, dy, w)


def _softplus(x):
    return jnp.maximum(x, 0.0) + jnp.log(1.0 + jnp.exp(-jnp.abs(x)))


def _ssd_chunk(xs, bm, cm, dtr, z, prev, dt_bias, a_log, d_skip, norm_g):
    tril = _tril(CHUNK)
    dt = _softplus(dtr + dt_bias)
    adt = dt * (-jnp.exp(a_log))
    a_cs = jnp.dot(tril.astype(F32), adt, precision=lax.Precision.HIGHEST, preferred_element_type=F32)
    a_cs_t = a_cs.T
    a_last = a_cs[CHUNK - 1:CHUNK, :]
    dt_f = _spread_heads(dt, B_HEAD_DIM)
    dec_f = _spread_heads(jnp.exp(a_last - a_cs), B_HEAD_DIM)
    ecs_f = _spread_heads(jnp.exp(a_cs), B_HEAD_DIM)
    dsk_f = _spread_heads(d_skip, B_HEAD_DIM)
    cd_t = jnp.exp(a_cs_t[:, CHUNK - 1:CHUNK])
    xdt = xs * dt_f
    xdd = xdt * dec_f
    colb = _spread_heads(a_cs, CHUNK)
    rowb = jnp.concatenate([jnp.broadcast_to(a_cs_t[hh:hh + 1, :], (CHUNK, CHUNK)) for hh in range(B_HEADS)], axis=1)
    wide = (CHUNK, B_HEADS * CHUNK)
    keep = lax.broadcasted_iota(jnp.int32, wide, 0) >= lax.broadcasted_iota(jnp.int32, wide, 1) % CHUNK
    decay = jnp.exp(jnp.where(keep, colb - rowb, -jnp.inf))
    hpg = B_HEADS // B_GROUPS
    gw = B_WIDTH // B_GROUPS
    low_half = lax.broadcasted_iota(jnp.int32, (CHUNK, 2 * B_HEAD_DIM), 1) < B_HEAD_DIM
    ys, nxt = [], []
    for g in range(B_GROUPS):
        bg = bm[:, g * B_STATE:(g + 1) * B_STATE].astype(BF16)
        cg = cm[:, g * B_STATE:(g + 1) * B_STATE].astype(BF16)
        cb = _dot_nt(cg, bg)
        cbl = (decay[:, g * hpg * CHUNK:(g + 1) * hpg * CHUNK] * jnp.concatenate([cb] * hpg, axis=1)).astype(BF16)
        pg = prev[g * gw:(g + 1) * gw, :]
        y_off = _dot_nt(cg, pg.astype(BF16)) * ecs_f[:, g * gw:(g + 1) * gw]
        st = _dot_tn(xdd[:, g * gw:(g + 1) * gw].astype(BF16), bg)
        cd = jnp.concatenate([jnp.broadcast_to(cd_t[g * hpg + r:g * hpg + r + 1, :], (B_HEAD_DIM, 1)) for r in range(hpg)], axis=0)
        nxt.append(pg * cd + st)
        pairs = []
        for j in range(hpg // 2):
            xp = xdt[:, g * gw + 2 * j * B_HEAD_DIM:g * gw + 2 * (j + 1) * B_HEAD_DIM]
            rhs = jnp.concatenate([jnp.where(low_half, xp, 0.0), jnp.where(low_half, 0.0, xp)], axis=0).astype(BF16)
            pairs.append(_dot_nn(cbl[:, 2 * j * CHUNK:2 * (j + 1) * CHUNK], rhs))
        ys.append(jnp.concatenate(pairs, axis=1) + y_off)
    y = (jnp.concatenate(ys, axis=1) + dsk_f * xs) * _silu(z)
    outs = []
    for g in range(B_GROUPS):
        yg = y[:, g * gw:(g + 1) * gw]
        outs.append(yg * lax.rsqrt(jnp.mean(yg * yg, axis=-1, keepdims=True) + EPS))
    return jnp.concatenate(outs, axis=1) * norm_g, jnp.concatenate(nxt, axis=0)


def _split3(v):
    hi = v.astype(BF16)
    r1 = v - hi.astype(F32)
    mid = r1.astype(BF16)
    return hi, mid, (r1 - mid.astype(F32)).astype(BF16)


def _head_one_hot(width, parts):
    n = B_HEADS * width
    shape = (parts * LANES, n)
    return (lax.broadcasted_iota(jnp.int32, shape, 0) % LANES == lax.broadcasted_iota(jnp.int32, shape, 1) // width).astype(BF16)


@functools.partial(jax.custom_vjp, nondiff_argnums=(1,))
def _spread_heads(v, width):
    return _dot_nn(jnp.concatenate(_split3(v), axis=1), _head_one_hot(width, 3))


def _spread_heads_fwd(v, width):
    return _spread_heads(v, width), None


def _spread_heads_bwd(width, _, g):
    return (_dot_nt(jnp.concatenate(_split3(g), axis=1), jnp.concatenate([_head_one_hot(width, 1)] * 3, axis=1)),)


_spread_heads.defvjp(_spread_heads_fwd, _spread_heads_bwd)


_SSD_PARAM_SHAPES = [(1, LANES), (1, LANES), (1, LANES), (1, B_WIDTH)]
_STATE_SHAPE = (B_WIDTH, B_STATE)


def ssd_fwd(xbc, dtr, h, dt_bias, a_log, d_skip, norm_g, rider=None):
    T = xbc.shape[0]
    nc = T // CHUNK

    def body(xs_ref, b_ref, c_ref, dt_ref, z_ref, p0, p1, p2, p3, y_ref, st_ref, state):
        @pl.when(pl.program_id(0) == 0)
        def _():
            state[...] = jnp.zeros_like(state)

        prev = state[...]
        st_ref[0] = prev
        yb, nxt = _ssd_chunk(xs_ref[...], b_ref[...], c_ref[...], dt_ref[...], z_ref[...], prev, p0[...], p1[...], p2[...], p3[...])
        y_ref[...] = yb.astype(BF16)
        state[...] = nxt

    outs, rider_outs = _call(
        body, name="ssd_fwd", grid=(nc,),
        in_specs=[_h_cols(B_WIDTH, 0), _h_cols(B_GROUPS * B_STATE, 2), _h_cols(B_GROUPS * B_STATE, 3), _h_cols(LANES, 0), _h_cols(B_WIDTH, 3)]
        + [_full(s) for s in _SSD_PARAM_SHAPES],
        out_specs=[_h_cols(B_WIDTH, 0), pl.BlockSpec((1,) + _STATE_SHAPE, lambda i: (i, 0, 0))],
        out_shape=[SDS((T, B_WIDTH), BF16), SDS((nc,) + _STATE_SHAPE, F32)],
        scratch_shapes=[pltpu.VMEM(_STATE_SHAPE, F32)], sem=("arbitrary",),
        args=(xbc, xbc, xbc, dtr, h, dt_bias, a_log, d_skip, norm_g), rider=rider)
    return outs if rider is None else (outs, rider_outs)


def ssd_bwd(xbc, dtr, h, states, dy, dt_bias, a_log, d_skip, norm_g, rider=None):
    T = xbc.shape[0]
    nc = T // CHUNK

    def rev(width, idx):
        return pl.BlockSpec((CHUNK, width), lambda j: (nc - 1 - j, idx))

    def body(xs_ref, b_ref, c_ref, dt_ref, z_ref, st_ref, dy_ref, p0, p1, p2, p3,
             dxbc_ref, ddt_ref, dz_ref, g0, g1, g2, g3, dstate):
        @pl.when(pl.program_id(0) == 0)
        def _():
            dstate[...] = jnp.zeros_like(dstate)
            for gref in (g0, g1, g2, g3):
                gref[...] = jnp.zeros_like(gref)

        _, vjp = jax.vjp(_ssd_chunk, xs_ref[...], b_ref[...], c_ref[...], dt_ref[...], z_ref[...], st_ref[0],
                         p0[...], p1[...], p2[...], p3[...])
        dxs, dbm, dcm, ddt, dz, dprev, d0, d1, d2, d3 = vjp((dy_ref[...], dstate[...]))
        dxbc_ref[:, :B_WIDTH] = dxs
        dxbc_ref[:, B_WIDTH:B_WIDTH + gn] = dbm
        dxbc_ref[:, B_WIDTH + gn:] = dcm
        ddt_ref[...] = ddt.astype(BF16)
        dz_ref[...] = dz.astype(BF16)
        dstate[...] = dprev
        g0[...] += d0
        g1[...] += d1
        g2[...] += d2
        g3[...] += d3

    gn = B_GROUPS * B_STATE
    outs, rider_outs = _call(
        body, name="ssd_bwd", grid=(nc,),
        in_specs=[rev(B_WIDTH, 0), rev(gn, 2), rev(gn, 3), rev(LANES, 0), rev(B_WIDTH, 3),
                  pl.BlockSpec((1,) + _STATE_SHAPE, lambda j: (nc - 1 - j, 0, 0)), rev(B_WIDTH, 1)]
        + [_full(s) for s in _SSD_PARAM_SHAPES],
        out_specs=[rev(B_XBC, 0), rev(LANES, 0), rev(B_WIDTH, 0)] + [_full(s) for s in _SSD_PARAM_SHAPES],
        out_shape=[SDS((T, B_XBC), F32), SDS((T, LANES), BF16), SDS((T, B_WIDTH), BF16)]
        + [SDS(s, F32) for s in _SSD_PARAM_SHAPES],
        scratch_shapes=[pltpu.VMEM(_STATE_SHAPE, F32)], sem=("arbitrary",),
        args=(xbc, xbc, xbc, dtr, h, states, dy, dt_bias, a_log, d_skip, norm_g), rider=rider)
    return outs if rider is None else (outs, rider_outs)


ATT_SCALE = D_HEAD_DIM ** -0.5
Q_COL, K_COL, V_COL, Z_COL = (4 * C_WIDTH // LANES + i * D_HEADS for i in range(4))


ATT_NBLK = ATT_SUPER // ATT_BLOCK


def _res_rows(r, first, count, dil):
    return pl.ds(r + dil * first, count) if dil == 1 else pl.ds(r + dil * first, count, stride=dil)


def _blocks(ref, dil, dtype=None):
    n = ATT_SUPER // dil
    parts = []
    for r in range(dil):
        v = ref[_res_rows(r, 0, n, dil), :]
        parts.append((v if dtype is None else v.astype(dtype)).reshape(n // ATT_BLOCK, ATT_BLOCK, D_HEAD_DIM))
    return parts[0] if dil == 1 else jnp.concatenate(parts, axis=0)


def _blocks_before(cur_ref, prev_ref, dil, dtype):
    n = ATT_SUPER // dil
    parts = []
    for r in range(dil):
        v = prev_ref[_res_rows(r, n - ATT_BLOCK, ATT_BLOCK, dil), :]
        if n > ATT_BLOCK:
            v = jnp.concatenate([v, cur_ref[_res_rows(r, 0, n - ATT_BLOCK, dil), :]], axis=0)
        parts.append(v.astype(dtype).reshape(n // ATT_BLOCK, ATT_BLOCK, D_HEAD_DIM))
    return parts[0] if dil == 1 else jnp.concatenate(parts, axis=0)


def _blocks_after(cur_ref, next_ref, dil, dtype=None):
    n = ATT_SUPER // dil
    parts = []
    for r in range(dil):
        v = next_ref[_res_rows(r, 0, ATT_BLOCK, dil), :]
        if n > ATT_BLOCK:
            v = jnp.concatenate([cur_ref[_res_rows(r, ATT_BLOCK, n - ATT_BLOCK, dil), :], v], axis=0)
        parts.append((v if dtype is None else v.astype(dtype)).reshape(n // ATT_BLOCK, ATT_BLOCK, D_HEAD_DIM))
    return parts[0] if dil == 1 else jnp.concatenate(parts, axis=0)


def _unblock(ref, val, dil, add=False):
    n = ATT_SUPER // dil
    nb = n // ATT_BLOCK
    for r in range(dil):
        v = val[r * nb:(r + 1) * nb].reshape(n, D_HEAD_DIM)
        if add:
            ref[_res_rows(r, 0, n, dil), :] += v
        else:
            ref[_res_rows(r, 0, n, dil), :] = v


def _att_masks(dil, edge_ok, edge_last=False):
    shape = (ATT_NBLK, ATT_BLOCK, ATT_BLOCK)
    blk = lax.broadcasted_iota(jnp.int32, shape, 0)
    row = lax.broadcasted_iota(jnp.int32, shape, 1)
    col = lax.broadcasted_iota(jnp.int32, shape, 2)
    nb = ATT_NBLK // dil
    at_edge = (blk % nb) == (nb - 1 if edge_last else 0)
    return col <= row, jnp.logical_and(col >= row, jnp.logical_or(jnp.logical_not(at_edge), edge_ok))


def _bdot_nt(a, b):
    return lax.dot_general(a, b, (((2,), (2,)), ((0,), (0,))), preferred_element_type=F32)


def _bdot_nn(a, b):
    return lax.dot_general(a, b, (((2,), (1,)), ((0,), (0,))), preferred_element_type=F32)


def _bdot_tn(a, b):
    return lax.dot_general(a, b, (((1,), (1,)), ((0,), (0,))), preferred_element_type=F32)


def _att_spec(col0, shift=0, last=None):
    def imap(hh, n):
        m = n + shift
        if shift < 0:
            m = jnp.maximum(m, 0)
        if shift > 0:
            m = jnp.minimum(m, last)
        return (m, col0 + hh)
    return pl.BlockSpec((ATT_SUPER, D_HEAD_DIM), imap)


def _att_out_spec():
    return pl.BlockSpec((ATT_SUPER, D_HEAD_DIM), lambda hh, n: (n, hh))


def attn_fwd(h):
    T = h.shape[0]
    npat = len(D_PATTERNS)

    def body(q_ref, kc_ref, kp_ref, vc_ref, vp_ref, z_ref, yd_ref, o_ref, lse_ref, *scratch):
        o_s, l_s = scratch[:npat], scratch[npat:]
        has_prev = pl.program_id(1) > 0
        for pi, (_, dil) in enumerate(D_PATTERNS):
            mask_c, mask_p = _att_masks(dil, has_prev)
            q = _blocks(q_ref, dil, BF16)
            kc, vc = _blocks(kc_ref, dil, BF16), _blocks(vc_ref, dil, BF16)
            kp, vp = _blocks_before(kc_ref, kp_ref, dil, BF16), _blocks_before(vc_ref, vp_ref, dil, BF16)
            s_c = jnp.where(mask_c, _bdot_nt(q, kc) * ATT_SCALE, -jnp.inf)
            s_p = jnp.where(mask_p, _bdot_nt(q, kp) * ATT_SCALE, -jnp.inf)
            m = jnp.maximum(jnp.max(s_c, axis=-1, keepdims=True), jnp.max(s_p, axis=-1, keepdims=True))
            p_c = jnp.exp(s_c - m)
            p_p = jnp.exp(s_p - m)
            l = jnp.sum(p_c, axis=-1, keepdims=True) + jnp.sum(p_p, axis=-1, keepdims=True)
            o = _bdot_nn((p_c / l).astype(BF16), vc) + _bdot_nn((p_p / l).astype(BF16), vp)
            _unblock(o_s[pi], o, dil)
            _unblock(l_s[pi], jnp.broadcast_to(m + jnp.log(l), o.shape), dil)
        lses = [l_s[pi][...] for pi in range(npat)]
        mx = functools.reduce(jnp.maximum, lses)
        ws = [jnp.exp(l - mx) for l in lses]
        den = functools.reduce(lambda a, b: a + b, ws)
        o = functools.reduce(lambda a, b: a + b, [(w / den) * o_s[pi][...] for pi, w in enumerate(ws)])
        o_ref[...] = o
        lse_ref[...] = mx + jnp.log(den)
        yd_ref[...] = (_silu(z_ref[...]) * o).astype(BF16)

    n_super = T // ATT_SUPER
    return pl.pallas_call(
        body, name="attn_fwd", grid=(D_HEADS, n_super),
        in_specs=[_att_spec(Q_COL), _att_spec(K_COL), _att_spec(K_COL, -1), _att_spec(V_COL), _att_spec(V_COL, -1), _att_spec(Z_COL)],
        out_specs=[_att_out_spec()] * 3,
        out_shape=[SDS((T, D_HEADS * D_HEAD_DIM), BF16), SDS((T, D_HEADS * D_HEAD_DIM), F32), SDS((T, D_HEADS * D_HEAD_DIM), F32)],
        scratch_shapes=[pltpu.VMEM((ATT_SUPER, D_HEAD_DIM), F32)] * (2 * npat),
        compiler_params=_cp("parallel", "arbitrary"),
    )(h, h, h, h, h, h)


def _dsilu(z):
    s = jax.nn.sigmoid(z)
    return s * (1.0 + z * (1.0 - s))


def attn_bwd_dq(h, o, lse, dy):
    T = h.shape[0]
    dy_col = C_WIDTH // LANES

    def body(q_ref, kc_ref, kp_ref, vc_ref, vp_ref, z_ref, o_ref, lse_ref, dy_ref, dq_ref, dz_ref, do_s, dd_s, dq_s):
        has_prev = pl.program_id(1) > 0
        z, oo, dyd = z_ref[...], o_ref[...], dy_ref[...]
        do = dyd * _silu(z)
        dz_ref[...] = (dyd * oo * _dsilu(z)).astype(BF16)
        do_s[...] = do
        dd_s[...] = jnp.broadcast_to(jnp.sum(do * oo, axis=-1, keepdims=True), (ATT_SUPER, D_HEAD_DIM))
        for pi, (_, dil) in enumerate(D_PATTERNS):
            mask_c, mask_p = _att_masks(dil, has_prev)
            q = _blocks(q_ref, dil, BF16)
            kc, vc = _blocks(kc_ref, dil, BF16), _blocks(vc_ref, dil, BF16)
            kp, vp = _blocks_before(kc_ref, kp_ref, dil, BF16), _blocks_before(vc_ref, vp_ref, dil, BF16)
            lse_b, dd_b, do_b = _blocks(lse_ref, dil), _blocks(dd_s, dil), _blocks(do_s, dil, BF16)
            p_c = jnp.where(mask_c, jnp.exp(_bdot_nt(q, kc) * ATT_SCALE - lse_b), 0.0)
            p_p = jnp.where(mask_p, jnp.exp(_bdot_nt(q, kp) * ATT_SCALE - lse_b), 0.0)
            ds_c = p_c * (_bdot_nt(do_b, vc) - dd_b) * ATT_SCALE
            ds_p = p_p * (_bdot_nt(do_b, vp) - dd_b) * ATT_SCALE
            dq = _bdot_nn(ds_c.astype(BF16), kc) + _bdot_nn(ds_p.astype(BF16), kp)
            _unblock(dq_s, dq, dil, add=pi > 0)
        dq_ref[...] = dq_s[...].astype(BF16)

    n_super = T // ATT_SUPER
    blk = (ATT_SUPER, D_HEAD_DIM)
    return pl.pallas_call(
        body, name="attn_bwd_dq", grid=(D_HEADS, n_super),
        in_specs=[_att_spec(Q_COL), _att_spec(K_COL), _att_spec(K_COL, -1), _att_spec(V_COL), _att_spec(V_COL, -1), _att_spec(Z_COL),
                  _att_spec(0), _att_spec(0), _att_spec(dy_col)],
        out_specs=[_att_out_spec()] * 2,
        out_shape=[SDS((T, D_HEADS * D_HEAD_DIM), BF16)] * 2,
        scratch_shapes=[pltpu.VMEM(blk, F32)] * 3,
        compiler_params=_cp("parallel", "arbitrary"),
    )(h, h, h, h, h, h, o, lse, dy)


def attn_bwd_dkv(h, o, lse, dy):
    T = h.shape[0]
    n_super = T // ATT_SUPER
    last = n_super - 1
    dy_col = C_WIDTH // LANES
    npat = len(D_PATTERNS)

    def body(k_ref, v_ref, qc_ref, qn_ref, zc_ref, zn_ref, oc_ref, on_ref, lc_ref, ln_ref, dyc_ref, dyn_ref, dk_ref, dv_ref,
             do_c, do_n, dd_c, dd_n, dk_s, dv_s):
        do_s, dd_s = (do_c, do_n), (dd_c, dd_n)
        has_next = pl.program_id(1) < last
        for i, (z_ref, oo_ref, dyd_ref) in enumerate(((zc_ref, oc_ref, dyc_ref), (zn_ref, on_ref, dyn_ref))):
            do = dyd_ref[...] * _silu(z_ref[...])
            do_s[i][...] = do
            dd_s[i][...] = jnp.broadcast_to(jnp.sum(do * oo_ref[...], axis=-1, keepdims=True), (ATT_SUPER, D_HEAD_DIM))
        for pi, (_, dil) in enumerate(D_PATTERNS):
            mask_c, mask_p = _att_masks(dil, has_next, edge_last=True)
            kb, vb = _blocks(k_ref, dil, BF16), _blocks(v_ref, dil, BF16)
            dk = dv = None
            for own in (True, False):
                if own:
                    q, lse_b = _blocks(qc_ref, dil, BF16), _blocks(lc_ref, dil)
                    do_b, dd_b = _blocks(do_c, dil, BF16), _blocks(dd_c, dil)
                else:
                    q, lse_b = _blocks_after(qc_ref, qn_ref, dil, BF16), _blocks_after(lc_ref, ln_ref, dil)
                    do_b, dd_b = _blocks_after(do_c, do_n, dil, BF16), _blocks_after(dd_c, dd_n, dil)
                p = jnp.where(mask_c if own else mask_p, jnp.exp(_bdot_nt(q, kb) * ATT_SCALE - lse_b), 0.0)
                ds = p * (_bdot_nt(do_b, vb) - dd_b) * ATT_SCALE
                dv_t = _bdot_tn(p.astype(BF16), do_b)
                dk_t = _bdot_tn(ds.astype(BF16), q)
                dk = dk_t if dk is None else dk + dk_t
                dv = dv_t if dv is None else dv + dv_t
            _unblock(dk_s, dk, dil, add=pi > 0)
            _unblock(dv_s, dv, dil, add=pi > 0)
        dk_ref[...] = dk_s[...].astype(BF16)
        dv_ref[...] = dv_s[...].astype(BF16)

    blk = (ATT_SUPER, D_HEAD_DIM)

    def pair(col0):
        return [_att_spec(col0), _att_spec(col0, 1, last)]

    return pl.pallas_call(
        body, name="attn_bwd_dkv", grid=(D_HEADS, n_super),
        in_specs=[_att_spec(K_COL), _att_spec(V_COL)] + pair(Q_COL) + pair(Z_COL) + pair(0) + pair(0) + pair(dy_col),
        out_specs=[_att_out_spec()] * 2,
        out_shape=[SDS((T, D_HEADS * D_HEAD_DIM), BF16)] * 2,
        scratch_shapes=[pltpu.VMEM(blk, F32)] * 6,
        compiler_params=_cp("parallel", "arbitrary"),
    )(h, h, h, h, h, h, o, o, lse, lse, dy, dy)


def attn_bwd(h, o, lse, dy):
    T = h.shape[0]
    last = T // ATT_SUPER - 1
    dy_col = C_WIDTH // LANES

    def spec(col0, shift=0):
        return pl.BlockSpec((ATT_SUPER, D_HEAD_DIM), lambda hh, j: (jnp.maximum(last - j + shift, 0), col0 + hh))

    def add_before(ref, carry_ref, val, dil):
        n = ATT_SUPER // dil
        nb = n // ATT_BLOCK
        for r in range(dil):
            carry_ref[_res_rows(r, n - ATT_BLOCK, ATT_BLOCK, dil), :] += val[r * nb]
            if nb > 1:
                ref[_res_rows(r, 0, n - ATT_BLOCK, dil), :] += val[r * nb + 1:(r + 1) * nb].reshape(n - ATT_BLOCK, D_HEAD_DIM)

    def body(q_ref, kc_ref, kp_ref, vc_ref, vp_ref, z_ref, o_ref, lse_ref, dy_ref, dq_ref, dk_ref, dv_ref, dz_ref,
             do_s, dd_s, dq_s, dk_s, dv_s, dk_carry, dv_carry):
        j = pl.program_id(1)
        has_prev = j < last

        @pl.when(j == 0)
        def _():
            dk_carry[...] = jnp.zeros_like(dk_carry)
            dv_carry[...] = jnp.zeros_like(dv_carry)

        dk_s[...] = dk_carry[...]
        dv_s[...] = dv_carry[...]
        dk_carry[...] = jnp.zeros_like(dk_carry)
        dv_carry[...] = jnp.zeros_like(dv_carry)
        z, oo, dyd = z_ref[...], o_ref[...], dy_ref[...]
        do = dyd * _silu(z)
        dz_ref[...] = (dyd * oo * _dsilu(z)).astype(BF16)
        do_s[...] = do
        dd_s[...] = jnp.broadcast_to(jnp.sum(do * oo, axis=-1, keepdims=True), (ATT_SUPER, D_HEAD_DIM))
        for pi, (_, dil) in enumerate(D_PATTERNS):
            mask_c, mask_p = _att_masks(dil, has_prev)
            q = _blocks(q_ref, dil, BF16)
            kc, vc = _blocks(kc_ref, dil, BF16), _blocks(vc_ref, dil, BF16)
            kp, vp = _blocks_before(kc_ref, kp_ref, dil, BF16), _blocks_before(vc_ref, vp_ref, dil, BF16)
            lse_b, dd_b, do_b = _blocks(lse_ref, dil), _blocks(dd_s, dil), _blocks(do_s, dil, BF16)
            p_c = jnp.where(mask_c, jnp.exp(_bdot_nt(q, kc) * ATT_SCALE - lse_b), 0.0)
            p_p = jnp.where(mask_p, jnp.exp(_bdot_nt(q, kp) * ATT_SCALE - lse_b), 0.0)
            ds_c = (p_c * (_bdot_nt(do_b, vc) - dd_b) * ATT_SCALE).astype(BF16)
            ds_p = (p_p * (_bdot_nt(do_b, vp) - dd_b) * ATT_SCALE).astype(BF16)
            _unblock(dq_s, _bdot_nn(ds_c, kc) + _bdot_nn(ds_p, kp), dil, add=pi > 0)
            _unblock(dk_s, _bdot_tn(ds_c, q), dil, add=True)
            _unblock(dv_s, _bdot_tn(p_c.astype(BF16), do_b), dil, add=True)
            add_before(dk_s, dk_carry, _bdot_tn(ds_p, q), dil)
            add_before(dv_s, dv_carry, _bdot_tn(p_p.astype(BF16), do_b), dil)
        dq_ref[...] = dq_s[...].astype(BF16)
        dk_ref[...] = dk_s[...].astype(BF16)
        dv_ref[...] = dv_s[...].astype(BF16)

    blk = (ATT_SUPER, D_HEAD_DIM)
    out = pl.BlockSpec(blk, lambda hh, j: (last - j, hh))
    return pl.pallas_call(
        body, name="attn_bwd", grid=(D_HEADS, last + 1),
        in_specs=[spec(Q_COL), spec(K_COL), spec(K_COL, -1), spec(V_COL), spec(V_COL, -1), spec(Z_COL), spec(0), spec(0), spec(dy_col)],
        out_specs=[out] * 4, out_shape=[SDS((T, D_HEADS * D_HEAD_DIM), BF16)] * 4,
        scratch_shapes=[pltpu.VMEM(blk, F32)] * 7, compiler_params=_cp("parallel", "arbitrary"),
    )(h, h, h, h, h, h, o, lse, dy)


ANY = pl.BlockSpec(memory_space=pl.ANY)
COMM_PARAMS = pltpu.CompilerParams()


def _place():
    x, y, c = lax.axis_index("x"), lax.axis_index("y"), lax.axis_index("c")
    return x, y, c, [(1 - x, y), (x, 1 - y), (1 - x, 1 - y)]


def _rcopy(src, dst, ssem, rsem, dev):
    return pltpu.make_async_remote_copy(src_ref=src, dst_ref=dst, send_sem=ssem, recv_sem=rsem, device_id=dev, device_id_type=MESH)


def gather_rider(arrs, fractions=(0.0, 0.6, 1.0)):
    n = len(arrs)
    per = 7

    def to_chips(ins, outs, ssem, rsem):
        x, y, c, chips = _place()
        return [_rcopy(ins[a].at[c], outs[a].at[2 * x + y, c], ssem.at[per * a + j], rsem.at[per * a + j], (px, py, c))
                for a in range(n) for j, (px, py) in enumerate(chips)]

    def passed_on(outs, ssem, rsem, half):
        x, y, c, chips = _place()
        cps = []
        for a in range(n):
            for j, (px, py) in enumerate(chips):
                slot = outs[a].at[2 * px + py, half(c)]
                cps.append(_rcopy(slot, slot, ssem.at[per * a + 3 + j], rsem.at[per * a + 3 + j], (x, y, 1 - c)))
        return cps

    def own(ins, outs, ssem, rsem):
        x, y, c, _ = _place()
        return [_rcopy(ins[a], outs[a].at[2 * x + y], ssem.at[per * a + 6], rsem.at[per * a + 6], (x, y, 1 - c)) for a in range(n)]

    def start(ins, outs, ssem, rsem):
        for cp in to_chips(ins, outs, ssem, rsem) + own(ins, outs, ssem, rsem):
            cp.start()

    def pass_on(ins, outs, ssem, rsem):
        x, y, c, chips = _place()
        landed = [_rcopy(outs[a].at[2 * px + py, c], outs[a].at[2 * px + py, c], ssem.at[per * a + j], rsem.at[per * a + j], (px, py, c))
                  for a in range(n) for j, (px, py) in enumerate(chips)]
        for arrival, cp in zip(landed, passed_on(outs, ssem, rsem, lambda c: c)):
            arrival.wait_recv()
            cp.start()

    def finish(ins, outs, ssem, rsem):
        for cp in passed_on(outs, ssem, rsem, lambda c: 1 - c):
            cp.wait_recv()
        for cp in to_chips(ins, outs, ssem, rsem) + passed_on(outs, ssem, rsem, lambda c: c):
            cp.wait_send()
        for cp in own(ins, outs, ssem, rsem):
            cp.wait()

    return Rider(arrs, [SDS((N_CHIPS,) + a.shape, a.dtype) for a in arrs], per * n,
                 [(fractions[0], start), (fractions[1], pass_on), (fractions[2], finish)])


def _copies_rider(ins, out_shapes, n_sems, make):
    def start(*refs):
        for cp in make(*refs):
            cp.start()

    def finish(*refs):
        for cp in make(*refs):
            cp.wait()

    return Rider(ins, out_shapes, n_sems, [(0.0, start), (1.0, finish)])


def swap_halves_rider(arrs):
    def make(ins, outs, ssem, rsem):
        x, y, c, _ = _place()
        return [_rcopy(ins[a].at[1 - c], outs[a], ssem.at[a], rsem.at[a], (x, y, 1 - c)) for a in range(len(arrs))]
    return _copies_rider(arrs, [SDS(a.shape[1:], a.dtype) for a in arrs], len(arrs), make)


def scatter_rider(arrs):
    def make(ins, outs, ssem, rsem):
        x, y, c, chips = _place()
        return [_rcopy(ins[a].at[2 * px + py], outs[a].at[j], ssem.at[3 * a + j], rsem.at[3 * a + j], (px, py, c))
                for a in range(len(arrs)) for j, (px, py) in enumerate(chips)]
    return _copies_rider(arrs, [SDS((N_CHIPS - 1,) + a.shape[1:], a.dtype) for a in arrs], 3 * len(arrs), make)


def swap_rider(arrs):
    def make(ins, outs, ssem, rsem):
        x, y, c, _ = _place()
        return [_rcopy(ins[a], outs[a], ssem.at[a], rsem.at[a], (x, y, 1 - c)) for a in range(len(arrs))]
    return _copies_rider(arrs, [SDS(a.shape, a.dtype) for a in arrs], len(arrs), make)


def gather_all(buf):
    def body(in_ref, out_ref, ssem, rsem, lsem):
        x, y, c, _ = _place()
        me = 4 * x + 2 * y + c
        local = pltpu.make_async_copy(in_ref, out_ref.at[me], lsem)
        local.start()
        flips = [(a, b, e) for a in (0, 1) for b in (0, 1) for e in (0, 1)][1:]
        cps = []
        for i, (a, b, e) in enumerate(flips):
            peer = (x ^ a, y ^ b, c ^ e)
            cps.append(_rcopy(in_ref, out_ref.at[me], ssem.at[i], rsem.at[i], peer))
        for cp in cps:
            cp.start()
        for i, (a, b, e) in enumerate(flips):
            cps[i].wait_send()
            slot = out_ref.at[4 * (x ^ a) + 2 * (y ^ b) + (c ^ e)]
            _rcopy(slot, slot, ssem.at[i], rsem.at[i], (x ^ a, y ^ b, c ^ e)).wait_recv()
        local.wait()

    return pl.pallas_call(
        body, name="comm_gather_all", in_specs=[ANY], out_specs=ANY, out_shape=SDS((N_DEV,) + buf.shape, buf.dtype),
        scratch_shapes=[pltpu.SemaphoreType.DMA((N_DEV - 1,)), pltpu.SemaphoreType.DMA((N_DEV - 1,)), pltpu.SemaphoreType.DMA],
        compiler_params=COMM_PARAMS,
    )(buf)


def _pack_offsets(parts):
    offs, r = [], 0
    for p in parts:
        offs.append(r)
        r += -(-p.shape[0] // SUBLANES) * SUBLANES
    return offs, r


def pack_rows(parts):
    offs, total = _pack_offsets(parts)

    def body(*refs):
        out = refs[-1]
        out[...] = jnp.zeros_like(out)
        for ref, off in zip(refs[:-1], offs):
            out[off:off + ref.shape[0], :] = ref[...]

    vmem = pl.BlockSpec(memory_space=pltpu.VMEM)
    return pl.pallas_call(body, name="pack_small", in_specs=[vmem] * len(parts), out_specs=vmem,
                          out_shape=SDS((total, LANES), F32))(*parts)


EVEN_SHARD = IN_EVEN // N_CHIPS


def wie_from_shards(g):
    tr = 256

    def body(g_ref, main_ref, dt_ref):
        full = jnp.concatenate([g_ref[k] for k in range(N_CHIPS)], axis=1)
        main_ref[...] = full[:, :EVEN_MAIN]
        dt_ref[...] = jnp.concatenate([full[:, EVEN_MAIN:], jnp.zeros((tr, LANES - B_HEADS), full.dtype)], axis=1)

    return pl.pallas_call(
        body, name="wie_from_shards", grid=(D_MODEL // tr,),
        in_specs=[pl.BlockSpec((N_CHIPS, tr, EVEN_SHARD), lambda i: (0, i, 0))],
        out_specs=[pl.BlockSpec((tr, EVEN_MAIN), lambda i: (i, 0)), pl.BlockSpec((tr, LANES), lambda i: (i, 0))],
        out_shape=[SDS((D_MODEL, EVEN_MAIN), g.dtype), SDS((D_MODEL, LANES), g.dtype)], compiler_params=_cp("parallel"))(g)


def wie_grad_to_pieces(main, dt):
    tr = 128
    per_half = D_MODEL // 2 // tr

    def body(m_ref, d_ref, o_ref):
        full = jnp.concatenate([m_ref[...], d_ref[:, :B_HEADS]], axis=1)
        for k in range(N_CHIPS):
            o_ref[0, k] = full[:, k * EVEN_SHARD:(k + 1) * EVEN_SHARD]

    return pl.pallas_call(
        body, name="wie_grad_to_pieces", grid=(2, per_half),
        in_specs=[pl.BlockSpec((tr, EVEN_MAIN), lambda c, i: (c * per_half + i, 0)), pl.BlockSpec((tr, LANES), lambda c, i: (c * per_half + i, 0))],
        out_specs=pl.BlockSpec((1, N_CHIPS, tr, EVEN_SHARD), lambda c, i: (c, 0, i, 0)),
        out_shape=SDS((2, N_CHIPS, D_MODEL // 2, EVEN_SHARD), F32), compiler_params=_cp("parallel", "parallel"))(main, dt)


def _unpack(buf, parts):
    offs, _ = _pack_offsets(parts)
    return [buf[off:off + p.shape[0]] for p, off in zip(parts, offs)]


def _pad_lanes(v):
    v = v.reshape(1, -1)
    return jnp.pad(v, ((0, 0), (0, LANES - v.shape[1])))


def _as2d(a):
    return a.reshape(1, -1) if a.ndim == 1 else a.reshape(-1, a.shape[-1])


def kernel(x, even_norm_g, even_w_in, gmlp_ln_g, gmlp_ln_b, gmlp_ws, gmlp_bs, ssd_conv_w, ssd_conv_b, ssd_dt_bias, ssd_a_log, ssd_d, ssd_norm_g, even_w_out, odd_norm_g, odd_w_in, sconv_w, odd_w_out, final_norm_g, loss_target, m_even_norm_g, m_even_w_in, m_gmlp_ln_g, m_gmlp_ln_b, m_gmlp_ws, m_gmlp_bs, m_ssd_conv_w, m_ssd_conv_b, m_ssd_dt_bias, m_ssd_a_log, m_ssd_d, m_ssd_norm_g, m_even_w_out, m_odd_norm_g, m_odd_w_in, m_sconv_w, m_odd_w_out, m_final_norm_g, v_even_norm_g, v_even_w_in, v_gmlp_ln_g, v_gmlp_ln_b, v_gmlp_ws, v_gmlp_bs, v_ssd_conv_w, v_ssd_conv_b, v_ssd_dt_bias, v_ssd_a_log, v_ssd_d, v_ssd_norm_g, v_even_w_out, v_odd_norm_g, v_odd_w_in, v_sconv_w, v_odd_w_out, v_final_norm_g):
    weights = dict(even_norm_g=even_norm_g, even_w_in=even_w_in, gmlp_ln_g=gmlp_ln_g, gmlp_ln_b=gmlp_ln_b, gmlp_ws=gmlp_ws, gmlp_bs=gmlp_bs, ssd_conv_w=ssd_conv_w, ssd_conv_b=ssd_conv_b, ssd_dt_bias=ssd_dt_bias, ssd_a_log=ssd_a_log, ssd_d=ssd_d, ssd_norm_g=ssd_norm_g, even_w_out=even_w_out, odd_norm_g=odd_norm_g, odd_w_in=odd_w_in, sconv_w=sconv_w, odd_w_out=odd_w_out, final_norm_g=final_norm_g)
    moms_m = dict(even_norm_g=m_even_norm_g, even_w_in=m_even_w_in, gmlp_ln_g=m_gmlp_ln_g, gmlp_ln_b=m_gmlp_ln_b, gmlp_ws=m_gmlp_ws, gmlp_bs=m_gmlp_bs, ssd_conv_w=m_ssd_conv_w, ssd_conv_b=m_ssd_conv_b, ssd_dt_bias=m_ssd_dt_bias, ssd_a_log=m_ssd_a_log, ssd_d=m_ssd_d, ssd_norm_g=m_ssd_norm_g, even_w_out=m_even_w_out, odd_norm_g=m_odd_norm_g, odd_w_in=m_odd_w_in, sconv_w=m_sconv_w, odd_w_out=m_odd_w_out, final_norm_g=m_final_norm_g)
    moms_v = dict(even_norm_g=v_even_norm_g, even_w_in=v_even_w_in, gmlp_ln_g=v_gmlp_ln_g, gmlp_ln_b=v_gmlp_ln_b, gmlp_ws=v_gmlp_ws, gmlp_bs=v_gmlp_bs, ssd_conv_w=v_ssd_conv_w, ssd_conv_b=v_ssd_conv_b, ssd_dt_bias=v_ssd_dt_bias, ssd_a_log=v_ssd_a_log, ssd_d=v_ssd_d, ssd_norm_g=v_ssd_norm_g, even_w_out=v_even_w_out, odd_norm_g=v_odd_norm_g, odd_w_in=v_odd_w_in, sconv_w=v_sconv_w, odd_w_out=v_odd_w_out, final_norm_g=v_final_norm_g)
    names = list(weights)

    xs = x[0]
    tgt = loss_target[0]
    T = xs.shape[0]
    chip = 2 * lax.axis_index("x") + lax.axis_index("y")
    core = lax.axis_index("c")
    cshard = B_XBC // N_CHIPS
    dshard = D_MODEL // N_CHIPS

    def halves(w):
        return w.astype(BF16).reshape(2, w.shape[0] // 2, w.shape[1])

    small_shard = jnp.concatenate([ssd_conv_w[0].reshape(-1), odd_norm_g[0], sconv_w[0].reshape(-1)])
    g_wie, g_small = run_rider(gather_rider([halves(even_w_in[0]), small_shard.reshape(2, -1, LANES)]), name="comm_gather_first")
    wie_main, wie_dt = wie_from_shards(g_wie.reshape(N_CHIPS, D_MODEL, EVEN_SHARD))
    g_small = g_small.reshape(N_CHIPS, -1)
    n_cw = B_CONV * cshard
    conv_w = g_small[:, :n_cw].reshape(N_CHIPS, B_CONV, cshard).transpose(1, 0, 2).reshape(B_CONV, B_XBC)
    odd_g = g_small[:, n_cw:n_cw + dshard].reshape(1, D_MODEL)
    sconv = g_small[:, n_cw + dshard:].reshape(N_CHIPS, C_CONV, dshard).transpose(1, 0, 2).reshape(C_CONV, C_WIDTH)

    even_g = even_norm_g
    ln_g, ln_b = gmlp_ln_g, gmlp_ln_b
    ws, bs_t = gmlp_ws[0], gmlp_bs[0].T
    conv_b = ssd_conv_b
    dt_bias, a_log, d_skip = _pad_lanes(ssd_dt_bias), _pad_lanes(ssd_a_log), _pad_lanes(ssd_d)
    norm_g = ssd_norm_g
    fin_g = final_norm_g.reshape(1, D_MODEL)

    xn0 = rmsnorm_fwd(xs, even_g, name="even_norm")
    h0, (g_wio,) = matmul(xn0, wie_main, "nn", name="even_in", rider=gather_rider([halves(odd_w_in[0])], (0.0, 0.88, 1.0)))
    wio = g_wio.reshape(N_CHIPS, D_MODEL, IN_ODD // N_CHIPS)
    dtr = matmul(xn0, wie_dt, "nn", name="even_in_dt", tk=D_MODEL)
    ya = gmlp_fwd(h0, ln_g, ln_b, ws, bs_t)
    xbc = ssd_conv_fwd(h0, conv_w, conv_b)
    (yb, states), (g_woe, g_woo) = ssd_fwd(xbc, dtr, h0, dt_bias, a_log, d_skip, norm_g,
                                           rider=gather_rider([halves(even_w_out[0]), halves(odd_w_out[0])]))
    woe = g_woe.reshape(2 * A_WIDTH, D_MODEL)
    woo = g_woo.reshape(2 * C_WIDTH, D_MODEL)
    y0 = [ya, yb]
    x1 = matmul(y0, woe, "nn", name="even_out", res=xs)

    xn1 = rmsnorm_fwd(x1, odd_g, name="odd_norm")
    h1 = matmul(xn1, wio, "nn", name="odd_in")
    yc = sconv_fwd(h1, sconv)
    yd, att_o, att_lse = attn_fwd(h1)
    y1 = [yc, yd]
    x2 = matmul(y1, woo, "nn", name="odd_out", res=x1)

    loss_part, dx2, dx2b, d_fin_g = loss_head(x2, fin_g, tgt)

    tile = 1024
    rows_layout = ((2, N_CHIPS, ROW_PIECE, D_MODEL), (1, 1, ROW_PIECE, tile), lambda i, j, k: (i % 2, i // 2, 0, j))
    per_chip = IN_ODD // N_CHIPS // tile
    cols_layout = ((2, N_CHIPS, D_MODEL // 2, IN_ODD // N_CHIPS), (1, 1, tile, tile), lambda i, j, k: (i, j // per_chip, 0, j % per_chip))
    dy1 = matmul(dx2b, woo, "nt", name="odd_out_dy")
    d_woo = matmul(y1, dx2b, "tn", name="odd_out_dw", tm=ROW_PIECE, out_layout=rows_layout)
    dbg, dcg, dhx, dzc, d_sconv = sconv_bwd(h1, dy1, sconv)
    dq, dk, dv, dzd = attn_bwd(h1, att_o, att_lse, dy1)
    dh1 = jnp.concatenate([dbg, dcg, dhx, dzc, dq, dk, dv, dzd], axis=1)
    dxn1 = matmul(dh1, wio, "nt", name="odd_in_dx")
    d_wio = matmul(xn1, dh1, "tn", name="odd_in_dw", out_layout=cols_layout)
    dx1, dx1b, d_odd_g = rmsnorm_bwd(x1, odd_g, dxn1, dx2, name="odd_norm_bwd")

    d_woe = matmul(y0, dx1b, "tn", name="even_out_dw", tm=ROW_PIECE, out_layout=rows_layout)
    first = [d_wio, d_woo, d_woe]
    dy0, first_sib = matmul(dx1b, woe, "nt", name="even_out_dy", rider=swap_halves_rider(first))
    first_sums = [chip_sum(p, s, core, name=f"chip_sum_first_{i}") for i, (p, s) in enumerate(zip(first, first_sib))]
    duvz, d_ln_g, d_ln_b, d_ws, d_bs_t = gmlp_bwd(h0, dy0, ln_g, ln_b, ws, bs_t)
    (dxbc_act, ddtr, dzb, d_dt_bias, d_a_log, d_d, d_norm_g), first_landed = ssd_bwd(
        xbc, dtr, h0, states, dy0, dt_bias, a_log, d_skip, norm_g, rider=scatter_rider(first_sums))
    first_totals = [total_sum(s, l, chip, name=f"total_first_{i}") for i, (s, l) in enumerate(zip(first_sums, first_landed))]
    (dxbc, d_conv_w, d_conv_b), first_totals_sib = ssd_conv_bwd(h0, dxbc_act, conv_w, conv_b, rider=swap_rider(first_totals))
    dh0 = jnp.concatenate([duvz, dzb, dxbc], axis=1)
    ddtr_b = ddtr
    d_wie_main = matmul(xn0, dh0, "tn", name="even_in_dw")
    d_wie_dt = matmul(xn0, ddtr_b, "tn", name="even_in_dw_dt")
    last = [wie_grad_to_pieces(d_wie_main, d_wie_dt)]
    dxn0, last_sib = matmul(ddtr_b, wie_dt, "nt", name="even_in_dx_dt", rider=swap_halves_rider(last))
    last_sums = [chip_sum(last[0], last_sib[0], core, name="chip_sum_last")]
    dxn0, last_landed = matmul(dh0, wie_main, "nt", name="even_in_dx", res=dxn0, rider=scatter_rider(last_sums))
    last_totals = [total_sum(last_sums[0], last_landed[0], chip, name="total_last")]
    grad_x, _, d_even_g = rmsnorm_bwd(xs, even_g, dxn0, dx1, name="even_norm_bwd")

    small_names = ["even_norm_g", "gmlp_ln_g", "gmlp_ln_b", "gmlp_ws", "gmlp_bs", "ssd_conv_w", "ssd_conv_b", "ssd_dt_bias",
                   "ssd_a_log", "ssd_d", "ssd_norm_g", "odd_norm_g", "sconv_w", "final_norm_g"]
    small_parts = [d_even_g, d_ln_g, d_ln_b, d_ws, d_bs_t.T, d_conv_w, d_conv_b, d_dt_bias, d_a_log, d_d, d_norm_g, d_odd_g, d_sconv, d_fin_g]
    small_shapes = [p.shape for p in small_parts]
    small_rows = [p.reshape(-1, LANES) for p in small_parts]
    small_sum, last_totals_sib = sum_leading(gather_all(pack_rows(small_rows)), name="small_sum", rider=swap_rider(last_totals))
    full = {nm: rows.reshape(shape) for nm, rows, shape in zip(small_names, _unpack(small_sum, small_rows), small_shapes)}
    joined = [jnp.where(core == 0, jnp.stack([t, s]), jnp.stack([s, t]))
              for t, s in zip(last_totals + first_totals, last_totals_sib + first_totals_sib)]
    grads = dict(even_w_in=joined[0].reshape(even_w_in.shape), odd_w_in=joined[1].reshape(odd_w_in.shape),
                 odd_w_out=joined[2].reshape(odd_w_out.shape), even_w_out=joined[3].reshape(even_w_out.shape))
    for nm in small_names:
        g = full[nm]
        if nm in ("ssd_dt_bias", "ssd_a_log", "ssd_d"):
            g = g[:, :B_HEADS]
        elif nm == "ssd_conv_w":
            g = lax.dynamic_slice_in_dim(g, chip * cshard, cshard, axis=1)
        elif nm in ("odd_norm_g", "sconv_w"):
            g = lax.dynamic_slice_in_dim(g, chip * dshard, dshard, axis=1)
        grads[nm] = g.reshape(weights[nm].shape)

    deltas, new_m, new_v = {}, {}, {}
    for nm in names:
        w = weights[nm]
        d, nm_, nv_ = adamw(_as2d(w), _as2d(grads[nm]), _as2d(moms_m[nm]), _as2d(moms_v[nm]), name=f"adamw_{nm}")
        deltas[nm], new_m[nm], new_v[nm] = d.reshape(w.shape), nm_.reshape(w.shape), nv_.reshape(w.shape)

    loss = lax.psum(loss_part[0, 0], ("x", "y", "c"))
    return (loss, grad_x[None], *[grads[n] for n in names], *[deltas[n] for n in names],
            *[new_m[n] for n in names], *[new_v[n] for n in names])
```

```python
import functools

import jax
import jax.numpy as jnp
from jax import lax
from jax.experimental import pallas as pl
from jax.experimental.pallas import tpu as pltpu

F32 = jnp.float32
BF16 = jnp.bfloat16
SDS = jax.ShapeDtypeStruct
MESH = pl.DeviceIdType.MESH

D_MODEL = 2048
A_WIDTH = 2048
A_GROUPS = 8
CHUNK = 128
B_WIDTH = 2048
B_HEADS = 32
B_HEAD_DIM = 64
B_GROUPS = 8
B_STATE = 128
B_CONV = 4
B_XBC = B_WIDTH + 2 * B_GROUPS * B_STATE
C_WIDTH = 2048
C_CONV = 3
D_HEADS = 16
D_HEAD_DIM = 128
D_PATTERNS = ((128, 1), (512, 4), (2048, 16))
ATT_BLOCK = 128
ATT_SUPER = 2048
EVEN_MAIN = 3 * A_WIDTH + B_WIDTH + B_XBC
IN_EVEN = EVEN_MAIN + B_HEADS
IN_ODD = 4 * C_WIDTH + 4 * D_HEADS * D_HEAD_DIM
LANES = 128
SUBLANES = 8
EPS = 1e-5
ADAM_LR = 0.001
ADAM_B1 = 0.9
ADAM_B2 = 0.999
ADAM_EPS = 1e-08
ADAM_WD = 0.01
ADAM_STEP = 10
N_CHIPS = 4
N_DEV = 8
VMEM_LIMIT_BYTES = 56 * 1024 * 1024


def _cp(*sem):
    return pltpu.CompilerParams(dimension_semantics=sem, vmem_limit_bytes=VMEM_LIMIT_BYTES)


def _full(shape):
    return pl.BlockSpec(shape, lambda *_: (0,) * len(shape))


def _silu(x):
    return x * jax.nn.sigmoid(x)


def _dot_nn(a, b):
    return lax.dot_general(a, b, (((1,), (0,)), ((), ())), preferred_element_type=F32)


def _dot_nt(a, b):
    return lax.dot_general(a, b, (((1,), (1,)), ((), ())), preferred_element_type=F32)


def _dot_tn(a, b):
    return lax.dot_general(a, b, (((0,), (0,)), ((), ())), preferred_element_type=F32)


def _tril(n):
    return lax.broadcasted_iota(jnp.int32, (n, n), 0) >= lax.broadcasted_iota(jnp.int32, (n, n), 1)


_DOTS = {"nn": _dot_nn, "nt": _dot_nt, "tn": _dot_tn}


class Rider:
    def __init__(self, ins, out_shapes, n_sems, phases, in_place=False):
        self.ins, self.out_shapes, self.n_sems, self.phases = list(ins), list(out_shapes), n_sems, list(phases)
        self.in_place = in_place


def _call(body, *, name, grid, in_specs, out_specs, out_shape, scratch_shapes, sem, args, rider=None, aliases=None):
    in_specs, out_specs, out_shape, scratch_shapes = list(in_specs), list(out_specs), list(out_shape), list(scratch_shapes)
    aliases = dict(aliases or {})
    if rider is None:
        res = pl.pallas_call(body, name=name, grid=grid, in_specs=in_specs, out_specs=out_specs, out_shape=out_shape,
                             scratch_shapes=scratch_shapes, input_output_aliases=aliases, compiler_params=_cp(*sem))(*args)
        return list(res), []
    if rider.in_place:
        aliases.update({len(in_specs) + k: len(out_specs) + k for k in range(len(rider.ins))})
    counts = [len(in_specs), len(rider.ins), len(out_specs), len(rider.out_shapes), len(scratch_shapes), 2]
    total = 1
    for g in grid:
        total *= g

    def wrapped(*refs):
        groups, pos = [], 0
        for n in counts:
            groups.append(refs[pos:pos + n])
            pos += n
        ins, rins, outs, routs, scr, (ssem, rsem) = groups
        step = 0
        for d, g in enumerate(grid):
            step = step * g + pl.program_id(d)
        for frac, fn in rider.phases:
            @pl.when(step == min(int(frac * total), total - 1))
            def _(fn=fn):
                fn(rins, routs, ssem, rsem)
        body(*ins, *outs, *scr)

    dma = pltpu.SemaphoreType.DMA((rider.n_sems,))
    res = pl.pallas_call(
        wrapped, name=name, grid=grid, in_specs=in_specs + [ANY] * len(rider.ins), out_specs=out_specs + [ANY] * len(rider.out_shapes),
        out_shape=out_shape + rider.out_shapes, scratch_shapes=scratch_shapes + [dma, dma], input_output_aliases=aliases,
        compiler_params=_cp(*(("arbitrary",) * len(grid))))(*args, *rider.ins)
    return list(res[:len(out_specs)]), list(res[len(out_specs):])


def run_rider(rider, *, name):
    def body(*refs):
        n_in, n_out = len(rider.ins), len(rider.out_shapes)
        ins, outs, (ssem, rsem) = refs[:n_in], refs[n_in:n_in + n_out], refs[n_in + n_out:]
        for _, fn in rider.phases:
            fn(ins, outs, ssem, rsem)

    dma = pltpu.SemaphoreType.DMA((rider.n_sems,))
    return list(pl.pallas_call(body, name=name, in_specs=[ANY] * len(rider.ins), out_specs=[ANY] * len(rider.out_shapes),
                               out_shape=rider.out_shapes, scratch_shapes=[dma, dma])(*rider.ins))


def matmul(a, b, mode, *, name, out_dtype=F32, res=None, tm=1024, tn=1024, tk=2048, out_layout=None, rider=None):
    a_parts = list(a) if isinstance(a, (list, tuple)) else [a]
    b_parts = list(b) if isinstance(b, (list, tuple)) else [b]
    assert len(b_parts) == 1 or mode == "tn"
    b0 = b_parts[0]
    shards = b0.shape[0] if b0.ndim == 3 else 1
    a_rows, a_cols = a_parts[0].shape[0], sum(p.shape[1] for p in a_parts)
    b_rows, b_cols = b0.shape[-2], sum(p.shape[-1] for p in b_parts) * shards
    if mode == "tn":
        (K, M), (K2, N) = (a_rows, a_cols), (b_rows, b_cols)
    elif mode == "nt":
        (M, K), (N, K2) = (a_rows, a_cols), (b_rows, b_cols)
    else:
        (M, K), (K2, N) = (a_rows, a_cols), (b_rows, b_cols)
    assert K == K2, (mode, K, K2)
    tm, tn, tk = min(tm, M), min(tn, N), min(tk, K)
    if shards > 1:
        assert mode != "tn" and b0.shape[-1] % (tk if mode == "nt" else tn) == 0
    assert M % tm == 0 and N % tn == 0 and K % tk == 0, (M, N, K, tm, tn, tk)
    nk = K // tk
    dot = _DOTS[mode]

    def spans(parts, tile):
        out, off = [], 0
        for p in parts:
            assert p.shape[-1] % tile == 0, (p.shape, tile)
            out.append((off, p.shape[-1] // tile))
            off += p.shape[-1] // tile
        return out

    a_axis = 0 if mode == "tn" else 2
    a_spans = spans(a_parts, tm if mode == "tn" else tk)
    b_spans = spans(b_parts, tn) if len(b_parts) > 1 else [(0, N // tn)]

    def inside(t, span):
        return jnp.logical_and(t >= span[0], t < span[0] + span[1])

    def body(*refs):
        a_refs, b_refs, rest = refs[:len(a_parts)], refs[len(a_parts):len(a_parts) + len(b_parts)], refs[len(a_parts) + len(b_parts):]
        r_ref = rest[0] if res is not None else None
        o_ref = rest[1] if res is not None else rest[0]
        k = pl.program_id(2)

        def finish(acc):
            if res is not None:
                acc = acc + r_ref[...]
            o_ref[...] = acc.astype(o_ref.dtype).reshape(o_ref.shape)

        def emit(a_ref, b_ref, conds, k_lo, k_hi):
            def region(*more):
                cs = conds + list(more)
                return pl.when(functools.reduce(jnp.logical_and, cs)) if cs else (lambda f: f())

            def product():
                return dot(a_ref[...], b_ref[0] if shards > 1 else b_ref[...])

            if nk == 1:
                region()(lambda: finish(product()))
                return
            acc_ref = rest[-1]
            if k_lo == 0:
                @region(k == 0)
                def _():
                    acc_ref[...] = product()

            if max(k_lo, 1) < min(k_hi, nk - 1):
                @region(k > 0, k < nk - 1)
                def _():
                    acc_ref[...] += product()

            if k_hi == nk:
                @region(k == nk - 1)
                def _():
                    finish(acc_ref[...] + product())

        for a_ref, a_span in zip(a_refs, a_spans):
            for b_ref, b_span in zip(b_refs, b_spans):
                by_k = a_axis == 2 and len(a_parts) > 1
                emit(a_ref, b_ref, ([inside(pl.program_id(a_axis), a_span)] if len(a_parts) > 1 else []) +
                     ([inside(pl.program_id(1), b_span)] if len(b_parts) > 1 else []),
                     a_span[0] if by_k else 0, a_span[0] + a_span[1] if by_k else nk)

    def piece_index(t, span):
        return jnp.clip(t - span[0], 0, span[1] - 1)

    def a_spec_of(span):
        if len(a_parts) == 1:
            return pl.BlockSpec((tk, tm), lambda i, j, k: (k, i)) if mode == "tn" else pl.BlockSpec((tm, tk), lambda i, j, k: (i, k))
        if mode == "tn":
            return pl.BlockSpec((tk, tm), lambda i, j, k: (jnp.where(inside(i, span), k, 0), piece_index(i, span)))
        return pl.BlockSpec((tm, tk), lambda i, j, k: (i, piece_index(k, span)))

    def b_spec_of(span):
        if shards > 1 and mode == "nn":
            per = b0.shape[-1] // tn
            return pl.BlockSpec((1, tk, tn), lambda i, j, k: (j // per, k, j % per))
        if shards > 1:
            per = b0.shape[-1] // tk
            return pl.BlockSpec((1, tn, tk), lambda i, j, k: (k // per, j, k % per))
        if mode == "nt":
            return pl.BlockSpec((tn, tk), lambda i, j, k: (j, k))
        if len(b_parts) == 1:
            return pl.BlockSpec((tk, tn), lambda i, j, k: (k, j))
        return pl.BlockSpec((tk, tn), lambda i, j, k: (jnp.where(inside(j, span), k, 0), piece_index(j, span)))

    o_spec = pl.BlockSpec((tm, tn), lambda i, j, k: (i, j))
    in_specs = [a_spec_of(s) for s in a_spans] + [b_spec_of(s) for s in b_spans]
    args = a_parts + b_parts
    if res is not None:
        in_specs.append(o_spec)
        args.append(res)
    out_shape = SDS((M, N), out_dtype)
    if out_layout is not None:
        out_shape, o_spec = SDS(out_layout[0], out_dtype), pl.BlockSpec(out_layout[1], out_layout[2])
    outs, rider_outs = _call(
        body, name=name, grid=(M // tm, N // tn, nk), in_specs=in_specs, out_specs=[o_spec], out_shape=[out_shape],
        scratch_shapes=[pltpu.VMEM((tm, tn), F32)] if nk > 1 else [], sem=("parallel", "parallel", "arbitrary"),
        args=args, rider=rider)
    return outs[0] if rider is None else (outs[0], rider_outs)


ROW_TILE = 512
ROW_PIECE = 512


def _rms(x, g):
    return x * lax.rsqrt(jnp.mean(x * x, axis=-1, keepdims=True) + EPS) * g


def rmsnorm_fwd(x, g, *, name):
    T, D = x.shape

    def body(x_ref, g_ref, o_ref):
        o_ref[...] = _rms(x_ref[...], g_ref[...]).astype(BF16)

    row = pl.BlockSpec((ROW_TILE, D), lambda i: (i, 0))
    return pl.pallas_call(body, name=name, grid=(T // ROW_TILE,), in_specs=[row, _full((1, D))], out_specs=row,
                          out_shape=SDS((T, D), BF16), compiler_params=_cp("parallel"))(x, g)


def rmsnorm_bwd(x, g, dxn, dres, *, name, rider=None):
    T, D = x.shape

    def body(x_ref, g_ref, dxn_ref, dres_ref, dx_ref, dxb_ref, dg_ref):
        _, vjp = jax.vjp(_rms, x_ref[...], g_ref[...])
        dx, dg = vjp(dxn_ref[...])
        dx = dx + dres_ref[...]
        dx_ref[...] = dx
        dxb_ref[...] = dx.astype(BF16)

        @pl.when(pl.program_id(0) == 0)
        def _():
            dg_ref[...] = jnp.zeros_like(dg_ref)

        dg_ref[...] += dg

    row = pl.BlockSpec((ROW_TILE, D), lambda i: (i, 0))
    outs, rider_outs = _call(
        body, name=name, grid=(T // ROW_TILE,), in_specs=[row, _full((1, D)), row, row],
        out_specs=[row, row, _full((1, D))], out_shape=[SDS((T, D), F32), SDS((T, D), BF16), SDS((1, D), F32)],
        scratch_shapes=[], sem=("arbitrary",), args=(x, g, dxn, dres), rider=rider)
    return outs if rider is None else (outs, rider_outs)


def _loss_tile(x, g, tgt):
    err = jnp.square(_rms(x, g) - tgt)
    return 0.5 * jnp.sum(jnp.mean(err, axis=-1))


def loss_head(x, g, tgt):
    T, D = x.shape

    def body(x_ref, g_ref, t_ref, loss_ref, dx_ref, dxb_ref, dg_ref):
        loss, vjp = jax.vjp(_loss_tile, x_ref[...], g_ref[...], t_ref[...])
        dx, dg, _ = vjp(jnp.ones((), F32))
        dx_ref[...] = dx
        dxb_ref[...] = dx.astype(BF16)

        @pl.when(pl.program_id(0) == 0)
        def _():
            dg_ref[...] = jnp.zeros_like(dg_ref)
            loss_ref[...] = jnp.zeros_like(loss_ref)

        dg_ref[...] += dg
        loss_ref[...] += jnp.reshape(loss, (1, 1))

    row = pl.BlockSpec((ROW_TILE, D), lambda i: (i, 0))
    return pl.pallas_call(
        body, name="loss_head", grid=(T // ROW_TILE,), in_specs=[row, _full((1, D)), row],
        out_specs=[_full((1, 1)), row, row, _full((1, D))],
        out_shape=[SDS((1, 1), F32), SDS((T, D), F32), SDS((T, D), BF16), SDS((1, D), F32)],
        compiler_params=_cp("arbitrary"))(x, g, tgt)


TILE_BYTES = 1 << 20


def _row_tile(rows, row_bytes):
    for cand in (512, 256, 128, 64, 32, 16, 8):
        if rows % cand == 0 and cand * row_bytes <= TILE_BYTES:
            return cand
    return rows


def adamw(w, g, m, v, *, name):
    R, C = w.shape
    tr = _row_tile(R, C * 4)

    def body(w_ref, g_ref, m_ref, v_ref, d_ref, nm_ref, nv_ref):
        gg = g_ref[...]
        mm = ADAM_B1 * m_ref[...] + (1.0 - ADAM_B1) * gg
        vv = ADAM_B2 * v_ref[...] + (1.0 - ADAM_B2) * jnp.square(gg)
        m_hat = mm / (1.0 - ADAM_B1 ** ADAM_STEP)
        v_hat = vv / (1.0 - ADAM_B2 ** ADAM_STEP)
        d_ref[...] = -ADAM_LR * (m_hat / (jnp.sqrt(v_hat) + ADAM_EPS) + ADAM_WD * w_ref[...])
        nm_ref[...] = mm
        nv_ref[...] = vv

    blk = pl.BlockSpec((tr, C), lambda i: (i, 0))
    return pl.pallas_call(body, name=name, grid=(R // tr,), in_specs=[blk] * 4, out_specs=[blk] * 3,
                          out_shape=[SDS((R, C), F32)] * 3, compiler_params=_cp("parallel"))(w, g, m, v)


def sum_leading(a, *, name, rider=None):
    n, R, C = a.shape
    tr = _row_tile(R, n * C * 4)

    def body(a_ref, o_ref):
        acc = a_ref[0]
        for j in range(1, n):
            acc = acc + a_ref[j]
        o_ref[...] = acc

    outs, rider_outs = _call(body, name=name, grid=(R // tr,), in_specs=[pl.BlockSpec((n, tr, C), lambda i: (0, i, 0))],
                             out_specs=[pl.BlockSpec((tr, C), lambda i: (i, 0))], out_shape=[SDS((R, C), F32)],
                             scratch_shapes=[], sem=("parallel",), args=(a,), rider=rider)
    return outs[0] if rider is None else (outs[0], rider_outs)


def chip_sum(pieces, from_sibling, core, *, name):
    _, n, R, C = pieces.shape

    def body(c_ref, a_ref, b_ref, o_ref):
        o_ref[...] = (a_ref[0] + b_ref[...]).astype(BF16)

    tr = _row_tile(R, C * 4)
    blk = pl.BlockSpec((1, tr, C), lambda k, i, c_ref: (k, i, 0))
    mine = pl.BlockSpec((1, 1, tr, C), lambda k, i, c_ref: (c_ref[0], k, i, 0))
    return pl.pallas_call(
        body, name=name, out_shape=SDS((n, R, C), BF16),
        grid_spec=pltpu.PrefetchScalarGridSpec(num_scalar_prefetch=1, grid=(n, R // tr), in_specs=[mine, blk], out_specs=blk),
        compiler_params=_cp("parallel", "parallel"))(core.reshape(1), pieces, from_sibling)


def total_sum(sums, landed, chip, core, *, name):
    n, R, C = landed.shape

    def body(k_ref, c_ref, s_ref, l_ref, o_ref):
        acc = s_ref[0].astype(F32)
        for j in range(n):
            acc = acc + l_ref[j].astype(F32)
        south = c_ref[0] == 0
        o_ref[0] = jnp.where(south, acc, 0.0)
        o_ref[1] = jnp.where(south, 0.0, acc)

    tr = _row_tile(R, n * C * 2)
    return pl.pallas_call(
        body, name=name, out_shape=SDS((2, R, C), F32),
        grid_spec=pltpu.PrefetchScalarGridSpec(
            num_scalar_prefetch=2, grid=(R // tr,),
            in_specs=[pl.BlockSpec((1, tr, C), lambda i, k_ref, c_ref: (k_ref[0], i, 0)),
                      pl.BlockSpec((n, tr, C), lambda i, k_ref, c_ref: (0, i, 0))],
            out_specs=pl.BlockSpec((2, tr, C), lambda i, k_ref, c_ref: (0, i, 0))),
        compiler_params=_cp("parallel"))(chip.reshape(1), core.reshape(1), sums, landed)


def _gmlp_chunk(u, v, z, ln_g, ln_b, wsc, bs_t):
    mu = jnp.mean(v, axis=-1, keepdims=True)
    xc = v - mu
    vn = xc * lax.rsqrt(jnp.mean(xc * xc, axis=-1, keepdims=True) + EPS) * ln_g + ln_b
    gw = A_WIDTH // A_GROUPS
    outs = []
    for g in range(A_GROUPS):
        m = _dot_nn(wsc[g].astype(BF16), vn[:, g * gw:(g + 1) * gw].astype(BF16))
        outs.append(m + bs_t[:, g:g + 1])
    return _silu(z) * (u * jnp.concatenate(outs, axis=1))


def _h_cols(width, idx, rows=CHUNK):
    return pl.BlockSpec((rows, width), lambda i: (i, idx))


def gmlp_fwd(h, ln_g, ln_b, ws, bs_t):
    T = h.shape[0]

    def body(u_ref, v_ref, z_ref, g_ref, b_ref, ws_ref, bs_ref, o_ref):
        wsc = jnp.where(_tril(CHUNK)[None], ws_ref[...], 0.0)
        o_ref[...] = _gmlp_chunk(u_ref[...], v_ref[...], z_ref[...], g_ref[...], b_ref[...], wsc, bs_ref[...]).astype(BF16)

    return pl.pallas_call(
        body, name="gmlp_fwd", grid=(T // CHUNK,),
        in_specs=[_h_cols(A_WIDTH, 0), _h_cols(A_WIDTH, 1), _h_cols(A_WIDTH, 2), _full((1, A_WIDTH)), _full((1, A_WIDTH)),
                  _full((A_GROUPS, CHUNK, CHUNK)), _full((CHUNK, A_GROUPS))],
        out_specs=_h_cols(A_WIDTH, 0), out_shape=SDS((T, A_WIDTH), BF16), compiler_params=_cp("parallel"),
    )(h, h, h, ln_g, ln_b, ws, bs_t)


def gmlp_bwd(h, dy, ln_g, ln_b, ws, bs_t):
    T = h.shape[0]

    def body(u_ref, v_ref, z_ref, dy_ref, g_ref, b_ref, ws_ref, bs_ref, duvz_ref, dg_ref, db_ref, dws_ref, dbs_ref):
        tri = _tril(CHUNK)[None]
        wsc = jnp.where(tri, ws_ref[...], 0.0)
        _, vjp = jax.vjp(_gmlp_chunk, u_ref[...], v_ref[...], z_ref[...], g_ref[...], b_ref[...], wsc, bs_ref[...])
        du, dv, dz, dg, db, dws, dbs = vjp(dy_ref[...])
        duvz_ref[:, :A_WIDTH] = du.astype(BF16)
        duvz_ref[:, A_WIDTH:2 * A_WIDTH] = dv.astype(BF16)
        duvz_ref[:, 2 * A_WIDTH:] = dz.astype(BF16)

        @pl.when(pl.program_id(0) == 0)
        def _():
            dg_ref[...] = jnp.zeros_like(dg_ref)
            db_ref[...] = jnp.zeros_like(db_ref)
            dws_ref[...] = jnp.zeros_like(dws_ref)
            dbs_ref[...] = jnp.zeros_like(dbs_ref)

        dg_ref[...] += dg
        db_ref[...] += db
        dws_ref[...] += jnp.where(tri, dws, 0.0)
        dbs_ref[...] += dbs

    pshapes = [(1, A_WIDTH), (1, A_WIDTH), (A_GROUPS, CHUNK, CHUNK), (CHUNK, A_GROUPS)]
    return pl.pallas_call(
        body, name="gmlp_bwd", grid=(T // CHUNK,),
        in_specs=[_h_cols(A_WIDTH, 0), _h_cols(A_WIDTH, 1), _h_cols(A_WIDTH, 2), _h_cols(A_WIDTH, 0)] + [_full(s) for s in pshapes],
        out_specs=[_h_cols(3 * A_WIDTH, 0)] + [_full(s) for s in pshapes],
        out_shape=[SDS((T, EVEN_MAIN), BF16)] + [SDS(s, F32) for s in pshapes],
        compiler_params=_cp("arbitrary"),
    )(h, h, h, dy, ln_g, ln_b, ws, bs_t)


CONV_ROWS = 256
CONV_COLS = 512


def _row8():
    return lax.broadcasted_iota(jnp.int32, (SUBLANES, 1), 0)


def _delayed(x, halo, s):
    if s == 0:
        return x
    r = pltpu.roll(x, s, axis=0)
    top = jnp.where(_row8() < s, pltpu.roll(halo, s, axis=0), r[:SUBLANES])
    return jnp.concatenate([top, r[SUBLANES:]], axis=0)


def _advanced(d, s):
    if s == 0:
        return d
    rows = d.shape[0]
    r = pltpu.roll(d, rows - s, axis=0)
    bottom = jnp.where(_row8() >= SUBLANES - s, 0.0, r[rows - SUBLANES:])
    return jnp.concatenate([r[:rows - SUBLANES], bottom], axis=0)


def _conv(x, halo, w):
    K = w.shape[0]
    acc = None
    for s in range(K):
        term = w[K - 1 - s:K - s, :] * _delayed(x, halo, s)
        acc = term if acc is None else acc + term
    return acc


def _conv_bwd(x, halo, w, d):
    K = w.shape[0]
    dx, dhalo, dws = None, None, [None] * K
    for s in range(K):
        wk = w[K - 1 - s:K - s, :]
        dws[K - 1 - s] = jnp.sum(d * _delayed(x, halo, s), axis=0, keepdims=True)
        term = wk * _advanced(d, s)
        dx = term if dx is None else dx + term
        if s:
            part = wk * jnp.where(_row8() >= SUBLANES - s, pltpu.roll(d[:SUBLANES], SUBLANES - s, axis=0), 0.0)
            dhalo = part if dhalo is None else dhalo + part
    return dx, dhalo, jnp.concatenate(dws, axis=0)


def _add_to_tail(d, carry):
    return jnp.concatenate([d[:d.shape[0] - SUBLANES], d[d.shape[0] - SUBLANES:] + carry], axis=0)


def _halo_spec(cols, col_idx, nt=None):
    rpb = CONV_ROWS // SUBLANES
    if nt is None:
        return pl.BlockSpec((SUBLANES, cols), lambda c, i: (jnp.maximum(i * rpb - 1, 0), col_idx(c)))
    return pl.BlockSpec((SUBLANES, cols), lambda c, j: (jnp.maximum((nt - 1 - j) * rpb - 1, 0), col_idx(c)))


def ssd_conv_fwd(h, w, b):
    T = h.shape[0]
    nc = B_XBC // CONV_COLS
    base = (3 * A_WIDTH + B_WIDTH) // CONV_COLS

    def body(x_ref, halo_ref, w_ref, b_ref, o_ref):
        halo = jnp.where(pl.program_id(1) > 0, halo_ref[...], 0.0)
        o_ref[...] = _silu(_conv(x_ref[...], halo, w_ref[...]) + b_ref[...])

    return pl.pallas_call(
        body, name="ssd_conv_fwd", grid=(nc, T // CONV_ROWS),
        in_specs=[pl.BlockSpec((CONV_ROWS, CONV_COLS), lambda c, i: (i, base + c)), _halo_spec(CONV_COLS, lambda c: base + c),
                  pl.BlockSpec((B_CONV, CONV_COLS), lambda c, i: (0, c)), pl.BlockSpec((1, CONV_COLS), lambda c, i: (0, c))],
        out_specs=pl.BlockSpec((CONV_ROWS, CONV_COLS), lambda c, i: (i, c)),
        out_shape=SDS((T, B_XBC), F32), compiler_params=_cp("parallel", "parallel"),
    )(h, h, w, b)


def ssd_conv_bwd(h, dy, w, b, dh, rider=None):
    T = h.shape[0]
    nc = B_XBC // CONV_COLS
    nt = T // CONV_ROWS
    base = (3 * A_WIDTH + B_WIDTH) // CONV_COLS

    def body(x_ref, halo_ref, dy_ref, w_ref, b_ref, dh_ref, dx_ref, dw_ref, db_ref, carry_ref):
        j = pl.program_id(1)
        halo = jnp.where(j < nt - 1, halo_ref[...], 0.0)
        x, w = x_ref[...], w_ref[...]
        pre = _conv(x, halo, w) + b_ref[...]
        dpre = dy_ref[...] * _dsilu(pre)
        dx, dhalo, dw = _conv_bwd(x, halo, w, dpre)

        @pl.when(j == 0)
        def _():
            carry_ref[...] = jnp.zeros_like(carry_ref)
            dw_ref[...] = jnp.zeros_like(dw_ref)
            db_ref[...] = jnp.zeros_like(db_ref)

        dx_ref[...] = _add_to_tail(dx, carry_ref[...]).astype(BF16)
        carry_ref[...] = dhalo
        dw_ref[...] += dw
        db_ref[...] += jnp.sum(dpre, axis=0, keepdims=True)

    outs, rider_outs = _call(
        body, name="ssd_conv_bwd", grid=(nc, nt),
        in_specs=[pl.BlockSpec((CONV_ROWS, CONV_COLS), lambda c, j: (nt - 1 - j, base + c)),
                  _halo_spec(CONV_COLS, lambda c: base + c, nt),
                  pl.BlockSpec((CONV_ROWS, CONV_COLS), lambda c, j: (nt - 1 - j, c)),
                  pl.BlockSpec((B_CONV, CONV_COLS), lambda c, j: (0, c)), pl.BlockSpec((1, CONV_COLS), lambda c, j: (0, c)), ANY],
        out_specs=[pl.BlockSpec((CONV_ROWS, CONV_COLS), lambda c, j: (nt - 1 - j, base + c)),
                   pl.BlockSpec((B_CONV, CONV_COLS), lambda c, j: (0, c)), pl.BlockSpec((1, CONV_COLS), lambda c, j: (0, c))],
        out_shape=[SDS(dh.shape, dh.dtype), SDS((B_CONV, B_XBC), F32), SDS((1, B_XBC), F32)],
        scratch_shapes=[pltpu.VMEM((SUBLANES, CONV_COLS), F32)], sem=("parallel", "arbitrary"),
        args=(h, h, dy, w, b, dh), rider=rider, aliases={5: 0})
    return outs if rider is None else (outs, rider_outs)


def sconv_fwd(h, w):
    T = h.shape[0]
    nc = C_WIDTH // CONV_COLS

    def col(seg):
        return pl.BlockSpec((CONV_ROWS, CONV_COLS), lambda c, i: (i, seg * nc + c))

    def body(bg_ref, cg_ref, hx_ref, z_ref, cgh_ref, hxh_ref, w_ref, o_ref):
        first = pl.program_id(1) == 0
        cgh = jnp.where(first, 0.0, cgh_ref[...])
        hxh = jnp.where(first, 0.0, hxh_ref[...])
        conv = _conv(cg_ref[...] * hx_ref[...], cgh * hxh, w_ref[...])
        o_ref[...] = (_silu(z_ref[...]) * (bg_ref[...] * conv)).astype(BF16)

    return pl.pallas_call(
        body, name="sconv_fwd", grid=(nc, T // CONV_ROWS),
        in_specs=[col(0), col(1), col(2), col(3), _halo_spec(CONV_COLS, lambda c: nc + c), _halo_spec(CONV_COLS, lambda c: 2 * nc + c),
                  pl.BlockSpec((C_CONV, CONV_COLS), lambda c, i: (0, c))],
        out_specs=pl.BlockSpec((CONV_ROWS, CONV_COLS), lambda c, i: (i, c)),
        out_shape=SDS((T, C_WIDTH), BF16), compiler_params=_cp("parallel", "parallel"),
    )(h, h, h, h, h, h, w)


def sconv_bwd(h, dy, w):
    T = h.shape[0]
    nc = C_WIDTH // CONV_COLS
    nt = T // CONV_ROWS

    def col(seg):
        return pl.BlockSpec((CONV_ROWS, CONV_COLS), lambda c, j: (nt - 1 - j, seg * nc + c))

    def body(bg_ref, cg_ref, hx_ref, z_ref, cgh_ref, hxh_ref, dy_ref, w_ref, dbg_ref, dcg_ref, dhx_ref, dz_ref, dw_ref, carry_ref):
        j = pl.program_id(1)
        first = j == nt - 1
        cgh = jnp.where(first, 0.0, cgh_ref[...])
        hxh = jnp.where(first, 0.0, hxh_ref[...])
        bg, cg, hx, z, w, dy = bg_ref[...], cg_ref[...], hx_ref[...], z_ref[...], w_ref[...], dy_ref[...]
        ch, ch_halo = cg * hx, cgh * hxh
        conv = _conv(ch, ch_halo, w)
        gated = dy * _silu(z)
        dch, dch_halo, dw = _conv_bwd(ch, ch_halo, w, gated * bg)

        @pl.when(j == 0)
        def _():
            carry_ref[...] = jnp.zeros_like(carry_ref)
            dw_ref[...] = jnp.zeros_like(dw_ref)

        dch = _add_to_tail(dch, carry_ref[...])
        dbg_ref[...] = (gated * conv).astype(BF16)
        dz_ref[...] = (dy * bg * conv * _dsilu(z)).astype(BF16)
        dcg_ref[...] = (dch * hx).astype(BF16)
        dhx_ref[...] = (dch * cg).astype(BF16)
        carry_ref[...] = dch_halo
        dw_ref[...] += dw

    out_row = pl.BlockSpec((CONV_ROWS, CONV_COLS), lambda c, j: (nt - 1 - j, c))
    wspec = pl.BlockSpec((C_CONV, CONV_COLS), lambda c, j: (0, c))
    return pl.pallas_call(
        body, name="sconv_bwd", grid=(nc, nt),
        in_specs=[col(0), col(1), col(2), col(3), _halo_spec(CONV_COLS, lambda c: nc + c, nt), _halo_spec(CONV_COLS, lambda c: 2 * nc + c, nt),
                  out_row, wspec],
        out_specs=[out_row] * 4 + [wspec],
        out_shape=[SDS((T, C_WIDTH), BF16)] * 4 + [SDS((C_CONV, C_WIDTH), F32)],
        scratch_shapes=[pltpu.VMEM((SUBLANES, CONV_COLS), F32)],
        compiler_params=_cp("parallel", "arbitrary"),
    )(h, h, h, h, h, h, dy, w)


def _softplus(x):
    return jnp.maximum(x, 0.0) + jnp.log(1.0 + jnp.exp(-jnp.abs(x)))


def _ssd_chunk(xs, bm, cm, dtr, z, prev, dt_bias, a_log, d_skip, norm_g):
    tril = _tril(CHUNK)
    dt = _softplus(dtr + dt_bias)
    adt = dt * (-jnp.exp(a_log))
    a_cs = jnp.dot(tril.astype(F32), adt, precision=lax.Precision.HIGHEST, preferred_element_type=F32)
    a_cs_t = a_cs.T
    a_last = a_cs[CHUNK - 1:CHUNK, :]
    dt_f = _spread_heads(dt, B_HEAD_DIM)
    dec_f = _spread_heads(jnp.exp(a_last - a_cs), B_HEAD_DIM)
    ecs_f = _spread_heads(jnp.exp(a_cs), B_HEAD_DIM)
    dsk_f = _spread_heads(d_skip, B_HEAD_DIM)
    cd_t = jnp.exp(a_cs_t[:, CHUNK - 1:CHUNK])
    xdt = xs * dt_f
    xdd = xdt * dec_f
    colb = _spread_heads(a_cs, CHUNK)
    rowb = jnp.concatenate([jnp.broadcast_to(a_cs_t[hh:hh + 1, :], (CHUNK, CHUNK)) for hh in range(B_HEADS)], axis=1)
    wide = (CHUNK, B_HEADS * CHUNK)
    keep = lax.broadcasted_iota(jnp.int32, wide, 0) >= lax.broadcasted_iota(jnp.int32, wide, 1) % CHUNK
    decay = jnp.exp(jnp.where(keep, colb - rowb, -jnp.inf))
    hpg = B_HEADS // B_GROUPS
    gw = B_WIDTH // B_GROUPS
    low_half = lax.broadcasted_iota(jnp.int32, (CHUNK, 2 * B_HEAD_DIM), 1) < B_HEAD_DIM
    ys, nxt = [], []
    for g in range(B_GROUPS):
        bg = bm[:, g * B_STATE:(g + 1) * B_STATE].astype(BF16)
        cg = cm[:, g * B_STATE:(g + 1) * B_STATE].astype(BF16)
        cb = _dot_nt(cg, bg)
        cbl = (decay[:, g * hpg * CHUNK:(g + 1) * hpg * CHUNK] * jnp.concatenate([cb] * hpg, axis=1)).astype(BF16)
        pg = prev[g * gw:(g + 1) * gw, :]
        y_off = _dot_nt(cg, pg.astype(BF16)) * ecs_f[:, g * gw:(g + 1) * gw]
        st = _dot_tn(xdd[:, g * gw:(g + 1) * gw].astype(BF16), bg)
        cd = jnp.concatenate([jnp.broadcast_to(cd_t[g * hpg + r:g * hpg + r + 1, :], (B_HEAD_DIM, 1)) for r in range(hpg)], axis=0)
        nxt.append(pg * cd + st)
        pairs = []
        for j in range(hpg // 2):
            xp = xdt[:, g * gw + 2 * j * B_HEAD_DIM:g * gw + 2 * (j + 1) * B_HEAD_DIM]
            rhs = jnp.concatenate([jnp.where(low_half, xp, 0.0), jnp.where(low_half, 0.0, xp)], axis=0).astype(BF16)
            pairs.append(_dot_nn(cbl[:, 2 * j * CHUNK:2 * (j + 1) * CHUNK], rhs))
        ys.append(jnp.concatenate(pairs, axis=1) + y_off)
    y = (jnp.concatenate(ys, axis=1) + dsk_f * xs) * _silu(z)
    outs = []
    for g in range(B_GROUPS):
        yg = y[:, g * gw:(g + 1) * gw]
        outs.append(yg * lax.rsqrt(jnp.mean(yg * yg, axis=-1, keepdims=True) + EPS))
    return jnp.concatenate(outs, axis=1) * norm_g, jnp.concatenate(nxt, axis=0)


def _split3(v):
    hi = v.astype(BF16)
    r1 = v - hi.astype(F32)
    mid = r1.astype(BF16)
    return hi, mid, (r1 - mid.astype(F32)).astype(BF16)


def _head_one_hot(width, parts):
    n = B_HEADS * width
    shape = (parts * LANES, n)
    return (lax.broadcasted_iota(jnp.int32, shape, 0) % LANES == lax.broadcasted_iota(jnp.int32, shape, 1) // width).astype(BF16)


@functools.partial(jax.custom_vjp, nondiff_argnums=(1,))
def _spread_heads(v, width):
    return _dot_nn(jnp.concatenate(_split3(v), axis=1), _head_one_hot(width, 3))


def _spread_heads_fwd(v, width):
    return _spread_heads(v, width), None


def _spread_heads_bwd(width, _, g):
    return (_dot_nt(jnp.concatenate(_split3(g), axis=1), jnp.concatenate([_head_one_hot(width, 1)] * 3, axis=1)),)


_spread_heads.defvjp(_spread_heads_fwd, _spread_heads_bwd)


_SSD_PARAM_SHAPES = [(1, LANES), (1, LANES), (1, LANES), (1, B_WIDTH)]
_STATE_SHAPE = (B_WIDTH, B_STATE)


def ssd_fwd(xbc, dtr, h, dt_bias, a_log, d_skip, norm_g, rider=None):
    T = xbc.shape[0]
    nc = T // CHUNK

    def body(xs_ref, b_ref, c_ref, dt_ref, z_ref, p0, p1, p2, p3, y_ref, st_ref, state):
        @pl.when(pl.program_id(0) == 0)
        def _():
            state[...] = jnp.zeros_like(state)

        prev = state[...]
        st_ref[0] = prev
        yb, nxt = _ssd_chunk(xs_ref[...], b_ref[...], c_ref[...], dt_ref[...], z_ref[...], prev, p0[...], p1[...], p2[...], p3[...])
        y_ref[...] = yb.astype(BF16)
        state[...] = nxt

    outs, rider_outs = _call(
        body, name="ssd_fwd", grid=(nc,),
        in_specs=[_h_cols(B_WIDTH, 0), _h_cols(B_GROUPS * B_STATE, 2), _h_cols(B_GROUPS * B_STATE, 3), _h_cols(LANES, 0), _h_cols(B_WIDTH, 3)]
        + [_full(s) for s in _SSD_PARAM_SHAPES],
        out_specs=[_h_cols(B_WIDTH, 0), pl.BlockSpec((1,) + _STATE_SHAPE, lambda i: (i, 0, 0))],
        out_shape=[SDS((T, B_WIDTH), BF16), SDS((nc,) + _STATE_SHAPE, F32)],
        scratch_shapes=[pltpu.VMEM(_STATE_SHAPE, F32)], sem=("arbitrary",),
        args=(xbc, xbc, xbc, dtr, h, dt_bias, a_log, d_skip, norm_g), rider=rider)
    return outs if rider is None else (outs, rider_outs)


def ssd_bwd(xbc, dtr, h, states, dy, dt_bias, a_log, d_skip, norm_g, dh, rider=None):
    T = xbc.shape[0]
    nc = T // CHUNK

    def rev(width, idx):
        return pl.BlockSpec((CHUNK, width), lambda j: (nc - 1 - j, idx))

    def body(xs_ref, b_ref, c_ref, dt_ref, z_ref, st_ref, dy_ref, p0, p1, p2, p3, dh_ref,
             dxbc_ref, ddt_ref, dz_ref, g0, g1, g2, g3, dstate):
        @pl.when(pl.program_id(0) == 0)
        def _():
            dstate[...] = jnp.zeros_like(dstate)
            for gref in (g0, g1, g2, g3):
                gref[...] = jnp.zeros_like(gref)

        _, vjp = jax.vjp(_ssd_chunk, xs_ref[...], b_ref[...], c_ref[...], dt_ref[...], z_ref[...], st_ref[0],
                         p0[...], p1[...], p2[...], p3[...])
        dxs, dbm, dcm, ddt, dz, dprev, d0, d1, d2, d3 = vjp((dy_ref[...], dstate[...]))
        dxbc_ref[:, :B_WIDTH] = dxs
        dxbc_ref[:, B_WIDTH:B_WIDTH + gn] = dbm
        dxbc_ref[:, B_WIDTH + gn:] = dcm
        ddt_ref[...] = ddt.astype(BF16)
        dz_ref[...] = dz.astype(BF16)
        dstate[...] = dprev
        g0[...] += d0
        g1[...] += d1
        g2[...] += d2
        g3[...] += d3

    gn = B_GROUPS * B_STATE
    outs, rider_outs = _call(
        body, name="ssd_bwd", grid=(nc,),
        in_specs=[rev(B_WIDTH, 0), rev(gn, 2), rev(gn, 3), rev(LANES, 0), rev(B_WIDTH, 3),
                  pl.BlockSpec((1,) + _STATE_SHAPE, lambda j: (nc - 1 - j, 0, 0)), rev(B_WIDTH, 1)]
        + [_full(s) for s in _SSD_PARAM_SHAPES] + [ANY],
        out_specs=[rev(B_XBC, 0), rev(LANES, 0), rev(B_WIDTH, 3)] + [_full(s) for s in _SSD_PARAM_SHAPES],
        out_shape=[SDS((T, B_XBC), F32), SDS((T, LANES), BF16), SDS(dh.shape, dh.dtype)]
        + [SDS(s, F32) for s in _SSD_PARAM_SHAPES],
        scratch_shapes=[pltpu.VMEM(_STATE_SHAPE, F32)], sem=("arbitrary",),
        args=(xbc, xbc, xbc, dtr, h, states, dy, dt_bias, a_log, d_skip, norm_g, dh), rider=rider, aliases={11: 2})
    return outs if rider is None else (outs, rider_outs)


ATT_SCALE = D_HEAD_DIM ** -0.5
Q_COL, K_COL, V_COL, Z_COL = (4 * C_WIDTH // LANES + i * D_HEADS for i in range(4))


ATT_NBLK = ATT_SUPER // ATT_BLOCK


def _res_rows(r, first, count, dil):
    return pl.ds(r + dil * first, count) if dil == 1 else pl.ds(r + dil * first, count, stride=dil)


def _blocks(ref, dil, dtype=None):
    n = ATT_SUPER // dil
    parts = []
    for r in range(dil):
        v = ref[_res_rows(r, 0, n, dil), :]
        parts.append((v if dtype is None else v.astype(dtype)).reshape(n // ATT_BLOCK, ATT_BLOCK, D_HEAD_DIM))
    return parts[0] if dil == 1 else jnp.concatenate(parts, axis=0)


def _blocks_before(cur_ref, prev_ref, dil, dtype):
    n = ATT_SUPER // dil
    parts = []
    for r in range(dil):
        v = prev_ref[_res_rows(r, n - ATT_BLOCK, ATT_BLOCK, dil), :]
        if n > ATT_BLOCK:
            v = jnp.concatenate([v, cur_ref[_res_rows(r, 0, n - ATT_BLOCK, dil), :]], axis=0)
        parts.append(v.astype(dtype).reshape(n // ATT_BLOCK, ATT_BLOCK, D_HEAD_DIM))
    return parts[0] if dil == 1 else jnp.concatenate(parts, axis=0)


def _blocks_after(cur_ref, next_ref, dil, dtype=None):
    n = ATT_SUPER // dil
    parts = []
    for r in range(dil):
        v = next_ref[_res_rows(r, 0, ATT_BLOCK, dil), :]
        if n > ATT_BLOCK:
            v = jnp.concatenate([cur_ref[_res_rows(r, ATT_BLOCK, n - ATT_BLOCK, dil), :], v], axis=0)
        parts.append((v if dtype is None else v.astype(dtype)).reshape(n // ATT_BLOCK, ATT_BLOCK, D_HEAD_DIM))
    return parts[0] if dil == 1 else jnp.concatenate(parts, axis=0)


def _unblock(ref, val, dil, add=False):
    n = ATT_SUPER // dil
    nb = n // ATT_BLOCK
    for r in range(dil):
        v = val[r * nb:(r + 1) * nb].reshape(n, D_HEAD_DIM)
        if add:
            ref[_res_rows(r, 0, n, dil), :] += v
        else:
            ref[_res_rows(r, 0, n, dil), :] = v


def _att_masks(dil, edge_ok, edge_last=False):
    shape = (ATT_NBLK, ATT_BLOCK, ATT_BLOCK)
    blk = lax.broadcasted_iota(jnp.int32, shape, 0)
    row = lax.broadcasted_iota(jnp.int32, shape, 1)
    col = lax.broadcasted_iota(jnp.int32, shape, 2)
    nb = ATT_NBLK // dil
    at_edge = (blk % nb) == (nb - 1 if edge_last else 0)
    return col <= row, jnp.logical_and(col >= row, jnp.logical_or(jnp.logical_not(at_edge), edge_ok))


def _bdot_nt(a, b):
    return lax.dot_general(a, b, (((2,), (2,)), ((0,), (0,))), preferred_element_type=F32)


def _bdot_nn(a, b):
    return lax.dot_general(a, b, (((2,), (1,)), ((0,), (0,))), preferred_element_type=F32)


def _bdot_tn(a, b):
    return lax.dot_general(a, b, (((1,), (1,)), ((0,), (0,))), preferred_element_type=F32)


def _att_spec(col0, shift=0, last=None):
    def imap(hh, n):
        m = n + shift
        if shift < 0:
            m = jnp.maximum(m, 0)
        if shift > 0:
            m = jnp.minimum(m, last)
        return (m, col0 + hh)
    return pl.BlockSpec((ATT_SUPER, D_HEAD_DIM), imap)


def _att_out_spec():
    return pl.BlockSpec((ATT_SUPER, D_HEAD_DIM), lambda hh, n: (n, hh))


def attn_fwd(h):
    T = h.shape[0]
    npat = len(D_PATTERNS)

    def body(q_ref, kc_ref, kp_ref, vc_ref, vp_ref, z_ref, yd_ref, o_ref, lse_ref, *scratch):
        o_s, l_s = scratch[:npat], scratch[npat:]
        has_prev = pl.program_id(1) > 0
        for pi, (_, dil) in enumerate(D_PATTERNS):
            mask_c, mask_p = _att_masks(dil, has_prev)
            q = _blocks(q_ref, dil, BF16)
            kc, vc = _blocks(kc_ref, dil, BF16), _blocks(vc_ref, dil, BF16)
            kp, vp = _blocks_before(kc_ref, kp_ref, dil, BF16), _blocks_before(vc_ref, vp_ref, dil, BF16)
            s_c = jnp.where(mask_c, _bdot_nt(q, kc) * ATT_SCALE, -jnp.inf)
            s_p = jnp.where(mask_p, _bdot_nt(q, kp) * ATT_SCALE, -jnp.inf)
            m = jnp.maximum(jnp.max(s_c, axis=-1, keepdims=True), jnp.max(s_p, axis=-1, keepdims=True))
            p_c = jnp.exp(s_c - m)
            p_p = jnp.exp(s_p - m)
            l = jnp.sum(p_c, axis=-1, keepdims=True) + jnp.sum(p_p, axis=-1, keepdims=True)
            o = _bdot_nn((p_c / l).astype(BF16), vc) + _bdot_nn((p_p / l).astype(BF16), vp)
            _unblock(o_s[pi], o, dil)
            _unblock(l_s[pi], jnp.broadcast_to(m + jnp.log(l), o.shape), dil)
        lses = [l_s[pi][...] for pi in range(npat)]
        mx = functools.reduce(jnp.maximum, lses)
        ws = [jnp.exp(l - mx) for l in lses]
        den = functools.reduce(lambda a, b: a + b, ws)
        o = functools.reduce(lambda a, b: a + b, [(w / den) * o_s[pi][...] for pi, w in enumerate(ws)])
        o_ref[...] = o
        lse_ref[...] = mx + jnp.log(den)
        yd_ref[...] = (_silu(z_ref[...]) * o).astype(BF16)

    n_super = T // ATT_SUPER
    return pl.pallas_call(
        body, name="attn_fwd", grid=(D_HEADS, n_super),
        in_specs=[_att_spec(Q_COL), _att_spec(K_COL), _att_spec(K_COL, -1), _att_spec(V_COL), _att_spec(V_COL, -1), _att_spec(Z_COL)],
        out_specs=[_att_out_spec()] * 3,
        out_shape=[SDS((T, D_HEADS * D_HEAD_DIM), BF16), SDS((T, D_HEADS * D_HEAD_DIM), F32), SDS((T, D_HEADS * D_HEAD_DIM), F32)],
        scratch_shapes=[pltpu.VMEM((ATT_SUPER, D_HEAD_DIM), F32)] * (2 * npat),
        compiler_params=_cp("parallel", "arbitrary"),
    )(h, h, h, h, h, h)


def _dsilu(z):
    s = jax.nn.sigmoid(z)
    return s * (1.0 + z * (1.0 - s))


def attn_bwd(h, o, lse, dy):
    T = h.shape[0]
    last = T // ATT_SUPER - 1
    dy_col = C_WIDTH // LANES

    def spec(col0, shift=0):
        return pl.BlockSpec((ATT_SUPER, D_HEAD_DIM), lambda hh, j: (jnp.maximum(last - j + shift, 0), col0 + hh))

    def add_before(ref, carry_ref, val, dil):
        n = ATT_SUPER // dil
        nb = n // ATT_BLOCK
        for r in range(dil):
            carry_ref[_res_rows(r, n - ATT_BLOCK, ATT_BLOCK, dil), :] += val[r * nb]
            if nb > 1:
                ref[_res_rows(r, 0, n - ATT_BLOCK, dil), :] += val[r * nb + 1:(r + 1) * nb].reshape(n - ATT_BLOCK, D_HEAD_DIM)

    def body(q_ref, kc_ref, kp_ref, vc_ref, vp_ref, z_ref, o_ref, lse_ref, dy_ref, dq_ref, dk_ref, dv_ref, dz_ref,
             do_s, dd_s, dq_s, dk_s, dv_s, dk_carry, dv_carry):
        j = pl.program_id(1)
        has_prev = j < last

        @pl.when(j == 0)
        def _():
            dk_carry[...] = jnp.zeros_like(dk_carry)
            dv_carry[...] = jnp.zeros_like(dv_carry)

        dk_s[...] = dk_carry[...]
        dv_s[...] = dv_carry[...]
        dk_carry[...] = jnp.zeros_like(dk_carry)
        dv_carry[...] = jnp.zeros_like(dv_carry)
        z, oo, dyd = z_ref[...], o_ref[...], dy_ref[...]
        do = dyd * _silu(z)
        dz_ref[...] = (dyd * oo * _dsilu(z)).astype(BF16)
        do_s[...] = do
        dd_s[...] = jnp.broadcast_to(jnp.sum(do * oo, axis=-1, keepdims=True), (ATT_SUPER, D_HEAD_DIM))
        for pi, (_, dil) in enumerate(D_PATTERNS):
            mask_c, mask_p = _att_masks(dil, has_prev)
            q = _blocks(q_ref, dil, BF16)
            kc, vc = _blocks(kc_ref, dil, BF16), _blocks(vc_ref, dil, BF16)
            kp, vp = _blocks_before(kc_ref, kp_ref, dil, BF16), _blocks_before(vc_ref, vp_ref, dil, BF16)
            lse_b, dd_b, do_b = _blocks(lse_ref, dil), _blocks(dd_s, dil), _blocks(do_s, dil, BF16)
            p_c = jnp.where(mask_c, jnp.exp(_bdot_nt(q, kc) * ATT_SCALE - lse_b), 0.0)
            p_p = jnp.where(mask_p, jnp.exp(_bdot_nt(q, kp) * ATT_SCALE - lse_b), 0.0)
            ds_c = (p_c * (_bdot_nt(do_b, vc) - dd_b) * ATT_SCALE).astype(BF16)
            ds_p = (p_p * (_bdot_nt(do_b, vp) - dd_b) * ATT_SCALE).astype(BF16)
            _unblock(dq_s, _bdot_nn(ds_c, kc) + _bdot_nn(ds_p, kp), dil, add=pi > 0)
            _unblock(dk_s, _bdot_tn(ds_c, q), dil, add=True)
            _unblock(dv_s, _bdot_tn(p_c.astype(BF16), do_b), dil, add=True)
            add_before(dk_s, dk_carry, _bdot_tn(ds_p, q), dil)
            add_before(dv_s, dv_carry, _bdot_tn(p_p.astype(BF16), do_b), dil)
        dq_ref[...] = dq_s[...].astype(BF16)
        dk_ref[...] = dk_s[...].astype(BF16)
        dv_ref[...] = dv_s[...].astype(BF16)

    blk = (ATT_SUPER, D_HEAD_DIM)
    out = pl.BlockSpec(blk, lambda hh, j: (last - j, hh))
    return pl.pallas_call(
        body, name="attn_bwd", grid=(D_HEADS, last + 1),
        in_specs=[spec(Q_COL), spec(K_COL), spec(K_COL, -1), spec(V_COL), spec(V_COL, -1), spec(Z_COL), spec(0), spec(0), spec(dy_col)],
        out_specs=[out] * 4, out_shape=[SDS((T, D_HEADS * D_HEAD_DIM), BF16)] * 4,
        scratch_shapes=[pltpu.VMEM(blk, F32)] * 7, compiler_params=_cp("parallel", "arbitrary"),
    )(h, h, h, h, h, h, o, lse, dy)


ANY = pl.BlockSpec(memory_space=pl.ANY)
COMM_PARAMS = pltpu.CompilerParams()


def _place():
    x, y, c = lax.axis_index("x"), lax.axis_index("y"), lax.axis_index("c")
    return x, y, c, [(1 - x, y), (x, 1 - y), (1 - x, 1 - y)]


def _rcopy(src, dst, ssem, rsem, dev):
    return pltpu.make_async_remote_copy(src_ref=src, dst_ref=dst, send_sem=ssem, recv_sem=rsem, device_id=dev, device_id_type=MESH)


def gather_rider(arrs, fractions=(0.0, 0.6, 1.0)):
    n = len(arrs)
    per = 7

    def to_chips(ins, outs, ssem, rsem):
        x, y, c, chips = _place()
        return [_rcopy(ins[a].at[c], outs[a].at[2 * x + y, c], ssem.at[per * a + j], rsem.at[per * a + j], (px, py, c))
                for a in range(n) for j, (px, py) in enumerate(chips)]

    def passed_on(outs, ssem, rsem, half):
        x, y, c, chips = _place()
        cps = []
        for a in range(n):
            for j, (px, py) in enumerate(chips):
                slot = outs[a].at[2 * px + py, half(c)]
                cps.append(_rcopy(slot, slot, ssem.at[per * a + 3 + j], rsem.at[per * a + 3 + j], (x, y, 1 - c)))
        return cps

    def own(ins, outs, ssem, rsem):
        x, y, c, _ = _place()
        return [_rcopy(ins[a], outs[a].at[2 * x + y], ssem.at[per * a + 6], rsem.at[per * a + 6], (x, y, 1 - c)) for a in range(n)]

    def start(ins, outs, ssem, rsem):
        for cp in to_chips(ins, outs, ssem, rsem) + own(ins, outs, ssem, rsem):
            cp.start()

    def pass_on(ins, outs, ssem, rsem):
        x, y, c, chips = _place()
        landed = [_rcopy(outs[a].at[2 * px + py, c], outs[a].at[2 * px + py, c], ssem.at[per * a + j], rsem.at[per * a + j], (px, py, c))
                  for a in range(n) for j, (px, py) in enumerate(chips)]
        for arrival, cp in zip(landed, passed_on(outs, ssem, rsem, lambda c: c)):
            arrival.wait_recv()
            cp.start()

    def finish(ins, outs, ssem, rsem):
        for cp in passed_on(outs, ssem, rsem, lambda c: 1 - c):
            cp.wait_recv()
        for cp in to_chips(ins, outs, ssem, rsem) + passed_on(outs, ssem, rsem, lambda c: c):
            cp.wait_send()
        for cp in own(ins, outs, ssem, rsem):
            cp.wait()

    return Rider(arrs, [SDS((N_CHIPS,) + a.shape, a.dtype) for a in arrs], per * n,
                 [(fractions[0], start), (fractions[1], pass_on), (fractions[2], finish)])


def _copies_rider(ins, out_shapes, n_sems, make):
    def start(*refs):
        for cp in make(*refs):
            cp.start()

    def finish(*refs):
        for cp in make(*refs):
            cp.wait()

    return Rider(ins, out_shapes, n_sems, [(0.0, start), (1.0, finish)])


def swap_halves_rider(arrs):
    def make(ins, outs, ssem, rsem):
        x, y, c, _ = _place()
        return [_rcopy(ins[a].at[1 - c], outs[a], ssem.at[a], rsem.at[a], (x, y, 1 - c)) for a in range(len(arrs))]
    return _copies_rider(arrs, [SDS(a.shape[1:], a.dtype) for a in arrs], len(arrs), make)


def scatter_rider(arrs):
    def make(ins, outs, ssem, rsem):
        x, y, c, chips = _place()
        return [_rcopy(ins[a].at[2 * px + py], outs[a].at[j], ssem.at[3 * a + j], rsem.at[3 * a + j], (px, py, c))
                for a in range(len(arrs)) for j, (px, py) in enumerate(chips)]
    return _copies_rider(arrs, [SDS((N_CHIPS - 1,) + a.shape[1:], a.dtype) for a in arrs], 3 * len(arrs), make)


def join_halves_rider(arrs):
    def make(ins, outs, ssem, rsem):
        x, y, c, _ = _place()
        return [_rcopy(ins[a].at[c], outs[a].at[c], ssem.at[a], rsem.at[a], (x, y, 1 - c)) for a in range(len(arrs))]

    def start(*refs):
        for cp in make(*refs):
            cp.start()

    def finish(ins, outs, ssem, rsem):
        x, y, c, _ = _place()
        for a, cp in enumerate(make(ins, outs, ssem, rsem)):
            cp.wait_send()
            _rcopy(ins[a].at[1 - c], outs[a].at[1 - c], ssem.at[a], rsem.at[a], (x, y, 1 - c)).wait_recv()

    return Rider(arrs, [SDS(a.shape, a.dtype) for a in arrs], len(arrs), [(0.0, start), (1.0, finish)], in_place=True)


def gather_all(buf):
    def body(in_ref, out_ref, ssem, rsem, lsem):
        x, y, c, _ = _place()
        me = 4 * x + 2 * y + c
        local = pltpu.make_async_copy(in_ref, out_ref.at[me], lsem)
        local.start()
        flips = [(a, b, e) for a in (0, 1) for b in (0, 1) for e in (0, 1)][1:]
        cps = []
        for i, (a, b, e) in enumerate(flips):
            peer = (x ^ a, y ^ b, c ^ e)
            cps.append(_rcopy(in_ref, out_ref.at[me], ssem.at[i], rsem.at[i], peer))
        for cp in cps:
            cp.start()
        for i, (a, b, e) in enumerate(flips):
            cps[i].wait_send()
            slot = out_ref.at[4 * (x ^ a) + 2 * (y ^ b) + (c ^ e)]
            _rcopy(slot, slot, ssem.at[i], rsem.at[i], (x ^ a, y ^ b, c ^ e)).wait_recv()
        local.wait()

    return pl.pallas_call(
        body, name="comm_gather_all", in_specs=[ANY], out_specs=ANY, out_shape=SDS((N_DEV,) + buf.shape, buf.dtype),
        scratch_shapes=[pltpu.SemaphoreType.DMA((N_DEV - 1,)), pltpu.SemaphoreType.DMA((N_DEV - 1,)), pltpu.SemaphoreType.DMA],
        compiler_params=COMM_PARAMS,
    )(buf)


def _pack_offsets(parts):
    offs, r = [], 0
    for p in parts:
        offs.append(r)
        r += -(-p.shape[0] // SUBLANES) * SUBLANES
    return offs, r


def pack_rows(parts):
    offs, total = _pack_offsets(parts)

    def body(*refs):
        out = refs[-1]
        out[...] = jnp.zeros_like(out)
        for ref, off in zip(refs[:-1], offs):
            out[off:off + ref.shape[0], :] = ref[...]

    vmem = pl.BlockSpec(memory_space=pltpu.VMEM)
    return pl.pallas_call(body, name="pack_small", in_specs=[vmem] * len(parts), out_specs=vmem,
                          out_shape=SDS((total, LANES), F32))(*parts)


EVEN_SHARD = IN_EVEN // N_CHIPS


def wie_from_shards(g):
    tr = 256

    def body(g_ref, main_ref, dt_ref):
        full = jnp.concatenate([g_ref[k] for k in range(N_CHIPS)], axis=1)
        main_ref[...] = full[:, :EVEN_MAIN]
        dt_ref[...] = jnp.concatenate([full[:, EVEN_MAIN:], jnp.zeros((tr, LANES - B_HEADS), full.dtype)], axis=1)

    return pl.pallas_call(
        body, name="wie_from_shards", grid=(D_MODEL // tr,),
        in_specs=[pl.BlockSpec((N_CHIPS, tr, EVEN_SHARD), lambda i: (0, i, 0))],
        out_specs=[pl.BlockSpec((tr, EVEN_MAIN), lambda i: (i, 0)), pl.BlockSpec((tr, LANES), lambda i: (i, 0))],
        out_shape=[SDS((D_MODEL, EVEN_MAIN), g.dtype), SDS((D_MODEL, LANES), g.dtype)], compiler_params=_cp("parallel"))(g)


def wie_grad_to_pieces(main, dt):
    tr = 128
    per_half = D_MODEL // 2 // tr

    def body(m_ref, d_ref, o_ref):
        full = jnp.concatenate([m_ref[...], d_ref[:, :B_HEADS]], axis=1)
        for k in range(N_CHIPS):
            o_ref[0, k] = full[:, k * EVEN_SHARD:(k + 1) * EVEN_SHARD]

    return pl.pallas_call(
        body, name="wie_grad_to_pieces", grid=(2, per_half),
        in_specs=[pl.BlockSpec((tr, EVEN_MAIN), lambda c, i: (c * per_half + i, 0)), pl.BlockSpec((tr, LANES), lambda c, i: (c * per_half + i, 0))],
        out_specs=pl.BlockSpec((1, N_CHIPS, tr, EVEN_SHARD), lambda c, i: (c, 0, i, 0)),
        out_shape=SDS((2, N_CHIPS, D_MODEL // 2, EVEN_SHARD), F32), compiler_params=_cp("parallel", "parallel"))(main, dt)


def _unpack(buf, parts):
    offs, _ = _pack_offsets(parts)
    return [buf[off:off + p.shape[0]] for p, off in zip(parts, offs)]


def _pad_lanes(v):
    v = v.reshape(1, -1)
    return jnp.pad(v, ((0, 0), (0, LANES - v.shape[1])))


def _as2d(a):
    return a.reshape(1, -1) if a.ndim == 1 else a.reshape(-1, a.shape[-1])


def kernel(x, even_norm_g, even_w_in, gmlp_ln_g, gmlp_ln_b, gmlp_ws, gmlp_bs, ssd_conv_w, ssd_conv_b, ssd_dt_bias, ssd_a_log, ssd_d, ssd_norm_g, even_w_out, odd_norm_g, odd_w_in, sconv_w, odd_w_out, final_norm_g, loss_target, m_even_norm_g, m_even_w_in, m_gmlp_ln_g, m_gmlp_ln_b, m_gmlp_ws, m_gmlp_bs, m_ssd_conv_w, m_ssd_conv_b, m_ssd_dt_bias, m_ssd_a_log, m_ssd_d, m_ssd_norm_g, m_even_w_out, m_odd_norm_g, m_odd_w_in, m_sconv_w, m_odd_w_out, m_final_norm_g, v_even_norm_g, v_even_w_in, v_gmlp_ln_g, v_gmlp_ln_b, v_gmlp_ws, v_gmlp_bs, v_ssd_conv_w, v_ssd_conv_b, v_ssd_dt_bias, v_ssd_a_log, v_ssd_d, v_ssd_norm_g, v_even_w_out, v_odd_norm_g, v_odd_w_in, v_sconv_w, v_odd_w_out, v_final_norm_g):
    weights = dict(even_norm_g=even_norm_g, even_w_in=even_w_in, gmlp_ln_g=gmlp_ln_g, gmlp_ln_b=gmlp_ln_b, gmlp_ws=gmlp_ws, gmlp_bs=gmlp_bs, ssd_conv_w=ssd_conv_w, ssd_conv_b=ssd_conv_b, ssd_dt_bias=ssd_dt_bias, ssd_a_log=ssd_a_log, ssd_d=ssd_d, ssd_norm_g=ssd_norm_g, even_w_out=even_w_out, odd_norm_g=odd_norm_g, odd_w_in=odd_w_in, sconv_w=sconv_w, odd_w_out=odd_w_out, final_norm_g=final_norm_g)
    moms_m = dict(even_norm_g=m_even_norm_g, even_w_in=m_even_w_in, gmlp_ln_g=m_gmlp_ln_g, gmlp_ln_b=m_gmlp_ln_b, gmlp_ws=m_gmlp_ws, gmlp_bs=m_gmlp_bs, ssd_conv_w=m_ssd_conv_w, ssd_conv_b=m_ssd_conv_b, ssd_dt_bias=m_ssd_dt_bias, ssd_a_log=m_ssd_a_log, ssd_d=m_ssd_d, ssd_norm_g=m_ssd_norm_g, even_w_out=m_even_w_out, odd_norm_g=m_odd_norm_g, odd_w_in=m_odd_w_in, sconv_w=m_sconv_w, odd_w_out=m_odd_w_out, final_norm_g=m_final_norm_g)
    moms_v = dict(even_norm_g=v_even_norm_g, even_w_in=v_even_w_in, gmlp_ln_g=v_gmlp_ln_g, gmlp_ln_b=v_gmlp_ln_b, gmlp_ws=v_gmlp_ws, gmlp_bs=v_gmlp_bs, ssd_conv_w=v_ssd_conv_w, ssd_conv_b=v_ssd_conv_b, ssd_dt_bias=v_ssd_dt_bias, ssd_a_log=v_ssd_a_log, ssd_d=v_ssd_d, ssd_norm_g=v_ssd_norm_g, even_w_out=v_even_w_out, odd_norm_g=v_odd_norm_g, odd_w_in=v_odd_w_in, sconv_w=v_sconv_w, odd_w_out=v_odd_w_out, final_norm_g=v_final_norm_g)
    names = list(weights)

    xs = x[0]
    tgt = loss_target[0]
    T = xs.shape[0]
    chip = 2 * lax.axis_index("x") + lax.axis_index("y")
    core = lax.axis_index("c")
    cshard = B_XBC // N_CHIPS
    dshard = D_MODEL // N_CHIPS

    def halves(w):
        return w.astype(BF16).reshape(2, w.shape[0] // 2, w.shape[1])

    small_shard = jnp.concatenate([ssd_conv_w[0].reshape(-1), odd_norm_g[0], sconv_w[0].reshape(-1)])
    g_wie, g_small = run_rider(gather_rider([halves(even_w_in[0]), small_shard.reshape(2, -1, LANES)]), name="comm_gather_first")
    wie_main, wie_dt = wie_from_shards(g_wie.reshape(N_CHIPS, D_MODEL, EVEN_SHARD))
    g_small = g_small.reshape(N_CHIPS, -1)
    n_cw = B_CONV * cshard
    conv_w = g_small[:, :n_cw].reshape(N_CHIPS, B_CONV, cshard).transpose(1, 0, 2).reshape(B_CONV, B_XBC)
    odd_g = g_small[:, n_cw:n_cw + dshard].reshape(1, D_MODEL)
    sconv = g_small[:, n_cw + dshard:].reshape(N_CHIPS, C_CONV, dshard).transpose(1, 0, 2).reshape(C_CONV, C_WIDTH)

    even_g = even_norm_g
    ln_g, ln_b = gmlp_ln_g, gmlp_ln_b
    ws, bs_t = gmlp_ws[0], gmlp_bs[0].T
    conv_b = ssd_conv_b
    dt_bias, a_log, d_skip = _pad_lanes(ssd_dt_bias), _pad_lanes(ssd_a_log), _pad_lanes(ssd_d)
    norm_g = ssd_norm_g
    fin_g = final_norm_g.reshape(1, D_MODEL)

    xn0 = rmsnorm_fwd(xs, even_g, name="even_norm")
    h0, (g_wio,) = matmul(xn0, wie_main, "nn", name="even_in", rider=gather_rider([halves(odd_w_in[0])], (0.0, 0.88, 1.0)))
    wio = g_wio.reshape(N_CHIPS, D_MODEL, IN_ODD // N_CHIPS)
    dtr = matmul(xn0, wie_dt, "nn", name="even_in_dt", tk=D_MODEL)
    ya = gmlp_fwd(h0, ln_g, ln_b, ws, bs_t)
    xbc = ssd_conv_fwd(h0, conv_w, conv_b)
    (yb, states), (g_woe, g_woo) = ssd_fwd(xbc, dtr, h0, dt_bias, a_log, d_skip, norm_g,
                                           rider=gather_rider([halves(even_w_out[0]), halves(odd_w_out[0])]))
    woe = g_woe.reshape(2 * A_WIDTH, D_MODEL)
    woo = g_woo.reshape(2 * C_WIDTH, D_MODEL)
    y0 = [ya, yb]
    x1 = matmul(y0, woe, "nn", name="even_out", res=xs)

    xn1 = rmsnorm_fwd(x1, odd_g, name="odd_norm")
    h1 = matmul(xn1, wio, "nn", name="odd_in")
    yc = sconv_fwd(h1, sconv)
    yd, att_o, att_lse = attn_fwd(h1)
    y1 = [yc, yd]
    x2 = matmul(y1, woo, "nn", name="odd_out", res=x1)

    loss_part, dx2, dx2b, d_fin_g = loss_head(x2, fin_g, tgt)

    tile = 1024
    rows_layout = ((2, N_CHIPS, ROW_PIECE, D_MODEL), (1, 1, ROW_PIECE, tile), lambda i, j, k: (i % 2, i // 2, 0, j))
    per_chip = IN_ODD // N_CHIPS // tile
    cols_layout = ((2, N_CHIPS, D_MODEL // 2, IN_ODD // N_CHIPS), (1, 1, tile, tile), lambda i, j, k: (i, j // per_chip, 0, j % per_chip))
    dy1 = matmul(dx2b, woo, "nt", name="odd_out_dy")
    d_woo = matmul(y1, dx2b, "tn", name="odd_out_dw", tm=ROW_PIECE, out_layout=rows_layout)
    dbg, dcg, dhx, dzc, d_sconv = sconv_bwd(h1, dy1, sconv)
    dq, dk, dv, dzd = attn_bwd(h1, att_o, att_lse, dy1)
    dh1 = jnp.concatenate([dbg, dcg, dhx, dzc, dq, dk, dv, dzd], axis=1)
    dxn1 = matmul(dh1, wio, "nt", name="odd_in_dx")
    d_wio = matmul(xn1, dh1, "tn", name="odd_in_dw", out_layout=cols_layout)
    dx1, dx1b, d_odd_g = rmsnorm_bwd(x1, odd_g, dxn1, dx2, name="odd_norm_bwd")

    d_woe = matmul(y0, dx1b, "tn", name="even_out_dw", tm=ROW_PIECE, out_layout=rows_layout)
    first = [d_wio, d_woo, d_woe]
    dy0, first_sib = matmul(dx1b, woe, "nt", name="even_out_dy", rider=swap_halves_rider(first))
    first_sums = [chip_sum(p, s, core, name=f"chip_sum_first_{i}") for i, (p, s) in enumerate(zip(first, first_sib))]
    dh0, d_ln_g, d_ln_b, d_ws, d_bs_t = gmlp_bwd(h0, dy0, ln_g, ln_b, ws, bs_t)
    (dxbc_act, ddtr, dh0, d_dt_bias, d_a_log, d_d, d_norm_g), first_landed = ssd_bwd(
        xbc, dtr, h0, states, dy0, dt_bias, a_log, d_skip, norm_g, dh0, rider=scatter_rider(first_sums))
    first_totals = [total_sum(s, l, chip, core, name=f"total_first_{i}") for i, (s, l) in enumerate(zip(first_sums, first_landed))]
    (dh0, d_conv_w, d_conv_b), first_joined = ssd_conv_bwd(h0, dxbc_act, conv_w, conv_b, dh0, rider=join_halves_rider(first_totals))
    ddtr_b = ddtr
    d_wie_main = matmul(xn0, dh0, "tn", name="even_in_dw")
    d_wie_dt = matmul(xn0, ddtr_b, "tn", name="even_in_dw_dt")
    last = [wie_grad_to_pieces(d_wie_main, d_wie_dt)]
    dxn0, last_sib = matmul(ddtr_b, wie_dt, "nt", name="even_in_dx_dt", rider=swap_halves_rider(last))
    last_sums = [chip_sum(last[0], last_sib[0], core, name="chip_sum_last")]
    dxn0, last_landed = matmul(dh0, wie_main, "nt", name="even_in_dx", res=dxn0, rider=scatter_rider(last_sums))
    last_totals = [total_sum(last_sums[0], last_landed[0], chip, core, name="total_last")]
    grad_x, _, d_even_g = rmsnorm_bwd(xs, even_g, dxn0, dx1, name="even_norm_bwd")

    small_names = ["even_norm_g", "gmlp_ln_g", "gmlp_ln_b", "gmlp_ws", "gmlp_bs", "ssd_conv_w", "ssd_conv_b", "ssd_dt_bias",
                   "ssd_a_log", "ssd_d", "ssd_norm_g", "odd_norm_g", "sconv_w", "final_norm_g"]
    small_parts = [d_even_g, d_ln_g, d_ln_b, d_ws, d_bs_t.T, d_conv_w, d_conv_b, d_dt_bias, d_a_log, d_d, d_norm_g, d_odd_g, d_sconv, d_fin_g]
    small_shapes = [p.shape for p in small_parts]
    small_rows = [p.reshape(-1, LANES) for p in small_parts]
    small_sum, last_joined = sum_leading(gather_all(pack_rows(small_rows)), name="small_sum", rider=join_halves_rider(last_totals))
    full = {nm: rows.reshape(shape) for nm, rows, shape in zip(small_names, _unpack(small_sum, small_rows), small_shapes)}
    joined = last_joined + first_joined
    grads = dict(even_w_in=joined[0].reshape(even_w_in.shape), odd_w_in=joined[1].reshape(odd_w_in.shape),
                 odd_w_out=joined[2].reshape(odd_w_out.shape), even_w_out=joined[3].reshape(even_w_out.shape))
    for nm in small_names:
        g = full[nm]
        if nm in ("ssd_dt_bias", "ssd_a_log", "ssd_d"):
            g = g[:, :B_HEADS]
        elif nm == "ssd_conv_w":
            g = lax.dynamic_slice_in_dim(g, chip * cshard, cshard, axis=1)
        elif nm in ("odd_norm_g", "sconv_w"):
            g = lax.dynamic_slice_in_dim(g, chip * dshard, dshard, axis=1)
        grads[nm] = g.reshape(weights[nm].shape)

    deltas, new_m, new_v = {}, {}, {}
    for nm in names:
        w = weights[nm]
        d, nm_, nv_ = adamw(_as2d(w), _as2d(grads[nm]), _as2d(moms_m[nm]), _as2d(moms_v[nm]), name=f"adamw_{nm}")
        deltas[nm], new_m[nm], new_v[nm] = d.reshape(w.shape), nm_.reshape(w.shape), nv_.reshape(w.shape)

    loss = lax.psum(loss_part[0, 0], ("x", "y", "c"))
    return (loss, grad_x[None], *[grads[n] for n in names], *[deltas[n] for n in names],
            *[new_m[n] for n in names], *[new_v[n] for n in names])
```

```python
import functools

import jax
import jax.numpy as jnp
from jax import lax
from jax.experimental import pallas as pl
from jax.experimental.pallas import tpu as pltpu

F32 = jnp.float32
BF16 = jnp.bfloat16
SDS = jax.ShapeDtypeStruct
MESH = pl.DeviceIdType.MESH

D_MODEL = 2048
A_WIDTH = 2048
A_GROUPS = 8
CHUNK = 128
B_WIDTH = 2048
B_HEADS = 32
B_HEAD_DIM = 64
B_GROUPS = 8
B_STATE = 128
B_CONV = 4
B_XBC = B_WIDTH + 2 * B_GROUPS * B_STATE
C_WIDTH = 2048
C_CONV = 3
D_HEADS = 16
D_HEAD_DIM = 128
D_PATTERNS = ((128, 1), (512, 4), (2048, 16))
ATT_BLOCK = 128
ATT_SUPER = 2048
EVEN_MAIN = 3 * A_WIDTH + B_WIDTH + B_XBC
IN_EVEN = EVEN_MAIN + B_HEADS
IN_ODD = 4 * C_WIDTH + 4 * D_HEADS * D_HEAD_DIM
LANES = 128
SUBLANES = 8
EPS = 1e-5
ADAM_LR = 0.001
ADAM_B1 = 0.9
ADAM_B2 = 0.999
ADAM_EPS = 1e-08
ADAM_WD = 0.01
ADAM_STEP = 10
N_CHIPS = 4
N_DEV = 8
VMEM_LIMIT_BYTES = 56 * 1024 * 1024


def _cp(*sem):
    return pltpu.CompilerParams(dimension_semantics=sem, vmem_limit_bytes=VMEM_LIMIT_BYTES)


def _full(shape):
    return pl.BlockSpec(shape, lambda *_: (0,) * len(shape))


def _silu(x):
    return x * jax.nn.sigmoid(x)


def _dot_nn(a, b):
    return lax.dot_general(a, b, (((1,), (0,)), ((), ())), preferred_element_type=F32)


def _dot_nt(a, b):
    return lax.dot_general(a, b, (((1,), (1,)), ((), ())), preferred_element_type=F32)


def _dot_tn(a, b):
    return lax.dot_general(a, b, (((0,), (0,)), ((), ())), preferred_element_type=F32)


def _tril(n):
    return lax.broadcasted_iota(jnp.int32, (n, n), 0) >= lax.broadcasted_iota(jnp.int32, (n, n), 1)


_DOTS = {"nn": _dot_nn, "nt": _dot_nt, "tn": _dot_tn}


class Rider:
    def __init__(self, ins, out_shapes, n_sems, phases, in_place=False):
        self.ins, self.out_shapes, self.n_sems, self.phases = list(ins), list(out_shapes), n_sems, list(phases)
        self.in_place = in_place


def _call(body, *, name, grid, in_specs, out_specs, out_shape, scratch_shapes, sem, args, rider=None, aliases=None):
    in_specs, out_specs, out_shape, scratch_shapes = list(in_specs), list(out_specs), list(out_shape), list(scratch_shapes)
    aliases = dict(aliases or {})
    if rider is None:
        res = pl.pallas_call(body, name=name, grid=grid, in_specs=in_specs, out_specs=out_specs, out_shape=out_shape,
                             scratch_shapes=scratch_shapes, input_output_aliases=aliases, compiler_params=_cp(*sem))(*args)
        return list(res), []
    if rider.in_place:
        aliases.update({len(in_specs) + k: len(out_specs) + k for k in range(len(rider.ins))})
    counts = [len(in_specs), len(rider.ins), len(out_specs), len(rider.out_shapes), len(scratch_shapes), 2]
    total = 1
    for g in grid:
        total *= g

    def wrapped(*refs):
        groups, pos = [], 0
        for n in counts:
            groups.append(refs[pos:pos + n])
            pos += n
        ins, rins, outs, routs, scr, (ssem, rsem) = groups
        step = 0
        for d, g in enumerate(grid):
            step = step * g + pl.program_id(d)
        for frac, fn in rider.phases:
            @pl.when(step == min(int(frac * total), total - 1))
            def _(fn=fn):
                fn(rins, routs, ssem, rsem)
        body(*ins, *outs, *scr)

    dma = pltpu.SemaphoreType.DMA((rider.n_sems,))
    res = pl.pallas_call(
        wrapped, name=name, grid=grid, in_specs=in_specs + [ANY] * len(rider.ins), out_specs=out_specs + [ANY] * len(rider.out_shapes),
        out_shape=out_shape + rider.out_shapes, scratch_shapes=scratch_shapes + [dma, dma], input_output_aliases=aliases,
        compiler_params=_cp(*(("arbitrary",) * len(grid))))(*args, *rider.ins)
    return list(res[:len(out_specs)]), list(res[len(out_specs):])


def run_rider(rider, *, name):
    def body(*refs):
        n_in, n_out = len(rider.ins), len(rider.out_shapes)
        ins, outs, (ssem, rsem) = refs[:n_in], refs[n_in:n_in + n_out], refs[n_in + n_out:]
        for _, fn in rider.phases:
            fn(ins, outs, ssem, rsem)

    dma = pltpu.SemaphoreType.DMA((rider.n_sems,))
    return list(pl.pallas_call(body, name=name, in_specs=[ANY] * len(rider.ins), out_specs=[ANY] * len(rider.out_shapes),
                               out_shape=rider.out_shapes, scratch_shapes=[dma, dma])(*rider.ins))


def matmul(a, b, mode, *, name, out_dtype=F32, res=None, tm=1024, tn=1024, tk=2048, out_layout=None, rider=None):
    a_parts = list(a) if isinstance(a, (list, tuple)) else [a]
    b_parts = list(b) if isinstance(b, (list, tuple)) else [b]
    assert len(b_parts) == 1 or mode == "tn"
    b0 = b_parts[0]
    shards = b0.shape[0] if b0.ndim == 3 else 1
    a_rows, a_cols = a_parts[0].shape[0], sum(p.shape[1] for p in a_parts)
    b_rows, b_cols = b0.shape[-2], sum(p.shape[-1] for p in b_parts) * shards
    if mode == "tn":
        (K, M), (K2, N) = (a_rows, a_cols), (b_rows, b_cols)
    elif mode == "nt":
        (M, K), (N, K2) = (a_rows, a_cols), (b_rows, b_cols)
    else:
        (M, K), (K2, N) = (a_rows, a_cols), (b_rows, b_cols)
    assert K == K2, (mode, K, K2)
    tm, tn, tk = min(tm, M), min(tn, N), min(tk, K)
    if shards > 1:
        assert mode != "tn" and b0.shape[-1] % (tk if mode == "nt" else tn) == 0
    assert M % tm == 0 and N % tn == 0 and K % tk == 0, (M, N, K, tm, tn, tk)
    nk = K // tk
    dot = _DOTS[mode]

    def spans(parts, tile):
        out, off = [], 0
        for p in parts:
            assert p.shape[-1] % tile == 0, (p.shape, tile)
            out.append((off, p.shape[-1] // tile))
            off += p.shape[-1] // tile
        return out

    a_axis = 0 if mode == "tn" else 2
    a_spans = spans(a_parts, tm if mode == "tn" else tk)
    b_spans = spans(b_parts, tn) if len(b_parts) > 1 else [(0, N // tn)]

    def inside(t, span):
        return jnp.logical_and(t >= span[0], t < span[0] + span[1])

    def body(*refs):
        a_refs, b_refs, rest = refs[:len(a_parts)], refs[len(a_parts):len(a_parts) + len(b_parts)], refs[len(a_parts) + len(b_parts):]
        r_ref = rest[0] if res is not None else None
        o_ref = rest[1] if res is not None else rest[0]
        k = pl.program_id(2)

        def finish(acc):
            if res is not None:
                acc = acc + r_ref[...]
            o_ref[...] = acc.astype(o_ref.dtype).reshape(o_ref.shape)

        def emit(a_ref, b_ref, conds, k_lo, k_hi):
            def region(*more):
                cs = conds + list(more)
                return pl.when(functools.reduce(jnp.logical_and, cs)) if cs else (lambda f: f())

            def product():
                return dot(a_ref[...], b_ref[0] if shards > 1 else b_ref[...])

            if nk == 1:
                region()(lambda: finish(product()))
                return
            acc_ref = rest[-1]
            if k_lo == 0:
                @region(k == 0)
                def _():
                    acc_ref[...] = product()

            if max(k_lo, 1) < min(k_hi, nk - 1):
                @region(k > 0, k < nk - 1)
                def _():
                    acc_ref[...] += product()

            if k_hi == nk:
                @region(k == nk - 1)
                def _():
                    finish(acc_ref[...] + product())

        for a_ref, a_span in zip(a_refs, a_spans):
            for b_ref, b_span in zip(b_refs, b_spans):
                by_k = a_axis == 2 and len(a_parts) > 1
                emit(a_ref, b_ref, ([inside(pl.program_id(a_axis), a_span)] if len(a_parts) > 1 else []) +
                     ([inside(pl.program_id(1), b_span)] if len(b_parts) > 1 else []),
                     a_span[0] if by_k else 0, a_span[0] + a_span[1] if by_k else nk)

    def piece_index(t, span):
        return jnp.clip(t - span[0], 0, span[1] - 1)

    def a_spec_of(span):
        if len(a_parts) == 1:
            return pl.BlockSpec((tk, tm), lambda i, j, k: (k, i)) if mode == "tn" else pl.BlockSpec((tm, tk), lambda i, j, k: (i, k))
        if mode == "tn":
            return pl.BlockSpec((tk, tm), lambda i, j, k: (jnp.where(inside(i, span), k, 0), piece_index(i, span)))
        return pl.BlockSpec((tm, tk), lambda i, j, k: (i, piece_index(k, span)))

    def b_spec_of(span):
        if shards > 1 and mode == "nn":
            per = b0.shape[-1] // tn
            return pl.BlockSpec((1, tk, tn), lambda i, j, k: (j // per, k, j % per))
        if shards > 1:
            per = b0.shape[-1] // tk
            return pl.BlockSpec((1, tn, tk), lambda i, j, k: (k // per, j, k % per))
        if mode == "nt":
            return pl.BlockSpec((tn, tk), lambda i, j, k: (j, k))
        if len(b_parts) == 1:
            return pl.BlockSpec((tk, tn), lambda i, j, k: (k, j))
        return pl.BlockSpec((tk, tn), lambda i, j, k: (jnp.where(inside(j, span), k, 0), piece_index(j, span)))

    o_spec = pl.BlockSpec((tm, tn), lambda i, j, k: (i, j))
    in_specs = [a_spec_of(s) for s in a_spans] + [b_spec_of(s) for s in b_spans]
    args = a_parts + b_parts
    if res is not None:
        in_specs.append(o_spec)
        args.append(res)
    out_shape = SDS((M, N), out_dtype)
    if out_layout is not None:
        out_shape, o_spec = SDS(out_layout[0], out_dtype), pl.BlockSpec(out_layout[1], out_layout[2])
    outs, rider_outs = _call(
        body, name=name, grid=(M // tm, N // tn, nk), in_specs=in_specs, out_specs=[o_spec], out_shape=[out_shape],
        scratch_shapes=[pltpu.VMEM((tm, tn), F32)] if nk > 1 else [], sem=("parallel", "parallel", "arbitrary"),
        args=args, rider=rider)
    return outs[0] if rider is None else (outs[0], rider_outs)


ROW_TILE = 512
ROW_PIECE = 512


def _rms(x, g):
    return x * lax.rsqrt(jnp.mean(x * x, axis=-1, keepdims=True) + EPS) * g


def rmsnorm_fwd(x, g, *, name):
    T, D = x.shape

    def body(x_ref, g_ref, o_ref):
        o_ref[...] = _rms(x_ref[...], g_ref[...]).astype(BF16)

    row = pl.BlockSpec((ROW_TILE, D), lambda i: (i, 0))
    return pl.pallas_call(body, name=name, grid=(T // ROW_TILE,), in_specs=[row, _full((1, D))], out_specs=row,
                          out_shape=SDS((T, D), BF16), compiler_params=_cp("parallel"))(x, g)


def rmsnorm_bwd(x, g, dxn, dres, *, name, rider=None):
    T, D = x.shape

    def body(x_ref, g_ref, dxn_ref, dres_ref, dx_ref, dxb_ref, dg_ref):
        _, vjp = jax.vjp(_rms, x_ref[...], g_ref[...])
        dx, dg = vjp(dxn_ref[...])
        dx = dx + dres_ref[...]
        dx_ref[...] = dx
        dxb_ref[...] = dx.astype(BF16)

        @pl.when(pl.program_id(0) == 0)
        def _():
            dg_ref[...] = jnp.zeros_like(dg_ref)

        dg_ref[...] += dg

    row = pl.BlockSpec((ROW_TILE, D), lambda i: (i, 0))
    outs, rider_outs = _call(
        body, name=name, grid=(T // ROW_TILE,), in_specs=[row, _full((1, D)), row, row],
        out_specs=[row, row, _full((1, D))], out_shape=[SDS((T, D), F32), SDS((T, D), BF16), SDS((1, D), F32)],
        scratch_shapes=[], sem=("arbitrary",), args=(x, g, dxn, dres), rider=rider)
    return outs if rider is None else (outs, rider_outs)


def _loss_tile(x, g, tgt):
    err = jnp.square(_rms(x, g) - tgt)
    return 0.5 * jnp.sum(jnp.mean(err, axis=-1))


def loss_head(x, g, tgt):
    T, D = x.shape

    def body(x_ref, g_ref, t_ref, loss_ref, dx_ref, dxb_ref, dg_ref):
        loss, vjp = jax.vjp(_loss_tile, x_ref[...], g_ref[...], t_ref[...])
        dx, dg, _ = vjp(jnp.ones((), F32))
        dx_ref[...] = dx
        dxb_ref[...] = dx.astype(BF16)

        @pl.when(pl.program_id(0) == 0)
        def _():
            dg_ref[...] = jnp.zeros_like(dg_ref)
            loss_ref[...] = jnp.zeros_like(loss_ref)

        dg_ref[...] += dg
        loss_ref[...] += jnp.reshape(loss, (1, 1))

    row = pl.BlockSpec((ROW_TILE, D), lambda i: (i, 0))
    return pl.pallas_call(
        body, name="loss_head", grid=(T // ROW_TILE,), in_specs=[row, _full((1, D)), row],
        out_specs=[_full((1, 1)), row, row, _full((1, D))],
        out_shape=[SDS((1, 1), F32), SDS((T, D), F32), SDS((T, D), BF16), SDS((1, D), F32)],
        compiler_params=_cp("arbitrary"))(x, g, tgt)


TILE_BYTES = 1 << 20


def _row_tile(rows, row_bytes):
    for cand in (512, 256, 128, 64, 32, 16, 8):
        if rows % cand == 0 and cand * row_bytes <= TILE_BYTES:
            return cand
    return rows


def adamw(w, g, m, v, *, name):
    R, C = w.shape
    tr = _row_tile(R, C * 4)

    def body(w_ref, g_ref, m_ref, v_ref, d_ref, nm_ref, nv_ref):
        gg = g_ref[...]
        mm = ADAM_B1 * m_ref[...] + (1.0 - ADAM_B1) * gg
        vv = ADAM_B2 * v_ref[...] + (1.0 - ADAM_B2) * jnp.square(gg)
        m_hat = mm / (1.0 - ADAM_B1 ** ADAM_STEP)
        v_hat = vv / (1.0 - ADAM_B2 ** ADAM_STEP)
        d_ref[...] = -ADAM_LR * (m_hat / (jnp.sqrt(v_hat) + ADAM_EPS) + ADAM_WD * w_ref[...])
        nm_ref[...] = mm
        nv_ref[...] = vv

    blk = pl.BlockSpec((tr, C), lambda i: (i, 0))
    return pl.pallas_call(body, name=name, grid=(R // tr,), in_specs=[blk] * 4, out_specs=[blk] * 3,
                          out_shape=[SDS((R, C), F32)] * 3, compiler_params=_cp("parallel"))(w, g, m, v)


def sum_leading(a, *, name, rider=None):
    n, R, C = a.shape
    tr = _row_tile(R, n * C * 4)

    def body(a_ref, o_ref):
        acc = a_ref[0]
        for j in range(1, n):
            acc = acc + a_ref[j]
        o_ref[...] = acc

    outs, rider_outs = _call(body, name=name, grid=(R // tr,), in_specs=[pl.BlockSpec((n, tr, C), lambda i: (0, i, 0))],
                             out_specs=[pl.BlockSpec((tr, C), lambda i: (i, 0))], out_shape=[SDS((R, C), F32)],
                             scratch_shapes=[], sem=("parallel",), args=(a,), rider=rider)
    return outs[0] if rider is None else (outs[0], rider_outs)


def chip_sum(pieces, from_sibling, core, *, name):
    _, n, R, C = pieces.shape

    def body(c_ref, a_ref, b_ref, o_ref):
        o_ref[...] = (a_ref[0] + b_ref[...]).astype(BF16)

    tr = _row_tile(R, C * 4)
    blk = pl.BlockSpec((1, tr, C), lambda k, i, c_ref: (k, i, 0))
    mine = pl.BlockSpec((1, 1, tr, C), lambda k, i, c_ref: (c_ref[0], k, i, 0))
    return pl.pallas_call(
        body, name=name, out_shape=SDS((n, R, C), BF16),
        grid_spec=pltpu.PrefetchScalarGridSpec(num_scalar_prefetch=1, grid=(n, R // tr), in_specs=[mine, blk], out_specs=blk),
        compiler_params=_cp("parallel", "parallel"))(core.reshape(1), pieces, from_sibling)


def total_sum(sums, landed, chip, core, *, name):
    n, R, C = landed.shape

    def body(k_ref, c_ref, s_ref, l_ref, o_ref):
        acc = s_ref[0].astype(F32)
        for j in range(n):
            acc = acc + l_ref[j].astype(F32)
        south = c_ref[0] == 0
        o_ref[0] = jnp.where(south, acc, 0.0)
        o_ref[1] = jnp.where(south, 0.0, acc)

    tr = _row_tile(R, n * C * 2)
    return pl.pallas_call(
        body, name=name, out_shape=SDS((2, R, C), F32),
        grid_spec=pltpu.PrefetchScalarGridSpec(
            num_scalar_prefetch=2, grid=(R // tr,),
            in_specs=[pl.BlockSpec((1, tr, C), lambda i, k_ref, c_ref: (k_ref[0], i, 0)),
                      pl.BlockSpec((n, tr, C), lambda i, k_ref, c_ref: (0, i, 0))],
            out_specs=pl.BlockSpec((2, tr, C), lambda i, k_ref, c_ref: (0, i, 0))),
        compiler_params=_cp("parallel"))(chip.reshape(1), core.reshape(1), sums, landed)


def _gmlp_chunk(u, v, z, ln_g, ln_b, wsc, bs_t):
    mu = jnp.mean(v, axis=-1, keepdims=True)
    xc = v - mu
    vn = xc * lax.rsqrt(jnp.mean(xc * xc, axis=-1, keepdims=True) + EPS) * ln_g + ln_b
    gw = A_WIDTH // A_GROUPS
    outs = []
    for g in range(A_GROUPS):
        m = _dot_nn(wsc[g].astype(BF16), vn[:, g * gw:(g + 1) * gw].astype(BF16))
        outs.append(m + bs_t[:, g:g + 1])
    return _silu(z) * (u * jnp.concatenate(outs, axis=1))


def _h_cols(width, idx, rows=CHUNK):
    return pl.BlockSpec((rows, width), lambda i: (i, idx))


def gmlp_fwd(h, ln_g, ln_b, ws, bs_t):
    T = h.shape[0]

    def body(u_ref, v_ref, z_ref, g_ref, b_ref, ws_ref, bs_ref, o_ref):
        wsc = jnp.where(_tril(CHUNK)[None], ws_ref[...], 0.0)
        o_ref[...] = _gmlp_chunk(u_ref[...], v_ref[...], z_ref[...], g_ref[...], b_ref[...], wsc, bs_ref[...]).astype(BF16)

    return pl.pallas_call(
        body, name="gmlp_fwd", grid=(T // CHUNK,),
        in_specs=[_h_cols(A_WIDTH, 0), _h_cols(A_WIDTH, 1), _h_cols(A_WIDTH, 2), _full((1, A_WIDTH)), _full((1, A_WIDTH)),
                  _full((A_GROUPS, CHUNK, CHUNK)), _full((CHUNK, A_GROUPS))],
        out_specs=_h_cols(A_WIDTH, 0), out_shape=SDS((T, A_WIDTH), BF16), compiler_params=_cp("parallel"),
    )(h, h, h, ln_g, ln_b, ws, bs_t)


def gmlp_bwd(h, dy, ln_g, ln_b, ws, bs_t):
    T = h.shape[0]

    def body(u_ref, v_ref, z_ref, dy_ref, g_ref, b_ref, ws_ref, bs_ref, duvz_ref, dg_ref, db_ref, dws_ref, dbs_ref):
        tri = _tril(CHUNK)[None]
        wsc = jnp.where(tri, ws_ref[...], 0.0)
        _, vjp = jax.vjp(_gmlp_chunk, u_ref[...], v_ref[...], z_ref[...], g_ref[...], b_ref[...], wsc, bs_ref[...])
        du, dv, dz, dg, db, dws, dbs = vjp(dy_ref[...])
        duvz_ref[:, :A_WIDTH] = du.astype(BF16)
        duvz_ref[:, A_WIDTH:2 * A_WIDTH] = dv.astype(BF16)
        duvz_ref[:, 2 * A_WIDTH:] = dz.astype(BF16)

        @pl.when(pl.program_id(0) == 0)
        def _():
            dg_ref[...] = jnp.zeros_like(dg_ref)
            db_ref[...] = jnp.zeros_like(db_ref)
            dws_ref[...] = jnp.zeros_like(dws_ref)
            dbs_ref[...] = jnp.zeros_like(dbs_ref)

        dg_ref[...] += dg
        db_ref[...] += db
        dws_ref[...] += jnp.where(tri, dws, 0.0)
        dbs_ref[...] += dbs

    pshapes = [(1, A_WIDTH), (1, A_WIDTH), (A_GROUPS, CHUNK, CHUNK), (CHUNK, A_GROUPS)]
    return pl.pallas_call(
        body, name="gmlp_bwd", grid=(T // CHUNK,),
        in_specs=[_h_cols(A_WIDTH, 0), _h_cols(A_WIDTH, 1), _h_cols(A_WIDTH, 2), _h_cols(A_WIDTH, 0)] + [_full(s) for s in pshapes],
        out_specs=[_h_cols(3 * A_WIDTH, 0)] + [_full(s) for s in pshapes],
        out_shape=[SDS((T, EVEN_MAIN), BF16)] + [SDS(s, F32) for s in pshapes],
        compiler_params=_cp("arbitrary"),
    )(h, h, h, dy, ln_g, ln_b, ws, bs_t)


CONV_ROWS = 256
CONV_COLS = 512


def _row8():
    return lax.broadcasted_iota(jnp.int32, (SUBLANES, 1), 0)


def _delayed(x, halo, s):
    if s == 0:
        return x
    r = pltpu.roll(x, s, axis=0)
    top = jnp.where(_row8() < s, pltpu.roll(halo, s, axis=0), r[:SUBLANES])
    return jnp.concatenate([top, r[SUBLANES:]], axis=0)


def _advanced(d, s):
    if s == 0:
        return d
    rows = d.shape[0]
    r = pltpu.roll(d, rows - s, axis=0)
    bottom = jnp.where(_row8() >= SUBLANES - s, 0.0, r[rows - SUBLANES:])
    return jnp.concatenate([r[:rows - SUBLANES], bottom], axis=0)


def _conv(x, halo, w):
    K = w.shape[0]
    acc = None
    for s in range(K):
        term = w[K - 1 - s:K - s, :] * _delayed(x, halo, s)
        acc = term if acc is None else acc + term
    return acc


def _conv_bwd(x, halo, w, d):
    K = w.shape[0]
    dx, dhalo, dws = None, None, [None] * K
    for s in range(K):
        wk = w[K - 1 - s:K - s, :]
        dws[K - 1 - s] = jnp.sum(d * _delayed(x, halo, s), axis=0, keepdims=True)
        term = wk * _advanced(d, s)
        dx = term if dx is None else dx + term
        if s:
            part = wk * jnp.where(_row8() >= SUBLANES - s, pltpu.roll(d[:SUBLANES], SUBLANES - s, axis=0), 0.0)
            dhalo = part if dhalo is None else dhalo + part
    return dx, dhalo, jnp.concatenate(dws, axis=0)


def _add_to_tail(d, carry):
    return jnp.concatenate([d[:d.shape[0] - SUBLANES], d[d.shape[0] - SUBLANES:] + carry], axis=0)


def _halo_spec(cols, col_idx, nt=None):
    rpb = CONV_ROWS // SUBLANES
    if nt is None:
        return pl.BlockSpec((SUBLANES, cols), lambda c, i: (jnp.maximum(i * rpb - 1, 0), col_idx(c)))
    return pl.BlockSpec((SUBLANES, cols), lambda c, j: (jnp.maximum((nt - 1 - j) * rpb - 1, 0), col_idx(c)))


def ssd_conv_fwd(h, w, b):
    T = h.shape[0]
    nc = B_XBC // CONV_COLS
    base = (3 * A_WIDTH + B_WIDTH) // CONV_COLS

    def body(x_ref, halo_ref, w_ref, b_ref, o_ref):
        halo = jnp.where(pl.program_id(1) > 0, halo_ref[...], 0.0)
        o_ref[...] = _silu(_conv(x_ref[...], halo, w_ref[...]) + b_ref[...])

    return pl.pallas_call(
        body, name="ssd_conv_fwd", grid=(nc, T // CONV_ROWS),
        in_specs=[pl.BlockSpec((CONV_ROWS, CONV_COLS), lambda c, i: (i, base + c)), _halo_spec(CONV_COLS, lambda c: base + c),
                  pl.BlockSpec((B_CONV, CONV_COLS), lambda c, i: (0, c)), pl.BlockSpec((1, CONV_COLS), lambda c, i: (0, c))],
        out_specs=pl.BlockSpec((CONV_ROWS, CONV_COLS), lambda c, i: (i, c)),
        out_shape=SDS((T, B_XBC), F32), compiler_params=_cp("parallel", "parallel"),
    )(h, h, w, b)


def ssd_conv_bwd(h, dy, w, b, dh, rider=None):
    T = h.shape[0]
    nc = B_XBC // CONV_COLS
    nt = T // CONV_ROWS
    base = (3 * A_WIDTH + B_WIDTH) // CONV_COLS

    def body(x_ref, halo_ref, dy_ref, w_ref, b_ref, dh_ref, dx_ref, dw_ref, db_ref, carry_ref):
        j = pl.program_id(1)
        halo = jnp.where(j < nt - 1, halo_ref[...], 0.0)
        x, w = x_ref[...], w_ref[...]
        pre = _conv(x, halo, w) + b_ref[...]
        dpre = dy_ref[...] * _dsilu(pre)
        dx, dhalo, dw = _conv_bwd(x, halo, w, dpre)

        @pl.when(j == 0)
        def _():
            carry_ref[...] = jnp.zeros_like(carry_ref)
            dw_ref[...] = jnp.zeros_like(dw_ref)
            db_ref[...] = jnp.zeros_like(db_ref)

        dx_ref[...] = _add_to_tail(dx, carry_ref[...]).astype(BF16)
        carry_ref[...] = dhalo
        dw_ref[...] += dw
        db_ref[...] += jnp.sum(dpre, axis=0, keepdims=True)

    outs, rider_outs = _call(
        body, name="ssd_conv_bwd", grid=(nc, nt),
        in_specs=[pl.BlockSpec((CONV_ROWS, CONV_COLS), lambda c, j: (nt - 1 - j, base + c)),
                  _halo_spec(CONV_COLS, lambda c: base + c, nt),
                  pl.BlockSpec((CONV_ROWS, CONV_COLS), lambda c, j: (nt - 1 - j, c)),
                  pl.BlockSpec((B_CONV, CONV_COLS), lambda c, j: (0, c)), pl.BlockSpec((1, CONV_COLS), lambda c, j: (0, c)), ANY],
        out_specs=[pl.BlockSpec((CONV_ROWS, CONV_COLS), lambda c, j: (nt - 1 - j, base + c)),
                   pl.BlockSpec((B_CONV, CONV_COLS), lambda c, j: (0, c)), pl.BlockSpec((1, CONV_COLS), lambda c, j: (0, c))],
        out_shape=[SDS(dh.shape, dh.dtype), SDS((B_CONV, B_XBC), F32), SDS((1, B_XBC), F32)],
        scratch_shapes=[pltpu.VMEM((SUBLANES, CONV_COLS), F32)], sem=("parallel", "arbitrary"),
        args=(h, h, dy, w, b, dh), rider=rider, aliases={5: 0})
    return outs if rider is None else (outs, rider_outs)


def sconv_fwd(h, w):
    T = h.shape[0]
    nc = C_WIDTH // CONV_COLS

    def col(seg):
        return pl.BlockSpec((CONV_ROWS, CONV_COLS), lambda c, i: (i, seg * nc + c))

    def body(bg_ref, cg_ref, hx_ref, z_ref, cgh_ref, hxh_ref, w_ref, o_ref):
        first = pl.program_id(1) == 0
        cgh = jnp.where(first, 0.0, cgh_ref[...])
        hxh = jnp.where(first, 0.0, hxh_ref[...])
        conv = _conv(cg_ref[...] * hx_ref[...], cgh * hxh, w_ref[...])
        o_ref[...] = (_silu(z_ref[...]) * (bg_ref[...] * conv)).astype(BF16)

    return pl.pallas_call(
        body, name="sconv_fwd", grid=(nc, T // CONV_ROWS),
        in_specs=[col(0), col(1), col(2), col(3), _halo_spec(CONV_COLS, lambda c: nc + c), _halo_spec(CONV_COLS, lambda c: 2 * nc + c),
                  pl.BlockSpec((C_CONV, CONV_COLS), lambda c, i: (0, c))],
        out_specs=pl.BlockSpec((CONV_ROWS, CONV_COLS), lambda c, i: (i, c)),
        out_shape=SDS((T, C_WIDTH), BF16), compiler_params=_cp("parallel", "parallel"),
    )(h, h, h, h, h, h, w)


def sconv_bwd(h, dy, w):
    T = h.shape[0]
    nc = C_WIDTH // CONV_COLS
    nt = T // CONV_ROWS

    def col(seg):
        return pl.BlockSpec((CONV_ROWS, CONV_COLS), lambda c, j: (nt - 1 - j, seg * nc + c))

    def body(bg_ref, cg_ref, hx_ref, z_ref, cgh_ref, hxh_ref, dy_ref, w_ref, dbg_ref, dcg_ref, dhx_ref, dz_ref, dw_ref, carry_ref):
        j = pl.program_id(1)
        first = j == nt - 1
        cgh = jnp.where(first, 0.0, cgh_ref[...])
        hxh = jnp.where(first, 0.0, hxh_ref[...])
        bg, cg, hx, z, w, dy = bg_ref[...], cg_ref[...], hx_ref[...], z_ref[...], w_ref[...], dy_ref[...]
        ch, ch_halo = cg * hx, cgh * hxh
        conv = _conv(ch, ch_halo, w)
        gated = dy * _silu(z)
        dch, dch_halo, dw = _conv_bwd(ch, ch_halo, w, gated * bg)

        @pl.when(j == 0)
        def _():
            carry_ref[...] = jnp.zeros_like(carry_ref)
            dw_ref[...] = jnp.zeros_like(dw_ref)

        dch = _add_to_tail(dch, carry_ref[...])
        dbg_ref[...] = (gated * conv).astype(BF16)
        dz_ref[...] = (dy * bg * conv * _dsilu(z)).astype(BF16)
        dcg_ref[...] = (dch * hx).astype(BF16)
        dhx_ref[...] = (dch * cg).astype(BF16)
        carry_ref[...] = dch_halo
        dw_ref[...] += dw

    out_row = pl.BlockSpec((CONV_ROWS, CONV_COLS), lambda c, j: (nt - 1 - j, c))
    wspec = pl.BlockSpec((C_CONV, CONV_COLS), lambda c, j: (0, c))
    return pl.pallas_call(
        body, name="sconv_bwd", grid=(nc, nt),
        in_specs=[col(0), col(1), col(2), col(3), _halo_spec(CONV_COLS, lambda c: nc + c, nt), _halo_spec(CONV_COLS, lambda c: 2 * nc + c, nt),
                  out_row, wspec],
        out_specs=[out_row] * 4 + [wspec],
        out_shape=[SDS((T, C_WIDTH), BF16)] * 4 + [SDS((C_CONV, C_WIDTH), F32)],
        scratch_shapes=[pltpu.VMEM((SUBLANES, CONV_COLS), F32)],
        compiler_params=_cp("parallel", "arbitrary"),
    )(h, h, h, h, h, h, dy, w)


def _softplus(x):
    return jnp.maximum(x, 0.0) + jnp.log(1.0 + jnp.exp(-jnp.abs(x)))


def _ssd_chunk(xs, bm, cm, dtr, z, prev, dt_bias, a_log, d_skip, norm_g):
    tril = _tril(CHUNK)
    dt = _softplus(dtr + dt_bias)
    adt = dt * (-jnp.exp(a_log))
    a_cs = jnp.dot(tril.astype(F32), adt, precision=lax.Precision.HIGHEST, preferred_element_type=F32)
    a_cs_t = a_cs.T
    a_last = a_cs[CHUNK - 1:CHUNK, :]
    dt_f = _spread_heads(dt, B_HEAD_DIM)
    dec_f = _spread_heads(jnp.exp(a_last - a_cs), B_HEAD_DIM)
    ecs_f = _spread_heads(jnp.exp(a_cs), B_HEAD_DIM)
    dsk_f = _spread_heads(d_skip, B_HEAD_DIM)
    cd_t = jnp.exp(a_cs_t[:, CHUNK - 1:CHUNK])
    xdt = xs * dt_f
    xdd = xdt * dec_f
    colb = _spread_heads(a_cs, CHUNK)
    rowb = jnp.concatenate([jnp.broadcast_to(a_cs_t[hh:hh + 1, :], (CHUNK, CHUNK)) for hh in range(B_HEADS)], axis=1)
    wide = (CHUNK, B_HEADS * CHUNK)
    keep = lax.broadcasted_iota(jnp.int32, wide, 0) >= lax.broadcasted_iota(jnp.int32, wide, 1) % CHUNK
    decay = jnp.exp(jnp.where(keep, colb - rowb, -jnp.inf))
    hpg = B_HEADS // B_GROUPS
    gw = B_WIDTH // B_GROUPS
    low_half = lax.broadcasted_iota(jnp.int32, (CHUNK, 2 * B_HEAD_DIM), 1) < B_HEAD_DIM
    ys, nxt = [], []
    for g in range(B_GROUPS):
        bg = bm[:, g * B_STATE:(g + 1) * B_STATE].astype(BF16)
        cg = cm[:, g * B_STATE:(g + 1) * B_STATE].astype(BF16)
        cb = _dot_nt(cg, bg)
        cbl = (decay[:, g * hpg * CHUNK:(g + 1) * hpg * CHUNK] * jnp.concatenate([cb] * hpg, axis=1)).astype(BF16)
        pg = prev[g * gw:(g + 1) * gw, :]
        y_off = _dot_nt(cg, pg.astype(BF16)) * ecs_f[:, g * gw:(g + 1) * gw]
        st = _dot_tn(xdd[:, g * gw:(g + 1) * gw].astype(BF16), bg)
        cd = jnp.concatenate([jnp.broadcast_to(cd_t[g * hpg + r:g * hpg + r + 1, :], (B_HEAD_DIM, 1)) for r in range(hpg)], axis=0)
        nxt.append(pg * cd + st)
        pairs = []
        for j in range(hpg // 2):
            xp = xdt[:, g * gw + 2 * j * B_HEAD_DIM:g * gw + 2 * (j + 1) * B_HEAD_DIM]
            rhs = jnp.concatenate([jnp.where(low_half, xp, 0.0), jnp.where(low_half, 0.0, xp)], axis=0).astype(BF16)
            pairs.append(_dot_nn(cbl[:, 2 * j * CHUNK:2 * (j + 1) * CHUNK], rhs))
        ys.append(jnp.concatenate(pairs, axis=1) + y_off)
    y = (jnp.concatenate(ys, axis=1) + dsk_f * xs) * _silu(z)
    outs = []
    for g in range(B_GROUPS):
        yg = y[:, g * gw:(g + 1) * gw]
        outs.append(yg * lax.rsqrt(jnp.mean(yg * yg, axis=-1, keepdims=True) + EPS))
    return jnp.concatenate(outs, axis=1) * norm_g, jnp.concatenate(nxt, axis=0)


def _split3(v):
    hi = v.astype(BF16)
    r1 = v - hi.astype(F32)
    mid = r1.astype(BF16)
    return hi, mid, (r1 - mid.astype(F32)).astype(BF16)


def _head_one_hot(width, parts):
    n = B_HEADS * width
    shape = (parts * LANES, n)
    return (lax.broadcasted_iota(jnp.int32, shape, 0) % LANES == lax.broadcasted_iota(jnp.int32, shape, 1) // width).astype(BF16)


@functools.partial(jax.custom_vjp, nondiff_argnums=(1,))
def _spread_heads(v, width):
    return _dot_nn(jnp.concatenate(_split3(v), axis=1), _head_one_hot(width, 3))


def _spread_heads_fwd(v, width):
    return _spread_heads(v, width), None


def _spread_heads_bwd(width, _, g):
    return (_dot_nt(jnp.concatenate(_split3(g), axis=1), jnp.concatenate([_head_one_hot(width, 1)] * 3, axis=1)),)


_spread_heads.defvjp(_spread_heads_fwd, _spread_heads_bwd)


_SSD_PARAM_SHAPES = [(1, LANES), (1, LANES), (1, LANES), (1, B_WIDTH)]
_STATE_SHAPE = (B_WIDTH, B_STATE)


def ssd_fwd(xbc, dtr, h, dt_bias, a_log, d_skip, norm_g, rider=None):
    T = xbc.shape[0]
    nc = T // CHUNK

    def body(xs_ref, b_ref, c_ref, dt_ref, z_ref, p0, p1, p2, p3, y_ref, st_ref, state):
        @pl.when(pl.program_id(0) == 0)
        def _():
            state[...] = jnp.zeros_like(state)

        prev = state[...]
        st_ref[0] = prev
        yb, nxt = _ssd_chunk(xs_ref[...], b_ref[...], c_ref[...], dt_ref[...], z_ref[...], prev, p0[...], p1[...], p2[...], p3[...])
        y_ref[...] = yb.astype(BF16)
        state[...] = nxt

    outs, rider_outs = _call(
        body, name="ssd_fwd", grid=(nc,),
        in_specs=[_h_cols(B_WIDTH, 0), _h_cols(B_GROUPS * B_STATE, 2), _h_cols(B_GROUPS * B_STATE, 3), _h_cols(LANES, 0), _h_cols(B_WIDTH, 3)]
        + [_full(s) for s in _SSD_PARAM_SHAPES],
        out_specs=[_h_cols(B_WIDTH, 0), pl.BlockSpec((1,) + _STATE_SHAPE, lambda i: (i, 0, 0))],
        out_shape=[SDS((T, B_WIDTH), BF16), SDS((nc,) + _STATE_SHAPE, F32)],
        scratch_shapes=[pltpu.VMEM(_STATE_SHAPE, F32)], sem=("arbitrary",),
        args=(xbc, xbc, xbc, dtr, h, dt_bias, a_log, d_skip, norm_g), rider=rider)
    return outs if rider is None else (outs, rider_outs)


def ssd_bwd(xbc, dtr, h, states, dy, dt_bias, a_log, d_skip, norm_g, dh, rider=None):
    T = xbc.shape[0]
    nc = T // CHUNK

    def rev(width, idx):
        return pl.BlockSpec((CHUNK, width), lambda j: (nc - 1 - j, idx))

    def body(xs_ref, b_ref, c_ref, dt_ref, z_ref, st_ref, dy_ref, p0, p1, p2, p3, dh_ref,
             dxbc_ref, ddt_ref, dz_ref, g0, g1, g2, g3, dstate):
        @pl.when(pl.program_id(0) == 0)
        def _():
            dstate[...] = jnp.zeros_like(dstate)
            for gref in (g0, g1, g2, g3):
                gref[...] = jnp.zeros_like(gref)

        _, vjp = jax.vjp(_ssd_chunk, xs_ref[...], b_ref[...], c_ref[...], dt_ref[...], z_ref[...], st_ref[0],
                         p0[...], p1[...], p2[...], p3[...])
        dxs, dbm, dcm, ddt, dz, dprev, d0, d1, d2, d3 = vjp((dy_ref[...], dstate[...]))
        dxbc_ref[:, :B_WIDTH] = dxs
        dxbc_ref[:, B_WIDTH:B_WIDTH + gn] = dbm
        dxbc_ref[:, B_WIDTH + gn:] = dcm
        ddt_ref[...] = ddt.astype(BF16)
        dz_ref[...] = dz.astype(BF16)
        dstate[...] = dprev
        g0[...] += d0
        g1[...] += d1
        g2[...] += d2
        g3[...] += d3

    gn = B_GROUPS * B_STATE
    outs, rider_outs = _call(
        body, name="ssd_bwd", grid=(nc,),
        in_specs=[rev(B_WIDTH, 0), rev(gn, 2), rev(gn, 3), rev(LANES, 0), rev(B_WIDTH, 3),
                  pl.BlockSpec((1,) + _STATE_SHAPE, lambda j: (nc - 1 - j, 0, 0)), rev(B_WIDTH, 1)]
        + [_full(s) for s in _SSD_PARAM_SHAPES] + [ANY],
        out_specs=[rev(B_XBC, 0), rev(LANES, 0), rev(B_WIDTH, 3)] + [_full(s) for s in _SSD_PARAM_SHAPES],
        out_shape=[SDS((T, B_XBC), F32), SDS((T, LANES), BF16), SDS(dh.shape, dh.dtype)]
        + [SDS(s, F32) for s in _SSD_PARAM_SHAPES],
        scratch_shapes=[pltpu.VMEM(_STATE_SHAPE, F32)], sem=("arbitrary",),
        args=(xbc, xbc, xbc, dtr, h, states, dy, dt_bias, a_log, d_skip, norm_g, dh), rider=rider, aliases={11: 2})
    return outs if rider is None else (outs, rider_outs)


ATT_SCALE = D_HEAD_DIM ** -0.5
Q_COL, K_COL, V_COL, Z_COL = (4 * C_WIDTH // LANES + i * D_HEADS for i in range(4))


ATT_NBLK = ATT_SUPER // ATT_BLOCK


def _res_rows(r, first, count, dil):
    return pl.ds(r + dil * first, count) if dil == 1 else pl.ds(r + dil * first, count, stride=dil)


def _blocks(ref, dil, dtype=None):
    n = ATT_SUPER // dil
    parts = []
    for r in range(dil):
        v = ref[_res_rows(r, 0, n, dil), :]
        parts.append((v if dtype is None else v.astype(dtype)).reshape(n // ATT_BLOCK, ATT_BLOCK, D_HEAD_DIM))
    return parts[0] if dil == 1 else jnp.concatenate(parts, axis=0)


def _blocks_before(cur_blocks, prev_ref, dil, dtype):
    n = ATT_SUPER // dil
    nb = n // ATT_BLOCK
    parts = []
    for r in range(dil):
        edge = prev_ref[_res_rows(r, n - ATT_BLOCK, ATT_BLOCK, dil), :].astype(dtype)
        parts.append(edge.reshape(1, ATT_BLOCK, D_HEAD_DIM))
        if nb > 1:
            parts.append(cur_blocks[r * nb:(r + 1) * nb - 1])
    return jnp.concatenate(parts, axis=0)


def _unblock(ref, val, dil, add=False):
    n = ATT_SUPER // dil
    nb = n // ATT_BLOCK
    for r in range(dil):
        v = val[r * nb:(r + 1) * nb].reshape(n, D_HEAD_DIM)
        if add:
            ref[_res_rows(r, 0, n, dil), :] += v
        else:
            ref[_res_rows(r, 0, n, dil), :] = v


def _att_masks(dil, edge_ok):
    shape = (ATT_NBLK, ATT_BLOCK, ATT_BLOCK)
    blk = lax.broadcasted_iota(jnp.int32, shape, 0)
    row = lax.broadcasted_iota(jnp.int32, shape, 1)
    col = lax.broadcasted_iota(jnp.int32, shape, 2)
    nb = ATT_NBLK // dil
    at_edge = (blk % nb) == 0
    return col <= row, jnp.logical_and(col >= row, jnp.logical_or(jnp.logical_not(at_edge), edge_ok))


def _bdot_nt(a, b):
    return lax.dot_general(a, b, (((2,), (2,)), ((0,), (0,))), preferred_element_type=F32)


def _bdot_nn(a, b):
    return lax.dot_general(a, b, (((2,), (1,)), ((0,), (0,))), preferred_element_type=F32)


def _bdot_tn(a, b):
    return lax.dot_general(a, b, (((1,), (1,)), ((0,), (0,))), preferred_element_type=F32)


def _att_spec(col0, shift=0, last=None):
    def imap(hh, n):
        m = n + shift
        if shift < 0:
            m = jnp.maximum(m, 0)
        if shift > 0:
            m = jnp.minimum(m, last)
        return (m, col0 + hh)
    return pl.BlockSpec((ATT_SUPER, D_HEAD_DIM), imap)


def _att_out_spec():
    return pl.BlockSpec((ATT_SUPER, D_HEAD_DIM), lambda hh, n: (n, hh))


def attn_fwd(h):
    T = h.shape[0]
    npat = len(D_PATTERNS)

    def body(q_ref, kc_ref, kp_ref, vc_ref, vp_ref, z_ref, yd_ref, o_ref, lse_ref, *scratch):
        o_s, l_s = scratch[:npat], scratch[npat:]
        has_prev = pl.program_id(1) > 0
        for pi, (_, dil) in enumerate(D_PATTERNS):
            mask_c, mask_p = _att_masks(dil, has_prev)
            q = _blocks(q_ref, dil, BF16)
            kc, vc = _blocks(kc_ref, dil, BF16), _blocks(vc_ref, dil, BF16)
            kp, vp = _blocks_before(kc, kp_ref, dil, BF16), _blocks_before(vc, vp_ref, dil, BF16)
            s_c = jnp.where(mask_c, _bdot_nt(q, kc) * ATT_SCALE, -jnp.inf)
            s_p = jnp.where(mask_p, _bdot_nt(q, kp) * ATT_SCALE, -jnp.inf)
            m = jnp.maximum(jnp.max(s_c, axis=-1, keepdims=True), jnp.max(s_p, axis=-1, keepdims=True))
            p_c = jnp.exp(s_c - m)
            p_p = jnp.exp(s_p - m)
            l = jnp.sum(p_c, axis=-1, keepdims=True) + jnp.sum(p_p, axis=-1, keepdims=True)
            o = _bdot_nn((p_c / l).astype(BF16), vc) + _bdot_nn((p_p / l).astype(BF16), vp)
            _unblock(o_s[pi], o, dil)
            _unblock(l_s[pi], jnp.broadcast_to(m + jnp.log(l), o.shape), dil)
        lses = [l_s[pi][...] for pi in range(npat)]
        mx = functools.reduce(jnp.maximum, lses)
        ws = [jnp.exp(l - mx) for l in lses]
        den = functools.reduce(lambda a, b: a + b, ws)
        o = functools.reduce(lambda a, b: a + b, [(w / den) * o_s[pi][...] for pi, w in enumerate(ws)])
        o_ref[...] = o
        lse_ref[...] = mx + jnp.log(den)
        yd_ref[...] = (_silu(z_ref[...]) * o).astype(BF16)

    n_super = T // ATT_SUPER
    return pl.pallas_call(
        body, name="attn_fwd", grid=(D_HEADS, n_super),
        in_specs=[_att_spec(Q_COL), _att_spec(K_COL), _att_spec(K_COL, -1), _att_spec(V_COL), _att_spec(V_COL, -1), _att_spec(Z_COL)],
        out_specs=[_att_out_spec()] * 3,
        out_shape=[SDS((T, D_HEADS * D_HEAD_DIM), BF16), SDS((T, D_HEADS * D_HEAD_DIM), F32), SDS((T, D_HEADS * D_HEAD_DIM), F32)],
        scratch_shapes=[pltpu.VMEM((ATT_SUPER, D_HEAD_DIM), F32)] * (2 * npat),
        compiler_params=_cp("parallel", "arbitrary"),
    )(h, h, h, h, h, h)


def _dsilu(z):
    s = jax.nn.sigmoid(z)
    return s * (1.0 + z * (1.0 - s))


def attn_bwd(h, o, lse, dy):
    T = h.shape[0]
    last = T // ATT_SUPER - 1
    dy_col = C_WIDTH // LANES

    def spec(col0, shift=0):
        return pl.BlockSpec((ATT_SUPER, D_HEAD_DIM), lambda hh, j: (jnp.maximum(last - j + shift, 0), col0 + hh))

    def add_before(ref, carry_ref, val, dil):
        n = ATT_SUPER // dil
        nb = n // ATT_BLOCK
        for r in range(dil):
            carry_ref[_res_rows(r, n - ATT_BLOCK, ATT_BLOCK, dil), :] += val[r * nb]
            if nb > 1:
                ref[_res_rows(r, 0, n - ATT_BLOCK, dil), :] += val[r * nb + 1:(r + 1) * nb].reshape(n - ATT_BLOCK, D_HEAD_DIM)

    def body(q_ref, kc_ref, kp_ref, vc_ref, vp_ref, z_ref, o_ref, lse_ref, dy_ref, dq_ref, dk_ref, dv_ref, dz_ref,
             do_s, dd_s, dq_s, dk_s, dv_s, dk_carry, dv_carry):
        j = pl.program_id(1)
        has_prev = j < last

        @pl.when(j == 0)
        def _():
            dk_carry[...] = jnp.zeros_like(dk_carry)
            dv_carry[...] = jnp.zeros_like(dv_carry)

        dk_s[...] = dk_carry[...]
        dv_s[...] = dv_carry[...]
        dk_carry[...] = jnp.zeros_like(dk_carry)
        dv_carry[...] = jnp.zeros_like(dv_carry)
        z, oo, dyd = z_ref[...], o_ref[...], dy_ref[...]
        do = dyd * _silu(z)
        dz_ref[...] = (dyd * oo * _dsilu(z)).astype(BF16)
        do_s[...] = do
        dd_s[...] = jnp.broadcast_to(jnp.sum(do * oo, axis=-1, keepdims=True), (ATT_SUPER, D_HEAD_DIM))
        for pi, (_, dil) in enumerate(D_PATTERNS):
            mask_c, mask_p = _att_masks(dil, has_prev)
            q = _blocks(q_ref, dil, BF16)
            kc, vc = _blocks(kc_ref, dil, BF16), _blocks(vc_ref, dil, BF16)
            kp, vp = _blocks_before(kc, kp_ref, dil, BF16), _blocks_before(vc, vp_ref, dil, BF16)
            lse_b, dd_b, do_b = _blocks(lse_ref, dil), _blocks(dd_s, dil), _blocks(do_s, dil, BF16)
            p_c = jnp.where(mask_c, jnp.exp(_bdot_nt(q, kc) * ATT_SCALE - lse_b), 0.0)
            p_p = jnp.where(mask_p, jnp.exp(_bdot_nt(q, kp) * ATT_SCALE - lse_b), 0.0)
            ds_c = (p_c * (_bdot_nt(do_b, vc) - dd_b) * ATT_SCALE).astype(BF16)
            ds_p = (p_p * (_bdot_nt(do_b, vp) - dd_b) * ATT_SCALE).astype(BF16)
            _unblock(dq_s, _bdot_nn(ds_c, kc) + _bdot_nn(ds_p, kp), dil, add=pi > 0)
            _unblock(dk_s, _bdot_tn(ds_c, q), dil, add=True)
            _unblock(dv_s, _bdot_tn(p_c.astype(BF16), do_b), dil, add=True)
            add_before(dk_s, dk_carry, _bdot_tn(ds_p, q), dil)
            add_before(dv_s, dv_carry, _bdot_tn(p_p.astype(BF16), do_b), dil)
        dq_ref[...] = dq_s[...].astype(BF16)
        dk_ref[...] = dk_s[...].astype(BF16)
        dv_ref[...] = dv_s[...].astype(BF16)

    blk = (ATT_SUPER, D_HEAD_DIM)
    out = pl.BlockSpec(blk, lambda hh, j: (last - j, hh))
    return pl.pallas_call(
        body, name="attn_bwd", grid=(D_HEADS, last + 1),
        in_specs=[spec(Q_COL), spec(K_COL), spec(K_COL, -1), spec(V_COL), spec(V_COL, -1), spec(Z_COL), spec(0), spec(0), spec(dy_col)],
        out_specs=[out] * 4, out_shape=[SDS((T, D_HEADS * D_HEAD_DIM), BF16)] * 4,
        scratch_shapes=[pltpu.VMEM(blk, F32)] * 7, compiler_params=_cp("parallel", "arbitrary"),
    )(h, h, h, h, h, h, o, lse, dy)


ANY = pl.BlockSpec(memory_space=pl.ANY)
COMM_PARAMS = pltpu.CompilerParams()


def _place():
    x, y, c = lax.axis_index("x"), lax.axis_index("y"), lax.axis_index("c")
    return x, y, c, [(1 - x, y), (x, 1 - y), (1 - x, 1 - y)]


def _rcopy(src, dst, ssem, rsem, dev):
    return pltpu.make_async_remote_copy(src_ref=src, dst_ref=dst, send_sem=ssem, recv_sem=rsem, device_id=dev, device_id_type=MESH)


def gather_rider(arrs, fractions=(0.0, 0.6, 1.0)):
    n = len(arrs)
    per = 7

    def to_chips(ins, outs, ssem, rsem):
        x, y, c, chips = _place()
        return [_rcopy(ins[a].at[c], outs[a].at[2 * x + y, c], ssem.at[per * a + j], rsem.at[per * a + j], (px, py, c))
                for a in range(n) for j, (px, py) in enumerate(chips)]

    def passed_on(outs, ssem, rsem, half):
        x, y, c, chips = _place()
        cps = []
        for a in range(n):
            for j, (px, py) in enumerate(chips):
                slot = outs[a].at[2 * px + py, half(c)]
                cps.append(_rcopy(slot, slot, ssem.at[per * a + 3 + j], rsem.at[per * a + 3 + j], (x, y, 1 - c)))
        return cps

    def own(ins, outs, ssem, rsem):
        x, y, c, _ = _place()
        return [_rcopy(ins[a], outs[a].at[2 * x + y], ssem.at[per * a + 6], rsem.at[per * a + 6], (x, y, 1 - c)) for a in range(n)]

    def start(ins, outs, ssem, rsem):
        for cp in to_chips(ins, outs, ssem, rsem) + own(ins, outs, ssem, rsem):
            cp.start()

    def pass_on(ins, outs, ssem, rsem):
        x, y, c, chips = _place()
        landed = [_rcopy(outs[a].at[2 * px + py, c], outs[a].at[2 * px + py, c], ssem.at[per * a + j], rsem.at[per * a + j], (px, py, c))
                  for a in range(n) for j, (px, py) in enumerate(chips)]
        for arrival, cp in zip(landed, passed_on(outs, ssem, rsem, lambda c: c)):
            arrival.wait_recv()
            cp.start()

    def finish(ins, outs, ssem, rsem):
        for cp in passed_on(outs, ssem, rsem, lambda c: 1 - c):
            cp.wait_recv()
        for cp in to_chips(ins, outs, ssem, rsem) + passed_on(outs, ssem, rsem, lambda c: c):
            cp.wait_send()
        for cp in own(ins, outs, ssem, rsem):
            cp.wait()

    return Rider(arrs, [SDS((N_CHIPS,) + a.shape, a.dtype) for a in arrs], per * n,
                 [(fractions[0], start), (fractions[1], pass_on), (fractions[2], finish)])


def _copies_rider(ins, out_shapes, n_sems, make):
    def start(*refs):
        for cp in make(*refs):
            cp.start()

    def finish(*refs):
        for cp in make(*refs):
            cp.wait()

    return Rider(ins, out_shapes, n_sems, [(0.0, start), (1.0, finish)])


def swap_halves_rider(arrs):
    def make(ins, outs, ssem, rsem):
        x, y, c, _ = _place()
        return [_rcopy(ins[a].at[1 - c], outs[a], ssem.at[a], rsem.at[a], (x, y, 1 - c)) for a in range(len(arrs))]
    return _copies_rider(arrs, [SDS(a.shape[1:], a.dtype) for a in arrs], len(arrs), make)


def scatter_rider(arrs):
    def make(ins, outs, ssem, rsem):
        x, y, c, chips = _place()
        return [_rcopy(ins[a].at[2 * px + py], outs[a].at[j], ssem.at[3 * a + j], rsem.at[3 * a + j], (px, py, c))
                for a in range(len(arrs)) for j, (px, py) in enumerate(chips)]
    return _copies_rider(arrs, [SDS((N_CHIPS - 1,) + a.shape[1:], a.dtype) for a in arrs], 3 * len(arrs), make)


def join_halves_rider(arrs):
    def make(ins, outs, ssem, rsem):
        x, y, c, _ = _place()
        return [_rcopy(ins[a].at[c], outs[a].at[c], ssem.at[a], rsem.at[a], (x, y, 1 - c)) for a in range(len(arrs))]

    def start(*refs):
        for cp in make(*refs):
            cp.start()

    def finish(ins, outs, ssem, rsem):
        x, y, c, _ = _place()
        for a, cp in enumerate(make(ins, outs, ssem, rsem)):
            cp.wait_send()
            _rcopy(ins[a].at[1 - c], outs[a].at[1 - c], ssem.at[a], rsem.at[a], (x, y, 1 - c)).wait_recv()

    return Rider(arrs, [SDS(a.shape, a.dtype) for a in arrs], len(arrs), [(0.0, start), (1.0, finish)], in_place=True)


def gather_all(buf):
    def body(in_ref, out_ref, ssem, rsem, lsem):
        x, y, c, _ = _place()
        me = 4 * x + 2 * y + c
        local = pltpu.make_async_copy(in_ref, out_ref.at[me], lsem)
        local.start()
        flips = [(a, b, e) for a in (0, 1) for b in (0, 1) for e in (0, 1)][1:]
        cps = []
        for i, (a, b, e) in enumerate(flips):
            peer = (x ^ a, y ^ b, c ^ e)
            cps.append(_rcopy(in_ref, out_ref.at[me], ssem.at[i], rsem.at[i], peer))
        for cp in cps:
            cp.start()
        for i, (a, b, e) in enumerate(flips):
            cps[i].wait_send()
            slot = out_ref.at[4 * (x ^ a) + 2 * (y ^ b) + (c ^ e)]
            _rcopy(slot, slot, ssem.at[i], rsem.at[i], (x ^ a, y ^ b, c ^ e)).wait_recv()
        local.wait()

    return pl.pallas_call(
        body, name="comm_gather_all", in_specs=[ANY], out_specs=ANY, out_shape=SDS((N_DEV,) + buf.shape, buf.dtype),
        scratch_shapes=[pltpu.SemaphoreType.DMA((N_DEV - 1,)), pltpu.SemaphoreType.DMA((N_DEV - 1,)), pltpu.SemaphoreType.DMA],
        compiler_params=COMM_PARAMS,
    )(buf)


def _pack_offsets(parts):
    offs, r = [], 0
    for p in parts:
        offs.append(r)
        r += -(-p.shape[0] // SUBLANES) * SUBLANES
    return offs, r


def pack_rows(parts):
    offs, total = _pack_offsets(parts)

    def body(*refs):
        out = refs[-1]
        out[...] = jnp.zeros_like(out)
        for ref, off in zip(refs[:-1], offs):
            out[off:off + ref.shape[0], :] = ref[...]

    vmem = pl.BlockSpec(memory_space=pltpu.VMEM)
    return pl.pallas_call(body, name="pack_small", in_specs=[vmem] * len(parts), out_specs=vmem,
                          out_shape=SDS((total, LANES), F32))(*parts)


EVEN_SHARD = IN_EVEN // N_CHIPS


def wie_from_shards(g):
    tr = 256

    def body(g_ref, main_ref, dt_ref):
        full = jnp.concatenate([g_ref[k] for k in range(N_CHIPS)], axis=1)
        main_ref[...] = full[:, :EVEN_MAIN]
        dt_ref[...] = jnp.concatenate([full[:, EVEN_MAIN:], jnp.zeros((tr, LANES - B_HEADS), full.dtype)], axis=1)

    return pl.pallas_call(
        body, name="wie_from_shards", grid=(D_MODEL // tr,),
        in_specs=[pl.BlockSpec((N_CHIPS, tr, EVEN_SHARD), lambda i: (0, i, 0))],
        out_specs=[pl.BlockSpec((tr, EVEN_MAIN), lambda i: (i, 0)), pl.BlockSpec((tr, LANES), lambda i: (i, 0))],
        out_shape=[SDS((D_MODEL, EVEN_MAIN), g.dtype), SDS((D_MODEL, LANES), g.dtype)], compiler_params=_cp("parallel"))(g)


def wie_grad_to_pieces(main, dt):
    tr = 128
    per_half = D_MODEL // 2 // tr

    def body(m_ref, d_ref, o_ref):
        full = jnp.concatenate([m_ref[...], d_ref[:, :B_HEADS]], axis=1)
        for k in range(N_CHIPS):
            o_ref[0, k] = full[:, k * EVEN_SHARD:(k + 1) * EVEN_SHARD]

    return pl.pallas_call(
        body, name="wie_grad_to_pieces", grid=(2, per_half),
        in_specs=[pl.BlockSpec((tr, EVEN_MAIN), lambda c, i: (c * per_half + i, 0)), pl.BlockSpec((tr, LANES), lambda c, i: (c * per_half + i, 0))],
        out_specs=pl.BlockSpec((1, N_CHIPS, tr, EVEN_SHARD), lambda c, i: (c, 0, i, 0)),
        out_shape=SDS((2, N_CHIPS, D_MODEL // 2, EVEN_SHARD), F32), compiler_params=_cp("parallel", "parallel"))(main, dt)


def _unpack(buf, parts):
    offs, _ = _pack_offsets(parts)
    return [buf[off:off + p.shape[0]] for p, off in zip(parts, offs)]


def _pad_lanes(v):
    v = v.reshape(1, -1)
    return jnp.pad(v, ((0, 0), (0, LANES - v.shape[1])))


def _as2d(a):
    return a.reshape(1, -1) if a.ndim == 1 else a.reshape(-1, a.shape[-1])


def kernel(x, even_norm_g, even_w_in, gmlp_ln_g, gmlp_ln_b, gmlp_ws, gmlp_bs, ssd_conv_w, ssd_conv_b, ssd_dt_bias, ssd_a_log, ssd_d, ssd_norm_g, even_w_out, odd_norm_g, odd_w_in, sconv_w, odd_w_out, final_norm_g, loss_target, m_even_norm_g, m_even_w_in, m_gmlp_ln_g, m_gmlp_ln_b, m_gmlp_ws, m_gmlp_bs, m_ssd_conv_w, m_ssd_conv_b, m_ssd_dt_bias, m_ssd_a_log, m_ssd_d, m_ssd_norm_g, m_even_w_out, m_odd_norm_g, m_odd_w_in, m_sconv_w, m_odd_w_out, m_final_norm_g, v_even_norm_g, v_even_w_in, v_gmlp_ln_g, v_gmlp_ln_b, v_gmlp_ws, v_gmlp_bs, v_ssd_conv_w, v_ssd_conv_b, v_ssd_dt_bias, v_ssd_a_log, v_ssd_d, v_ssd_norm_g, v_even_w_out, v_odd_norm_g, v_odd_w_in, v_sconv_w, v_odd_w_out, v_final_norm_g):
    weights = dict(even_norm_g=even_norm_g, even_w_in=even_w_in, gmlp_ln_g=gmlp_ln_g, gmlp_ln_b=gmlp_ln_b, gmlp_ws=gmlp_ws, gmlp_bs=gmlp_bs, ssd_conv_w=ssd_conv_w, ssd_conv_b=ssd_conv_b, ssd_dt_bias=ssd_dt_bias, ssd_a_log=ssd_a_log, ssd_d=ssd_d, ssd_norm_g=ssd_norm_g, even_w_out=even_w_out, odd_norm_g=odd_norm_g, odd_w_in=odd_w_in, sconv_w=sconv_w, odd_w_out=odd_w_out, final_norm_g=final_norm_g)
    moms_m = dict(even_norm_g=m_even_norm_g, even_w_in=m_even_w_in, gmlp_ln_g=m_gmlp_ln_g, gmlp_ln_b=m_gmlp_ln_b, gmlp_ws=m_gmlp_ws, gmlp_bs=m_gmlp_bs, ssd_conv_w=m_ssd_conv_w, ssd_conv_b=m_ssd_conv_b, ssd_dt_bias=m_ssd_dt_bias, ssd_a_log=m_ssd_a_log, ssd_d=m_ssd_d, ssd_norm_g=m_ssd_norm_g, even_w_out=m_even_w_out, odd_norm_g=m_odd_norm_g, odd_w_in=m_odd_w_in, sconv_w=m_sconv_w, odd_w_out=m_odd_w_out, final_norm_g=m_final_norm_g)
    moms_v = dict(even_norm_g=v_even_norm_g, even_w_in=v_even_w_in, gmlp_ln_g=v_gmlp_ln_g, gmlp_ln_b=v_gmlp_ln_b, gmlp_ws=v_gmlp_ws, gmlp_bs=v_gmlp_bs, ssd_conv_w=v_ssd_conv_w, ssd_conv_b=v_ssd_conv_b, ssd_dt_bias=v_ssd_dt_bias, ssd_a_log=v_ssd_a_log, ssd_d=v_ssd_d, ssd_norm_g=v_ssd_norm_g, even_w_out=v_even_w_out, odd_norm_g=v_odd_norm_g, odd_w_in=v_odd_w_in, sconv_w=v_sconv_w, odd_w_out=v_odd_w_out, final_norm_g=v_final_norm_g)
    names = list(weights)

    xs = x[0]
    tgt = loss_target[0]
    T = xs.shape[0]
    chip = 2 * lax.axis_index("x") + lax.axis_index("y")
    core = lax.axis_index("c")
    cshard = B_XBC // N_CHIPS
    dshard = D_MODEL // N_CHIPS

    def halves(w):
        return w.astype(BF16).reshape(2, w.shape[0] // 2, w.shape[1])

    small_shard = jnp.concatenate([ssd_conv_w[0].reshape(-1), odd_norm_g[0], sconv_w[0].reshape(-1)])
    g_wie, g_small = run_rider(gather_rider([halves(even_w_in[0]), small_shard.reshape(2, -1, LANES)]), name="comm_gather_first")
    wie_main, wie_dt = wie_from_shards(g_wie.reshape(N_CHIPS, D_MODEL, EVEN_SHARD))
    g_small = g_small.reshape(N_CHIPS, -1)
    n_cw = B_CONV * cshard
    conv_w = g_small[:, :n_cw].reshape(N_CHIPS, B_CONV, cshard).transpose(1, 0, 2).reshape(B_CONV, B_XBC)
    odd_g = g_small[:, n_cw:n_cw + dshard].reshape(1, D_MODEL)
    sconv = g_small[:, n_cw + dshard:].reshape(N_CHIPS, C_CONV, dshard).transpose(1, 0, 2).reshape(C_CONV, C_WIDTH)

    even_g = even_norm_g
    ln_g, ln_b = gmlp_ln_g, gmlp_ln_b
    ws, bs_t = gmlp_ws[0], gmlp_bs[0].T
    conv_b = ssd_conv_b
    dt_bias, a_log, d_skip = _pad_lanes(ssd_dt_bias), _pad_lanes(ssd_a_log), _pad_lanes(ssd_d)
    norm_g = ssd_norm_g
    fin_g = final_norm_g.reshape(1, D_MODEL)

    xn0 = rmsnorm_fwd(xs, even_g, name="even_norm")
    h0, (g_wio,) = matmul(xn0, wie_main, "nn", name="even_in", rider=gather_rider([halves(odd_w_in[0])], (0.0, 0.88, 1.0)))
    wio = g_wio.reshape(N_CHIPS, D_MODEL, IN_ODD // N_CHIPS)
    dtr = matmul(xn0, wie_dt, "nn", name="even_in_dt", tk=D_MODEL)
    ya = gmlp_fwd(h0, ln_g, ln_b, ws, bs_t)
    xbc = ssd_conv_fwd(h0, conv_w, conv_b)
    (yb, states), (g_woe, g_woo) = ssd_fwd(xbc, dtr, h0, dt_bias, a_log, d_skip, norm_g,
                                           rider=gather_rider([halves(even_w_out[0]), halves(odd_w_out[0])]))
    woe = g_woe.reshape(2 * A_WIDTH, D_MODEL)
    woo = g_woo.reshape(2 * C_WIDTH, D_MODEL)
    y0 = [ya, yb]
    x1 = matmul(y0, woe, "nn", name="even_out", res=xs)

    xn1 = rmsnorm_fwd(x1, odd_g, name="odd_norm")
    h1 = matmul(xn1, wio, "nn", name="odd_in")
    yc = sconv_fwd(h1, sconv)
    yd, att_o, att_lse = attn_fwd(h1)
    y1 = [yc, yd]
    x2 = matmul(y1, woo, "nn", name="odd_out", res=x1)

    loss_part, dx2, dx2b, d_fin_g = loss_head(x2, fin_g, tgt)

    tile = 1024
    rows_layout = ((2, N_CHIPS, ROW_PIECE, D_MODEL), (1, 1, ROW_PIECE, tile), lambda i, j, k: (i % 2, i // 2, 0, j))
    per_chip = IN_ODD // N_CHIPS // tile
    cols_layout = ((2, N_CHIPS, D_MODEL // 2, IN_ODD // N_CHIPS), (1, 1, tile, tile), lambda i, j, k: (i, j // per_chip, 0, j % per_chip))
    dy1 = matmul(dx2b, woo, "nt", name="odd_out_dy")
    d_woo = matmul(y1, dx2b, "tn", name="odd_out_dw", tm=ROW_PIECE, out_layout=rows_layout)
    dbg, dcg, dhx, dzc, d_sconv = sconv_bwd(h1, dy1, sconv)
    dq, dk, dv, dzd = attn_bwd(h1, att_o, att_lse, dy1)
    dh1 = jnp.concatenate([dbg, dcg, dhx, dzc, dq, dk, dv, dzd], axis=1)
    dxn1 = matmul(dh1, wio, "nt", name="odd_in_dx")
    d_wio = matmul(xn1, dh1, "tn", name="odd_in_dw", out_layout=cols_layout)
    dx1, dx1b, d_odd_g = rmsnorm_bwd(x1, odd_g, dxn1, dx2, name="odd_norm_bwd")

    d_woe = matmul(y0, dx1b, "tn", name="even_out_dw", tm=ROW_PIECE, out_layout=rows_layout)
    first = [d_wio, d_woo, d_woe]
    dy0, first_sib = matmul(dx1b, woe, "nt", name="even_out_dy", rider=swap_halves_rider(first))
    first_sums = [chip_sum(p, s, core, name=f"chip_sum_first_{i}") for i, (p, s) in enumerate(zip(first, first_sib))]
    dh0, d_ln_g, d_ln_b, d_ws, d_bs_t = gmlp_bwd(h0, dy0, ln_g, ln_b, ws, bs_t)
    (dxbc_act, ddtr, dh0, d_dt_bias, d_a_log, d_d, d_norm_g), first_landed = ssd_bwd(
        xbc, dtr, h0, states, dy0, dt_bias, a_log, d_skip, norm_g, dh0, rider=scatter_rider(first_sums))
    first_totals = [total_sum(s, l, chip, core, name=f"total_first_{i}") for i, (s, l) in enumerate(zip(first_sums, first_landed))]
    (dh0, d_conv_w, d_conv_b), first_joined = ssd_conv_bwd(h0, dxbc_act, conv_w, conv_b, dh0, rider=join_halves_rider(first_totals))
    ddtr_b = ddtr
    d_wie_main = matmul(xn0, dh0, "tn", name="even_in_dw")
    d_wie_dt = matmul(xn0, ddtr_b, "tn", name="even_in_dw_dt")
    last = [wie_grad_to_pieces(d_wie_main, d_wie_dt)]
    dxn0, last_sib = matmul(ddtr_b, wie_dt, "nt", name="even_in_dx_dt", rider=swap_halves_rider(last))
    last_sums = [chip_sum(last[0], last_sib[0], core, name="chip_sum_last")]
    dxn0, last_landed = matmul(dh0, wie_main, "nt", name="even_in_dx", res=dxn0, rider=scatter_rider(last_sums))
    last_totals = [total_sum(last_sums[0], last_landed[0], chip, core, name="total_last")]
    grad_x, _, d_even_g = rmsnorm_bwd(xs, even_g, dxn0, dx1, name="even_norm_bwd")

    small_names = ["even_norm_g", "gmlp_ln_g", "gmlp_ln_b", "gmlp_ws", "gmlp_bs", "ssd_conv_w", "ssd_conv_b", "ssd_dt_bias",
                   "ssd_a_log", "ssd_d", "ssd_norm_g", "odd_norm_g", "sconv_w", "final_norm_g"]
    small_parts = [d_even_g, d_ln_g, d_ln_b, d_ws, d_bs_t.T, d_conv_w, d_conv_b, d_dt_bias, d_a_log, d_d, d_norm_g, d_odd_g, d_sconv, d_fin_g]
    small_shapes = [p.shape for p in small_parts]
    small_rows = [p.reshape(-1, LANES) for p in small_parts]
    small_sum, last_joined = sum_leading(gather_all(pack_rows(small_rows)), name="small_sum", rider=join_halves_rider(last_totals))
    full = {nm: rows.reshape(shape) for nm, rows, shape in zip(small_names, _unpack(small_sum, small_rows), small_shapes)}
    joined = last_joined + first_joined
    grads = dict(even_w_in=joined[0].reshape(even_w_in.shape), odd_w_in=joined[1].reshape(odd_w_in.shape),
                 odd_w_out=joined[2].reshape(odd_w_out.shape), even_w_out=joined[3].reshape(even_w_out.shape))
    for nm in small_names:
        g = full[nm]
        if nm in ("ssd_dt_bias", "ssd_a_log", "ssd_d"):
            g = g[:, :B_HEADS]
        elif nm == "ssd_conv_w":
            g = lax.dynamic_slice_in_dim(g, chip * cshard, cshard, axis=1)
        elif nm in ("odd_norm_g", "sconv_w"):
            g = lax.dynamic_slice_in_dim(g, chip * dshard, dshard, axis=1)
        grads[nm] = g.reshape(weights[nm].shape)

    deltas, new_m, new_v = {}, {}, {}
    for nm in names:
        w = weights[nm]
        d, nm_, nv_ = adamw(_as2d(w), _as2d(grads[nm]), _as2d(moms_m[nm]), _as2d(moms_v[nm]), name=f"adamw_{nm}")
        deltas[nm], new_m[nm], new_v[nm] = d.reshape(w.shape), nm_.reshape(w.shape), nv_.reshape(w.shape)

    loss = lax.psum(loss_part[0, 0], ("x", "y", "c"))
    return (loss, grad_x[None], *[grads[n] for n in names], *[deltas[n] for n in names],
            *[new_m[n] for n in names], *[new_v[n] for n in names])
```

```python
import functools

import jax
import jax.numpy as jnp
from jax import lax
from jax.experimental import pallas as pl
from jax.experimental.pallas import tpu as pltpu

F32 = jnp.float32
BF16 = jnp.bfloat16
SDS = jax.ShapeDtypeStruct
MESH = pl.DeviceIdType.MESH

D_MODEL = 2048
A_WIDTH = 2048
A_GROUPS = 8
CHUNK = 128
B_WIDTH = 2048
B_HEADS = 32
B_HEAD_DIM = 64
B_GROUPS = 8
B_STATE = 128
B_CONV = 4
B_XBC = B_WIDTH + 2 * B_GROUPS * B_STATE
C_WIDTH = 2048
C_CONV = 3
D_HEADS = 16
D_HEAD_DIM = 128
D_PATTERNS = ((128, 1), (512, 4), (2048, 16))
ATT_BLOCK = 128
ATT_SUPER = 2048
EVEN_MAIN = 3 * A_WIDTH + B_WIDTH + B_XBC
IN_EVEN = EVEN_MAIN + B_HEADS
IN_ODD = 4 * C_WIDTH + 4 * D_HEADS * D_HEAD_DIM
LANES = 128
SUBLANES = 8
EPS = 1e-5
ADAM_LR = 0.001
ADAM_B1 = 0.9
ADAM_B2 = 0.999
ADAM_EPS = 1e-08
ADAM_WD = 0.01
ADAM_STEP = 10
N_CHIPS = 4
N_DEV = 8
VMEM_LIMIT_BYTES = 56 * 1024 * 1024


def _cp(*sem):
    return pltpu.CompilerParams(dimension_semantics=sem, vmem_limit_bytes=VMEM_LIMIT_BYTES)


def _full(shape):
    return pl.BlockSpec(shape, lambda *_: (0,) * len(shape))


def _silu(x):
    return x * jax.nn.sigmoid(x)


def _dot_nn(a, b):
    return lax.dot_general(a, b, (((1,), (0,)), ((), ())), preferred_element_type=F32)


def _dot_nt(a, b):
    return lax.dot_general(a, b, (((1,), (1,)), ((), ())), preferred_element_type=F32)


def _dot_tn(a, b):
    return lax.dot_general(a, b, (((0,), (0,)), ((), ())), preferred_element_type=F32)


def _tril(n):
    return lax.broadcasted_iota(jnp.int32, (n, n), 0) >= lax.broadcasted_iota(jnp.int32, (n, n), 1)


_DOTS = {"nn": _dot_nn, "nt": _dot_nt, "tn": _dot_tn}


class Rider:
    def __init__(self, ins, out_shapes, n_sems, phases, in_place=False):
        self.ins, self.out_shapes, self.n_sems, self.phases = list(ins), list(out_shapes), n_sems, list(phases)
        self.in_place = in_place


def _call(body, *, name, grid, in_specs, out_specs, out_shape, scratch_shapes, sem, args, rider=None, aliases=None):
    in_specs, out_specs, out_shape, scratch_shapes = list(in_specs), list(out_specs), list(out_shape), list(scratch_shapes)
    aliases = dict(aliases or {})
    if rider is None:
        res = pl.pallas_call(body, name=name, grid=grid, in_specs=in_specs, out_specs=out_specs, out_shape=out_shape,
                             scratch_shapes=scratch_shapes, input_output_aliases=aliases, compiler_params=_cp(*sem))(*args)
        return list(res), []
    if rider.in_place:
        aliases.update({len(in_specs) + k: len(out_specs) + k for k in range(len(rider.ins))})
    counts = [len(in_specs), len(rider.ins), len(out_specs), len(rider.out_shapes), len(scratch_shapes), 2]
    total = 1
    for g in grid:
        total *= g

    def wrapped(*refs):
        groups, pos = [], 0
        for n in counts:
            groups.append(refs[pos:pos + n])
            pos += n
        ins, rins, outs, routs, scr, (ssem, rsem) = groups
        step = 0
        for d, g in enumerate(grid):
            step = step * g + pl.program_id(d)
        for frac, fn in rider.phases:
            @pl.when(step == min(int(frac * total), total - 1))
            def _(fn=fn):
                fn(rins, routs, ssem, rsem)
        body(*ins, *outs, *scr)

    dma = pltpu.SemaphoreType.DMA((rider.n_sems,))
    res = pl.pallas_call(
        wrapped, name=name, grid=grid, in_specs=in_specs + [ANY] * len(rider.ins), out_specs=out_specs + [ANY] * len(rider.out_shapes),
        out_shape=out_shape + rider.out_shapes, scratch_shapes=scratch_shapes + [dma, dma], input_output_aliases=aliases,
        compiler_params=_cp(*(("arbitrary",) * len(grid))))(*args, *rider.ins)
    return list(res[:len(out_specs)]), list(res[len(out_specs):])


def run_rider(rider, *, name):
    def body(*refs):
        n_in, n_out = len(rider.ins), len(rider.out_shapes)
        ins, outs, (ssem, rsem) = refs[:n_in], refs[n_in:n_in + n_out], refs[n_in + n_out:]
        for _, fn in rider.phases:
            fn(ins, outs, ssem, rsem)

    dma = pltpu.SemaphoreType.DMA((rider.n_sems,))
    return list(pl.pallas_call(body, name=name, in_specs=[ANY] * len(rider.ins), out_specs=[ANY] * len(rider.out_shapes),
                               out_shape=rider.out_shapes, scratch_shapes=[dma, dma])(*rider.ins))


def matmul(a, b, mode, *, name, out_dtype=F32, res=None, tm=1024, tn=1024, tk=2048, out_layout=None, rider=None):
    a_parts = list(a) if isinstance(a, (list, tuple)) else [a]
    b_parts = list(b) if isinstance(b, (list, tuple)) else [b]
    assert len(b_parts) == 1 or mode == "tn"
    b0 = b_parts[0]
    shards = b0.shape[0] if b0.ndim == 3 else 1
    a_rows, a_cols = a_parts[0].shape[0], sum(p.shape[1] for p in a_parts)
    b_rows, b_cols = b0.shape[-2], sum(p.shape[-1] for p in b_parts) * shards
    if mode == "tn":
        (K, M), (K2, N) = (a_rows, a_cols), (b_rows, b_cols)
    elif mode == "nt":
        (M, K), (N, K2) = (a_rows, a_cols), (b_rows, b_cols)
    else:
        (M, K), (K2, N) = (a_rows, a_cols), (b_rows, b_cols)
    assert K == K2, (mode, K, K2)
    tm, tn, tk = min(tm, M), min(tn, N), min(tk, K)
    if shards > 1:
        assert mode != "tn" and b0.shape[-1] % (tk if mode == "nt" else tn) == 0
    assert M % tm == 0 and N % tn == 0 and K % tk == 0, (M, N, K, tm, tn, tk)
    nk = K // tk
    dot = _DOTS[mode]

    def spans(parts, tile):
        out, off = [], 0
        for p in parts:
            assert p.shape[-1] % tile == 0, (p.shape, tile)
            out.append((off, p.shape[-1] // tile))
            off += p.shape[-1] // tile
        return out

    a_axis = 0 if mode == "tn" else 2
    a_spans = spans(a_parts, tm if mode == "tn" else tk)
    b_spans = spans(b_parts, tn) if len(b_parts) > 1 else [(0, N // tn)]

    def inside(t, span):
        return jnp.logical_and(t >= span[0], t < span[0] + span[1])

    def body(*refs):
        a_refs, b_refs, rest = refs[:len(a_parts)], refs[len(a_parts):len(a_parts) + len(b_parts)], refs[len(a_parts) + len(b_parts):]
        r_ref = rest[0] if res is not None else None
        o_ref = rest[1] if res is not None else rest[0]
        k = pl.program_id(2)

        def finish(acc):
            if res is not None:
                acc = acc + r_ref[...]
            o_ref[...] = acc.astype(o_ref.dtype).reshape(o_ref.shape)

        def emit(a_ref, b_ref, conds, k_lo, k_hi):
            def region(*more):
                cs = conds + list(more)
                return pl.when(functools.reduce(jnp.logical_and, cs)) if cs else (lambda f: f())

            def product():
                return dot(a_ref[...], b_ref[0] if shards > 1 else b_ref[...])

            if nk == 1:
                region()(lambda: finish(product()))
                return
            acc_ref = rest[-1]
            if k_lo == 0:
                @region(k == 0)
                def _():
                    acc_ref[...] = product()

            if max(k_lo, 1) < min(k_hi, nk - 1):
                @region(k > 0, k < nk - 1)
                def _():
                    acc_ref[...] += product()

            if k_hi == nk:
                @region(k == nk - 1)
                def _():
                    finish(acc_ref[...] + product())

        for a_ref, a_span in zip(a_refs, a_spans):
            for b_ref, b_span in zip(b_refs, b_spans):
                by_k = a_axis == 2 and len(a_parts) > 1
                emit(a_ref, b_ref, ([inside(pl.program_id(a_axis), a_span)] if len(a_parts) > 1 else []) +
                     ([inside(pl.program_id(1), b_span)] if len(b_parts) > 1 else []),
                     a_span[0] if by_k else 0, a_span[0] + a_span[1] if by_k else nk)

    def piece_index(t, span):
        return jnp.clip(t - span[0], 0, span[1] - 1)

    def a_spec_of(span):
        if len(a_parts) == 1:
            return pl.BlockSpec((tk, tm), lambda i, j, k: (k, i)) if mode == "tn" else pl.BlockSpec((tm, tk), lambda i, j, k: (i, k))
        if mode == "tn":
            return pl.BlockSpec((tk, tm), lambda i, j, k: (jnp.where(inside(i, span), k, 0), piece_index(i, span)))
        return pl.BlockSpec((tm, tk), lambda i, j, k: (i, piece_index(k, span)))

    def b_spec_of(span):
        if shards > 1 and mode == "nn":
            per = b0.shape[-1] // tn
            return pl.BlockSpec((1, tk, tn), lambda i, j, k: (j // per, k, j % per))
        if shards > 1:
            per = b0.shape[-1] // tk
            return pl.BlockSpec((1, tn, tk), lambda i, j, k: (k // per, j, k % per))
        if mode == "nt":
            return pl.BlockSpec((tn, tk), lambda i, j, k: (j, k))
        if len(b_parts) == 1:
            return pl.BlockSpec((tk, tn), lambda i, j, k: (k, j))
        return pl.BlockSpec((tk, tn), lambda i, j, k: (jnp.where(inside(j, span), k, 0), piece_index(j, span)))

    o_spec = pl.BlockSpec((tm, tn), lambda i, j, k: (i, j))
    in_specs = [a_spec_of(s) for s in a_spans] + [b_spec_of(s) for s in b_spans]
    args = a_parts + b_parts
    if res is not None:
        in_specs.append(o_spec)
        args.append(res)
    out_shape = SDS((M, N), out_dtype)
    if out_layout is not None:
        out_shape, o_spec = SDS(out_layout[0], out_dtype), pl.BlockSpec(out_layout[1], out_layout[2])
    outs, rider_outs = _call(
        body, name=name, grid=(M // tm, N // tn, nk), in_specs=in_specs, out_specs=[o_spec], out_shape=[out_shape],
        scratch_shapes=[pltpu.VMEM((tm, tn), F32)] if nk > 1 else [], sem=("parallel", "parallel", "arbitrary"),
        args=args, rider=rider)
    return outs[0] if rider is None else (outs[0], rider_outs)


ROW_TILE = 512
ROW_PIECE = 512


def _rms(x, g):
    return x * lax.rsqrt(jnp.mean(x * x, axis=-1, keepdims=True) + EPS) * g


def rmsnorm_fwd(x, g, *, name):
    T, D = x.shape

    def body(x_ref, g_ref, o_ref):
        o_ref[...] = _rms(x_ref[...], g_ref[...]).astype(BF16)

    row = pl.BlockSpec((ROW_TILE, D), lambda i: (i, 0))
    return pl.pallas_call(body, name=name, grid=(T // ROW_TILE,), in_specs=[row, _full((1, D))], out_specs=row,
                          out_shape=SDS((T, D), BF16), compiler_params=_cp("parallel"))(x, g)


def rmsnorm_bwd(x, g, dxn, dres, *, name, rider=None):
    T, D = x.shape

    def body(x_ref, g_ref, dxn_ref, dres_ref, dx_ref, dxb_ref, dg_ref):
        _, vjp = jax.vjp(_rms, x_ref[...], g_ref[...])
        dx, dg = vjp(dxn_ref[...])
        dx = dx + dres_ref[...]
        dx_ref[...] = dx
        dxb_ref[...] = dx.astype(BF16)

        @pl.when(pl.program_id(0) == 0)
        def _():
            dg_ref[...] = jnp.zeros_like(dg_ref)

        dg_ref[...] += dg

    row = pl.BlockSpec((ROW_TILE, D), lambda i: (i, 0))
    outs, rider_outs = _call(
        body, name=name, grid=(T // ROW_TILE,), in_specs=[row, _full((1, D)), row, row],
        out_specs=[row, row, _full((1, D))], out_shape=[SDS((T, D), F32), SDS((T, D), BF16), SDS((1, D), F32)],
        scratch_shapes=[], sem=("arbitrary",), args=(x, g, dxn, dres), rider=rider)
    return outs if rider is None else (outs, rider_outs)


def _loss_tile(x, g, tgt):
    err = jnp.square(_rms(x, g) - tgt)
    return 0.5 * jnp.sum(jnp.mean(err, axis=-1))


def loss_head(x, g, tgt):
    T, D = x.shape

    def body(x_ref, g_ref, t_ref, loss_ref, dx_ref, dxb_ref, dg_ref):
        loss, vjp = jax.vjp(_loss_tile, x_ref[...], g_ref[...], t_ref[...])
        dx, dg, _ = vjp(jnp.ones((), F32))
        dx_ref[...] = dx
        dxb_ref[...] = dx.astype(BF16)

        @pl.when(pl.program_id(0) == 0)
        def _():
            dg_ref[...] = jnp.zeros_like(dg_ref)
            loss_ref[...] = jnp.zeros_like(loss_ref)

        dg_ref[...] += dg
        loss_ref[...] += jnp.reshape(loss, (1, 1))

    row = pl.BlockSpec((ROW_TILE, D), lambda i: (i, 0))
    return pl.pallas_call(
        body, name="loss_head", grid=(T // ROW_TILE,), in_specs=[row, _full((1, D)), row],
        out_specs=[_full((1, 1)), row, row, _full((1, D))],
        out_shape=[SDS((1, 1), F32), SDS((T, D), F32), SDS((T, D), BF16), SDS((1, D), F32)],
        compiler_params=_cp("arbitrary"))(x, g, tgt)


TILE_BYTES = 1 << 20


def _row_tile(rows, row_bytes):
    for cand in (512, 256, 128, 64, 32, 16, 8):
        if rows % cand == 0 and cand * row_bytes <= TILE_BYTES:
            return cand
    return rows


def adamw(w, g, m, v, *, name):
    R, C = w.shape
    tr = _row_tile(R, C * 4)

    def body(w_ref, g_ref, m_ref, v_ref, d_ref, nm_ref, nv_ref):
        gg = g_ref[...]
        mm = ADAM_B1 * m_ref[...] + (1.0 - ADAM_B1) * gg
        vv = ADAM_B2 * v_ref[...] + (1.0 - ADAM_B2) * jnp.square(gg)
        m_hat = mm / (1.0 - ADAM_B1 ** ADAM_STEP)
        v_hat = vv / (1.0 - ADAM_B2 ** ADAM_STEP)
        d_ref[...] = -ADAM_LR * (m_hat / (jnp.sqrt(v_hat) + ADAM_EPS) + ADAM_WD * w_ref[...])
        nm_ref[...] = mm
        nv_ref[...] = vv

    blk = pl.BlockSpec((tr, C), lambda i: (i, 0))
    return pl.pallas_call(body, name=name, grid=(R // tr,), in_specs=[blk] * 4, out_specs=[blk] * 3,
                          out_shape=[SDS((R, C), F32)] * 3, compiler_params=_cp("parallel"))(w, g, m, v)


def sum_leading(a, *, name, rider=None):
    n, R, C = a.shape
    tr = _row_tile(R, n * C * 4)

    def body(a_ref, o_ref):
        acc = a_ref[0]
        for j in range(1, n):
            acc = acc + a_ref[j]
        o_ref[...] = acc

    outs, rider_outs = _call(body, name=name, grid=(R // tr,), in_specs=[pl.BlockSpec((n, tr, C), lambda i: (0, i, 0))],
                             out_specs=[pl.BlockSpec((tr, C), lambda i: (i, 0))], out_shape=[SDS((R, C), F32)],
                             scratch_shapes=[], sem=("parallel",), args=(a,), rider=rider)
    return outs[0] if rider is None else (outs[0], rider_outs)


def chip_sum(pieces, from_sibling, core, *, name):
    _, n, R, C = pieces.shape

    def body(c_ref, a_ref, b_ref, o_ref):
        o_ref[...] = (a_ref[0] + b_ref[...]).astype(BF16)

    tr = _row_tile(R, C * 4)
    blk = pl.BlockSpec((1, tr, C), lambda k, i, c_ref: (k, i, 0))
    mine = pl.BlockSpec((1, 1, tr, C), lambda k, i, c_ref: (c_ref[0], k, i, 0))
    return pl.pallas_call(
        body, name=name, out_shape=SDS((n, R, C), BF16),
        grid_spec=pltpu.PrefetchScalarGridSpec(num_scalar_prefetch=1, grid=(n, R // tr), in_specs=[mine, blk], out_specs=blk),
        compiler_params=_cp("parallel", "parallel"))(core.reshape(1), pieces, from_sibling)


def total_sum(sums, landed, chip, core, *, name):
    n, R, C = landed.shape

    def body(k_ref, c_ref, s_ref, l_ref, o_ref):
        acc = s_ref[0].astype(F32)
        for j in range(n):
            acc = acc + l_ref[j].astype(F32)
        south = c_ref[0] == 0
        o_ref[0] = jnp.where(south, acc, 0.0)
        o_ref[1] = jnp.where(south, 0.0, acc)

    tr = _row_tile(R, n * C * 2)
    return pl.pallas_call(
        body, name=name, out_shape=SDS((2, R, C), F32),
        grid_spec=pltpu.PrefetchScalarGridSpec(
            num_scalar_prefetch=2, grid=(R // tr,),
            in_specs=[pl.BlockSpec((1, tr, C), lambda i, k_ref, c_ref: (k_ref[0], i, 0)),
                      pl.BlockSpec((n, tr, C), lambda i, k_ref, c_ref: (0, i, 0))],
            out_specs=pl.BlockSpec((2, tr, C), lambda i, k_ref, c_ref: (0, i, 0))),
        compiler_params=_cp("parallel"))(chip.reshape(1), core.reshape(1), sums, landed)


def _gmlp_chunk(u, v, z, ln_g, ln_b, wsc, bs_t):
    mu = jnp.mean(v, axis=-1, keepdims=True)
    xc = v - mu
    vn = xc * lax.rsqrt(jnp.mean(xc * xc, axis=-1, keepdims=True) + EPS) * ln_g + ln_b
    gw = A_WIDTH // A_GROUPS
    outs = []
    for g in range(A_GROUPS):
        m = _dot_nn(wsc[g].astype(BF16), vn[:, g * gw:(g + 1) * gw].astype(BF16))
        outs.append(m + bs_t[:, g:g + 1])
    return _silu(z) * (u * jnp.concatenate(outs, axis=1))


def _h_cols(width, idx, rows=CHUNK):
    return pl.BlockSpec((rows, width), lambda i: (i, idx))


def gmlp_fwd(h, ln_g, ln_b, ws, bs_t):
    T = h.shape[0]

    def body(u_ref, v_ref, z_ref, g_ref, b_ref, ws_ref, bs_ref, o_ref):
        wsc = jnp.where(_tril(CHUNK)[None], ws_ref[...], 0.0)
        o_ref[...] = _gmlp_chunk(u_ref[...], v_ref[...], z_ref[...], g_ref[...], b_ref[...], wsc, bs_ref[...]).astype(BF16)

    return pl.pallas_call(
        body, name="gmlp_fwd", grid=(T // CHUNK,),
        in_specs=[_h_cols(A_WIDTH, 0), _h_cols(A_WIDTH, 1), _h_cols(A_WIDTH, 2), _full((1, A_WIDTH)), _full((1, A_WIDTH)),
                  _full((A_GROUPS, CHUNK, CHUNK)), _full((CHUNK, A_GROUPS))],
        out_specs=_h_cols(A_WIDTH, 0), out_shape=SDS((T, A_WIDTH), BF16), compiler_params=_cp("parallel"),
    )(h, h, h, ln_g, ln_b, ws, bs_t)


def gmlp_bwd(h, dy, ln_g, ln_b, ws, bs_t):
    T = h.shape[0]

    def body(u_ref, v_ref, z_ref, dy_ref, g_ref, b_ref, ws_ref, bs_ref, duvz_ref, dg_ref, db_ref, dws_ref, dbs_ref):
        tri = _tril(CHUNK)[None]
        wsc = jnp.where(tri, ws_ref[...], 0.0)
        _, vjp = jax.vjp(_gmlp_chunk, u_ref[...], v_ref[...], z_ref[...], g_ref[...], b_ref[...], wsc, bs_ref[...])
        du, dv, dz, dg, db, dws, dbs = vjp(dy_ref[...])
        duvz_ref[:, :A_WIDTH] = du.astype(BF16)
        duvz_ref[:, A_WIDTH:2 * A_WIDTH] = dv.astype(BF16)
        duvz_ref[:, 2 * A_WIDTH:] = dz.astype(BF16)

        @pl.when(pl.program_id(0) == 0)
        def _():
            dg_ref[...] = jnp.zeros_like(dg_ref)
            db_ref[...] = jnp.zeros_like(db_ref)
            dws_ref[...] = jnp.zeros_like(dws_ref)
            dbs_ref[...] = jnp.zeros_like(dbs_ref)

        dg_ref[...] += dg
        db_ref[...] += db
        dws_ref[...] += jnp.where(tri, dws, 0.0)
        dbs_ref[...] += dbs

    pshapes = [(1, A_WIDTH), (1, A_WIDTH), (A_GROUPS, CHUNK, CHUNK), (CHUNK, A_GROUPS)]
    return pl.pallas_call(
        body, name="gmlp_bwd", grid=(T // CHUNK,),
        in_specs=[_h_cols(A_WIDTH, 0), _h_cols(A_WIDTH, 1), _h_cols(A_WIDTH, 2), _h_cols(A_WIDTH, 0)] + [_full(s) for s in pshapes],
        out_specs=[_h_cols(3 * A_WIDTH, 0)] + [_full(s) for s in pshapes],
        out_shape=[SDS((T, EVEN_MAIN), BF16)] + [SDS(s, F32) for s in pshapes],
        compiler_params=_cp("arbitrary"),
    )(h, h, h, dy, ln_g, ln_b, ws, bs_t)


CONV_ROWS = 256
CONV_COLS = 512


def _row8():
    return lax.broadcasted_iota(jnp.int32, (SUBLANES, 1), 0)


def _delayed(x, halo, s):
    if s == 0:
        return x
    r = pltpu.roll(x, s, axis=0)
    top = jnp.where(_row8() < s, pltpu.roll(halo, s, axis=0), r[:SUBLANES])
    return jnp.concatenate([top, r[SUBLANES:]], axis=0)


def _advanced(d, s):
    if s == 0:
        return d
    rows = d.shape[0]
    r = pltpu.roll(d, rows - s, axis=0)
    bottom = jnp.where(_row8() >= SUBLANES - s, 0.0, r[rows - SUBLANES:])
    return jnp.concatenate([r[:rows - SUBLANES], bottom], axis=0)


def _conv(x, halo, w):
    K = w.shape[0]
    acc = None
    for s in range(K):
        term = w[K - 1 - s:K - s, :] * _delayed(x, halo, s)
        acc = term if acc is None else acc + term
    return acc


def _conv_bwd(x, halo, w, d):
    K = w.shape[0]
    dx, dhalo, dws = None, None, [None] * K
    for s in range(K):
        wk = w[K - 1 - s:K - s, :]
        dws[K - 1 - s] = jnp.sum(d * _delayed(x, halo, s), axis=0, keepdims=True)
        term = wk * _advanced(d, s)
        dx = term if dx is None else dx + term
        if s:
            part = wk * jnp.where(_row8() >= SUBLANES - s, pltpu.roll(d[:SUBLANES], SUBLANES - s, axis=0), 0.0)
            dhalo = part if dhalo is None else dhalo + part
    return dx, dhalo, jnp.concatenate(dws, axis=0)


def _add_to_tail(d, carry):
    return jnp.concatenate([d[:d.shape[0] - SUBLANES], d[d.shape[0] - SUBLANES:] + carry], axis=0)


def _halo_spec(cols, col_idx, nt=None):
    rpb = CONV_ROWS // SUBLANES
    if nt is None:
        return pl.BlockSpec((SUBLANES, cols), lambda c, i: (jnp.maximum(i * rpb - 1, 0), col_idx(c)))
    return pl.BlockSpec((SUBLANES, cols), lambda c, j: (jnp.maximum((nt - 1 - j) * rpb - 1, 0), col_idx(c)))


def ssd_conv_fwd(h, w, b):
    T = h.shape[0]
    nc = B_XBC // CONV_COLS
    base = (3 * A_WIDTH + B_WIDTH) // CONV_COLS

    def body(x_ref, halo_ref, w_ref, b_ref, o_ref):
        halo = jnp.where(pl.program_id(1) > 0, halo_ref[...], 0.0)
        o_ref[...] = _silu(_conv(x_ref[...], halo, w_ref[...]) + b_ref[...])

    return pl.pallas_call(
        body, name="ssd_conv_fwd", grid=(nc, T // CONV_ROWS),
        in_specs=[pl.BlockSpec((CONV_ROWS, CONV_COLS), lambda c, i: (i, base + c)), _halo_spec(CONV_COLS, lambda c: base + c),
                  pl.BlockSpec((B_CONV, CONV_COLS), lambda c, i: (0, c)), pl.BlockSpec((1, CONV_COLS), lambda c, i: (0, c))],
        out_specs=pl.BlockSpec((CONV_ROWS, CONV_COLS), lambda c, i: (i, c)),
        out_shape=SDS((T, B_XBC), F32), compiler_params=_cp("parallel", "parallel"),
    )(h, h, w, b)


def ssd_conv_bwd(h, dy, w, b, dh, rider=None):
    T = h.shape[0]
    nc = B_XBC // CONV_COLS
    nt = T // CONV_ROWS
    base = (3 * A_WIDTH + B_WIDTH) // CONV_COLS

    def body(x_ref, halo_ref, dy_ref, w_ref, b_ref, dh_ref, dx_ref, dw_ref, db_ref, carry_ref):
        j = pl.program_id(1)
        halo = jnp.where(j < nt - 1, halo_ref[...], 0.0)
        x, w = x_ref[...], w_ref[...]
        pre = _conv(x, halo, w) + b_ref[...]
        dpre = dy_ref[...] * _dsilu(pre)
        dx, dhalo, dw = _conv_bwd(x, halo, w, dpre)

        @pl.when(j == 0)
        def _():
            carry_ref[...] = jnp.zeros_like(carry_ref)
            dw_ref[...] = jnp.zeros_like(dw_ref)
            db_ref[...] = jnp.zeros_like(db_ref)

        dx_ref[...] = _add_to_tail(dx, carry_ref[...]).astype(BF16)
        carry_ref[...] = dhalo
        dw_ref[...] += dw
        db_ref[...] += jnp.sum(dpre, axis=0, keepdims=True)

    outs, rider_outs = _call(
        body, name="ssd_conv_bwd", grid=(nc, nt),
        in_specs=[pl.BlockSpec((CONV_ROWS, CONV_COLS), lambda c, j: (nt - 1 - j, base + c)),
                  _halo_spec(CONV_COLS, lambda c: base + c, nt),
                  pl.BlockSpec((CONV_ROWS, CONV_COLS), lambda c, j: (nt - 1 - j, c)),
                  pl.BlockSpec((B_CONV, CONV_COLS), lambda c, j: (0, c)), pl.BlockSpec((1, CONV_COLS), lambda c, j: (0, c)), ANY],
        out_specs=[pl.BlockSpec((CONV_ROWS, CONV_COLS), lambda c, j: (nt - 1 - j, base + c)),
                   pl.BlockSpec((B_CONV, CONV_COLS), lambda c, j: (0, c)), pl.BlockSpec((1, CONV_COLS), lambda c, j: (0, c))],
        out_shape=[SDS(dh.shape, dh.dtype), SDS((B_CONV, B_XBC), F32), SDS((1, B_XBC), F32)],
        scratch_shapes=[pltpu.VMEM((SUBLANES, CONV_COLS), F32)], sem=("parallel", "arbitrary"),
        args=(h, h, dy, w, b, dh), rider=rider, aliases={5: 0})
    return outs if rider is None else (outs, rider_outs)


def sconv_fwd(h, w):
    T = h.shape[0]
    nc = C_WIDTH // CONV_COLS

    def col(seg):
        return pl.BlockSpec((CONV_ROWS, CONV_COLS), lambda c, i: (i, seg * nc + c))

    def body(bg_ref, cg_ref, hx_ref, z_ref, cgh_ref, hxh_ref, w_ref, o_ref):
        first = pl.program_id(1) == 0
        cgh = jnp.where(first, 0.0, cgh_ref[...])
        hxh = jnp.where(first, 0.0, hxh_ref[...])
        conv = _conv(cg_ref[...] * hx_ref[...], cgh * hxh, w_ref[...])
        o_ref[...] = (_silu(z_ref[...]) * (bg_ref[...] * conv)).astype(BF16)

    return pl.pallas_call(
        body, name="sconv_fwd", grid=(nc, T // CONV_ROWS),
        in_specs=[col(0), col(1), col(2), col(3), _halo_spec(CONV_COLS, lambda c: nc + c), _halo_spec(CONV_COLS, lambda c: 2 * nc + c),
                  pl.BlockSpec((C_CONV, CONV_COLS), lambda c, i: (0, c))],
        out_specs=pl.BlockSpec((CONV_ROWS, CONV_COLS), lambda c, i: (i, c)),
        out_shape=SDS((T, C_WIDTH), BF16), compiler_params=_cp("parallel", "parallel"),
    )(h, h, h, h, h, h, w)


def sconv_bwd(h, dy, w):
    T = h.shape[0]
    nc = C_WIDTH // CONV_COLS
    nt = T // CONV_ROWS

    def col(seg):
        return pl.BlockSpec((CONV_ROWS, CONV_COLS), lambda c, j: (nt - 1 - j, seg * nc + c))

    def body(bg_ref, cg_ref, hx_ref, z_ref, cgh_ref, hxh_ref, dy_ref, w_ref, dbg_ref, dcg_ref, dhx_ref, dz_ref, dw_ref, carry_ref):
        j = pl.program_id(1)
        first = j == nt - 1
        cgh = jnp.where(first, 0.0, cgh_ref[...])
        hxh = jnp.where(first, 0.0, hxh_ref[...])
        bg, cg, hx, z, w, dy = bg_ref[...], cg_ref[...], hx_ref[...], z_ref[...], w_ref[...], dy_ref[...]
        ch, ch_halo = cg * hx, cgh * hxh
        conv = _conv(ch, ch_halo, w)
        gated = dy * _silu(z)
        dch, dch_halo, dw = _conv_bwd(ch, ch_halo, w, gated * bg)

        @pl.when(j == 0)
        def _():
            carry_ref[...] = jnp.zeros_like(carry_ref)
            dw_ref[...] = jnp.zeros_like(dw_ref)

        dch = _add_to_tail(dch, carry_ref[...])
        dbg_ref[...] = (gated * conv).astype(BF16)
        dz_ref[...] = (dy * bg * conv * _dsilu(z)).astype(BF16)
        dcg_ref[...] = (dch * hx).astype(BF16)
        dhx_ref[...] = (dch * cg).astype(BF16)
        carry_ref[...] = dch_halo
        dw_ref[...] += dw

    out_row = pl.BlockSpec((CONV_ROWS, CONV_COLS), lambda c, j: (nt - 1 - j, c))
    wspec = pl.BlockSpec((C_CONV, CONV_COLS), lambda c, j: (0, c))
    return pl.pallas_call(
        body, name="sconv_bwd", grid=(nc, nt),
        in_specs=[col(0), col(1), col(2), col(3), _halo_spec(CONV_COLS, lambda c: nc + c, nt), _halo_spec(CONV_COLS, lambda c: 2 * nc + c, nt),
                  out_row, wspec],
        out_specs=[out_row] * 4 + [wspec],
        out_shape=[SDS((T, C_WIDTH), BF16)] * 4 + [SDS((C_CONV, C_WIDTH), F32)],
        scratch_shapes=[pltpu.VMEM((SUBLANES, CONV_COLS), F32)],
        compiler_params=_cp("parallel", "arbitrary"),
    )(h, h, h, h, h, h, dy, w)


def _softplus(x):
    return jnp.maximum(x, 0.0) + jnp.log(1.0 + jnp.exp(-jnp.abs(x)))


def _ssd_chunk(xs, bm, cm, dtr, z, prev, dt_bias, a_log, d_skip, norm_g):
    tril = _tril(CHUNK)
    dt = _softplus(dtr + dt_bias)
    adt = dt * (-jnp.exp(a_log))
    a_cs = jnp.dot(tril.astype(F32), adt, precision=lax.Precision.HIGHEST, preferred_element_type=F32)
    a_cs_t = a_cs.T
    a_last = a_cs[CHUNK - 1:CHUNK, :]
    dt_f = _spread_heads(dt, B_HEAD_DIM)
    dec_f = _spread_heads(jnp.exp(a_last - a_cs), B_HEAD_DIM)
    ecs_f = _spread_heads(jnp.exp(a_cs), B_HEAD_DIM)
    dsk_f = _spread_heads(d_skip, B_HEAD_DIM)
    cd_t = jnp.exp(a_cs_t[:, CHUNK - 1:CHUNK])
    xdt = xs * dt_f
    xdd = xdt * dec_f
    colb = _spread_heads(a_cs, CHUNK)
    rowb = jnp.concatenate([jnp.broadcast_to(a_cs_t[hh:hh + 1, :], (CHUNK, CHUNK)) for hh in range(B_HEADS)], axis=1)
    wide = (CHUNK, B_HEADS * CHUNK)
    keep = lax.broadcasted_iota(jnp.int32, wide, 0) >= lax.broadcasted_iota(jnp.int32, wide, 1) % CHUNK
    decay = jnp.exp(jnp.where(keep, colb - rowb, -jnp.inf))
    hpg = B_HEADS // B_GROUPS
    gw = B_WIDTH // B_GROUPS
    low_half = lax.broadcasted_iota(jnp.int32, (CHUNK, 2 * B_HEAD_DIM), 1) < B_HEAD_DIM
    ys, nxt = [], []
    for g in range(B_GROUPS):
        bg = bm[:, g * B_STATE:(g + 1) * B_STATE].astype(BF16)
        cg = cm[:, g * B_STATE:(g + 1) * B_STATE].astype(BF16)
        cb = _dot_nt(cg, bg)
        cbl = (decay[:, g * hpg * CHUNK:(g + 1) * hpg * CHUNK] * jnp.concatenate([cb] * hpg, axis=1)).astype(BF16)
        pg = prev[g * gw:(g + 1) * gw, :]
        y_off = _dot_nt(cg, pg.astype(BF16)) * ecs_f[:, g * gw:(g + 1) * gw]
        st = _dot_tn(xdd[:, g * gw:(g + 1) * gw].astype(BF16), bg)
        cd = jnp.concatenate([jnp.broadcast_to(cd_t[g * hpg + r:g * hpg + r + 1, :], (B_HEAD_DIM, 1)) for r in range(hpg)], axis=0)
        nxt.append(pg * cd + st)
        pairs = []
        for j in range(hpg // 2):
            xp = xdt[:, g * gw + 2 * j * B_HEAD_DIM:g * gw + 2 * (j + 1) * B_HEAD_DIM]
            rhs = jnp.concatenate([jnp.where(low_half, xp, 0.0), jnp.where(low_half, 0.0, xp)], axis=0).astype(BF16)
            pairs.append(_dot_nn(cbl[:, 2 * j * CHUNK:2 * (j + 1) * CHUNK], rhs))
        ys.append(jnp.concatenate(pairs, axis=1) + y_off)
    y = (jnp.concatenate(ys, axis=1) + dsk_f * xs) * _silu(z)
    outs = []
    for g in range(B_GROUPS):
        yg = y[:, g * gw:(g + 1) * gw]
        outs.append(yg * lax.rsqrt(jnp.mean(yg * yg, axis=-1, keepdims=True) + EPS))
    return jnp.concatenate(outs, axis=1) * norm_g, jnp.concatenate(nxt, axis=0)


def _split3(v):
    hi = v.astype(BF16)
    r1 = v - hi.astype(F32)
    mid = r1.astype(BF16)
    return hi, mid, (r1 - mid.astype(F32)).astype(BF16)


def _head_one_hot(width, parts):
    n = B_HEADS * width
    shape = (parts * LANES, n)
    return (lax.broadcasted_iota(jnp.int32, shape, 0) % LANES == lax.broadcasted_iota(jnp.int32, shape, 1) // width).astype(BF16)


@functools.partial(jax.custom_vjp, nondiff_argnums=(1,))
def _spread_heads(v, width):
    return _dot_nn(jnp.concatenate(_split3(v), axis=1), _head_one_hot(width, 3))


def _spread_heads_fwd(v, width):
    return _spread_heads(v, width), None


def _spread_heads_bwd(width, _, g):
    return (_dot_nt(jnp.concatenate(_split3(g), axis=1), jnp.concatenate([_head_one_hot(width, 1)] * 3, axis=1)),)


_spread_heads.defvjp(_spread_heads_fwd, _spread_heads_bwd)


_SSD_PARAM_SHAPES = [(1, LANES), (1, LANES), (1, LANES), (1, B_WIDTH)]
_STATE_SHAPE = (B_WIDTH, B_STATE)


def ssd_fwd(xbc, dtr, h, dt_bias, a_log, d_skip, norm_g, rider=None):
    T = xbc.shape[0]
    nc = T // CHUNK

    def body(xs_ref, b_ref, c_ref, dt_ref, z_ref, p0, p1, p2, p3, y_ref, st_ref, state):
        @pl.when(pl.program_id(0) == 0)
        def _():
            state[...] = jnp.zeros_like(state)

        prev = state[...]
        st_ref[0] = prev
        yb, nxt = _ssd_chunk(xs_ref[...], b_ref[...], c_ref[...], dt_ref[...], z_ref[...], prev, p0[...], p1[...], p2[...], p3[...])
        y_ref[...] = yb.astype(BF16)
        state[...] = nxt

    outs, rider_outs = _call(
        body, name="ssd_fwd", grid=(nc,),
        in_specs=[_h_cols(B_WIDTH, 0), _h_cols(B_GROUPS * B_STATE, 2), _h_cols(B_GROUPS * B_STATE, 3), _h_cols(LANES, 0), _h_cols(B_WIDTH, 3)]
        + [_full(s) for s in _SSD_PARAM_SHAPES],
        out_specs=[_h_cols(B_WIDTH, 0), pl.BlockSpec((1,) + _STATE_SHAPE, lambda i: (i, 0, 0))],
        out_shape=[SDS((T, B_WIDTH), BF16), SDS((nc,) + _STATE_SHAPE, F32)],
        scratch_shapes=[pltpu.VMEM(_STATE_SHAPE, F32)], sem=("arbitrary",),
        args=(xbc, xbc, xbc, dtr, h, dt_bias, a_log, d_skip, norm_g), rider=rider)
    return outs if rider is None else (outs, rider_outs)


def ssd_bwd(xbc, dtr, h, states, dy, dt_bias, a_log, d_skip, norm_g, dh, rider=None):
    T = xbc.shape[0]
    nc = T // CHUNK

    def rev(width, idx):
        return pl.BlockSpec((CHUNK, width), lambda j: (nc - 1 - j, idx))

    def body(xs_ref, b_ref, c_ref, dt_ref, z_ref, st_ref, dy_ref, p0, p1, p2, p3, dh_ref,
             dxbc_ref, ddt_ref, dz_ref, g0, g1, g2, g3, dstate):
        @pl.when(pl.program_id(0) == 0)
        def _():
            dstate[...] = jnp.zeros_like(dstate)
            for gref in (g0, g1, g2, g3):
                gref[...] = jnp.zeros_like(gref)

        _, vjp = jax.vjp(_ssd_chunk, xs_ref[...], b_ref[...], c_ref[...], dt_ref[...], z_ref[...], st_ref[0],
                         p0[...], p1[...], p2[...], p3[...])
        dxs, dbm, dcm, ddt, dz, dprev, d0, d1, d2, d3 = vjp((dy_ref[...], dstate[...]))
        dxbc_ref[:, :B_WIDTH] = dxs
        dxbc_ref[:, B_WIDTH:B_WIDTH + gn] = dbm
        dxbc_ref[:, B_WIDTH + gn:] = dcm
        ddt_ref[...] = ddt.astype(BF16)
        dz_ref[...] = dz.astype(BF16)
        dstate[...] = dprev
        g0[...] += d0
        g1[...] += d1
        g2[...] += d2
        g3[...] += d3

    gn = B_GROUPS * B_STATE
    outs, rider_outs = _call(
        body, name="ssd_bwd", grid=(nc,),
        in_specs=[rev(B_WIDTH, 0), rev(gn, 2), rev(gn, 3), rev(LANES, 0), rev(B_WIDTH, 3),
                  pl.BlockSpec((1,) + _STATE_SHAPE, lambda j: (nc - 1 - j, 0, 0)), rev(B_WIDTH, 1)]
        + [_full(s) for s in _SSD_PARAM_SHAPES] + [ANY],
        out_specs=[rev(B_XBC, 0), rev(LANES, 0), rev(B_WIDTH, 3)] + [_full(s) for s in _SSD_PARAM_SHAPES],
        out_shape=[SDS((T, B_XBC), F32), SDS((T, LANES), BF16), SDS(dh.shape, dh.dtype)]
        + [SDS(s, F32) for s in _SSD_PARAM_SHAPES],
        scratch_shapes=[pltpu.VMEM(_STATE_SHAPE, F32)], sem=("arbitrary",),
        args=(xbc, xbc, xbc, dtr, h, states, dy, dt_bias, a_log, d_skip, norm_g, dh), rider=rider, aliases={11: 2})
    return outs if rider is None else (outs, rider_outs)


ATT_SCALE = D_HEAD_DIM ** -0.5
Q_COL, K_COL, V_COL, Z_COL = (4 * C_WIDTH // LANES + i * D_HEADS for i in range(4))


ATT_NBLK = ATT_SUPER // ATT_BLOCK


def _res_rows(r, first, count, dil):
    return pl.ds(r + dil * first, count) if dil == 1 else pl.ds(r + dil * first, count, stride=dil)


def _blocks(ref, dil, dtype=None):
    n = ATT_SUPER // dil
    parts = []
    for r in range(dil):
        v = ref[_res_rows(r, 0, n, dil), :]
        parts.append((v if dtype is None else v.astype(dtype)).reshape(n // ATT_BLOCK, ATT_BLOCK, D_HEAD_DIM))
    return parts[0] if dil == 1 else jnp.concatenate(parts, axis=0)


def _blocks_before(cur_blocks, prev_ref, dil, dtype):
    n = ATT_SUPER // dil
    nb = n // ATT_BLOCK
    parts = []
    for r in range(dil):
        edge = prev_ref[_res_rows(r, n - ATT_BLOCK, ATT_BLOCK, dil), :].astype(dtype)
        parts.append(edge.reshape(1, ATT_BLOCK, D_HEAD_DIM))
        if nb > 1:
            parts.append(cur_blocks[r * nb:(r + 1) * nb - 1])
    return jnp.concatenate(parts, axis=0)


def _unblock(ref, val, dil, add=False):
    n = ATT_SUPER // dil
    nb = n // ATT_BLOCK
    for r in range(dil):
        v = val[r * nb:(r + 1) * nb].reshape(n, D_HEAD_DIM)
        if add:
            ref[_res_rows(r, 0, n, dil), :] += v
        else:
            ref[_res_rows(r, 0, n, dil), :] = v


def _att_masks(dil, edge_ok):
    shape = (ATT_NBLK, ATT_BLOCK, ATT_BLOCK)
    blk = lax.broadcasted_iota(jnp.int32, shape, 0)
    row = lax.broadcasted_iota(jnp.int32, shape, 1)
    col = lax.broadcasted_iota(jnp.int32, shape, 2)
    nb = ATT_NBLK // dil
    at_edge = (blk % nb) == 0
    return col <= row, jnp.logical_and(col >= row, jnp.logical_or(jnp.logical_not(at_edge), edge_ok))


def _bdot_nt(a, b):
    return lax.dot_general(a, b, (((2,), (2,)), ((0,), (0,))), preferred_element_type=F32)


def _bdot_nn(a, b):
    return lax.dot_general(a, b, (((2,), (1,)), ((0,), (0,))), preferred_element_type=F32)


def _bdot_tn(a, b):
    return lax.dot_general(a, b, (((1,), (1,)), ((0,), (0,))), preferred_element_type=F32)


def _att_spec(col0, shift=0, last=None):
    def imap(hh, n):
        m = n + shift
        if shift < 0:
            m = jnp.maximum(m, 0)
        if shift > 0:
            m = jnp.minimum(m, last)
        return (m, col0 + hh)
    return pl.BlockSpec((ATT_SUPER, D_HEAD_DIM), imap)


def _att_out_spec():
    return pl.BlockSpec((ATT_SUPER, D_HEAD_DIM), lambda hh, n: (n, hh))


def attn_fwd(h):
    T = h.shape[0]
    npat = len(D_PATTERNS)

    def body(q_ref, kc_ref, kp_ref, vc_ref, vp_ref, z_ref, yd_ref, o_ref, lse_ref, *scratch):
        o_s, l_s = scratch[:npat], scratch[npat:]
        has_prev = pl.program_id(1) > 0
        for pi, (_, dil) in enumerate(D_PATTERNS):
            mask_c, mask_p = _att_masks(dil, has_prev)
            q = _blocks(q_ref, dil, BF16)
            kc, vc = _blocks(kc_ref, dil, BF16), _blocks(vc_ref, dil, BF16)
            kp, vp = _blocks_before(kc, kp_ref, dil, BF16), _blocks_before(vc, vp_ref, dil, BF16)
            s_c = jnp.where(mask_c, _bdot_nt(q, kc) * ATT_SCALE, -jnp.inf)
            s_p = jnp.where(mask_p, _bdot_nt(q, kp) * ATT_SCALE, -jnp.inf)
            m = jnp.maximum(jnp.max(s_c, axis=-1, keepdims=True), jnp.max(s_p, axis=-1, keepdims=True))
            p_c = jnp.exp(s_c - m)
            p_p = jnp.exp(s_p - m)
            l = jnp.sum(p_c, axis=-1, keepdims=True) + jnp.sum(p_p, axis=-1, keepdims=True)
            o = _bdot_nn((p_c / l).astype(BF16), vc) + _bdot_nn((p_p / l).astype(BF16), vp)
            _unblock(o_s[pi], o, dil)
            _unblock(l_s[pi], jnp.broadcast_to(m + jnp.log(l), o.shape), dil)
        lses = [l_s[pi][...] for pi in range(npat)]
        mx = functools.reduce(jnp.maximum, lses)
        ws = [jnp.exp(l - mx) for l in lses]
        den = functools.reduce(lambda a, b: a + b, ws)
        o = functools.reduce(lambda a, b: a + b, [(w / den) * o_s[pi][...] for pi, w in enumerate(ws)])
        o_ref[...] = o
        lse_ref[...] = mx + jnp.log(den)
        yd_ref[...] = (_silu(z_ref[...]) * o).astype(BF16)

    n_super = T // ATT_SUPER
    return pl.pallas_call(
        body, name="attn_fwd", grid=(D_HEADS, n_super),
        in_specs=[_att_spec(Q_COL), _att_spec(K_COL), _att_spec(K_COL, -1), _att_spec(V_COL), _att_spec(V_COL, -1), _att_spec(Z_COL)],
        out_specs=[_att_out_spec()] * 3,
        out_shape=[SDS((T, D_HEADS * D_HEAD_DIM), BF16), SDS((T, D_HEADS * D_HEAD_DIM), F32), SDS((T, D_HEADS * D_HEAD_DIM), F32)],
        scratch_shapes=[pltpu.VMEM((ATT_SUPER, D_HEAD_DIM), F32)] * (2 * npat),
        compiler_params=_cp("parallel", "arbitrary"),
    )(h, h, h, h, h, h)


def _dsilu(z):
    s = jax.nn.sigmoid(z)
    return s * (1.0 + z * (1.0 - s))


def attn_bwd(h, o, lse, dy):
    T = h.shape[0]
    last = T // ATT_SUPER - 1
    dy_col = C_WIDTH // LANES

    def spec(col0, shift=0):
        return pl.BlockSpec((ATT_SUPER, D_HEAD_DIM), lambda hh, j: (jnp.maximum(last - j + shift, 0), col0 + hh))

    def add_before(ref, carry_ref, val, dil):
        n = ATT_SUPER // dil
        nb = n // ATT_BLOCK
        for r in range(dil):
            carry_ref[_res_rows(r, n - ATT_BLOCK, ATT_BLOCK, dil), :] += val[r * nb]
            if nb > 1:
                ref[_res_rows(r, 0, n - ATT_BLOCK, dil), :] += val[r * nb + 1:(r + 1) * nb].reshape(n - ATT_BLOCK, D_HEAD_DIM)

    def body(q_ref, kc_ref, kp_ref, vc_ref, vp_ref, z_ref, o_ref, lse_ref, dy_ref, dq_ref, dk_ref, dv_ref, dz_ref,
             do_s, dd_s, dq_s, dk_s, dv_s, dk_carry, dv_carry):
        j = pl.program_id(1)
        has_prev = j < last

        @pl.when(j == 0)
        def _():
            dk_carry[...] = jnp.zeros_like(dk_carry)
            dv_carry[...] = jnp.zeros_like(dv_carry)

        dk_s[...] = dk_carry[...]
        dv_s[...] = dv_carry[...]
        dk_carry[...] = jnp.zeros_like(dk_carry)
        dv_carry[...] = jnp.zeros_like(dv_carry)
        z, oo, dyd = z_ref[...], o_ref[...], dy_ref[...]
        do = dyd * _silu(z)
        dz_ref[...] = (dyd * oo * _dsilu(z)).astype(BF16)
        do_s[...] = do
        dd_s[...] = jnp.broadcast_to(jnp.sum(do * oo, axis=-1, keepdims=True), (ATT_SUPER, D_HEAD_DIM))
        for pi, (_, dil) in enumerate(D_PATTERNS):
            mask_c, mask_p = _att_masks(dil, has_prev)
            q = _blocks(q_ref, dil, BF16)
            kc, vc = _blocks(kc_ref, dil, BF16), _blocks(vc_ref, dil, BF16)
            kp, vp = _blocks_before(kc, kp_ref, dil, BF16), _blocks_before(vc, vp_ref, dil, BF16)
            lse_b, dd_b, do_b = _blocks(lse_ref, dil), _blocks(dd_s, dil), _blocks(do_s, dil, BF16)
            p_c = jnp.where(mask_c, jnp.exp(_bdot_nt(q, kc) * ATT_SCALE - lse_b), 0.0)
            p_p = jnp.where(mask_p, jnp.exp(_bdot_nt(q, kp) * ATT_SCALE - lse_b), 0.0)
            ds_c = (p_c * (_bdot_nt(do_b, vc) - dd_b) * ATT_SCALE).astype(BF16)
            ds_p = (p_p * (_bdot_nt(do_b, vp) - dd_b) * ATT_SCALE).astype(BF16)
            _unblock(dq_s, _bdot_nn(ds_c, kc) + _bdot_nn(ds_p, kp), dil, add=pi > 0)
            _unblock(dk_s, _bdot_tn(ds_c, q), dil, add=True)
            _unblock(dv_s, _bdot_tn(p_c.astype(BF16), do_b), dil, add=True)
            add_before(dk_s, dk_carry, _bdot_tn(ds_p, q), dil)
            add_before(dv_s, dv_carry, _bdot_tn(p_p.astype(BF16), do_b), dil)
        dq_ref[...] = dq_s[...].astype(BF16)
        dk_ref[...] = dk_s[...].astype(BF16)
        dv_ref[...] = dv_s[...].astype(BF16)

    blk = (ATT_SUPER, D_HEAD_DIM)
    out = pl.BlockSpec(blk, lambda hh, j: (last - j, hh))
    return pl.pallas_call(
        body, name="attn_bwd", grid=(D_HEADS, last + 1),
        in_specs=[spec(Q_COL), spec(K_COL), spec(K_COL, -1), spec(V_COL), spec(V_COL, -1), spec(Z_COL), spec(0), spec(0), spec(dy_col)],
        out_specs=[out] * 4, out_shape=[SDS((T, D_HEADS * D_HEAD_DIM), BF16)] * 4,
        scratch_shapes=[pltpu.VMEM(blk, F32)] * 7, compiler_params=_cp("parallel", "arbitrary"),
    )(h, h, h, h, h, h, o, lse, dy)


ANY = pl.BlockSpec(memory_space=pl.ANY)
COMM_PARAMS = pltpu.CompilerParams()


def _place():
    x, y, c = lax.axis_index("x"), lax.axis_index("y"), lax.axis_index("c")
    return x, y, c, [(1 - x, y), (x, 1 - y), (1 - x, 1 - y)]


def _rcopy(src, dst, ssem, rsem, dev):
    return pltpu.make_async_remote_copy(src_ref=src, dst_ref=dst, send_sem=ssem, recv_sem=rsem, device_id=dev, device_id_type=MESH)


def gather_rider(arrs, fractions=(0.0, 0.6, 1.0)):
    n = len(arrs)
    per = 7

    def to_chips(ins, outs, ssem, rsem):
        x, y, c, chips = _place()
        return [_rcopy(ins[a].at[c], outs[a].at[2 * x + y, c], ssem.at[per * a + j], rsem.at[per * a + j], (px, py, c))
                for a in range(n) for j, (px, py) in enumerate(chips)]

    def passed_on(outs, ssem, rsem, half):
        x, y, c, chips = _place()
        cps = []
        for a in range(n):
            for j, (px, py) in enumerate(chips):
                slot = outs[a].at[2 * px + py, half(c)]
                cps.append(_rcopy(slot, slot, ssem.at[per * a + 3 + j], rsem.at[per * a + 3 + j], (x, y, 1 - c)))
        return cps

    def own(ins, outs, ssem, rsem):
        x, y, c, _ = _place()
        return [_rcopy(ins[a], outs[a].at[2 * x + y], ssem.at[per * a + 6], rsem.at[per * a + 6], (x, y, 1 - c)) for a in range(n)]

    def start(ins, outs, ssem, rsem):
        for cp in to_chips(ins, outs, ssem, rsem) + own(ins, outs, ssem, rsem):
            cp.start()

    def pass_on(ins, outs, ssem, rsem):
        x, y, c, chips = _place()
        landed = [_rcopy(outs[a].at[2 * px + py, c], outs[a].at[2 * px + py, c], ssem.at[per * a + j], rsem.at[per * a + j], (px, py, c))
                  for a in range(n) for j, (px, py) in enumerate(chips)]
        for arrival, cp in zip(landed, passed_on(outs, ssem, rsem, lambda c: c)):
            arrival.wait_recv()
            cp.start()

    def finish(ins, outs, ssem, rsem):
        for cp in passed_on(outs, ssem, rsem, lambda c: 1 - c):
            cp.wait_recv()
        for cp in to_chips(ins, outs, ssem, rsem) + passed_on(outs, ssem, rsem, lambda c: c):
            cp.wait_send()
        for cp in own(ins, outs, ssem, rsem):
            cp.wait()

    return Rider(arrs, [SDS((N_CHIPS,) + a.shape, a.dtype) for a in arrs], per * n,
                 [(fractions[0], start), (fractions[1], pass_on), (fractions[2], finish)])


def _copies_rider(ins, out_shapes, n_sems, make):
    def start(*refs):
        for cp in make(*refs):
            cp.start()

    def finish(*refs):
        for cp in make(*refs):
            cp.wait()

    return Rider(ins, out_shapes, n_sems, [(0.0, start), (1.0, finish)])


def swap_halves_rider(arrs):
    def make(ins, outs, ssem, rsem):
        x, y, c, _ = _place()
        return [_rcopy(ins[a].at[1 - c], outs[a], ssem.at[a], rsem.at[a], (x, y, 1 - c)) for a in range(len(arrs))]
    return _copies_rider(arrs, [SDS(a.shape[1:], a.dtype) for a in arrs], len(arrs), make)


def scatter_rider(arrs):
    def make(ins, outs, ssem, rsem):
        x, y, c, chips = _place()
        return [_rcopy(ins[a].at[2 * px + py], outs[a].at[j], ssem.at[3 * a + j], rsem.at[3 * a + j], (px, py, c))
                for a in range(len(arrs)) for j, (px, py) in enumerate(chips)]
    return _copies_rider(arrs, [SDS((N_CHIPS - 1,) + a.shape[1:], a.dtype) for a in arrs], 3 * len(arrs), make)


def join_halves_rider(arrs):
    def make(ins, outs, ssem, rsem):
        x, y, c, _ = _place()
        return [_rcopy(ins[a].at[c], outs[a].at[c], ssem.at[a], rsem.at[a], (x, y, 1 - c)) for a in range(len(arrs))]

    def start(*refs):
        for cp in make(*refs):
            cp.start()

    def finish(ins, outs, ssem, rsem):
        x, y, c, _ = _place()
        for a, cp in enumerate(make(ins, outs, ssem, rsem)):
            cp.wait_send()
            _rcopy(ins[a].at[1 - c], outs[a].at[1 - c], ssem.at[a], rsem.at[a], (x, y, 1 - c)).wait_recv()

    return Rider(arrs, [SDS(a.shape, a.dtype) for a in arrs], len(arrs), [(0.0, start), (1.0, finish)], in_place=True)


def gather_all(buf):
    def body(in_ref, out_ref, ssem, rsem, lsem):
        x, y, c, _ = _place()
        me = 4 * x + 2 * y + c
        local = pltpu.make_async_copy(in_ref, out_ref.at[me], lsem)
        local.start()
        flips = [(a, b, e) for a in (0, 1) for b in (0, 1) for e in (0, 1)][1:]
        cps = []
        for i, (a, b, e) in enumerate(flips):
            peer = (x ^ a, y ^ b, c ^ e)
            cps.append(_rcopy(in_ref, out_ref.at[me], ssem.at[i], rsem.at[i], peer))
        for cp in cps:
            cp.start()
        for i, (a, b, e) in enumerate(flips):
            cps[i].wait_send()
            slot = out_ref.at[4 * (x ^ a) + 2 * (y ^ b) + (c ^ e)]
            _rcopy(slot, slot, ssem.at[i], rsem.at[i], (x ^ a, y ^ b, c ^ e)).wait_recv()
        local.wait()

    return pl.pallas_call(
        body, name="comm_gather_all", in_specs=[ANY], out_specs=ANY, out_shape=SDS((N_DEV,) + buf.shape, buf.dtype),
        scratch_shapes=[pltpu.SemaphoreType.DMA((N_DEV - 1,)), pltpu.SemaphoreType.DMA((N_DEV - 1,)), pltpu.SemaphoreType.DMA],
        compiler_params=COMM_PARAMS,
    )(buf)


def _pack_offsets(parts):
    offs, r = [], 0
    for p in parts:
        offs.append(r)
        r += -(-p.shape[0] // SUBLANES) * SUBLANES
    return offs, r


def pack_rows(parts):
    offs, total = _pack_offsets(parts)

    def body(*refs):
        out = refs[-1]
        out[...] = jnp.zeros_like(out)
        for ref, off in zip(refs[:-1], offs):
            out[off:off + ref.shape[0], :] = ref[...]

    vmem = pl.BlockSpec(memory_space=pltpu.VMEM)
    return pl.pallas_call(body, name="pack_small", in_specs=[vmem] * len(parts), out_specs=vmem,
                          out_shape=SDS((total, LANES), F32))(*parts)


EVEN_SHARD = IN_EVEN // N_CHIPS


def wie_from_shards(g):
    tr = 256

    def body(g_ref, main_ref, dt_ref):
        full = jnp.concatenate([g_ref[k] for k in range(N_CHIPS)], axis=1)
        main_ref[...] = full[:, :EVEN_MAIN]
        dt_ref[...] = jnp.concatenate([full[:, EVEN_MAIN:], jnp.zeros((tr, LANES - B_HEADS), full.dtype)], axis=1)

    return pl.pallas_call(
        body, name="wie_from_shards", grid=(D_MODEL // tr,),
        in_specs=[pl.BlockSpec((N_CHIPS, tr, EVEN_SHARD), lambda i: (0, i, 0))],
        out_specs=[pl.BlockSpec((tr, EVEN_MAIN), lambda i: (i, 0)), pl.BlockSpec((tr, LANES), lambda i: (i, 0))],
        out_shape=[SDS((D_MODEL, EVEN_MAIN), g.dtype), SDS((D_MODEL, LANES), g.dtype)], compiler_params=_cp("parallel"))(g)


def wie_grad_to_pieces(main, dt):
    tr = 128
    per_half = D_MODEL // 2 // tr

    def body(m_ref, d_ref, o_ref):
        full = jnp.concatenate([m_ref[...], d_ref[:, :B_HEADS]], axis=1)
        for k in range(N_CHIPS):
            o_ref[0, k] = full[:, k * EVEN_SHARD:(k + 1) * EVEN_SHARD]

    return pl.pallas_call(
        body, name="wie_grad_to_pieces", grid=(2, per_half),
        in_specs=[pl.BlockSpec((tr, EVEN_MAIN), lambda c, i: (c * per_half + i, 0)), pl.BlockSpec((tr, LANES), lambda c, i: (c * per_half + i, 0))],
        out_specs=pl.BlockSpec((1, N_CHIPS, tr, EVEN_SHARD), lambda c, i: (c, 0, i, 0)),
        out_shape=SDS((2, N_CHIPS, D_MODEL // 2, EVEN_SHARD), F32), compiler_params=_cp("parallel", "parallel"))(main, dt)


def _unpack(buf, parts):
    offs, _ = _pack_offsets(parts)
    return [buf[off:off + p.shape[0]] for p, off in zip(parts, offs)]


def _pad_lanes(v):
    v = v.reshape(1, -1)
    return jnp.pad(v, ((0, 0), (0, LANES - v.shape[1])))


def _as2d(a):
    return a.reshape(1, -1) if a.ndim == 1 else a.reshape(-1, a.shape[-1])


def kernel(x, even_norm_g, even_w_in, gmlp_ln_g, gmlp_ln_b, gmlp_ws, gmlp_bs, ssd_conv_w, ssd_conv_b, ssd_dt_bias, ssd_a_log, ssd_d, ssd_norm_g, even_w_out, odd_norm_g, odd_w_in, sconv_w, odd_w_out, final_norm_g, loss_target, m_even_norm_g, m_even_w_in, m_gmlp_ln_g, m_gmlp_ln_b, m_gmlp_ws, m_gmlp_bs, m_ssd_conv_w, m_ssd_conv_b, m_ssd_dt_bias, m_ssd_a_log, m_ssd_d, m_ssd_norm_g, m_even_w_out, m_odd_norm_g, m_odd_w_in, m_sconv_w, m_odd_w_out, m_final_norm_g, v_even_norm_g, v_even_w_in, v_gmlp_ln_g, v_gmlp_ln_b, v_gmlp_ws, v_gmlp_bs, v_ssd_conv_w, v_ssd_conv_b, v_ssd_dt_bias, v_ssd_a_log, v_ssd_d, v_ssd_norm_g, v_even_w_out, v_odd_norm_g, v_odd_w_in, v_sconv_w, v_odd_w_out, v_final_norm_g):
    weights = dict(even_norm_g=even_norm_g, even_w_in=even_w_in, gmlp_ln_g=gmlp_ln_g, gmlp_ln_b=gmlp_ln_b, gmlp_ws=gmlp_ws, gmlp_bs=gmlp_bs, ssd_conv_w=ssd_conv_w, ssd_conv_b=ssd_conv_b, ssd_dt_bias=ssd_dt_bias, ssd_a_log=ssd_a_log, ssd_d=ssd_d, ssd_norm_g=ssd_norm_g, even_w_out=even_w_out, odd_norm_g=odd_norm_g, odd_w_in=odd_w_in, sconv_w=sconv_w, odd_w_out=odd_w_out, final_norm_g=final_norm_g)
    moms_m = dict(even_norm_g=m_even_norm_g, even_w_in=m_even_w_in, gmlp_ln_g=m_gmlp_ln_g, gmlp_ln_b=m_gmlp_ln_b, gmlp_ws=m_gmlp_ws, gmlp_bs=m_gmlp_bs, ssd_conv_w=m_ssd_conv_w, ssd_conv_b=m_ssd_conv_b, ssd_dt_bias=m_ssd_dt_bias, ssd_a_log=m_ssd_a_log, ssd_d=m_ssd_d, ssd_norm_g=m_ssd_norm_g, even_w_out=m_even_w_out, odd_norm_g=m_odd_norm_g, odd_w_in=m_odd_w_in, sconv_w=m_sconv_w, odd_w_out=m_odd_w_out, final_norm_g=m_final_norm_g)
    moms_v = dict(even_norm_g=v_even_norm_g, even_w_in=v_even_w_in, gmlp_ln_g=v_gmlp_ln_g, gmlp_ln_b=v_gmlp_ln_b, gmlp_ws=v_gmlp_ws, gmlp_bs=v_gmlp_bs, ssd_conv_w=v_ssd_conv_w, ssd_conv_b=v_ssd_conv_b, ssd_dt_bias=v_ssd_dt_bias, ssd_a_log=v_ssd_a_log, ssd_d=v_ssd_d, ssd_norm_g=v_ssd_norm_g, even_w_out=v_even_w_out, odd_norm_g=v_odd_norm_g, odd_w_in=v_odd_w_in, sconv_w=v_sconv_w, odd_w_out=v_odd_w_out, final_norm_g=v_final_norm_g)
    names = list(weights)

    xs = x[0]
    tgt = loss_target[0]
    T = xs.shape[0]
    chip = 2 * lax.axis_index("x") + lax.axis_index("y")
    core = lax.axis_index("c")
    cshard = B_XBC // N_CHIPS
    dshard = D_MODEL // N_CHIPS

    def halves(w):
        return w.astype(BF16).reshape(2, w.shape[0] // 2, w.shape[1])

    small_shard = jnp.concatenate([ssd_conv_w[0].reshape(-1), odd_norm_g[0], sconv_w[0].reshape(-1)])
    g_wie, g_small = run_rider(gather_rider([halves(even_w_in[0]), small_shard.reshape(2, -1, LANES)]), name="comm_gather_first")
    wie_main, wie_dt = wie_from_shards(g_wie.reshape(N_CHIPS, D_MODEL, EVEN_SHARD))
    g_small = g_small.reshape(N_CHIPS, -1)
    n_cw = B_CONV * cshard
    conv_w = g_small[:, :n_cw].reshape(N_CHIPS, B_CONV, cshard).transpose(1, 0, 2).reshape(B_CONV, B_XBC)
    odd_g = g_small[:, n_cw:n_cw + dshard].reshape(1, D_MODEL)
    sconv = g_small[:, n_cw + dshard:].reshape(N_CHIPS, C_CONV, dshard).transpose(1, 0, 2).reshape(C_CONV, C_WIDTH)

    even_g = even_norm_g
    ln_g, ln_b = gmlp_ln_g, gmlp_ln_b
    ws, bs_t = gmlp_ws[0], gmlp_bs[0].T
    conv_b = ssd_conv_b
    dt_bias, a_log, d_skip = _pad_lanes(ssd_dt_bias), _pad_lanes(ssd_a_log), _pad_lanes(ssd_d)
    norm_g = ssd_norm_g
    fin_g = final_norm_g.reshape(1, D_MODEL)

    xn0 = rmsnorm_fwd(xs, even_g, name="even_norm")
    h0, (g_wio,) = matmul(xn0, wie_main, "nn", name="even_in", rider=gather_rider([halves(odd_w_in[0])], (0.0, 0.88, 1.0)))
    wio = g_wio.reshape(N_CHIPS, D_MODEL, IN_ODD // N_CHIPS)
    dtr = matmul(xn0, wie_dt, "nn", name="even_in_dt", tk=D_MODEL)
    ya = gmlp_fwd(h0, ln_g, ln_b, ws, bs_t)
    xbc = ssd_conv_fwd(h0, conv_w, conv_b)
    (yb, states), (g_woe, g_woo) = ssd_fwd(xbc, dtr, h0, dt_bias, a_log, d_skip, norm_g,
                                           rider=gather_rider([halves(even_w_out[0]), halves(odd_w_out[0])]))
    woe = g_woe.reshape(2 * A_WIDTH, D_MODEL)
    woo = g_woo.reshape(2 * C_WIDTH, D_MODEL)
    y0 = [ya, yb]
    x1 = matmul(y0, woe, "nn", name="even_out", res=xs)

    xn1 = rmsnorm_fwd(x1, odd_g, name="odd_norm")
    h1 = matmul(xn1, wio, "nn", name="odd_in")
    yc = sconv_fwd(h1, sconv)
    yd, att_o, att_lse = attn_fwd(h1)
    y1 = [yc, yd]
    x2 = matmul(y1, woo, "nn", name="odd_out", res=x1)

    loss_part, dx2, dx2b, d_fin_g = loss_head(x2, fin_g, tgt)

    tile = 1024
    rows_layout = ((2, N_CHIPS, ROW_PIECE, D_MODEL), (1, 1, ROW_PIECE, tile), lambda i, j, k: (i % 2, i // 2, 0, j))
    per_chip = IN_ODD // N_CHIPS // tile
    cols_layout = ((2, N_CHIPS, D_MODEL // 2, IN_ODD // N_CHIPS), (1, 1, tile, tile), lambda i, j, k: (i, j // per_chip, 0, j % per_chip))
    dy1 = matmul(dx2b, woo, "nt", name="odd_out_dy")
    d_woo = matmul(y1, dx2b, "tn", name="odd_out_dw", tm=ROW_PIECE, out_layout=rows_layout)
    dbg, dcg, dhx, dzc, d_sconv = sconv_bwd(h1, dy1, sconv)
    dq, dk, dv, dzd = attn_bwd(h1, att_o, att_lse, dy1)
    dh1 = jnp.concatenate([dbg, dcg, dhx, dzc, dq, dk, dv, dzd], axis=1)
    dxn1, (d_woo_sib,) = matmul(dh1, wio, "nt", name="odd_in_dx", rider=swap_halves_rider([d_woo]))
    d_wio = matmul(xn1, dh1, "tn", name="odd_in_dw", out_layout=cols_layout)
    dx1, dx1b, d_odd_g = rmsnorm_bwd(x1, odd_g, dxn1, dx2, name="odd_norm_bwd")

    d_woe = matmul(y0, dx1b, "tn", name="even_out_dw", tm=ROW_PIECE, out_layout=rows_layout)
    first = [d_wio, d_woo, d_woe]
    dy0, (d_wio_sib, d_woe_sib) = matmul(dx1b, woe, "nt", name="even_out_dy", rider=swap_halves_rider([d_wio, d_woe]))
    first_sib = [d_wio_sib, d_woo_sib, d_woe_sib]
    first_sums = [chip_sum(p, s, core, name=f"chip_sum_first_{i}") for i, (p, s) in enumerate(zip(first, first_sib))]
    dh0, d_ln_g, d_ln_b, d_ws, d_bs_t = gmlp_bwd(h0, dy0, ln_g, ln_b, ws, bs_t)
    (dxbc_act, ddtr, dh0, d_dt_bias, d_a_log, d_d, d_norm_g), first_landed = ssd_bwd(
        xbc, dtr, h0, states, dy0, dt_bias, a_log, d_skip, norm_g, dh0, rider=scatter_rider(first_sums))
    first_totals = [total_sum(s, l, chip, core, name=f"total_first_{i}") for i, (s, l) in enumerate(zip(first_sums, first_landed))]
    (dh0, d_conv_w, d_conv_b), first_joined = ssd_conv_bwd(h0, dxbc_act, conv_w, conv_b, dh0, rider=join_halves_rider(first_totals))
    ddtr_b = ddtr
    d_wie_main = matmul(xn0, dh0, "tn", name="even_in_dw")
    d_wie_dt = matmul(xn0, ddtr_b, "tn", name="even_in_dw_dt")
    last = [wie_grad_to_pieces(d_wie_main, d_wie_dt)]
    dxn0, last_sib = matmul(ddtr_b, wie_dt, "nt", name="even_in_dx_dt", rider=swap_halves_rider(last))
    last_sums = [chip_sum(last[0], last_sib[0], core, name="chip_sum_last")]
    dxn0, last_landed = matmul(dh0, wie_main, "nt", name="even_in_dx", res=dxn0, rider=scatter_rider(last_sums))
    last_totals = [total_sum(last_sums[0], last_landed[0], chip, core, name="total_last")]
    grad_x, _, d_even_g = rmsnorm_bwd(xs, even_g, dxn0, dx1, name="even_norm_bwd")

    small_names = ["even_norm_g", "gmlp_ln_g", "gmlp_ln_b", "gmlp_ws", "gmlp_bs", "ssd_conv_w", "ssd_conv_b", "ssd_dt_bias",
                   "ssd_a_log", "ssd_d", "ssd_norm_g", "odd_norm_g", "sconv_w", "final_norm_g"]
    small_parts = [d_even_g, d_ln_g, d_ln_b, d_ws, d_bs_t.T, d_conv_w, d_conv_b, d_dt_bias, d_a_log, d_d, d_norm_g, d_odd_g, d_sconv, d_fin_g]
    small_shapes = [p.shape for p in small_parts]
    small_rows = [p.reshape(-1, LANES) for p in small_parts]
    small_sum, last_joined = sum_leading(gather_all(pack_rows(small_rows)), name="small_sum", rider=join_halves_rider(last_totals))
    full = {nm: rows.reshape(shape) for nm, rows, shape in zip(small_names, _unpack(small_sum, small_rows), small_shapes)}
    joined = last_joined + first_joined
    grads = dict(even_w_in=joined[0].reshape(even_w_in.shape), odd_w_in=joined[1].reshape(odd_w_in.shape),
                 odd_w_out=joined[2].reshape(odd_w_out.shape), even_w_out=joined[3].reshape(even_w_out.shape))
    for nm in small_names:
        g = full[nm]
        if nm in ("ssd_dt_bias", "ssd_a_log", "ssd_d"):
            g = g[:, :B_HEADS]
        elif nm == "ssd_conv_w":
            g = lax.dynamic_slice_in_dim(g, chip * cshard, cshard, axis=1)
        elif nm in ("odd_norm_g", "sconv_w"):
            g = lax.dynamic_slice_in_dim(g, chip * dshard, dshard, axis=1)
        grads[nm] = g.reshape(weights[nm].shape)

    deltas, new_m, new_v = {}, {}, {}
    for nm in names:
        w = weights[nm]
        d, nm_, nv_ = adamw(_as2d(w), _as2d(grads[nm]), _as2d(moms_m[nm]), _as2d(moms_v[nm]), name=f"adamw_{nm}")
        deltas[nm], new_m[nm], new_v[nm] = d.reshape(w.shape), nm_.reshape(w.shape), nv_.reshape(w.shape)

    loss = lax.psum(loss_part[0, 0], ("x", "y", "c"))
    return (loss, grad_x[None], *[grads[n] for n in names], *[deltas[n] for n in names],
            *[new_m[n] for n in names], *[new_v[n] for n in names])
```

```python
import functools

import jax
import jax.numpy as jnp
from jax import lax
from jax.experimental import pallas as pl
from jax.experimental.pallas import tpu as pltpu

F32 = jnp.float32
BF16 = jnp.bfloat16
SDS = jax.ShapeDtypeStruct
MESH = pl.DeviceIdType.MESH

D_MODEL = 2048
A_WIDTH = 2048
A_GROUPS = 8
CHUNK = 128
B_WIDTH = 2048
B_HEADS = 32
B_HEAD_DIM = 64
B_GROUPS = 8
B_STATE = 128
B_CONV = 4
B_XBC = B_WIDTH + 2 * B_GROUPS * B_STATE
C_WIDTH = 2048
C_CONV = 3
D_HEADS = 16
D_HEAD_DIM = 128
D_PATTERNS = ((128, 1), (512, 4), (2048, 16))
ATT_BLOCK = 128
ATT_SUPER = 2048
EVEN_MAIN = 3 * A_WIDTH + B_WIDTH + B_XBC
IN_EVEN = EVEN_MAIN + B_HEADS
IN_ODD = 4 * C_WIDTH + 4 * D_HEADS * D_HEAD_DIM
LANES = 128
SUBLANES = 8
EPS = 1e-5
ADAM_LR = 0.001
ADAM_B1 = 0.9
ADAM_B2 = 0.999
ADAM_EPS = 1e-08
ADAM_WD = 0.01
ADAM_STEP = 10
N_CHIPS = 4
N_DEV = 8
VMEM_LIMIT_BYTES = 56 * 1024 * 1024


def _cp(*sem):
    return pltpu.CompilerParams(dimension_semantics=sem, vmem_limit_bytes=VMEM_LIMIT_BYTES)


def _full(shape):
    return pl.BlockSpec(shape, lambda *_: (0,) * len(shape))


def _silu(x):
    return x * jax.nn.sigmoid(x)


def _dot_nn(a, b):
    return lax.dot_general(a, b, (((1,), (0,)), ((), ())), preferred_element_type=F32)


def _dot_nt(a, b):
    return lax.dot_general(a, b, (((1,), (1,)), ((), ())), preferred_element_type=F32)


def _dot_tn(a, b):
    return lax.dot_general(a, b, (((0,), (0,)), ((), ())), preferred_element_type=F32)


def _tril(n):
    return lax.broadcasted_iota(jnp.int32, (n, n), 0) >= lax.broadcasted_iota(jnp.int32, (n, n), 1)


_DOTS = {"nn": _dot_nn, "nt": _dot_nt, "tn": _dot_tn}


class Rider:
    def __init__(self, ins, out_shapes, n_sems, phases, in_place=False):
        self.ins, self.out_shapes, self.n_sems, self.phases = list(ins), list(out_shapes), n_sems, list(phases)
        self.in_place = in_place


def _call(body, *, name, grid, in_specs, out_specs, out_shape, scratch_shapes, sem, args, rider=None, aliases=None):
    in_specs, out_specs, out_shape, scratch_shapes = list(in_specs), list(out_specs), list(out_shape), list(scratch_shapes)
    aliases = dict(aliases or {})
    if rider is None:
        res = pl.pallas_call(body, name=name, grid=grid, in_specs=in_specs, out_specs=out_specs, out_shape=out_shape,
                             scratch_shapes=scratch_shapes, input_output_aliases=aliases, compiler_params=_cp(*sem))(*args)
        return list(res), []
    if rider.in_place:
        aliases.update({len(in_specs) + k: len(out_specs) + k for k in range(len(rider.ins))})
    counts = [len(in_specs), len(rider.ins), len(out_specs), len(rider.out_shapes), len(scratch_shapes), 2]
    total = 1
    for g in grid:
        total *= g

    def wrapped(*refs):
        groups, pos = [], 0
        for n in counts:
            groups.append(refs[pos:pos + n])
            pos += n
        ins, rins, outs, routs, scr, (ssem, rsem) = groups
        step = 0
        for d, g in enumerate(grid):
            step = step * g + pl.program_id(d)
        for frac, fn in rider.phases:
            @pl.when(step == min(int(frac * total), total - 1))
            def _(fn=fn):
                fn(rins, routs, ssem, rsem)
        body(*ins, *outs, *scr)

    dma = pltpu.SemaphoreType.DMA((rider.n_sems,))
    res = pl.pallas_call(
        wrapped, name=name, grid=grid, in_specs=in_specs + [ANY] * len(rider.ins), out_specs=out_specs + [ANY] * len(rider.out_shapes),
        out_shape=out_shape + rider.out_shapes, scratch_shapes=scratch_shapes + [dma, dma], input_output_aliases=aliases,
        compiler_params=_cp(*(("arbitrary",) * len(grid))))(*args, *rider.ins)
    return list(res[:len(out_specs)]), list(res[len(out_specs):])


def run_rider(rider, *, name):
    def body(*refs):
        n_in, n_out = len(rider.ins), len(rider.out_shapes)
        ins, outs, (ssem, rsem) = refs[:n_in], refs[n_in:n_in + n_out], refs[n_in + n_out:]
        for _, fn in rider.phases:
            fn(ins, outs, ssem, rsem)

    dma = pltpu.SemaphoreType.DMA((rider.n_sems,))
    return list(pl.pallas_call(body, name=name, in_specs=[ANY] * len(rider.ins), out_specs=[ANY] * len(rider.out_shapes),
                               out_shape=rider.out_shapes, scratch_shapes=[dma, dma])(*rider.ins))


def matmul(a, b, mode, *, name, out_dtype=F32, res=None, tm=1024, tn=1024, tk=2048, out_layout=None, rider=None):
    a_parts = list(a) if isinstance(a, (list, tuple)) else [a]
    b_parts = list(b) if isinstance(b, (list, tuple)) else [b]
    assert len(b_parts) == 1 or mode == "tn"
    b0 = b_parts[0]
    shards = b0.shape[0] if b0.ndim == 3 else 1
    a_rows, a_cols = a_parts[0].shape[0], sum(p.shape[1] for p in a_parts)
    b_rows, b_cols = b0.shape[-2], sum(p.shape[-1] for p in b_parts) * shards
    if mode == "tn":
        (K, M), (K2, N) = (a_rows, a_cols), (b_rows, b_cols)
    elif mode == "nt":
        (M, K), (N, K2) = (a_rows, a_cols), (b_rows, b_cols)
    else:
        (M, K), (K2, N) = (a_rows, a_cols), (b_rows, b_cols)
    assert K == K2, (mode, K, K2)
    tm, tn, tk = min(tm, M), min(tn, N), min(tk, K)
    if shards > 1:
        assert mode != "tn" and b0.shape[-1] % (tk if mode == "nt" else tn) == 0
    assert M % tm == 0 and N % tn == 0 and K % tk == 0, (M, N, K, tm, tn, tk)
    nk = K // tk
    dot = _DOTS[mode]

    def spans(parts, tile):
        out, off = [], 0
        for p in parts:
            assert p.shape[-1] % tile == 0, (p.shape, tile)
            out.append((off, p.shape[-1] // tile))
            off += p.shape[-1] // tile
        return out

    a_axis = 0 if mode == "tn" else 2
    a_spans = spans(a_parts, tm if mode == "tn" else tk)
    b_spans = spans(b_parts, tn) if len(b_parts) > 1 else [(0, N // tn)]

    def inside(t, span):
        return jnp.logical_and(t >= span[0], t < span[0] + span[1])

    def body(*refs):
        a_refs, b_refs, rest = refs[:len(a_parts)], refs[len(a_parts):len(a_parts) + len(b_parts)], refs[len(a_parts) + len(b_parts):]
        r_ref = rest[0] if res is not None else None
        o_ref = rest[1] if res is not None else rest[0]
        k = pl.program_id(2)

        def finish(acc):
            if res is not None:
                acc = acc + r_ref[...]
            o_ref[...] = acc.astype(o_ref.dtype).reshape(o_ref.shape)

        def emit(a_ref, b_ref, conds, k_lo, k_hi):
            def region(*more):
                cs = conds + list(more)
                return pl.when(functools.reduce(jnp.logical_and, cs)) if cs else (lambda f: f())

            def product():
                return dot(a_ref[...], b_ref[0] if shards > 1 else b_ref[...])

            if nk == 1:
                region()(lambda: finish(product()))
                return
            acc_ref = rest[-1]
            if k_lo == 0:
                @region(k == 0)
                def _():
                    acc_ref[...] = product()

            if max(k_lo, 1) < min(k_hi, nk - 1):
                @region(k > 0, k < nk - 1)
                def _():
                    acc_ref[...] += product()

            if k_hi == nk:
                @region(k == nk - 1)
                def _():
                    finish(acc_ref[...] + product())

        for a_ref, a_span in zip(a_refs, a_spans):
            for b_ref, b_span in zip(b_refs, b_spans):
                by_k = a_axis == 2 and len(a_parts) > 1
                emit(a_ref, b_ref, ([inside(pl.program_id(a_axis), a_span)] if len(a_parts) > 1 else []) +
                     ([inside(pl.program_id(1), b_span)] if len(b_parts) > 1 else []),
                     a_span[0] if by_k else 0, a_span[0] + a_span[1] if by_k else nk)

    def piece_index(t, span):
        return jnp.clip(t - span[0], 0, span[1] - 1)

    def a_spec_of(span):
        if len(a_parts) == 1:
            return pl.BlockSpec((tk, tm), lambda i, j, k: (k, i)) if mode == "tn" else pl.BlockSpec((tm, tk), lambda i, j, k: (i, k))
        if mode == "tn":
            return pl.BlockSpec((tk, tm), lambda i, j, k: (jnp.where(inside(i, span), k, 0), piece_index(i, span)))
        return pl.BlockSpec((tm, tk), lambda i, j, k: (i, piece_index(k, span)))

    def b_spec_of(span):
        if shards > 1 and mode == "nn":
            per = b0.shape[-1] // tn
            return pl.BlockSpec((1, tk, tn), lambda i, j, k: (j // per, k, j % per))
        if shards > 1:
            per = b0.shape[-1] // tk
            return pl.BlockSpec((1, tn, tk), lambda i, j, k: (k // per, j, k % per))
        if mode == "nt":
            return pl.BlockSpec((tn, tk), lambda i, j, k: (j, k))
        if len(b_parts) == 1:
            return pl.BlockSpec((tk, tn), lambda i, j, k: (k, j))
        return pl.BlockSpec((tk, tn), lambda i, j, k: (jnp.where(inside(j, span), k, 0), piece_index(j, span)))

    o_spec = pl.BlockSpec((tm, tn), lambda i, j, k: (i, j))
    in_specs = [a_spec_of(s) for s in a_spans] + [b_spec_of(s) for s in b_spans]
    args = a_parts + b_parts
    if res is not None:
        in_specs.append(o_spec)
        args.append(res)
    out_shape = SDS((M, N), out_dtype)
    if out_layout is not None:
        out_shape, o_spec = SDS(out_layout[0], out_dtype), pl.BlockSpec(out_layout[1], out_layout[2])
    outs, rider_outs = _call(
        body, name=name, grid=(M // tm, N // tn, nk), in_specs=in_specs, out_specs=[o_spec], out_shape=[out_shape],
        scratch_shapes=[pltpu.VMEM((tm, tn), F32)] if nk > 1 else [], sem=("parallel", "parallel", "arbitrary"),
        args=args, rider=rider)
    return outs[0] if rider is None else (outs[0], rider_outs)


ROW_TILE = 512
ROW_PIECE = 512


def _rms(x, g):
    return x * lax.rsqrt(jnp.mean(x * x, axis=-1, keepdims=True) + EPS) * g


def rmsnorm_fwd(x, g, *, name):
    T, D = x.shape

    def body(x_ref, g_ref, o_ref):
        o_ref[...] = _rms(x_ref[...], g_ref[...]).astype(BF16)

    row = pl.BlockSpec((ROW_TILE, D), lambda i: (i, 0))
    return pl.pallas_call(body, name=name, grid=(T // ROW_TILE,), in_specs=[row, _full((1, D))], out_specs=row,
                          out_shape=SDS((T, D), BF16), compiler_params=_cp("parallel"))(x, g)


def rmsnorm_bwd(x, g, dxn, dres, *, name, rider=None):
    T, D = x.shape

    def body(x_ref, g_ref, dxn_ref, dres_ref, dx_ref, dxb_ref, dg_ref):
        _, vjp = jax.vjp(_rms, x_ref[...], g_ref[...])
        dx, dg = vjp(dxn_ref[...])
        dx = dx + dres_ref[...]
        dx_ref[...] = dx
        dxb_ref[...] = dx.astype(BF16)

        @pl.when(pl.program_id(0) == 0)
        def _():
            dg_ref[...] = jnp.zeros_like(dg_ref)

        dg_ref[...] += dg

    row = pl.BlockSpec((ROW_TILE, D), lambda i: (i, 0))
    outs, rider_outs = _call(
        body, name=name, grid=(T // ROW_TILE,), in_specs=[row, _full((1, D)), row, row],
        out_specs=[row, row, _full((1, D))], out_shape=[SDS((T, D), F32), SDS((T, D), BF16), SDS((1, D), F32)],
        scratch_shapes=[], sem=("arbitrary",), args=(x, g, dxn, dres), rider=rider)
    return outs if rider is None else (outs, rider_outs)


def _loss_tile(x, g, tgt):
    err = jnp.square(_rms(x, g) - tgt)
    return 0.5 * jnp.sum(jnp.mean(err, axis=-1))


def loss_head(x, g, tgt):
    T, D = x.shape

    def body(x_ref, g_ref, t_ref, loss_ref, dx_ref, dxb_ref, dg_ref):
        loss, vjp = jax.vjp(_loss_tile, x_ref[...], g_ref[...], t_ref[...])
        dx, dg, _ = vjp(jnp.ones((), F32))
        dx_ref[...] = dx
        dxb_ref[...] = dx.astype(BF16)

        @pl.when(pl.program_id(0) == 0)
        def _():
            dg_ref[...] = jnp.zeros_like(dg_ref)
            loss_ref[...] = jnp.zeros_like(loss_ref)

        dg_ref[...] += dg
        loss_ref[...] += jnp.reshape(loss, (1, 1))

    row = pl.BlockSpec((ROW_TILE, D), lambda i: (i, 0))
    return pl.pallas_call(
        body, name="loss_head", grid=(T // ROW_TILE,), in_specs=[row, _full((1, D)), row],
        out_specs=[_full((1, 1)), row, row, _full((1, D))],
        out_shape=[SDS((1, 1), F32), SDS((T, D), F32), SDS((T, D), BF16), SDS((1, D), F32)],
        compiler_params=_cp("arbitrary"))(x, g, tgt)


TILE_BYTES = 1 << 20


def _row_tile(rows, row_bytes):
    for cand in (512, 256, 128, 64, 32, 16, 8):
        if rows % cand == 0 and cand * row_bytes <= TILE_BYTES:
            return cand
    return rows


def adamw(w, g, m, v, *, name):
    R, C = w.shape
    tr = _row_tile(R, C * 4)

    def body(w_ref, g_ref, m_ref, v_ref, d_ref, nm_ref, nv_ref):
        gg = g_ref[...]
        mm = ADAM_B1 * m_ref[...] + (1.0 - ADAM_B1) * gg
        vv = ADAM_B2 * v_ref[...] + (1.0 - ADAM_B2) * jnp.square(gg)
        m_hat = mm / (1.0 - ADAM_B1 ** ADAM_STEP)
        v_hat = vv / (1.0 - ADAM_B2 ** ADAM_STEP)
        d_ref[...] = -ADAM_LR * (m_hat / (jnp.sqrt(v_hat) + ADAM_EPS) + ADAM_WD * w_ref[...])
        nm_ref[...] = mm
        nv_ref[...] = vv

    blk = pl.BlockSpec((tr, C), lambda i: (i, 0))
    return pl.pallas_call(body, name=name, grid=(R // tr,), in_specs=[blk] * 4, out_specs=[blk] * 3,
                          out_shape=[SDS((R, C), F32)] * 3, compiler_params=_cp("parallel"))(w, g, m, v)


def sum_leading(a, *, name, rider=None):
    n, R, C = a.shape
    tr = _row_tile(R, n * C * 4)

    def body(a_ref, o_ref):
        acc = a_ref[0]
        for j in range(1, n):
            acc = acc + a_ref[j]
        o_ref[...] = acc

    outs, rider_outs = _call(body, name=name, grid=(R // tr,), in_specs=[pl.BlockSpec((n, tr, C), lambda i: (0, i, 0))],
                             out_specs=[pl.BlockSpec((tr, C), lambda i: (i, 0))], out_shape=[SDS((R, C), F32)],
                             scratch_shapes=[], sem=("parallel",), args=(a,), rider=rider)
    return outs[0] if rider is None else (outs[0], rider_outs)


def chip_sum(pieces, from_sibling, core, *, name):
    _, n, R, C = pieces.shape

    def body(c_ref, a_ref, b_ref, o_ref):
        o_ref[...] = (a_ref[0] + b_ref[...]).astype(BF16)

    tr = _row_tile(R, C * 4)
    blk = pl.BlockSpec((1, tr, C), lambda k, i, c_ref: (k, i, 0))
    mine = pl.BlockSpec((1, 1, tr, C), lambda k, i, c_ref: (c_ref[0], k, i, 0))
    return pl.pallas_call(
        body, name=name, out_shape=SDS((n, R, C), BF16),
        grid_spec=pltpu.PrefetchScalarGridSpec(num_scalar_prefetch=1, grid=(n, R // tr), in_specs=[mine, blk], out_specs=blk),
        compiler_params=_cp("parallel", "parallel"))(core.reshape(1), pieces, from_sibling)


def total_sum(sums, landed, chip, core, *, name):
    n, R, C = landed.shape

    def body(k_ref, c_ref, s_ref, l_ref, o_ref):
        acc = s_ref[0].astype(F32)
        for j in range(n):
            acc = acc + l_ref[j].astype(F32)
        south = c_ref[0] == 0
        o_ref[0] = jnp.where(south, acc, 0.0)
        o_ref[1] = jnp.where(south, 0.0, acc)

    tr = _row_tile(R, n * C * 2)
    return pl.pallas_call(
        body, name=name, out_shape=SDS((2, R, C), F32),
        grid_spec=pltpu.PrefetchScalarGridSpec(
            num_scalar_prefetch=2, grid=(R // tr,),
            in_specs=[pl.BlockSpec((1, tr, C), lambda i, k_ref, c_ref: (k_ref[0], i, 0)),
                      pl.BlockSpec((n, tr, C), lambda i, k_ref, c_ref: (0, i, 0))],
            out_specs=pl.BlockSpec((2, tr, C), lambda i, k_ref, c_ref: (0, i, 0))),
        compiler_params=_cp("parallel"))(chip.reshape(1), core.reshape(1), sums, landed)


def _gmlp_chunk(u, v, z, ln_g, ln_b, wsc, bs_t):
    mu = jnp.mean(v, axis=-1, keepdims=True)
    xc = v - mu
    vn = xc * lax.rsqrt(jnp.mean(xc * xc, axis=-1, keepdims=True) + EPS) * ln_g + ln_b
    gw = A_WIDTH // A_GROUPS
    outs = []
    for g in range(A_GROUPS):
        m = _dot_nn(wsc[g].astype(BF16), vn[:, g * gw:(g + 1) * gw].astype(BF16))
        outs.append(m + bs_t[:, g:g + 1])
    return _silu(z) * (u * jnp.concatenate(outs, axis=1))


def _h_cols(width, idx, rows=CHUNK):
    return pl.BlockSpec((rows, width), lambda i: (i, idx))


def gmlp_fwd(h, ln_g, ln_b, ws, bs_t):
    T = h.shape[0]

    def body(u_ref, v_ref, z_ref, g_ref, b_ref, ws_ref, bs_ref, o_ref):
        wsc = jnp.where(_tril(CHUNK)[None], ws_ref[...], 0.0)
        o_ref[...] = _gmlp_chunk(u_ref[...], v_ref[...], z_ref[...], g_ref[...], b_ref[...], wsc, bs_ref[...]).astype(BF16)

    return pl.pallas_call(
        body, name="gmlp_fwd", grid=(T // CHUNK,),
        in_specs=[_h_cols(A_WIDTH, 0), _h_cols(A_WIDTH, 1), _h_cols(A_WIDTH, 2), _full((1, A_WIDTH)), _full((1, A_WIDTH)),
                  _full((A_GROUPS, CHUNK, CHUNK)), _full((CHUNK, A_GROUPS))],
        out_specs=_h_cols(A_WIDTH, 0), out_shape=SDS((T, A_WIDTH), BF16), compiler_params=_cp("parallel"),
    )(h, h, h, ln_g, ln_b, ws, bs_t)


def gmlp_bwd(h, dy, ln_g, ln_b, ws, bs_t):
    T = h.shape[0]

    def body(u_ref, v_ref, z_ref, dy_ref, g_ref, b_ref, ws_ref, bs_ref, duvz_ref, dg_ref, db_ref, dws_ref, dbs_ref):
        tri = _tril(CHUNK)[None]
        wsc = jnp.where(tri, ws_ref[...], 0.0)
        _, vjp = jax.vjp(_gmlp_chunk, u_ref[...], v_ref[...], z_ref[...], g_ref[...], b_ref[...], wsc, bs_ref[...])
        du, dv, dz, dg, db, dws, dbs = vjp(dy_ref[...])
        duvz_ref[:, :A_WIDTH] = du.astype(BF16)
        duvz_ref[:, A_WIDTH:2 * A_WIDTH] = dv.astype(BF16)
        duvz_ref[:, 2 * A_WIDTH:] = dz.astype(BF16)

        @pl.when(pl.program_id(0) == 0)
        def _():
            dg_ref[...] = jnp.zeros_like(dg_ref)
            db_ref[...] = jnp.zeros_like(db_ref)
            dws_ref[...] = jnp.zeros_like(dws_ref)
            dbs_ref[...] = jnp.zeros_like(dbs_ref)

        dg_ref[...] += dg
        db_ref[...] += db
        dws_ref[...] += jnp.where(tri, dws, 0.0)
        dbs_ref[...] += dbs

    pshapes = [(1, A_WIDTH), (1, A_WIDTH), (A_GROUPS, CHUNK, CHUNK), (CHUNK, A_GROUPS)]
    return pl.pallas_call(
        body, name="gmlp_bwd", grid=(T // CHUNK,),
        in_specs=[_h_cols(A_WIDTH, 0), _h_cols(A_WIDTH, 1), _h_cols(A_WIDTH, 2), _h_cols(A_WIDTH, 0)] + [_full(s) for s in pshapes],
        out_specs=[_h_cols(3 * A_WIDTH, 0)] + [_full(s) for s in pshapes],
        out_shape=[SDS((T, EVEN_MAIN), BF16)] + [SDS(s, F32) for s in pshapes],
        compiler_params=_cp("arbitrary"),
    )(h, h, h, dy, ln_g, ln_b, ws, bs_t)


CONV_ROWS = 256
CONV_COLS = 512


def _row8():
    return lax.broadcasted_iota(jnp.int32, (SUBLANES, 1), 0)


def _delayed(x, halo, s):
    if s == 0:
        return x
    r = pltpu.roll(x, s, axis=0)
    top = jnp.where(_row8() < s, pltpu.roll(halo, s, axis=0), r[:SUBLANES])
    return jnp.concatenate([top, r[SUBLANES:]], axis=0)


def _advanced(d, s):
    if s == 0:
        return d
    rows = d.shape[0]
    r = pltpu.roll(d, rows - s, axis=0)
    bottom = jnp.where(_row8() >= SUBLANES - s, 0.0, r[rows - SUBLANES:])
    return jnp.concatenate([r[:rows - SUBLANES], bottom], axis=0)


def _conv(x, halo, w):
    K = w.shape[0]
    acc = None
    for s in range(K):
        term = w[K - 1 - s:K - s, :] * _delayed(x, halo, s)
        acc = term if acc is None else acc + term
    return acc


def _conv_bwd(x, halo, w, d):
    K = w.shape[0]
    dx, dhalo, dws = None, None, [None] * K
    for s in range(K):
        wk = w[K - 1 - s:K - s, :]
        dws[K - 1 - s] = jnp.sum(d * _delayed(x, halo, s), axis=0, keepdims=True)
        term = wk * _advanced(d, s)
        dx = term if dx is None else dx + term
        if s:
            part = wk * jnp.where(_row8() >= SUBLANES - s, pltpu.roll(d[:SUBLANES], SUBLANES - s, axis=0), 0.0)
            dhalo = part if dhalo is None else dhalo + part
    return dx, dhalo, jnp.concatenate(dws, axis=0)


def _add_to_tail(d, carry):
    return jnp.concatenate([d[:d.shape[0] - SUBLANES], d[d.shape[0] - SUBLANES:] + carry], axis=0)


def _halo_spec(cols, col_idx, nt=None):
    rpb = CONV_ROWS // SUBLANES
    if nt is None:
        return pl.BlockSpec((SUBLANES, cols), lambda c, i: (jnp.maximum(i * rpb - 1, 0), col_idx(c)))
    return pl.BlockSpec((SUBLANES, cols), lambda c, j: (jnp.maximum((nt - 1 - j) * rpb - 1, 0), col_idx(c)))


def ssd_conv_fwd(h, w, b):
    T = h.shape[0]
    nc = B_XBC // CONV_COLS
    base = (3 * A_WIDTH + B_WIDTH) // CONV_COLS

    def body(x_ref, halo_ref, w_ref, b_ref, o_ref):
        halo = jnp.where(pl.program_id(1) > 0, halo_ref[...], 0.0)
        o_ref[...] = _silu(_conv(x_ref[...], halo, w_ref[...]) + b_ref[...])

    return pl.pallas_call(
        body, name="ssd_conv_fwd", grid=(nc, T // CONV_ROWS),
        in_specs=[pl.BlockSpec((CONV_ROWS, CONV_COLS), lambda c, i: (i, base + c)), _halo_spec(CONV_COLS, lambda c: base + c),
                  pl.BlockSpec((B_CONV, CONV_COLS), lambda c, i: (0, c)), pl.BlockSpec((1, CONV_COLS), lambda c, i: (0, c))],
        out_specs=pl.BlockSpec((CONV_ROWS, CONV_COLS), lambda c, i: (i, c)),
        out_shape=SDS((T, B_XBC), F32), compiler_params=_cp("parallel", "parallel"),
    )(h, h, w, b)


def ssd_conv_bwd(h, dy, w, b, dh, rider=None):
    T = h.shape[0]
    nc = B_XBC // CONV_COLS
    nt = T // CONV_ROWS
    base = (3 * A_WIDTH + B_WIDTH) // CONV_COLS

    def body(x_ref, halo_ref, dy_ref, w_ref, b_ref, dh_ref, dx_ref, dw_ref, db_ref, carry_ref):
        j = pl.program_id(1)
        halo = jnp.where(j < nt - 1, halo_ref[...], 0.0)
        x, w = x_ref[...], w_ref[...]
        pre = _conv(x, halo, w) + b_ref[...]
        dpre = dy_ref[...] * _dsilu(pre)
        dx, dhalo, dw = _conv_bwd(x, halo, w, dpre)

        @pl.when(j == 0)
        def _():
            carry_ref[...] = jnp.zeros_like(carry_ref)
            dw_ref[...] = jnp.zeros_like(dw_ref)
            db_ref[...] = jnp.zeros_like(db_ref)

        dx_ref[...] = _add_to_tail(dx, carry_ref[...]).astype(BF16)
        carry_ref[...] = dhalo
        dw_ref[...] += dw
        db_ref[...] += jnp.sum(dpre, axis=0, keepdims=True)

    outs, rider_outs = _call(
        body, name="ssd_conv_bwd", grid=(nc, nt),
        in_specs=[pl.BlockSpec((CONV_ROWS, CONV_COLS), lambda c, j: (nt - 1 - j, base + c)),
                  _halo_spec(CONV_COLS, lambda c: base + c, nt),
                  pl.BlockSpec((CONV_ROWS, CONV_COLS), lambda c, j: (nt - 1 - j, c)),
                  pl.BlockSpec((B_CONV, CONV_COLS), lambda c, j: (0, c)), pl.BlockSpec((1, CONV_COLS), lambda c, j: (0, c)), ANY],
        out_specs=[pl.BlockSpec((CONV_ROWS, CONV_COLS), lambda c, j: (nt - 1 - j, base + c)),
                   pl.BlockSpec((B_CONV, CONV_COLS), lambda c, j: (0, c)), pl.BlockSpec((1, CONV_COLS), lambda c, j: (0, c))],
        out_shape=[SDS(dh.shape, dh.dtype), SDS((B_CONV, B_XBC), F32), SDS((1, B_XBC), F32)],
        scratch_shapes=[pltpu.VMEM((SUBLANES, CONV_COLS), F32)], sem=("parallel", "arbitrary"),
        args=(h, h, dy, w, b, dh), rider=rider, aliases={5: 0})
    return outs if rider is None else (outs, rider_outs)


def sconv_fwd(h, w):
    T = h.shape[0]
    nc = C_WIDTH // CONV_COLS

    def col(seg):
        return pl.BlockSpec((CONV_ROWS, CONV_COLS), lambda c, i: (i, seg * nc + c))

    def body(bg_ref, cg_ref, hx_ref, z_ref, cgh_ref, hxh_ref, w_ref, o_ref):
        first = pl.program_id(1) == 0
        cgh = jnp.where(first, 0.0, cgh_ref[...])
        hxh = jnp.where(first, 0.0, hxh_ref[...])
        conv = _conv(cg_ref[...] * hx_ref[...], cgh * hxh, w_ref[...])
        o_ref[...] = (_silu(z_ref[...]) * (bg_ref[...] * conv)).astype(BF16)

    return pl.pallas_call(
        body, name="sconv_fwd", grid=(nc, T // CONV_ROWS),
        in_specs=[col(0), col(1), col(2), col(3), _halo_spec(CONV_COLS, lambda c: nc + c), _halo_spec(CONV_COLS, lambda c: 2 * nc + c),
                  pl.BlockSpec((C_CONV, CONV_COLS), lambda c, i: (0, c))],
        out_specs=pl.BlockSpec((CONV_ROWS, CONV_COLS), lambda c, i: (i, c)),
        out_shape=SDS((T, C_WIDTH), BF16), compiler_params=_cp("parallel", "parallel"),
    )(h, h, h, h, h, h, w)


def sconv_bwd(h, dy, w):
    T = h.shape[0]
    nc = C_WIDTH // CONV_COLS
    nt = T // CONV_ROWS

    def col(seg):
        return pl.BlockSpec((CONV_ROWS, CONV_COLS), lambda c, j: (nt - 1 - j, seg * nc + c))

    def body(bg_ref, cg_ref, hx_ref, z_ref, cgh_ref, hxh_ref, dy_ref, w_ref, dbg_ref, dcg_ref, dhx_ref, dz_ref, dw_ref, carry_ref):
        j = pl.program_id(1)
        first = j == nt - 1
        cgh = jnp.where(first, 0.0, cgh_ref[...])
        hxh = jnp.where(first, 0.0, hxh_ref[...])
        bg, cg, hx, z, w, dy = bg_ref[...], cg_ref[...], hx_ref[...], z_ref[...], w_ref[...], dy_ref[...]
        ch, ch_halo = cg * hx, cgh * hxh
        conv = _conv(ch, ch_halo, w)
        sig = jax.nn.sigmoid(z)
        gated = dy * (z * sig)
        dch, dch_halo, dw = _conv_bwd(ch, ch_halo, w, gated * bg)

        @pl.when(j == 0)
        def _():
            carry_ref[...] = jnp.zeros_like(carry_ref)
            dw_ref[...] = jnp.zeros_like(dw_ref)

        dch = _add_to_tail(dch, carry_ref[...])
        dbg_ref[...] = (gated * conv).astype(BF16)
        dz_ref[...] = (dy * bg * conv * (sig * (1.0 + z * (1.0 - sig)))).astype(BF16)
        dcg_ref[...] = (dch * hx).astype(BF16)
        dhx_ref[...] = (dch * cg).astype(BF16)
        carry_ref[...] = dch_halo
        dw_ref[...] += dw

    out_row = pl.BlockSpec((CONV_ROWS, CONV_COLS), lambda c, j: (nt - 1 - j, c))
    wspec = pl.BlockSpec((C_CONV, CONV_COLS), lambda c, j: (0, c))
    return pl.pallas_call(
        body, name="sconv_bwd", grid=(nc, nt),
        in_specs=[col(0), col(1), col(2), col(3), _halo_spec(CONV_COLS, lambda c: nc + c, nt), _halo_spec(CONV_COLS, lambda c: 2 * nc + c, nt),
                  out_row, wspec],
        out_specs=[out_row] * 4 + [wspec],
        out_shape=[SDS((T, C_WIDTH), BF16)] * 4 + [SDS((C_CONV, C_WIDTH), F32)],
        scratch_shapes=[pltpu.VMEM((SUBLANES, CONV_COLS), F32)],
        compiler_params=_cp("parallel", "arbitrary"),
    )(h, h, h, h, h, h, dy, w)


def _softplus(x):
    return jnp.maximum(x, 0.0) + jnp.log(1.0 + jnp.exp(-jnp.abs(x)))


def _ssd_chunk(xs, bm, cm, dtr, z, prev, dt_bias, a_log, d_skip, norm_g):
    tril = _tril(CHUNK)
    dt = _softplus(dtr + dt_bias)
    adt = dt * (-jnp.exp(a_log))
    a_cs = jnp.dot(tril.astype(F32), adt, precision=lax.Precision.HIGHEST, preferred_element_type=F32)
    a_cs_t = a_cs.T
    a_last = a_cs[CHUNK - 1:CHUNK, :]
    dt_f = _spread_heads(dt, B_HEAD_DIM)
    dec_f = _spread_heads(jnp.exp(a_last - a_cs), B_HEAD_DIM)
    ecs_f = _spread_heads(jnp.exp(a_cs), B_HEAD_DIM)
    dsk_f = _spread_heads(d_skip, B_HEAD_DIM)
    cd_t = jnp.exp(a_cs_t[:, CHUNK - 1:CHUNK])
    xdt = xs * dt_f
    xdd = xdt * dec_f
    colb = _spread_heads(a_cs, CHUNK)
    rowb = jnp.concatenate([jnp.broadcast_to(a_cs_t[hh:hh + 1, :], (CHUNK, CHUNK)) for hh in range(B_HEADS)], axis=1)
    wide = (CHUNK, B_HEADS * CHUNK)
    keep = lax.broadcasted_iota(jnp.int32, wide, 0) >= lax.broadcasted_iota(jnp.int32, wide, 1) % CHUNK
    decay = jnp.exp(jnp.where(keep, colb - rowb, -jnp.inf))
    hpg = B_HEADS // B_GROUPS
    gw = B_WIDTH // B_GROUPS
    low_half = lax.broadcasted_iota(jnp.int32, (CHUNK, 2 * B_HEAD_DIM), 1) < B_HEAD_DIM
    ys, nxt = [], []
    for g in range(B_GROUPS):
        bg = bm[:, g * B_STATE:(g + 1) * B_STATE].astype(BF16)
        cg = cm[:, g * B_STATE:(g + 1) * B_STATE].astype(BF16)
        cb = _dot_nt(cg, bg)
        cbl = (decay[:, g * hpg * CHUNK:(g + 1) * hpg * CHUNK] * jnp.concatenate([cb] * hpg, axis=1)).astype(BF16)
        pg = prev[g * gw:(g + 1) * gw, :]
        y_off = _dot_nt(cg, pg.astype(BF16)) * ecs_f[:, g * gw:(g + 1) * gw]
        st = _dot_tn(xdd[:, g * gw:(g + 1) * gw].astype(BF16), bg)
        cd = jnp.concatenate([jnp.broadcast_to(cd_t[g * hpg + r:g * hpg + r + 1, :], (B_HEAD_DIM, 1)) for r in range(hpg)], axis=0)
        nxt.append(pg * cd + st)
        pairs = []
        for j in range(hpg // 2):
            xp = xdt[:, g * gw + 2 * j * B_HEAD_DIM:g * gw + 2 * (j + 1) * B_HEAD_DIM]
            rhs = jnp.concatenate([jnp.where(low_half, xp, 0.0), jnp.where(low_half, 0.0, xp)], axis=0).astype(BF16)
            pairs.append(_dot_nn(cbl[:, 2 * j * CHUNK:2 * (j + 1) * CHUNK], rhs))
        ys.append(jnp.concatenate(pairs, axis=1) + y_off)
    y = (jnp.concatenate(ys, axis=1) + dsk_f * xs) * _silu(z)
    outs = []
    for g in range(B_GROUPS):
        yg = y[:, g * gw:(g + 1) * gw]
        outs.append(yg * lax.rsqrt(jnp.mean(yg * yg, axis=-1, keepdims=True) + EPS))
    return jnp.concatenate(outs, axis=1) * norm_g, jnp.concatenate(nxt, axis=0)


def _split3(v):
    hi = v.astype(BF16)
    r1 = v - hi.astype(F32)
    mid = r1.astype(BF16)
    return hi, mid, (r1 - mid.astype(F32)).astype(BF16)


def _head_one_hot(width, parts):
    n = B_HEADS * width
    shape = (parts * LANES, n)
    return (lax.broadcasted_iota(jnp.int32, shape, 0) % LANES == lax.broadcasted_iota(jnp.int32, shape, 1) // width).astype(BF16)


@functools.partial(jax.custom_vjp, nondiff_argnums=(1,))
def _spread_heads(v, width):
    return _dot_nn(jnp.concatenate(_split3(v), axis=1), _head_one_hot(width, 3))


def _spread_heads_fwd(v, width):
    return _spread_heads(v, width), None


def _spread_heads_bwd(width, _, g):
    return (_dot_nt(jnp.concatenate(_split3(g), axis=1), jnp.concatenate([_head_one_hot(width, 1)] * 3, axis=1)),)


_spread_heads.defvjp(_spread_heads_fwd, _spread_heads_bwd)


_SSD_PARAM_SHAPES = [(1, LANES), (1, LANES), (1, LANES), (1, B_WIDTH)]
_STATE_SHAPE = (B_WIDTH, B_STATE)


def ssd_fwd(xbc, dtr, h, dt_bias, a_log, d_skip, norm_g, rider=None):
    T = xbc.shape[0]
    nc = T // CHUNK

    def body(xs_ref, b_ref, c_ref, dt_ref, z_ref, p0, p1, p2, p3, y_ref, st_ref, state):
        @pl.when(pl.program_id(0) == 0)
        def _():
            state[...] = jnp.zeros_like(state)

        prev = state[...]
        st_ref[0] = prev
        yb, nxt = _ssd_chunk(xs_ref[...], b_ref[...], c_ref[...], dt_ref[...], z_ref[...], prev, p0[...], p1[...], p2[...], p3[...])
        y_ref[...] = yb.astype(BF16)
        state[...] = nxt

    outs, rider_outs = _call(
        body, name="ssd_fwd", grid=(nc,),
        in_specs=[_h_cols(B_WIDTH, 0), _h_cols(B_GROUPS * B_STATE, 2), _h_cols(B_GROUPS * B_STATE, 3), _h_cols(LANES, 0), _h_cols(B_WIDTH, 3)]
        + [_full(s) for s in _SSD_PARAM_SHAPES],
        out_specs=[_h_cols(B_WIDTH, 0), pl.BlockSpec((1,) + _STATE_SHAPE, lambda i: (i, 0, 0))],
        out_shape=[SDS((T, B_WIDTH), BF16), SDS((nc,) + _STATE_SHAPE, F32)],
        scratch_shapes=[pltpu.VMEM(_STATE_SHAPE, F32)], sem=("arbitrary",),
        args=(xbc, xbc, xbc, dtr, h, dt_bias, a_log, d_skip, norm_g), rider=rider)
    return outs if rider is None else (outs, rider_outs)


def ssd_bwd(xbc, dtr, h, states, dy, dt_bias, a_log, d_skip, norm_g, dh, rider=None):
    T = xbc.shape[0]
    nc = T // CHUNK

    def rev(width, idx):
        return pl.BlockSpec((CHUNK, width), lambda j: (nc - 1 - j, idx))

    def body(xs_ref, b_ref, c_ref, dt_ref, z_ref, st_ref, dy_ref, p0, p1, p2, p3, dh_ref,
             dxbc_ref, ddt_ref, dz_ref, g0, g1, g2, g3, dstate):
        @pl.when(pl.program_id(0) == 0)
        def _():
            dstate[...] = jnp.zeros_like(dstate)
            for gref in (g0, g1, g2, g3):
                gref[...] = jnp.zeros_like(gref)

        _, vjp = jax.vjp(_ssd_chunk, xs_ref[...], b_ref[...], c_ref[...], dt_ref[...], z_ref[...], st_ref[0],
                         p0[...], p1[...], p2[...], p3[...])
        dxs, dbm, dcm, ddt, dz, dprev, d0, d1, d2, d3 = vjp((dy_ref[...], dstate[...]))
        dxbc_ref[:, :B_WIDTH] = dxs
        dxbc_ref[:, B_WIDTH:B_WIDTH + gn] = dbm
        dxbc_ref[:, B_WIDTH + gn:] = dcm
        ddt_ref[...] = ddt.astype(BF16)
        dz_ref[...] = dz.astype(BF16)
        dstate[...] = dprev
        g0[...] += d0
        g1[...] += d1
        g2[...] += d2
        g3[...] += d3

    gn = B_GROUPS * B_STATE
    outs, rider_outs = _call(
        body, name="ssd_bwd", grid=(nc,),
        in_specs=[rev(B_WIDTH, 0), rev(gn, 2), rev(gn, 3), rev(LANES, 0), rev(B_WIDTH, 3),
                  pl.BlockSpec((1,) + _STATE_SHAPE, lambda j: (nc - 1 - j, 0, 0)), rev(B_WIDTH, 1)]
        + [_full(s) for s in _SSD_PARAM_SHAPES] + [ANY],
        out_specs=[rev(B_XBC, 0), rev(LANES, 0), rev(B_WIDTH, 3)] + [_full(s) for s in _SSD_PARAM_SHAPES],
        out_shape=[SDS((T, B_XBC), F32), SDS((T, LANES), BF16), SDS(dh.shape, dh.dtype)]
        + [SDS(s, F32) for s in _SSD_PARAM_SHAPES],
        scratch_shapes=[pltpu.VMEM(_STATE_SHAPE, F32)], sem=("arbitrary",),
        args=(xbc, xbc, xbc, dtr, h, states, dy, dt_bias, a_log, d_skip, norm_g, dh), rider=rider, aliases={11: 2})
    return outs if rider is None else (outs, rider_outs)


ATT_SCALE = D_HEAD_DIM ** -0.5
Q_COL, K_COL, V_COL, Z_COL = (4 * C_WIDTH // LANES + i * D_HEADS for i in range(4))


ATT_NBLK = ATT_SUPER // ATT_BLOCK


def _res_rows(r, first, count, dil):
    return pl.ds(r + dil * first, count) if dil == 1 else pl.ds(r + dil * first, count, stride=dil)


def _blocks(ref, dil, dtype=None):
    n = ATT_SUPER // dil
    parts = []
    for r in range(dil):
        v = ref[_res_rows(r, 0, n, dil), :]
        parts.append((v if dtype is None else v.astype(dtype)).reshape(n // ATT_BLOCK, ATT_BLOCK, D_HEAD_DIM))
    return parts[0] if dil == 1 else jnp.concatenate(parts, axis=0)


def _blocks_before(cur_blocks, prev_ref, dil, dtype):
    n = ATT_SUPER // dil
    nb = n // ATT_BLOCK
    parts = []
    for r in range(dil):
        edge = prev_ref[_res_rows(r, n - ATT_BLOCK, ATT_BLOCK, dil), :].astype(dtype)
        parts.append(edge.reshape(1, ATT_BLOCK, D_HEAD_DIM))
        if nb > 1:
            parts.append(cur_blocks[r * nb:(r + 1) * nb - 1])
    return jnp.concatenate(parts, axis=0)


def _unblock(ref, val, dil, add=False):
    n = ATT_SUPER // dil
    nb = n // ATT_BLOCK
    for r in range(dil):
        v = val[r * nb:(r + 1) * nb].reshape(n, D_HEAD_DIM)
        if add:
            ref[_res_rows(r, 0, n, dil), :] += v
        else:
            ref[_res_rows(r, 0, n, dil), :] = v


def _att_masks(dil, edge_ok):
    shape = (ATT_NBLK, ATT_BLOCK, ATT_BLOCK)
    blk = lax.broadcasted_iota(jnp.int32, shape, 0)
    row = lax.broadcasted_iota(jnp.int32, shape, 1)
    col = lax.broadcasted_iota(jnp.int32, shape, 2)
    nb = ATT_NBLK // dil
    at_edge = (blk % nb) == 0
    return col <= row, jnp.logical_and(col >= row, jnp.logical_or(jnp.logical_not(at_edge), edge_ok))


def _bdot_nt(a, b):
    return lax.dot_general(a, b, (((2,), (2,)), ((0,), (0,))), preferred_element_type=F32)


def _bdot_nn(a, b):
    return lax.dot_general(a, b, (((2,), (1,)), ((0,), (0,))), preferred_element_type=F32)


def _bdot_tn(a, b):
    return lax.dot_general(a, b, (((1,), (1,)), ((0,), (0,))), preferred_element_type=F32)


def _att_spec(col0, shift=0, last=None):
    def imap(hh, n):
        m = n + shift
        if shift < 0:
            m = jnp.maximum(m, 0)
        if shift > 0:
            m = jnp.minimum(m, last)
        return (m, col0 + hh)
    return pl.BlockSpec((ATT_SUPER, D_HEAD_DIM), imap)


def _att_out_spec():
    return pl.BlockSpec((ATT_SUPER, D_HEAD_DIM), lambda hh, n: (n, hh))


def attn_fwd(h):
    T = h.shape[0]
    npat = len(D_PATTERNS)

    def body(q_ref, kc_ref, kp_ref, vc_ref, vp_ref, z_ref, yd_ref, o_ref, lse_ref, *scratch):
        o_s, l_s = scratch[:npat], scratch[npat:]
        has_prev = pl.program_id(1) > 0
        for pi, (_, dil) in enumerate(D_PATTERNS):
            mask_c, mask_p = _att_masks(dil, has_prev)
            q = _blocks(q_ref, dil, BF16)
            kc, vc = _blocks(kc_ref, dil, BF16), _blocks(vc_ref, dil, BF16)
            kp, vp = _blocks_before(kc, kp_ref, dil, BF16), _blocks_before(vc, vp_ref, dil, BF16)
            s_c = jnp.where(mask_c, _bdot_nt(q, kc) * ATT_SCALE, -jnp.inf)
            s_p = jnp.where(mask_p, _bdot_nt(q, kp) * ATT_SCALE, -jnp.inf)
            m = jnp.maximum(jnp.max(s_c, axis=-1, keepdims=True), jnp.max(s_p, axis=-1, keepdims=True))
            p_c = jnp.exp(s_c - m)
            p_p = jnp.exp(s_p - m)
            l = jnp.sum(p_c, axis=-1, keepdims=True) + jnp.sum(p_p, axis=-1, keepdims=True)
            o = _bdot_nn((p_c / l).astype(BF16), vc) + _bdot_nn((p_p / l).astype(BF16), vp)
            _unblock(o_s[pi], o, dil)
            _unblock(l_s[pi], jnp.broadcast_to(m + jnp.log(l), o.shape), dil)
        lses = [l_s[pi][...] for pi in range(npat)]
        mx = functools.reduce(jnp.maximum, lses)
        ws = [jnp.exp(l - mx) for l in lses]
        den = functools.reduce(lambda a, b: a + b, ws)
        o = functools.reduce(lambda a, b: a + b, [(w / den) * o_s[pi][...] for pi, w in enumerate(ws)])
        o_ref[...] = o
        lse_ref[...] = mx + jnp.log(den)
        yd_ref[...] = (_silu(z_ref[...]) * o).astype(BF16)

    n_super = T // ATT_SUPER
    return pl.pallas_call(
        body, name="attn_fwd", grid=(D_HEADS, n_super),
        in_specs=[_att_spec(Q_COL), _att_spec(K_COL), _att_spec(K_COL, -1), _att_spec(V_COL), _att_spec(V_COL, -1), _att_spec(Z_COL)],
        out_specs=[_att_out_spec()] * 3,
        out_shape=[SDS((T, D_HEADS * D_HEAD_DIM), BF16), SDS((T, D_HEADS * D_HEAD_DIM), F32), SDS((T, D_HEADS * D_HEAD_DIM), F32)],
        scratch_shapes=[pltpu.VMEM((ATT_SUPER, D_HEAD_DIM), F32)] * (2 * npat),
        compiler_params=_cp("parallel", "arbitrary"),
    )(h, h, h, h, h, h)


def _dsilu(z):
    s = jax.nn.sigmoid(z)
    return s * (1.0 + z * (1.0 - s))


def attn_bwd(h, o, lse, dy):
    T = h.shape[0]
    last = T // ATT_SUPER - 1
    dy_col = C_WIDTH // LANES

    def spec(col0, shift=0):
        return pl.BlockSpec((ATT_SUPER, D_HEAD_DIM), lambda hh, j: (jnp.maximum(last - j + shift, 0), col0 + hh))

    def add_before(ref, carry_ref, val, dil):
        n = ATT_SUPER // dil
        nb = n // ATT_BLOCK
        for r in range(dil):
            carry_ref[_res_rows(r, n - ATT_BLOCK, ATT_BLOCK, dil), :] += val[r * nb]
            if nb > 1:
                ref[_res_rows(r, 0, n - ATT_BLOCK, dil), :] += val[r * nb + 1:(r + 1) * nb].reshape(n - ATT_BLOCK, D_HEAD_DIM)

    def body(q_ref, kc_ref, kp_ref, vc_ref, vp_ref, z_ref, o_ref, lse_ref, dy_ref, dq_ref, dk_ref, dv_ref, dz_ref,
             do_s, dd_s, dq_s, dk_s, dv_s, dk_carry, dv_carry):
        j = pl.program_id(1)
        has_prev = j < last

        @pl.when(j == 0)
        def _():
            dk_carry[...] = jnp.zeros_like(dk_carry)
            dv_carry[...] = jnp.zeros_like(dv_carry)

        dk_s[...] = dk_carry[...]
        dv_s[...] = dv_carry[...]
        dk_carry[...] = jnp.zeros_like(dk_carry)
        dv_carry[...] = jnp.zeros_like(dv_carry)
        z, oo, dyd = z_ref[...], o_ref[...], dy_ref[...]
        sig = jax.nn.sigmoid(z)
        do = dyd * (z * sig)
        dz_ref[...] = (dyd * oo * (sig * (1.0 + z * (1.0 - sig)))).astype(BF16)
        do_s[...] = do
        dd_s[...] = jnp.broadcast_to(jnp.sum(do * oo, axis=-1, keepdims=True), (ATT_SUPER, D_HEAD_DIM))
        for pi, (_, dil) in enumerate(D_PATTERNS):
            mask_c, mask_p = _att_masks(dil, has_prev)
            q = _blocks(q_ref, dil, BF16)
            kc, vc = _blocks(kc_ref, dil, BF16), _blocks(vc_ref, dil, BF16)
            kp, vp = _blocks_before(kc, kp_ref, dil, BF16), _blocks_before(vc, vp_ref, dil, BF16)
            lse_b, dd_b, do_b = _blocks(lse_ref, dil), _blocks(dd_s, dil), _blocks(do_s, dil, BF16)
            p_c = jnp.where(mask_c, jnp.exp(_bdot_nt(q, kc) * ATT_SCALE - lse_b), 0.0)
            p_p = jnp.where(mask_p, jnp.exp(_bdot_nt(q, kp) * ATT_SCALE - lse_b), 0.0)
            ds_c = (p_c * (_bdot_nt(do_b, vc) - dd_b) * ATT_SCALE).astype(BF16)
            ds_p = (p_p * (_bdot_nt(do_b, vp) - dd_b) * ATT_SCALE).astype(BF16)
            _unblock(dq_s, _bdot_nn(ds_c, kc) + _bdot_nn(ds_p, kp), dil, add=pi > 0)
            _unblock(dk_s, _bdot_tn(ds_c, q), dil, add=True)
            _unblock(dv_s, _bdot_tn(p_c.astype(BF16), do_b), dil, add=True)
            add_before(dk_s, dk_carry, _bdot_tn(ds_p, q), dil)
            add_before(dv_s, dv_carry, _bdot_tn(p_p.astype(BF16), do_b), dil)
        dq_ref[...] = dq_s[...].astype(BF16)
        dk_ref[...] = dk_s[...].astype(BF16)
        dv_ref[...] = dv_s[...].astype(BF16)

    blk = (ATT_SUPER, D_HEAD_DIM)
    out = pl.BlockSpec(blk, lambda hh, j: (last - j, hh))
    return pl.pallas_call(
        body, name="attn_bwd", grid=(D_HEADS, last + 1),
        in_specs=[spec(Q_COL), spec(K_COL), spec(K_COL, -1), spec(V_COL), spec(V_COL, -1), spec(Z_COL), spec(0), spec(0), spec(dy_col)],
        out_specs=[out] * 4, out_shape=[SDS((T, D_HEADS * D_HEAD_DIM), BF16)] * 4,
        scratch_shapes=[pltpu.VMEM(blk, F32)] * 7, compiler_params=_cp("parallel", "arbitrary"),
    )(h, h, h, h, h, h, o, lse, dy)


ANY = pl.BlockSpec(memory_space=pl.ANY)
COMM_PARAMS = pltpu.CompilerParams()


def _place():
    x, y, c = lax.axis_index("x"), lax.axis_index("y"), lax.axis_index("c")
    return x, y, c, [(1 - x, y), (x, 1 - y), (1 - x, 1 - y)]


def _rcopy(src, dst, ssem, rsem, dev):
    return pltpu.make_async_remote_copy(src_ref=src, dst_ref=dst, send_sem=ssem, recv_sem=rsem, device_id=dev, device_id_type=MESH)


def gather_rider(arrs, fractions=(0.0, 0.6, 1.0)):
    n = len(arrs)
    per = 7

    def to_chips(ins, outs, ssem, rsem):
        x, y, c, chips = _place()
        return [_rcopy(ins[a].at[c], outs[a].at[2 * x + y, c], ssem.at[per * a + j], rsem.at[per * a + j], (px, py, c))
                for a in range(n) for j, (px, py) in enumerate(chips)]

    def passed_on(outs, ssem, rsem, half):
        x, y, c, chips = _place()
        cps = []
        for a in range(n):
            for j, (px, py) in enumerate(chips):
                slot = outs[a].at[2 * px + py, half(c)]
                cps.append(_rcopy(slot, slot, ssem.at[per * a + 3 + j], rsem.at[per * a + 3 + j], (x, y, 1 - c)))
        return cps

    def own(ins, outs, ssem, rsem):
        x, y, c, _ = _place()
        return [_rcopy(ins[a], outs[a].at[2 * x + y], ssem.at[per * a + 6], rsem.at[per * a + 6], (x, y, 1 - c)) for a in range(n)]

    def start(ins, outs, ssem, rsem):
        for cp in to_chips(ins, outs, ssem, rsem) + own(ins, outs, ssem, rsem):
            cp.start()

    def pass_on(ins, outs, ssem, rsem):
        x, y, c, chips = _place()
        landed = [_rcopy(outs[a].at[2 * px + py, c], outs[a].at[2 * px + py, c], ssem.at[per * a + j], rsem.at[per * a + j], (px, py, c))
                  for a in range(n) for j, (px, py) in enumerate(chips)]
        for arrival, cp in zip(landed, passed_on(outs, ssem, rsem, lambda c: c)):
            arrival.wait_recv()
            cp.start()

    def finish(ins, outs, ssem, rsem):
        for cp in passed_on(outs, ssem, rsem, lambda c: 1 - c):
            cp.wait_recv()
        for cp in to_chips(ins, outs, ssem, rsem) + passed_on(outs, ssem, rsem, lambda c: c):
            cp.wait_send()
        for cp in own(ins, outs, ssem, rsem):
            cp.wait()

    return Rider(arrs, [SDS((N_CHIPS,) + a.shape, a.dtype) for a in arrs], per * n,
                 [(fractions[0], start), (fractions[1], pass_on), (fractions[2], finish)])


def _copies_rider(ins, out_shapes, n_sems, make):
    def start(*refs):
        for cp in make(*refs):
            cp.start()

    def finish(*refs):
        for cp in make(*refs):
            cp.wait()

    return Rider(ins, out_shapes, n_sems, [(0.0, start), (1.0, finish)])


def swap_halves_rider(arrs):
    def make(ins, outs, ssem, rsem):
        x, y, c, _ = _place()
        return [_rcopy(ins[a].at[1 - c], outs[a], ssem.at[a], rsem.at[a], (x, y, 1 - c)) for a in range(len(arrs))]
    return _copies_rider(arrs, [SDS(a.shape[1:], a.dtype) for a in arrs], len(arrs), make)


def scatter_rider(arrs):
    def make(ins, outs, ssem, rsem):
        x, y, c, chips = _place()
        return [_rcopy(ins[a].at[2 * px + py], outs[a].at[j], ssem.at[3 * a + j], rsem.at[3 * a + j], (px, py, c))
                for a in range(len(arrs)) for j, (px, py) in enumerate(chips)]
    return _copies_rider(arrs, [SDS((N_CHIPS - 1,) + a.shape[1:], a.dtype) for a in arrs], 3 * len(arrs), make)


def join_halves_rider(arrs):
    def make(ins, outs, ssem, rsem):
        x, y, c, _ = _place()
        return [_rcopy(ins[a].at[c], outs[a].at[c], ssem.at[a], rsem.at[a], (x, y, 1 - c)) for a in range(len(arrs))]

    def start(*refs):
        for cp in make(*refs):
            cp.start()

    def finish(ins, outs, ssem, rsem):
        x, y, c, _ = _place()
        for a, cp in enumerate(make(ins, outs, ssem, rsem)):
            cp.wait_send()
            _rcopy(ins[a].at[1 - c], outs[a].at[1 - c], ssem.at[a], rsem.at[a], (x, y, 1 - c)).wait_recv()

    return Rider(arrs, [SDS(a.shape, a.dtype) for a in arrs], len(arrs), [(0.0, start), (1.0, finish)], in_place=True)


def gather_all(buf):
    def body(in_ref, out_ref, ssem, rsem, lsem):
        x, y, c, _ = _place()
        me = 4 * x + 2 * y + c
        local = pltpu.make_async_copy(in_ref, out_ref.at[me], lsem)
        local.start()
        flips = [(a, b, e) for a in (0, 1) for b in (0, 1) for e in (0, 1)][1:]
        cps = []
        for i, (a, b, e) in enumerate(flips):
            peer = (x ^ a, y ^ b, c ^ e)
            cps.append(_rcopy(in_ref, out_ref.at[me], ssem.at[i], rsem.at[i], peer))
        for cp in cps:
            cp.start()
        for i, (a, b, e) in enumerate(flips):
            cps[i].wait_send()
            slot = out_ref.at[4 * (x ^ a) + 2 * (y ^ b) + (c ^ e)]
            _rcopy(slot, slot, ssem.at[i], rsem.at[i], (x ^ a, y ^ b, c ^ e)).wait_recv()
        local.wait()

    return pl.pallas_call(
        body, name="comm_gather_all", in_specs=[ANY], out_specs=ANY, out_shape=SDS((N_DEV,) + buf.shape, buf.dtype),
        scratch_shapes=[pltpu.SemaphoreType.DMA((N_DEV - 1,)), pltpu.SemaphoreType.DMA((N_DEV - 1,)), pltpu.SemaphoreType.DMA],
        compiler_params=COMM_PARAMS,
    )(buf)


def _pack_offsets(parts):
    offs, r = [], 0
    for p in parts:
        offs.append(r)
        r += -(-p.shape[0] // SUBLANES) * SUBLANES
    return offs, r


def pack_rows(parts):
    offs, total = _pack_offsets(parts)

    def body(*refs):
        out = refs[-1]
        out[...] = jnp.zeros_like(out)
        for ref, off in zip(refs[:-1], offs):
            out[off:off + ref.shape[0], :] = ref[...]

    vmem = pl.BlockSpec(memory_space=pltpu.VMEM)
    return pl.pallas_call(body, name="pack_small", in_specs=[vmem] * len(parts), out_specs=vmem,
                          out_shape=SDS((total, LANES), F32))(*parts)


EVEN_SHARD = IN_EVEN // N_CHIPS


def wie_from_shards(g):
    tr = 256

    def body(g_ref, main_ref, dt_ref):
        full = jnp.concatenate([g_ref[k] for k in range(N_CHIPS)], axis=1)
        main_ref[...] = full[:, :EVEN_MAIN]
        dt_ref[...] = jnp.concatenate([full[:, EVEN_MAIN:], jnp.zeros((tr, LANES - B_HEADS), full.dtype)], axis=1)

    return pl.pallas_call(
        body, name="wie_from_shards", grid=(D_MODEL // tr,),
        in_specs=[pl.BlockSpec((N_CHIPS, tr, EVEN_SHARD), lambda i: (0, i, 0))],
        out_specs=[pl.BlockSpec((tr, EVEN_MAIN), lambda i: (i, 0)), pl.BlockSpec((tr, LANES), lambda i: (i, 0))],
        out_shape=[SDS((D_MODEL, EVEN_MAIN), g.dtype), SDS((D_MODEL, LANES), g.dtype)], compiler_params=_cp("parallel"))(g)


def wie_grad_to_pieces(main, dt):
    tr = 128
    per_half = D_MODEL // 2 // tr

    def body(m_ref, d_ref, o_ref):
        full = jnp.concatenate([m_ref[...], d_ref[:, :B_HEADS]], axis=1)
        for k in range(N_CHIPS):
            o_ref[0, k] = full[:, k * EVEN_SHARD:(k + 1) * EVEN_SHARD]

    return pl.pallas_call(
        body, name="wie_grad_to_pieces", grid=(2, per_half),
        in_specs=[pl.BlockSpec((tr, EVEN_MAIN), lambda c, i: (c * per_half + i, 0)), pl.BlockSpec((tr, LANES), lambda c, i: (c * per_half + i, 0))],
        out_specs=pl.BlockSpec((1, N_CHIPS, tr, EVEN_SHARD), lambda c, i: (c, 0, i, 0)),
        out_shape=SDS((2, N_CHIPS, D_MODEL // 2, EVEN_SHARD), F32), compiler_params=_cp("parallel", "parallel"))(main, dt)


def _unpack(buf, parts):
    offs, _ = _pack_offsets(parts)
    return [buf[off:off + p.shape[0]] for p, off in zip(parts, offs)]


def _pad_lanes(v):
    v = v.reshape(1, -1)
    return jnp.pad(v, ((0, 0), (0, LANES - v.shape[1])))


def _as2d(a):
    return a.reshape(1, -1) if a.ndim == 1 else a.reshape(-1, a.shape[-1])


def kernel(x, even_norm_g, even_w_in, gmlp_ln_g, gmlp_ln_b, gmlp_ws, gmlp_bs, ssd_conv_w, ssd_conv_b, ssd_dt_bias, ssd_a_log, ssd_d, ssd_norm_g, even_w_out, odd_norm_g, odd_w_in, sconv_w, odd_w_out, final_norm_g, loss_target, m_even_norm_g, m_even_w_in, m_gmlp_ln_g, m_gmlp_ln_b, m_gmlp_ws, m_gmlp_bs, m_ssd_conv_w, m_ssd_conv_b, m_ssd_dt_bias, m_ssd_a_log, m_ssd_d, m_ssd_norm_g, m_even_w_out, m_odd_norm_g, m_odd_w_in, m_sconv_w, m_odd_w_out, m_final_norm_g, v_even_norm_g, v_even_w_in, v_gmlp_ln_g, v_gmlp_ln_b, v_gmlp_ws, v_gmlp_bs, v_ssd_conv_w, v_ssd_conv_b, v_ssd_dt_bias, v_ssd_a_log, v_ssd_d, v_ssd_norm_g, v_even_w_out, v_odd_norm_g, v_odd_w_in, v_sconv_w, v_odd_w_out, v_final_norm_g):
    weights = dict(even_norm_g=even_norm_g, even_w_in=even_w_in, gmlp_ln_g=gmlp_ln_g, gmlp_ln_b=gmlp_ln_b, gmlp_ws=gmlp_ws, gmlp_bs=gmlp_bs, ssd_conv_w=ssd_conv_w, ssd_conv_b=ssd_conv_b, ssd_dt_bias=ssd_dt_bias, ssd_a_log=ssd_a_log, ssd_d=ssd_d, ssd_norm_g=ssd_norm_g, even_w_out=even_w_out, odd_norm_g=odd_norm_g, odd_w_in=odd_w_in, sconv_w=sconv_w, odd_w_out=odd_w_out, final_norm_g=final_norm_g)
    moms_m = dict(even_norm_g=m_even_norm_g, even_w_in=m_even_w_in, gmlp_ln_g=m_gmlp_ln_g, gmlp_ln_b=m_gmlp_ln_b, gmlp_ws=m_gmlp_ws, gmlp_bs=m_gmlp_bs, ssd_conv_w=m_ssd_conv_w, ssd_conv_b=m_ssd_conv_b, ssd_dt_bias=m_ssd_dt_bias, ssd_a_log=m_ssd_a_log, ssd_d=m_ssd_d, ssd_norm_g=m_ssd_norm_g, even_w_out=m_even_w_out, odd_norm_g=m_odd_norm_g, odd_w_in=m_odd_w_in, sconv_w=m_sconv_w, odd_w_out=m_odd_w_out, final_norm_g=m_final_norm_g)
    moms_v = dict(even_norm_g=v_even_norm_g, even_w_in=v_even_w_in, gmlp_ln_g=v_gmlp_ln_g, gmlp_ln_b=v_gmlp_ln_b, gmlp_ws=v_gmlp_ws, gmlp_bs=v_gmlp_bs, ssd_conv_w=v_ssd_conv_w, ssd_conv_b=v_ssd_conv_b, ssd_dt_bias=v_ssd_dt_bias, ssd_a_log=v_ssd_a_log, ssd_d=v_ssd_d, ssd_norm_g=v_ssd_norm_g, even_w_out=v_even_w_out, odd_norm_g=v_odd_norm_g, odd_w_in=v_odd_w_in, sconv_w=v_sconv_w, odd_w_out=v_odd_w_out, final_norm_g=v_final_norm_g)
    names = list(weights)

    xs = x[0]
    tgt = loss_target[0]
    T = xs.shape[0]
    chip = 2 * lax.axis_index("x") + lax.axis_index("y")
    core = lax.axis_index("c")
    cshard = B_XBC // N_CHIPS
    dshard = D_MODEL // N_CHIPS

    def halves(w):
        return w.astype(BF16).reshape(2, w.shape[0] // 2, w.shape[1])

    small_shard = jnp.concatenate([ssd_conv_w[0].reshape(-1), odd_norm_g[0], sconv_w[0].reshape(-1)])
    g_wie, g_small = run_rider(gather_rider([halves(even_w_in[0]), small_shard.reshape(2, -1, LANES)]), name="comm_gather_first")
    wie_main, wie_dt = wie_from_shards(g_wie.reshape(N_CHIPS, D_MODEL, EVEN_SHARD))
    g_small = g_small.reshape(N_CHIPS, -1)
    n_cw = B_CONV * cshard
    conv_w = g_small[:, :n_cw].reshape(N_CHIPS, B_CONV, cshard).transpose(1, 0, 2).reshape(B_CONV, B_XBC)
    odd_g = g_small[:, n_cw:n_cw + dshard].reshape(1, D_MODEL)
    sconv = g_small[:, n_cw + dshard:].reshape(N_CHIPS, C_CONV, dshard).transpose(1, 0, 2).reshape(C_CONV, C_WIDTH)

    even_g = even_norm_g
    ln_g, ln_b = gmlp_ln_g, gmlp_ln_b
    ws, bs_t = gmlp_ws[0], gmlp_bs[0].T
    conv_b = ssd_conv_b
    dt_bias, a_log, d_skip = _pad_lanes(ssd_dt_bias), _pad_lanes(ssd_a_log), _pad_lanes(ssd_d)
    norm_g = ssd_norm_g
    fin_g = final_norm_g.reshape(1, D_MODEL)

    xn0 = rmsnorm_fwd(xs, even_g, name="even_norm")
    h0, (g_wio,) = matmul(xn0, wie_main, "nn", name="even_in", rider=gather_rider([halves(odd_w_in[0])], (0.0, 0.88, 1.0)))
    wio = g_wio.reshape(N_CHIPS, D_MODEL, IN_ODD // N_CHIPS)
    dtr = matmul(xn0, wie_dt, "nn", name="even_in_dt", tk=D_MODEL)
    ya = gmlp_fwd(h0, ln_g, ln_b, ws, bs_t)
    xbc = ssd_conv_fwd(h0, conv_w, conv_b)
    (yb, states), (g_woe, g_woo) = ssd_fwd(xbc, dtr, h0, dt_bias, a_log, d_skip, norm_g,
                                           rider=gather_rider([halves(even_w_out[0]), halves(odd_w_out[0])]))
    woe = g_woe.reshape(2 * A_WIDTH, D_MODEL)
    woo = g_woo.reshape(2 * C_WIDTH, D_MODEL)
    y0 = [ya, yb]
    x1 = matmul(y0, woe, "nn", name="even_out", res=xs)

    xn1 = rmsnorm_fwd(x1, odd_g, name="odd_norm")
    h1 = matmul(xn1, wio, "nn", name="odd_in")
    yc = sconv_fwd(h1, sconv)
    yd, att_o, att_lse = attn_fwd(h1)
    y1 = [yc, yd]
    x2 = matmul(y1, woo, "nn", name="odd_out", res=x1)

    loss_part, dx2, dx2b, d_fin_g = loss_head(x2, fin_g, tgt)

    tile = 1024
    rows_layout = ((2, N_CHIPS, ROW_PIECE, D_MODEL), (1, 1, ROW_PIECE, tile), lambda i, j, k: (i % 2, i // 2, 0, j))
    per_chip = IN_ODD // N_CHIPS // tile
    cols_layout = ((2, N_CHIPS, D_MODEL // 2, IN_ODD // N_CHIPS), (1, 1, tile, tile), lambda i, j, k: (i, j // per_chip, 0, j % per_chip))
    dy1 = matmul(dx2b, woo, "nt", name="odd_out_dy")
    d_woo = matmul(y1, dx2b, "tn", name="odd_out_dw", tm=ROW_PIECE, out_layout=rows_layout)
    dbg, dcg, dhx, dzc, d_sconv = sconv_bwd(h1, dy1, sconv)
    dq, dk, dv, dzd = attn_bwd(h1, att_o, att_lse, dy1)
    dh1 = jnp.concatenate([dbg, dcg, dhx, dzc, dq, dk, dv, dzd], axis=1)
    dxn1, (d_woo_sib,) = matmul(dh1, wio, "nt", name="odd_in_dx", rider=swap_halves_rider([d_woo]))
    d_wio = matmul(xn1, dh1, "tn", name="odd_in_dw", out_layout=cols_layout)
    dx1, dx1b, d_odd_g = rmsnorm_bwd(x1, odd_g, dxn1, dx2, name="odd_norm_bwd")

    d_woe = matmul(y0, dx1b, "tn", name="even_out_dw", tm=ROW_PIECE, out_layout=rows_layout)
    first = [d_wio, d_woo, d_woe]
    dy0, (d_wio_sib, d_woe_sib) = matmul(dx1b, woe, "nt", name="even_out_dy", rider=swap_halves_rider([d_wio, d_woe]))
    first_sib = [d_wio_sib, d_woo_sib, d_woe_sib]
    first_sums = [chip_sum(p, s, core, name=f"chip_sum_first_{i}") for i, (p, s) in enumerate(zip(first, first_sib))]
    dh0, d_ln_g, d_ln_b, d_ws, d_bs_t = gmlp_bwd(h0, dy0, ln_g, ln_b, ws, bs_t)
    (dxbc_act, ddtr, dh0, d_dt_bias, d_a_log, d_d, d_norm_g), first_landed = ssd_bwd(
        xbc, dtr, h0, states, dy0, dt_bias, a_log, d_skip, norm_g, dh0, rider=scatter_rider(first_sums))
    first_totals = [total_sum(s, l, chip, core, name=f"total_first_{i}") for i, (s, l) in enumerate(zip(first_sums, first_landed))]
    (dh0, d_conv_w, d_conv_b), first_joined = ssd_conv_bwd(h0, dxbc_act, conv_w, conv_b, dh0, rider=join_halves_rider(first_totals))
    ddtr_b = ddtr
    d_wie_main = matmul(xn0, dh0, "tn", name="even_in_dw")
    d_wie_dt = matmul(xn0, ddtr_b, "tn", name="even_in_dw_dt")
    last = [wie_grad_to_pieces(d_wie_main, d_wie_dt)]
    dxn0, last_sib = matmul(ddtr_b, wie_dt, "nt", name="even_in_dx_dt", rider=swap_halves_rider(last))
    last_sums = [chip_sum(last[0], last_sib[0], core, name="chip_sum_last")]
    dxn0, last_landed = matmul(dh0, wie_main, "nt", name="even_in_dx", res=dxn0, rider=scatter_rider(last_sums))
    last_totals = [total_sum(last_sums[0], last_landed[0], chip, core, name="total_last")]
    grad_x, _, d_even_g = rmsnorm_bwd(xs, even_g, dxn0, dx1, name="even_norm_bwd")

    small_names = ["even_norm_g", "gmlp_ln_g", "gmlp_ln_b", "gmlp_ws", "gmlp_bs", "ssd_conv_w", "ssd_conv_b", "ssd_dt_bias",
                   "ssd_a_log", "ssd_d", "ssd_norm_g", "odd_norm_g", "sconv_w", "final_norm_g"]
    small_parts = [d_even_g, d_ln_g, d_ln_b, d_ws, d_bs_t.T, d_conv_w, d_conv_b, d_dt_bias, d_a_log, d_d, d_norm_g, d_odd_g, d_sconv, d_fin_g]
    small_shapes = [p.shape for p in small_parts]
    small_rows = [p.reshape(-1, LANES) for p in small_parts]
    small_sum, last_joined = sum_leading(gather_all(pack_rows(small_rows)), name="small_sum", rider=join_halves_rider(last_totals))
    full = {nm: rows.reshape(shape) for nm, rows, shape in zip(small_names, _unpack(small_sum, small_rows), small_shapes)}
    joined = last_joined + first_joined
    grads = dict(even_w_in=joined[0].reshape(even_w_in.shape), odd_w_in=joined[1].reshape(odd_w_in.shape),
                 odd_w_out=joined[2].reshape(odd_w_out.shape), even_w_out=joined[3].reshape(even_w_out.shape))
    for nm in small_names:
        g = full[nm]
        if nm in ("ssd_dt_bias", "ssd_a_log", "ssd_d"):
            g = g[:, :B_HEADS]
        elif nm == "ssd_conv_w":
            g = lax.dynamic_slice_in_dim(g, chip * cshard, cshard, axis=1)
        elif nm in ("odd_norm_g", "sconv_w"):
            g = lax.dynamic_slice_in_dim(g, chip * dshard, dshard, axis=1)
        grads[nm] = g.reshape(weights[nm].shape)

    deltas, new_m, new_v = {}, {}, {}
    for nm in names:
        w = weights[nm]
        d, nm_, nv_ = adamw(_as2d(w), _as2d(grads[nm]), _as2d(moms_m[nm]), _as2d(moms_v[nm]), name=f"adamw_{nm}")
        deltas[nm], new_m[nm], new_v[nm] = d.reshape(w.shape), nm_.reshape(w.shape), nv_.reshape(w.shape)

    loss = lax.psum(loss_part[0, 0], ("x", "y", "c"))
    return (loss, grad_x[None], *[grads[n] for n in names], *[deltas[n] for n in names],
            *[new_m[n] for n in names], *[new_v[n] for n in names])
```
